```python
import math
import jax
import jax.numpy as jnp
from jax import lax
import numpy as np

D_MODEL = 1024
BATCH = 8
SEQ = 8192
DEPTH = 4

D_MIX = D_MODEL
D_CONV = D_MIX // 2
CONV_GROUPS = 8
D_GLA = D_MIX - D_CONV
GLA_HEADS = 4
HEAD_V = D_GLA // GLA_HEADS
HEAD_K = HEAD_V // 2
D_GLA_K = GLA_HEADS * HEAD_K
GATE_RANK = 16
GATE_NORMALIZER = 16.0
CHUNK = 64
D_FF = 2816
CONV_WIDTH = 3
EPS = 1e-6
SPLIT_SIZES = (D_CONV, D_CONV, D_CONV,
               D_GLA_K, D_GLA_K, D_GLA, D_GLA,
               GATE_RANK, GATE_RANK)
D_IN = sum(SPLIT_SIZES)

kernel_name = "hybrid_shortconv_gla_convffn_encoder"


def rmsnorm(x, g):
    xf = x.astype(jnp.float32)
    y = xf * lax.rsqrt(jnp.mean(xf * xf, axis=-1, keepdims=True) + EPS)
    return (y * g.astype(jnp.float32)).astype(x.dtype)


def dwconv3(x, w):
    xp = jnp.pad(x, ((0, 0), (1, 1), (0, 0)))
    return w[0] * xp[:, :-2] + w[1] * xp[:, 1:-1] + w[2] * xp[:, 2:]


def split_cols(p):
    idx, acc = [], 0
    for s in SPLIT_SIZES[:-1]:
        acc += s
        idx.append(acc)
    return jnp.split(p, idx, axis=-1)


def gla_chunked(q, k, v, log_a):
    b_, h_, L, dk = q.shape
    dv = v.shape[-1]
    n = L // CHUNK
    q = q.reshape(b_, h_, n, CHUNK, dk)
    k = k.reshape(b_, h_, n, CHUNK, dk)
    v = v.reshape(b_, h_, n, CHUNK, dv)
    cum = jnp.cumsum(log_a.reshape(b_, h_, n, CHUNK, dk), axis=3)
    cum_last = cum[:, :, :, -1:, :]
    q_in = q * jnp.exp(cum)
    k_in = k * jnp.exp(-cum)
    k_out = k * jnp.exp(cum_last - cum)
    mask = jnp.tril(jnp.ones((CHUNK, CHUNK), dtype=bool))
    scores = jnp.einsum('bhnid,bhnjd->bhnij', q_in, k_in)
    scores = jnp.where(mask, scores, 0.0)
    o_intra = jnp.einsum('bhnij,bhnje->bhnie', scores, v)
    kv = jnp.einsum('bhnjd,bhnje->bhnde', k_out, v)
    decay = jnp.exp(cum_last[:, :, :, 0, :])

    def step(state, inp):
        d_n, kv_n = inp
        return d_n[..., None] * state + kv_n, state

    _, s_prev = lax.scan(step, jnp.zeros((b_, h_, dk, dv), jnp.float32),
                         (jnp.moveaxis(decay, 2, 0), jnp.moveaxis(kv, 2, 0)))
    s_prev = jnp.moveaxis(s_prev, 0, 2)
    o_inter = jnp.einsum('bhnid,bhnde->bhnie', q_in, s_prev)
    return (o_intra + o_inter).reshape(b_, h_, L, dv)


def to_heads(t, d_head):
    b_, L, _ = t.shape
    return t.reshape(b_, L, -1, d_head).transpose(0, 2, 1, 3)


def gate_log_decay(lr, w_up, bias):
    pre = (lr @ w_up + bias).astype(jnp.float32)
    return jax.nn.log_sigmoid(pre) / GATE_NORMALIZER


def mixer(h, w_in, conv_a, gate_up_fwd, gate_bias_fwd, gate_up_bwd, gate_bias_bwd,
          gla_head_norm, w_out):
    p = h @ w_in
    gb, gc, gv, q, k, v, go, lr_f, lr_b = split_cols(p)
    y_a = gb * dwconv3(gc * gv, conv_a)
    la_f = gate_log_decay(lr_f, gate_up_fwd, gate_bias_fwd)
    la_b = gate_log_decay(lr_b, gate_up_bwd, gate_bias_bwd)
    qh = to_heads(q, HEAD_K).astype(jnp.float32) * (HEAD_K ** -0.5)
    kh = to_heads(k, HEAD_K).astype(jnp.float32)
    vh = to_heads(v, HEAD_V).astype(jnp.float32)
    af = to_heads(la_f, HEAD_K)
    ab = to_heads(la_b, HEAD_K)
    flip = lambda t: jnp.flip(t, axis=2)
    o_f = gla_chunked(qh, kh, vh, af)
    o_b = flip(gla_chunked(flip(qh), flip(kh), flip(vh), flip(ab)))
    o = o_f + o_b
    o = o * lax.rsqrt(jnp.mean(o * o, axis=-1, keepdims=True) + EPS) * gla_head_norm.astype(jnp.float32)
    b_, _, L, _ = o.shape
    o = o.transpose(0, 2, 1, 3).reshape(b_, L, D_GLA).astype(h.dtype)
    y_b = jax.nn.silu(go) * o
    y = jnp.concatenate([y_a, y_b], axis=-1)
    return y @ w_out


def conv_mlp(h, w_up, conv_w, w_down):
    u = dwconv3(h @ w_up, conv_w)
    gate, val = jnp.split(u, 2, axis=-1)
    return (jax.nn.silu(gate) * val) @ w_down


def _fwd_setup_inputs(seed: int = 0) -> dict:
    key = jax.random.key(seed)
    ks = jax.random.split(key, 20)
    f32 = jnp.float32
    nrm = lambda k, shape, scale: jax.random.normal(k, shape, f32) * scale
    gain = lambda k, shape: 1.0 + 0.05 * jax.random.normal(k, shape, f32)
    return {
        "x": jax.random.normal(ks[0], (BATCH, SEQ, D_MODEL), f32),
        "norm_mix_pre": gain(ks[1], (DEPTH, D_MODEL)),
        "norm_mix_post": gain(ks[2], (DEPTH, D_MODEL)),
        "norm_ffn_pre": gain(ks[3], (DEPTH, D_MODEL)),
        "norm_ffn_post": gain(ks[4], (DEPTH, D_MODEL)),
        "w_in": nrm(ks[5], (DEPTH, D_MODEL, D_IN), D_MODEL ** -0.5),
        "conv_a": nrm(ks[6], (DEPTH, CONV_WIDTH, D_CONV), CONV_WIDTH ** -0.5),
        "gate_up_fwd": nrm(ks[7], (DEPTH, GATE_RANK, D_GLA_K), GATE_RANK ** -0.5),
        "gate_bias_fwd": nrm(ks[8], (DEPTH, D_GLA_K), 0.1),
        "gate_up_bwd": nrm(ks[9], (DEPTH, GATE_RANK, D_GLA_K), GATE_RANK ** -0.5),
        "gate_bias_bwd": nrm(ks[10], (DEPTH, D_GLA_K), 0.1),
        "gla_head_norm": gain(ks[11], (DEPTH, HEAD_V)),
        "w_out": nrm(ks[12], (DEPTH, D_MIX, D_MODEL), D_MIX ** -0.5),
        "w_up": nrm(ks[13], (DEPTH, D_MODEL, 2 * D_FF), D_MODEL ** -0.5),
        "conv_ffn": nrm(ks[14], (DEPTH, CONV_WIDTH, 2 * D_FF), CONV_WIDTH ** -0.5),
        "w_down": nrm(ks[15], (DEPTH, D_FF, D_MODEL), D_FF ** -0.5),
    }


def _fwd_reference(x, norm_mix_pre, norm_mix_post, norm_ffn_pre, norm_ffn_post, w_in, conv_a,
              gate_up_fwd, gate_bias_fwd, gate_up_bwd, gate_bias_bwd, gla_head_norm,
              w_out, w_up, conv_ffn, w_down):
    for l in range(DEPTH):
        h = rmsnorm(x, norm_mix_pre[l])
        y = mixer(h, w_in[l], conv_a[l], gate_up_fwd[l], gate_bias_fwd[l], gate_up_bwd[l],
                  gate_bias_bwd[l], gla_head_norm[l], w_out[l])
        x = x + rmsnorm(y, norm_mix_post[l])
        h = rmsnorm(x, norm_ffn_pre[l])
        y = conv_mlp(h, w_up[l], conv_ffn[l], w_down[l])
        x = x + rmsnorm(y, norm_ffn_post[l])
    return x


import jax as _jax
import jax.numpy as _jnp

TWIN_FORMAT = 'train_step'
FWD_PARAMS = ['x', 'norm_mix_pre', 'norm_mix_post', 'norm_ffn_pre', 'norm_ffn_post', 'w_in', 'conv_a', 'gate_up_fwd', 'gate_bias_fwd', 'gate_up_bwd', 'gate_bias_bwd', 'gla_head_norm', 'w_out', 'w_up', 'conv_ffn', 'w_down']
TWIN_WEIGHTS = ['norm_mix_pre', 'norm_mix_post', 'norm_ffn_pre', 'norm_ffn_post', 'w_in', 'conv_a', 'gate_up_fwd', 'gate_bias_fwd', 'gate_up_bwd', 'gate_bias_bwd', 'gla_head_norm', 'w_out', 'w_up', 'conv_ffn', 'w_down']
TWIN_DIFF_INPUT = 'x'
TWIN_INPUTS = ['x', 'norm_mix_pre', 'norm_mix_post', 'norm_ffn_pre', 'norm_ffn_post', 'w_in', 'conv_a', 'gate_up_fwd', 'gate_bias_fwd', 'gate_up_bwd', 'gate_bias_bwd', 'gla_head_norm', 'w_out', 'w_up', 'conv_ffn', 'w_down', 'loss_target', 'm_norm_mix_pre', 'm_norm_mix_post', 'm_norm_ffn_pre', 'm_norm_ffn_post', 'm_w_in', 'm_conv_a', 'm_gate_up_fwd', 'm_gate_bias_fwd', 'm_gate_up_bwd', 'm_gate_bias_bwd', 'm_gla_head_norm', 'm_w_out', 'm_w_up', 'm_conv_ffn', 'm_w_down', 'v_norm_mix_pre', 'v_norm_mix_post', 'v_norm_ffn_pre', 'v_norm_ffn_post', 'v_w_in', 'v_conv_a', 'v_gate_up_fwd', 'v_gate_bias_fwd', 'v_gate_up_bwd', 'v_gate_bias_bwd', 'v_gla_head_norm', 'v_w_out', 'v_w_up', 'v_conv_ffn', 'v_w_down']
TWIN_OUTPUTS = ['loss', 'grad_x', 'grad_norm_mix_pre', 'grad_norm_mix_post', 'grad_norm_ffn_pre', 'grad_norm_ffn_post', 'grad_w_in', 'grad_conv_a', 'grad_gate_up_fwd', 'grad_gate_bias_fwd', 'grad_gate_up_bwd', 'grad_gate_bias_bwd', 'grad_gla_head_norm', 'grad_w_out', 'grad_w_up', 'grad_conv_ffn', 'grad_w_down', 'delta_norm_mix_pre', 'delta_norm_mix_post', 'delta_norm_ffn_pre', 'delta_norm_ffn_post', 'delta_w_in', 'delta_conv_a', 'delta_gate_up_fwd', 'delta_gate_bias_fwd', 'delta_gate_up_bwd', 'delta_gate_bias_bwd', 'delta_gla_head_norm', 'delta_w_out', 'delta_w_up', 'delta_conv_ffn', 'delta_w_down', 'new_m_norm_mix_pre', 'new_m_norm_mix_post', 'new_m_norm_ffn_pre', 'new_m_norm_ffn_post', 'new_m_w_in', 'new_m_conv_a', 'new_m_gate_up_fwd', 'new_m_gate_bias_fwd', 'new_m_gate_up_bwd', 'new_m_gate_bias_bwd', 'new_m_gla_head_norm', 'new_m_w_out', 'new_m_w_up', 'new_m_conv_ffn', 'new_m_w_down', 'new_v_norm_mix_pre', 'new_v_norm_mix_post', 'new_v_norm_ffn_pre', 'new_v_norm_ffn_post', 'new_v_w_in', 'new_v_conv_a', 'new_v_gate_up_fwd', 'new_v_gate_bias_fwd', 'new_v_gate_up_bwd', 'new_v_gate_bias_bwd', 'new_v_gla_head_norm', 'new_v_w_out', 'new_v_w_up', 'new_v_conv_ffn', 'new_v_w_down']
TWIN_LEAF_KINDS = {'loss': 'loss', 'grad_x': 'grad_x', 'grad_norm_mix_pre': 'grad_w', 'grad_norm_mix_post': 'grad_w', 'grad_norm_ffn_pre': 'grad_w', 'grad_norm_ffn_post': 'grad_w', 'grad_w_in': 'grad_w', 'grad_conv_a': 'grad_w', 'grad_gate_up_fwd': 'grad_w', 'grad_gate_bias_fwd': 'grad_w', 'grad_gate_up_bwd': 'grad_w', 'grad_gate_bias_bwd': 'grad_w', 'grad_gla_head_norm': 'grad_w', 'grad_w_out': 'grad_w', 'grad_w_up': 'grad_w', 'grad_conv_ffn': 'grad_w', 'grad_w_down': 'grad_w', 'delta_norm_mix_pre': 'delta_w', 'delta_norm_mix_post': 'delta_w', 'delta_norm_ffn_pre': 'delta_w', 'delta_norm_ffn_post': 'delta_w', 'delta_w_in': 'delta_w', 'delta_conv_a': 'delta_w', 'delta_gate_up_fwd': 'delta_w', 'delta_gate_bias_fwd': 'delta_w', 'delta_gate_up_bwd': 'delta_w', 'delta_gate_bias_bwd': 'delta_w', 'delta_gla_head_norm': 'delta_w', 'delta_w_out': 'delta_w', 'delta_w_up': 'delta_w', 'delta_conv_ffn': 'delta_w', 'delta_w_down': 'delta_w', 'new_m_norm_mix_pre': 'new_m', 'new_m_norm_mix_post': 'new_m', 'new_m_norm_ffn_pre': 'new_m', 'new_m_norm_ffn_post': 'new_m', 'new_m_w_in': 'new_m', 'new_m_conv_a': 'new_m', 'new_m_gate_up_fwd': 'new_m', 'new_m_gate_bias_fwd': 'new_m', 'new_m_gate_up_bwd': 'new_m', 'new_m_gate_bias_bwd': 'new_m', 'new_m_gla_head_norm': 'new_m', 'new_m_w_out': 'new_m', 'new_m_w_up': 'new_m', 'new_m_conv_ffn': 'new_m', 'new_m_w_down': 'new_m', 'new_v_norm_mix_pre': 'new_v', 'new_v_norm_mix_post': 'new_v', 'new_v_norm_ffn_pre': 'new_v', 'new_v_norm_ffn_post': 'new_v', 'new_v_w_in': 'new_v', 'new_v_conv_a': 'new_v', 'new_v_gate_up_fwd': 'new_v', 'new_v_gate_bias_fwd': 'new_v', 'new_v_gate_up_bwd': 'new_v', 'new_v_gate_bias_bwd': 'new_v', 'new_v_gla_head_norm': 'new_v', 'new_v_w_out': 'new_v', 'new_v_w_up': 'new_v', 'new_v_conv_ffn': 'new_v', 'new_v_w_down': 'new_v'}


def _forward(args):
    return _fwd_reference(*[args[k] for k in FWD_PARAMS])


def _output_shape():
    def fwd():
        inp = _fwd_setup_inputs(0)
        return _fwd_reference(*[inp[k] for k in FWD_PARAMS])
    out = _jax.eval_shape(fwd)
    return out.shape, out.dtype

N_MICROBATCH = 1
ADAM_LR = 0.001
ADAM_B1 = 0.9
ADAM_B2 = 0.999
ADAM_EPS = 1e-08
ADAM_WD = 0.01
ADAM_STEP = 10
PER_EXAMPLE_BATCH_AXIS = {'x': 0, 'loss_target': 0}
SHARED_INPUTS = []
_WEIGHT_DTYPES = {'norm_mix_pre': _jnp.float32, 'norm_mix_post': _jnp.float32, 'norm_ffn_pre': _jnp.float32, 'norm_ffn_post': _jnp.float32, 'w_in': _jnp.float32, 'conv_a': _jnp.float32, 'gate_up_fwd': _jnp.float32, 'gate_bias_fwd': _jnp.float32, 'gate_up_bwd': _jnp.float32, 'gate_bias_bwd': _jnp.float32, 'gla_head_norm': _jnp.float32, 'w_out': _jnp.float32, 'w_up': _jnp.float32, 'conv_ffn': _jnp.float32, 'w_down': _jnp.float32}
MOMENT_SCALE = {'norm_mix_pre': 4.416534e+00, 'norm_mix_post': 6.277920e+01, 'norm_ffn_pre': 2.422722e+00, 'norm_ffn_post': 6.356530e+01, 'w_in': 2.447318e+00, 'conv_a': 2.834778e+00, 'gate_up_fwd': 1.818855e-01, 'gate_bias_fwd': 8.044888e-01, 'gate_up_bwd': 1.840018e-01, 'gate_bias_bwd': 7.535651e-01, 'gla_head_norm': 3.401280e+00, 'w_out': 2.325699e+00, 'w_up': 1.067082e+00, 'conv_ffn': 1.121397e+00, 'w_down': 1.806445e+00}


def _to_microbatches(a, axis):
    t = _jnp.moveaxis(a, axis, 0)
    t = t.reshape((N_MICROBATCH, t.shape[0] // N_MICROBATCH) + t.shape[1:])
    return _jnp.moveaxis(t, 1, axis + 1)


def setup_inputs(seed: int = 0) -> dict:
    inp = _fwd_setup_inputs(seed)
    key = _jax.random.fold_in(_jax.random.key(seed), 7919)
    shape, _ = _output_shape()
    out = dict(inp)
    out["loss_target"] = _jax.random.normal(_jax.random.fold_in(key, 0), shape, _jnp.float32)
    for i, name in enumerate(TWIN_WEIGHTS):
        w = inp[name].astype(_jnp.float32)
        if MOMENT_SCALE is None:
            s = _jnp.sqrt(_jnp.mean(_jnp.square(w)) + 1e-30)
        else:
            s = MOMENT_SCALE[name]
        km, kv = _jax.random.split(_jax.random.fold_in(key, i + 1))
        out[name] = w
        out["m_" + name] = s * _jax.random.normal(km, w.shape, _jnp.float32)
        out["v_" + name] = (s * s) * _jax.random.uniform(kv, w.shape, _jnp.float32, 0.5, 1.5)
    if N_MICROBATCH > 1:
        for name, axis in PER_EXAMPLE_BATCH_AXIS.items():
            out[name] = _to_microbatches(out[name], axis)
    return {'x': out['x'], 'norm_mix_pre': out['norm_mix_pre'], 'norm_mix_post': out['norm_mix_post'], 'norm_ffn_pre': out['norm_ffn_pre'], 'norm_ffn_post': out['norm_ffn_post'], 'w_in': out['w_in'], 'conv_a': out['conv_a'], 'gate_up_fwd': out['gate_up_fwd'], 'gate_bias_fwd': out['gate_bias_fwd'], 'gate_up_bwd': out['gate_up_bwd'], 'gate_bias_bwd': out['gate_bias_bwd'], 'gla_head_norm': out['gla_head_norm'], 'w_out': out['w_out'], 'w_up': out['w_up'], 'conv_ffn': out['conv_ffn'], 'w_down': out['w_down'], 'loss_target': out['loss_target'], 'm_norm_mix_pre': out['m_norm_mix_pre'], 'm_norm_mix_post': out['m_norm_mix_post'], 'm_norm_ffn_pre': out['m_norm_ffn_pre'], 'm_norm_ffn_post': out['m_norm_ffn_post'], 'm_w_in': out['m_w_in'], 'm_conv_a': out['m_conv_a'], 'm_gate_up_fwd': out['m_gate_up_fwd'], 'm_gate_bias_fwd': out['m_gate_bias_fwd'], 'm_gate_up_bwd': out['m_gate_up_bwd'], 'm_gate_bias_bwd': out['m_gate_bias_bwd'], 'm_gla_head_norm': out['m_gla_head_norm'], 'm_w_out': out['m_w_out'], 'm_w_up': out['m_w_up'], 'm_conv_ffn': out['m_conv_ffn'], 'm_w_down': out['m_w_down'], 'v_norm_mix_pre': out['v_norm_mix_pre'], 'v_norm_mix_post': out['v_norm_mix_post'], 'v_norm_ffn_pre': out['v_norm_ffn_pre'], 'v_norm_ffn_post': out['v_norm_ffn_post'], 'v_w_in': out['v_w_in'], 'v_conv_a': out['v_conv_a'], 'v_gate_up_fwd': out['v_gate_up_fwd'], 'v_gate_bias_fwd': out['v_gate_bias_fwd'], 'v_gate_up_bwd': out['v_gate_up_bwd'], 'v_gate_bias_bwd': out['v_gate_bias_bwd'], 'v_gla_head_norm': out['v_gla_head_norm'], 'v_w_out': out['v_w_out'], 'v_w_up': out['v_w_up'], 'v_conv_ffn': out['v_conv_ffn'], 'v_w_down': out['v_w_down']}


def _loss(weights, diff, rest, loss_target):
    with _jax.named_scope("forward"):
        args = {**rest, TWIN_DIFF_INPUT: diff, **{k: w.astype(_WEIGHT_DTYPES[k]) for k, w in weights.items()}}
        y = _forward(args)
    with _jax.named_scope("loss_head"):
        err = _jnp.square(y.astype(_jnp.float32) - loss_target)
        return 0.5 * _jnp.sum(_jnp.mean(err, axis=-1)) if err.ndim else 0.5 * err


def _adamw(w, g, m, v):
    m = ADAM_B1 * m + (1.0 - ADAM_B1) * g
    v = ADAM_B2 * v + (1.0 - ADAM_B2) * _jnp.square(g)
    m_hat = m / (1.0 - ADAM_B1 ** ADAM_STEP)
    v_hat = v / (1.0 - ADAM_B2 ** ADAM_STEP)
    delta = -ADAM_LR * (m_hat / (_jnp.sqrt(v_hat) + ADAM_EPS) + ADAM_WD * w)
    return delta, m, v


def reference(x, norm_mix_pre, norm_mix_post, norm_ffn_pre, norm_ffn_post, w_in, conv_a, gate_up_fwd, gate_bias_fwd, gate_up_bwd, gate_bias_bwd, gla_head_norm, w_out, w_up, conv_ffn, w_down, loss_target, m_norm_mix_pre, m_norm_mix_post, m_norm_ffn_pre, m_norm_ffn_post, m_w_in, m_conv_a, m_gate_up_fwd, m_gate_bias_fwd, m_gate_up_bwd, m_gate_bias_bwd, m_gla_head_norm, m_w_out, m_w_up, m_conv_ffn, m_w_down, v_norm_mix_pre, v_norm_mix_post, v_norm_ffn_pre, v_norm_ffn_post, v_w_in, v_conv_a, v_gate_up_fwd, v_gate_bias_fwd, v_gate_up_bwd, v_gate_bias_bwd, v_gla_head_norm, v_w_out, v_w_up, v_conv_ffn, v_w_down):
    given = dict(x=x, norm_mix_pre=norm_mix_pre, norm_mix_post=norm_mix_post, norm_ffn_pre=norm_ffn_pre, norm_ffn_post=norm_ffn_post, w_in=w_in, conv_a=conv_a, gate_up_fwd=gate_up_fwd, gate_bias_fwd=gate_bias_fwd, gate_up_bwd=gate_up_bwd, gate_bias_bwd=gate_bias_bwd, gla_head_norm=gla_head_norm, w_out=w_out, w_up=w_up, conv_ffn=conv_ffn, w_down=w_down, loss_target=loss_target, m_norm_mix_pre=m_norm_mix_pre, m_norm_mix_post=m_norm_mix_post, m_norm_ffn_pre=m_norm_ffn_pre, m_norm_ffn_post=m_norm_ffn_post, m_w_in=m_w_in, m_conv_a=m_conv_a, m_gate_up_fwd=m_gate_up_fwd, m_gate_bias_fwd=m_gate_bias_fwd, m_gate_up_bwd=m_gate_up_bwd, m_gate_bias_bwd=m_gate_bias_bwd, m_gla_head_norm=m_gla_head_norm, m_w_out=m_w_out, m_w_up=m_w_up, m_conv_ffn=m_conv_ffn, m_w_down=m_w_down, v_norm_mix_pre=v_norm_mix_pre, v_norm_mix_post=v_norm_mix_post, v_norm_ffn_pre=v_norm_ffn_pre, v_norm_ffn_post=v_norm_ffn_post, v_w_in=v_w_in, v_conv_a=v_conv_a, v_gate_up_fwd=v_gate_up_fwd, v_gate_bias_fwd=v_gate_bias_fwd, v_gate_up_bwd=v_gate_up_bwd, v_gate_bias_bwd=v_gate_bias_bwd, v_gla_head_norm=v_gla_head_norm, v_w_out=v_w_out, v_w_up=v_w_up, v_conv_ffn=v_conv_ffn, v_w_down=v_w_down)
    weights = {n: given[n] for n in TWIN_WEIGHTS}
    shared = {n: given[n] for n in SHARED_INPUTS}
    per_example = {n: given[n] for n in ['x']}
    grad_fn = _jax.value_and_grad(_loss, argnums=(0, 1))

    def one_microbatch(ex, loss_target):
        ex = dict(ex)
        diff = ex.pop(TWIN_DIFF_INPUT)
        return grad_fn(weights, diff, {**shared, **ex}, loss_target)

    if N_MICROBATCH == 1:
        loss, (grad_w, grad_x) = one_microbatch(per_example, given["loss_target"])
    else:
        def body(carry, xs):
            loss_sum, grad_sum = carry
            l_k, (gw_k, gx_k) = one_microbatch(xs[0], xs[1])
            with _jax.named_scope("update"):
                return (loss_sum + l_k, _jax.tree.map(_jnp.add, grad_sum, gw_k)), gx_k

        init = (_jnp.zeros((), _jnp.float32), _jax.tree.map(_jnp.zeros_like, weights))
        (loss, grad_w), grad_x = _jax.lax.scan(body, init, (per_example, given["loss_target"]))
    with _jax.named_scope("update"):
        delta_w, new_m, new_v = {}, {}, {}
        for n in TWIN_WEIGHTS:
            delta_w[n], new_m[n], new_v[n] = _adamw(weights[n], grad_w[n], given["m_" + n], given["v_" + n])
    return (loss, grad_x, *[grad_w[n] for n in TWIN_WEIGHTS], *[delta_w[n] for n in TWIN_WEIGHTS],
            *[new_m[n] for n in TWIN_WEIGHTS], *[new_v[n] for n in TWIN_WEIGHTS])
```

```python
import functools

import jax
import jax.numpy as jnp
from jax import lax
from jax.experimental import pallas as pl
from jax.experimental.pallas import tpu as pltpu

F32 = jnp.float32
BF16 = jnp.bfloat16
MXU_DTYPE = jnp.bfloat16
MESH = pl.DeviceIdType.MESH

D = 1024
DC = 512
DG = 512
NH = 4
HV = 128
HK = 64
DK = 256
RK = 16
CH = 64
DFF = 2816
DIN = 3104
DINP = 3200
LRW = 128
DEPTH = 4
EPS = 1e-6
QSCALE = HK ** -0.5
GATE_NORM = 1.0 / 16.0
CB_GB, CB_GC, CB_GV, CB_GO = 0, 1, 2, 5
CB_Q, CB_K = 6, 7
CB_V = 4
CB_LR = 24
LR = 0.001
B1 = 0.9
B2 = 0.999
AEPS = 1e-08
WD = 0.01
STEP = 10
R_IN, R_OUT, R_UP, R_DOWN = 3104, 1024, 5632, 2816
R_USED = R_IN + R_OUT + R_UP + R_DOWN
R_PACK = 12800
HALF = R_PACK // 2
TM_PROJ = 512
TL_GLA = 512
TL_MIX = 256
TL_FFN = 256
VMEM_LIMIT = 56 * 1024 * 1024


def _cp(*sem):
    return pltpu.CompilerParams(dimension_semantics=sem if sem else None, vmem_limit_bytes=VMEM_LIMIT)


def _mm(a, b):
    return jnp.dot(a.astype(MXU_DTYPE), b.astype(MXU_DTYPE), preferred_element_type=F32)


def _mm_nt(a, b):
    return lax.dot_general(a.astype(MXU_DTYPE), b.astype(MXU_DTYPE), (((1,), (1,)), ((), ())),
                           preferred_element_type=F32)


def _mm_tn(a, b):
    return lax.dot_general(a.astype(MXU_DTYPE), b.astype(MXU_DTYPE), (((0,), (0,)), ((), ())),
                           preferred_element_type=F32)


def _mm_hi(a, b):
    return jnp.dot(a, b, precision=lax.Precision.HIGHEST, preferred_element_type=F32)


def _rms(x, g):
    r = lax.rsqrt(jnp.mean(x * x, axis=-1, keepdims=True) + EPS)
    return x * r * g


def _rms_bwd(dout, y, g):
    r = lax.rsqrt(jnp.mean(y * y, axis=-1, keepdims=True) + EPS)
    yh = y * r
    dyh = dout * g
    dy = r * (dyh - yh * jnp.mean(dyh * yh, axis=-1, keepdims=True))
    dg = jnp.sum(dout * yh, axis=0, keepdims=True)
    return dy, dg


def _sigmoid(x):
    return 1.0 / (1.0 + jnp.exp(-x))


def _logsig(x):
    return jnp.minimum(x, 0.0) - jnp.log1p(jnp.exp(-jnp.abs(x)))


def _shift_down(x, first_row):
    row = lax.broadcasted_iota(jnp.int32, x.shape, 0)
    return jnp.where(row == 0, first_row, pltpu.roll(x, 1, 0))


def _shift_up(x, last_row):
    n = x.shape[0]
    row = lax.broadcasted_iota(jnp.int32, x.shape, 0)
    return jnp.where(row == n - 1, last_row, pltpu.roll(x, n - 1, 0))


def _halo_rows(prev_ref, next_ref, i, last):
    p = jnp.where(i == 0, 0.0, prev_ref[7:8, :])
    n = jnp.where(i == last, 0.0, next_ref[0:1, :])
    return p, n


def _conv3(x, xp, xn, w_ref):
    xm1 = _shift_down(x, xp)
    xp1 = _shift_up(x, xn)
    return w_ref[0:1, :] * xm1 + w_ref[1:2, :] * x + w_ref[2:3, :] * xp1, xm1, xp1


def _conv3_t(d, dp, dn, w_ref):
    return w_ref[0:1, :] * _shift_up(d, dn) + w_ref[1:2, :] * d + w_ref[2:3, :] * _shift_down(d, dp)


def _prev_blk(tl, cb):
    return lambda i: (jnp.maximum(i * (tl // 8) - 1, 0), cb)


def _next_blk(tl, nrows, cb):
    return lambda i: (jnp.minimum((i + 1) * (tl // 8), nrows // 8 - 1), cb)


def rms_matmul(x, g, w, tn, name):
    L = x.shape[0]
    N = w.shape[1]
    tm = min(L, TM_PROJ)

    def body(x_ref, g_ref, w_ref, o_ref, h_ref):
        @pl.when(pl.program_id(1) == 0)
        def _():
            h_ref[...] = _rms(x_ref[...], g_ref[...]).astype(BF16)

        o_ref[...] = _mm(h_ref[...], w_ref[...])

    return pl.pallas_call(
        body, name=name, grid=(L // tm, N // tn),
        in_specs=[pl.BlockSpec((tm, D), lambda i, j: (i, 0)), pl.BlockSpec((1, D), lambda i, j: (0, 0)),
                  pl.BlockSpec((D, tn), lambda i, j: (0, j))],
        out_specs=[pl.BlockSpec((tm, tn), lambda i, j: (i, j)), pl.BlockSpec((tm, D), lambda i, j: (i, 0))],
        out_shape=[jax.ShapeDtypeStruct((L, N), F32), jax.ShapeDtypeStruct((L, D), BF16)],
        compiler_params=_cp("parallel", "arbitrary"),
    )(x, g, w)


def _gla_masks():
    def blk(shape, rdiv, cdiv):
        r = lax.broadcasted_iota(jnp.int32, shape, 0) // rdiv
        c = lax.broadcasted_iota(jnp.int32, shape, 1) // cdiv
        return (r == c).astype(F32)

    r = lax.broadcasted_iota(jnp.int32, (CH, CH), 0)
    c = lax.broadcasted_iota(jnp.int32, (CH, CH), 1)
    r4 = lax.broadcasted_iota(jnp.int32, (NH * CH, CH), 0) % CH
    c4 = lax.broadcasted_iota(jnp.int32, (NH * CH, CH), 1)
    return dict(
        bdq=blk((NH * CH, DK), CH, HK),
        bdo=blk((NH * CH, DG), CH, HV),
        bds=blk((DG, DK), HV, HK),
        tril=(r >= c).astype(F32), triu=(r <= c).astype(F32),
        tril4=r4 >= c4, triu4=r4 <= c4,
    )


def _tile4(x):
    return jnp.concatenate([x, x, x, x], axis=0)


def _gla_prep(q, k, a, m, rev):
    cum = _mm_hi(m["triu"] if rev else m["tril"], a)
    tot = jnp.sum(a, axis=0, keepdims=True)
    e = jnp.exp(cum)
    einv = jnp.exp(-cum)
    eout = jnp.exp(tot - cum)
    qt = q * QSCALE * e
    kt = k * einv
    kh = k * eout
    qs = _tile4(qt) * m["bdq"]
    sc = _mm_nt(qs, kt)
    sc = jnp.where(m["triu4"] if rev else m["tril4"], sc, 0.0)
    return dict(e=e, einv=einv, eout=eout, dec=jnp.exp(tot), qt=qt, kt=kt, kh=kh, qs=qs, sc=sc)


def _gla_chunk_fwd(q, k, v, a, st_ref, m, rev):
    p = _gla_prep(q, k, a, m, rev)
    r = _mm(p["sc"], v)
    o_intra = jnp.concatenate([r[h * CH:(h + 1) * CH, h * HV:(h + 1) * HV] for h in range(NH)], axis=1)
    st = st_ref[...]
    st16 = st.astype(BF16)
    o = o_intra + _mm_nt(p["qt"], st16)
    st_ref[...] = st * p["dec"] + _mm_tn(v, p["kh"]) * m["bds"]
    return o, st16


def gla_fwd(P, gcat, gbias, tl):
    L = P.shape[0]
    nb = L // tl
    nc = tl // CH

    def body(qf, kf, vf, lf, qb, kb, vb, lb, gc_ref, bs_ref, of, ob, sf, sb, stf, stb, af, ab):
        @pl.when(pl.program_id(0) == 0)
        def _():
            stf[...] = jnp.zeros_like(stf)
            stb[...] = jnp.zeros_like(stb)

        af[...] = _logsig(_mm(lf[...], gc_ref[:, 0:DK]) + bs_ref[:, 0:DK]) * GATE_NORM
        ab[...] = _logsig(_mm(lb[...], gc_ref[:, DK:2 * DK]) + bs_ref[:, DK:2 * DK]) * GATE_NORM
        m = _gla_masks()

        def chunk(c, carry):
            rows = pl.ds(pl.multiple_of(c * CH, CH), CH)
            o, st = _gla_chunk_fwd(qf[rows, :], kf[rows, :], vf[rows, :], af[rows, :], stf, m, False)
            of[rows, :] = o
            sf[c] = st
            cb = nc - 1 - c
            rows = pl.ds(pl.multiple_of(cb * CH, CH), CH)
            o, st = _gla_chunk_fwd(qb[rows, :], kb[rows, :], vb[rows, :], ab[rows, :], stb, m, True)
            ob[rows, :] = o
            sb[cb] = st
            return carry

        lax.fori_loop(0, nc, chunk, 0)

    fw = lambda cb: (lambda i: (i, cb))
    bw = lambda cb: (lambda i: (nb - 1 - i, cb))
    return pl.pallas_call(
        body, name="gla_fwd", grid=(nb,),
        in_specs=[pl.BlockSpec((tl, DK), fw(CB_Q)), pl.BlockSpec((tl, DK), fw(CB_K)), pl.BlockSpec((tl, DG), fw(CB_V)),
                  pl.BlockSpec((tl, LRW), fw(CB_LR)),
                  pl.BlockSpec((tl, DK), bw(CB_Q)), pl.BlockSpec((tl, DK), bw(CB_K)), pl.BlockSpec((tl, DG), bw(CB_V)),
                  pl.BlockSpec((tl, LRW), bw(CB_LR)),
                  pl.BlockSpec((LRW, 2 * DK), lambda i: (0, 0)), pl.BlockSpec((1, 2 * DK), lambda i: (0, 0))],
        out_specs=[pl.BlockSpec((tl, DG), lambda i: (i, 0)), pl.BlockSpec((tl, DG), lambda i: (nb - 1 - i, 0)),
                   pl.BlockSpec((nc, DG, DK), lambda i: (i, 0, 0)), pl.BlockSpec((nc, DG, DK), lambda i: (nb - 1 - i, 0, 0))],
        out_shape=[jax.ShapeDtypeStruct((L, DG), F32), jax.ShapeDtypeStruct((L, DG), F32),
                   jax.ShapeDtypeStruct((L // CH, DG, DK), BF16), jax.ShapeDtypeStruct((L // CH, DG, DK), BF16)],
        scratch_shapes=[pltpu.VMEM((DG, DK), F32), pltpu.VMEM((DG, DK), F32),
                        pltpu.VMEM((tl, DK), F32), pltpu.VMEM((tl, DK), F32)],
        compiler_params=_cp("arbitrary"),
    )(P, P, P, P, P, P, P, P, gcat, gbias)


def _headnorm(o):
    oh, rs = [], []
    for h in range(NH):
        oo = o[:, h * HV:(h + 1) * HV]
        r = lax.rsqrt(jnp.mean(oo * oo, axis=-1, keepdims=True) + EPS)
        oh.append(oo * r)
        rs.append(r)
    return jnp.concatenate(oh, axis=1), rs


def mix_out(P, o_f, o_b, conv_a, ghn4, w_out, g2, x, tl):
    L = P.shape[0]
    nt = L // tl

    def body(gb, gc, gv, go, gcp, gvp, gcn, gvn, of, ob, ca, gh, wo, g2r, xr, ycat, yr, x1):
        i = pl.program_id(0)
        cp, cn = _halo_rows(gcp, gcn, i, nt - 1)
        vp, vn = _halo_rows(gvp, gvn, i, nt - 1)
        c = gc[...] * gv[...]
        cc, _, _ = _conv3(c, cp * vp, cn * vn, ca)
        ya = gb[...] * cc
        oh, _ = _headnorm(of[...] + ob[...])
        g = go[...]
        yb = g * _sigmoid(g) * (oh * gh[...])
        yc = jnp.concatenate([ya, yb], axis=1).astype(BF16)
        ycat[...] = yc
        y = _mm(yc, wo[...])
        yr[...] = y
        x1[...] = xr[...] + _rms(y, g2r[...])

    t = lambda cb: pl.BlockSpec((tl, DC), lambda i: (i, cb))
    hp = lambda cb: pl.BlockSpec((8, DC), _prev_blk(tl, cb))
    hn = lambda cb: pl.BlockSpec((8, DC), _next_blk(tl, L, cb))
    row = lambda n: pl.BlockSpec((tl, n), lambda i: (i, 0))
    full = lambda a: pl.BlockSpec(a.shape, lambda i: (0, 0))
    return pl.pallas_call(
        body, name="mix_out", grid=(nt,),
        in_specs=[t(CB_GB), t(CB_GC), t(CB_GV), t(CB_GO), hp(CB_GC), hp(CB_GV), hn(CB_GC), hn(CB_GV),
                  row(DG), row(DG), full(conv_a), full(ghn4), full(w_out), full(g2), row(D)],
        out_specs=[row(D), row(D), row(D)],
        out_shape=[jax.ShapeDtypeStruct((L, D), BF16), jax.ShapeDtypeStruct((L, D), F32),
                   jax.ShapeDtypeStruct((L, D), F32)],
        compiler_params=_cp("parallel"),
    )(P, P, P, P, P, P, P, P, o_f, o_b, conv_a, ghn4, w_out, g2, x)


NFF = 2
WFF = DFF // NFF


def ffn_down(U, conv_ffn, w_down, g4, x1, tl):
    L = U.shape[0]
    nt = L // tl

    def body(u, up, un, cf, wd, g4r, x1r, y2, x2):
        i = pl.program_id(0)
        acc = jnp.zeros((tl, D), F32)
        for j in range(NFF):
            gs = slice(j * WFF, (j + 1) * WFF)
            vs = slice(DFF + j * WFF, DFF + (j + 1) * WFF)
            z = []
            for s in (gs, vs):
                p = jnp.where(i == 0, 0.0, up[7:8, s])
                n = jnp.where(i == nt - 1, 0.0, un[0:1, s])
                z.append(_conv3(u[:, s], p, n, cf.at[:, s])[0])
            zz = z[0] * _sigmoid(z[0]) * z[1]
            acc = acc + _mm(zz, wd[gs, :])
        y2[...] = acc
        x2[...] = x1r[...] + _rms(acc, g4r[...])

    row = lambda n: pl.BlockSpec((tl, n), lambda i: (i, 0))
    full = lambda a: pl.BlockSpec(a.shape, lambda i: (0, 0))
    return pl.pallas_call(
        body, name="ffn_down", grid=(nt,),
        in_specs=[row(2 * DFF), pl.BlockSpec((8, 2 * DFF), _prev_blk(tl, 0)), pl.BlockSpec((8, 2 * DFF), _next_blk(tl, L, 0)),
                  full(conv_ffn), full(w_down), full(g4), row(D)],
        out_specs=[row(D), row(D)],
        out_shape=[jax.ShapeDtypeStruct((L, D), F32), jax.ShapeDtypeStruct((L, D), F32)],
        compiler_params=_cp("parallel"),
    )(U, U, U, conv_ffn, w_down, g4, x1)


def loss_head(y, target, tl):
    L = y.shape[0]

    def body(yr, tr, dy, ls):
        @pl.when(pl.program_id(0) == 0)
        def _():
            ls[...] = jnp.zeros_like(ls)

        err = yr[...] - tr[...]
        dy[...] = err * (1.0 / D)
        ls[...] += (0.5 / D) * jnp.sum(err * err)

    row = pl.BlockSpec((tl, D), lambda i: (i, 0))
    return pl.pallas_call(
        body, name="loss_head", grid=(L // tl,), in_specs=[row, row],
        out_specs=[row, pl.BlockSpec((8, 128), lambda i: (0, 0))],
        out_shape=[jax.ShapeDtypeStruct((L, D), F32), jax.ShapeDtypeStruct((8, 128), F32)],
        compiler_params=_cp("arbitrary"),
    )(y, target)


def rms_bwd_pre(dout, y, g, tl):
    L = y.shape[0]

    def body(dr, yr, gr, dy, dg):
        @pl.when(pl.program_id(0) == 0)
        def _():
            dg[...] = jnp.zeros_like(dg)

        a, b = _rms_bwd(dr[...], yr[...], gr[...])
        dy[...] = a.astype(BF16)
        dg[...] += b

    row = pl.BlockSpec((tl, D), lambda i: (i, 0))
    vec = pl.BlockSpec((1, D), lambda i: (0, 0))
    return pl.pallas_call(
        body, name="rms_bwd_pre", grid=(L // tl,), in_specs=[row, row, vec], out_specs=[row, vec],
        out_shape=[jax.ShapeDtypeStruct((L, D), BF16), jax.ShapeDtypeStruct((1, D), F32)],
        compiler_params=_cp("arbitrary"),
    )(dout, y, g)


def ffn_bwd1(dy2, U, conv_ffn, w_down, tl):
    L = U.shape[0]
    nt = L // tl

    def body(dy, ug, uv, ugp, ugn, uvp, uvn, cg, cv, wd, dug, duv, zr, dcg, dcv):
        i = pl.program_id(1)

        @pl.when(i == 0)
        def _():
            dcg[...] = jnp.zeros_like(dcg)
            dcv[...] = jnp.zeros_like(dcv)

        gp, gn = _halo_rows(ugp, ugn, i, nt - 1)
        vp, vn = _halo_rows(uvp, uvn, i, nt - 1)
        a, a_m1, a_p1 = _conv3(ug[...], gp, gn, cg)
        b, b_m1, b_p1 = _conv3(uv[...], vp, vn, cv)
        sg = _sigmoid(a)
        silu = a * sg
        zr[...] = (silu * b).astype(BF16)
        dz = _mm_nt(dy[...], wd[...])
        dval = dz * silu
        dgate = dz * b * (sg * (1.0 + a * (1.0 - sg)))
        dug[...] = dgate
        duv[...] = dval
        for k, (sa, sb) in enumerate(((a_m1, b_m1), (ug[...], uv[...]), (a_p1, b_p1))):
            dcg[k:k + 1, :] += jnp.sum(dgate * sa, axis=0, keepdims=True)
            dcv[k:k + 1, :] += jnp.sum(dval * sb, axis=0, keepdims=True)

    tile = lambda off: pl.BlockSpec((tl, WFF), lambda j, i: (i, off + j))
    prev = lambda off: pl.BlockSpec((8, WFF), lambda j, i: (jnp.maximum(i * (tl // 8) - 1, 0), off + j))
    nxt = lambda off: pl.BlockSpec((8, WFF), lambda j, i: (jnp.minimum((i + 1) * (tl // 8), L // 8 - 1), off + j))
    cw = lambda off: pl.BlockSpec((3, WFF), lambda j, i: (0, off + j))
    acc = pl.BlockSpec((8, WFF), lambda j, i: (0, j))
    return pl.pallas_call(
        body, name="ffn_bwd1", grid=(NFF, nt),
        in_specs=[pl.BlockSpec((tl, D), lambda j, i: (i, 0)), tile(0), tile(NFF), prev(0), nxt(0), prev(NFF), nxt(NFF),
                  cw(0), cw(NFF), pl.BlockSpec((WFF, D), lambda j, i: (j, 0))],
        out_specs=[tile(0), tile(0), tile(0), acc, acc],
        out_shape=[jax.ShapeDtypeStruct((L, DFF), F32), jax.ShapeDtypeStruct((L, DFF), F32),
                   jax.ShapeDtypeStruct((L, DFF), BF16), jax.ShapeDtypeStruct((8, DFF), F32),
                   jax.ShapeDtypeStruct((8, DFF), F32)],
        compiler_params=_cp("parallel", "arbitrary"),
    )(dy2, U, U, U, U, U, U, conv_ffn, conv_ffn, w_down)


def ffn_bwd2(du_g, du_v, conv_ffn, w_up, x1, dres, g3, tl):
    L = x1.shape[0]
    nt = L // tl

    def body(dg_, dv_, dgp, dgn, dvp, dvn, cg, cv, wg, wv, x1r, drr, g3r, dUg, dUv, dx1, dg3, acc):
        i = pl.program_id(0)
        j = pl.program_id(1)
        gp, gn = _halo_rows(dgp, dgn, i, nt - 1)
        vp, vn = _halo_rows(dvp, dvn, i, nt - 1)
        a = _conv3_t(dg_[...], gp, gn, cg).astype(BF16)
        b = _conv3_t(dv_[...], vp, vn, cv).astype(BF16)
        dUg[...] = a
        dUv[...] = b
        part = _mm_nt(a, wg[...]) + _mm_nt(b, wv[...])

        @pl.when(j == 0)
        def _():
            acc[...] = part

        @pl.when(j > 0)
        def _():
            acc[...] += part

        @pl.when((i == 0) & (j == 0))
        def _():
            dg3[...] = jnp.zeros_like(dg3)

        @pl.when(j == NFF - 1)
        def _():
            dx, dg = _rms_bwd(acc[...], x1r[...], g3r[...])
            dx1[...] = drr[...] + dx
            dg3[...] += dg

    tile = pl.BlockSpec((tl, WFF), lambda i, j: (i, j))
    prev = pl.BlockSpec((8, WFF), lambda i, j: (jnp.maximum(i * (tl // 8) - 1, 0), j))
    nxt = pl.BlockSpec((8, WFF), lambda i, j: (jnp.minimum((i + 1) * (tl // 8), L // 8 - 1), j))
    cw = lambda off: pl.BlockSpec((3, WFF), lambda i, j: (0, off + j))
    ww = lambda off: pl.BlockSpec((D, WFF), lambda i, j: (0, off + j))
    row = pl.BlockSpec((tl, D), lambda i, j: (i, 0))
    vec = pl.BlockSpec((1, D), lambda i, j: (0, 0))
    return pl.pallas_call(
        body, name="ffn_bwd2", grid=(nt, NFF),
        in_specs=[tile, tile, prev, nxt, prev, nxt, cw(0), cw(NFF), ww(0), ww(NFF), row, row, vec],
        out_specs=[tile, tile, row, vec],
        out_shape=[jax.ShapeDtypeStruct((L, DFF), BF16), jax.ShapeDtypeStruct((L, DFF), BF16),
                   jax.ShapeDtypeStruct((L, D), F32), jax.ShapeDtypeStruct((1, D), F32)],
        scratch_shapes=[pltpu.VMEM((tl, D), F32)],
        compiler_params=_cp("arbitrary", "arbitrary"),
    )(du_g, du_v, du_g, du_g, du_v, du_v, conv_ffn, conv_ffn, w_up, w_up, x1, dres, g3)


def matmul_tn(a, b, ta, tn, tl, name):
    L, Ka = a.shape
    N = b.shape[1]

    def body(ar, br, o):
        @pl.when(pl.program_id(2) == 0)
        def _():
            o[...] = jnp.zeros_like(o)

        o[...] += _mm_tn(ar[...], br[...])

    return pl.pallas_call(
        body, name=name, grid=(Ka // ta, N // tn, L // tl),
        in_specs=[pl.BlockSpec((tl, ta), lambda p, q, l: (l, p)), pl.BlockSpec((tl, tn), lambda p, q, l: (l, q))],
        out_specs=pl.BlockSpec((ta, tn), lambda p, q, l: (p, q)),
        out_shape=jax.ShapeDtypeStruct((Ka, N), F32),
        compiler_params=_cp("parallel", "parallel", "arbitrary"),
    )(a, b)


def mix_bwd1(dy, w_out, P, o_f, o_b, conv_a, ghn4, tl):
    L = P.shape[0]
    nt = L // tl

    def body(dyr, wo, gb, gc, gv, go, gcp, gvp, gcn, gvn, of, ob, ca, gh, dgb, dcc, dgo, do, dca, dgh):
        i = pl.program_id(0)

        @pl.when(i == 0)
        def _():
            dca[...] = jnp.zeros_like(dca)
            dgh[...] = jnp.zeros_like(dgh)

        dycat = _mm_nt(dyr[...], wo[...])
        dya = dycat[:, 0:DC]
        dyb = dycat[:, DC:D]
        cp, cn = _halo_rows(gcp, gcn, i, nt - 1)
        vp, vn = _halo_rows(gvp, gvn, i, nt - 1)
        c = gc[...] * gv[...]
        cc, c_m1, c_p1 = _conv3(c, cp * vp, cn * vn, ca)
        dgb[...] = dya * cc
        d = dya * gb[...]
        dcc[...] = d
        for k, s in enumerate((c_m1, c, c_p1)):
            dca[k:k + 1, :] += jnp.sum(d * s, axis=0, keepdims=True)
        oh, rs = _headnorm(of[...] + ob[...])
        g = go[...]
        sg = _sigmoid(g)
        silu = g * sg
        dgo[...] = dyb * (oh * gh[...]) * (sg * (1.0 + g * (1.0 - sg)))
        don = dyb * silu
        t = jnp.sum(don * oh, axis=0, keepdims=True)
        dgh[0:1, :] += t[:, 0:HV] + t[:, HV:2 * HV] + t[:, 2 * HV:3 * HV] + t[:, 3 * HV:4 * HV]
        doh = don * gh[...]
        parts = []
        for h in range(NH):
            hs = slice(h * HV, (h + 1) * HV)
            parts.append(rs[h] * (doh[:, hs] - oh[:, hs] * jnp.mean(doh[:, hs] * oh[:, hs], axis=-1, keepdims=True)))
        do[...] = jnp.concatenate(parts, axis=1)

    t = lambda cb: pl.BlockSpec((tl, DC), lambda i: (i, cb))
    hp = lambda cb: pl.BlockSpec((8, DC), _prev_blk(tl, cb))
    hn = lambda cb: pl.BlockSpec((8, DC), _next_blk(tl, L, cb))
    row = lambda n: pl.BlockSpec((tl, n), lambda i: (i, 0))
    full = lambda a: pl.BlockSpec(a.shape, lambda i: (0, 0))
    f32o = lambda n: jax.ShapeDtypeStruct((L, n), F32)
    return pl.pallas_call(
        body, name="mix_bwd1", grid=(nt,),
        in_specs=[row(D), full(w_out), t(CB_GB), t(CB_GC), t(CB_GV), t(CB_GO), hp(CB_GC), hp(CB_GV), hn(CB_GC), hn(CB_GV),
                  row(DG), row(DG), full(conv_a), full(ghn4)],
        out_specs=[row(DC), row(DC), row(DG), row(DG), pl.BlockSpec((8, DC), lambda i: (0, 0)),
                   pl.BlockSpec((8, HV), lambda i: (0, 0))],
        out_shape=[f32o(DC), f32o(DC), f32o(DG), f32o(DG), jax.ShapeDtypeStruct((8, DC), F32),
                   jax.ShapeDtypeStruct((8, HV), F32)],
        compiler_params=_cp("arbitrary"),
    )(dy, w_out, P, P, P, P, P, P, P, P, o_f, o_b, conv_a, ghn4)


def _gla_chunk_bwd(q, k, v, a, do, st16, g_ref, m, rev):
    p = _gla_prep(q, k, a, m, rev)
    g = g_ref[...]
    dob = _tile4(do) * m["bdo"]
    dv = _mm_tn(p["sc"], dob) + _mm_nt(p["kh"], g)
    dsc = jnp.where(m["triu4"] if rev else m["tril4"], _mm_nt(dob, v), 0.0)
    r1 = _mm(dsc, p["kt"]) * m["bdq"]
    dqt = r1[0:CH] + r1[CH:2 * CH] + r1[2 * CH:3 * CH] + r1[3 * CH:4 * CH] + _mm(do, st16)
    dkt = _mm_tn(dsc, p["qs"])
    dkh = _mm(v, g)
    dd = jnp.sum(g * st16.astype(F32), axis=0, keepdims=True)
    g_ref[...] = g * p["dec"] + _mm_tn(do, p["qt"]) * m["bds"]
    kk = dkh * p["kh"]
    dcum = dqt * p["qt"] - dkt * p["kt"] - kk
    dtot = jnp.sum(kk, axis=0, keepdims=True) + dd * p["dec"]
    da = _mm_hi(m["tril"] if rev else m["triu"], dcum) + dtot
    dq = dqt * p["e"] * QSCALE
    dk = dkt * p["einv"] + dkh * p["eout"]
    return dq, dk, dv, da


def gla_bwd(P, do, sf, sb, gcat, gbias, tl):
    L = P.shape[0]
    nb = L // tl
    nc = tl // CH

    def body(qf, kf, vf, lf, dof, sfr, qb, kb, vb, lb, dob, sbr, gc_ref, bs_ref,
             dqf, dkf, dvf, daf, dqb, dkb, dvb, dab, gf, gbk, af, ab):
        @pl.when(pl.program_id(0) == 0)
        def _():
            gf[...] = jnp.zeros_like(gf)
            gbk[...] = jnp.zeros_like(gbk)

        af[...] = _logsig(_mm(lf[...], gc_ref[:, 0:DK]) + bs_ref[:, 0:DK]) * GATE_NORM
        ab[...] = _logsig(_mm(lb[...], gc_ref[:, DK:2 * DK]) + bs_ref[:, DK:2 * DK]) * GATE_NORM
        m = _gla_masks()

        def chunk(c, carry):
            cf = nc - 1 - c
            rows = pl.ds(pl.multiple_of(cf * CH, CH), CH)
            dq, dk, dv, da = _gla_chunk_bwd(qf[rows, :], kf[rows, :], vf[rows, :], af[rows, :], dof[rows, :],
                                            sfr[cf], gf, m, False)
            dqf[rows, :] = dq
            dkf[rows, :] = dk
            dvf[rows, :] = dv
            daf[rows, :] = da
            rows = pl.ds(pl.multiple_of(c * CH, CH), CH)
            dq, dk, dv, da = _gla_chunk_bwd(qb[rows, :], kb[rows, :], vb[rows, :], ab[rows, :], dob[rows, :],
                                            sbr[c], gbk, m, True)
            dqb[rows, :] = dq
            dkb[rows, :] = dk
            dvb[rows, :] = dv
            dab[rows, :] = da
            return carry

        lax.fori_loop(0, nc, chunk, 0)

    fwd_dir = lambda cb: (lambda i: (nb - 1 - i, cb))
    bwd_dir = lambda cb: (lambda i: (i, cb))

    def side(ix):
        return [pl.BlockSpec((tl, DK), ix(CB_Q)), pl.BlockSpec((tl, DK), ix(CB_K)), pl.BlockSpec((tl, DG), ix(CB_V)),
                pl.BlockSpec((tl, LRW), ix(CB_LR)), pl.BlockSpec((tl, DG), ix(0)),
                pl.BlockSpec((nc, DG, DK), lambda i: (ix(0)(i)[0], 0, 0))]

    def outs(ix):
        return [pl.BlockSpec((tl, DK), ix(0)), pl.BlockSpec((tl, DK), ix(0)), pl.BlockSpec((tl, DG), ix(0)),
                pl.BlockSpec((tl, DK), ix(0))]

    o_shape = [jax.ShapeDtypeStruct((L, DK), F32), jax.ShapeDtypeStruct((L, DK), F32),
               jax.ShapeDtypeStruct((L, DG), F32), jax.ShapeDtypeStruct((L, DK), F32)]
    return pl.pallas_call(
        body, name="gla_bwd", grid=(nb,),
        in_specs=side(fwd_dir) + side(bwd_dir) + [pl.BlockSpec((LRW, 2 * DK), lambda i: (0, 0)),
                                                  pl.BlockSpec((1, 2 * DK), lambda i: (0, 0))],
        out_specs=outs(fwd_dir) + outs(bwd_dir),
        out_shape=o_shape + o_shape,
        scratch_shapes=[pltpu.VMEM((DG, DK), F32), pltpu.VMEM((DG, DK), F32),
                        pltpu.VMEM((tl, DK), F32), pltpu.VMEM((tl, DK), F32)],
        compiler_params=_cp("arbitrary"),
    )(P, P, P, P, do, sf, P, P, P, P, do, sb, gcat, gbias)


def mix_bwd2(dgb, dcc, dgo, gl, P, conv_a, gcat, gbias, w_in, x, dres, g1, tl):
    L = P.shape[0]
    nt = L // tl

    def body(dgbr, dccr, dccp, dccn, dgor, dqf, dkf, dvf, daf, dqb, dkb, dvb, dab, gc, gv, lr, ca, gcr, bsr, wi,
             xr, drr, g1r, dP, dx, dg1, dgcat, dbias):
        i = pl.program_id(0)

        @pl.when(i == 0)
        def _():
            dg1[...] = jnp.zeros_like(dg1)
            dgcat[...] = jnp.zeros_like(dgcat)
            dbias[...] = jnp.zeros_like(dbias)

        p, n = _halo_rows(dccp, dccn, i, nt - 1)
        dc = _conv3_t(dccr[...], p, n, ca)
        pre = _mm(lr[...], gcr[...]) + bsr[...]
        da = jnp.concatenate([daf[...], dab[...]], axis=1)
        dpre = da * GATE_NORM * (1.0 - _sigmoid(pre))
        dpre16 = dpre.astype(BF16)
        dP[:, 0:DC] = dgbr[...].astype(BF16)
        dP[:, DC:2 * DC] = (dc * gv[...]).astype(BF16)
        dP[:, 2 * DC:3 * DC] = (dc * gc[...]).astype(BF16)
        dP[:, 1536:1792] = (dqf[...] + dqb[...]).astype(BF16)
        dP[:, 1792:2048] = (dkf[...] + dkb[...]).astype(BF16)
        dP[:, 2048:2560] = (dvf[...] + dvb[...]).astype(BF16)
        dP[:, 2560:3072] = dgor[...].astype(BF16)
        dP[:, 3072:3200] = _mm_nt(dpre16, gcr[...]).astype(BF16)
        dgcat[...] += _mm_tn(lr[...], dpre16)
        dbias[0:1, :] += jnp.sum(dpre, axis=0, keepdims=True)
        dh, dg = _rms_bwd(_mm_nt(dP[...], wi[...]), xr[...], g1r[...])
        dx[...] = drr[...] + dh
        dg1[...] += dg

    row = lambda n: pl.BlockSpec((tl, n), lambda i: (i, 0))
    t = lambda w, cb: pl.BlockSpec((tl, w), lambda i: (i, cb))
    full = lambda a: pl.BlockSpec(a.shape, lambda i: (0, 0))
    return pl.pallas_call(
        body, name="mix_bwd2", grid=(nt,),
        in_specs=[row(DC), row(DC), pl.BlockSpec((8, DC), _prev_blk(tl, 0)), pl.BlockSpec((8, DC), _next_blk(tl, L, 0)),
                  row(DG), row(DK), row(DK), row(DG), row(DK), row(DK), row(DK), row(DG), row(DK),
                  t(DC, CB_GC), t(DC, CB_GV), t(LRW, CB_LR), full(conv_a), full(gcat), full(gbias), full(w_in),
                  row(D), row(D), full(g1)],
        out_specs=[row(DINP), row(D), pl.BlockSpec((1, D), lambda i: (0, 0)), pl.BlockSpec((LRW, 2 * DK), lambda i: (0, 0)),
                   pl.BlockSpec((8, 2 * DK), lambda i: (0, 0))],
        out_shape=[jax.ShapeDtypeStruct((L, DINP), BF16), jax.ShapeDtypeStruct((L, D), F32),
                   jax.ShapeDtypeStruct((1, D), F32), jax.ShapeDtypeStruct((LRW, 2 * DK), F32),
                   jax.ShapeDtypeStruct((8, 2 * DK), F32)],
        compiler_params=_cp("arbitrary"),
    )(dgb, dcc, dcc, dcc, dgo, *gl, P, P, P, conv_a, gcat, gbias, w_in, x, dres, g1)


def _row_tile(rows, cols):
    if rows * cols * 4 <= 2 * 1024 * 1024:
        return rows
    best = 8
    for t in range(8, rows, 8):
        if rows % t == 0 and t * cols * 4 <= 2 * 1024 * 1024:
            best = t
    return best


def adamw(w, g, m, v, name):
    shape = w.shape
    cols = shape[-1]
    w2, g2, m2, v2 = (a.reshape(-1, cols) for a in (w, g, m, v))
    rows = w2.shape[0]
    tr = _row_tile(rows, cols)

    def body(wr, gr, mr, vr, dl, nm, nv):
        gg = gr[...]
        mm = B1 * mr[...] + (1.0 - B1) * gg
        vv = B2 * vr[...] + (1.0 - B2) * (gg * gg)
        m_hat = mm / (1.0 - B1 ** STEP)
        v_hat = vv / (1.0 - B2 ** STEP)
        dl[...] = -LR * (m_hat / (jnp.sqrt(v_hat) + AEPS) + WD * wr[...])
        nm[...] = mm
        nv[...] = vv

    blk = pl.BlockSpec((tr, cols), lambda i: (i, 0))
    o = jax.ShapeDtypeStruct((rows, cols), F32)
    d, nm, nv = pl.pallas_call(
        body, name=name, grid=(rows // tr,), in_specs=[blk] * 4, out_specs=[blk] * 3, out_shape=[o, o, o],
        compiler_params=_cp("parallel"),
    )(w2, g2, m2, v2)
    return d.reshape(shape), nm.reshape(shape), nv.reshape(shape)


def _place():
    return lax.axis_index("x"), lax.axis_index("y"), lax.axis_index("c")


def allgather8(v, name):
    mp, n = v.shape

    def body(x_ref, out_ref, send_sems, recv_sems, local_sem):
        x, y, c = _place()
        me, sibling = (x, y, c), (x, y, 1 - c)
        chips = [(1 - x, y), (x, 1 - y), (1 - x, 1 - y)]

        def rows(px, py, pc):
            return out_ref.at[pl.ds((4 * px + 2 * py + pc) * mp, mp), :]

        def copy(k, block, to, src=None):
            return pltpu.make_async_remote_copy(
                src_ref=rows(*block) if src is None else src, dst_ref=rows(*block),
                send_sem=send_sems.at[k], recv_sem=recv_sems.at[k], device_id=to, device_id_type=MESH)

        mine = pltpu.make_async_copy(x_ref, rows(*me), local_sem)
        mine.start()
        first = [copy(0, me, sibling, src=x_ref)]
        first += [copy(1 + j, me, (*chip, c), src=x_ref) for j, chip in enumerate(chips)]
        for cp in first:
            cp.start()
        passed = [copy(4 + j, (*chip, c), sibling) for j, chip in enumerate(chips)]
        for j, chip in enumerate(chips):
            copy(1 + j, (*chip, c), me).wait_recv()
            passed[j].start()
        copy(0, sibling, me).wait_recv()
        for j, chip in enumerate(chips):
            copy(4 + j, (*chip, 1 - c), me).wait_recv()
        for cp in first + passed:
            cp.wait_send()
        mine.wait()

    return pl.pallas_call(
        body, name=name, out_shape=jax.ShapeDtypeStruct((8 * mp, n), v.dtype),
        in_specs=[pl.BlockSpec(memory_space=pltpu.VMEM)], out_specs=pl.BlockSpec(memory_space=pltpu.VMEM),
        scratch_shapes=[pltpu.SemaphoreType.DMA((7,)), pltpu.SemaphoreType.DMA((7,)), pltpu.SemaphoreType.DMA],
        compiler_params=pltpu.CompilerParams(vmem_limit_bytes=VMEM_LIMIT),
    )(v)


def sum8(v, mp):
    def body(x_ref, o_ref):
        acc = x_ref[0:mp, :]
        for d in range(1, 8):
            acc = acc + x_ref[d * mp:(d + 1) * mp, :]
        o_ref[...] = acc

    return pl.pallas_call(body, name="sum8", out_shape=jax.ShapeDtypeStruct((mp, v.shape[1]), F32),
                          compiler_params=pltpu.CompilerParams(vmem_limit_bytes=VMEM_LIMIT))(v)


def allgather_weights(shard):
    def body(s_ref, o_ref, send_sems, recv_sems, local_sem):
        x, y, c = _place()
        me = 2 * x + y
        sibling = (x, y, 1 - c)
        chips = [(1 - x, y), (x, 1 - y), (1 - x, 1 - y)]

        def half(ref, h):
            return ref.at[pl.ds(h * HALF, HALF), :]

        def copy(k, src, dst, to):
            return pltpu.make_async_remote_copy(src_ref=src, dst_ref=dst, send_sem=send_sems.at[k],
                                                recv_sem=recv_sems.at[k], device_id=to, device_id_type=MESH)

        own = pltpu.make_async_copy(s_ref, o_ref.at[me], local_sem)
        own.start()
        first = [copy(k, half(s_ref, c), half(o_ref.at[me], c), (px, py, c)) for k, (px, py) in enumerate(chips)]
        for cp in first:
            cp.start()
        passed = []
        for k, (px, py) in enumerate(chips):
            got = half(o_ref.at[2 * px + py], c)
            copy(k, half(s_ref, c), got, (px, py, c)).wait_recv()
            cp = copy(3 + k, got, got, sibling)
            cp.start()
            passed.append(cp)
        for k, (px, py) in enumerate(chips):
            got = half(o_ref.at[2 * px + py], 1 - c)
            copy(3 + k, got, got, sibling).wait_recv()
        for cp in first + passed:
            cp.wait_send()
        own.wait()

    return pl.pallas_call(
        body, name="allgather_weights", out_shape=jax.ShapeDtypeStruct((4, R_PACK, D), BF16),
        in_specs=[pl.BlockSpec(memory_space=pl.ANY)], out_specs=pl.BlockSpec(memory_space=pl.ANY),
        scratch_shapes=[pltpu.SemaphoreType.DMA((6,)), pltpu.SemaphoreType.DMA((6,)), pltpu.SemaphoreType.DMA],
    )(shard)


def rs_sibling_halves(gpack):
    def body(g_ref, r_ref, send_sem, recv_sem):
        x, y, c = _place()
        cp = pltpu.make_async_remote_copy(
            src_ref=g_ref.at[:, pl.ds((1 - c) * HALF, HALF), :], dst_ref=r_ref, send_sem=send_sem, recv_sem=recv_sem,
            device_id=(x, y, 1 - c), device_id_type=MESH)
        cp.start()
        cp.wait()

    return pl.pallas_call(
        body, name="rs_sibling_halves", out_shape=jax.ShapeDtypeStruct((4, HALF, D), F32),
        in_specs=[pl.BlockSpec(memory_space=pl.ANY)], out_specs=pl.BlockSpec(memory_space=pl.ANY),
        scratch_shapes=[pltpu.SemaphoreType.DMA, pltpu.SemaphoreType.DMA],
    )(gpack)


TR_RS = 640


def rs_chipsum16(gpack, recv1, cidx):
    nrt = HALF // TR_RS

    def body(c_ref, g_ref, r_ref, o_ref):
        o_ref[...] = (g_ref[...] + r_ref[...]).astype(BF16)

    return pl.pallas_call(
        body, name="rs_chipsum16", out_shape=jax.ShapeDtypeStruct((4, HALF, D), BF16),
        grid_spec=pltpu.PrefetchScalarGridSpec(
            num_scalar_prefetch=1, grid=(4, nrt),
            in_specs=[pl.BlockSpec((1, TR_RS, D), lambda j, r, c: (j, c[0] * nrt + r, 0)),
                      pl.BlockSpec((1, TR_RS, D), lambda j, r, c: (j, r, 0))],
            out_specs=pl.BlockSpec((1, TR_RS, D), lambda j, r, c: (j, r, 0))),
        compiler_params=_cp("parallel", "parallel"),
    )(cidx, gpack, recv1)


def rs_exchange_chips(cs16):
    def body(s_ref, r_ref, send_sems, recv_sems):
        x, y, c = _place()
        chips = [(1 - x, y), (x, 1 - y), (1 - x, 1 - y)]
        cps = [pltpu.make_async_remote_copy(
            src_ref=s_ref.at[2 * px + py], dst_ref=r_ref.at[k], send_sem=send_sems.at[k], recv_sem=recv_sems.at[k],
            device_id=(px, py, c), device_id_type=MESH) for k, (px, py) in enumerate(chips)]
        for cp in cps:
            cp.start()
        for cp in cps:
            cp.wait()

    return pl.pallas_call(
        body, name="rs_exchange_chips", out_shape=jax.ShapeDtypeStruct((3, HALF, D), BF16),
        in_specs=[pl.BlockSpec(memory_space=pl.ANY)], out_specs=pl.BlockSpec(memory_space=pl.ANY),
        scratch_shapes=[pltpu.SemaphoreType.DMA((3,)), pltpu.SemaphoreType.DMA((3,))],
    )(cs16)


def rs_final_sum(gpack, recv1, recv2, idx):
    nrt = HALF // TR_RS

    def body(i_ref, g_ref, r1_ref, r2_ref, o_ref):
        acc = g_ref[0] + r1_ref[0]
        for k in range(3):
            acc = acc + r2_ref[k].astype(F32)
        o_ref[...] = acc

    return pl.pallas_call(
        body, name="rs_final_sum", out_shape=jax.ShapeDtypeStruct((HALF, D), F32),
        grid_spec=pltpu.PrefetchScalarGridSpec(
            num_scalar_prefetch=1, grid=(nrt,),
            in_specs=[pl.BlockSpec((1, TR_RS, D), lambda r, ix: (ix[0], ix[1] * nrt + r, 0)),
                      pl.BlockSpec((1, TR_RS, D), lambda r, ix: (ix[0], r, 0)),
                      pl.BlockSpec((3, TR_RS, D), lambda r, ix: (0, r, 0))],
            out_specs=pl.BlockSpec((TR_RS, D), lambda r, ix: (r, 0))),
        compiler_params=_cp("parallel"),
    )(idx, gpack, recv1, recv2)


def rs_share_halves(ghalf):
    def body(h_ref, o_ref, send_sem, recv_sem, local_sem):
        x, y, c = _place()
        mine = o_ref.at[pl.ds(c * HALF, HALF), :]
        theirs = o_ref.at[pl.ds((1 - c) * HALF, HALF), :]
        own = pltpu.make_async_copy(h_ref, mine, local_sem)
        own.start()
        cp = pltpu.make_async_remote_copy(src_ref=h_ref, dst_ref=mine, send_sem=send_sem, recv_sem=recv_sem,
                                          device_id=(x, y, 1 - c), device_id_type=MESH)
        cp.start()
        cp.wait_send()
        pltpu.make_async_remote_copy(src_ref=h_ref, dst_ref=theirs, send_sem=send_sem, recv_sem=recv_sem,
                                     device_id=(x, y, 1 - c), device_id_type=MESH).wait_recv()
        own.wait()

    return pl.pallas_call(
        body, name="rs_share_halves", out_shape=jax.ShapeDtypeStruct((R_PACK, D), F32),
        in_specs=[pl.BlockSpec(memory_space=pl.ANY)], out_specs=pl.BlockSpec(memory_space=pl.ANY),
        scratch_shapes=[pltpu.SemaphoreType.DMA, pltpu.SemaphoreType.DMA, pltpu.SemaphoreType.DMA],
    )(ghalf)


def _pack_rows(w_in_s, w_out_s, w_up_s, w_down_s, dtype):
    parts = [w_in_s.reshape(R_IN, D), w_out_s.reshape(R_OUT, D), w_up_s.reshape(R_UP, D), w_down_s.reshape(R_DOWN, D),
             jnp.zeros((R_PACK - R_USED, D), w_in_s.dtype)]
    return jnp.concatenate(parts, axis=0).astype(dtype)


def _unpack_rows(p):
    o = 0
    out = []
    for rows, shape in ((R_IN, (DEPTH, D, DIN // 4)), (R_OUT, (DEPTH, D // 4, D)), (R_UP, (DEPTH, D, 2 * DFF // 4)),
                        (R_DOWN, (DEPTH, DFF // 4, D))):
        out.append(p[o:o + rows].reshape(shape))
        o += rows
    return out


SMALL_W = (("conv_a", 3 * 128), ("gate_up_fwd", RK * 64), ("gate_up_bwd", RK * 64), ("conv_ffn", 3 * 1408))


def kernel(x, norm_mix_pre, norm_mix_post, norm_ffn_pre, norm_ffn_post, w_in, conv_a, gate_up_fwd, gate_bias_fwd, gate_up_bwd, gate_bias_bwd, gla_head_norm, w_out, w_up, conv_ffn, w_down, loss_target, m_norm_mix_pre, m_norm_mix_post, m_norm_ffn_pre, m_norm_ffn_post, m_w_in, m_conv_a, m_gate_up_fwd, m_gate_bias_fwd, m_gate_up_bwd, m_gate_bias_bwd, m_gla_head_norm, m_w_out, m_w_up, m_conv_ffn, m_w_down, v_norm_mix_pre, v_norm_mix_post, v_norm_ffn_pre, v_norm_ffn_post, v_w_in, v_conv_a, v_gate_up_fwd, v_gate_bias_fwd, v_gate_up_bwd, v_gate_bias_bwd, v_gla_head_norm, v_w_out, v_w_up, v_conv_ffn, v_w_down):
    L = x.shape[1]
    xi, yi, ci = _place()
    chip = 2 * xi + yi
    tl_gla, tl_mix, tl_ffn = min(L, TL_GLA), min(L, TL_MIX), min(L, TL_FFN)

    wg = allgather_weights(_pack_rows(w_in, w_out, w_up, w_down, BF16))
    per_chip = [_unpack_rows(wg[j]) for j in range(4)]
    W_in = jnp.concatenate([p[0] for p in per_chip], axis=2)
    W_in = jnp.pad(W_in, ((0, 0), (0, 0), (0, DINP - DIN)))
    W_out = jnp.concatenate([p[1] for p in per_chip], axis=1)
    W_up = jnp.concatenate([p[2] for p in per_chip], axis=2)
    W_down = jnp.concatenate([p[3] for p in per_chip], axis=1)

    small = jnp.concatenate([conv_a.reshape(-1), gate_up_fwd.reshape(-1), gate_up_bwd.reshape(-1), conv_ffn.reshape(-1)])
    ms = small.shape[0] // 128
    sg = allgather8(small.reshape(ms, 128), "allgather_small_weights").reshape(4, 2, ms * 128)[:, 0]

    def small_full(off, shape):
        n = shape[0] * shape[1] * shape[2]
        return jnp.concatenate([sg[j, off:off + n].reshape(shape) for j in range(4)], axis=2)

    o1 = DEPTH * 3 * 128
    o2 = o1 + DEPTH * RK * 64
    o3 = o2 + DEPTH * RK * 64
    conv_a_f = small_full(0, (DEPTH, 3, 128))
    gup_f = small_full(o1, (DEPTH, RK, 64))
    gup_b = small_full(o2, (DEPTH, RK, 64))
    conv_ffn_f = small_full(o3, (DEPTH, 3, 1408))

    def gcat_of(l):
        g = jnp.zeros((LRW, 2 * DK), F32)
        g = g.at[0:RK, 0:DK].set(gup_f[l]).at[RK:2 * RK, DK:2 * DK].set(gup_b[l])
        return g.astype(BF16)

    gcats = [gcat_of(l) for l in range(DEPTH)]
    gbiases = [jnp.concatenate([gate_bias_fwd[l], gate_bias_bwd[l]])[None, :] for l in range(DEPTH)]
    ghn4s = [jnp.tile(gla_head_norm[l], NH)[None, :] for l in range(DEPTH)]

    xc = x.reshape(L, D)
    saved = []
    for l in range(DEPTH):
        P, h1 = rms_matmul(xc, norm_mix_pre[l][None, :], W_in[l], 640, "proj_in")
        o_f, o_b, sf, sb = gla_fwd(P, gcats[l], gbiases[l], tl_gla)
        ycat, y, x1 = mix_out(P, o_f, o_b, conv_a_f[l], ghn4s[l], W_out[l], norm_mix_post[l][None, :], xc, tl_mix)
        U, h2 = rms_matmul(x1, norm_ffn_pre[l][None, :], W_up[l], 512, "proj_up")
        y2, x2 = ffn_down(U, conv_ffn_f[l], W_down[l], norm_ffn_post[l][None, :], x1, tl_ffn)
        saved.append(dict(x=xc, h1=h1, P=P, o_f=o_f, o_b=o_b, sf=sf, sb=sb, ycat=ycat, y=y, x1=x1, h2=h2, U=U, y2=y2))
        xc = x2

    dx, loss_blk = loss_head(xc, loss_target.reshape(L, D), tl_mix)

    grads = [None] * DEPTH
    for l in reversed(range(DEPTH)):
        s = saved[l]
        dy2, dg4 = rms_bwd_pre(dx, s["y2"], norm_ffn_post[l][None, :], tl_mix)
        du_g, du_v, z, dcf_g, dcf_v = ffn_bwd1(dy2, s["U"], conv_ffn_f[l], W_down[l], tl_ffn)
        dW_down = matmul_tn(z, dy2, DFF // 2, D, min(L, 1024), "dw_down")
        dU_g, dU_v, dx1, dg3 = ffn_bwd2(du_g, du_v, conv_ffn_f[l], W_up[l], s["x1"], dx, norm_ffn_pre[l][None, :], tl_ffn)
        dW_up = jnp.concatenate([matmul_tn(s["h2"], dU_g, D, WFF, min(L, 1024), "dw_up_gate"),
                                 matmul_tn(s["h2"], dU_v, D, WFF, min(L, 1024), "dw_up_val")], axis=1)
        dy, dg2 = rms_bwd_pre(dx1, s["y"], norm_mix_post[l][None, :], tl_mix)
        dgb, dcc, dgo, do, dca, dghn = mix_bwd1(dy, W_out[l], s["P"], s["o_f"], s["o_b"], conv_a_f[l], ghn4s[l], tl_mix)
        dW_out = matmul_tn(s["ycat"], dy, D, D, min(L, 1024), "dw_out")
        gl = gla_bwd(s["P"], do, s["sf"], s["sb"], gcats[l], gbiases[l], tl_gla)
        dP, dx, dg1, dgcat, dbias = mix_bwd2(dgb, dcc, dgo, gl, s["P"], conv_a_f[l], gcats[l], gbiases[l], W_in[l],
                                             s["x"], dx1, norm_mix_pre[l][None, :], tl_mix)
        dW_in = matmul_tn(s["h1"], dP, D, 640, min(L, 1024), "dw_in")
        grads[l] = dict(
            norm_mix_pre=dg1[0], norm_mix_post=dg2[0], norm_ffn_pre=dg3[0], norm_ffn_post=dg4[0],
            w_in=dW_in[:, :DIN], conv_a=dca[0:3], gate_up_fwd=dgcat[0:RK, 0:DK], gate_bias_fwd=dbias[0, 0:DK],
            gate_up_bwd=dgcat[RK:2 * RK, DK:2 * DK], gate_bias_bwd=dbias[0, DK:2 * DK], gla_head_norm=dghn[0],
            w_out=dW_out, w_up=dW_up, conv_ffn=jnp.concatenate([dcf_g[0:3], dcf_v[0:3]], axis=1), w_down=dW_down)

    G = {k: jnp.stack([grads[l][k] for l in range(DEPTH)]) for k in grads[0]}

    small_names = ["norm_mix_pre", "norm_mix_post", "norm_ffn_pre", "norm_ffn_post", "conv_a", "gate_up_fwd",
                   "gate_bias_fwd", "gate_up_bwd", "gate_bias_bwd", "gla_head_norm", "conv_ffn"]
    flat = jnp.concatenate([G[k].reshape(-1) for k in small_names] + [loss_blk[0, 0:1]])
    n_small = flat.shape[0]
    mp = -(-n_small // 1024) * 8
    flat = jnp.pad(flat, (0, mp * 128 - n_small)).reshape(mp, 128)
    tot = sum8(allgather8(flat, "allgather_small_grads"), mp).reshape(-1)
    gsm = {}
    o = 0
    for k in small_names:
        n = G[k].size
        gsm[k] = tot[o:o + n].reshape(G[k].shape)
        o += n
    loss = tot[o]

    def my_cols(a, width):
        return lax.dynamic_slice_in_dim(a, chip * width, width, axis=2)

    gsm["conv_a"] = my_cols(gsm["conv_a"], 128)
    gsm["gate_up_fwd"] = my_cols(gsm["gate_up_fwd"], 64)
    gsm["gate_up_bwd"] = my_cols(gsm["gate_up_bwd"], 64)
    gsm["conv_ffn"] = my_cols(gsm["conv_ffn"], 1408)

    gpack = jnp.stack([_pack_rows(G["w_in"][:, :, 776 * j:776 * (j + 1)], G["w_out"][:, 256 * j:256 * (j + 1), :],
                                  G["w_up"][:, :, 1408 * j:1408 * (j + 1)], G["w_down"][:, 704 * j:704 * (j + 1), :], F32)
                       for j in range(4)])
    recv1 = rs_sibling_halves(gpack)
    cs16 = rs_chipsum16(gpack, recv1, jnp.reshape(ci, (1,)).astype(jnp.int32))
    recv2 = rs_exchange_chips(cs16)
    ghalf = rs_final_sum(gpack, recv1, recv2, jnp.stack([chip, ci]).astype(jnp.int32))
    g_in, g_out, g_up, g_down = _unpack_rows(rs_share_halves(ghalf))
    gsm.update(w_in=g_in, w_out=g_out, w_up=g_up, w_down=g_down)

    names = ["norm_mix_pre", "norm_mix_post", "norm_ffn_pre", "norm_ffn_post", "w_in", "conv_a", "gate_up_fwd",
             "gate_bias_fwd", "gate_up_bwd", "gate_bias_bwd", "gla_head_norm", "w_out", "w_up", "conv_ffn", "w_down"]
    w = dict(norm_mix_pre=norm_mix_pre, norm_mix_post=norm_mix_post, norm_ffn_pre=norm_ffn_pre, norm_ffn_post=norm_ffn_post,
             w_in=w_in, conv_a=conv_a, gate_up_fwd=gate_up_fwd, gate_bias_fwd=gate_bias_fwd, gate_up_bwd=gate_up_bwd,
             gate_bias_bwd=gate_bias_bwd, gla_head_norm=gla_head_norm, w_out=w_out, w_up=w_up, conv_ffn=conv_ffn, w_down=w_down)
    m = dict(norm_mix_pre=m_norm_mix_pre, norm_mix_post=m_norm_mix_post, norm_ffn_pre=m_norm_ffn_pre, norm_ffn_post=m_norm_ffn_post,
             w_in=m_w_in, conv_a=m_conv_a, gate_up_fwd=m_gate_up_fwd, gate_bias_fwd=m_gate_bias_fwd, gate_up_bwd=m_gate_up_bwd,
             gate_bias_bwd=m_gate_bias_bwd, gla_head_norm=m_gla_head_norm, w_out=m_w_out, w_up=m_w_up, conv_ffn=m_conv_ffn, w_down=m_w_down)
    v = dict(norm_mix_pre=v_norm_mix_pre, norm_mix_post=v_norm_mix_post, norm_ffn_pre=v_norm_ffn_pre, norm_ffn_post=v_norm_ffn_post,
             w_in=v_w_in, conv_a=v_conv_a, gate_up_fwd=v_gate_up_fwd, gate_bias_fwd=v_gate_bias_fwd, gate_up_bwd=v_gate_up_bwd,
             gate_bias_bwd=v_gate_bias_bwd, gla_head_norm=v_gla_head_norm, w_out=v_w_out, w_up=v_w_up, conv_ffn=v_conv_ffn, w_down=v_w_down)
    upd = {k: adamw(w[k], gsm[k], m[k], v[k], "adamw_" + k) for k in names}
    return (loss, dx.reshape(1, L, D), *[gsm[k] for k in names], *[upd[k][0] for k in names],
            *[upd[k][1] for k in names], *[upd[k][2] for k in names])
```

```python
import functools

import jax
import jax.numpy as jnp
from jax import lax
from jax.experimental import pallas as pl
from jax.experimental.pallas import tpu as pltpu

F32 = jnp.float32
BF16 = jnp.bfloat16
MXU_DTYPE = jnp.bfloat16
MESH = pl.DeviceIdType.MESH

D = 1024
DC = 512
DG = 512
NH = 4
HV = 128
HK = 64
DK = 256
RK = 16
CH = 64
DFF = 2816
DIN = 3104
DINP = 3200
LRW = 128
DEPTH = 4
EPS = 1e-6
QSCALE = HK ** -0.5
GATE_NORM = 1.0 / 16.0
CB_GB, CB_GC, CB_GV, CB_GO = 0, 1, 2, 5
CB_Q, CB_K = 6, 7
CB_V = 4
CB_LR = 24
LR = 0.001
B1 = 0.9
B2 = 0.999
AEPS = 1e-08
WD = 0.01
STEP = 10
TM_PROJ = 1024
TL_GLA = 512
TL_MIX = 256
TL_FFN = 256
VMEM_LIMIT = 56 * 1024 * 1024


def _cp(*sem):
    return pltpu.CompilerParams(dimension_semantics=sem if sem else None, vmem_limit_bytes=VMEM_LIMIT)


def _mm(a, b):
    return jnp.dot(a.astype(MXU_DTYPE), b.astype(MXU_DTYPE), preferred_element_type=F32)


def _mm_nt(a, b):
    return lax.dot_general(a.astype(MXU_DTYPE), b.astype(MXU_DTYPE), (((1,), (1,)), ((), ())),
                           preferred_element_type=F32)


def _mm_tn(a, b):
    return lax.dot_general(a.astype(MXU_DTYPE), b.astype(MXU_DTYPE), (((0,), (0,)), ((), ())),
                           preferred_element_type=F32)


def _mm_hi(a, b):
    return jnp.dot(a, b, precision=lax.Precision.HIGHEST, preferred_element_type=F32)


def _rms(x, g):
    r = lax.rsqrt(jnp.mean(x * x, axis=-1, keepdims=True) + EPS)
    return x * r * g


def _rms_bwd(dout, y, g):
    r = lax.rsqrt(jnp.mean(y * y, axis=-1, keepdims=True) + EPS)
    yh = y * r
    dyh = dout * g
    dy = r * (dyh - yh * jnp.mean(dyh * yh, axis=-1, keepdims=True))
    dg = jnp.sum(dout * yh, axis=0, keepdims=True)
    return dy, dg


def _sigmoid(x):
    return 1.0 / (1.0 + jnp.exp(-x))


def _logsig(x):
    return jnp.minimum(x, 0.0) - jnp.log1p(jnp.exp(-jnp.abs(x)))


def _shift_down(x, first_row):
    row = lax.broadcasted_iota(jnp.int32, x.shape, 0)
    return jnp.where(row == 0, first_row, pltpu.roll(x, 1, 0))


def _shift_up(x, last_row):
    n = x.shape[0]
    row = lax.broadcasted_iota(jnp.int32, x.shape, 0)
    return jnp.where(row == n - 1, last_row, pltpu.roll(x, n - 1, 0))


def _halo_rows(prev_ref, next_ref, i, last):
    p = jnp.where(i == 0, 0.0, prev_ref[7:8, :])
    n = jnp.where(i == last, 0.0, next_ref[0:1, :])
    return p, n


def _conv3(x, xp, xn, w_ref):
    xm1 = _shift_down(x, xp)
    xp1 = _shift_up(x, xn)
    return w_ref[0:1, :] * xm1 + w_ref[1:2, :] * x + w_ref[2:3, :] * xp1, xm1, xp1


def _conv3_t(d, dp, dn, w_ref):
    return w_ref[0:1, :] * _shift_up(d, dn) + w_ref[1:2, :] * d + w_ref[2:3, :] * _shift_down(d, dp)


def _prev_blk(tl, cb):
    return lambda i: (jnp.maximum(i * (tl // 8) - 1, 0), cb)


def _next_blk(tl, nrows, cb):
    return lambda i: (jnp.minimum((i + 1) * (tl // 8), nrows // 8 - 1), cb)


def rms_matmul(x, g, w, tn, name, w_spec=None, n_out=None):
    L = x.shape[0]
    N = w.shape[1] if n_out is None else n_out
    tm = min(L, TM_PROJ)
    if w_spec is None:
        w_spec = pl.BlockSpec((D, tn), lambda i, j: (0, j))

    def body(x_ref, g_ref, w_ref, o_ref, h_ref):
        @pl.when(pl.program_id(1) == 0)
        def _():
            h_ref[...] = _rms(x_ref[...], g_ref[...]).astype(BF16)

        o_ref[...] = _mm(h_ref[...], w_ref[...])

    return pl.pallas_call(
        body, name=name, grid=(L // tm, N // tn),
        in_specs=[pl.BlockSpec((tm, D), lambda i, j: (i, 0)), pl.BlockSpec((1, D), lambda i, j: (0, 0)), w_spec],
        out_specs=[pl.BlockSpec((tm, tn), lambda i, j: (i, j)), pl.BlockSpec((tm, D), lambda i, j: (i, 0))],
        out_shape=[jax.ShapeDtypeStruct((L, N), F32), jax.ShapeDtypeStruct((L, D), BF16)],
        compiler_params=_cp("parallel", "arbitrary"),
    )(x, g, w)


def _gla_masks():
    def blk(shape, rdiv, cdiv):
        r = lax.broadcasted_iota(jnp.int32, shape, 0) // rdiv
        c = lax.broadcasted_iota(jnp.int32, shape, 1) // cdiv
        return (r == c).astype(F32)

    r = lax.broadcasted_iota(jnp.int32, (CH, CH), 0)
    c = lax.broadcasted_iota(jnp.int32, (CH, CH), 1)
    r4 = lax.broadcasted_iota(jnp.int32, (NH * CH, CH), 0) % CH
    c4 = lax.broadcasted_iota(jnp.int32, (NH * CH, CH), 1)
    return dict(
        bdq=blk((NH * CH, DK), CH, HK),
        bdo=blk((NH * CH, DG), CH, HV),
        bds=blk((DG, DK), HV, HK),
        tril=(r >= c).astype(F32), triu=(r <= c).astype(F32),
        tril4=r4 >= c4, triu4=r4 <= c4,
    )


def _tile4(x):
    return jnp.concatenate([x, x, x, x], axis=0)


def _gla_prep(q, k, a, m, rev):
    cum = _mm_hi(m["triu"] if rev else m["tril"], a)
    tot = jnp.sum(a, axis=0, keepdims=True)
    e = jnp.exp(cum)
    einv = jnp.exp(-cum)
    eout = jnp.exp(tot - cum)
    qt = q * QSCALE * e
    kt = k * einv
    kh = k * eout
    qs = _tile4(qt) * m["bdq"]
    sc = _mm_nt(qs, kt)
    sc = jnp.where(m["triu4"] if rev else m["tril4"], sc, 0.0)
    return dict(e=e, einv=einv, eout=eout, dec=jnp.exp(tot), qt=qt, kt=kt, kh=kh, qs=qs, sc=sc)


def _gla_chunk_fwd(q, k, v, a, st_ref, m, rev):
    p = _gla_prep(q, k, a, m, rev)
    r = _mm(p["sc"], v)
    o_intra = jnp.concatenate([r[h * CH:(h + 1) * CH, h * HV:(h + 1) * HV] for h in range(NH)], axis=1)
    st = st_ref[...]
    st16 = st.astype(BF16)
    o = o_intra + _mm_nt(p["qt"], st16)
    st_ref[...] = st * p["dec"] + _mm_tn(v, p["kh"]) * m["bds"]
    return o, st16


def gla_fwd(P, gcat, gbias, tl):
    L = P.shape[0]
    nb = L // tl
    nc = tl // CH

    def body(qf, kf, vf, lf, qb, kb, vb, lb, gc_ref, bs_ref, of, ob, sf, sb, stf, stb, af, ab):
        @pl.when(pl.program_id(0) == 0)
        def _():
            stf[...] = jnp.zeros_like(stf)
            stb[...] = jnp.zeros_like(stb)

        af[...] = _logsig(_mm(lf[...], gc_ref[:, 0:DK]) + bs_ref[:, 0:DK]) * GATE_NORM
        ab[...] = _logsig(_mm(lb[...], gc_ref[:, DK:2 * DK]) + bs_ref[:, DK:2 * DK]) * GATE_NORM
        m = _gla_masks()

        def chunk(c, carry):
            rows = pl.ds(pl.multiple_of(c * CH, CH), CH)
            o, st = _gla_chunk_fwd(qf[rows, :], kf[rows, :], vf[rows, :], af[rows, :], stf, m, False)
            of[rows, :] = o
            sf[c] = st
            cb = nc - 1 - c
            rows = pl.ds(pl.multiple_of(cb * CH, CH), CH)
            o, st = _gla_chunk_fwd(qb[rows, :], kb[rows, :], vb[rows, :], ab[rows, :], stb, m, True)
            ob[rows, :] = o
            sb[cb] = st
            return carry

        lax.fori_loop(0, nc, chunk, 0)

    fw = lambda cb: (lambda i: (i, cb))
    bw = lambda cb: (lambda i: (nb - 1 - i, cb))
    return pl.pallas_call(
        body, name="gla_fwd", grid=(nb,),
        in_specs=[pl.BlockSpec((tl, DK), fw(CB_Q)), pl.BlockSpec((tl, DK), fw(CB_K)), pl.BlockSpec((tl, DG), fw(CB_V)),
                  pl.BlockSpec((tl, LRW), fw(CB_LR)),
                  pl.BlockSpec((tl, DK), bw(CB_Q)), pl.BlockSpec((tl, DK), bw(CB_K)), pl.BlockSpec((tl, DG), bw(CB_V)),
                  pl.BlockSpec((tl, LRW), bw(CB_LR)),
                  pl.BlockSpec((LRW, 2 * DK), lambda i: (0, 0)), pl.BlockSpec((1, 2 * DK), lambda i: (0, 0))],
        out_specs=[pl.BlockSpec((tl, DG), lambda i: (i, 0)), pl.BlockSpec((tl, DG), lambda i: (nb - 1 - i, 0)),
                   pl.BlockSpec((nc, DG, DK), lambda i: (i, 0, 0)), pl.BlockSpec((nc, DG, DK), lambda i: (nb - 1 - i, 0, 0))],
        out_shape=[jax.ShapeDtypeStruct((L, DG), F32), jax.ShapeDtypeStruct((L, DG), F32),
                   jax.ShapeDtypeStruct((L // CH, DG, DK), BF16), jax.ShapeDtypeStruct((L // CH, DG, DK), BF16)],
        scratch_shapes=[pltpu.VMEM((DG, DK), F32), pltpu.VMEM((DG, DK), F32),
                        pltpu.VMEM((tl, DK), F32), pltpu.VMEM((tl, DK), F32)],
        compiler_params=_cp("arbitrary"),
    )(P, P, P, P, P, P, P, P, gcat, gbias)


def _headnorm(o):
    oh, rs = [], []
    for h in range(NH):
        oo = o[:, h * HV:(h + 1) * HV]
        r = lax.rsqrt(jnp.mean(oo * oo, axis=-1, keepdims=True) + EPS)
        oh.append(oo * r)
        rs.append(r)
    return jnp.concatenate(oh, axis=1), rs


def mix_out(P, o_f, o_b, conv_a, ghn4, w_out, g2, x, tl):
    L = P.shape[0]
    nt = L // tl

    def body(gb, gc, gv, go, gcp, gvp, gcn, gvn, of, ob, ca, gh, wo, g2r, xr, ycat, yr, x1):
        i = pl.program_id(0)
        cp, cn = _halo_rows(gcp, gcn, i, nt - 1)
        vp, vn = _halo_rows(gvp, gvn, i, nt - 1)
        c = gc[...] * gv[...]
        cc, _, _ = _conv3(c, cp * vp, cn * vn, ca)
        ya = gb[...] * cc
        oh, _ = _headnorm(of[...] + ob[...])
        g = go[...]
        yb = g * _sigmoid(g) * (oh * gh[...])
        yc = jnp.concatenate([ya, yb], axis=1).astype(BF16)
        ycat[...] = yc
        y = _mm(yc, wo[...])
        yr[...] = y
        x1[...] = xr[...] + _rms(y, g2r[...])

    t = lambda cb: pl.BlockSpec((tl, DC), lambda i: (i, cb))
    hp = lambda cb: pl.BlockSpec((8, DC), _prev_blk(tl, cb))
    hn = lambda cb: pl.BlockSpec((8, DC), _next_blk(tl, L, cb))
    row = lambda n: pl.BlockSpec((tl, n), lambda i: (i, 0))
    full = lambda a: pl.BlockSpec(a.shape, lambda i: (0, 0))
    return pl.pallas_call(
        body, name="mix_out", grid=(nt,),
        in_specs=[t(CB_GB), t(CB_GC), t(CB_GV), t(CB_GO), hp(CB_GC), hp(CB_GV), hn(CB_GC), hn(CB_GV),
                  row(DG), row(DG), full(conv_a), full(ghn4), full(w_out), full(g2), row(D)],
        out_specs=[row(D), row(D), row(D)],
        out_shape=[jax.ShapeDtypeStruct((L, D), BF16), jax.ShapeDtypeStruct((L, D), F32),
                   jax.ShapeDtypeStruct((L, D), F32)],
        compiler_params=_cp("parallel"),
    )(P, P, P, P, P, P, P, P, o_f, o_b, conv_a, ghn4, w_out, g2, x)


NFF = 2
WFF = DFF // NFF


def ffn_down(U, conv_ffn, w_down, g4, x1, tl):
    L = U.shape[0]
    nt = L // tl

    def body(u, up, un, cf, wd, g4r, x1r, y2, x2):
        i = pl.program_id(0)
        acc = jnp.zeros((tl, D), F32)
        for j in range(NFF):
            gs = slice(j * WFF, (j + 1) * WFF)
            vs = slice(DFF + j * WFF, DFF + (j + 1) * WFF)
            z = []
            for s in (gs, vs):
                p = jnp.where(i == 0, 0.0, up[7:8, s])
                n = jnp.where(i == nt - 1, 0.0, un[0:1, s])
                z.append(_conv3(u[:, s], p, n, cf.at[:, s])[0])
            zz = z[0] * _sigmoid(z[0]) * z[1]
            acc = acc + _mm(zz, wd[gs, :])
        y2[...] = acc
        x2[...] = x1r[...] + _rms(acc, g4r[...])

    row = lambda n: pl.BlockSpec((tl, n), lambda i: (i, 0))
    full = lambda a: pl.BlockSpec(a.shape, lambda i: (0, 0))
    return pl.pallas_call(
        body, name="ffn_down", grid=(nt,),
        in_specs=[row(2 * DFF), pl.BlockSpec((8, 2 * DFF), _prev_blk(tl, 0)), pl.BlockSpec((8, 2 * DFF), _next_blk(tl, L, 0)),
                  full(conv_ffn), full(w_down), full(g4), row(D)],
        out_specs=[row(D), row(D)],
        out_shape=[jax.ShapeDtypeStruct((L, D), F32), jax.ShapeDtypeStruct((L, D), F32)],
        compiler_params=_cp("parallel"),
    )(U, U, U, conv_ffn, w_down, g4, x1)


def loss_head(y, target, tl):
    L = y.shape[0]

    def body(yr, tr, dy, ls):
        @pl.when(pl.program_id(0) == 0)
        def _():
            ls[...] = jnp.zeros_like(ls)

        err = yr[...] - tr[...]
        dy[...] = err * (1.0 / D)
        ls[...] += (0.5 / D) * jnp.sum(err * err)

    row = pl.BlockSpec((tl, D), lambda i: (i, 0))
    return pl.pallas_call(
        body, name="loss_head", grid=(L // tl,), in_specs=[row, row],
        out_specs=[row, pl.BlockSpec((8, 128), lambda i: (0, 0))],
        out_shape=[jax.ShapeDtypeStruct((L, D), F32), jax.ShapeDtypeStruct((8, 128), F32)],
        compiler_params=_cp("arbitrary"),
    )(y, target)


def rms_bwd_pre(dout, y, g, tl):
    L = y.shape[0]

    def body(dr, yr, gr, dy, dg):
        @pl.when(pl.program_id(0) == 0)
        def _():
            dg[...] = jnp.zeros_like(dg)

        a, b = _rms_bwd(dr[...], yr[...], gr[...])
        dy[...] = a.astype(BF16)
        dg[...] += b

    row = pl.BlockSpec((tl, D), lambda i: (i, 0))
    vec = pl.BlockSpec((1, D), lambda i: (0, 0))
    return pl.pallas_call(
        body, name="rms_bwd_pre", grid=(L // tl,), in_specs=[row, row, vec], out_specs=[row, vec],
        out_shape=[jax.ShapeDtypeStruct((L, D), BF16), jax.ShapeDtypeStruct((1, D), F32)],
        compiler_params=_cp("arbitrary"),
    )(dout, y, g)


def ffn_bwd1(dy2, U, conv_ffn, w_down, tl):
    L = U.shape[0]
    nt = L // tl

    def body(dy, ug, uv, ugp, ugn, uvp, uvn, cg, cv, wd, dug, duv, zr, dcg, dcv):
        i = pl.program_id(1)

        @pl.when(i == 0)
        def _():
            dcg[...] = jnp.zeros_like(dcg)
            dcv[...] = jnp.zeros_like(dcv)

        gp, gn = _halo_rows(ugp, ugn, i, nt - 1)
        vp, vn = _halo_rows(uvp, uvn, i, nt - 1)
        a, a_m1, a_p1 = _conv3(ug[...], gp, gn, cg)
        b, b_m1, b_p1 = _conv3(uv[...], vp, vn, cv)
        sg = _sigmoid(a)
        silu = a * sg
        zr[...] = (silu * b).astype(BF16)
        dz = _mm_nt(dy[...], wd[...])
        dval = dz * silu
        dgate = dz * b * (sg * (1.0 + a * (1.0 - sg)))
        dug[...] = dgate
        duv[...] = dval
        for k, (sa, sb) in enumerate(((a_m1, b_m1), (ug[...], uv[...]), (a_p1, b_p1))):
            dcg[k:k + 1, :] += jnp.sum(dgate * sa, axis=0, keepdims=True)
            dcv[k:k + 1, :] += jnp.sum(dval * sb, axis=0, keepdims=True)

    tile = lambda off: pl.BlockSpec((tl, WFF), lambda j, i: (i, off + j))
    prev = lambda off: pl.BlockSpec((8, WFF), lambda j, i: (jnp.maximum(i * (tl // 8) - 1, 0), off + j))
    nxt = lambda off: pl.BlockSpec((8, WFF), lambda j, i: (jnp.minimum((i + 1) * (tl // 8), L // 8 - 1), off + j))
    cw = lambda off: pl.BlockSpec((3, WFF), lambda j, i: (0, off + j))
    acc = pl.BlockSpec((8, WFF), lambda j, i: (0, j))
    return pl.pallas_call(
        body, name="ffn_bwd1", grid=(NFF, nt),
        in_specs=[pl.BlockSpec((tl, D), lambda j, i: (i, 0)), tile(0), tile(NFF), prev(0), nxt(0), prev(NFF), nxt(NFF),
                  cw(0), cw(NFF), pl.BlockSpec((WFF, D), lambda j, i: (j, 0))],
        out_specs=[tile(0), tile(0), tile(0), acc, acc],
        out_shape=[jax.ShapeDtypeStruct((L, DFF), F32), jax.ShapeDtypeStruct((L, DFF), F32),
                   jax.ShapeDtypeStruct((L, DFF), BF16), jax.ShapeDtypeStruct((8, DFF), F32),
                   jax.ShapeDtypeStruct((8, DFF), F32)],
        compiler_params=_cp("parallel", "arbitrary"),
    )(dy2, U, U, U, U, U, U, conv_ffn, conv_ffn, w_down)


def ffn_bwd2(du_g, du_v, conv_ffn, w_up, layer, x1, dres, g3, tl):
    L = x1.shape[0]
    nt = L // tl

    def body(dg_, dv_, dgp, dgn, dvp, dvn, cg, cv, wg, wv, x1r, drr, g3r, dUg, dUv, dx1, dg3, acc):
        i = pl.program_id(0)
        j = pl.program_id(1)
        gp, gn = _halo_rows(dgp, dgn, i, nt - 1)
        vp, vn = _halo_rows(dvp, dvn, i, nt - 1)
        a = _conv3_t(dg_[...], gp, gn, cg).astype(BF16)
        b = _conv3_t(dv_[...], vp, vn, cv).astype(BF16)
        dUg[...] = a
        dUv[...] = b
        part = _mm_nt(a, wg[...]) + _mm_nt(b, wv[...])

        @pl.when(j == 0)
        def _():
            acc[...] = part

        @pl.when(j > 0)
        def _():
            acc[...] += part

        @pl.when((i == 0) & (j == 0))
        def _():
            dg3[...] = jnp.zeros_like(dg3)

        @pl.when(j == NFF - 1)
        def _():
            dx, dg = _rms_bwd(acc[...], x1r[...], g3r[...])
            dx1[...] = drr[...] + dx
            dg3[...] += dg

    tile = pl.BlockSpec((tl, WFF), lambda i, j: (i, j))
    prev = pl.BlockSpec((8, WFF), lambda i, j: (jnp.maximum(i * (tl // 8) - 1, 0), j))
    nxt = pl.BlockSpec((8, WFF), lambda i, j: (jnp.minimum((i + 1) * (tl // 8), L // 8 - 1), j))
    cw = lambda off: pl.BlockSpec((3, WFF), lambda i, j: (0, off + j))
    ww = lambda off: pl.BlockSpec((None, None, D, WFF), lambda i, j: (off + j, layer, 0, 0))
    row = pl.BlockSpec((tl, D), lambda i, j: (i, 0))
    vec = pl.BlockSpec((1, D), lambda i, j: (0, 0))
    return pl.pallas_call(
        body, name="ffn_bwd2", grid=(nt, NFF),
        in_specs=[tile, tile, prev, nxt, prev, nxt, cw(0), cw(NFF), ww(0), ww(NFF), row, row, vec],
        out_specs=[tile, tile, row, vec],
        out_shape=[jax.ShapeDtypeStruct((L, DFF), BF16), jax.ShapeDtypeStruct((L, DFF), BF16),
                   jax.ShapeDtypeStruct((L, D), F32), jax.ShapeDtypeStruct((1, D), F32)],
        scratch_shapes=[pltpu.VMEM((tl, D), F32)],
        compiler_params=_cp("arbitrary", "arbitrary"),
    )(du_g, du_v, du_g, du_g, du_v, du_v, conv_ffn, conv_ffn, w_up, w_up, x1, dres, g3)


def matmul_tn(a, b, ta, tn, tl, name, into=None):
    L, Ka = a.shape
    N = b.shape[1]

    def body(ar, br, *rest):
        o = rest[-1]

        @pl.when(pl.program_id(2) == 0)
        def _():
            o[...] = jnp.zeros_like(o)

        o[...] += _mm_tn(ar[...], br[...]).reshape(o.shape)

    in_specs = [pl.BlockSpec((tl, ta), lambda p, q, l: (l, p)), pl.BlockSpec((tl, tn), lambda p, q, l: (l, q))]
    if into is None:
        return pl.pallas_call(
            body, name=name, grid=(Ka // ta, N // tn, L // tl), in_specs=in_specs,
            out_specs=pl.BlockSpec((ta, tn), lambda p, q, l: (p, q)),
            out_shape=jax.ShapeDtypeStruct((Ka, N), F32),
            compiler_params=_cp("parallel", "parallel", "arbitrary"),
        )(a, b)
    buf, blk, idx = into
    return pl.pallas_call(
        body, name=name, grid=(Ka // ta, N // tn, L // tl), in_specs=in_specs + [_ANY],
        out_specs=pl.BlockSpec(blk, lambda p, q, l: idx(p, q)),
        out_shape=jax.ShapeDtypeStruct(buf.shape, F32), input_output_aliases={2: 0},
        compiler_params=_cp("parallel", "parallel", "arbitrary"),
    )(a, b, buf)


def mix_bwd1(dy, w_out, P, o_f, o_b, conv_a, ghn4, tl):
    L = P.shape[0]
    nt = L // tl

    def body(dyr, wo, gb, gc, gv, go, gcp, gvp, gcn, gvn, of, ob, ca, gh, dgb, dcc, dgo, do, dca, dgh):
        i = pl.program_id(0)

        @pl.when(i == 0)
        def _():
            dca[...] = jnp.zeros_like(dca)
            dgh[...] = jnp.zeros_like(dgh)

        dycat = _mm_nt(dyr[...], wo[...])
        dya = dycat[:, 0:DC]
        dyb = dycat[:, DC:D]
        cp, cn = _halo_rows(gcp, gcn, i, nt - 1)
        vp, vn = _halo_rows(gvp, gvn, i, nt - 1)
        c = gc[...] * gv[...]
        cc, c_m1, c_p1 = _conv3(c, cp * vp, cn * vn, ca)
        dgb[...] = dya * cc
        d = dya * gb[...]
        dcc[...] = d
        for k, s in enumerate((c_m1, c, c_p1)):
            dca[k:k + 1, :] += jnp.sum(d * s, axis=0, keepdims=True)
        oh, rs = _headnorm(of[...] + ob[...])
        g = go[...]
        sg = _sigmoid(g)
        silu = g * sg
        dgo[...] = dyb * (oh * gh[...]) * (sg * (1.0 + g * (1.0 - sg)))
        don = dyb * silu
        t = jnp.sum(don * oh, axis=0, keepdims=True)
        dgh[0:1, :] += t[:, 0:HV] + t[:, HV:2 * HV] + t[:, 2 * HV:3 * HV] + t[:, 3 * HV:4 * HV]
        doh = don * gh[...]
        parts = []
        for h in range(NH):
            hs = slice(h * HV, (h + 1) * HV)
            parts.append(rs[h] * (doh[:, hs] - oh[:, hs] * jnp.mean(doh[:, hs] * oh[:, hs], axis=-1, keepdims=True)))
        do[...] = jnp.concatenate(parts, axis=1)

    t = lambda cb: pl.BlockSpec((tl, DC), lambda i: (i, cb))
    hp = lambda cb: pl.BlockSpec((8, DC), _prev_blk(tl, cb))
    hn = lambda cb: pl.BlockSpec((8, DC), _next_blk(tl, L, cb))
    row = lambda n: pl.BlockSpec((tl, n), lambda i: (i, 0))
    full = lambda a: pl.BlockSpec(a.shape, lambda i: (0, 0))
    f32o = lambda n: jax.ShapeDtypeStruct((L, n), F32)
    return pl.pallas_call(
        body, name="mix_bwd1", grid=(nt,),
        in_specs=[row(D), full(w_out), t(CB_GB), t(CB_GC), t(CB_GV), t(CB_GO), hp(CB_GC), hp(CB_GV), hn(CB_GC), hn(CB_GV),
                  row(DG), row(DG), full(conv_a), full(ghn4)],
        out_specs=[row(DC), row(DC), row(DG), row(DG), pl.BlockSpec((8, DC), lambda i: (0, 0)),
                   pl.BlockSpec((8, HV), lambda i: (0, 0))],
        out_shape=[f32o(DC), f32o(DC), f32o(DG), f32o(DG), jax.ShapeDtypeStruct((8, DC), F32),
                   jax.ShapeDtypeStruct((8, HV), F32)],
        compiler_params=_cp("arbitrary"),
    )(dy, w_out, P, P, P, P, P, P, P, P, o_f, o_b, conv_a, ghn4)


def _gla_chunk_bwd(q, k, v, a, do, st16, g_ref, m, rev):
    p = _gla_prep(q, k, a, m, rev)
    g = g_ref[...]
    dob = _tile4(do) * m["bdo"]
    dv = _mm_tn(p["sc"], dob) + _mm_nt(p["kh"], g)
    dsc = jnp.where(m["triu4"] if rev else m["tril4"], _mm_nt(dob, v), 0.0)
    r1 = _mm(dsc, p["kt"]) * m["bdq"]
    dqt = r1[0:CH] + r1[CH:2 * CH] + r1[2 * CH:3 * CH] + r1[3 * CH:4 * CH] + _mm(do, st16)
    dkt = _mm_tn(dsc, p["qs"])
    dkh = _mm(v, g)
    dd = jnp.sum(g * st16.astype(F32), axis=0, keepdims=True)
    g_ref[...] = g * p["dec"] + _mm_tn(do, p["qt"]) * m["bds"]
    kk = dkh * p["kh"]
    dcum = dqt * p["qt"] - dkt * p["kt"] - kk
    dtot = jnp.sum(kk, axis=0, keepdims=True) + dd * p["dec"]
    da = _mm_hi(m["tril"] if rev else m["triu"], dcum) + dtot
    dq = dqt * p["e"] * QSCALE
    dk = dkt * p["einv"] + dkh * p["eout"]
    return dq, dk, dv, da


def gla_bwd(P, do, sf, sb, gcat, gbias, tl):
    L = P.shape[0]
    nb = L // tl
    nc = tl // CH

    def body(qf, kf, vf, lf, dof, sfr, qb, kb, vb, lb, dob, sbr, gc_ref, bs_ref,
             dqf, dkf, dvf, daf, dqb, dkb, dvb, dab, gf, gbk, af, ab):
        @pl.when(pl.program_id(0) == 0)
        def _():
            gf[...] = jnp.zeros_like(gf)
            gbk[...] = jnp.zeros_like(gbk)

        af[...] = _logsig(_mm(lf[...], gc_ref[:, 0:DK]) + bs_ref[:, 0:DK]) * GATE_NORM
        ab[...] = _logsig(_mm(lb[...], gc_ref[:, DK:2 * DK]) + bs_ref[:, DK:2 * DK]) * GATE_NORM
        m = _gla_masks()

        def chunk(c, carry):
            cf = nc - 1 - c
            rows = pl.ds(pl.multiple_of(cf * CH, CH), CH)
            dq, dk, dv, da = _gla_chunk_bwd(qf[rows, :], kf[rows, :], vf[rows, :], af[rows, :], dof[rows, :],
                                            sfr[cf], gf, m, False)
            dqf[rows, :] = dq
            dkf[rows, :] = dk
            dvf[rows, :] = dv
            daf[rows, :] = da
            rows = pl.ds(pl.multiple_of(c * CH, CH), CH)
            dq, dk, dv, da = _gla_chunk_bwd(qb[rows, :], kb[rows, :], vb[rows, :], ab[rows, :], dob[rows, :],
                                            sbr[c], gbk, m, True)
            dqb[rows, :] = dq
            dkb[rows, :] = dk
            dvb[rows, :] = dv
            dab[rows, :] = da
            return carry

        lax.fori_loop(0, nc, chunk, 0)

    fwd_dir = lambda cb: (lambda i: (nb - 1 - i, cb))
    bwd_dir = lambda cb: (lambda i: (i, cb))

    def side(ix):
        return [pl.BlockSpec((tl, DK), ix(CB_Q)), pl.BlockSpec((tl, DK), ix(CB_K)), pl.BlockSpec((tl, DG), ix(CB_V)),
                pl.BlockSpec((tl, LRW), ix(CB_LR)), pl.BlockSpec((tl, DG), ix(0)),
                pl.BlockSpec((nc, DG, DK), lambda i: (ix(0)(i)[0], 0, 0))]

    def outs(ix):
        return [pl.BlockSpec((tl, DK), ix(0)), pl.BlockSpec((tl, DK), ix(0)), pl.BlockSpec((tl, DG), ix(0)),
                pl.BlockSpec((tl, DK), ix(0))]

    o_shape = [jax.ShapeDtypeStruct((L, DK), F32), jax.ShapeDtypeStruct((L, DK), F32),
               jax.ShapeDtypeStruct((L, DG), F32), jax.ShapeDtypeStruct((L, DK), F32)]
    return pl.pallas_call(
        body, name="gla_bwd", grid=(nb,),
        in_specs=side(fwd_dir) + side(bwd_dir) + [pl.BlockSpec((LRW, 2 * DK), lambda i: (0, 0)),
                                                  pl.BlockSpec((1, 2 * DK), lambda i: (0, 0))],
        out_specs=outs(fwd_dir) + outs(bwd_dir),
        out_shape=o_shape + o_shape,
        scratch_shapes=[pltpu.VMEM((DG, DK), F32), pltpu.VMEM((DG, DK), F32),
                        pltpu.VMEM((tl, DK), F32), pltpu.VMEM((tl, DK), F32)],
        compiler_params=_cp("arbitrary"),
    )(P, P, P, P, do, sf, P, P, P, P, do, sb, gcat, gbias)


def mix_bwd2(dgb, dcc, dgo, gl, P, conv_a, gcat, gbias, w_in, x, dres, g1, tl):
    L = P.shape[0]
    nt = L // tl

    def body(dgbr, dccr, dccp, dccn, dgor, dqf, dkf, dvf, daf, dqb, dkb, dvb, dab, gc, gv, lr, ca, gcr, bsr, wi,
             xr, drr, g1r, dP, dx, dg1, dgcat, dbias):
        i = pl.program_id(0)

        @pl.when(i == 0)
        def _():
            dg1[...] = jnp.zeros_like(dg1)
            dgcat[...] = jnp.zeros_like(dgcat)
            dbias[...] = jnp.zeros_like(dbias)

        p, n = _halo_rows(dccp, dccn, i, nt - 1)
        dc = _conv3_t(dccr[...], p, n, ca)
        pre = _mm(lr[...], gcr[...]) + bsr[...]
        da = jnp.concatenate([daf[...], dab[...]], axis=1)
        dpre = da * GATE_NORM * (1.0 - _sigmoid(pre))
        dpre16 = dpre.astype(BF16)
        dP[:, 0:DC] = dgbr[...].astype(BF16)
        dP[:, DC:2 * DC] = (dc * gv[...]).astype(BF16)
        dP[:, 2 * DC:3 * DC] = (dc * gc[...]).astype(BF16)
        dP[:, 1536:1792] = (dqf[...] + dqb[...]).astype(BF16)
        dP[:, 1792:2048] = (dkf[...] + dkb[...]).astype(BF16)
        dP[:, 2048:2560] = (dvf[...] + dvb[...]).astype(BF16)
        dP[:, 2560:3072] = dgor[...].astype(BF16)
        dP[:, 3072:3200] = _mm_nt(dpre16, gcr[...]).astype(BF16)
        dgcat[...] += _mm_tn(lr[...], dpre16)
        dbias[0:1, :] += jnp.sum(dpre, axis=0, keepdims=True)
        dh, dg = _rms_bwd(_mm_nt(dP[...], wi[...]), xr[...], g1r[...])
        dx[...] = drr[...] + dh
        dg1[...] += dg

    row = lambda n: pl.BlockSpec((tl, n), lambda i: (i, 0))
    t = lambda w, cb: pl.BlockSpec((tl, w), lambda i: (i, cb))
    full = lambda a: pl.BlockSpec(a.shape, lambda i: (0, 0))
    return pl.pallas_call(
        body, name="mix_bwd2", grid=(nt,),
        in_specs=[row(DC), row(DC), pl.BlockSpec((8, DC), _prev_blk(tl, 0)), pl.BlockSpec((8, DC), _next_blk(tl, L, 0)),
                  row(DG), row(DK), row(DK), row(DG), row(DK), row(DK), row(DK), row(DG), row(DK),
                  t(DC, CB_GC), t(DC, CB_GV), t(LRW, CB_LR), full(conv_a), full(gcat), full(gbias), full(w_in),
                  row(D), row(D), full(g1)],
        out_specs=[row(DINP), row(D), pl.BlockSpec((1, D), lambda i: (0, 0)), pl.BlockSpec((LRW, 2 * DK), lambda i: (0, 0)),
                   pl.BlockSpec((8, 2 * DK), lambda i: (0, 0))],
        out_shape=[jax.ShapeDtypeStruct((L, DINP), BF16), jax.ShapeDtypeStruct((L, D), F32),
                   jax.ShapeDtypeStruct((1, D), F32), jax.ShapeDtypeStruct((LRW, 2 * DK), F32),
                   jax.ShapeDtypeStruct((8, 2 * DK), F32)],
        compiler_params=_cp("arbitrary"),
    )(dgb, dcc, dcc, dcc, dgo, *gl, P, P, P, conv_a, gcat, gbias, w_in, x, dres, g1)


def _row_tile(rows, cols):
    if rows * cols * 4 <= 2 * 1024 * 1024:
        return rows
    best = 8
    for t in range(8, rows, 8):
        if rows % t == 0 and t * cols * 4 <= 2 * 1024 * 1024:
            best = t
    return best


def adamw(w, g, m, v, name):
    shape = w.shape
    cols = shape[-1]
    w2, g2, m2, v2 = (a.reshape(-1, cols) for a in (w, g, m, v))
    rows = w2.shape[0]
    tr = _row_tile(rows, cols)

    def body(wr, gr, mr, vr, dl, nm, nv):
        gg = gr[...]
        mm = B1 * mr[...] + (1.0 - B1) * gg
        vv = B2 * vr[...] + (1.0 - B2) * (gg * gg)
        m_hat = mm / (1.0 - B1 ** STEP)
        v_hat = vv / (1.0 - B2 ** STEP)
        dl[...] = -LR * (m_hat / (jnp.sqrt(v_hat) + AEPS) + WD * wr[...])
        nm[...] = mm
        nv[...] = vv

    blk = pl.BlockSpec((tr, cols), lambda i: (i, 0))
    o = jax.ShapeDtypeStruct((rows, cols), F32)
    d, nm, nv = pl.pallas_call(
        body, name=name, grid=(rows // tr,), in_specs=[blk] * 4, out_specs=[blk] * 3, out_shape=[o, o, o],
        compiler_params=_cp("parallel"),
    )(w2, g2, m2, v2)
    return d.reshape(shape), nm.reshape(shape), nv.reshape(shape)


def _place():
    return lax.axis_index("x"), lax.axis_index("y"), lax.axis_index("c")


def allgather8(v, name):
    mp, n = v.shape

    def body(x_ref, out_ref, send_sems, recv_sems, local_sem):
        x, y, c = _place()
        me, sibling = (x, y, c), (x, y, 1 - c)
        chips = [(1 - x, y), (x, 1 - y), (1 - x, 1 - y)]

        def rows(px, py, pc):
            return out_ref.at[pl.ds((4 * px + 2 * py + pc) * mp, mp), :]

        def copy(k, block, to, src=None):
            return pltpu.make_async_remote_copy(
                src_ref=rows(*block) if src is None else src, dst_ref=rows(*block),
                send_sem=send_sems.at[k], recv_sem=recv_sems.at[k], device_id=to, device_id_type=MESH)

        mine = pltpu.make_async_copy(x_ref, rows(*me), local_sem)
        mine.start()
        first = [copy(0, me, sibling, src=x_ref)]
        first += [copy(1 + j, me, (*chip, c), src=x_ref) for j, chip in enumerate(chips)]
        for cp in first:
            cp.start()
        passed = [copy(4 + j, (*chip, c), sibling) for j, chip in enumerate(chips)]
        for j, chip in enumerate(chips):
            copy(1 + j, (*chip, c), me).wait_recv()
            passed[j].start()
        copy(0, sibling, me).wait_recv()
        for j, chip in enumerate(chips):
            copy(4 + j, (*chip, 1 - c), me).wait_recv()
        for cp in first + passed:
            cp.wait_send()
        mine.wait()

    return pl.pallas_call(
        body, name=name, out_shape=jax.ShapeDtypeStruct((8 * mp, n), v.dtype),
        in_specs=[pl.BlockSpec(memory_space=pltpu.VMEM)], out_specs=pl.BlockSpec(memory_space=pltpu.VMEM),
        scratch_shapes=[pltpu.SemaphoreType.DMA((7,)), pltpu.SemaphoreType.DMA((7,)), pltpu.SemaphoreType.DMA],
        compiler_params=pltpu.CompilerParams(vmem_limit_bytes=VMEM_LIMIT),
    )(v)


def sum8(v, mp):
    def body(x_ref, o_ref):
        acc = x_ref[0:mp, :]
        for d in range(1, 8):
            acc = acc + x_ref[d * mp:(d + 1) * mp, :]
        o_ref[...] = acc

    return pl.pallas_call(body, name="sum8", out_shape=jax.ShapeDtypeStruct((mp, v.shape[1]), F32),
                          compiler_params=pltpu.CompilerParams(vmem_limit_bytes=VMEM_LIMIT))(v)


_ANY = pl.BlockSpec(memory_space=pl.ANY)


def _row_half(ref, lead, h):
    hr = ref.shape[-2] // 2
    return ref.at[(*lead, pl.ds(h * hr, hr), slice(None))]


def allgather_weights(slots):
    n = len(slots)

    def body(*refs):
        s_refs, o_refs, (send_sems, recv_sems) = refs[:n], refs[n:2 * n], refs[2 * n:]
        x, y, c = _place()
        me = 2 * x + y
        sibling = (x, y, 1 - c)
        chips = [(1 - x, y), (x, 1 - y), (1 - x, 1 - y)]

        def half(ref, slot, h):
            return _row_half(ref, (slot, slice(None)), h)

        def copy(k, src, dst, to):
            return pltpu.make_async_remote_copy(src_ref=src, dst_ref=dst, send_sem=send_sems.at[k],
                                                recv_sem=recv_sems.at[k], device_id=to, device_id_type=MESH)

        first = [copy(6 * a + k, half(s_refs[a], me, c), half(o_refs[a], me, c), (px, py, c))
                 for k, (px, py) in enumerate(chips) for a in range(n)]
        for cp in first:
            cp.start()
        passed = []
        for k, (px, py) in enumerate(chips):
            for a in range(n):
                got = half(o_refs[a], 2 * px + py, c)
                copy(6 * a + k, half(s_refs[a], me, c), got, (px, py, c)).wait_recv()
                cp = copy(6 * a + 3 + k, got, got, sibling)
                cp.start()
                passed.append(cp)
        for k, (px, py) in enumerate(chips):
            for a in range(n):
                got = half(o_refs[a], 2 * px + py, 1 - c)
                copy(6 * a + 3 + k, got, got, sibling).wait_recv()
        for cp in first + passed:
            cp.wait_send()

    return pl.pallas_call(
        body, name="allgather_weights", out_shape=[jax.ShapeDtypeStruct(s.shape, s.dtype) for s in slots],
        in_specs=[_ANY] * n, out_specs=[_ANY] * n, input_output_aliases={a: a for a in range(n)},
        scratch_shapes=[pltpu.SemaphoreType.DMA((6 * n,)), pltpu.SemaphoreType.DMA((6 * n,))],
    )(*slots)


def rs_sibling_halves(gs):
    n = len(gs)

    def body(*refs):
        g_refs, r_refs, (send_sems, recv_sems) = refs[:n], refs[n:2 * n], refs[2 * n:]
        x, y, c = _place()
        cps = [pltpu.make_async_remote_copy(
            src_ref=_row_half(g_refs[a], (slice(None), slice(None)), 1 - c), dst_ref=r_refs[a],
            send_sem=send_sems.at[a], recv_sem=recv_sems.at[a], device_id=(x, y, 1 - c), device_id_type=MESH)
            for a in range(n)]
        for cp in cps:
            cp.start()
        for cp in cps:
            cp.wait()

    return pl.pallas_call(
        body, name="rs_sibling_halves",
        out_shape=[jax.ShapeDtypeStruct((*g.shape[:2], g.shape[2] // 2, g.shape[3]), F32) for g in gs],
        in_specs=[_ANY] * n, out_specs=[_ANY] * n,
        scratch_shapes=[pltpu.SemaphoreType.DMA((n,)), pltpu.SemaphoreType.DMA((n,))],
    )(*gs)


def rs_chipsum16(g, recv1, cidx, name):
    nl, hr, cols = recv1.shape[1:]

    def body(c_ref, g_ref, r_ref, o_ref):
        o_ref[...] = (g_ref[...] + r_ref[...]).astype(BF16)

    blk = (1, 1, hr, cols)
    return pl.pallas_call(
        body, name=name, out_shape=jax.ShapeDtypeStruct(recv1.shape, BF16),
        grid_spec=pltpu.PrefetchScalarGridSpec(
            num_scalar_prefetch=1, grid=(4, nl),
            in_specs=[pl.BlockSpec(blk, lambda j, l, c: (j, l, c[0], 0)), pl.BlockSpec(blk, lambda j, l, c: (j, l, 0, 0))],
            out_specs=pl.BlockSpec(blk, lambda j, l, c: (j, l, 0, 0))),
        compiler_params=_cp("parallel", "parallel"),
    )(cidx, g, recv1)


def rs_exchange_chips(cs):
    n = len(cs)

    def body(*refs):
        s_refs, r_refs, (send_sems, recv_sems) = refs[:n], refs[n:2 * n], refs[2 * n:]
        x, y, c = _place()
        chips = [(1 - x, y), (x, 1 - y), (1 - x, 1 - y)]
        cps = [pltpu.make_async_remote_copy(
            src_ref=s_refs[a].at[2 * px + py], dst_ref=r_refs[a].at[k], send_sem=send_sems.at[3 * a + k],
            recv_sem=recv_sems.at[3 * a + k], device_id=(px, py, c), device_id_type=MESH)
            for k, (px, py) in enumerate(chips) for a in range(n)]
        for cp in cps:
            cp.start()
        for cp in cps:
            cp.wait()

    return pl.pallas_call(
        body, name="rs_exchange_chips", out_shape=[jax.ShapeDtypeStruct((3, *s.shape[1:]), BF16) for s in cs],
        in_specs=[_ANY] * n, out_specs=[_ANY] * n,
        scratch_shapes=[pltpu.SemaphoreType.DMA((3 * n,)), pltpu.SemaphoreType.DMA((3 * n,))],
    )(*cs)


def rs_final_sum(g, recv1, recv2, idx, name):
    nl, hr, cols = recv1.shape[1:]

    def body(i_ref, g_ref, r1_ref, r2_ref, o_ref):
        acc = g_ref[0, 0] + r1_ref[0, 0]
        for k in range(3):
            acc = acc + r2_ref[k, 0].astype(F32)
        o_ref[0] = acc

    blk = (1, 1, hr, cols)
    return pl.pallas_call(
        body, name=name, out_shape=jax.ShapeDtypeStruct((nl, 2 * hr, cols), F32),
        grid_spec=pltpu.PrefetchScalarGridSpec(
            num_scalar_prefetch=1, grid=(nl,),
            in_specs=[pl.BlockSpec(blk, lambda l, ix: (ix[0], l, ix[1], 0)), pl.BlockSpec(blk, lambda l, ix: (ix[0], l, 0, 0)),
                      pl.BlockSpec((3, 1, hr, cols), lambda l, ix: (0, l, 0, 0))],
            out_specs=pl.BlockSpec((1, hr, cols), lambda l, ix: (l, ix[1], 0))),
        compiler_params=_cp("parallel"),
    )(idx, g, recv1, recv2)


def rs_share_halves(fulls):
    n = len(fulls)

    def body(*refs):
        h_refs, o_refs, (send_sems, recv_sems) = refs[:n], refs[n:2 * n], refs[2 * n:]
        x, y, c = _place()
        sibling = (x, y, 1 - c)

        def copy(a, h):
            return pltpu.make_async_remote_copy(
                src_ref=_row_half(h_refs[a], (slice(None),), h), dst_ref=_row_half(o_refs[a], (slice(None),), h),
                send_sem=send_sems.at[a], recv_sem=recv_sems.at[a], device_id=sibling, device_id_type=MESH)

        for a in range(n):
            copy(a, c).start()
        for a in range(n):
            copy(a, c).wait_send()
            copy(a, 1 - c).wait_recv()

    return pl.pallas_call(
        body, name="rs_share_halves", out_shape=[jax.ShapeDtypeStruct(f.shape, F32) for f in fulls],
        in_specs=[_ANY] * n, out_specs=[_ANY] * n, input_output_aliases={a: a for a in range(n)},
        scratch_shapes=[pltpu.SemaphoreType.DMA((n,)), pltpu.SemaphoreType.DMA((n,))],
    )(*fulls)


def _own_slot(shard, chip, dtype):
    return lax.dynamic_update_slice(lax.empty((4, *shard.shape), dtype), shard.astype(dtype)[None], (chip, 0, 0, 0))


def kernel(x, norm_mix_pre, norm_mix_post, norm_ffn_pre, norm_ffn_post, w_in, conv_a, gate_up_fwd, gate_bias_fwd, gate_up_bwd, gate_bias_bwd, gla_head_norm, w_out, w_up, conv_ffn, w_down, loss_target, m_norm_mix_pre, m_norm_mix_post, m_norm_ffn_pre, m_norm_ffn_post, m_w_in, m_conv_a, m_gate_up_fwd, m_gate_bias_fwd, m_gate_up_bwd, m_gate_bias_bwd, m_gla_head_norm, m_w_out, m_w_up, m_conv_ffn, m_w_down, v_norm_mix_pre, v_norm_mix_post, v_norm_ffn_pre, v_norm_ffn_post, v_w_in, v_conv_a, v_gate_up_fwd, v_gate_bias_fwd, v_gate_up_bwd, v_gate_bias_bwd, v_gla_head_norm, v_w_out, v_w_up, v_conv_ffn, v_w_down):
    L = x.shape[1]
    xi, yi, ci = _place()
    chip = 2 * xi + yi
    tl_gla, tl_mix, tl_ffn = min(L, TL_GLA), min(L, TL_MIX), min(L, TL_FFN)

    a_in, a_out, a_up, a_down = allgather_weights([_own_slot(w, chip, BF16) for w in (w_in, w_out, w_up, w_down)])
    W_in = jnp.pad(jnp.concatenate([a_in[j] for j in range(4)], axis=2), ((0, 0), (0, 0), (0, DINP - DIN)))
    W_out = jnp.transpose(a_out, (1, 0, 2, 3)).reshape(DEPTH, D, D)
    W_down = jnp.transpose(a_down, (1, 0, 2, 3)).reshape(DEPTH, DFF, D)

    small = jnp.concatenate([conv_a.reshape(-1), gate_up_fwd.reshape(-1), gate_up_bwd.reshape(-1), conv_ffn.reshape(-1)])
    ms = small.shape[0] // 128
    sg = allgather8(small.reshape(ms, 128), "allgather_small_weights").reshape(4, 2, ms * 128)[:, 0]

    def small_full(off, shape):
        n = shape[0] * shape[1] * shape[2]
        return jnp.concatenate([sg[j, off:off + n].reshape(shape) for j in range(4)], axis=2)

    o1 = DEPTH * 3 * 128
    o2 = o1 + DEPTH * RK * 64
    o3 = o2 + DEPTH * RK * 64
    conv_a_f = small_full(0, (DEPTH, 3, 128))
    gup_f = small_full(o1, (DEPTH, RK, 64))
    gup_b = small_full(o2, (DEPTH, RK, 64))
    conv_ffn_f = small_full(o3, (DEPTH, 3, 1408))

    def gcat_of(l):
        g = jnp.zeros((LRW, 2 * DK), F32)
        g = g.at[0:RK, 0:DK].set(gup_f[l]).at[RK:2 * RK, DK:2 * DK].set(gup_b[l])
        return g.astype(BF16)

    gcats = [gcat_of(l) for l in range(DEPTH)]
    gbiases = [jnp.concatenate([gate_bias_fwd[l], gate_bias_bwd[l]])[None, :] for l in range(DEPTH)]
    ghn4s = [jnp.tile(gla_head_norm[l], NH)[None, :] for l in range(DEPTH)]

    xc = x.reshape(L, D)
    saved = []
    for l in range(DEPTH):
        P, h1 = rms_matmul(xc, norm_mix_pre[l][None, :], W_in[l], 640, "proj_in")
        o_f, o_b, sf, sb = gla_fwd(P, gcats[l], gbiases[l], tl_gla)
        ycat, y, x1 = mix_out(P, o_f, o_b, conv_a_f[l], ghn4s[l], W_out[l], norm_mix_post[l][None, :], xc, tl_mix)
        U, h2 = rms_matmul(x1, norm_ffn_pre[l][None, :], a_up, WFF, "proj_up", n_out=2 * DFF,
                           w_spec=pl.BlockSpec((None, None, D, WFF), lambda i, j, l=l: (j, l, 0, 0)))
        y2, x2 = ffn_down(U, conv_ffn_f[l], W_down[l], norm_ffn_post[l][None, :], x1, tl_ffn)
        saved.append(dict(x=xc, h1=h1, P=P, o_f=o_f, o_b=o_b, sf=sf, sb=sb, ycat=ycat, y=y, x1=x1, h2=h2, U=U, y2=y2))
        xc = x2

    dx, loss_blk = loss_head(xc, loss_target.reshape(L, D), tl_mix)

    g_out = lax.empty((4, DEPTH, D // 4, D), F32)
    g_up = lax.empty((4, DEPTH, D, WFF), F32)
    g_down = lax.empty((4, DEPTH, DFF // 4, D), F32)
    g_in_layers = [None] * DEPTH
    grads = [None] * DEPTH
    tl_dw = min(L, 1024)
    for l in reversed(range(DEPTH)):
        s = saved[l]
        dy2, dg4 = rms_bwd_pre(dx, s["y2"], norm_ffn_post[l][None, :], tl_mix)
        du_g, du_v, z, dcf_g, dcf_v = ffn_bwd1(dy2, s["U"], conv_ffn_f[l], W_down[l], tl_ffn)
        g_down = matmul_tn(z, dy2, DFF // 2, D, tl_dw, "dw_down",
                           into=(g_down, (2, None, DFF // 4, D), lambda p, q, l=l: (p, l, 0, 0)))
        dU_g, dU_v, dx1, dg3 = ffn_bwd2(du_g, du_v, conv_ffn_f[l], a_up, l, s["x1"], dx, norm_ffn_pre[l][None, :], tl_ffn)
        g_up = matmul_tn(s["h2"], dU_g, D, WFF, tl_dw, "dw_up_gate",
                         into=(g_up, (None, None, D, WFF), lambda p, q, l=l: (q, l, 0, 0)))
        g_up = matmul_tn(s["h2"], dU_v, D, WFF, tl_dw, "dw_up_val",
                         into=(g_up, (None, None, D, WFF), lambda p, q, l=l: (NFF + q, l, 0, 0)))
        dy, dg2 = rms_bwd_pre(dx1, s["y"], norm_mix_post[l][None, :], tl_mix)
        dgb, dcc, dgo, do, dca, dghn = mix_bwd1(dy, W_out[l], s["P"], s["o_f"], s["o_b"], conv_a_f[l], ghn4s[l], tl_mix)
        g_out = matmul_tn(s["ycat"], dy, D, D, tl_dw, "dw_out",
                          into=(g_out, (4, None, D // 4, D), lambda p, q, l=l: (0, l, 0, 0)))
        gl = gla_bwd(s["P"], do, s["sf"], s["sb"], gcats[l], gbiases[l], tl_gla)
        dP, dx, dg1, dgcat, dbias = mix_bwd2(dgb, dcc, dgo, gl, s["P"], conv_a_f[l], gcats[l], gbiases[l], W_in[l],
                                             s["x"], dx1, norm_mix_pre[l][None, :], tl_mix)
        dW_in = matmul_tn(s["h1"], dP, D, 640, tl_dw, "dw_in")
        g_in_layers[l] = jnp.stack([dW_in[:, (DIN // 4) * j:(DIN // 4) * (j + 1)] for j in range(4)])
        grads[l] = dict(
            norm_mix_pre=dg1[0], norm_mix_post=dg2[0], norm_ffn_pre=dg3[0], norm_ffn_post=dg4[0],
            conv_a=dca[0:3], gate_up_fwd=dgcat[0:RK, 0:DK], gate_bias_fwd=dbias[0, 0:DK],
            gate_up_bwd=dgcat[RK:2 * RK, DK:2 * DK], gate_bias_bwd=dbias[0, DK:2 * DK], gla_head_norm=dghn[0],
            conv_ffn=jnp.concatenate([dcf_g[0:3], dcf_v[0:3]], axis=1))

    g_in = jnp.stack(g_in_layers, axis=1)
    G = {k: jnp.stack([grads[l][k] for l in range(DEPTH)]) for k in grads[0]}

    small_names = ["norm_mix_pre", "norm_mix_post", "norm_ffn_pre", "norm_ffn_post", "conv_a", "gate_up_fwd",
                   "gate_bias_fwd", "gate_up_bwd", "gate_bias_bwd", "gla_head_norm", "conv_ffn"]
    flat = jnp.concatenate([G[k].reshape(-1) for k in small_names] + [loss_blk[0, 0:1]])
    n_small = flat.shape[0]
    mp = -(-n_small // 1024) * 8
    flat = jnp.pad(flat, (0, mp * 128 - n_small)).reshape(mp, 128)
    tot = sum8(allgather8(flat, "allgather_small_grads"), mp).reshape(-1)
    gsm = {}
    o = 0
    for k in small_names:
        n = G[k].size
        gsm[k] = tot[o:o + n].reshape(G[k].shape)
        o += n
    loss = tot[o]

    def my_cols(a, width):
        return lax.dynamic_slice_in_dim(a, chip * width, width, axis=2)

    gsm["conv_a"] = my_cols(gsm["conv_a"], 128)
    gsm["gate_up_fwd"] = my_cols(gsm["gate_up_fwd"], 64)
    gsm["gate_up_bwd"] = my_cols(gsm["gate_up_bwd"], 64)
    gsm["conv_ffn"] = my_cols(gsm["conv_ffn"], 1408)

    big = ("w_in", "w_out", "w_up", "w_down")
    gs = [g_in, g_out, g_up, g_down]
    recv1 = rs_sibling_halves(gs)
    cidx = jnp.reshape(ci, (1,)).astype(jnp.int32)
    cs16 = [rs_chipsum16(g, r, cidx, "rs_chipsum16_" + k) for g, r, k in zip(gs, recv1, big)]
    recv2 = rs_exchange_chips(cs16)
    idx = jnp.stack([chip, ci]).astype(jnp.int32)
    halves = [rs_final_sum(g, r1, r2, idx, "rs_final_sum_" + k) for g, r1, r2, k in zip(gs, recv1, recv2, big)]
    gsm.update(zip(big, rs_share_halves(halves)))

    names = ["norm_mix_pre", "norm_mix_post", "norm_ffn_pre", "norm_ffn_post", "w_in", "conv_a", "gate_up_fwd",
             "gate_bias_fwd", "gate_up_bwd", "gate_bias_bwd", "gla_head_norm", "w_out", "w_up", "conv_ffn", "w_down"]
    w = dict(norm_mix_pre=norm_mix_pre, norm_mix_post=norm_mix_post, norm_ffn_pre=norm_ffn_pre, norm_ffn_post=norm_ffn_post,
             w_in=w_in, conv_a=conv_a, gate_up_fwd=gate_up_fwd, gate_bias_fwd=gate_bias_fwd, gate_up_bwd=gate_up_bwd,
             gate_bias_bwd=gate_bias_bwd, gla_head_norm=gla_head_norm, w_out=w_out, w_up=w_up, conv_ffn=conv_ffn, w_down=w_down)
    m = dict(norm_mix_pre=m_norm_mix_pre, norm_mix_post=m_norm_mix_post, norm_ffn_pre=m_norm_ffn_pre, norm_ffn_post=m_norm_ffn_post,
             w_in=m_w_in, conv_a=m_conv_a, gate_up_fwd=m_gate_up_fwd, gate_bias_fwd=m_gate_bias_fwd, gate_up_bwd=m_gate_up_bwd,
             gate_bias_bwd=m_gate_bias_bwd, gla_head_norm=m_gla_head_norm, w_out=m_w_out, w_up=m_w_up, conv_ffn=m_conv_ffn, w_down=m_w_down)
    v = dict(norm_mix_pre=v_norm_mix_pre, norm_mix_post=v_norm_mix_post, norm_ffn_pre=v_norm_ffn_pre, norm_ffn_post=v_norm_ffn_post,
             w_in=v_w_in, conv_a=v_conv_a, gate_up_fwd=v_gate_up_fwd, gate_bias_fwd=v_gate_bias_fwd, gate_up_bwd=v_gate_up_bwd,
             gate_bias_bwd=v_gate_bias_bwd, gla_head_norm=v_gla_head_norm, w_out=v_w_out, w_up=v_w_up, conv_ffn=v_conv_ffn, w_down=v_w_down)
    upd = {k: adamw(w[k], gsm[k], m[k], v[k], "adamw_" + k) for k in names}
    return (loss, dx.reshape(1, L, D), *[gsm[k] for k in names], *[upd[k][0] for k in names],
            *[upd[k][1] for k in names], *[upd[k][2] for k in names])
```

```python
import functools

import jax
import jax.numpy as jnp
from jax import lax
from jax.experimental import pallas as pl
from jax.experimental.pallas import tpu as pltpu

F32 = jnp.float32
BF16 = jnp.bfloat16
MXU_DTYPE = jnp.bfloat16
MESH = pl.DeviceIdType.MESH

D = 1024
DC = 512
DG = 512
NH = 4
HV = 128
HK = 64
DK = 256
RK = 16
CH = 64
DFF = 2816
DIN = 3104
DINP = 3200
LRW = 128
DEPTH = 4
EPS = 1e-6
QSCALE = HK ** -0.5
GATE_NORM = 1.0 / 16.0
CB_GB, CB_GC, CB_GV, CB_GO = 0, 1, 2, 5
CB_Q, CB_K = 6, 7
CB_V = 4
CB_LR = 24
LR = 0.001
B1 = 0.9
B2 = 0.999
AEPS = 1e-08
WD = 0.01
STEP = 10
TM_PROJ = 1024
TL_GLA = 512
TL_MIX = 256
TL_FFN = 256
TL_FFN2 = 512
VMEM_LIMIT = 56 * 1024 * 1024


def _cp(*sem):
    return pltpu.CompilerParams(dimension_semantics=sem if sem else None, vmem_limit_bytes=VMEM_LIMIT)


def _mm(a, b):
    return jnp.dot(a.astype(MXU_DTYPE), b.astype(MXU_DTYPE), preferred_element_type=F32)


def _mm_nt(a, b):
    return lax.dot_general(a.astype(MXU_DTYPE), b.astype(MXU_DTYPE), (((1,), (1,)), ((), ())),
                           preferred_element_type=F32)


def _mm_tn(a, b):
    return lax.dot_general(a.astype(MXU_DTYPE), b.astype(MXU_DTYPE), (((0,), (0,)), ((), ())),
                           preferred_element_type=F32)


def _mm_tri(tri, b):
    t = tri.astype(BF16)
    b1 = b.astype(BF16)
    r1 = b - b1.astype(F32)
    b2 = r1.astype(BF16)
    b3 = (r1 - b2.astype(F32)).astype(BF16)
    dot = lambda u: jnp.dot(t, u, preferred_element_type=F32)
    return dot(b1) + dot(b2) + dot(b3)


def _rms(x, g):
    r = lax.rsqrt(jnp.mean(x * x, axis=-1, keepdims=True) + EPS)
    return x * r * g


def _rms_bwd(dout, y, g):
    r = lax.rsqrt(jnp.mean(y * y, axis=-1, keepdims=True) + EPS)
    yh = y * r
    dyh = dout * g
    dy = r * (dyh - yh * jnp.mean(dyh * yh, axis=-1, keepdims=True))
    dg = jnp.sum(dout * yh, axis=0, keepdims=True)
    return dy, dg


def _sigmoid(x):
    return 0.5 * jnp.tanh(0.5 * x) + 0.5


def _logsig(x):
    return jnp.minimum(x, 0.0) - jnp.log1p(jnp.exp(-jnp.abs(x)))


def _shift_down(x, first_row):
    row = lax.broadcasted_iota(jnp.int32, x.shape, 0)
    return jnp.where(row == 0, first_row, pltpu.roll(x, 1, 0))


def _shift_up(x, last_row):
    n = x.shape[0]
    row = lax.broadcasted_iota(jnp.int32, x.shape, 0)
    return jnp.where(row == n - 1, last_row, pltpu.roll(x, n - 1, 0))


def _halo_rows(prev_ref, next_ref, i, last):
    hr = prev_ref.shape[0]
    p = jnp.where(i == 0, 0.0, prev_ref[...].astype(F32)[hr - 1:hr, :])
    n = jnp.where(i == last, 0.0, next_ref[...].astype(F32)[0:1, :])
    return p, n


def _conv3(x, xp, xn, w_ref):
    xm1 = _shift_down(x, xp)
    xp1 = _shift_up(x, xn)
    return w_ref[0:1, :] * xm1 + w_ref[1:2, :] * x + w_ref[2:3, :] * xp1, xm1, xp1


def _conv3_t(d, dp, dn, w_ref):
    return w_ref[0:1, :] * _shift_up(d, dn) + w_ref[1:2, :] * d + w_ref[2:3, :] * _shift_down(d, dp)


HALO32 = 8
HALO16 = 16


def _prev_row_blk(i, tl, hr):
    return jnp.maximum(i * (tl // hr) - 1, 0)


def _next_row_blk(i, tl, nrows, hr):
    return jnp.minimum((i + 1) * (tl // hr), nrows // hr - 1)


def _prev_blk(tl, cb, hr=HALO32):
    return lambda i: (_prev_row_blk(i, tl, hr), cb)


def _next_blk(tl, nrows, cb, hr=HALO32):
    return lambda i: (_next_row_blk(i, tl, nrows, hr), cb)


def rms_matmul(x, g, w, tn, name, w_spec=None, n_out=None, out_dtype=F32):
    L = x.shape[0]
    N = w.shape[1] if n_out is None else n_out
    tm = min(L, TM_PROJ)
    if w_spec is None:
        w_spec = pl.BlockSpec((D, tn), lambda i, j: (0, j))

    def body(x_ref, g_ref, w_ref, o_ref, h_ref):
        @pl.when(pl.program_id(1) == 0)
        def _():
            h_ref[...] = _rms(x_ref[...], g_ref[...]).astype(BF16)

        o_ref[...] = _mm(h_ref[...], w_ref[...]).astype(out_dtype)

    return pl.pallas_call(
        body, name=name, grid=(L // tm, N // tn),
        in_specs=[pl.BlockSpec((tm, D), lambda i, j: (i, 0)), pl.BlockSpec((1, D), lambda i, j: (0, 0)), w_spec],
        out_specs=[pl.BlockSpec((tm, tn), lambda i, j: (i, j)), pl.BlockSpec((tm, D), lambda i, j: (i, 0))],
        out_shape=[jax.ShapeDtypeStruct((L, N), out_dtype), jax.ShapeDtypeStruct((L, D), BF16)],
        compiler_params=_cp("parallel", "arbitrary"),
    )(x, g, w)


def _gla_masks():
    def blk(shape, rdiv, cdiv):
        r = lax.broadcasted_iota(jnp.int32, shape, 0) // rdiv
        c = lax.broadcasted_iota(jnp.int32, shape, 1) // cdiv
        return (r == c).astype(F32)

    r = lax.broadcasted_iota(jnp.int32, (CH, CH), 0)
    c = lax.broadcasted_iota(jnp.int32, (CH, CH), 1)
    r4 = lax.broadcasted_iota(jnp.int32, (NH * CH, CH), 0) % CH
    c4 = lax.broadcasted_iota(jnp.int32, (NH * CH, CH), 1)
    return dict(
        bdq=blk((NH * CH, DK), CH, HK),
        bdo=blk((NH * CH, DG), CH, HV),
        bds=blk((DG, DK), HV, HK),
        tril=(r >= c).astype(F32), triu=(r <= c).astype(F32),
        tril4=r4 >= c4, triu4=r4 <= c4,
    )


def _tile4(x):
    return jnp.concatenate([x, x, x, x], axis=0)


def _gla_prep(q, k, a, m, rev):
    cum = _mm_tri(m["triu"] if rev else m["tril"], a)
    tot = jnp.sum(a, axis=0, keepdims=True)
    e = jnp.exp(cum)
    einv = jnp.exp(-cum)
    eout = jnp.exp(tot - cum)
    qt = q * QSCALE * e
    kt = k * einv
    kh = k * eout
    qs = _tile4(qt) * m["bdq"]
    sc = _mm_nt(qs, kt)
    sc = jnp.where(m["triu4"] if rev else m["tril4"], sc, 0.0)
    return dict(e=e, einv=einv, eout=eout, dec=jnp.exp(tot), qt=qt, kt=kt, kh=kh, qs=qs, sc=sc)


def _gla_chunk_fwd(q, k, v, a, st_ref, m, rev):
    p = _gla_prep(q, k, a, m, rev)
    r = _mm(p["sc"], v)
    o_intra = jnp.concatenate([r[h * CH:(h + 1) * CH, h * HV:(h + 1) * HV] for h in range(NH)], axis=1)
    st = st_ref[...]
    st16 = st.astype(BF16)
    o = o_intra + _mm_nt(p["qt"], st16)
    st_ref[...] = st * p["dec"] + _mm_tn(v, p["kh"]) * m["bds"]
    return o, st16


def gla_fwd(P, gcat, gbias, tl):
    L = P.shape[0]
    nb = L // tl
    nc = tl // CH

    def body(qf, kf, vf, lf, qb, kb, vb, lb, gc_ref, bs_ref, of, ob, sf, sb, stf, stb, af, ab):
        @pl.when(pl.program_id(0) == 0)
        def _():
            stf[...] = jnp.zeros_like(stf)
            stb[...] = jnp.zeros_like(stb)

        af[...] = _logsig(_mm(lf[...], gc_ref[:, 0:DK]) + bs_ref[:, 0:DK]) * GATE_NORM
        ab[...] = _logsig(_mm(lb[...], gc_ref[:, DK:2 * DK]) + bs_ref[:, DK:2 * DK]) * GATE_NORM
        m = _gla_masks()

        def chunk(c, carry):
            rows = pl.ds(pl.multiple_of(c * CH, CH), CH)
            o, st = _gla_chunk_fwd(qf[rows, :], kf[rows, :], vf[rows, :], af[rows, :], stf, m, False)
            of[rows, :] = o
            sf[c] = st
            cb = nc - 1 - c
            rows = pl.ds(pl.multiple_of(cb * CH, CH), CH)
            o, st = _gla_chunk_fwd(qb[rows, :], kb[rows, :], vb[rows, :], ab[rows, :], stb, m, True)
            ob[rows, :] = o
            sb[cb] = st
            return carry

        lax.fori_loop(0, nc, chunk, 0, unroll=2)

    fw = lambda cb: (lambda i: (i, cb))
    bw = lambda cb: (lambda i: (nb - 1 - i, cb))
    return pl.pallas_call(
        body, name="gla_fwd", grid=(nb,),
        in_specs=[pl.BlockSpec((tl, DK), fw(CB_Q)), pl.BlockSpec((tl, DK), fw(CB_K)), pl.BlockSpec((tl, DG), fw(CB_V)),
                  pl.BlockSpec((tl, LRW), fw(CB_LR)),
                  pl.BlockSpec((tl, DK), bw(CB_Q)), pl.BlockSpec((tl, DK), bw(CB_K)), pl.BlockSpec((tl, DG), bw(CB_V)),
                  pl.BlockSpec((tl, LRW), bw(CB_LR)),
                  pl.BlockSpec((LRW, 2 * DK), lambda i: (0, 0)), pl.BlockSpec((1, 2 * DK), lambda i: (0, 0))],
        out_specs=[pl.BlockSpec((tl, DG), lambda i: (i, 0)), pl.BlockSpec((tl, DG), lambda i: (nb - 1 - i, 0)),
                   pl.BlockSpec((nc, DG, DK), lambda i: (i, 0, 0)), pl.BlockSpec((nc, DG, DK), lambda i: (nb - 1 - i, 0, 0))],
        out_shape=[jax.ShapeDtypeStruct((L, DG), F32), jax.ShapeDtypeStruct((L, DG), F32),
                   jax.ShapeDtypeStruct((L // CH, DG, DK), BF16), jax.ShapeDtypeStruct((L // CH, DG, DK), BF16)],
        scratch_shapes=[pltpu.VMEM((DG, DK), F32), pltpu.VMEM((DG, DK), F32),
                        pltpu.VMEM((tl, DK), F32), pltpu.VMEM((tl, DK), F32)],
        compiler_params=_cp("arbitrary"),
    )(P, P, P, P, P, P, P, P, gcat, gbias)


def _headnorm(o):
    oh, rs = [], []
    for h in range(NH):
        oo = o[:, h * HV:(h + 1) * HV]
        r = lax.rsqrt(jnp.mean(oo * oo, axis=-1, keepdims=True) + EPS)
        oh.append(oo * r)
        rs.append(r)
    return jnp.concatenate(oh, axis=1), rs


def mix_out(P, o_f, o_b, conv_a, ghn4, w_out, g2, x, tl):
    L = P.shape[0]
    nt = L // tl

    def body(gb, gc, gv, go, gcp, gvp, gcn, gvn, of, ob, ca, gh, wo, g2r, xr, ycat, yr, x1):
        i = pl.program_id(0)
        cp, cn = _halo_rows(gcp, gcn, i, nt - 1)
        vp, vn = _halo_rows(gvp, gvn, i, nt - 1)
        c = gc[...] * gv[...]
        cc, _, _ = _conv3(c, cp * vp, cn * vn, ca)
        ya = gb[...] * cc
        oh, _ = _headnorm(of[...] + ob[...])
        g = go[...]
        yb = g * _sigmoid(g) * (oh * gh[...])
        yc = jnp.concatenate([ya, yb], axis=1).astype(BF16)
        ycat[...] = yc
        y = _mm(yc, wo[...])
        yr[...] = y
        x1[...] = xr[...] + _rms(y, g2r[...])

    t = lambda cb: pl.BlockSpec((tl, DC), lambda i: (i, cb))
    hp = lambda cb: pl.BlockSpec((8, DC), _prev_blk(tl, cb))
    hn = lambda cb: pl.BlockSpec((8, DC), _next_blk(tl, L, cb))
    row = lambda n: pl.BlockSpec((tl, n), lambda i: (i, 0))
    full = lambda a: pl.BlockSpec(a.shape, lambda i: (0, 0))
    return pl.pallas_call(
        body, name="mix_out", grid=(nt,),
        in_specs=[t(CB_GB), t(CB_GC), t(CB_GV), t(CB_GO), hp(CB_GC), hp(CB_GV), hn(CB_GC), hn(CB_GV),
                  row(DG), row(DG), full(conv_a), full(ghn4), full(w_out), full(g2), row(D)],
        out_specs=[row(D), row(D), row(D)],
        out_shape=[jax.ShapeDtypeStruct((L, D), BF16), jax.ShapeDtypeStruct((L, D), F32),
                   jax.ShapeDtypeStruct((L, D), F32)],
        compiler_params=_cp("parallel"),
    )(P, P, P, P, P, P, P, P, o_f, o_b, conv_a, ghn4, w_out, g2, x)


NFF = 2
WFF = DFF // NFF


def ffn_down(U, conv_ffn, w_down, g4, x1, tl):
    L = U.shape[0]
    nt = L // tl

    def body(u, up, un, cf, wd, g4r, x1r, y2, x2):
        i = pl.program_id(0)
        acc = jnp.zeros((tl, D), F32)
        for j in range(NFF):
            gs = slice(j * WFF, (j + 1) * WFF)
            vs = slice(DFF + j * WFF, DFF + (j + 1) * WFF)
            z = []
            for s in (gs, vs):
                p, n = _halo_rows(up.at[:, s], un.at[:, s], i, nt - 1)
                z.append(_conv3(u[:, s].astype(F32), p, n, cf.at[:, s])[0])
            zz = z[0] * _sigmoid(z[0]) * z[1]
            acc = acc + _mm(zz, wd[gs, :])
        y2[...] = acc
        x2[...] = x1r[...] + _rms(acc, g4r[...])

    row = lambda n: pl.BlockSpec((tl, n), lambda i: (i, 0))
    full = lambda a: pl.BlockSpec(a.shape, lambda i: (0, 0))
    return pl.pallas_call(
        body, name="ffn_down", grid=(nt,),
        in_specs=[row(2 * DFF), pl.BlockSpec((HALO16, 2 * DFF), _prev_blk(tl, 0, HALO16)),
                  pl.BlockSpec((HALO16, 2 * DFF), _next_blk(tl, L, 0, HALO16)),
                  full(conv_ffn), full(w_down), full(g4), row(D)],
        out_specs=[row(D), row(D)],
        out_shape=[jax.ShapeDtypeStruct((L, D), F32), jax.ShapeDtypeStruct((L, D), F32)],
        compiler_params=_cp("parallel"),
    )(U, U, U, conv_ffn, w_down, g4, x1)


def loss_head(y, target, tl):
    L = y.shape[0]

    def body(yr, tr, dy, ls):
        @pl.when(pl.program_id(0) == 0)
        def _():
            ls[...] = jnp.zeros_like(ls)

        err = yr[...] - tr[...]
        dy[...] = err * (1.0 / D)
        ls[...] += (0.5 / D) * jnp.sum(err * err)

    row = pl.BlockSpec((tl, D), lambda i: (i, 0))
    return pl.pallas_call(
        body, name="loss_head", grid=(L // tl,), in_specs=[row, row],
        out_specs=[row, pl.BlockSpec((8, 128), lambda i: (0, 0))],
        out_shape=[jax.ShapeDtypeStruct((L, D), F32), jax.ShapeDtypeStruct((8, 128), F32)],
        compiler_params=_cp("arbitrary"),
    )(y, target)


def rms_bwd_pre(dout, y, g, tl):
    L = y.shape[0]

    def body(dr, yr, gr, dy, dg):
        @pl.when(pl.program_id(0) == 0)
        def _():
            dg[...] = jnp.zeros_like(dg)

        a, b = _rms_bwd(dr[...], yr[...], gr[...])
        dy[...] = a.astype(BF16)
        dg[...] += b

    row = pl.BlockSpec((tl, D), lambda i: (i, 0))
    vec = pl.BlockSpec((1, D), lambda i: (0, 0))
    return pl.pallas_call(
        body, name="rms_bwd_pre", grid=(L // tl,), in_specs=[row, row, vec], out_specs=[row, vec],
        out_shape=[jax.ShapeDtypeStruct((L, D), BF16), jax.ShapeDtypeStruct((1, D), F32)],
        compiler_params=_cp("arbitrary"),
    )(dout, y, g)


def ffn_bwd1(dy2, U, conv_ffn, w_down, tl):
    L = U.shape[0]
    nt = L // tl

    def body(dy, ug, uv, ugp, ugn, uvp, uvn, cg, cv, wd, dug, duv, zr, dcg, dcv):
        i = pl.program_id(1)

        @pl.when(i == 0)
        def _():
            dcg[...] = jnp.zeros_like(dcg)
            dcv[...] = jnp.zeros_like(dcv)

        gp, gn = _halo_rows(ugp, ugn, i, nt - 1)
        vp, vn = _halo_rows(uvp, uvn, i, nt - 1)
        ug32 = ug[...].astype(F32)
        uv32 = uv[...].astype(F32)
        a, a_m1, a_p1 = _conv3(ug32, gp, gn, cg)
        b, b_m1, b_p1 = _conv3(uv32, vp, vn, cv)
        sg = _sigmoid(a)
        silu = a * sg
        zr[...] = (silu * b).astype(BF16)
        dz = _mm_nt(dy[...], wd[...])
        dval = dz * silu
        dgate = dz * b * (sg * (1.0 + a * (1.0 - sg)))
        dug[...] = dgate.astype(BF16)
        duv[...] = dval.astype(BF16)
        for k, (sa, sb) in enumerate(((a_m1, b_m1), (ug32, uv32), (a_p1, b_p1))):
            dcg[k:k + 1, :] += jnp.sum(dgate * sa, axis=0, keepdims=True)
            dcv[k:k + 1, :] += jnp.sum(dval * sb, axis=0, keepdims=True)

    tile = lambda off: pl.BlockSpec((tl, WFF), lambda j, i: (i, off + j))
    prev = lambda off: pl.BlockSpec((HALO16, WFF), lambda j, i: (_prev_row_blk(i, tl, HALO16), off + j))
    nxt = lambda off: pl.BlockSpec((HALO16, WFF), lambda j, i: (_next_row_blk(i, tl, L, HALO16), off + j))
    cw = lambda off: pl.BlockSpec((3, WFF), lambda j, i: (0, off + j))
    acc = pl.BlockSpec((8, WFF), lambda j, i: (0, j))
    return pl.pallas_call(
        body, name="ffn_bwd1", grid=(NFF, nt),
        in_specs=[pl.BlockSpec((tl, D), lambda j, i: (i, 0)), tile(0), tile(NFF), prev(0), nxt(0), prev(NFF), nxt(NFF),
                  cw(0), cw(NFF), pl.BlockSpec((WFF, D), lambda j, i: (j, 0))],
        out_specs=[tile(0), tile(0), tile(0), acc, acc],
        out_shape=[jax.ShapeDtypeStruct((L, DFF), BF16), jax.ShapeDtypeStruct((L, DFF), BF16),
                   jax.ShapeDtypeStruct((L, DFF), BF16), jax.ShapeDtypeStruct((8, DFF), F32),
                   jax.ShapeDtypeStruct((8, DFF), F32)],
        compiler_params=_cp("parallel", "arbitrary"),
    )(dy2, U, U, U, U, U, U, conv_ffn, conv_ffn, w_down)


def ffn_bwd2(du_g, du_v, conv_ffn, w_up, layer, x1, dres, g3, tl):
    L = x1.shape[0]
    nt = L // tl

    def body(dg_, dv_, dgp, dgn, dvp, dvn, cg, cv, wg, wv, x1r, drr, g3r, dUg, dUv, dx1, dg3, acc):
        i = pl.program_id(0)
        j = pl.program_id(1)
        gp, gn = _halo_rows(dgp, dgn, i, nt - 1)
        vp, vn = _halo_rows(dvp, dvn, i, nt - 1)
        a = _conv3_t(dg_[...].astype(F32), gp, gn, cg).astype(BF16)
        b = _conv3_t(dv_[...].astype(F32), vp, vn, cv).astype(BF16)
        dUg[...] = a
        dUv[...] = b
        part = _mm_nt(a, wg[...]) + _mm_nt(b, wv[...])

        @pl.when(j == 0)
        def _():
            acc[...] = part

        @pl.when(j > 0)
        def _():
            acc[...] += part

        @pl.when((i == 0) & (j == 0))
        def _():
            dg3[...] = jnp.zeros_like(dg3)

        @pl.when(j == NFF - 1)
        def _():
            dx, dg = _rms_bwd(acc[...], x1r[...], g3r[...])
            dx1[...] = drr[...] + dx
            dg3[...] += dg

    tile = pl.BlockSpec((tl, WFF), lambda i, j: (i, j))
    prev = pl.BlockSpec((HALO16, WFF), lambda i, j: (_prev_row_blk(i, tl, HALO16), j))
    nxt = pl.BlockSpec((HALO16, WFF), lambda i, j: (_next_row_blk(i, tl, L, HALO16), j))
    cw = lambda off: pl.BlockSpec((3, WFF), lambda i, j: (0, off + j))
    ww = lambda off: pl.BlockSpec((None, None, D, WFF), lambda i, j: (off + j, layer, 0, 0))
    row = pl.BlockSpec((tl, D), lambda i, j: (i, 0))
    vec = pl.BlockSpec((1, D), lambda i, j: (0, 0))
    return pl.pallas_call(
        body, name="ffn_bwd2", grid=(nt, NFF),
        in_specs=[tile, tile, prev, nxt, prev, nxt, cw(0), cw(NFF), ww(0), ww(NFF), row, row, vec],
        out_specs=[tile, tile, row, vec],
        out_shape=[jax.ShapeDtypeStruct((L, DFF), BF16), jax.ShapeDtypeStruct((L, DFF), BF16),
                   jax.ShapeDtypeStruct((L, D), F32), jax.ShapeDtypeStruct((1, D), F32)],
        scratch_shapes=[pltpu.VMEM((tl, D), F32)],
        compiler_params=_cp("arbitrary", "arbitrary"),
    )(du_g, du_v, du_g, du_g, du_v, du_v, conv_ffn, conv_ffn, w_up, w_up, x1, dres, g3)


def matmul_tn(a, b, ta, tn, tl, name, into=None):
    L, Ka = a.shape
    N = b.shape[1]

    def body(ar, br, *rest):
        o = rest[-1]

        @pl.when(pl.program_id(2) == 0)
        def _():
            o[...] = jnp.zeros_like(o)

        o[...] += _mm_tn(ar[...], br[...]).reshape(o.shape)

    in_specs = [pl.BlockSpec((tl, ta), lambda p, q, l: (l, p)), pl.BlockSpec((tl, tn), lambda p, q, l: (l, q))]
    if into is None:
        return pl.pallas_call(
            body, name=name, grid=(Ka // ta, N // tn, L // tl), in_specs=in_specs,
            out_specs=pl.BlockSpec((ta, tn), lambda p, q, l: (p, q)),
            out_shape=jax.ShapeDtypeStruct((Ka, N), F32),
            compiler_params=_cp("parallel", "parallel", "arbitrary"),
        )(a, b)
    buf, blk, idx = into
    return pl.pallas_call(
        body, name=name, grid=(Ka // ta, N // tn, L // tl), in_specs=in_specs + [_ANY],
        out_specs=pl.BlockSpec(blk, lambda p, q, l: idx(p, q)),
        out_shape=jax.ShapeDtypeStruct(buf.shape, F32), input_output_aliases={2: 0},
        compiler_params=_cp("parallel", "parallel", "arbitrary"),
    )(a, b, buf)


def mix_bwd1(dy, w_out, P, o_f, o_b, conv_a, ghn4, tl):
    L = P.shape[0]
    nt = L // tl

    def body(dyr, wo, gb, gc, gv, go, gcp, gvp, gcn, gvn, of, ob, ca, gh, dgb, dcc, dgo, do, dca, dgh):
        i = pl.program_id(0)

        @pl.when(i == 0)
        def _():
            dca[...] = jnp.zeros_like(dca)
            dgh[...] = jnp.zeros_like(dgh)

        dycat = _mm_nt(dyr[...], wo[...])
        dya = dycat[:, 0:DC]
        dyb = dycat[:, DC:D]
        cp, cn = _halo_rows(gcp, gcn, i, nt - 1)
        vp, vn = _halo_rows(gvp, gvn, i, nt - 1)
        c = gc[...] * gv[...]
        cc, c_m1, c_p1 = _conv3(c, cp * vp, cn * vn, ca)
        dgb[...] = dya * cc
        d = dya * gb[...]
        dcc[...] = d
        for k, s in enumerate((c_m1, c, c_p1)):
            dca[k:k + 1, :] += jnp.sum(d * s, axis=0, keepdims=True)
        oh, rs = _headnorm(of[...] + ob[...])
        g = go[...]
        sg = _sigmoid(g)
        silu = g * sg
        dgo[...] = dyb * (oh * gh[...]) * (sg * (1.0 + g * (1.0 - sg)))
        don = dyb * silu
        t = jnp.sum(don * oh, axis=0, keepdims=True)
        dgh[0:1, :] += t[:, 0:HV] + t[:, HV:2 * HV] + t[:, 2 * HV:3 * HV] + t[:, 3 * HV:4 * HV]
        doh = don * gh[...]
        parts = []
        for h in range(NH):
            hs = slice(h * HV, (h + 1) * HV)
            parts.append(rs[h] * (doh[:, hs] - oh[:, hs] * jnp.mean(doh[:, hs] * oh[:, hs], axis=-1, keepdims=True)))
        do[...] = jnp.concatenate(parts, axis=1)

    t = lambda cb: pl.BlockSpec((tl, DC), lambda i: (i, cb))
    hp = lambda cb: pl.BlockSpec((8, DC), _prev_blk(tl, cb))
    hn = lambda cb: pl.BlockSpec((8, DC), _next_blk(tl, L, cb))
    row = lambda n: pl.BlockSpec((tl, n), lambda i: (i, 0))
    full = lambda a: pl.BlockSpec(a.shape, lambda i: (0, 0))
    f32o = lambda n: jax.ShapeDtypeStruct((L, n), F32)
    return pl.pallas_call(
        body, name="mix_bwd1", grid=(nt,),
        in_specs=[row(D), full(w_out), t(CB_GB), t(CB_GC), t(CB_GV), t(CB_GO), hp(CB_GC), hp(CB_GV), hn(CB_GC), hn(CB_GV),
                  row(DG), row(DG), full(conv_a), full(ghn4)],
        out_specs=[row(DC), row(DC), row(DG), row(DG), pl.BlockSpec((8, DC), lambda i: (0, 0)),
                   pl.BlockSpec((8, HV), lambda i: (0, 0))],
        out_shape=[f32o(DC), f32o(DC), f32o(DG), f32o(DG), jax.ShapeDtypeStruct((8, DC), F32),
                   jax.ShapeDtypeStruct((8, HV), F32)],
        compiler_params=_cp("arbitrary"),
    )(dy, w_out, P, P, P, P, P, P, P, P, o_f, o_b, conv_a, ghn4)


def _gla_chunk_bwd(q, k, v, a, do, st16, g_ref, m, rev):
    p = _gla_prep(q, k, a, m, rev)
    g = g_ref[...]
    dob = _tile4(do) * m["bdo"]
    dv = _mm_tn(p["sc"], dob) + _mm_nt(p["kh"], g)
    dsc = jnp.where(m["triu4"] if rev else m["tril4"], _mm_nt(dob, v), 0.0)
    r1 = _mm(dsc, p["kt"]) * m["bdq"]
    dqt = r1[0:CH] + r1[CH:2 * CH] + r1[2 * CH:3 * CH] + r1[3 * CH:4 * CH] + _mm(do, st16)
    dkt = _mm_tn(dsc, p["qs"])
    dkh = _mm(v, g)
    dd = jnp.sum(g * st16.astype(F32), axis=0, keepdims=True)
    g_ref[...] = g * p["dec"] + _mm_tn(do, p["qt"]) * m["bds"]
    kk = dkh * p["kh"]
    dcum = dqt * p["qt"] - dkt * p["kt"] - kk
    dtot = jnp.sum(kk, axis=0, keepdims=True) + dd * p["dec"]
    da = _mm_tri(m["tril"] if rev else m["triu"], dcum) + dtot
    dq = dqt * p["e"] * QSCALE
    dk = dkt * p["einv"] + dkh * p["eout"]
    return dq, dk, dv, da


def gla_bwd(P, do, sf, sb, gcat, gbias, tl):
    L = P.shape[0]
    nb = L // tl
    nc = tl // CH

    def body(qf, kf, vf, lf, dof, sfr, qb, kb, vb, lb, dob, sbr, gc_ref, bs_ref,
             dqf, dkf, dvf, daf, dqb, dkb, dvb, dab, gf, gbk, af, ab):
        @pl.when(pl.program_id(0) == 0)
        def _():
            gf[...] = jnp.zeros_like(gf)
            gbk[...] = jnp.zeros_like(gbk)

        af[...] = _logsig(_mm(lf[...], gc_ref[:, 0:DK]) + bs_ref[:, 0:DK]) * GATE_NORM
        ab[...] = _logsig(_mm(lb[...], gc_ref[:, DK:2 * DK]) + bs_ref[:, DK:2 * DK]) * GATE_NORM
        m = _gla_masks()

        def chunk(c, carry):
            cf = nc - 1 - c
            rows = pl.ds(pl.multiple_of(cf * CH, CH), CH)
            dq, dk, dv, da = _gla_chunk_bwd(qf[rows, :], kf[rows, :], vf[rows, :], af[rows, :], dof[rows, :],
                                            sfr[cf], gf, m, False)
            dqf[rows, :] = dq
            dkf[rows, :] = dk
            dvf[rows, :] = dv
            daf[rows, :] = da
            rows = pl.ds(pl.multiple_of(c * CH, CH), CH)
            dq, dk, dv, da = _gla_chunk_bwd(qb[rows, :], kb[rows, :], vb[rows, :], ab[rows, :], dob[rows, :],
                                            sbr[c], gbk, m, True)
            dqb[rows, :] = dq
            dkb[rows, :] = dk
            dvb[rows, :] = dv
            dab[rows, :] = da
            return carry

        lax.fori_loop(0, nc, chunk, 0, unroll=2)

    fwd_dir = lambda cb: (lambda i: (nb - 1 - i, cb))
    bwd_dir = lambda cb: (lambda i: (i, cb))

    def side(ix):
        return [pl.BlockSpec((tl, DK), ix(CB_Q)), pl.BlockSpec((tl, DK), ix(CB_K)), pl.BlockSpec((tl, DG), ix(CB_V)),
                pl.BlockSpec((tl, LRW), ix(CB_LR)), pl.BlockSpec((tl, DG), ix(0)),
                pl.BlockSpec((nc, DG, DK), lambda i: (ix(0)(i)[0], 0, 0))]

    def outs(ix):
        return [pl.BlockSpec((tl, DK), ix(0)), pl.BlockSpec((tl, DK), ix(0)), pl.BlockSpec((tl, DG), ix(0)),
                pl.BlockSpec((tl, DK), ix(0))]

    o_shape = [jax.ShapeDtypeStruct((L, DK), F32), jax.ShapeDtypeStruct((L, DK), F32),
               jax.ShapeDtypeStruct((L, DG), F32), jax.ShapeDtypeStruct((L, DK), F32)]
    return pl.pallas_call(
        body, name="gla_bwd", grid=(nb,),
        in_specs=side(fwd_dir) + side(bwd_dir) + [pl.BlockSpec((LRW, 2 * DK), lambda i: (0, 0)),
                                                  pl.BlockSpec((1, 2 * DK), lambda i: (0, 0))],
        out_specs=outs(fwd_dir) + outs(bwd_dir),
        out_shape=o_shape + o_shape,
        scratch_shapes=[pltpu.VMEM((DG, DK), F32), pltpu.VMEM((DG, DK), F32),
                        pltpu.VMEM((tl, DK), F32), pltpu.VMEM((tl, DK), F32)],
        compiler_params=_cp("arbitrary"),
    )(P, P, P, P, do, sf, P, P, P, P, do, sb, gcat, gbias)


def mix_bwd2(dgb, dcc, dgo, gl, P, conv_a, gcat, gbias, w_in, x, dres, g1, tl):
    L = P.shape[0]
    nt = L // tl

    def body(dgbr, dccr, dccp, dccn, dgor, dqf, dkf, dvf, daf, dqb, dkb, dvb, dab, gc, gv, lr, ca, gcr, bsr, wi,
             xr, drr, g1r, dP, dx, dg1, dgcat, dbias):
        i = pl.program_id(0)

        @pl.when(i == 0)
        def _():
            dg1[...] = jnp.zeros_like(dg1)
            dgcat[...] = jnp.zeros_like(dgcat)
            dbias[...] = jnp.zeros_like(dbias)

        p, n = _halo_rows(dccp, dccn, i, nt - 1)
        dc = _conv3_t(dccr[...], p, n, ca)
        pre = _mm(lr[...], gcr[...]) + bsr[...]
        da = jnp.concatenate([daf[...], dab[...]], axis=1)
        dpre = da * GATE_NORM * (1.0 - _sigmoid(pre))
        dpre16 = dpre.astype(BF16)
        dP[:, 0:DC] = dgbr[...].astype(BF16)
        dP[:, DC:2 * DC] = (dc * gv[...]).astype(BF16)
        dP[:, 2 * DC:3 * DC] = (dc * gc[...]).astype(BF16)
        dP[:, 1536:1792] = (dqf[...] + dqb[...]).astype(BF16)
        dP[:, 1792:2048] = (dkf[...] + dkb[...]).astype(BF16)
        dP[:, 2048:2560] = (dvf[...] + dvb[...]).astype(BF16)
        dP[:, 2560:3072] = dgor[...].astype(BF16)
        dP[:, 3072:3200] = _mm_nt(dpre16, gcr[...]).astype(BF16)
        dgcat[...] += _mm_tn(lr[...], dpre16)
        dbias[0:1, :] += jnp.sum(dpre, axis=0, keepdims=True)
        dh, dg = _rms_bwd(_mm_nt(dP[...], wi[...]), xr[...], g1r[...])
        dx[...] = drr[...] + dh
        dg1[...] += dg

    row = lambda n: pl.BlockSpec((tl, n), lambda i: (i, 0))
    t = lambda w, cb: pl.BlockSpec((tl, w), lambda i: (i, cb))
    full = lambda a: pl.BlockSpec(a.shape, lambda i: (0, 0))
    return pl.pallas_call(
        body, name="mix_bwd2", grid=(nt,),
        in_specs=[row(DC), row(DC), pl.BlockSpec((8, DC), _prev_blk(tl, 0)), pl.BlockSpec((8, DC), _next_blk(tl, L, 0)),
                  row(DG), row(DK), row(DK), row(DG), row(DK), row(DK), row(DK), row(DG), row(DK),
                  t(DC, CB_GC), t(DC, CB_GV), t(LRW, CB_LR), full(conv_a), full(gcat), full(gbias), full(w_in),
                  row(D), row(D), full(g1)],
        out_specs=[row(DINP), row(D), pl.BlockSpec((1, D), lambda i: (0, 0)), pl.BlockSpec((LRW, 2 * DK), lambda i: (0, 0)),
                   pl.BlockSpec((8, 2 * DK), lambda i: (0, 0))],
        out_shape=[jax.ShapeDtypeStruct((L, DINP), BF16), jax.ShapeDtypeStruct((L, D), F32),
                   jax.ShapeDtypeStruct((1, D), F32), jax.ShapeDtypeStruct((LRW, 2 * DK), F32),
                   jax.ShapeDtypeStruct((8, 2 * DK), F32)],
        compiler_params=_cp("arbitrary"),
    )(dgb, dcc, dcc, dcc, dgo, *gl, P, P, P, conv_a, gcat, gbias, w_in, x, dres, g1)


def _row_tile(rows, cols):
    if rows * cols * 4 <= 2 * 1024 * 1024:
        return rows
    best = 8
    for t in range(8, rows, 8):
        if rows % t == 0 and t * cols * 4 <= 2 * 1024 * 1024:
            best = t
    return best


def adamw(w, g, m, v, name):
    shape = w.shape
    cols = shape[-1]
    w2, g2, m2, v2 = (a.reshape(-1, cols) for a in (w, g, m, v))
    rows = w2.shape[0]
    tr = _row_tile(rows, cols)

    def body(wr, gr, mr, vr, dl, nm, nv):
        gg = gr[...]
        mm = B1 * mr[...] + (1.0 - B1) * gg
        vv = B2 * vr[...] + (1.0 - B2) * (gg * gg)
        m_hat = mm / (1.0 - B1 ** STEP)
        v_hat = vv / (1.0 - B2 ** STEP)
        dl[...] = -LR * (m_hat / (jnp.sqrt(v_hat) + AEPS) + WD * wr[...])
        nm[...] = mm
        nv[...] = vv

    blk = pl.BlockSpec((tr, cols), lambda i: (i, 0))
    o = jax.ShapeDtypeStruct((rows, cols), F32)
    d, nm, nv = pl.pallas_call(
        body, name=name, grid=(rows // tr,), in_specs=[blk] * 4, out_specs=[blk] * 3, out_shape=[o, o, o],
        compiler_params=_cp("parallel"),
    )(w2, g2, m2, v2)
    return d.reshape(shape), nm.reshape(shape), nv.reshape(shape)


def _place():
    return lax.axis_index("x"), lax.axis_index("y"), lax.axis_index("c")


def allgather8(v, name):
    mp, n = v.shape

    def body(x_ref, out_ref, send_sems, recv_sems, local_sem):
        x, y, c = _place()
        me, sibling = (x, y, c), (x, y, 1 - c)
        chips = [(1 - x, y), (x, 1 - y), (1 - x, 1 - y)]

        def rows(px, py, pc):
            return out_ref.at[pl.ds((4 * px + 2 * py + pc) * mp, mp), :]

        def copy(k, block, to, src=None):
            return pltpu.make_async_remote_copy(
                src_ref=rows(*block) if src is None else src, dst_ref=rows(*block),
                send_sem=send_sems.at[k], recv_sem=recv_sems.at[k], device_id=to, device_id_type=MESH)

        mine = pltpu.make_async_copy(x_ref, rows(*me), local_sem)
        mine.start()
        first = [copy(0, me, sibling, src=x_ref)]
        first += [copy(1 + j, me, (*chip, c), src=x_ref) for j, chip in enumerate(chips)]
        for cp in first:
            cp.start()
        passed = [copy(4 + j, (*chip, c), sibling) for j, chip in enumerate(chips)]
        for j, chip in enumerate(chips):
            copy(1 + j, (*chip, c), me).wait_recv()
            passed[j].start()
        copy(0, sibling, me).wait_recv()
        for j, chip in enumerate(chips):
            copy(4 + j, (*chip, 1 - c), me).wait_recv()
        for cp in first + passed:
            cp.wait_send()
        mine.wait()

    return pl.pallas_call(
        body, name=name, out_shape=jax.ShapeDtypeStruct((8 * mp, n), v.dtype),
        in_specs=[pl.BlockSpec(memory_space=pltpu.VMEM)], out_specs=pl.BlockSpec(memory_space=pltpu.VMEM),
        scratch_shapes=[pltpu.SemaphoreType.DMA((7,)), pltpu.SemaphoreType.DMA((7,)), pltpu.SemaphoreType.DMA],
        compiler_params=pltpu.CompilerParams(vmem_limit_bytes=VMEM_LIMIT),
    )(v)


def sum8(v, mp):
    def body(x_ref, o_ref):
        acc = x_ref[0:mp, :]
        for d in range(1, 8):
            acc = acc + x_ref[d * mp:(d + 1) * mp, :]
        o_ref[...] = acc

    return pl.pallas_call(body, name="sum8", out_shape=jax.ShapeDtypeStruct((mp, v.shape[1]), F32),
                          compiler_params=pltpu.CompilerParams(vmem_limit_bytes=VMEM_LIMIT))(v)


_ANY = pl.BlockSpec(memory_space=pl.ANY)


def _row_half(ref, lead, h):
    hr = ref.shape[-2] // 2
    return ref.at[(*lead, pl.ds(h * hr, hr), slice(None))]


def allgather_weights(slots):
    n = len(slots)

    def body(*refs):
        s_refs, o_refs, (send_sems, recv_sems) = refs[:n], refs[n:2 * n], refs[2 * n:]
        x, y, c = _place()
        me = 2 * x + y
        sibling = (x, y, 1 - c)
        chips = [(1 - x, y), (x, 1 - y), (1 - x, 1 - y)]

        def half(ref, slot, h):
            return _row_half(ref, (slot, slice(None)), h)

        def copy(k, src, dst, to):
            return pltpu.make_async_remote_copy(src_ref=src, dst_ref=dst, send_sem=send_sems.at[k],
                                                recv_sem=recv_sems.at[k], device_id=to, device_id_type=MESH)

        first = [copy(6 * a + k, half(s_refs[a], me, c), half(o_refs[a], me, c), (px, py, c))
                 for k, (px, py) in enumerate(chips) for a in range(n)]
        for cp in first:
            cp.start()
        passed = []
        for k, (px, py) in enumerate(chips):
            for a in range(n):
                got = half(o_refs[a], 2 * px + py, c)
                copy(6 * a + k, half(s_refs[a], me, c), got, (px, py, c)).wait_recv()
                cp = copy(6 * a + 3 + k, got, got, sibling)
                cp.start()
                passed.append(cp)
        for k, (px, py) in enumerate(chips):
            for a in range(n):
                got = half(o_refs[a], 2 * px + py, 1 - c)
                copy(6 * a + 3 + k, got, got, sibling).wait_recv()
        for cp in first + passed:
            cp.wait_send()

    return pl.pallas_call(
        body, name="allgather_weights", out_shape=[jax.ShapeDtypeStruct(s.shape, s.dtype) for s in slots],
        in_specs=[_ANY] * n, out_specs=[_ANY] * n, input_output_aliases={a: a for a in range(n)},
        scratch_shapes=[pltpu.SemaphoreType.DMA((6 * n,)), pltpu.SemaphoreType.DMA((6 * n,))],
    )(*slots)


def rs_sibling_halves(gs):
    n = len(gs)

    def body(*refs):
        g_refs, r_refs, (send_sems, recv_sems) = refs[:n], refs[n:2 * n], refs[2 * n:]
        x, y, c = _place()
        cps = [pltpu.make_async_remote_copy(
            src_ref=_row_half(g_refs[a], (slice(None), slice(None)), 1 - c), dst_ref=r_refs[a],
            send_sem=send_sems.at[a], recv_sem=recv_sems.at[a], device_id=(x, y, 1 - c), device_id_type=MESH)
            for a in range(n)]
        for cp in cps:
            cp.start()
        for cp in cps:
            cp.wait()

    return pl.pallas_call(
        body, name="rs_sibling_halves",
        out_shape=[jax.ShapeDtypeStruct((*g.shape[:2], g.shape[2] // 2, g.shape[3]), F32) for g in gs],
        in_specs=[_ANY] * n, out_specs=[_ANY] * n,
        scratch_shapes=[pltpu.SemaphoreType.DMA((n,)), pltpu.SemaphoreType.DMA((n,))],
    )(*gs)


def rs_chipsum16(g, recv1, cidx, name):
    nl, hr, cols = recv1.shape[1:]

    def body(c_ref, g_ref, r_ref, o_ref):
        o_ref[...] = (g_ref[...] + r_ref[...]).astype(BF16)

    blk = (1, 1, hr, cols)
    return pl.pallas_call(
        body, name=name, out_shape=jax.ShapeDtypeStruct(recv1.shape, BF16),
        grid_spec=pltpu.PrefetchScalarGridSpec(
            num_scalar_prefetch=1, grid=(4, nl),
            in_specs=[pl.BlockSpec(blk, lambda j, l, c: (j, l, c[0], 0)), pl.BlockSpec(blk, lambda j, l, c: (j, l, 0, 0))],
            out_specs=pl.BlockSpec(blk, lambda j, l, c: (j, l, 0, 0))),
        compiler_params=_cp("parallel", "parallel"),
    )(cidx, g, recv1)


def rs_exchange_chips(cs):
    n = len(cs)

    def body(*refs):
        s_refs, r_refs, (send_sems, recv_sems) = refs[:n], refs[n:2 * n], refs[2 * n:]
        x, y, c = _place()
        chips = [(1 - x, y), (x, 1 - y), (1 - x, 1 - y)]
        cps = [pltpu.make_async_remote_copy(
            src_ref=s_refs[a].at[2 * px + py], dst_ref=r_refs[a].at[k], send_sem=send_sems.at[3 * a + k],
            recv_sem=recv_sems.at[3 * a + k], device_id=(px, py, c), device_id_type=MESH)
            for k, (px, py) in enumerate(chips) for a in range(n)]
        for cp in cps:
            cp.start()
        for cp in cps:
            cp.wait()

    return pl.pallas_call(
        body, name="rs_exchange_chips", out_shape=[jax.ShapeDtypeStruct((3, *s.shape[1:]), BF16) for s in cs],
        in_specs=[_ANY] * n, out_specs=[_ANY] * n,
        scratch_shapes=[pltpu.SemaphoreType.DMA((3 * n,)), pltpu.SemaphoreType.DMA((3 * n,))],
    )(*cs)


def rs_final_sum(g, recv1, recv2, idx, name):
    nl, hr, cols = recv1.shape[1:]

    def body(i_ref, g_ref, r1_ref, r2_ref, o_ref):
        acc = g_ref[0, 0] + r1_ref[0, 0]
        for k in range(3):
            acc = acc + r2_ref[k, 0].astype(F32)
        o_ref[0] = acc

    blk = (1, 1, hr, cols)
    return pl.pallas_call(
        body, name=name, out_shape=jax.ShapeDtypeStruct((nl, 2 * hr, cols), F32),
        grid_spec=pltpu.PrefetchScalarGridSpec(
            num_scalar_prefetch=1, grid=(nl,),
            in_specs=[pl.BlockSpec(blk, lambda l, ix: (ix[0], l, ix[1], 0)), pl.BlockSpec(blk, lambda l, ix: (ix[0], l, 0, 0)),
                      pl.BlockSpec((3, 1, hr, cols), lambda l, ix: (0, l, 0, 0))],
            out_specs=pl.BlockSpec((1, hr, cols), lambda l, ix: (l, ix[1], 0))),
        compiler_params=_cp("parallel"),
    )(idx, g, recv1, recv2)


def rs_share_halves(fulls):
    n = len(fulls)

    def body(*refs):
        h_refs, o_refs, (send_sems, recv_sems) = refs[:n], refs[n:2 * n], refs[2 * n:]
        x, y, c = _place()
        sibling = (x, y, 1 - c)

        def copy(a, h):
            return pltpu.make_async_remote_copy(
                src_ref=_row_half(h_refs[a], (slice(None),), h), dst_ref=_row_half(o_refs[a], (slice(None),), h),
                send_sem=send_sems.at[a], recv_sem=recv_sems.at[a], device_id=sibling, device_id_type=MESH)

        for a in range(n):
            copy(a, c).start()
        for a in range(n):
            copy(a, c).wait_send()
            copy(a, 1 - c).wait_recv()

    return pl.pallas_call(
        body, name="rs_share_halves", out_shape=[jax.ShapeDtypeStruct(f.shape, F32) for f in fulls],
        in_specs=[_ANY] * n, out_specs=[_ANY] * n, input_output_aliases={a: a for a in range(n)},
        scratch_shapes=[pltpu.SemaphoreType.DMA((n,)), pltpu.SemaphoreType.DMA((n,))],
    )(*fulls)


def _own_slot(shard, chip, dtype):
    return lax.dynamic_update_slice(lax.empty((4, *shard.shape), dtype), shard.astype(dtype)[None], (chip, 0, 0, 0))


def kernel(x, norm_mix_pre, norm_mix_post, norm_ffn_pre, norm_ffn_post, w_in, conv_a, gate_up_fwd, gate_bias_fwd, gate_up_bwd, gate_bias_bwd, gla_head_norm, w_out, w_up, conv_ffn, w_down, loss_target, m_norm_mix_pre, m_norm_mix_post, m_norm_ffn_pre, m_norm_ffn_post, m_w_in, m_conv_a, m_gate_up_fwd, m_gate_bias_fwd, m_gate_up_bwd, m_gate_bias_bwd, m_gla_head_norm, m_w_out, m_w_up, m_conv_ffn, m_w_down, v_norm_mix_pre, v_norm_mix_post, v_norm_ffn_pre, v_norm_ffn_post, v_w_in, v_conv_a, v_gate_up_fwd, v_gate_bias_fwd, v_gate_up_bwd, v_gate_bias_bwd, v_gla_head_norm, v_w_out, v_w_up, v_conv_ffn, v_w_down):
    L = x.shape[1]
    xi, yi, ci = _place()
    chip = 2 * xi + yi
    tl_gla, tl_mix, tl_ffn = min(L, TL_GLA), min(L, TL_MIX), min(L, TL_FFN)

    a_in, a_out, a_up, a_down = allgather_weights([_own_slot(w, chip, BF16) for w in (w_in, w_out, w_up, w_down)])
    W_in = jnp.pad(jnp.concatenate([a_in[j] for j in range(4)], axis=2), ((0, 0), (0, 0), (0, DINP - DIN)))
    W_out = jnp.transpose(a_out, (1, 0, 2, 3)).reshape(DEPTH, D, D)
    W_down = jnp.transpose(a_down, (1, 0, 2, 3)).reshape(DEPTH, DFF, D)

    small = jnp.concatenate([conv_a.reshape(-1), gate_up_fwd.reshape(-1), gate_up_bwd.reshape(-1), conv_ffn.reshape(-1)])
    ms = small.shape[0] // 128
    sg = allgather8(small.reshape(ms, 128), "allgather_small_weights").reshape(4, 2, ms * 128)[:, 0]

    def small_full(off, shape):
        n = shape[0] * shape[1] * shape[2]
        return jnp.concatenate([sg[j, off:off + n].reshape(shape) for j in range(4)], axis=2)

    o1 = DEPTH * 3 * 128
    o2 = o1 + DEPTH * RK * 64
    o3 = o2 + DEPTH * RK * 64
    conv_a_f = small_full(0, (DEPTH, 3, 128))
    gup_f = small_full(o1, (DEPTH, RK, 64))
    gup_b = small_full(o2, (DEPTH, RK, 64))
    conv_ffn_f = small_full(o3, (DEPTH, 3, 1408))

    def gcat_of(l):
        g = jnp.zeros((LRW, 2 * DK), F32)
        g = g.at[0:RK, 0:DK].set(gup_f[l]).at[RK:2 * RK, DK:2 * DK].set(gup_b[l])
        return g.astype(BF16)

    gcats = [gcat_of(l) for l in range(DEPTH)]
    gbiases = [jnp.concatenate([gate_bias_fwd[l], gate_bias_bwd[l]])[None, :] for l in range(DEPTH)]
    ghn4s = [jnp.tile(gla_head_norm[l], NH)[None, :] for l in range(DEPTH)]

    xc = x.reshape(L, D)
    saved = []
    for l in range(DEPTH):
        P, h1 = rms_matmul(xc, norm_mix_pre[l][None, :], W_in[l], 640, "proj_in")
        o_f, o_b, sf, sb = gla_fwd(P, gcats[l], gbiases[l], tl_gla)
        ycat, y, x1 = mix_out(P, o_f, o_b, conv_a_f[l], ghn4s[l], W_out[l], norm_mix_post[l][None, :], xc, tl_mix)
        U, h2 = rms_matmul(x1, norm_ffn_pre[l][None, :], a_up, WFF, "proj_up", n_out=2 * DFF, out_dtype=BF16,
                           w_spec=pl.BlockSpec((None, None, D, WFF), lambda i, j, l=l: (j, l, 0, 0)))
        y2, x2 = ffn_down(U, conv_ffn_f[l], W_down[l], norm_ffn_post[l][None, :], x1, tl_ffn)
        saved.append(dict(x=xc, h1=h1, P=P, o_f=o_f, o_b=o_b, sf=sf, sb=sb, ycat=ycat, y=y, x1=x1, h2=h2, U=U, y2=y2))
        xc = x2

    dx, loss_blk = loss_head(xc, loss_target.reshape(L, D), tl_mix)

    g_out = lax.empty((4, DEPTH, D // 4, D), F32)
    g_up = lax.empty((4, DEPTH, D, WFF), F32)
    g_down = lax.empty((4, DEPTH, DFF // 4, D), F32)
    g_in_layers = [None] * DEPTH
    grads = [None] * DEPTH
    tl_dw = min(L, 1024)
    for l in reversed(range(DEPTH)):
        s = saved[l]
        dy2, dg4 = rms_bwd_pre(dx, s["y2"], norm_ffn_post[l][None, :], tl_mix)
        du_g, du_v, z, dcf_g, dcf_v = ffn_bwd1(dy2, s["U"], conv_ffn_f[l], W_down[l], tl_ffn)
        g_down = matmul_tn(z, dy2, DFF // 2, D, tl_dw, "dw_down",
                           into=(g_down, (2, None, DFF // 4, D), lambda p, q, l=l: (p, l, 0, 0)))
        dU_g, dU_v, dx1, dg3 = ffn_bwd2(du_g, du_v, conv_ffn_f[l], a_up, l, s["x1"], dx, norm_ffn_pre[l][None, :],
                                        min(L, TL_FFN2))
        g_up = matmul_tn(s["h2"], dU_g, D, WFF, tl_dw, "dw_up_gate",
                         into=(g_up, (None, None, D, WFF), lambda p, q, l=l: (q, l, 0, 0)))
        g_up = matmul_tn(s["h2"], dU_v, D, WFF, tl_dw, "dw_up_val",
                         into=(g_up, (None, None, D, WFF), lambda p, q, l=l: (NFF + q, l, 0, 0)))
        dy, dg2 = rms_bwd_pre(dx1, s["y"], norm_mix_post[l][None, :], tl_mix)
        dgb, dcc, dgo, do, dca, dghn = mix_bwd1(dy, W_out[l], s["P"], s["o_f"], s["o_b"], conv_a_f[l], ghn4s[l], tl_mix)
        g_out = matmul_tn(s["ycat"], dy, D, D, tl_dw, "dw_out",
                          into=(g_out, (4, None, D // 4, D), lambda p, q, l=l: (0, l, 0, 0)))
        gl = gla_bwd(s["P"], do, s["sf"], s["sb"], gcats[l], gbiases[l], tl_gla)
        dP, dx, dg1, dgcat, dbias = mix_bwd2(dgb, dcc, dgo, gl, s["P"], conv_a_f[l], gcats[l], gbiases[l], W_in[l],
                                             s["x"], dx1, norm_mix_pre[l][None, :], tl_mix)
        dW_in = matmul_tn(s["h1"], dP, D, 640, tl_dw, "dw_in")
        g_in_layers[l] = jnp.stack([dW_in[:, (DIN // 4) * j:(DIN // 4) * (j + 1)] for j in range(4)])
        grads[l] = dict(
            norm_mix_pre=dg1[0], norm_mix_post=dg2[0], norm_ffn_pre=dg3[0], norm_ffn_post=dg4[0],
            conv_a=dca[0:3], gate_up_fwd=dgcat[0:RK, 0:DK], gate_bias_fwd=dbias[0, 0:DK],
            gate_up_bwd=dgcat[RK:2 * RK, DK:2 * DK], gate_bias_bwd=dbias[0, DK:2 * DK], gla_head_norm=dghn[0],
            conv_ffn=jnp.concatenate([dcf_g[0:3], dcf_v[0:3]], axis=1))

    g_in = jnp.stack(g_in_layers, axis=1)
    G = {k: jnp.stack([grads[l][k] for l in range(DEPTH)]) for k in grads[0]}

    small_names = ["norm_mix_pre", "norm_mix_post", "norm_ffn_pre", "norm_ffn_post", "conv_a", "gate_up_fwd",
                   "gate_bias_fwd", "gate_up_bwd", "gate_bias_bwd", "gla_head_norm", "conv_ffn"]
    flat = jnp.concatenate([G[k].reshape(-1) for k in small_names] + [loss_blk[0, 0:1]])
    n_small = flat.shape[0]
    mp = -(-n_small // 1024) * 8
    flat = jnp.pad(flat, (0, mp * 128 - n_small)).reshape(mp, 128)
    tot = sum8(allgather8(flat, "allgather_small_grads"), mp).reshape(-1)
    gsm = {}
    o = 0
    for k in small_names:
        n = G[k].size
        gsm[k] = tot[o:o + n].reshape(G[k].shape)
        o += n
    loss = tot[o]

    def my_cols(a, width):
        return lax.dynamic_slice_in_dim(a, chip * width, width, axis=2)

    gsm["conv_a"] = my_cols(gsm["conv_a"], 128)
    gsm["gate_up_fwd"] = my_cols(gsm["gate_up_fwd"], 64)
    gsm["gate_up_bwd"] = my_cols(gsm["gate_up_bwd"], 64)
    gsm["conv_ffn"] = my_cols(gsm["conv_ffn"], 1408)

    big = ("w_in", "w_out", "w_up", "w_down")
    gs = [g_in, g_out, g_up, g_down]
    recv1 = rs_sibling_halves(gs)
    cidx = jnp.reshape(ci, (1,)).astype(jnp.int32)
    cs16 = [rs_chipsum16(g, r, cidx, "rs_chipsum16_" + k) for g, r, k in zip(gs, recv1, big)]
    recv2 = rs_exchange_chips(cs16)
    idx = jnp.stack([chip, ci]).astype(jnp.int32)
    halves = [rs_final_sum(g, r1, r2, idx, "rs_final_sum_" + k) for g, r1, r2, k in zip(gs, recv1, recv2, big)]
    gsm.update(zip(big, rs_share_halves(halves)))

    names = ["norm_mix_pre", "norm_mix_post", "norm_ffn_pre", "norm_ffn_post", "w_in", "conv_a", "gate_up_fwd",
             "gate_bias_fwd", "gate_up_bwd", "gate_bias_bwd", "gla_head_norm", "w_out", "w_up", "conv_ffn", "w_down"]
    w = dict(norm_mix_pre=norm_mix_pre, norm_mix_post=norm_mix_post, norm_ffn_pre=norm_ffn_pre, norm_ffn_post=norm_ffn_post,
             w_in=w_in, conv_a=conv_a, gate_up_fwd=gate_up_fwd, gate_bias_fwd=gate_bias_fwd, gate_up_bwd=gate_up_bwd,
             gate_bias_bwd=gate_bias_bwd, gla_head_norm=gla_head_norm, w_out=w_out, w_up=w_up, conv_ffn=conv_ffn, w_down=w_down)
    m = dict(norm_mix_pre=m_norm_mix_pre, norm_mix_post=m_norm_mix_post, norm_ffn_pre=m_norm_ffn_pre, norm_ffn_post=m_norm_ffn_post,
             w_in=m_w_in, conv_a=m_conv_a, gate_up_fwd=m_gate_up_fwd, gate_bias_fwd=m_gate_bias_fwd, gate_up_bwd=m_gate_up_bwd,
             gate_bias_bwd=m_gate_bias_bwd, gla_head_norm=m_gla_head_norm, w_out=m_w_out, w_up=m_w_up, conv_ffn=m_conv_ffn, w_down=m_w_down)
    v = dict(norm_mix_pre=v_norm_mix_pre, norm_mix_post=v_norm_mix_post, norm_ffn_pre=v_norm_ffn_pre, norm_ffn_post=v_norm_ffn_post,
             w_in=v_w_in, conv_a=v_conv_a, gate_up_fwd=v_gate_up_fwd, gate_bias_fwd=v_gate_bias_fwd, gate_up_bwd=v_gate_up_bwd,
             gate_bias_bwd=v_gate_bias_bwd, gla_head_norm=v_gla_head_norm, w_out=v_w_out, w_up=v_w_up, conv_ffn=v_conv_ffn, w_down=v_w_down)
    upd = {k: adamw(w[k], gsm[k], m[k], v[k], "adamw_" + k) for k in names}
    return (loss, dx.reshape(1, L, D), *[gsm[k] for k in names], *[upd[k][0] for k in names],
            *[upd[k][1] for k in names], *[upd[k][2] for k in names])
```

```python
import functools

import jax
import jax.numpy as jnp
from jax import lax
from jax.experimental import pallas as pl
from jax.experimental.pallas import tpu as pltpu

F32 = jnp.float32
BF16 = jnp.bfloat16
MXU_DTYPE = jnp.bfloat16
MESH = pl.DeviceIdType.MESH

D = 1024
DC = 512
DG = 512
NH = 4
HV = 128
HK = 64
DK = 256
RK = 16
CH = 64
DFF = 2816
DIN = 3104
DINP = 3200
LRW = 128
DEPTH = 4
EPS = 1e-6
QSCALE = HK ** -0.5
GATE_NORM = 1.0 / 16.0
CB_GB, CB_GC, CB_GV, CB_GO = 0, 1, 2, 5
CB_Q, CB_K = 6, 7
CB_V = 4
CB_LR = 24
LR = 0.001
B1 = 0.9
B2 = 0.999
AEPS = 1e-08
WD = 0.01
STEP = 10
TM_PROJ = 1024
TL_GLA = 512
TL_MIX = 256
TL_FFN = 256
TL_FFN2 = 512
VMEM_LIMIT = 56 * 1024 * 1024


def _cp(*sem):
    return pltpu.CompilerParams(dimension_semantics=sem if sem else None, vmem_limit_bytes=VMEM_LIMIT)


def _mm(a, b):
    return jnp.dot(a.astype(MXU_DTYPE), b.astype(MXU_DTYPE), preferred_element_type=F32)


def _mm_nt(a, b):
    return lax.dot_general(a.astype(MXU_DTYPE), b.astype(MXU_DTYPE), (((1,), (1,)), ((), ())),
                           preferred_element_type=F32)


def _mm_tn(a, b):
    return lax.dot_general(a.astype(MXU_DTYPE), b.astype(MXU_DTYPE), (((0,), (0,)), ((), ())),
                           preferred_element_type=F32)


def _mm_tri(tri, b):
    return jnp.dot(tri, b, precision=lax.Precision.HIGHEST, preferred_element_type=F32)


def _rms(x, g):
    r = lax.rsqrt(jnp.mean(x * x, axis=-1, keepdims=True) + EPS)
    return x * r * g


def _rms_bwd(dout, y, g):
    r = lax.rsqrt(jnp.mean(y * y, axis=-1, keepdims=True) + EPS)
    yh = y * r
    dyh = dout * g
    dy = r * (dyh - yh * jnp.mean(dyh * yh, axis=-1, keepdims=True))
    dg = jnp.sum(dout * yh, axis=0, keepdims=True)
    return dy, dg


def _sigmoid(x):
    return 0.5 * jnp.tanh(0.5 * x) + 0.5


def _logsig(x):
    return jnp.minimum(x, 0.0) - jnp.log1p(jnp.exp(-jnp.abs(x)))


def _shifts(x, p8, n8):
    n = x.shape[0]
    xe = jnp.concatenate([p8, x, n8], axis=0)
    return pltpu.roll(xe, 1, 0)[8:8 + n], pltpu.roll(xe, n + 15, 0)[8:8 + n]


def _halo_rows(prev_ref, next_ref, i, last):
    hr = prev_ref.shape[0]
    p = jnp.where(i == 0, 0.0, prev_ref[...].astype(F32)[hr - 8:hr, :])
    n = jnp.where(i == last, 0.0, next_ref[...].astype(F32)[0:8, :])
    return p, n


def _conv3(x, xp, xn, w_ref):
    xm1, xp1 = _shifts(x, xp, xn)
    return w_ref[0:1, :] * xm1 + w_ref[1:2, :] * x + w_ref[2:3, :] * xp1, xm1, xp1


def _conv3_t(d, dp, dn, w_ref):
    dm1, dp1 = _shifts(d, dp, dn)
    return w_ref[0:1, :] * dp1 + w_ref[1:2, :] * d + w_ref[2:3, :] * dm1


HALO32 = 8
HALO16 = 16


def _prev_row_blk(i, tl, hr):
    return jnp.maximum(i * (tl // hr) - 1, 0)


def _next_row_blk(i, tl, nrows, hr):
    return jnp.minimum((i + 1) * (tl // hr), nrows // hr - 1)


def _prev_blk(tl, cb, hr=HALO32):
    return lambda i: (_prev_row_blk(i, tl, hr), cb)


def _next_blk(tl, nrows, cb, hr=HALO32):
    return lambda i: (_next_row_blk(i, tl, nrows, hr), cb)


def rms_matmul(x, g, w, tn, name, w_spec=None, n_out=None, out_dtype=F32, after=()):
    L = x.shape[0]
    N = w.shape[1] if n_out is None else n_out
    tm = min(L, TM_PROJ)
    if w_spec is None:
        w_spec = pl.BlockSpec((D, tn), lambda i, j: (0, j))

    def body(x_ref, g_ref, w_ref, *rest):
        o_ref, h_ref = rest[-2:]

        @pl.when(pl.program_id(1) == 0)
        def _():
            h_ref[...] = _rms(x_ref[...], g_ref[...]).astype(BF16)

        o_ref[...] = _mm(h_ref[...], w_ref[...]).astype(out_dtype)

    return pl.pallas_call(
        body, name=name, grid=(L // tm, N // tn),
        in_specs=[pl.BlockSpec((tm, D), lambda i, j: (i, 0)), pl.BlockSpec((1, D), lambda i, j: (0, 0)), w_spec]
        + [_ANY] * len(after),
        out_specs=[pl.BlockSpec((tm, tn), lambda i, j: (i, j)), pl.BlockSpec((tm, D), lambda i, j: (i, 0))],
        out_shape=[jax.ShapeDtypeStruct((L, N), out_dtype), jax.ShapeDtypeStruct((L, D), BF16)],
        compiler_params=_cp("parallel", "arbitrary"),
    )(x, g, w, *after)


def _gla_masks():
    def blk(shape, rdiv, cdiv):
        r = lax.broadcasted_iota(jnp.int32, shape, 0) // rdiv
        c = lax.broadcasted_iota(jnp.int32, shape, 1) // cdiv
        return (r == c).astype(F32)

    r = lax.broadcasted_iota(jnp.int32, (CH, CH), 0)
    c = lax.broadcasted_iota(jnp.int32, (CH, CH), 1)
    r4 = lax.broadcasted_iota(jnp.int32, (NH * CH, CH), 0) % CH
    c4 = lax.broadcasted_iota(jnp.int32, (NH * CH, CH), 1)
    return dict(
        bdq=blk((NH * CH, DK), CH, HK),
        bdo=blk((NH * CH, DG), CH, HV),
        bds=blk((DG, DK), HV, HK),
        tril=(r >= c).astype(F32), triu=(r <= c).astype(F32),
        tril4=r4 >= c4, triu4=r4 <= c4,
    )


def _tile4(x):
    return jnp.concatenate([x, x, x, x], axis=0)


def _gla_prep(q, k, a, m, rev):
    cum = _mm_tri(m["triu"] if rev else m["tril"], a)
    tot = jnp.sum(a, axis=0, keepdims=True)
    e = jnp.exp(cum)
    einv = jnp.exp(-cum)
    eout = jnp.exp(tot - cum)
    qt = q * QSCALE * e
    kt = k * einv
    kh = k * eout
    qs = _tile4(qt) * m["bdq"]
    sc = _mm_nt(qs, kt)
    sc = jnp.where(m["triu4"] if rev else m["tril4"], sc, 0.0)
    return dict(e=e, einv=einv, eout=eout, dec=jnp.exp(tot), qt=qt, kt=kt, kh=kh, qs=qs, sc=sc)


def _gla_chunk_fwd(q, k, v, a, st_ref, m, rev):
    p = _gla_prep(q, k, a, m, rev)
    r = _mm(p["sc"], v)
    o_intra = jnp.concatenate([r[h * CH:(h + 1) * CH, h * HV:(h + 1) * HV] for h in range(NH)], axis=1)
    st = st_ref[...]
    st16 = st.astype(BF16)
    o = o_intra + _mm_nt(p["qt"], st16)
    st_ref[...] = st * p["dec"] + _mm_tn(v, p["kh"]) * m["bds"]
    return o, st16


def gla_fwd(P, gcat, gbias, tl):
    L = P.shape[0]
    nb = L // tl
    nc = tl // CH

    def body(qf, kf, vf, lf, qb, kb, vb, lb, gc_ref, bs_ref, of, ob, sf, sb, stf, stb, af, ab):
        @pl.when(pl.program_id(0) == 0)
        def _():
            stf[...] = jnp.zeros_like(stf)
            stb[...] = jnp.zeros_like(stb)

        af[...] = _logsig(_mm(lf[...], gc_ref[:, 0:DK]) + bs_ref[:, 0:DK]) * GATE_NORM
        ab[...] = _logsig(_mm(lb[...], gc_ref[:, DK:2 * DK]) + bs_ref[:, DK:2 * DK]) * GATE_NORM
        m = _gla_masks()

        def chunk(c, carry):
            rows = pl.ds(pl.multiple_of(c * CH, CH), CH)
            o, st = _gla_chunk_fwd(qf[rows, :], kf[rows, :], vf[rows, :], af[rows, :], stf, m, False)
            of[rows, :] = o
            sf[c] = st
            cb = nc - 1 - c
            rows = pl.ds(pl.multiple_of(cb * CH, CH), CH)
            o, st = _gla_chunk_fwd(qb[rows, :], kb[rows, :], vb[rows, :], ab[rows, :], stb, m, True)
            ob[rows, :] = o
            sb[cb] = st
            return carry

        lax.fori_loop(0, nc, chunk, 0, unroll=2)

    fw = lambda cb: (lambda i: (i, cb))
    bw = lambda cb: (lambda i: (nb - 1 - i, cb))
    return pl.pallas_call(
        body, name="gla_fwd", grid=(nb,),
        in_specs=[pl.BlockSpec((tl, DK), fw(CB_Q)), pl.BlockSpec((tl, DK), fw(CB_K)), pl.BlockSpec((tl, DG), fw(CB_V)),
                  pl.BlockSpec((tl, LRW), fw(CB_LR)),
                  pl.BlockSpec((tl, DK), bw(CB_Q)), pl.BlockSpec((tl, DK), bw(CB_K)), pl.BlockSpec((tl, DG), bw(CB_V)),
                  pl.BlockSpec((tl, LRW), bw(CB_LR)),
                  pl.BlockSpec((LRW, 2 * DK), lambda i: (0, 0)), pl.BlockSpec((1, 2 * DK), lambda i: (0, 0))],
        out_specs=[pl.BlockSpec((tl, DG), lambda i: (i, 0)), pl.BlockSpec((tl, DG), lambda i: (nb - 1 - i, 0)),
                   pl.BlockSpec((nc, DG, DK), lambda i: (i, 0, 0)), pl.BlockSpec((nc, DG, DK), lambda i: (nb - 1 - i, 0, 0))],
        out_shape=[jax.ShapeDtypeStruct((L, DG), F32), jax.ShapeDtypeStruct((L, DG), F32),
                   jax.ShapeDtypeStruct((L // CH, DG, DK), BF16), jax.ShapeDtypeStruct((L // CH, DG, DK), BF16)],
        scratch_shapes=[pltpu.VMEM((DG, DK), F32), pltpu.VMEM((DG, DK), F32),
                        pltpu.VMEM((tl, DK), F32), pltpu.VMEM((tl, DK), F32)],
        compiler_params=_cp("arbitrary"),
    )(P, P, P, P, P, P, P, P, gcat, gbias)


def _headnorm(o):
    oh, rs = [], []
    for h in range(NH):
        oo = o[:, h * HV:(h + 1) * HV]
        r = lax.rsqrt(jnp.mean(oo * oo, axis=-1, keepdims=True) + EPS)
        oh.append(oo * r)
        rs.append(r)
    return jnp.concatenate(oh, axis=1), rs


def mix_out(P, o_f, o_b, conv_a, ghn4, w_out, g2, x, tl):
    L = P.shape[0]
    nt = L // tl

    def body(gb, gc, gv, go, gcp, gvp, gcn, gvn, of, ob, ca, gh, wo, g2r, xr, ycat, yr, x1):
        i = pl.program_id(0)
        cp, cn = _halo_rows(gcp, gcn, i, nt - 1)
        vp, vn = _halo_rows(gvp, gvn, i, nt - 1)
        c = gc[...] * gv[...]
        cc, _, _ = _conv3(c, cp * vp, cn * vn, ca)
        ya = gb[...] * cc
        oh, _ = _headnorm(of[...] + ob[...])
        g = go[...]
        yb = g * _sigmoid(g) * (oh * gh[...])
        yc = jnp.concatenate([ya, yb], axis=1).astype(BF16)
        ycat[...] = yc
        y = _mm(yc, wo[...])
        yr[...] = y
        x1[...] = xr[...] + _rms(y, g2r[...])

    t = lambda cb: pl.BlockSpec((tl, DC), lambda i: (i, cb))
    hp = lambda cb: pl.BlockSpec((8, DC), _prev_blk(tl, cb))
    hn = lambda cb: pl.BlockSpec((8, DC), _next_blk(tl, L, cb))
    row = lambda n: pl.BlockSpec((tl, n), lambda i: (i, 0))
    full = lambda a: pl.BlockSpec(a.shape, lambda i: (0, 0))
    return pl.pallas_call(
        body, name="mix_out", grid=(nt,),
        in_specs=[t(CB_GB), t(CB_GC), t(CB_GV), t(CB_GO), hp(CB_GC), hp(CB_GV), hn(CB_GC), hn(CB_GV),
                  row(DG), row(DG), full(conv_a), full(ghn4), full(w_out), full(g2), row(D)],
        out_specs=[row(D), row(D), row(D)],
        out_shape=[jax.ShapeDtypeStruct((L, D), BF16), jax.ShapeDtypeStruct((L, D), F32),
                   jax.ShapeDtypeStruct((L, D), F32)],
        compiler_params=_cp("parallel"),
    )(P, P, P, P, P, P, P, P, o_f, o_b, conv_a, ghn4, w_out, g2, x)


NFF = 2
WFF = DFF // NFF


def ffn_down(U, conv_ffn, w_down, g4, x1, tl):
    L = U.shape[0]
    nt = L // tl

    def body(u, up, un, cf, wd, g4r, x1r, y2, x2):
        i = pl.program_id(0)
        acc = jnp.zeros((tl, D), F32)
        for j in range(NFF):
            gs = slice(j * WFF, (j + 1) * WFF)
            vs = slice(DFF + j * WFF, DFF + (j + 1) * WFF)
            z = []
            for s in (gs, vs):
                p, n = _halo_rows(up.at[:, s], un.at[:, s], i, nt - 1)
                z.append(_conv3(u[:, s].astype(F32), p, n, cf.at[:, s])[0])
            zz = z[0] * _sigmoid(z[0]) * z[1]
            acc = acc + _mm(zz, wd[gs, :])
        y2[...] = acc
        x2[...] = x1r[...] + _rms(acc, g4r[...])

    row = lambda n: pl.BlockSpec((tl, n), lambda i: (i, 0))
    full = lambda a: pl.BlockSpec(a.shape, lambda i: (0, 0))
    return pl.pallas_call(
        body, name="ffn_down", grid=(nt,),
        in_specs=[row(2 * DFF), pl.BlockSpec((HALO16, 2 * DFF), _prev_blk(tl, 0, HALO16)),
                  pl.BlockSpec((HALO16, 2 * DFF), _next_blk(tl, L, 0, HALO16)),
                  full(conv_ffn), full(w_down), full(g4), row(D)],
        out_specs=[row(D), row(D)],
        out_shape=[jax.ShapeDtypeStruct((L, D), F32), jax.ShapeDtypeStruct((L, D), F32)],
        compiler_params=_cp("parallel"),
    )(U, U, U, conv_ffn, w_down, g4, x1)


def loss_head(y, target, tl):
    L = y.shape[0]

    def body(yr, tr, dy, ls):
        @pl.when(pl.program_id(0) == 0)
        def _():
            ls[...] = jnp.zeros_like(ls)

        err = yr[...] - tr[...]
        dy[...] = err * (1.0 / D)
        ls[...] += (0.5 / D) * jnp.sum(err * err)

    row = pl.BlockSpec((tl, D), lambda i: (i, 0))
    return pl.pallas_call(
        body, name="loss_head", grid=(L // tl,), in_specs=[row, row],
        out_specs=[row, pl.BlockSpec((8, 128), lambda i: (0, 0))],
        out_shape=[jax.ShapeDtypeStruct((L, D), F32), jax.ShapeDtypeStruct((8, 128), F32)],
        compiler_params=_cp("arbitrary"),
    )(y, target)


def rms_bwd_pre(dout, y, g, tl):
    L = y.shape[0]

    def body(dr, yr, gr, dy, dg):
        @pl.when(pl.program_id(0) == 0)
        def _():
            dg[...] = jnp.zeros_like(dg)

        a, b = _rms_bwd(dr[...], yr[...], gr[...])
        dy[...] = a.astype(BF16)
        dg[...] += b

    row = pl.BlockSpec((tl, D), lambda i: (i, 0))
    vec = pl.BlockSpec((1, D), lambda i: (0, 0))
    return pl.pallas_call(
        body, name="rms_bwd_pre", grid=(L // tl,), in_specs=[row, row, vec], out_specs=[row, vec],
        out_shape=[jax.ShapeDtypeStruct((L, D), BF16), jax.ShapeDtypeStruct((1, D), F32)],
        compiler_params=_cp("arbitrary"),
    )(dout, y, g)


def ffn_bwd1(dy2, U, conv_ffn, w_down, tl):
    L = U.shape[0]
    nt = L // tl

    def body(dy, ug, uv, ugp, ugn, uvp, uvn, cg, cv, wd, dug, duv, zr, dcg, dcv):
        i = pl.program_id(1)

        @pl.when(i == 0)
        def _():
            dcg[...] = jnp.zeros_like(dcg)
            dcv[...] = jnp.zeros_like(dcv)

        gp, gn = _halo_rows(ugp, ugn, i, nt - 1)
        vp, vn = _halo_rows(uvp, uvn, i, nt - 1)
        ug32 = ug[...].astype(F32)
        uv32 = uv[...].astype(F32)
        a, a_m1, a_p1 = _conv3(ug32, gp, gn, cg)
        b, b_m1, b_p1 = _conv3(uv32, vp, vn, cv)
        sg = _sigmoid(a)
        silu = a * sg
        zr[...] = (silu * b).astype(BF16)
        dz = _mm_nt(dy[...], wd[...])
        dval = dz * silu
        dgate = dz * b * (sg * (1.0 + a * (1.0 - sg)))
        dug[...] = dgate.astype(BF16)
        duv[...] = dval.astype(BF16)
        for k, (sa, sb) in enumerate(((a_m1, b_m1), (ug32, uv32), (a_p1, b_p1))):
            dcg[k:k + 1, :] += jnp.sum(dgate * sa, axis=0, keepdims=True)
            dcv[k:k + 1, :] += jnp.sum(dval * sb, axis=0, keepdims=True)

    tile = lambda off: pl.BlockSpec((tl, WFF), lambda j, i: (i, off + j))
    prev = lambda off: pl.BlockSpec((HALO16, WFF), lambda j, i: (_prev_row_blk(i, tl, HALO16), off + j))
    nxt = lambda off: pl.BlockSpec((HALO16, WFF), lambda j, i: (_next_row_blk(i, tl, L, HALO16), off + j))
    cw = lambda off: pl.BlockSpec((3, WFF), lambda j, i: (0, off + j))
    acc = pl.BlockSpec((8, WFF), lambda j, i: (0, j))
    return pl.pallas_call(
        body, name="ffn_bwd1", grid=(NFF, nt),
        in_specs=[pl.BlockSpec((tl, D), lambda j, i: (i, 0)), tile(0), tile(NFF), prev(0), nxt(0), prev(NFF), nxt(NFF),
                  cw(0), cw(NFF), pl.BlockSpec((WFF, D), lambda j, i: (j, 0))],
        out_specs=[tile(0), tile(0), tile(0), acc, acc],
        out_shape=[jax.ShapeDtypeStruct((L, DFF), BF16), jax.ShapeDtypeStruct((L, DFF), BF16),
                   jax.ShapeDtypeStruct((L, DFF), BF16), jax.ShapeDtypeStruct((8, DFF), F32),
                   jax.ShapeDtypeStruct((8, DFF), F32)],
        compiler_params=_cp("parallel", "arbitrary"),
    )(dy2, U, U, U, U, U, U, conv_ffn, conv_ffn, w_down)


def ffn_bwd2(du_g, du_v, conv_ffn, w_up, x1, dres, g3, tl):
    L = x1.shape[0]
    nt = L // tl

    def body(dg_, dv_, dgp, dgn, dvp, dvn, cg, cv, wg, wv, x1r, drr, g3r, dUg, dUv, dx1, dg3, acc):
        i = pl.program_id(0)
        j = pl.program_id(1)
        gp, gn = _halo_rows(dgp, dgn, i, nt - 1)
        vp, vn = _halo_rows(dvp, dvn, i, nt - 1)
        a = _conv3_t(dg_[...].astype(F32), gp, gn, cg).astype(BF16)
        b = _conv3_t(dv_[...].astype(F32), vp, vn, cv).astype(BF16)
        dUg[...] = a
        dUv[...] = b
        part = _mm_nt(a, wg[...]) + _mm_nt(b, wv[...])

        @pl.when(j == 0)
        def _():
            acc[...] = part

        @pl.when(j > 0)
        def _():
            acc[...] += part

        @pl.when((i == 0) & (j == 0))
        def _():
            dg3[...] = jnp.zeros_like(dg3)

        @pl.when(j == NFF - 1)
        def _():
            dx, dg = _rms_bwd(acc[...], x1r[...], g3r[...])
            dx1[...] = drr[...] + dx
            dg3[...] += dg

    tile = pl.BlockSpec((tl, WFF), lambda i, j: (i, j))
    prev = pl.BlockSpec((HALO16, WFF), lambda i, j: (_prev_row_blk(i, tl, HALO16), j))
    nxt = pl.BlockSpec((HALO16, WFF), lambda i, j: (_next_row_blk(i, tl, L, HALO16), j))
    cw = lambda off: pl.BlockSpec((3, WFF), lambda i, j: (0, off + j))
    ww = lambda off: pl.BlockSpec((None, D, WFF), lambda i, j: (off + j, 0, 0))
    row = pl.BlockSpec((tl, D), lambda i, j: (i, 0))
    vec = pl.BlockSpec((1, D), lambda i, j: (0, 0))
    return pl.pallas_call(
        body, name="ffn_bwd2", grid=(nt, NFF),
        in_specs=[tile, tile, prev, nxt, prev, nxt, cw(0), cw(NFF), ww(0), ww(NFF), row, row, vec],
        out_specs=[tile, tile, row, vec],
        out_shape=[jax.ShapeDtypeStruct((L, DFF), BF16), jax.ShapeDtypeStruct((L, DFF), BF16),
                   jax.ShapeDtypeStruct((L, D), F32), jax.ShapeDtypeStruct((1, D), F32)],
        scratch_shapes=[pltpu.VMEM((tl, D), F32)],
        compiler_params=_cp("arbitrary", "arbitrary"),
    )(du_g, du_v, du_g, du_g, du_v, du_v, conv_ffn, conv_ffn, w_up, w_up, x1, dres, g3)


def matmul_tn(a, b, ta, tn, tl, name, into=None):
    L, Ka = a.shape
    N = b.shape[1]

    def body(ar, br, *rest):
        o = rest[-1]

        @pl.when(pl.program_id(2) == 0)
        def _():
            o[...] = jnp.zeros_like(o)

        o[...] += _mm_tn(ar[...], br[...]).reshape(o.shape)

    in_specs = [pl.BlockSpec((tl, ta), lambda p, q, l: (l, p)), pl.BlockSpec((tl, tn), lambda p, q, l: (l, q))]
    if into is None:
        return pl.pallas_call(
            body, name=name, grid=(Ka // ta, N // tn, L // tl), in_specs=in_specs,
            out_specs=pl.BlockSpec((ta, tn), lambda p, q, l: (p, q)),
            out_shape=jax.ShapeDtypeStruct((Ka, N), F32),
            compiler_params=_cp("parallel", "parallel", "arbitrary"),
        )(a, b)
    buf, blk, idx = into
    return pl.pallas_call(
        body, name=name, grid=(Ka // ta, N // tn, L // tl), in_specs=in_specs + [_ANY],
        out_specs=pl.BlockSpec(blk, lambda p, q, l: idx(p, q)),
        out_shape=jax.ShapeDtypeStruct(buf.shape, F32), input_output_aliases={2: 0},
        compiler_params=_cp("parallel", "parallel", "arbitrary"),
    )(a, b, buf)


def mix_bwd1(dy, w_out, P, o_f, o_b, conv_a, ghn4, tl):
    L = P.shape[0]
    nt = L // tl

    def body(dyr, wo, gb, gc, gv, go, gcp, gvp, gcn, gvn, of, ob, ca, gh, dgb, dcc, dgo, do, dca, dgh):
        i = pl.program_id(0)

        @pl.when(i == 0)
        def _():
            dca[...] = jnp.zeros_like(dca)
            dgh[...] = jnp.zeros_like(dgh)

        dycat = _mm_nt(dyr[...], wo[...])
        dya = dycat[:, 0:DC]
        dyb = dycat[:, DC:D]
        cp, cn = _halo_rows(gcp, gcn, i, nt - 1)
        vp, vn = _halo_rows(gvp, gvn, i, nt - 1)
        c = gc[...] * gv[...]
        cc, c_m1, c_p1 = _conv3(c, cp * vp, cn * vn, ca)
        dgb[...] = dya * cc
        d = dya * gb[...]
        dcc[...] = d
        for k, s in enumerate((c_m1, c, c_p1)):
            dca[k:k + 1, :] += jnp.sum(d * s, axis=0, keepdims=True)
        oh, rs = _headnorm(of[...] + ob[...])
        g = go[...]
        sg = _sigmoid(g)
        silu = g * sg
        dgo[...] = dyb * (oh * gh[...]) * (sg * (1.0 + g * (1.0 - sg)))
        don = dyb * silu
        t = jnp.sum(don * oh, axis=0, keepdims=True)
        dgh[0:1, :] += t[:, 0:HV] + t[:, HV:2 * HV] + t[:, 2 * HV:3 * HV] + t[:, 3 * HV:4 * HV]
        doh = don * gh[...]
        parts = []
        for h in range(NH):
            hs = slice(h * HV, (h + 1) * HV)
            parts.append(rs[h] * (doh[:, hs] - oh[:, hs] * jnp.mean(doh[:, hs] * oh[:, hs], axis=-1, keepdims=True)))
        do[...] = jnp.concatenate(parts, axis=1)

    t = lambda cb: pl.BlockSpec((tl, DC), lambda i: (i, cb))
    hp = lambda cb: pl.BlockSpec((8, DC), _prev_blk(tl, cb))
    hn = lambda cb: pl.BlockSpec((8, DC), _next_blk(tl, L, cb))
    row = lambda n: pl.BlockSpec((tl, n), lambda i: (i, 0))
    full = lambda a: pl.BlockSpec(a.shape, lambda i: (0, 0))
    f32o = lambda n: jax.ShapeDtypeStruct((L, n), F32)
    return pl.pallas_call(
        body, name="mix_bwd1", grid=(nt,),
        in_specs=[row(D), full(w_out), t(CB_GB), t(CB_GC), t(CB_GV), t(CB_GO), hp(CB_GC), hp(CB_GV), hn(CB_GC), hn(CB_GV),
                  row(DG), row(DG), full(conv_a), full(ghn4)],
        out_specs=[row(DC), row(DC), row(DG), row(DG), pl.BlockSpec((8, DC), lambda i: (0, 0)),
                   pl.BlockSpec((8, HV), lambda i: (0, 0))],
        out_shape=[f32o(DC), f32o(DC), f32o(DG), f32o(DG), jax.ShapeDtypeStruct((8, DC), F32),
                   jax.ShapeDtypeStruct((8, HV), F32)],
        compiler_params=_cp("arbitrary"),
    )(dy, w_out, P, P, P, P, P, P, P, P, o_f, o_b, conv_a, ghn4)


def _gla_chunk_bwd(q, k, v, a, do, st16, g_ref, m, rev):
    p = _gla_prep(q, k, a, m, rev)
    g = g_ref[...]
    dob = _tile4(do) * m["bdo"]
    dv = _mm_tn(p["sc"], dob) + _mm_nt(p["kh"], g)
    dsc = jnp.where(m["triu4"] if rev else m["tril4"], _mm_nt(dob, v), 0.0)
    r1 = _mm(dsc, p["kt"]) * m["bdq"]
    dqt = r1[0:CH] + r1[CH:2 * CH] + r1[2 * CH:3 * CH] + r1[3 * CH:4 * CH] + _mm(do, st16)
    dkt = _mm_tn(dsc, p["qs"])
    dkh = _mm(v, g)
    dd = jnp.sum(g * st16.astype(F32), axis=0, keepdims=True)
    g_ref[...] = g * p["dec"] + _mm_tn(do, p["qt"]) * m["bds"]
    kk = dkh * p["kh"]
    dcum = dqt * p["qt"] - dkt * p["kt"] - kk
    dtot = jnp.sum(kk, axis=0, keepdims=True) + dd * p["dec"]
    da = _mm_tri(m["tril"] if rev else m["triu"], dcum) + dtot
    dq = dqt * p["e"] * QSCALE
    dk = dkt * p["einv"] + dkh * p["eout"]
    return dq, dk, dv, da


def gla_bwd(P, do, sf, sb, gcat, gbias, tl):
    L = P.shape[0]
    nb = L // tl
    nc = tl // CH

    def body(qf, kf, vf, lf, dof, sfr, qb, kb, vb, lb, dob, sbr, gc_ref, bs_ref,
             dqf, dkf, dvf, daf, dqb, dkb, dvb, dab, gf, gbk, af, ab):
        @pl.when(pl.program_id(0) == 0)
        def _():
            gf[...] = jnp.zeros_like(gf)
            gbk[...] = jnp.zeros_like(gbk)

        af[...] = _logsig(_mm(lf[...], gc_ref[:, 0:DK]) + bs_ref[:, 0:DK]) * GATE_NORM
        ab[...] = _logsig(_mm(lb[...], gc_ref[:, DK:2 * DK]) + bs_ref[:, DK:2 * DK]) * GATE_NORM
        m = _gla_masks()

        def chunk(c, carry):
            cf = nc - 1 - c
            rows = pl.ds(pl.multiple_of(cf * CH, CH), CH)
            dq, dk, dv, da = _gla_chunk_bwd(qf[rows, :], kf[rows, :], vf[rows, :], af[rows, :], dof[rows, :],
                                            sfr[cf], gf, m, False)
            dqf[rows, :] = dq
            dkf[rows, :] = dk
            dvf[rows, :] = dv
            daf[rows, :] = da
            rows = pl.ds(pl.multiple_of(c * CH, CH), CH)
            dq, dk, dv, da = _gla_chunk_bwd(qb[rows, :], kb[rows, :], vb[rows, :], ab[rows, :], dob[rows, :],
                                            sbr[c], gbk, m, True)
            dqb[rows, :] = dq
            dkb[rows, :] = dk
            dvb[rows, :] = dv
            dab[rows, :] = da
            return carry

        lax.fori_loop(0, nc, chunk, 0)

    fwd_dir = lambda cb: (lambda i: (nb - 1 - i, cb))
    bwd_dir = lambda cb: (lambda i: (i, cb))

    def side(ix):
        return [pl.BlockSpec((tl, DK), ix(CB_Q)), pl.BlockSpec((tl, DK), ix(CB_K)), pl.BlockSpec((tl, DG), ix(CB_V)),
                pl.BlockSpec((tl, LRW), ix(CB_LR)), pl.BlockSpec((tl, DG), ix(0)),
                pl.BlockSpec((nc, DG, DK), lambda i: (ix(0)(i)[0], 0, 0))]

    def outs(ix):
        return [pl.BlockSpec((tl, DK), ix(0)), pl.BlockSpec((tl, DK), ix(0)), pl.BlockSpec((tl, DG), ix(0)),
                pl.BlockSpec((tl, DK), ix(0))]

    o_shape = [jax.ShapeDtypeStruct((L, DK), F32), jax.ShapeDtypeStruct((L, DK), F32),
               jax.ShapeDtypeStruct((L, DG), F32), jax.ShapeDtypeStruct((L, DK), F32)]
    return pl.pallas_call(
        body, name="gla_bwd", grid=(nb,),
        in_specs=side(fwd_dir) + side(bwd_dir) + [pl.BlockSpec((LRW, 2 * DK), lambda i: (0, 0)),
                                                  pl.BlockSpec((1, 2 * DK), lambda i: (0, 0))],
        out_specs=outs(fwd_dir) + outs(bwd_dir),
        out_shape=o_shape + o_shape,
        scratch_shapes=[pltpu.VMEM((DG, DK), F32), pltpu.VMEM((DG, DK), F32),
                        pltpu.VMEM((tl, DK), F32), pltpu.VMEM((tl, DK), F32)],
        compiler_params=_cp("arbitrary"),
    )(P, P, P, P, do, sf, P, P, P, P, do, sb, gcat, gbias)


def mix_bwd2(dgb, dcc, dgo, gl, P, conv_a, gcat, gbias, w_in, x, dres, g1, tl):
    L = P.shape[0]
    nt = L // tl

    def body(dgbr, dccr, dccp, dccn, dgor, dqf, dkf, dvf, daf, dqb, dkb, dvb, dab, gc, gv, lr, ca, gcr, bsr, wi,
             xr, drr, g1r, dP, dx, dg1, dgcat, dbias):
        i = pl.program_id(0)

        @pl.when(i == 0)
        def _():
            dg1[...] = jnp.zeros_like(dg1)
            dgcat[...] = jnp.zeros_like(dgcat)
            dbias[...] = jnp.zeros_like(dbias)

        p, n = _halo_rows(dccp, dccn, i, nt - 1)
        dc = _conv3_t(dccr[...], p, n, ca)
        pre = _mm(lr[...], gcr[...]) + bsr[...]
        da = jnp.concatenate([daf[...], dab[...]], axis=1)
        dpre = da * GATE_NORM * (1.0 - _sigmoid(pre))
        dpre16 = dpre.astype(BF16)
        dP[:, 0:DC] = dgbr[...].astype(BF16)
        dP[:, DC:2 * DC] = (dc * gv[...]).astype(BF16)
        dP[:, 2 * DC:3 * DC] = (dc * gc[...]).astype(BF16)
        dP[:, 1536:1792] = (dqf[...] + dqb[...]).astype(BF16)
        dP[:, 1792:2048] = (dkf[...] + dkb[...]).astype(BF16)
        dP[:, 2048:2560] = (dvf[...] + dvb[...]).astype(BF16)
        dP[:, 2560:3072] = dgor[...].astype(BF16)
        dP[:, 3072:3200] = _mm_nt(dpre16, gcr[...]).astype(BF16)
        dgcat[...] += _mm_tn(lr[...], dpre16)
        dbias[0:1, :] += jnp.sum(dpre, axis=0, keepdims=True)
        dh, dg = _rms_bwd(_mm_nt(dP[...], wi[...]), xr[...], g1r[...])
        dx[...] = drr[...] + dh
        dg1[...] += dg

    row = lambda n: pl.BlockSpec((tl, n), lambda i: (i, 0))
    t = lambda w, cb: pl.BlockSpec((tl, w), lambda i: (i, cb))
    full = lambda a: pl.BlockSpec(a.shape, lambda i: (0, 0))
    return pl.pallas_call(
        body, name="mix_bwd2", grid=(nt,),
        in_specs=[row(DC), row(DC), pl.BlockSpec((8, DC), _prev_blk(tl, 0)), pl.BlockSpec((8, DC), _next_blk(tl, L, 0)),
                  row(DG), row(DK), row(DK), row(DG), row(DK), row(DK), row(DK), row(DG), row(DK),
                  t(DC, CB_GC), t(DC, CB_GV), t(LRW, CB_LR), full(conv_a), full(gcat), full(gbias), full(w_in),
                  row(D), row(D), full(g1)],
        out_specs=[row(DINP), row(D), pl.BlockSpec((1, D), lambda i: (0, 0)), pl.BlockSpec((LRW, 2 * DK), lambda i: (0, 0)),
                   pl.BlockSpec((8, 2 * DK), lambda i: (0, 0))],
        out_shape=[jax.ShapeDtypeStruct((L, DINP), BF16), jax.ShapeDtypeStruct((L, D), F32),
                   jax.ShapeDtypeStruct((1, D), F32), jax.ShapeDtypeStruct((LRW, 2 * DK), F32),
                   jax.ShapeDtypeStruct((8, 2 * DK), F32)],
        compiler_params=_cp("arbitrary"),
    )(dgb, dcc, dcc, dcc, dgo, *gl, P, P, P, conv_a, gcat, gbias, w_in, x, dres, g1)


def _row_tile(rows, cols):
    if rows * cols * 4 <= 2 * 1024 * 1024:
        return rows
    best = 8
    for t in range(8, rows, 8):
        if rows % t == 0 and t * cols * 4 <= 2 * 1024 * 1024:
            best = t
    return best


def adamw(w, g, m, v, name):
    shape = w.shape
    cols = shape[-1]
    w2, g2, m2, v2 = (a.reshape(-1, cols) for a in (w, g, m, v))
    rows = w2.shape[0]
    tr = _row_tile(rows, cols)

    def body(wr, gr, mr, vr, dl, nm, nv):
        gg = gr[...]
        mm = B1 * mr[...] + (1.0 - B1) * gg
        vv = B2 * vr[...] + (1.0 - B2) * (gg * gg)
        m_hat = mm / (1.0 - B1 ** STEP)
        v_hat = vv / (1.0 - B2 ** STEP)
        dl[...] = -LR * (m_hat / (jnp.sqrt(v_hat) + AEPS) + WD * wr[...])
        nm[...] = mm
        nv[...] = vv

    blk = pl.BlockSpec((tr, cols), lambda i: (i, 0))
    o = jax.ShapeDtypeStruct((rows, cols), F32)
    d, nm, nv = pl.pallas_call(
        body, name=name, grid=(rows // tr,), in_specs=[blk] * 4, out_specs=[blk] * 3, out_shape=[o, o, o],
        compiler_params=_cp("parallel"),
    )(w2, g2, m2, v2)
    return d.reshape(shape), nm.reshape(shape), nv.reshape(shape)


def _place():
    return lax.axis_index("x"), lax.axis_index("y"), lax.axis_index("c")


def allgather8(v, name):
    mp, n = v.shape

    def body(x_ref, out_ref, send_sems, recv_sems, local_sem):
        x, y, c = _place()
        me, sibling = (x, y, c), (x, y, 1 - c)
        chips = [(1 - x, y), (x, 1 - y), (1 - x, 1 - y)]

        def rows(px, py, pc):
            return out_ref.at[pl.ds((4 * px + 2 * py + pc) * mp, mp), :]

        def copy(k, block, to, src=None):
            return pltpu.make_async_remote_copy(
                src_ref=rows(*block) if src is None else src, dst_ref=rows(*block),
                send_sem=send_sems.at[k], recv_sem=recv_sems.at[k], device_id=to, device_id_type=MESH)

        mine = pltpu.make_async_copy(x_ref, rows(*me), local_sem)
        mine.start()
        first = [copy(0, me, sibling, src=x_ref)]
        first += [copy(1 + j, me, (*chip, c), src=x_ref) for j, chip in enumerate(chips)]
        for cp in first:
            cp.start()
        passed = [copy(4 + j, (*chip, c), sibling) for j, chip in enumerate(chips)]
        for j, chip in enumerate(chips):
            copy(1 + j, (*chip, c), me).wait_recv()
            passed[j].start()
        copy(0, sibling, me).wait_recv()
        for j, chip in enumerate(chips):
            copy(4 + j, (*chip, 1 - c), me).wait_recv()
        for cp in first + passed:
            cp.wait_send()
        mine.wait()

    return pl.pallas_call(
        body, name=name, out_shape=jax.ShapeDtypeStruct((8 * mp, n), v.dtype),
        in_specs=[pl.BlockSpec(memory_space=pltpu.VMEM)], out_specs=pl.BlockSpec(memory_space=pltpu.VMEM),
        scratch_shapes=[pltpu.SemaphoreType.DMA((7,)), pltpu.SemaphoreType.DMA((7,)), pltpu.SemaphoreType.DMA],
        compiler_params=pltpu.CompilerParams(vmem_limit_bytes=VMEM_LIMIT),
    )(v)


def sum8(v, mp):
    def body(x_ref, o_ref):
        acc = x_ref[0:mp, :]
        for d in range(1, 8):
            acc = acc + x_ref[d * mp:(d + 1) * mp, :]
        o_ref[...] = acc

    return pl.pallas_call(body, name="sum8", out_shape=jax.ShapeDtypeStruct((mp, v.shape[1]), F32),
                          compiler_params=pltpu.CompilerParams(vmem_limit_bytes=VMEM_LIMIT))(v)


_ANY = pl.BlockSpec(memory_space=pl.ANY)


def _row_half(ref, lead, h):
    hr = ref.shape[-2] // 2
    return ref.at[(*lead, pl.ds(h * hr, hr), slice(None))]


def allgather_weights(slots):
    n = len(slots)

    def body(*refs):
        s_refs, o_refs, (send_sems, recv_sems) = refs[:n], refs[n:2 * n], refs[2 * n:]
        x, y, c = _place()
        me = 2 * x + y
        sibling = (x, y, 1 - c)
        chips = [(1 - x, y), (x, 1 - y), (1 - x, 1 - y)]

        def half(ref, slot, h):
            return _row_half(ref, (slot, slice(None)), h)

        def copy(k, src, dst, to):
            return pltpu.make_async_remote_copy(src_ref=src, dst_ref=dst, send_sem=send_sems.at[k],
                                                recv_sem=recv_sems.at[k], device_id=to, device_id_type=MESH)

        first = [copy(6 * a + k, half(s_refs[a], me, c), half(o_refs[a], me, c), (px, py, c))
                 for k, (px, py) in enumerate(chips) for a in range(n)]
        for cp in first:
            cp.start()
        passed = []
        for k, (px, py) in enumerate(chips):
            for a in range(n):
                got = half(o_refs[a], 2 * px + py, c)
                copy(6 * a + k, half(s_refs[a], me, c), got, (px, py, c)).wait_recv()
                cp = copy(6 * a + 3 + k, got, got, sibling)
                cp.start()
                passed.append(cp)
        for k, (px, py) in enumerate(chips):
            for a in range(n):
                got = half(o_refs[a], 2 * px + py, 1 - c)
                copy(6 * a + 3 + k, got, got, sibling).wait_recv()
        for cp in first + passed:
            cp.wait_send()

    return pl.pallas_call(
        body, name="allgather_weights", out_shape=[jax.ShapeDtypeStruct(s.shape, s.dtype) for s in slots],
        in_specs=[_ANY] * n, out_specs=[_ANY] * n, input_output_aliases={a: a for a in range(n)},
        scratch_shapes=[pltpu.SemaphoreType.DMA((6 * n,)), pltpu.SemaphoreType.DMA((6 * n,))],
    )(*slots)


_HBM = pl.BlockSpec(memory_space=pltpu.HBM)
_SEM = pl.BlockSpec(memory_space=pltpu.SEMAPHORE)
_EFFECT = pltpu.SideEffectType.DATAFLOW_SIDE_EFFECTING


def gather_start(slots, name):
    n = len(slots)

    def body(*refs):
        s_refs, send_sems, recv_sems, token = refs[:n], refs[n], refs[n + 1], refs[-1]
        x, y, c = _place()
        me = 2 * x + y
        for k, (px, py) in enumerate([(1 - x, y), (x, 1 - y), (1 - x, 1 - y)]):
            for a in range(n):
                pltpu.make_async_remote_copy(
                    src_ref=s_refs[a].at[me], dst_ref=s_refs[a].at[me], send_sem=send_sems.at[3 * a + k],
                    recv_sem=recv_sems.at[3 * a + k], device_id=(px, py, c), device_id_type=MESH).start()
        token[...] = jnp.zeros_like(token)

    out = pl.pallas_call(
        body, name=name,
        out_shape=(pltpu.SemaphoreType.DMA((3 * n,)), pltpu.SemaphoreType.DMA((3 * n,)),
                   *[pltpu.HBM(s.shape, s.dtype) for s in slots], jax.ShapeDtypeStruct((8, 128), F32)),
        in_specs=[_HBM] * n, out_specs=(_SEM, _SEM, *[_HBM] * n, pl.BlockSpec(memory_space=pltpu.VMEM)),
        input_output_aliases={a: 2 + a for a in range(n)},
        compiler_params=pltpu.CompilerParams(has_side_effects=_EFFECT),
    )(*[pltpu.with_memory_space_constraint(s, pltpu.HBM) for s in slots])
    return out[0], out[1], list(out[2:2 + n]), out[-1]


def gather_wait(send_sems, recv_sems, slots, after, name):
    n = len(slots)

    def body(*refs):
        s_refs, ssem, rsem = refs[:n], refs[n], refs[n + 1]
        x, y, c = _place()
        me = 2 * x + y
        for k, (px, py) in enumerate([(1 - x, y), (x, 1 - y), (1 - x, 1 - y)]):
            for a in range(n):
                cp = pltpu.make_async_remote_copy(
                    src_ref=s_refs[a].at[me], dst_ref=s_refs[a].at[2 * px + py], send_sem=ssem.at[3 * a + k],
                    recv_sem=rsem.at[3 * a + k], device_id=(px, py, c), device_id_type=MESH)
                cp.wait_send()
                cp.wait_recv()

    return pl.pallas_call(
        body, name=name, out_shape=[pltpu.HBM(s.shape, s.dtype) for s in slots],
        in_specs=[_HBM] * n + [_SEM, _SEM, _ANY], out_specs=[_HBM] * n,
        input_output_aliases={a: a for a in range(n)},
        compiler_params=pltpu.CompilerParams(has_side_effects=_EFFECT),
    )(*slots, send_sems, recv_sems, after)


def rs_sibling_halves(gs):
    n = len(gs)

    def body(*refs):
        g_refs, r_refs, (send_sems, recv_sems) = refs[:n], refs[n:2 * n], refs[2 * n:]
        x, y, c = _place()
        cps = [pltpu.make_async_remote_copy(
            src_ref=_row_half(g_refs[a], (slice(None), slice(None)), 1 - c), dst_ref=r_refs[a],
            send_sem=send_sems.at[a], recv_sem=recv_sems.at[a], device_id=(x, y, 1 - c), device_id_type=MESH)
            for a in range(n)]
        for cp in cps:
            cp.start()
        for cp in cps:
            cp.wait()

    return pl.pallas_call(
        body, name="rs_sibling_halves",
        out_shape=[jax.ShapeDtypeStruct((*g.shape[:2], g.shape[2] // 2, g.shape[3]), F32) for g in gs],
        in_specs=[_ANY] * n, out_specs=[_ANY] * n,
        scratch_shapes=[pltpu.SemaphoreType.DMA((n,)), pltpu.SemaphoreType.DMA((n,))],
    )(*gs)


def rs_chipsum16(g, recv1, cidx, name):
    nl, hr, cols = recv1.shape[1:]

    def body(c_ref, g_ref, r_ref, o_ref):
        o_ref[...] = (g_ref[...] + r_ref[...]).astype(BF16)

    blk = (1, 1, hr, cols)
    return pl.pallas_call(
        body, name=name, out_shape=jax.ShapeDtypeStruct(recv1.shape, BF16),
        grid_spec=pltpu.PrefetchScalarGridSpec(
            num_scalar_prefetch=1, grid=(4, nl),
            in_specs=[pl.BlockSpec(blk, lambda j, l, c: (j, l, c[0], 0)), pl.BlockSpec(blk, lambda j, l, c: (j, l, 0, 0))],
            out_specs=pl.BlockSpec(blk, lambda j, l, c: (j, l, 0, 0))),
        compiler_params=_cp("parallel", "parallel"),
    )(cidx, g, recv1)


def rs_exchange_chips(cs):
    n = len(cs)

    def body(*refs):
        s_refs, r_refs, (send_sems, recv_sems) = refs[:n], refs[n:2 * n], refs[2 * n:]
        x, y, c = _place()
        chips = [(1 - x, y), (x, 1 - y), (1 - x, 1 - y)]
        cps = [pltpu.make_async_remote_copy(
            src_ref=s_refs[a].at[2 * px + py], dst_ref=r_refs[a].at[k], send_sem=send_sems.at[3 * a + k],
            recv_sem=recv_sems.at[3 * a + k], device_id=(px, py, c), device_id_type=MESH)
            for k, (px, py) in enumerate(chips) for a in range(n)]
        for cp in cps:
            cp.start()
        for cp in cps:
            cp.wait()

    return pl.pallas_call(
        body, name="rs_exchange_chips", out_shape=[jax.ShapeDtypeStruct((3, *s.shape[1:]), BF16) for s in cs],
        in_specs=[_ANY] * n, out_specs=[_ANY] * n,
        scratch_shapes=[pltpu.SemaphoreType.DMA((3 * n,)), pltpu.SemaphoreType.DMA((3 * n,))],
    )(*cs)


def rs_final_sum(g, recv1, recv2, idx, name):
    nl, hr, cols = recv1.shape[1:]

    def body(i_ref, g_ref, r1_ref, r2_ref, o_ref):
        acc = g_ref[0, 0] + r1_ref[0, 0]
        for k in range(3):
            acc = acc + r2_ref[k, 0].astype(F32)
        o_ref[0] = acc

    blk = (1, 1, hr, cols)
    return pl.pallas_call(
        body, name=name, out_shape=jax.ShapeDtypeStruct((nl, 2 * hr, cols), F32),
        grid_spec=pltpu.PrefetchScalarGridSpec(
            num_scalar_prefetch=1, grid=(nl,),
            in_specs=[pl.BlockSpec(blk, lambda l, ix: (ix[0], l, ix[1], 0)), pl.BlockSpec(blk, lambda l, ix: (ix[0], l, 0, 0)),
                      pl.BlockSpec((3, 1, hr, cols), lambda l, ix: (0, l, 0, 0))],
            out_specs=pl.BlockSpec((1, hr, cols), lambda l, ix: (l, ix[1], 0))),
        compiler_params=_cp("parallel"),
    )(idx, g, recv1, recv2)


def rs_share_halves(fulls):
    n = len(fulls)

    def body(*refs):
        h_refs, o_refs, (send_sems, recv_sems) = refs[:n], refs[n:2 * n], refs[2 * n:]
        x, y, c = _place()
        sibling = (x, y, 1 - c)

        def copy(a, h):
            return pltpu.make_async_remote_copy(
                src_ref=_row_half(h_refs[a], (slice(None),), h), dst_ref=_row_half(o_refs[a], (slice(None),), h),
                send_sem=send_sems.at[a], recv_sem=recv_sems.at[a], device_id=sibling, device_id_type=MESH)

        for a in range(n):
            copy(a, c).start()
        for a in range(n):
            copy(a, c).wait_send()
            copy(a, 1 - c).wait_recv()

    return pl.pallas_call(
        body, name="rs_share_halves", out_shape=[jax.ShapeDtypeStruct(f.shape, F32) for f in fulls],
        in_specs=[_ANY] * n, out_specs=[_ANY] * n, input_output_aliases={a: a for a in range(n)},
        scratch_shapes=[pltpu.SemaphoreType.DMA((n,)), pltpu.SemaphoreType.DMA((n,))],
    )(*fulls)


def _own_slot(shard, chip, dtype):
    return lax.dynamic_update_slice(lax.empty((4, *shard.shape), dtype), shard.astype(dtype)[None],
                                    (chip,) + (0,) * shard.ndim)


def kernel(x, norm_mix_pre, norm_mix_post, norm_ffn_pre, norm_ffn_post, w_in, conv_a, gate_up_fwd, gate_bias_fwd, gate_up_bwd, gate_bias_bwd, gla_head_norm, w_out, w_up, conv_ffn, w_down, loss_target, m_norm_mix_pre, m_norm_mix_post, m_norm_ffn_pre, m_norm_ffn_post, m_w_in, m_conv_a, m_gate_up_fwd, m_gate_bias_fwd, m_gate_up_bwd, m_gate_bias_bwd, m_gla_head_norm, m_w_out, m_w_up, m_conv_ffn, m_w_down, v_norm_mix_pre, v_norm_mix_post, v_norm_ffn_pre, v_norm_ffn_post, v_w_in, v_conv_a, v_gate_up_fwd, v_gate_bias_fwd, v_gate_up_bwd, v_gate_bias_bwd, v_gla_head_norm, v_w_out, v_w_up, v_conv_ffn, v_w_down):
    L = x.shape[1]
    xi, yi, ci = _place()
    chip = 2 * xi + yi
    tl_gla, tl_mix, tl_ffn = min(L, TL_GLA), min(L, TL_MIX), min(L, TL_FFN)

    big_w = (w_in, w_out, w_up, w_down)
    started = [gather_start([_own_slot(w[l], chip, BF16) for w in big_w], f"gather_start_{l}") for l in range(1, DEPTH)]
    tokens = [s[3] for s in started]
    gathered = [[a[:, 0] for a in allgather_weights([_own_slot(w[0:1], chip, BF16) for w in big_w])]]

    def layer_weights(bufs):
        a_in, a_out, a_up, a_down = bufs
        w_in_l = jnp.pad(jnp.concatenate([a_in[j] for j in range(4)], axis=1), ((0, 0), (0, DINP - DIN)))
        return w_in_l, a_out.reshape(D, D), a_up, a_down.reshape(DFF, D)

    small = jnp.concatenate([conv_a.reshape(-1), gate_up_fwd.reshape(-1), gate_up_bwd.reshape(-1), conv_ffn.reshape(-1)])
    ms = small.shape[0] // 128
    sg = allgather8(small.reshape(ms, 128), "allgather_small_weights").reshape(4, 2, ms * 128)[:, 0]

    def small_full(off, shape):
        n = shape[0] * shape[1] * shape[2]
        return jnp.concatenate([sg[j, off:off + n].reshape(shape) for j in range(4)], axis=2)

    o1 = DEPTH * 3 * 128
    o2 = o1 + DEPTH * RK * 64
    o3 = o2 + DEPTH * RK * 64
    conv_a_f = small_full(0, (DEPTH, 3, 128))
    gup_f = small_full(o1, (DEPTH, RK, 64))
    gup_b = small_full(o2, (DEPTH, RK, 64))
    conv_ffn_f = small_full(o3, (DEPTH, 3, 1408))

    def gcat_of(l):
        g = jnp.zeros((LRW, 2 * DK), F32)
        g = g.at[0:RK, 0:DK].set(gup_f[l]).at[RK:2 * RK, DK:2 * DK].set(gup_b[l])
        return g.astype(BF16)

    gcats = [gcat_of(l) for l in range(DEPTH)]
    gbiases = [jnp.concatenate([gate_bias_fwd[l], gate_bias_bwd[l]])[None, :] for l in range(DEPTH)]
    ghn4s = [jnp.tile(gla_head_norm[l], NH)[None, :] for l in range(DEPTH)]

    xc = x.reshape(L, D)
    saved = []
    W_in, W_out, W_up, W_down = [], [], [], []
    for l in range(DEPTH):
        if l > 0:
            ssem, rsem, bufs, _ = started[l - 1]
            gathered.append(gather_wait(ssem, rsem, bufs, xc, f"gather_wait_{l}"))
        for lst, w in zip((W_in, W_out, W_up, W_down), layer_weights(gathered[l])):
            lst.append(w)
        P, h1 = rms_matmul(xc, norm_mix_pre[l][None, :], W_in[l], 640, "proj_in", after=tokens if l == 0 else ())
        o_f, o_b, sf, sb = gla_fwd(P, gcats[l], gbiases[l], tl_gla)
        ycat, y, x1 = mix_out(P, o_f, o_b, conv_a_f[l], ghn4s[l], W_out[l], norm_mix_post[l][None, :], xc, tl_mix)
        U, h2 = rms_matmul(x1, norm_ffn_pre[l][None, :], W_up[l], WFF, "proj_up", n_out=2 * DFF, out_dtype=BF16,
                           w_spec=pl.BlockSpec((None, D, WFF), lambda i, j: (j, 0, 0)))
        y2, x2 = ffn_down(U, conv_ffn_f[l], W_down[l], norm_ffn_post[l][None, :], x1, tl_ffn)
        saved.append(dict(x=xc, h1=h1, P=P, o_f=o_f, o_b=o_b, sf=sf, sb=sb, ycat=ycat, y=y, x1=x1, h2=h2, U=U, y2=y2))
        xc = x2

    dx, loss_blk = loss_head(xc, loss_target.reshape(L, D), tl_mix)

    g_out = lax.empty((4, DEPTH, D // 4, D), F32)
    g_up = lax.empty((4, DEPTH, D, WFF), F32)
    g_down = lax.empty((4, DEPTH, DFF // 4, D), F32)
    g_in_layers = [None] * DEPTH
    grads = [None] * DEPTH
    tl_dw = min(L, 1024)
    for l in reversed(range(DEPTH)):
        s = saved[l]
        dy2, dg4 = rms_bwd_pre(dx, s["y2"], norm_ffn_post[l][None, :], tl_mix)
        du_g, du_v, z, dcf_g, dcf_v = ffn_bwd1(dy2, s["U"], conv_ffn_f[l], W_down[l], tl_ffn)
        g_down = matmul_tn(z, dy2, DFF // 2, D, tl_dw, "dw_down",
                           into=(g_down, (2, None, DFF // 4, D), lambda p, q, l=l: (p, l, 0, 0)))
        dU_g, dU_v, dx1, dg3 = ffn_bwd2(du_g, du_v, conv_ffn_f[l], W_up[l], s["x1"], dx, norm_ffn_pre[l][None, :],
                                        min(L, TL_FFN2))
        g_up = matmul_tn(s["h2"], dU_g, D, WFF, tl_dw, "dw_up_gate",
                         into=(g_up, (None, None, D, WFF), lambda p, q, l=l: (q, l, 0, 0)))
        g_up = matmul_tn(s["h2"], dU_v, D, WFF, tl_dw, "dw_up_val",
                         into=(g_up, (None, None, D, WFF), lambda p, q, l=l: (NFF + q, l, 0, 0)))
        dy, dg2 = rms_bwd_pre(dx1, s["y"], norm_mix_post[l][None, :], tl_mix)
        dgb, dcc, dgo, do, dca, dghn = mix_bwd1(dy, W_out[l], s["P"], s["o_f"], s["o_b"], conv_a_f[l], ghn4s[l], tl_mix)
        g_out = matmul_tn(s["ycat"], dy, D, D, tl_dw, "dw_out",
                          into=(g_out, (4, None, D // 4, D), lambda p, q, l=l: (0, l, 0, 0)))
        gl = gla_bwd(s["P"], do, s["sf"], s["sb"], gcats[l], gbiases[l], tl_gla)
        dP, dx, dg1, dgcat, dbias = mix_bwd2(dgb, dcc, dgo, gl, s["P"], conv_a_f[l], gcats[l], gbiases[l], W_in[l],
                                             s["x"], dx1, norm_mix_pre[l][None, :], tl_mix)
        dW_in = matmul_tn(s["h1"], dP, D, 640, tl_dw, "dw_in")
        g_in_layers[l] = jnp.stack([dW_in[:, (DIN // 4) * j:(DIN // 4) * (j + 1)] for j in range(4)])
        grads[l] = dict(
            norm_mix_pre=dg1[0], norm_mix_post=dg2[0], norm_ffn_pre=dg3[0], norm_ffn_post=dg4[0],
            conv_a=dca[0:3], gate_up_fwd=dgcat[0:RK, 0:DK], gate_bias_fwd=dbias[0, 0:DK],
            gate_up_bwd=dgcat[RK:2 * RK, DK:2 * DK], gate_bias_bwd=dbias[0, DK:2 * DK], gla_head_norm=dghn[0],
            conv_ffn=jnp.concatenate([dcf_g[0:3], dcf_v[0:3]], axis=1))

    g_in = jnp.stack(g_in_layers, axis=1)
    G = {k: jnp.stack([grads[l][k] for l in range(DEPTH)]) for k in grads[0]}

    small_names = ["norm_mix_pre", "norm_mix_post", "norm_ffn_pre", "norm_ffn_post", "conv_a", "gate_up_fwd",
                   "gate_bias_fwd", "gate_up_bwd", "gate_bias_bwd", "gla_head_norm", "conv_ffn"]
    flat = jnp.concatenate([G[k].reshape(-1) for k in small_names] + [loss_blk[0, 0:1]])
    n_small = flat.shape[0]
    mp = -(-n_small // 1024) * 8
    flat = jnp.pad(flat, (0, mp * 128 - n_small)).reshape(mp, 128)
    tot = sum8(allgather8(flat, "allgather_small_grads"), mp).reshape(-1)
    gsm = {}
    o = 0
    for k in small_names:
        n = G[k].size
        gsm[k] = tot[o:o + n].reshape(G[k].shape)
        o += n
    loss = tot[o]

    def my_cols(a, width):
        return lax.dynamic_slice_in_dim(a, chip * width, width, axis=2)

    gsm["conv_a"] = my_cols(gsm["conv_a"], 128)
    gsm["gate_up_fwd"] = my_cols(gsm["gate_up_fwd"], 64)
    gsm["gate_up_bwd"] = my_cols(gsm["gate_up_bwd"], 64)
    gsm["conv_ffn"] = my_cols(gsm["conv_ffn"], 1408)

    big = ("w_in", "w_out", "w_up", "w_down")
    gs = [g_in, g_out, g_up, g_down]
    recv1 = rs_sibling_halves(gs)
    cidx = jnp.reshape(ci, (1,)).astype(jnp.int32)
    cs16 = [rs_chipsum16(g, r, cidx, "rs_chipsum16_" + k) for g, r, k in zip(gs, recv1, big)]
    recv2 = rs_exchange_chips(cs16)
    idx = jnp.stack([chip, ci]).astype(jnp.int32)
    halves = [rs_final_sum(g, r1, r2, idx, "rs_final_sum_" + k) for g, r1, r2, k in zip(gs, recv1, recv2, big)]
    gsm.update(zip(big, rs_share_halves(halves)))

    names = ["norm_mix_pre", "norm_mix_post", "norm_ffn_pre", "norm_ffn_post", "w_in", "conv_a", "gate_up_fwd",
             "gate_bias_fwd", "gate_up_bwd", "gate_bias_bwd", "gla_head_norm", "w_out", "w_up", "conv_ffn", "w_down"]
    w = dict(norm_mix_pre=norm_mix_pre, norm_mix_post=norm_mix_post, norm_ffn_pre=norm_ffn_pre, norm_ffn_post=norm_ffn_post,
             w_in=w_in, conv_a=conv_a, gate_up_fwd=gate_up_fwd, gate_bias_fwd=gate_bias_fwd, gate_up_bwd=gate_up_bwd,
             gate_bias_bwd=gate_bias_bwd, gla_head_norm=gla_head_norm, w_out=w_out, w_up=w_up, conv_ffn=conv_ffn, w_down=w_down)
    m = dict(norm_mix_pre=m_norm_mix_pre, norm_mix_post=m_norm_mix_post, norm_ffn_pre=m_norm_ffn_pre, norm_ffn_post=m_norm_ffn_post,
             w_in=m_w_in, conv_a=m_conv_a, gate_up_fwd=m_gate_up_fwd, gate_bias_fwd=m_gate_bias_fwd, gate_up_bwd=m_gate_up_bwd,
             gate_bias_bwd=m_gate_bias_bwd, gla_head_norm=m_gla_head_norm, w_out=m_w_out, w_up=m_w_up, conv_ffn=m_conv_ffn, w_down=m_w_down)
    v = dict(norm_mix_pre=v_norm_mix_pre, norm_mix_post=v_norm_mix_post, norm_ffn_pre=v_norm_ffn_pre, norm_ffn_post=v_norm_ffn_post,
             w_in=v_w_in, conv_a=v_conv_a, gate_up_fwd=v_gate_up_fwd, gate_bias_fwd=v_gate_bias_fwd, gate_up_bwd=v_gate_up_bwd,
             gate_bias_bwd=v_gate_bias_bwd, gla_head_norm=v_gla_head_norm, w_out=v_w_out, w_up=v_w_up, conv_ffn=v_conv_ffn, w_down=v_w_down)
    upd = {k: adamw(w[k], gsm[k], m[k], v[k], "adamw_" + k) for k in names}
    return (loss, dx.reshape(1, L, D), *[gsm[k] for k in names], *[upd[k][0] for k in names],
            *[upd[k][1] for k in names], *[upd[k][2] for k in names])
```

```python
import functools

import jax
import jax.numpy as jnp
from jax import lax
from jax.experimental import pallas as pl
from jax.experimental.pallas import tpu as pltpu

F32 = jnp.float32
BF16 = jnp.bfloat16
MXU_DTYPE = jnp.bfloat16
MESH = pl.DeviceIdType.MESH

D = 1024
DC = 512
DG = 512
NH = 4
HV = 128
HK = 64
DK = 256
RK = 16
CH = 64
DFF = 2816
DIN = 3104
DINP = 3200
LRW = 128
DEPTH = 4
EPS = 1e-6
QSCALE = HK ** -0.5
GATE_NORM = 1.0 / 16.0
CB_GB, CB_GC, CB_GV, CB_GO = 0, 1, 2, 5
CB_Q, CB_K = 6, 7
CB_V = 4
CB_LR = 24
LR = 0.001
B1 = 0.9
B2 = 0.999
AEPS = 1e-08
WD = 0.01
STEP = 10
TM_PROJ = 1024
TL_GLA = 512
TL_MIX = 256
TL_FFN = 256
TL_FFN2 = 512
VMEM_LIMIT = 56 * 1024 * 1024


def _cp(*sem):
    return pltpu.CompilerParams(dimension_semantics=sem if sem else None, vmem_limit_bytes=VMEM_LIMIT)


def _mm(a, b):
    return jnp.dot(a.astype(MXU_DTYPE), b.astype(MXU_DTYPE), preferred_element_type=F32)


def _mm_nt(a, b):
    return lax.dot_general(a.astype(MXU_DTYPE), b.astype(MXU_DTYPE), (((1,), (1,)), ((), ())),
                           preferred_element_type=F32)


def _mm_tn(a, b):
    return lax.dot_general(a.astype(MXU_DTYPE), b.astype(MXU_DTYPE), (((0,), (0,)), ((), ())),
                           preferred_element_type=F32)


def _mm_tri(tri, b):
    return jnp.dot(tri, b, precision=lax.Precision.HIGHEST, preferred_element_type=F32)


def _rms(x, g):
    r = lax.rsqrt(jnp.mean(x * x, axis=-1, keepdims=True) + EPS)
    return x * r * g


def _rms_bwd(dout, y, g):
    r = lax.rsqrt(jnp.mean(y * y, axis=-1, keepdims=True) + EPS)
    yh = y * r
    dyh = dout * g
    dy = r * (dyh - yh * jnp.mean(dyh * yh, axis=-1, keepdims=True))
    dg = jnp.sum(dout * yh, axis=0, keepdims=True)
    return dy, dg


def _sigmoid(x):
    return 0.5 * jnp.tanh(0.5 * x) + 0.5


def _logsig(x):
    return jnp.minimum(x, 0.0) - jnp.log1p(jnp.exp(-jnp.abs(x)))


def _shifts(x, p8, n8):
    n = x.shape[0]
    xe = jnp.concatenate([p8, x, n8], axis=0)
    return pltpu.roll(xe, 1, 0)[8:8 + n], pltpu.roll(xe, n + 15, 0)[8:8 + n]


def _halo_rows(prev_ref, next_ref, i, last):
    hr = prev_ref.shape[0]
    p = jnp.where(i == 0, 0.0, prev_ref[...].astype(F32)[hr - 8:hr, :])
    n = jnp.where(i == last, 0.0, next_ref[...].astype(F32)[0:8, :])
    return p, n


def _conv3(x, xp, xn, w_ref):
    xm1, xp1 = _shifts(x, xp, xn)
    return w_ref[0:1, :] * xm1 + w_ref[1:2, :] * x + w_ref[2:3, :] * xp1, xm1, xp1


def _conv3_t(d, dp, dn, w_ref):
    dm1, dp1 = _shifts(d, dp, dn)
    return w_ref[0:1, :] * dp1 + w_ref[1:2, :] * d + w_ref[2:3, :] * dm1


HALO32 = 8
HALO16 = 16


def _prev_row_blk(i, tl, hr):
    return jnp.maximum(i * (tl // hr) - 1, 0)


def _next_row_blk(i, tl, nrows, hr):
    return jnp.minimum((i + 1) * (tl // hr), nrows // hr - 1)


def _prev_blk(tl, cb, hr=HALO32):
    return lambda i: (_prev_row_blk(i, tl, hr), cb)


def _next_blk(tl, nrows, cb, hr=HALO32):
    return lambda i: (_next_row_blk(i, tl, nrows, hr), cb)


def rms_matmul(x, g, w, tn, name, w_spec=None, n_out=None, out_dtype=F32, after=()):
    L = x.shape[0]
    N = w.shape[1] if n_out is None else n_out
    tm = min(L, TM_PROJ)
    if w_spec is None:
        w_spec = pl.BlockSpec((D, tn), lambda i, j: (0, j))

    def body(x_ref, g_ref, w_ref, *rest):
        o_ref, h_ref = rest[-2:]

        @pl.when(pl.program_id(1) == 0)
        def _():
            h_ref[...] = _rms(x_ref[...], g_ref[...]).astype(BF16)

        o_ref[...] = _mm(h_ref[...], w_ref[...]).astype(out_dtype)

    return pl.pallas_call(
        body, name=name, grid=(L // tm, N // tn),
        in_specs=[pl.BlockSpec((tm, D), lambda i, j: (i, 0)), pl.BlockSpec((1, D), lambda i, j: (0, 0)), w_spec]
        + [_ANY] * len(after),
        out_specs=[pl.BlockSpec((tm, tn), lambda i, j: (i, j)), pl.BlockSpec((tm, D), lambda i, j: (i, 0))],
        out_shape=[jax.ShapeDtypeStruct((L, N), out_dtype), jax.ShapeDtypeStruct((L, D), BF16)],
        compiler_params=_cp("parallel", "arbitrary"),
    )(x, g, w, *after)


def _gla_masks():
    def blk(shape, rdiv, cdiv):
        r = lax.broadcasted_iota(jnp.int32, shape, 0) // rdiv
        c = lax.broadcasted_iota(jnp.int32, shape, 1) // cdiv
        return (r == c).astype(F32)

    r = lax.broadcasted_iota(jnp.int32, (CH, CH), 0)
    c = lax.broadcasted_iota(jnp.int32, (CH, CH), 1)
    r4 = lax.broadcasted_iota(jnp.int32, (NH * CH, CH), 0) % CH
    c4 = lax.broadcasted_iota(jnp.int32, (NH * CH, CH), 1)
    return dict(
        bdq=blk((NH * CH, DK), CH, HK),
        bdo=blk((NH * CH, DG), CH, HV),
        bds=blk((DG, DK), HV, HK),
        tril=(r >= c).astype(F32), triu=(r <= c).astype(F32),
        tril4=r4 >= c4, triu4=r4 <= c4,
    )


def _tile4(x):
    return jnp.concatenate([x, x, x, x], axis=0)


def _gla_prep(q, k, a, m, rev):
    cum = _mm_tri(m["triu"] if rev else m["tril"], a)
    tot = jnp.sum(a, axis=0, keepdims=True)
    e = jnp.exp(cum)
    einv = jnp.exp(-cum)
    eout = jnp.exp(tot - cum)
    qt = q * QSCALE * e
    kt = k * einv
    kh = k * eout
    qs = _tile4(qt) * m["bdq"]
    sc = _mm_nt(qs, kt)
    sc = jnp.where(m["triu4"] if rev else m["tril4"], sc, 0.0)
    return dict(e=e, einv=einv, eout=eout, dec=jnp.exp(tot), qt=qt, kt=kt, kh=kh, qs=qs, sc=sc)


def _gla_chunk_fwd(q, k, v, a, st_ref, m, rev):
    p = _gla_prep(q, k, a, m, rev)
    r = _mm(p["sc"], v)
    o_intra = jnp.concatenate([r[h * CH:(h + 1) * CH, h * HV:(h + 1) * HV] for h in range(NH)], axis=1)
    st = st_ref[...]
    st16 = st.astype(BF16)
    o = o_intra + _mm_nt(p["qt"], st16)
    st_ref[...] = st * p["dec"] + _mm_tn(v, p["kh"]) * m["bds"]
    return o, st16


def gla_fwd(P, gcat, gbias, tl):
    L = P.shape[0]
    nb = L // tl
    nc = tl // CH

    def body(qf, kf, vf, lf, qb, kb, vb, lb, gc_ref, bs_ref, of, ob, sf, sb, stf, stb, af, ab):
        @pl.when(pl.program_id(0) == 0)
        def _():
            stf[...] = jnp.zeros_like(stf)
            stb[...] = jnp.zeros_like(stb)

        af[...] = _logsig(_mm(lf[...], gc_ref[:, 0:DK]) + bs_ref[:, 0:DK]) * GATE_NORM
        ab[...] = _logsig(_mm(lb[...], gc_ref[:, DK:2 * DK]) + bs_ref[:, DK:2 * DK]) * GATE_NORM
        m = _gla_masks()

        def chunk(c, carry):
            rows = pl.ds(pl.multiple_of(c * CH, CH), CH)
            o, st = _gla_chunk_fwd(qf[rows, :], kf[rows, :], vf[rows, :], af[rows, :], stf, m, False)
            of[rows, :] = o
            sf[c] = st
            cb = nc - 1 - c
            rows = pl.ds(pl.multiple_of(cb * CH, CH), CH)
            o, st = _gla_chunk_fwd(qb[rows, :], kb[rows, :], vb[rows, :], ab[rows, :], stb, m, True)
            ob[rows, :] = o
            sb[cb] = st
            return carry

        lax.fori_loop(0, nc, chunk, 0, unroll=2)

    fw = lambda cb: (lambda i: (i, cb))
    bw = lambda cb: (lambda i: (nb - 1 - i, cb))
    return pl.pallas_call(
        body, name="gla_fwd", grid=(nb,),
        in_specs=[pl.BlockSpec((tl, DK), fw(CB_Q)), pl.BlockSpec((tl, DK), fw(CB_K)), pl.BlockSpec((tl, DG), fw(CB_V)),
                  pl.BlockSpec((tl, LRW), fw(CB_LR)),
                  pl.BlockSpec((tl, DK), bw(CB_Q)), pl.BlockSpec((tl, DK), bw(CB_K)), pl.BlockSpec((tl, DG), bw(CB_V)),
                  pl.BlockSpec((tl, LRW), bw(CB_LR)),
                  pl.BlockSpec((LRW, 2 * DK), lambda i: (0, 0)), pl.BlockSpec((1, 2 * DK), lambda i: (0, 0))],
        out_specs=[pl.BlockSpec((tl, DG), lambda i: (i, 0)), pl.BlockSpec((tl, DG), lambda i: (nb - 1 - i, 0)),
                   pl.BlockSpec((nc, DG, DK), lambda i: (i, 0, 0)), pl.BlockSpec((nc, DG, DK), lambda i: (nb - 1 - i, 0, 0))],
        out_shape=[jax.ShapeDtypeStruct((L, DG), F32), jax.ShapeDtypeStruct((L, DG), F32),
                   jax.ShapeDtypeStruct((L // CH, DG, DK), BF16), jax.ShapeDtypeStruct((L // CH, DG, DK), BF16)],
        scratch_shapes=[pltpu.VMEM((DG, DK), F32), pltpu.VMEM((DG, DK), F32),
                        pltpu.VMEM((tl, DK), F32), pltpu.VMEM((tl, DK), F32)],
        compiler_params=_cp("arbitrary"),
    )(P, P, P, P, P, P, P, P, gcat, gbias)


def _headnorm(o):
    oh, rs = [], []
    for h in range(NH):
        oo = o[:, h * HV:(h + 1) * HV]
        r = lax.rsqrt(jnp.mean(oo * oo, axis=-1, keepdims=True) + EPS)
        oh.append(oo * r)
        rs.append(r)
    return jnp.concatenate(oh, axis=1), rs


def mix_out(P, o_f, o_b, conv_a, ghn4, w_out, g2, x, tl):
    L = P.shape[0]
    nt = L // tl

    def body(gb, gc, gv, go, gcp, gvp, gcn, gvn, of, ob, ca, gh, wo, g2r, xr, ycat, yr, x1):
        i = pl.program_id(0)
        cp, cn = _halo_rows(gcp, gcn, i, nt - 1)
        vp, vn = _halo_rows(gvp, gvn, i, nt - 1)
        c = gc[...] * gv[...]
        cc, _, _ = _conv3(c, cp * vp, cn * vn, ca)
        ya = gb[...] * cc
        oh, _ = _headnorm(of[...] + ob[...])
        g = go[...]
        yb = g * _sigmoid(g) * (oh * gh[...])
        yc = jnp.concatenate([ya, yb], axis=1).astype(BF16)
        ycat[...] = yc
        y = _mm(yc, wo[...])
        yr[...] = y
        x1[...] = xr[...] + _rms(y, g2r[...])

    t = lambda cb: pl.BlockSpec((tl, DC), lambda i: (i, cb))
    hp = lambda cb: pl.BlockSpec((8, DC), _prev_blk(tl, cb))
    hn = lambda cb: pl.BlockSpec((8, DC), _next_blk(tl, L, cb))
    row = lambda n: pl.BlockSpec((tl, n), lambda i: (i, 0))
    full = lambda a: pl.BlockSpec(a.shape, lambda i: (0, 0))
    return pl.pallas_call(
        body, name="mix_out", grid=(nt,),
        in_specs=[t(CB_GB), t(CB_GC), t(CB_GV), t(CB_GO), hp(CB_GC), hp(CB_GV), hn(CB_GC), hn(CB_GV),
                  row(DG), row(DG), full(conv_a), full(ghn4), full(w_out), full(g2), row(D)],
        out_specs=[row(D), row(D), row(D)],
        out_shape=[jax.ShapeDtypeStruct((L, D), BF16), jax.ShapeDtypeStruct((L, D), F32),
                   jax.ShapeDtypeStruct((L, D), F32)],
        compiler_params=_cp("parallel"),
    )(P, P, P, P, P, P, P, P, o_f, o_b, conv_a, ghn4, w_out, g2, x)


NFF = 2
WFF = DFF // NFF


def ffn_down(U, conv_ffn, w_down, g4, x1, tl):
    L = U.shape[0]
    nt = L // tl

    def body(u, up, un, cf, wd, g4r, x1r, y2, x2):
        i = pl.program_id(0)
        acc = jnp.zeros((tl, D), F32)
        for j in range(NFF):
            gs = slice(j * WFF, (j + 1) * WFF)
            vs = slice(DFF + j * WFF, DFF + (j + 1) * WFF)
            z = []
            for s in (gs, vs):
                p, n = _halo_rows(up.at[:, s], un.at[:, s], i, nt - 1)
                z.append(_conv3(u[:, s].astype(F32), p, n, cf.at[:, s])[0])
            zz = z[0] * _sigmoid(z[0]) * z[1]
            acc = acc + _mm(zz, wd[gs, :])
        y2[...] = acc
        x2[...] = x1r[...] + _rms(acc, g4r[...])

    row = lambda n: pl.BlockSpec((tl, n), lambda i: (i, 0))
    full = lambda a: pl.BlockSpec(a.shape, lambda i: (0, 0))
    return pl.pallas_call(
        body, name="ffn_down", grid=(nt,),
        in_specs=[row(2 * DFF), pl.BlockSpec((HALO16, 2 * DFF), _prev_blk(tl, 0, HALO16)),
                  pl.BlockSpec((HALO16, 2 * DFF), _next_blk(tl, L, 0, HALO16)),
                  full(conv_ffn), full(w_down), full(g4), row(D)],
        out_specs=[row(D), row(D)],
        out_shape=[jax.ShapeDtypeStruct((L, D), F32), jax.ShapeDtypeStruct((L, D), F32)],
        compiler_params=_cp("parallel"),
    )(U, U, U, conv_ffn, w_down, g4, x1)


def loss_head(y, target, tl):
    L = y.shape[0]

    def body(yr, tr, dy, ls):
        @pl.when(pl.program_id(0) == 0)
        def _():
            ls[...] = jnp.zeros_like(ls)

        err = yr[...] - tr[...]
        dy[...] = err * (1.0 / D)
        ls[...] += (0.5 / D) * jnp.sum(err * err)

    row = pl.BlockSpec((tl, D), lambda i: (i, 0))
    return pl.pallas_call(
        body, name="loss_head", grid=(L // tl,), in_specs=[row, row],
        out_specs=[row, pl.BlockSpec((8, 128), lambda i: (0, 0))],
        out_shape=[jax.ShapeDtypeStruct((L, D), F32), jax.ShapeDtypeStruct((8, 128), F32)],
        compiler_params=_cp("arbitrary"),
    )(y, target)


def rms_bwd_pre(dout, y, g, tl, after=()):
    L = y.shape[0]

    def body(dr, yr, gr, *rest):
        dy, dg = rest[-2:]

        @pl.when(pl.program_id(0) == 0)
        def _():
            dg[...] = jnp.zeros_like(dg)

        a, b = _rms_bwd(dr[...], yr[...], gr[...])
        dy[...] = a.astype(BF16)
        dg[...] += b

    row = pl.BlockSpec((tl, D), lambda i: (i, 0))
    vec = pl.BlockSpec((1, D), lambda i: (0, 0))
    return pl.pallas_call(
        body, name="rms_bwd_pre", grid=(L // tl,), in_specs=[row, row, vec] + [_ANY] * len(after), out_specs=[row, vec],
        out_shape=[jax.ShapeDtypeStruct((L, D), BF16), jax.ShapeDtypeStruct((1, D), F32)],
        compiler_params=_cp("arbitrary"),
    )(dout, y, g, *after)


def ffn_bwd1(dy2, U, conv_ffn, w_down, tl):
    L = U.shape[0]
    nt = L // tl

    def body(dy, ug, uv, ugp, ugn, uvp, uvn, cg, cv, wd, dug, duv, zr, dcg, dcv):
        i = pl.program_id(1)

        @pl.when(i == 0)
        def _():
            dcg[...] = jnp.zeros_like(dcg)
            dcv[...] = jnp.zeros_like(dcv)

        gp, gn = _halo_rows(ugp, ugn, i, nt - 1)
        vp, vn = _halo_rows(uvp, uvn, i, nt - 1)
        ug32 = ug[...].astype(F32)
        uv32 = uv[...].astype(F32)
        a, a_m1, a_p1 = _conv3(ug32, gp, gn, cg)
        b, b_m1, b_p1 = _conv3(uv32, vp, vn, cv)
        sg = _sigmoid(a)
        silu = a * sg
        zr[...] = (silu * b).astype(BF16)
        dz = _mm_nt(dy[...], wd[...])
        dval = dz * silu
        dgate = dz * b * (sg * (1.0 + a * (1.0 - sg)))
        dug[...] = dgate.astype(BF16)
        duv[...] = dval.astype(BF16)
        for k, (sa, sb) in enumerate(((a_m1, b_m1), (ug32, uv32), (a_p1, b_p1))):
            dcg[k:k + 1, :] += jnp.sum(dgate * sa, axis=0, keepdims=True)
            dcv[k:k + 1, :] += jnp.sum(dval * sb, axis=0, keepdims=True)

    tile = lambda off: pl.BlockSpec((tl, WFF), lambda j, i: (i, off + j))
    prev = lambda off: pl.BlockSpec((HALO16, WFF), lambda j, i: (_prev_row_blk(i, tl, HALO16), off + j))
    nxt = lambda off: pl.BlockSpec((HALO16, WFF), lambda j, i: (_next_row_blk(i, tl, L, HALO16), off + j))
    cw = lambda off: pl.BlockSpec((3, WFF), lambda j, i: (0, off + j))
    acc = pl.BlockSpec((8, WFF), lambda j, i: (0, j))
    return pl.pallas_call(
        body, name="ffn_bwd1", grid=(NFF, nt),
        in_specs=[pl.BlockSpec((tl, D), lambda j, i: (i, 0)), tile(0), tile(NFF), prev(0), nxt(0), prev(NFF), nxt(NFF),
                  cw(0), cw(NFF), pl.BlockSpec((WFF, D), lambda j, i: (j, 0))],
        out_specs=[tile(0), tile(0), tile(0), acc, acc],
        out_shape=[jax.ShapeDtypeStruct((L, DFF), BF16), jax.ShapeDtypeStruct((L, DFF), BF16),
                   jax.ShapeDtypeStruct((L, DFF), BF16), jax.ShapeDtypeStruct((8, DFF), F32),
                   jax.ShapeDtypeStruct((8, DFF), F32)],
        compiler_params=_cp("parallel", "arbitrary"),
    )(dy2, U, U, U, U, U, U, conv_ffn, conv_ffn, w_down)


def ffn_bwd2(du_g, du_v, conv_ffn, w_up, x1, dres, g3, tl):
    L = x1.shape[0]
    nt = L // tl

    def body(dg_, dv_, dgp, dgn, dvp, dvn, cg, cv, wg, wv, x1r, drr, g3r, dUg, dUv, dx1, dg3, acc):
        i = pl.program_id(0)
        j = pl.program_id(1)
        gp, gn = _halo_rows(dgp, dgn, i, nt - 1)
        vp, vn = _halo_rows(dvp, dvn, i, nt - 1)
        a = _conv3_t(dg_[...].astype(F32), gp, gn, cg).astype(BF16)
        b = _conv3_t(dv_[...].astype(F32), vp, vn, cv).astype(BF16)
        dUg[...] = a
        dUv[...] = b
        part = _mm_nt(a, wg[...]) + _mm_nt(b, wv[...])

        @pl.when(j == 0)
        def _():
            acc[...] = part

        @pl.when(j > 0)
        def _():
            acc[...] += part

        @pl.when((i == 0) & (j == 0))
        def _():
            dg3[...] = jnp.zeros_like(dg3)

        @pl.when(j == NFF - 1)
        def _():
            dx, dg = _rms_bwd(acc[...], x1r[...], g3r[...])
            dx1[...] = drr[...] + dx
            dg3[...] += dg

    tile = pl.BlockSpec((tl, WFF), lambda i, j: (i, j))
    prev = pl.BlockSpec((HALO16, WFF), lambda i, j: (_prev_row_blk(i, tl, HALO16), j))
    nxt = pl.BlockSpec((HALO16, WFF), lambda i, j: (_next_row_blk(i, tl, L, HALO16), j))
    cw = lambda off: pl.BlockSpec((3, WFF), lambda i, j: (0, off + j))
    ww = lambda off: pl.BlockSpec((None, D, WFF), lambda i, j: (off + j, 0, 0))
    row = pl.BlockSpec((tl, D), lambda i, j: (i, 0))
    vec = pl.BlockSpec((1, D), lambda i, j: (0, 0))
    return pl.pallas_call(
        body, name="ffn_bwd2", grid=(nt, NFF),
        in_specs=[tile, tile, prev, nxt, prev, nxt, cw(0), cw(NFF), ww(0), ww(NFF), row, row, vec],
        out_specs=[tile, tile, row, vec],
        out_shape=[jax.ShapeDtypeStruct((L, DFF), BF16), jax.ShapeDtypeStruct((L, DFF), BF16),
                   jax.ShapeDtypeStruct((L, D), F32), jax.ShapeDtypeStruct((1, D), F32)],
        scratch_shapes=[pltpu.VMEM((tl, D), F32)],
        compiler_params=_cp("arbitrary", "arbitrary"),
    )(du_g, du_v, du_g, du_g, du_v, du_v, conv_ffn, conv_ffn, w_up, w_up, x1, dres, g3)


def matmul_tn(a, b, ta, tn, tl, name, into=None):
    L, Ka = a.shape
    N = b.shape[1]

    def body(ar, br, *rest):
        o = rest[-1]

        @pl.when(pl.program_id(2) == 0)
        def _():
            o[...] = jnp.zeros_like(o)

        o[...] += _mm_tn(ar[...], br[...]).reshape(o.shape)

    in_specs = [pl.BlockSpec((tl, ta), lambda p, q, l: (l, p)), pl.BlockSpec((tl, tn), lambda p, q, l: (l, q))]
    if into is None:
        return pl.pallas_call(
            body, name=name, grid=(Ka // ta, N // tn, L // tl), in_specs=in_specs,
            out_specs=pl.BlockSpec((ta, tn), lambda p, q, l: (p, q)),
            out_shape=jax.ShapeDtypeStruct((Ka, N), F32),
            compiler_params=_cp("parallel", "parallel", "arbitrary"),
        )(a, b)
    buf, blk, idx = into
    return pl.pallas_call(
        body, name=name, grid=(Ka // ta, N // tn, L // tl), in_specs=in_specs + [_ANY],
        out_specs=pl.BlockSpec(blk, lambda p, q, l: idx(p, q)),
        out_shape=jax.ShapeDtypeStruct(buf.shape, F32), input_output_aliases={2: 0},
        compiler_params=_cp("parallel", "parallel", "arbitrary"),
    )(a, b, buf)


def mix_bwd1(dy, w_out, P, o_f, o_b, conv_a, ghn4, tl):
    L = P.shape[0]
    nt = L // tl

    def body(dyr, wo, gb, gc, gv, go, gcp, gvp, gcn, gvn, of, ob, ca, gh, dgb, dcc, dgo, do, dca, dgh):
        i = pl.program_id(0)

        @pl.when(i == 0)
        def _():
            dca[...] = jnp.zeros_like(dca)
            dgh[...] = jnp.zeros_like(dgh)

        dycat = _mm_nt(dyr[...], wo[...])
        dya = dycat[:, 0:DC]
        dyb = dycat[:, DC:D]
        cp, cn = _halo_rows(gcp, gcn, i, nt - 1)
        vp, vn = _halo_rows(gvp, gvn, i, nt - 1)
        c = gc[...] * gv[...]
        cc, c_m1, c_p1 = _conv3(c, cp * vp, cn * vn, ca)
        dgb[...] = dya * cc
        d = dya * gb[...]
        dcc[...] = d
        for k, s in enumerate((c_m1, c, c_p1)):
            dca[k:k + 1, :] += jnp.sum(d * s, axis=0, keepdims=True)
        oh, rs = _headnorm(of[...] + ob[...])
        g = go[...]
        sg = _sigmoid(g)
        silu = g * sg
        dgo[...] = dyb * (oh * gh[...]) * (sg * (1.0 + g * (1.0 - sg)))
        don = dyb * silu
        t = jnp.sum(don * oh, axis=0, keepdims=True)
        dgh[0:1, :] += t[:, 0:HV] + t[:, HV:2 * HV] + t[:, 2 * HV:3 * HV] + t[:, 3 * HV:4 * HV]
        doh = don * gh[...]
        parts = []
        for h in range(NH):
            hs = slice(h * HV, (h + 1) * HV)
            parts.append(rs[h] * (doh[:, hs] - oh[:, hs] * jnp.mean(doh[:, hs] * oh[:, hs], axis=-1, keepdims=True)))
        do[...] = jnp.concatenate(parts, axis=1)

    t = lambda cb: pl.BlockSpec((tl, DC), lambda i: (i, cb))
    hp = lambda cb: pl.BlockSpec((8, DC), _prev_blk(tl, cb))
    hn = lambda cb: pl.BlockSpec((8, DC), _next_blk(tl, L, cb))
    row = lambda n: pl.BlockSpec((tl, n), lambda i: (i, 0))
    full = lambda a: pl.BlockSpec(a.shape, lambda i: (0, 0))
    f32o = lambda n: jax.ShapeDtypeStruct((L, n), F32)
    return pl.pallas_call(
        body, name="mix_bwd1", grid=(nt,),
        in_specs=[row(D), full(w_out), t(CB_GB), t(CB_GC), t(CB_GV), t(CB_GO), hp(CB_GC), hp(CB_GV), hn(CB_GC), hn(CB_GV),
                  row(DG), row(DG), full(conv_a), full(ghn4)],
        out_specs=[row(DC), row(DC), row(DG), row(DG), pl.BlockSpec((8, DC), lambda i: (0, 0)),
                   pl.BlockSpec((8, HV), lambda i: (0, 0))],
        out_shape=[f32o(DC), f32o(DC), f32o(DG), f32o(DG), jax.ShapeDtypeStruct((8, DC), F32),
                   jax.ShapeDtypeStruct((8, HV), F32)],
        compiler_params=_cp("arbitrary"),
    )(dy, w_out, P, P, P, P, P, P, P, P, o_f, o_b, conv_a, ghn4)


def _gla_chunk_bwd(q, k, v, a, do, st16, g_ref, m, rev):
    p = _gla_prep(q, k, a, m, rev)
    g = g_ref[...]
    dob = _tile4(do) * m["bdo"]
    dv = _mm_tn(p["sc"], dob) + _mm_nt(p["kh"], g)
    dsc = jnp.where(m["triu4"] if rev else m["tril4"], _mm_nt(dob, v), 0.0)
    r1 = _mm(dsc, p["kt"]) * m["bdq"]
    dqt = r1[0:CH] + r1[CH:2 * CH] + r1[2 * CH:3 * CH] + r1[3 * CH:4 * CH] + _mm(do, st16)
    dkt = _mm_tn(dsc, p["qs"])
    dkh = _mm(v, g)
    dd = jnp.sum(g * st16.astype(F32), axis=0, keepdims=True)
    g_ref[...] = g * p["dec"] + _mm_tn(do, p["qt"]) * m["bds"]
    kk = dkh * p["kh"]
    dcum = dqt * p["qt"] - dkt * p["kt"] - kk
    dtot = jnp.sum(kk, axis=0, keepdims=True) + dd * p["dec"]
    da = _mm_tri(m["tril"] if rev else m["triu"], dcum) + dtot
    dq = dqt * p["e"] * QSCALE
    dk = dkt * p["einv"] + dkh * p["eout"]
    return dq, dk, dv, da


def gla_bwd(P, do, sf, sb, gcat, gbias, tl):
    L = P.shape[0]
    nb = L // tl
    nc = tl // CH

    def body(qf, kf, vf, lf, dof, sfr, qb, kb, vb, lb, dob, sbr, gc_ref, bs_ref,
             dqf, dkf, dvf, daf, dqb, dkb, dvb, dab, gf, gbk, af, ab):
        @pl.when(pl.program_id(0) == 0)
        def _():
            gf[...] = jnp.zeros_like(gf)
            gbk[...] = jnp.zeros_like(gbk)

        af[...] = _logsig(_mm(lf[...], gc_ref[:, 0:DK]) + bs_ref[:, 0:DK]) * GATE_NORM
        ab[...] = _logsig(_mm(lb[...], gc_ref[:, DK:2 * DK]) + bs_ref[:, DK:2 * DK]) * GATE_NORM
        m = _gla_masks()

        def chunk(c, carry):
            cf = nc - 1 - c
            rows = pl.ds(pl.multiple_of(cf * CH, CH), CH)
            dq, dk, dv, da = _gla_chunk_bwd(qf[rows, :], kf[rows, :], vf[rows, :], af[rows, :], dof[rows, :],
                                            sfr[cf], gf, m, False)
            dqf[rows, :] = dq
            dkf[rows, :] = dk
            dvf[rows, :] = dv
            daf[rows, :] = da
            rows = pl.ds(pl.multiple_of(c * CH, CH), CH)
            dq, dk, dv, da = _gla_chunk_bwd(qb[rows, :], kb[rows, :], vb[rows, :], ab[rows, :], dob[rows, :],
                                            sbr[c], gbk, m, True)
            dqb[rows, :] = dq
            dkb[rows, :] = dk
            dvb[rows, :] = dv
            dab[rows, :] = da
            return carry

        lax.fori_loop(0, nc, chunk, 0)

    fwd_dir = lambda cb: (lambda i: (nb - 1 - i, cb))
    bwd_dir = lambda cb: (lambda i: (i, cb))

    def side(ix):
        return [pl.BlockSpec((tl, DK), ix(CB_Q)), pl.BlockSpec((tl, DK), ix(CB_K)), pl.BlockSpec((tl, DG), ix(CB_V)),
                pl.BlockSpec((tl, LRW), ix(CB_LR)), pl.BlockSpec((tl, DG), ix(0)),
                pl.BlockSpec((nc, DG, DK), lambda i: (ix(0)(i)[0], 0, 0))]

    def outs(ix):
        return [pl.BlockSpec((tl, DK), ix(0)), pl.BlockSpec((tl, DK), ix(0)), pl.BlockSpec((tl, DG), ix(0)),
                pl.BlockSpec((tl, DK), ix(0))]

    o_shape = [jax.ShapeDtypeStruct((L, DK), F32), jax.ShapeDtypeStruct((L, DK), F32),
               jax.ShapeDtypeStruct((L, DG), F32), jax.ShapeDtypeStruct((L, DK), F32)]
    return pl.pallas_call(
        body, name="gla_bwd", grid=(nb,),
        in_specs=side(fwd_dir) + side(bwd_dir) + [pl.BlockSpec((LRW, 2 * DK), lambda i: (0, 0)),
                                                  pl.BlockSpec((1, 2 * DK), lambda i: (0, 0))],
        out_specs=outs(fwd_dir) + outs(bwd_dir),
        out_shape=o_shape + o_shape,
        scratch_shapes=[pltpu.VMEM((DG, DK), F32), pltpu.VMEM((DG, DK), F32),
                        pltpu.VMEM((tl, DK), F32), pltpu.VMEM((tl, DK), F32)],
        compiler_params=_cp("arbitrary"),
    )(P, P, P, P, do, sf, P, P, P, P, do, sb, gcat, gbias)


def mix_bwd2(dgb, dcc, dgo, gl, P, conv_a, gcat, gbias, w_in, x, dres, g1, tl):
    L = P.shape[0]
    nt = L // tl

    def body(dgbr, dccr, dccp, dccn, dgor, dqf, dkf, dvf, daf, dqb, dkb, dvb, dab, gc, gv, lr, ca, gcr, bsr, wi,
             xr, drr, g1r, dP, dx, dg1, dgcat, dbias):
        i = pl.program_id(0)

        @pl.when(i == 0)
        def _():
            dg1[...] = jnp.zeros_like(dg1)
            dgcat[...] = jnp.zeros_like(dgcat)
            dbias[...] = jnp.zeros_like(dbias)

        p, n = _halo_rows(dccp, dccn, i, nt - 1)
        dc = _conv3_t(dccr[...], p, n, ca)
        pre = _mm(lr[...], gcr[...]) + bsr[...]
        da = jnp.concatenate([daf[...], dab[...]], axis=1)
        dpre = da * GATE_NORM * (1.0 - _sigmoid(pre))
        dpre16 = dpre.astype(BF16)
        dP[:, 0:DC] = dgbr[...].astype(BF16)
        dP[:, DC:2 * DC] = (dc * gv[...]).astype(BF16)
        dP[:, 2 * DC:3 * DC] = (dc * gc[...]).astype(BF16)
        dP[:, 1536:1792] = (dqf[...] + dqb[...]).astype(BF16)
        dP[:, 1792:2048] = (dkf[...] + dkb[...]).astype(BF16)
        dP[:, 2048:2560] = (dvf[...] + dvb[...]).astype(BF16)
        dP[:, 2560:3072] = dgor[...].astype(BF16)
        dP[:, 3072:3200] = _mm_nt(dpre16, gcr[...]).astype(BF16)
        dgcat[...] += _mm_tn(lr[...], dpre16)
        dbias[0:1, :] += jnp.sum(dpre, axis=0, keepdims=True)
        dh, dg = _rms_bwd(_mm_nt(dP[...], wi[...]), xr[...], g1r[...])
        dx[...] = drr[...] + dh
        dg1[...] += dg

    row = lambda n: pl.BlockSpec((tl, n), lambda i: (i, 0))
    t = lambda w, cb: pl.BlockSpec((tl, w), lambda i: (i, cb))
    full = lambda a: pl.BlockSpec(a.shape, lambda i: (0, 0))
    return pl.pallas_call(
        body, name="mix_bwd2", grid=(nt,),
        in_specs=[row(DC), row(DC), pl.BlockSpec((8, DC), _prev_blk(tl, 0)), pl.BlockSpec((8, DC), _next_blk(tl, L, 0)),
                  row(DG), row(DK), row(DK), row(DG), row(DK), row(DK), row(DK), row(DG), row(DK),
                  t(DC, CB_GC), t(DC, CB_GV), t(LRW, CB_LR), full(conv_a), full(gcat), full(gbias), full(w_in),
                  row(D), row(D), full(g1)],
        out_specs=[row(DINP), row(D), pl.BlockSpec((1, D), lambda i: (0, 0)), pl.BlockSpec((LRW, 2 * DK), lambda i: (0, 0)),
                   pl.BlockSpec((8, 2 * DK), lambda i: (0, 0))],
        out_shape=[jax.ShapeDtypeStruct((L, DINP), BF16), jax.ShapeDtypeStruct((L, D), F32),
                   jax.ShapeDtypeStruct((1, D), F32), jax.ShapeDtypeStruct((LRW, 2 * DK), F32),
                   jax.ShapeDtypeStruct((8, 2 * DK), F32)],
        compiler_params=_cp("arbitrary"),
    )(dgb, dcc, dcc, dcc, dgo, *gl, P, P, P, conv_a, gcat, gbias, w_in, x, dres, g1)


def _row_tile(rows, cols):
    if rows * cols * 4 <= 2 * 1024 * 1024:
        return rows
    best = 8
    for t in range(8, rows, 8):
        if rows % t == 0 and t * cols * 4 <= 2 * 1024 * 1024:
            best = t
    return best


def adamw(w, g, m, v, name):
    shape = w.shape
    cols = shape[-1]
    w2, g2, m2, v2 = (a.reshape(-1, cols) for a in (w, g, m, v))
    rows = w2.shape[0]
    tr = _row_tile(rows, cols)

    def body(wr, gr, mr, vr, dl, nm, nv):
        gg = gr[...]
        mm = B1 * mr[...] + (1.0 - B1) * gg
        vv = B2 * vr[...] + (1.0 - B2) * (gg * gg)
        m_hat = mm / (1.0 - B1 ** STEP)
        v_hat = vv / (1.0 - B2 ** STEP)
        dl[...] = -LR * (m_hat / (jnp.sqrt(v_hat) + AEPS) + WD * wr[...])
        nm[...] = mm
        nv[...] = vv

    blk = pl.BlockSpec((tr, cols), lambda i: (i, 0))
    o = jax.ShapeDtypeStruct((rows, cols), F32)
    d, nm, nv = pl.pallas_call(
        body, name=name, grid=(rows // tr,), in_specs=[blk] * 4, out_specs=[blk] * 3, out_shape=[o, o, o],
        compiler_params=_cp("parallel"),
    )(w2, g2, m2, v2)
    return d.reshape(shape), nm.reshape(shape), nv.reshape(shape)


def _place():
    return lax.axis_index("x"), lax.axis_index("y"), lax.axis_index("c")


def allgather8(v, name):
    mp, n = v.shape

    def body(x_ref, out_ref, send_sems, recv_sems, local_sem):
        x, y, c = _place()
        me, sibling = (x, y, c), (x, y, 1 - c)
        chips = [(1 - x, y), (x, 1 - y), (1 - x, 1 - y)]

        def rows(px, py, pc):
            return out_ref.at[pl.ds((4 * px + 2 * py + pc) * mp, mp), :]

        def copy(k, block, to, src=None):
            return pltpu.make_async_remote_copy(
                src_ref=rows(*block) if src is None else src, dst_ref=rows(*block),
                send_sem=send_sems.at[k], recv_sem=recv_sems.at[k], device_id=to, device_id_type=MESH)

        mine = pltpu.make_async_copy(x_ref, rows(*me), local_sem)
        mine.start()
        first = [copy(0, me, sibling, src=x_ref)]
        first += [copy(1 + j, me, (*chip, c), src=x_ref) for j, chip in enumerate(chips)]
        for cp in first:
            cp.start()
        passed = [copy(4 + j, (*chip, c), sibling) for j, chip in enumerate(chips)]
        for j, chip in enumerate(chips):
            copy(1 + j, (*chip, c), me).wait_recv()
            passed[j].start()
        copy(0, sibling, me).wait_recv()
        for j, chip in enumerate(chips):
            copy(4 + j, (*chip, 1 - c), me).wait_recv()
        for cp in first + passed:
            cp.wait_send()
        mine.wait()

    return pl.pallas_call(
        body, name=name, out_shape=jax.ShapeDtypeStruct((8 * mp, n), v.dtype),
        in_specs=[pl.BlockSpec(memory_space=pltpu.VMEM)], out_specs=pl.BlockSpec(memory_space=pltpu.VMEM),
        scratch_shapes=[pltpu.SemaphoreType.DMA((7,)), pltpu.SemaphoreType.DMA((7,)), pltpu.SemaphoreType.DMA],
        compiler_params=pltpu.CompilerParams(vmem_limit_bytes=VMEM_LIMIT),
    )(v)


def sum8(v, mp):
    def body(x_ref, o_ref):
        acc = x_ref[0:mp, :]
        for d in range(1, 8):
            acc = acc + x_ref[d * mp:(d + 1) * mp, :]
        o_ref[...] = acc

    return pl.pallas_call(body, name="sum8", out_shape=jax.ShapeDtypeStruct((mp, v.shape[1]), F32),
                          compiler_params=pltpu.CompilerParams(vmem_limit_bytes=VMEM_LIMIT))(v)


_ANY = pl.BlockSpec(memory_space=pl.ANY)


def _row_half(ref, lead, h):
    hr = ref.shape[-2] // 2
    return ref.at[(*lead, pl.ds(h * hr, hr), slice(None))]


def allgather_weights(slots):
    n = len(slots)

    def body(*refs):
        s_refs, o_refs, (send_sems, recv_sems) = refs[:n], refs[n:2 * n], refs[2 * n:]
        x, y, c = _place()
        me = 2 * x + y
        sibling = (x, y, 1 - c)
        chips = [(1 - x, y), (x, 1 - y), (1 - x, 1 - y)]

        def half(ref, slot, h):
            return _row_half(ref, (slot, slice(None)), h)

        def copy(k, src, dst, to):
            return pltpu.make_async_remote_copy(src_ref=src, dst_ref=dst, send_sem=send_sems.at[k],
                                                recv_sem=recv_sems.at[k], device_id=to, device_id_type=MESH)

        first = [copy(6 * a + k, half(s_refs[a], me, c), half(o_refs[a], me, c), (px, py, c))
                 for k, (px, py) in enumerate(chips) for a in range(n)]
        for cp in first:
            cp.start()
        passed = []
        for k, (px, py) in enumerate(chips):
            for a in range(n):
                got = half(o_refs[a], 2 * px + py, c)
                copy(6 * a + k, half(s_refs[a], me, c), got, (px, py, c)).wait_recv()
                cp = copy(6 * a + 3 + k, got, got, sibling)
                cp.start()
                passed.append(cp)
        for k, (px, py) in enumerate(chips):
            for a in range(n):
                got = half(o_refs[a], 2 * px + py, 1 - c)
                copy(6 * a + 3 + k, got, got, sibling).wait_recv()
        for cp in first + passed:
            cp.wait_send()

    return pl.pallas_call(
        body, name="allgather_weights", out_shape=[jax.ShapeDtypeStruct(s.shape, s.dtype) for s in slots],
        in_specs=[_ANY] * n, out_specs=[_ANY] * n, input_output_aliases={a: a for a in range(n)},
        scratch_shapes=[pltpu.SemaphoreType.DMA((6 * n,)), pltpu.SemaphoreType.DMA((6 * n,))],
    )(*slots)


_HBM = pl.BlockSpec(memory_space=pltpu.HBM)
_SEM = pl.BlockSpec(memory_space=pltpu.SEMAPHORE)
_EFFECT = pltpu.SideEffectType.DATAFLOW_SIDE_EFFECTING


def gather_start(slots, name):
    n = len(slots)

    def body(*refs):
        s_refs, send_sems, recv_sems, token = refs[:n], refs[n], refs[n + 1], refs[-1]
        x, y, c = _place()
        me = 2 * x + y
        for k, (px, py) in enumerate([(1 - x, y), (x, 1 - y), (1 - x, 1 - y)]):
            for a in range(n):
                pltpu.make_async_remote_copy(
                    src_ref=s_refs[a].at[me], dst_ref=s_refs[a].at[me], send_sem=send_sems.at[3 * a + k],
                    recv_sem=recv_sems.at[3 * a + k], device_id=(px, py, c), device_id_type=MESH).start()
        token[...] = jnp.zeros_like(token)

    out = pl.pallas_call(
        body, name=name,
        out_shape=(pltpu.SemaphoreType.DMA((3 * n,)), pltpu.SemaphoreType.DMA((3 * n,)),
                   *[pltpu.HBM(s.shape, s.dtype) for s in slots], jax.ShapeDtypeStruct((8, 128), F32)),
        in_specs=[_HBM] * n, out_specs=(_SEM, _SEM, *[_HBM] * n, pl.BlockSpec(memory_space=pltpu.VMEM)),
        input_output_aliases={a: 2 + a for a in range(n)},
        compiler_params=pltpu.CompilerParams(has_side_effects=_EFFECT),
    )(*[pltpu.with_memory_space_constraint(s, pltpu.HBM) for s in slots])
    return out[0], out[1], list(out[2:2 + n]), out[-1]


def gather_wait(send_sems, recv_sems, slots, after, name):
    n = len(slots)

    def body(*refs):
        s_refs, ssem, rsem = refs[:n], refs[n], refs[n + 1]
        x, y, c = _place()
        me = 2 * x + y
        for k, (px, py) in enumerate([(1 - x, y), (x, 1 - y), (1 - x, 1 - y)]):
            for a in range(n):
                cp = pltpu.make_async_remote_copy(
                    src_ref=s_refs[a].at[me], dst_ref=s_refs[a].at[2 * px + py], send_sem=ssem.at[3 * a + k],
                    recv_sem=rsem.at[3 * a + k], device_id=(px, py, c), device_id_type=MESH)
                cp.wait_send()
                cp.wait_recv()

    return pl.pallas_call(
        body, name=name, out_shape=[pltpu.HBM(s.shape, s.dtype) for s in slots],
        in_specs=[_HBM] * n + [_SEM, _SEM, _ANY], out_specs=[_HBM] * n,
        input_output_aliases={a: a for a in range(n)},
        compiler_params=pltpu.CompilerParams(has_side_effects=_EFFECT),
    )(*slots, send_sems, recv_sems, after)


def rs_sibling_halves(gs):
    n = len(gs)

    def body(*refs):
        g_refs, r_refs, (send_sems, recv_sems) = refs[:n], refs[n:2 * n], refs[2 * n:]
        x, y, c = _place()
        cps = [pltpu.make_async_remote_copy(
            src_ref=_row_half(g_refs[a], (slice(None), slice(None)), 1 - c), dst_ref=r_refs[a],
            send_sem=send_sems.at[a], recv_sem=recv_sems.at[a], device_id=(x, y, 1 - c), device_id_type=MESH)
            for a in range(n)]
        for cp in cps:
            cp.start()
        for cp in cps:
            cp.wait()

    return pl.pallas_call(
        body, name="rs_sibling_halves",
        out_shape=[jax.ShapeDtypeStruct((*g.shape[:2], g.shape[2] // 2, g.shape[3]), F32) for g in gs],
        in_specs=[_ANY] * n, out_specs=[_ANY] * n,
        scratch_shapes=[pltpu.SemaphoreType.DMA((n,)), pltpu.SemaphoreType.DMA((n,))],
    )(*gs)


def rs_chipsum16(g, recv1, cidx, name):
    nl, hr, cols = recv1.shape[1:]

    def body(c_ref, g_ref, r_ref, o_ref):
        o_ref[...] = (g_ref[...] + r_ref[...]).astype(BF16)

    blk = (1, 1, hr, cols)
    return pl.pallas_call(
        body, name=name, out_shape=jax.ShapeDtypeStruct(recv1.shape, BF16),
        grid_spec=pltpu.PrefetchScalarGridSpec(
            num_scalar_prefetch=1, grid=(4, nl),
            in_specs=[pl.BlockSpec(blk, lambda j, l, c: (j, l, c[0], 0)), pl.BlockSpec(blk, lambda j, l, c: (j, l, 0, 0))],
            out_specs=pl.BlockSpec(blk, lambda j, l, c: (j, l, 0, 0))),
        compiler_params=_cp("parallel", "parallel"),
    )(cidx, g, recv1)


def rs_exchange_chips(cs):
    n = len(cs)

    def body(*refs):
        s_refs, r_refs, (send_sems, recv_sems) = refs[:n], refs[n:2 * n], refs[2 * n:]
        x, y, c = _place()
        chips = [(1 - x, y), (x, 1 - y), (1 - x, 1 - y)]
        cps = [pltpu.make_async_remote_copy(
            src_ref=s_refs[a].at[2 * px + py], dst_ref=r_refs[a].at[k], send_sem=send_sems.at[3 * a + k],
            recv_sem=recv_sems.at[3 * a + k], device_id=(px, py, c), device_id_type=MESH)
            for k, (px, py) in enumerate(chips) for a in range(n)]
        for cp in cps:
            cp.start()
        for cp in cps:
            cp.wait()

    return pl.pallas_call(
        body, name="rs_exchange_chips", out_shape=[jax.ShapeDtypeStruct((3, *s.shape[1:]), BF16) for s in cs],
        in_specs=[_ANY] * n, out_specs=[_ANY] * n,
        scratch_shapes=[pltpu.SemaphoreType.DMA((3 * n,)), pltpu.SemaphoreType.DMA((3 * n,))],
    )(*cs)


def exchange_start(cs, name):
    n = len(cs)
    lands = [lax.empty((3, *c.shape[1:]), BF16) for c in cs]

    def body(*refs):
        s_refs, l_refs, send_sems, recv_sems, token = refs[:n], refs[n:2 * n], refs[2 * n], refs[2 * n + 1], refs[-1]
        x, y, c = _place()
        for k, (px, py) in enumerate([(1 - x, y), (x, 1 - y), (1 - x, 1 - y)]):
            for a in range(n):
                pltpu.make_async_remote_copy(
                    src_ref=s_refs[a].at[2 * px + py], dst_ref=l_refs[a].at[k], send_sem=send_sems.at[3 * a + k],
                    recv_sem=recv_sems.at[3 * a + k], device_id=(px, py, c), device_id_type=MESH).start()
        token[...] = jnp.zeros_like(token)

    bufs = list(cs) + lands
    out = pl.pallas_call(
        body, name=name,
        out_shape=(pltpu.SemaphoreType.DMA((3 * n,)), pltpu.SemaphoreType.DMA((3 * n,)),
                   *[pltpu.HBM(b.shape, b.dtype) for b in bufs], jax.ShapeDtypeStruct((8, 128), F32)),
        in_specs=[_HBM] * (2 * n), out_specs=(_SEM, _SEM, *[_HBM] * (2 * n), pl.BlockSpec(memory_space=pltpu.VMEM)),
        input_output_aliases={i: 2 + i for i in range(2 * n)},
        compiler_params=pltpu.CompilerParams(has_side_effects=_EFFECT),
    )(*[pltpu.with_memory_space_constraint(b, pltpu.HBM) for b in bufs])
    return out[0], out[1], list(out[2:2 + n]), list(out[2 + n:2 + 2 * n]), out[-1]


def exchange_wait(send_sems, recv_sems, cs, lands, after, name):
    n = len(cs)

    def body(*refs):
        s_refs, l_refs, ssem, rsem = refs[:n], refs[n:2 * n], refs[2 * n], refs[2 * n + 1]
        x, y, c = _place()
        for k, (px, py) in enumerate([(1 - x, y), (x, 1 - y), (1 - x, 1 - y)]):
            for a in range(n):
                cp = pltpu.make_async_remote_copy(
                    src_ref=s_refs[a].at[2 * px + py], dst_ref=l_refs[a].at[k], send_sem=ssem.at[3 * a + k],
                    recv_sem=rsem.at[3 * a + k], device_id=(px, py, c), device_id_type=MESH)
                cp.wait_send()
                cp.wait_recv()

    bufs = list(cs) + list(lands)
    out = pl.pallas_call(
        body, name=name, out_shape=[pltpu.HBM(b.shape, b.dtype) for b in bufs],
        in_specs=[_HBM] * (2 * n) + [_SEM, _SEM, _ANY], out_specs=[_HBM] * (2 * n),
        input_output_aliases={i: i for i in range(2 * n)},
        compiler_params=pltpu.CompilerParams(has_side_effects=_EFFECT),
    )(*bufs, send_sems, recv_sems, after)
    return list(out[n:])


def rs_final_sum(g, recv1, recv2, idx, name):
    nl, hr, cols = recv1.shape[1:]

    def body(i_ref, g_ref, r1_ref, r2_ref, o_ref):
        acc = g_ref[0, 0] + r1_ref[0, 0]
        for k in range(3):
            acc = acc + r2_ref[k, 0].astype(F32)
        o_ref[0] = acc

    blk = (1, 1, hr, cols)
    return pl.pallas_call(
        body, name=name, out_shape=jax.ShapeDtypeStruct((nl, 2 * hr, cols), F32),
        grid_spec=pltpu.PrefetchScalarGridSpec(
            num_scalar_prefetch=1, grid=(nl,),
            in_specs=[pl.BlockSpec(blk, lambda l, ix: (ix[0], l, ix[1], 0)), pl.BlockSpec(blk, lambda l, ix: (ix[0], l, 0, 0)),
                      pl.BlockSpec((3, 1, hr, cols), lambda l, ix: (0, l, 0, 0))],
            out_specs=pl.BlockSpec((1, hr, cols), lambda l, ix: (l, ix[1], 0))),
        compiler_params=_cp("parallel"),
    )(idx, g, recv1, recv2)


def rs_share_halves(fulls):
    n = len(fulls)

    def body(*refs):
        h_refs, o_refs, (send_sems, recv_sems) = refs[:n], refs[n:2 * n], refs[2 * n:]
        x, y, c = _place()
        sibling = (x, y, 1 - c)

        def copy(a, h):
            return pltpu.make_async_remote_copy(
                src_ref=_row_half(h_refs[a], (slice(None),), h), dst_ref=_row_half(o_refs[a], (slice(None),), h),
                send_sem=send_sems.at[a], recv_sem=recv_sems.at[a], device_id=sibling, device_id_type=MESH)

        for a in range(n):
            copy(a, c).start()
        for a in range(n):
            copy(a, c).wait_send()
            copy(a, 1 - c).wait_recv()

    return pl.pallas_call(
        body, name="rs_share_halves", out_shape=[jax.ShapeDtypeStruct(f.shape, F32) for f in fulls],
        in_specs=[_ANY] * n, out_specs=[_ANY] * n, input_output_aliases={a: a for a in range(n)},
        scratch_shapes=[pltpu.SemaphoreType.DMA((n,)), pltpu.SemaphoreType.DMA((n,))],
    )(*fulls)


def _own_slot(shard, chip, dtype):
    return lax.dynamic_update_slice(lax.empty((4, *shard.shape), dtype), shard.astype(dtype)[None],
                                    (chip,) + (0,) * shard.ndim)


def kernel(x, norm_mix_pre, norm_mix_post, norm_ffn_pre, norm_ffn_post, w_in, conv_a, gate_up_fwd, gate_bias_fwd, gate_up_bwd, gate_bias_bwd, gla_head_norm, w_out, w_up, conv_ffn, w_down, loss_target, m_norm_mix_pre, m_norm_mix_post, m_norm_ffn_pre, m_norm_ffn_post, m_w_in, m_conv_a, m_gate_up_fwd, m_gate_bias_fwd, m_gate_up_bwd, m_gate_bias_bwd, m_gla_head_norm, m_w_out, m_w_up, m_conv_ffn, m_w_down, v_norm_mix_pre, v_norm_mix_post, v_norm_ffn_pre, v_norm_ffn_post, v_w_in, v_conv_a, v_gate_up_fwd, v_gate_bias_fwd, v_gate_up_bwd, v_gate_bias_bwd, v_gla_head_norm, v_w_out, v_w_up, v_conv_ffn, v_w_down):
    L = x.shape[1]
    xi, yi, ci = _place()
    chip = 2 * xi + yi
    tl_gla, tl_mix, tl_ffn = min(L, TL_GLA), min(L, TL_MIX), min(L, TL_FFN)

    big_w = (w_in, w_out, w_up, w_down)
    started = [gather_start([_own_slot(w[l], chip, BF16) for w in big_w], f"gather_start_{l}") for l in range(1, DEPTH)]
    tokens = [s[3] for s in started]
    gathered = [[a[:, 0] for a in allgather_weights([_own_slot(w[0:1], chip, BF16) for w in big_w])]]

    def layer_weights(bufs):
        a_in, a_out, a_up, a_down = bufs
        w_in_l = jnp.pad(jnp.concatenate([a_in[j] for j in range(4)], axis=1), ((0, 0), (0, DINP - DIN)))
        return w_in_l, a_out.reshape(D, D), a_up, a_down.reshape(DFF, D)

    small = jnp.concatenate([conv_a.reshape(-1), gate_up_fwd.reshape(-1), gate_up_bwd.reshape(-1), conv_ffn.reshape(-1)])
    ms = small.shape[0] // 128
    sg = allgather8(small.reshape(ms, 128), "allgather_small_weights").reshape(4, 2, ms * 128)[:, 0]

    def small_full(off, shape):
        n = shape[0] * shape[1] * shape[2]
        return jnp.concatenate([sg[j, off:off + n].reshape(shape) for j in range(4)], axis=2)

    o1 = DEPTH * 3 * 128
    o2 = o1 + DEPTH * RK * 64
    o3 = o2 + DEPTH * RK * 64
    conv_a_f = small_full(0, (DEPTH, 3, 128))
    gup_f = small_full(o1, (DEPTH, RK, 64))
    gup_b = small_full(o2, (DEPTH, RK, 64))
    conv_ffn_f = small_full(o3, (DEPTH, 3, 1408))

    def gcat_of(l):
        g = jnp.zeros((LRW, 2 * DK), F32)
        g = g.at[0:RK, 0:DK].set(gup_f[l]).at[RK:2 * RK, DK:2 * DK].set(gup_b[l])
        return g.astype(BF16)

    gcats = [gcat_of(l) for l in range(DEPTH)]
    gbiases = [jnp.concatenate([gate_bias_fwd[l], gate_bias_bwd[l]])[None, :] for l in range(DEPTH)]
    ghn4s = [jnp.tile(gla_head_norm[l], NH)[None, :] for l in range(DEPTH)]

    xc = x.reshape(L, D)
    saved = []
    W_in, W_out, W_up, W_down = [], [], [], []
    for l in range(DEPTH):
        if l > 0:
            ssem, rsem, bufs, _ = started[l - 1]
            gathered.append(gather_wait(ssem, rsem, bufs, xc, f"gather_wait_{l}"))
        for lst, w in zip((W_in, W_out, W_up, W_down), layer_weights(gathered[l])):
            lst.append(w)
        P, h1 = rms_matmul(xc, norm_mix_pre[l][None, :], W_in[l], 640, "proj_in", after=tokens if l == 0 else ())
        o_f, o_b, sf, sb = gla_fwd(P, gcats[l], gbiases[l], tl_gla)
        ycat, y, x1 = mix_out(P, o_f, o_b, conv_a_f[l], ghn4s[l], W_out[l], norm_mix_post[l][None, :], xc, tl_mix)
        U, h2 = rms_matmul(x1, norm_ffn_pre[l][None, :], W_up[l], WFF, "proj_up", n_out=2 * DFF, out_dtype=BF16,
                           w_spec=pl.BlockSpec((None, D, WFF), lambda i, j: (j, 0, 0)))
        y2, x2 = ffn_down(U, conv_ffn_f[l], W_down[l], norm_ffn_post[l][None, :], x1, tl_ffn)
        saved.append(dict(x=xc, h1=h1, P=P, o_f=o_f, o_b=o_b, sf=sf, sb=sb, ycat=ycat, y=y, x1=x1, h2=h2, U=U, y2=y2))
        xc = x2

    dx, loss_blk = loss_head(xc, loss_target.reshape(L, D), tl_mix)

    big = ("w_in", "w_out", "w_up", "w_down")
    cidx = jnp.reshape(ci, (1,)).astype(jnp.int32)
    idx = jnp.stack([chip, ci]).astype(jnp.int32)
    grads = [None] * DEPTH
    reduced = [None] * DEPTH
    tl_dw = min(L, 1024)
    pending = None
    token = ()

    def finish(pend, after):
        lp, gs_p, recv1_p, (ssem, rsem, cs_thru, lands, _) = pend
        recv2 = exchange_wait(ssem, rsem, cs_thru, lands, after, f"exchange_wait_{lp}")
        halves = [rs_final_sum(g, r1, r2, idx, "rs_final_sum_" + k) for g, r1, r2, k in zip(gs_p, recv1_p, recv2, big)]
        reduced[lp] = rs_share_halves(halves)

    for l in reversed(range(DEPTH)):
        s = saved[l]
        dy2, dg4 = rms_bwd_pre(dx, s["y2"], norm_ffn_post[l][None, :], tl_mix, after=token)
        du_g, du_v, z, dcf_g, dcf_v = ffn_bwd1(dy2, s["U"], conv_ffn_f[l], W_down[l], tl_ffn)
        g_down = matmul_tn(z, dy2, DFF // 2, D, tl_dw, "dw_down").reshape(4, 1, DFF // 4, D)
        dU_g, dU_v, dx1, dg3 = ffn_bwd2(du_g, du_v, conv_ffn_f[l], W_up[l], s["x1"], dx, norm_ffn_pre[l][None, :],
                                        min(L, TL_FFN2))
        g_up = matmul_tn(s["h2"], dU_g, D, WFF, tl_dw, "dw_up_gate",
                         into=(lax.empty((4, 1, D, WFF), F32), (None, None, D, WFF), lambda p, q: (q, 0, 0, 0)))
        g_up = matmul_tn(s["h2"], dU_v, D, WFF, tl_dw, "dw_up_val",
                         into=(g_up, (None, None, D, WFF), lambda p, q: (NFF + q, 0, 0, 0)))
        dy, dg2 = rms_bwd_pre(dx1, s["y"], norm_mix_post[l][None, :], tl_mix)
        dgb, dcc, dgo, do, dca, dghn = mix_bwd1(dy, W_out[l], s["P"], s["o_f"], s["o_b"], conv_a_f[l], ghn4s[l], tl_mix)
        g_out = matmul_tn(s["ycat"], dy, D, D, tl_dw, "dw_out").reshape(4, 1, D // 4, D)
        gl = gla_bwd(s["P"], do, s["sf"], s["sb"], gcats[l], gbiases[l], tl_gla)
        dP, dx, dg1, dgcat, dbias = mix_bwd2(dgb, dcc, dgo, gl, s["P"], conv_a_f[l], gcats[l], gbiases[l], W_in[l],
                                             s["x"], dx1, norm_mix_pre[l][None, :], tl_mix)
        dW_in = matmul_tn(s["h1"], dP, D, 640, tl_dw, "dw_in")
        g_in = jnp.stack([dW_in[:, (DIN // 4) * j:(DIN // 4) * (j + 1)] for j in range(4)])[:, None]
        grads[l] = dict(
            norm_mix_pre=dg1[0], norm_mix_post=dg2[0], norm_ffn_pre=dg3[0], norm_ffn_post=dg4[0],
            conv_a=dca[0:3], gate_up_fwd=dgcat[0:RK, 0:DK], gate_bias_fwd=dbias[0, 0:DK],
            gate_up_bwd=dgcat[RK:2 * RK, DK:2 * DK], gate_bias_bwd=dbias[0, DK:2 * DK], gla_head_norm=dghn[0],
            conv_ffn=jnp.concatenate([dcf_g[0:3], dcf_v[0:3]], axis=1))
        gs = [g_in, g_out, g_up, g_down]
        recv1 = rs_sibling_halves(gs)
        cs16 = [rs_chipsum16(g, r, cidx, "rs_chipsum16_" + k) for g, r, k in zip(gs, recv1, big)]
        flight = exchange_start(cs16, f"exchange_start_{l}")
        token = (flight[4],)
        if pending is not None:
            finish(pending, flight[4])
        pending = (l, gs, recv1, flight)
    finish(pending, pending[3][4])

    G = {k: jnp.stack([grads[l][k] for l in range(DEPTH)]) for k in grads[0]}

    small_names = ["norm_mix_pre", "norm_mix_post", "norm_ffn_pre", "norm_ffn_post", "conv_a", "gate_up_fwd",
                   "gate_bias_fwd", "gate_up_bwd", "gate_bias_bwd", "gla_head_norm", "conv_ffn"]
    flat = jnp.concatenate([G[k].reshape(-1) for k in small_names] + [loss_blk[0, 0:1]])
    n_small = flat.shape[0]
    mp = -(-n_small // 1024) * 8
    flat = jnp.pad(flat, (0, mp * 128 - n_small)).reshape(mp, 128)
    tot = sum8(allgather8(flat, "allgather_small_grads"), mp).reshape(-1)
    gsm = {}
    o = 0
    for k in small_names:
        n = G[k].size
        gsm[k] = tot[o:o + n].reshape(G[k].shape)
        o += n
    loss = tot[o]

    def my_cols(a, width):
        return lax.dynamic_slice_in_dim(a, chip * width, width, axis=2)

    gsm["conv_a"] = my_cols(gsm["conv_a"], 128)
    gsm["gate_up_fwd"] = my_cols(gsm["gate_up_fwd"], 64)
    gsm["gate_up_bwd"] = my_cols(gsm["gate_up_bwd"], 64)
    gsm["conv_ffn"] = my_cols(gsm["conv_ffn"], 1408)

    for a, k in enumerate(big):
        gsm[k] = jnp.concatenate([reduced[l][a] for l in range(DEPTH)], axis=0)

    names = ["norm_mix_pre", "norm_mix_post", "norm_ffn_pre", "norm_ffn_post", "w_in", "conv_a", "gate_up_fwd",
             "gate_bias_fwd", "gate_up_bwd", "gate_bias_bwd", "gla_head_norm", "w_out", "w_up", "conv_ffn", "w_down"]
    w = dict(norm_mix_pre=norm_mix_pre, norm_mix_post=norm_mix_post, norm_ffn_pre=norm_ffn_pre, norm_ffn_post=norm_ffn_post,
             w_in=w_in, conv_a=conv_a, gate_up_fwd=gate_up_fwd, gate_bias_fwd=gate_bias_fwd, gate_up_bwd=gate_up_bwd,
             gate_bias_bwd=gate_bias_bwd, gla_head_norm=gla_head_norm, w_out=w_out, w_up=w_up, conv_ffn=conv_ffn, w_down=w_down)
    m = dict(norm_mix_pre=m_norm_mix_pre, norm_mix_post=m_norm_mix_post, norm_ffn_pre=m_norm_ffn_pre, norm_ffn_post=m_norm_ffn_post,
             w_in=m_w_in, conv_a=m_conv_a, gate_up_fwd=m_gate_up_fwd, gate_bias_fwd=m_gate_bias_fwd, gate_up_bwd=m_gate_up_bwd,
             gate_bias_bwd=m_gate_bias_bwd, gla_head_norm=m_gla_head_norm, w_out=m_w_out, w_up=m_w_up, conv_ffn=m_conv_ffn, w_down=m_w_down)
    v = dict(norm_mix_pre=v_norm_mix_pre, norm_mix_post=v_norm_mix_post, norm_ffn_pre=v_norm_ffn_pre, norm_ffn_post=v_norm_ffn_post,
             w_in=v_w_in, conv_a=v_conv_a, gate_up_fwd=v_gate_up_fwd, gate_bias_fwd=v_gate_bias_fwd, gate_up_bwd=v_gate_up_bwd,
             gate_bias_bwd=v_gate_bias_bwd, gla_head_norm=v_gla_head_norm, w_out=v_w_out, w_up=v_w_up, conv_ffn=v_conv_ffn, w_down=v_w_down)
    upd = {k: adamw(w[k], gsm[k], m[k], v[k], "adamw_" + k) for k in names}
    return (loss, dx.reshape(1, L, D), *[gsm[k] for k in names], *[upd[k][0] for k in names],
            *[upd[k][1] for k in names], *[upd[k][2] for k in names])
```

```python
import functools

import jax
import jax.numpy as jnp
from jax import lax
from jax.experimental import pallas as pl
from jax.experimental.pallas import tpu as pltpu

F32 = jnp.float32
BF16 = jnp.bfloat16
MXU_DTYPE = jnp.bfloat16
MESH = pl.DeviceIdType.MESH

D = 1024
DC = 512
DG = 512
NH = 4
HV = 128
HK = 64
DK = 256
RK = 16
CH = 64
DFF = 2816
DIN = 3104
DINP = 3200
LRW = 128
DEPTH = 4
EPS = 1e-6
QSCALE = HK ** -0.5
GATE_NORM = 1.0 / 16.0
CB_GB, CB_GC, CB_GV, CB_GO = 0, 1, 2, 5
CB_Q, CB_K = 6, 7
CB_V = 4
CB_LR = 24
LR = 0.001
B1 = 0.9
B2 = 0.999
AEPS = 1e-08
WD = 0.01
STEP = 10
TM_PROJ = 1024
TL_GLA = 512
TL_MIX = 256
TL_FFN = 256
TL_FFN2 = 512
VMEM_LIMIT = 56 * 1024 * 1024


def _cp(*sem):
    return pltpu.CompilerParams(dimension_semantics=sem if sem else None, vmem_limit_bytes=VMEM_LIMIT)


def _mm(a, b):
    return jnp.dot(a.astype(MXU_DTYPE), b.astype(MXU_DTYPE), preferred_element_type=F32)


def _mm_nt(a, b):
    return lax.dot_general(a.astype(MXU_DTYPE), b.astype(MXU_DTYPE), (((1,), (1,)), ((), ())),
                           preferred_element_type=F32)


def _mm_tn(a, b):
    return lax.dot_general(a.astype(MXU_DTYPE), b.astype(MXU_DTYPE), (((0,), (0,)), ((), ())),
                           preferred_element_type=F32)


def _mm_tri(tri, b):
    t = tri.astype(BF16)
    b1 = b.astype(BF16)
    r1 = b - b1.astype(F32)
    b2 = r1.astype(BF16)
    b3 = (r1 - b2.astype(F32)).astype(BF16)
    dot = lambda u: jnp.dot(t, u, preferred_element_type=F32)
    return dot(b1) + dot(b2) + dot(b3)


def _rms(x, g):
    r = lax.rsqrt(jnp.mean(x * x, axis=-1, keepdims=True) + EPS)
    return x * r * g


def _rms_bwd(dout, y, g):
    r = lax.rsqrt(jnp.mean(y * y, axis=-1, keepdims=True) + EPS)
    yh = y * r
    dyh = dout * g
    dy = r * (dyh - yh * jnp.mean(dyh * yh, axis=-1, keepdims=True))
    dg = jnp.sum(dout * yh, axis=0, keepdims=True)
    return dy, dg


def _sigmoid(x):
    return 0.5 * jnp.tanh(0.5 * x) + 0.5


def _logsig(x):
    return jnp.minimum(x, 0.0) - jnp.log1p(jnp.exp(-jnp.abs(x)))


def _shifts(x, p8, n8):
    n = x.shape[0]
    xe = jnp.concatenate([p8, x, n8], axis=0)
    return pltpu.roll(xe, 1, 0)[8:8 + n], pltpu.roll(xe, n + 15, 0)[8:8 + n]


def _halo_rows(prev_ref, next_ref, i, last):
    hr = prev_ref.shape[0]
    p = jnp.where(i == 0, 0.0, prev_ref[...].astype(F32)[hr - 8:hr, :])
    n = jnp.where(i == last, 0.0, next_ref[...].astype(F32)[0:8, :])
    return p, n


def _conv3(x, xp, xn, w_ref):
    xm1, xp1 = _shifts(x, xp, xn)
    return w_ref[0:1, :] * xm1 + w_ref[1:2, :] * x + w_ref[2:3, :] * xp1, xm1, xp1


def _conv3_t(d, dp, dn, w_ref):
    dm1, dp1 = _shifts(d, dp, dn)
    return w_ref[0:1, :] * dp1 + w_ref[1:2, :] * d + w_ref[2:3, :] * dm1


HALO32 = 8
HALO16 = 16


def _prev_row_blk(i, tl, hr):
    return jnp.maximum(i * (tl // hr) - 1, 0)


def _next_row_blk(i, tl, nrows, hr):
    return jnp.minimum((i + 1) * (tl // hr), nrows // hr - 1)


def _prev_blk(tl, cb, hr=HALO32):
    return lambda i: (_prev_row_blk(i, tl, hr), cb)


def _next_blk(tl, nrows, cb, hr=HALO32):
    return lambda i: (_next_row_blk(i, tl, nrows, hr), cb)


def rms_matmul(x, g, w, tn, name, w_spec=None, n_out=None, out_dtype=F32, after=()):
    L = x.shape[0]
    N = w.shape[1] if n_out is None else n_out
    tm = min(L, TM_PROJ)
    if w_spec is None:
        w_spec = pl.BlockSpec((D, tn), lambda i, j: (0, j))

    def body(x_ref, g_ref, w_ref, *rest):
        o_ref, h_ref = rest[-2:]

        @pl.when(pl.program_id(1) == 0)
        def _():
            h_ref[...] = _rms(x_ref[...], g_ref[...]).astype(BF16)

        o_ref[...] = _mm(h_ref[...], w_ref[...]).astype(out_dtype)

    return pl.pallas_call(
        body, name=name, grid=(L // tm, N // tn),
        in_specs=[pl.BlockSpec((tm, D), lambda i, j: (i, 0)), pl.BlockSpec((1, D), lambda i, j: (0, 0)), w_spec]
        + [_ANY] * len(after),
        out_specs=[pl.BlockSpec((tm, tn), lambda i, j: (i, j)), pl.BlockSpec((tm, D), lambda i, j: (i, 0))],
        out_shape=[jax.ShapeDtypeStruct((L, N), out_dtype), jax.ShapeDtypeStruct((L, D), BF16)],
        compiler_params=_cp("parallel", "arbitrary"),
    )(x, g, w, *after)


def _gla_masks():
    def blk(shape, rdiv, cdiv):
        r = lax.broadcasted_iota(jnp.int32, shape, 0) // rdiv
        c = lax.broadcasted_iota(jnp.int32, shape, 1) // cdiv
        return (r == c).astype(F32)

    r = lax.broadcasted_iota(jnp.int32, (CH, CH), 0)
    c = lax.broadcasted_iota(jnp.int32, (CH, CH), 1)
    r4 = lax.broadcasted_iota(jnp.int32, (NH * CH, CH), 0) % CH
    c4 = lax.broadcasted_iota(jnp.int32, (NH * CH, CH), 1)
    return dict(
        bdq=blk((NH * CH, DK), CH, HK),
        bdo=blk((NH * CH, DG), CH, HV),
        bds=blk((DG, DK), HV, HK),
        tril=(r >= c).astype(F32), triu=(r <= c).astype(F32),
        tril4=r4 >= c4, triu4=r4 <= c4,
    )


def _tile4(x):
    return jnp.concatenate([x, x, x, x], axis=0)


def _gla_tile_prep(q, k, a, m, rev, nc):
    tri = m["triu"] if rev else m["tril"]
    chunks = [a[c * CH:(c + 1) * CH] for c in range(nc)]
    cum = jnp.concatenate([_mm_tri(tri, ac) for ac in chunks], axis=0)
    tot = jnp.concatenate([jnp.sum(ac, axis=0, keepdims=True) for ac in chunks], axis=0)
    tot_rows = jnp.concatenate([jnp.broadcast_to(tot[c:c + 1], (CH, DK)) for c in range(nc)], axis=0)
    e = jnp.exp(cum)
    einv = jnp.exp(-cum)
    eout = jnp.exp(tot_rows - cum)
    return dict(e=e, einv=einv, eout=eout, dec=jnp.exp(tot), qt=q * QSCALE * e, kt=k * einv, kh=k * eout)


def _gla_scores(qt16, kt16, m, rev):
    qs = _tile4(qt16) * m["bdq"].astype(qt16.dtype)
    return qs, jnp.where(m["triu4"] if rev else m["tril4"], _mm_nt(qs, kt16), 0.0)


def _gla_chunk_fwd(qt16, kt16, kh16, v, dec, st_ref, m, rev):
    _, sc = _gla_scores(qt16, kt16, m, rev)
    v16 = v.astype(BF16)
    r = _mm(sc, v16)
    o_intra = jnp.concatenate([r[h * CH:(h + 1) * CH, h * HV:(h + 1) * HV] for h in range(NH)], axis=1)
    st = st_ref[...]
    st16 = st.astype(BF16)
    o = o_intra + _mm_nt(qt16, st16)
    st_ref[...] = st * dec + _mm_tn(v16, kh16) * m["bds"]
    return o, st16


def _gates(lr_ref, gc_ref, bs_ref, cols):
    return _logsig(_mm(lr_ref[...], gc_ref[:, cols]) + bs_ref[:, cols]) * GATE_NORM


def gla_fwd(P, gcat, gbias, tl):
    L = P.shape[0]
    nb = L // tl
    nc = tl // CH

    def body(qf, kf, vf, lf, qb, kb, vb, lb, gc_ref, bs_ref, of, ob, sf, sb, stf, stb,
             qtf, ktf, khf, dcf, qtb, ktb, khb, dcb):
        @pl.when(pl.program_id(0) == 0)
        def _():
            stf[...] = jnp.zeros_like(stf)
            stb[...] = jnp.zeros_like(stb)

        m = _gla_masks()
        for (q, k, lr, cols, rev, qt, kt, kh, dc) in ((qf, kf, lf, slice(0, DK), False, qtf, ktf, khf, dcf),
                                                      (qb, kb, lb, slice(DK, 2 * DK), True, qtb, ktb, khb, dcb)):
            p = _gla_tile_prep(q[...], k[...], _gates(lr, gc_ref, bs_ref, cols), m, rev, nc)
            qt[...] = p["qt"].astype(BF16)
            kt[...] = p["kt"].astype(BF16)
            kh[...] = p["kh"].astype(BF16)
            dc[...] = p["dec"]

        def chunk(c, carry):
            rows = pl.ds(pl.multiple_of(c * CH, CH), CH)
            o, st = _gla_chunk_fwd(qtf[rows, :], ktf[rows, :], khf[rows, :], vf[rows, :], dcf[pl.ds(c, 1), :], stf, m, False)
            of[rows, :] = o
            sf[c] = st
            cb = nc - 1 - c
            rows = pl.ds(pl.multiple_of(cb * CH, CH), CH)
            o, st = _gla_chunk_fwd(qtb[rows, :], ktb[rows, :], khb[rows, :], vb[rows, :], dcb[pl.ds(cb, 1), :], stb, m, True)
            ob[rows, :] = o
            sb[cb] = st
            return carry

        lax.fori_loop(0, nc, chunk, 0, unroll=2)

    fw = lambda cb: (lambda i: (i, cb))
    bw = lambda cb: (lambda i: (nb - 1 - i, cb))
    return pl.pallas_call(
        body, name="gla_fwd", grid=(nb,),
        in_specs=[pl.BlockSpec((tl, DK), fw(CB_Q)), pl.BlockSpec((tl, DK), fw(CB_K)), pl.BlockSpec((tl, DG), fw(CB_V)),
                  pl.BlockSpec((tl, LRW), fw(CB_LR)),
                  pl.BlockSpec((tl, DK), bw(CB_Q)), pl.BlockSpec((tl, DK), bw(CB_K)), pl.BlockSpec((tl, DG), bw(CB_V)),
                  pl.BlockSpec((tl, LRW), bw(CB_LR)),
                  pl.BlockSpec((LRW, 2 * DK), lambda i: (0, 0)), pl.BlockSpec((1, 2 * DK), lambda i: (0, 0))],
        out_specs=[pl.BlockSpec((tl, DG), lambda i: (i, 0)), pl.BlockSpec((tl, DG), lambda i: (nb - 1 - i, 0)),
                   pl.BlockSpec((nc, DG, DK), lambda i: (i, 0, 0)), pl.BlockSpec((nc, DG, DK), lambda i: (nb - 1 - i, 0, 0))],
        out_shape=[jax.ShapeDtypeStruct((L, DG), F32), jax.ShapeDtypeStruct((L, DG), F32),
                   jax.ShapeDtypeStruct((L // CH, DG, DK), BF16), jax.ShapeDtypeStruct((L // CH, DG, DK), BF16)],
        scratch_shapes=[pltpu.VMEM((DG, DK), F32), pltpu.VMEM((DG, DK), F32)]
        + [pltpu.VMEM((tl, DK), BF16)] * 3 + [pltpu.VMEM((nc, DK), F32)]
        + [pltpu.VMEM((tl, DK), BF16)] * 3 + [pltpu.VMEM((nc, DK), F32)],
        compiler_params=_cp("arbitrary"),
    )(P, P, P, P, P, P, P, P, gcat, gbias)


def _headnorm(o):
    oh, rs = [], []
    for h in range(NH):
        oo = o[:, h * HV:(h + 1) * HV]
        r = lax.rsqrt(jnp.mean(oo * oo, axis=-1, keepdims=True) + EPS)
        oh.append(oo * r)
        rs.append(r)
    return jnp.concatenate(oh, axis=1), rs


def mix_out(P, o_f, o_b, conv_a, ghn4, w_out, g2, x, tl):
    L = P.shape[0]
    nt = L // tl

    def body(gb, gc, gv, go, gcp, gvp, gcn, gvn, of, ob, ca, gh, wo, g2r, xr, ycat, yr, x1):
        i = pl.program_id(0)
        cp, cn = _halo_rows(gcp, gcn, i, nt - 1)
        vp, vn = _halo_rows(gvp, gvn, i, nt - 1)
        c = gc[...] * gv[...]
        cc, _, _ = _conv3(c, cp * vp, cn * vn, ca)
        ya = gb[...] * cc
        oh, _ = _headnorm(of[...] + ob[...])
        g = go[...]
        yb = g * _sigmoid(g) * (oh * gh[...])
        yc = jnp.concatenate([ya, yb], axis=1).astype(BF16)
        ycat[...] = yc
        y = _mm(yc, wo[...])
        yr[...] = y
        x1[...] = xr[...] + _rms(y, g2r[...])

    t = lambda cb: pl.BlockSpec((tl, DC), lambda i: (i, cb))
    hp = lambda cb: pl.BlockSpec((8, DC), _prev_blk(tl, cb))
    hn = lambda cb: pl.BlockSpec((8, DC), _next_blk(tl, L, cb))
    row = lambda n: pl.BlockSpec((tl, n), lambda i: (i, 0))
    full = lambda a: pl.BlockSpec(a.shape, lambda i: (0, 0))
    return pl.pallas_call(
        body, name="mix_out", grid=(nt,),
        in_specs=[t(CB_GB), t(CB_GC), t(CB_GV), t(CB_GO), hp(CB_GC), hp(CB_GV), hn(CB_GC), hn(CB_GV),
                  row(DG), row(DG), full(conv_a), full(ghn4), full(w_out), full(g2), row(D)],
        out_specs=[row(D), row(D), row(D)],
        out_shape=[jax.ShapeDtypeStruct((L, D), BF16), jax.ShapeDtypeStruct((L, D), F32),
                   jax.ShapeDtypeStruct((L, D), F32)],
        compiler_params=_cp("parallel"),
    )(P, P, P, P, P, P, P, P, o_f, o_b, conv_a, ghn4, w_out, g2, x)


NFF = 2
WFF = DFF // NFF


def ffn_down(U, conv_ffn, w_down, g4, x1, tl):
    L = U.shape[0]
    nt = L // tl

    def body(u, up, un, cf, wd, g4r, x1r, y2, x2):
        i = pl.program_id(0)
        acc = jnp.zeros((tl, D), F32)
        for j in range(NFF):
            gs = slice(j * WFF, (j + 1) * WFF)
            vs = slice(DFF + j * WFF, DFF + (j + 1) * WFF)
            z = []
            for s in (gs, vs):
                p, n = _halo_rows(up.at[:, s], un.at[:, s], i, nt - 1)
                z.append(_conv3(u[:, s].astype(F32), p, n, cf.at[:, s])[0])
            zz = z[0] * _sigmoid(z[0]) * z[1]
            acc = acc + _mm(zz, wd[gs, :])
        y2[...] = acc
        x2[...] = x1r[...] + _rms(acc, g4r[...])

    row = lambda n: pl.BlockSpec((tl, n), lambda i: (i, 0))
    full = lambda a: pl.BlockSpec(a.shape, lambda i: (0, 0))
    return pl.pallas_call(
        body, name="ffn_down", grid=(nt,),
        in_specs=[row(2 * DFF), pl.BlockSpec((HALO16, 2 * DFF), _prev_blk(tl, 0, HALO16)),
                  pl.BlockSpec((HALO16, 2 * DFF), _next_blk(tl, L, 0, HALO16)),
                  full(conv_ffn), full(w_down), full(g4), row(D)],
        out_specs=[row(D), row(D)],
        out_shape=[jax.ShapeDtypeStruct((L, D), F32), jax.ShapeDtypeStruct((L, D), F32)],
        compiler_params=_cp("parallel"),
    )(U, U, U, conv_ffn, w_down, g4, x1)


def loss_head(y, target, tl):
    L = y.shape[0]

    def body(yr, tr, dy, ls):
        @pl.when(pl.program_id(0) == 0)
        def _():
            ls[...] = jnp.zeros_like(ls)

        err = yr[...] - tr[...]
        dy[...] = err * (1.0 / D)
        ls[...] += (0.5 / D) * jnp.sum(err * err)

    row = pl.BlockSpec((tl, D), lambda i: (i, 0))
    return pl.pallas_call(
        body, name="loss_head", grid=(L // tl,), in_specs=[row, row],
        out_specs=[row, pl.BlockSpec((8, 128), lambda i: (0, 0))],
        out_shape=[jax.ShapeDtypeStruct((L, D), F32), jax.ShapeDtypeStruct((8, 128), F32)],
        compiler_params=_cp("arbitrary"),
    )(y, target)


def rms_bwd_pre(dout, y, g, tl, after=()):
    L = y.shape[0]

    def body(dr, yr, gr, *rest):
        dy, dg = rest[-2:]

        @pl.when(pl.program_id(0) == 0)
        def _():
            dg[...] = jnp.zeros_like(dg)

        a, b = _rms_bwd(dr[...], yr[...], gr[...])
        dy[...] = a.astype(BF16)
        dg[...] += b

    row = pl.BlockSpec((tl, D), lambda i: (i, 0))
    vec = pl.BlockSpec((1, D), lambda i: (0, 0))
    return pl.pallas_call(
        body, name="rms_bwd_pre", grid=(L // tl,), in_specs=[row, row, vec] + [_ANY] * len(after), out_specs=[row, vec],
        out_shape=[jax.ShapeDtypeStruct((L, D), BF16), jax.ShapeDtypeStruct((1, D), F32)],
        compiler_params=_cp("arbitrary"),
    )(dout, y, g, *after)


def ffn_bwd1(dy2, U, conv_ffn, w_down, tl):
    L = U.shape[0]
    nt = L // tl

    def body(dy, ug, uv, ugp, ugn, uvp, uvn, cg, cv, wd, dug, duv, zr, dcg, dcv):
        i = pl.program_id(1)

        @pl.when(i == 0)
        def _():
            dcg[...] = jnp.zeros_like(dcg)
            dcv[...] = jnp.zeros_like(dcv)

        gp, gn = _halo_rows(ugp, ugn, i, nt - 1)
        vp, vn = _halo_rows(uvp, uvn, i, nt - 1)
        ug32 = ug[...].astype(F32)
        uv32 = uv[...].astype(F32)
        a, a_m1, a_p1 = _conv3(ug32, gp, gn, cg)
        b, b_m1, b_p1 = _conv3(uv32, vp, vn, cv)
        sg = _sigmoid(a)
        silu = a * sg
        zr[...] = (silu * b).astype(BF16)
        dz = _mm_nt(dy[...], wd[...])
        dval = dz * silu
        dgate = dz * b * (sg * (1.0 + a * (1.0 - sg)))
        dug[...] = dgate.astype(BF16)
        duv[...] = dval.astype(BF16)
        for k, (sa, sb) in enumerate(((a_m1, b_m1), (ug32, uv32), (a_p1, b_p1))):
            dcg[k:k + 1, :] += jnp.sum(dgate * sa, axis=0, keepdims=True)
            dcv[k:k + 1, :] += jnp.sum(dval * sb, axis=0, keepdims=True)

    tile = lambda off: pl.BlockSpec((tl, WFF), lambda j, i: (i, off + j))
    prev = lambda off: pl.BlockSpec((HALO16, WFF), lambda j, i: (_prev_row_blk(i, tl, HALO16), off + j))
    nxt = lambda off: pl.BlockSpec((HALO16, WFF), lambda j, i: (_next_row_blk(i, tl, L, HALO16), off + j))
    cw = lambda off: pl.BlockSpec((3, WFF), lambda j, i: (0, off + j))
    acc = pl.BlockSpec((8, WFF), lambda j, i: (0, j))
    return pl.pallas_call(
        body, name="ffn_bwd1", grid=(NFF, nt),
        in_specs=[pl.BlockSpec((tl, D), lambda j, i: (i, 0)), tile(0), tile(NFF), prev(0), nxt(0), prev(NFF), nxt(NFF),
                  cw(0), cw(NFF), pl.BlockSpec((WFF, D), lambda j, i: (j, 0))],
        out_specs=[tile(0), tile(0), tile(0), acc, acc],
        out_shape=[jax.ShapeDtypeStruct((L, DFF), BF16), jax.ShapeDtypeStruct((L, DFF), BF16),
                   jax.ShapeDtypeStruct((L, DFF), BF16), jax.ShapeDtypeStruct((8, DFF), F32),
                   jax.ShapeDtypeStruct((8, DFF), F32)],
        compiler_params=_cp("parallel", "arbitrary"),
    )(dy2, U, U, U, U, U, U, conv_ffn, conv_ffn, w_down)


def ffn_bwd2(du_g, du_v, conv_ffn, w_up, x1, dres, g3, tl):
    L = x1.shape[0]
    nt = L // tl

    def body(dg_, dv_, dgp, dgn, dvp, dvn, cg, cv, wg, wv, x1r, drr, g3r, dUg, dUv, dx1, dg3, acc):
        i = pl.program_id(0)
        j = pl.program_id(1)
        gp, gn = _halo_rows(dgp, dgn, i, nt - 1)
        vp, vn = _halo_rows(dvp, dvn, i, nt - 1)
        a = _conv3_t(dg_[...].astype(F32), gp, gn, cg).astype(BF16)
        b = _conv3_t(dv_[...].astype(F32), vp, vn, cv).astype(BF16)
        dUg[...] = a
        dUv[...] = b
        part = _mm_nt(a, wg[...]) + _mm_nt(b, wv[...])

        @pl.when(j == 0)
        def _():
            acc[...] = part

        @pl.when(j > 0)
        def _():
            acc[...] += part

        @pl.when((i == 0) & (j == 0))
        def _():
            dg3[...] = jnp.zeros_like(dg3)

        @pl.when(j == NFF - 1)
        def _():
            dx, dg = _rms_bwd(acc[...], x1r[...], g3r[...])
            dx1[...] = drr[...] + dx
            dg3[...] += dg

    tile = pl.BlockSpec((tl, WFF), lambda i, j: (i, j))
    prev = pl.BlockSpec((HALO16, WFF), lambda i, j: (_prev_row_blk(i, tl, HALO16), j))
    nxt = pl.BlockSpec((HALO16, WFF), lambda i, j: (_next_row_blk(i, tl, L, HALO16), j))
    cw = lambda off: pl.BlockSpec((3, WFF), lambda i, j: (0, off + j))
    ww = lambda off: pl.BlockSpec((None, D, WFF), lambda i, j: (off + j, 0, 0))
    row = pl.BlockSpec((tl, D), lambda i, j: (i, 0))
    vec = pl.BlockSpec((1, D), lambda i, j: (0, 0))
    return pl.pallas_call(
        body, name="ffn_bwd2", grid=(nt, NFF),
        in_specs=[tile, tile, prev, nxt, prev, nxt, cw(0), cw(NFF), ww(0), ww(NFF), row, row, vec],
        out_specs=[tile, tile, row, vec],
        out_shape=[jax.ShapeDtypeStruct((L, DFF), BF16), jax.ShapeDtypeStruct((L, DFF), BF16),
                   jax.ShapeDtypeStruct((L, D), F32), jax.ShapeDtypeStruct((1, D), F32)],
        scratch_shapes=[pltpu.VMEM((tl, D), F32)],
        compiler_params=_cp("arbitrary", "arbitrary"),
    )(du_g, du_v, du_g, du_g, du_v, du_v, conv_ffn, conv_ffn, w_up, w_up, x1, dres, g3)


def matmul_tn(a, b, ta, tn, tl, name, into=None):
    L, Ka = a.shape
    N = b.shape[1]

    def body(ar, br, *rest):
        o = rest[-1]

        @pl.when(pl.program_id(2) == 0)
        def _():
            o[...] = jnp.zeros_like(o)

        o[...] += _mm_tn(ar[...], br[...]).reshape(o.shape)

    in_specs = [pl.BlockSpec((tl, ta), lambda p, q, l: (l, p)), pl.BlockSpec((tl, tn), lambda p, q, l: (l, q))]
    if into is None:
        return pl.pallas_call(
            body, name=name, grid=(Ka // ta, N // tn, L // tl), in_specs=in_specs,
            out_specs=pl.BlockSpec((ta, tn), lambda p, q, l: (p, q)),
            out_shape=jax.ShapeDtypeStruct((Ka, N), F32),
            compiler_params=_cp("parallel", "parallel", "arbitrary"),
        )(a, b)
    buf, blk, idx = into
    return pl.pallas_call(
        body, name=name, grid=(Ka // ta, N // tn, L // tl), in_specs=in_specs + [_ANY],
        out_specs=pl.BlockSpec(blk, lambda p, q, l: idx(p, q)),
        out_shape=jax.ShapeDtypeStruct(buf.shape, F32), input_output_aliases={2: 0},
        compiler_params=_cp("parallel", "parallel", "arbitrary"),
    )(a, b, buf)


def mix_bwd1(dy, w_out, P, o_f, o_b, conv_a, ghn4, tl):
    L = P.shape[0]
    nt = L // tl

    def body(dyr, wo, gb, gc, gv, go, gcp, gvp, gcn, gvn, of, ob, ca, gh, dgb, dcc, dgo, do, dca, dgh):
        i = pl.program_id(0)

        @pl.when(i == 0)
        def _():
            dca[...] = jnp.zeros_like(dca)
            dgh[...] = jnp.zeros_like(dgh)

        dycat = _mm_nt(dyr[...], wo[...])
        dya = dycat[:, 0:DC]
        dyb = dycat[:, DC:D]
        cp, cn = _halo_rows(gcp, gcn, i, nt - 1)
        vp, vn = _halo_rows(gvp, gvn, i, nt - 1)
        c = gc[...] * gv[...]
        cc, c_m1, c_p1 = _conv3(c, cp * vp, cn * vn, ca)
        dgb[...] = dya * cc
        d = dya * gb[...]
        dcc[...] = d
        for k, s in enumerate((c_m1, c, c_p1)):
            dca[k:k + 1, :] += jnp.sum(d * s, axis=0, keepdims=True)
        oh, rs = _headnorm(of[...] + ob[...])
        g = go[...]
        sg = _sigmoid(g)
        silu = g * sg
        dgo[...] = dyb * (oh * gh[...]) * (sg * (1.0 + g * (1.0 - sg)))
        don = dyb * silu
        t = jnp.sum(don * oh, axis=0, keepdims=True)
        dgh[0:1, :] += t[:, 0:HV] + t[:, HV:2 * HV] + t[:, 2 * HV:3 * HV] + t[:, 3 * HV:4 * HV]
        doh = don * gh[...]
        parts = []
        for h in range(NH):
            hs = slice(h * HV, (h + 1) * HV)
            parts.append(rs[h] * (doh[:, hs] - oh[:, hs] * jnp.mean(doh[:, hs] * oh[:, hs], axis=-1, keepdims=True)))
        do[...] = jnp.concatenate(parts, axis=1)

    t = lambda cb: pl.BlockSpec((tl, DC), lambda i: (i, cb))
    hp = lambda cb: pl.BlockSpec((8, DC), _prev_blk(tl, cb))
    hn = lambda cb: pl.BlockSpec((8, DC), _next_blk(tl, L, cb))
    row = lambda n: pl.BlockSpec((tl, n), lambda i: (i, 0))
    full = lambda a: pl.BlockSpec(a.shape, lambda i: (0, 0))
    f32o = lambda n: jax.ShapeDtypeStruct((L, n), F32)
    return pl.pallas_call(
        body, name="mix_bwd1", grid=(nt,),
        in_specs=[row(D), full(w_out), t(CB_GB), t(CB_GC), t(CB_GV), t(CB_GO), hp(CB_GC), hp(CB_GV), hn(CB_GC), hn(CB_GV),
                  row(DG), row(DG), full(conv_a), full(ghn4)],
        out_specs=[row(DC), row(DC), row(DG), row(DG), pl.BlockSpec((8, DC), lambda i: (0, 0)),
                   pl.BlockSpec((8, HV), lambda i: (0, 0))],
        out_shape=[f32o(DC), f32o(DC), f32o(DG), f32o(DG), jax.ShapeDtypeStruct((8, DC), F32),
                   jax.ShapeDtypeStruct((8, HV), F32)],
        compiler_params=_cp("arbitrary"),
    )(dy, w_out, P, P, P, P, P, P, P, P, o_f, o_b, conv_a, ghn4)


def _gla_chunk_bwd(qt, kt, kh, v, do, st16, dec, g_ref, m, rev):
    qt16, kt16, kh16, v16, do16 = (t.astype(BF16) for t in (qt, kt, kh, v, do))
    qs, sc = _gla_scores(qt16, kt16, m, rev)
    g = g_ref[...]
    g16 = g.astype(BF16)
    dob = _tile4(do16) * m["bdo"].astype(BF16)
    dv = _mm_tn(sc, dob) + _mm_nt(kh16, g16)
    dsc = jnp.where(m["triu4"] if rev else m["tril4"], _mm_nt(dob, v16), 0.0)
    r1 = _mm(dsc, kt16) * m["bdq"]
    dqt = r1[0:CH] + r1[CH:2 * CH] + r1[2 * CH:3 * CH] + r1[3 * CH:4 * CH] + _mm(do16, st16)
    dkt = _mm_tn(dsc, qs)
    dkh = _mm(v16, g16)
    dd = jnp.sum(g * st16.astype(F32), axis=0, keepdims=True)
    g_ref[...] = g * dec + _mm_tn(do16, qt16) * m["bds"]
    return dv, dqt, dkt, dkh, dd


def gla_bwd(P, do, sf, sb, gcat, gbias, tl):
    L = P.shape[0]
    nb = L // tl
    nc = tl // CH

    def body(qf, kf, vf, lf, dof, sfr, qb, kb, vb, lb, dob, sbr, gc_ref, bs_ref,
             dqf, dkf, dvf, daf, dqb, dkb, dvb, dab, gf, gbk, *scr):
        @pl.when(pl.program_id(0) == 0)
        def _():
            gf[...] = jnp.zeros_like(gf)
            gbk[...] = jnp.zeros_like(gbk)

        m = _gla_masks()
        keys = ("qt", "kt", "kh", "e", "einv", "eout", "dec")
        pf = dict(zip(keys + ("dd",), scr[0:8]))
        pb = dict(zip(keys + ("dd",), scr[8:16]))
        for (q, k, lr, cols, rev, pr) in ((qf, kf, lf, slice(0, DK), False, pf), (qb, kb, lb, slice(DK, 2 * DK), True, pb)):
            p = _gla_tile_prep(q[...], k[...], _gates(lr, gc_ref, bs_ref, cols), m, rev, nc)
            for key in keys:
                pr[key][...] = p[key]

        def step(c, v, dor, st, g_ref, pr, dq, dk, dv, da, rev):
            rows = pl.ds(pl.multiple_of(c * CH, CH), CH)
            dvc, dqt, dkt, dkh, dd = _gla_chunk_bwd(pr["qt"][rows, :], pr["kt"][rows, :], pr["kh"][rows, :], v[rows, :],
                                                    dor[rows, :], st[c], pr["dec"][pl.ds(c, 1), :], g_ref, m, rev)
            dv[rows, :] = dvc
            dq[rows, :] = dqt
            dk[rows, :] = dkt
            da[rows, :] = dkh
            pr["dd"][pl.ds(c, 1), :] = dd

        def chunk(c, carry):
            step(nc - 1 - c, vf, dof, sfr, gf, pf, dqf, dkf, dvf, daf, False)
            step(c, vb, dob, sbr, gbk, pb, dqb, dkb, dvb, dab, True)
            return carry

        lax.fori_loop(0, nc, chunk, 0, unroll=2)

        def finish(pr, dq, dk, da, rev):
            dqt, dkt, dkh = dq[...], dk[...], da[...]
            kk = dkh * pr["kh"][...]
            dcum = dqt * pr["qt"][...] - dkt * pr["kt"][...] - kk
            dtot = pr["dd"][...] * pr["dec"][...]
            tri_t = m["tril"] if rev else m["triu"]
            parts = []
            for c in range(nc):
                rs = slice(c * CH, (c + 1) * CH)
                parts.append(_mm_tri(tri_t, dcum[rs]) + (jnp.sum(kk[rs], axis=0, keepdims=True) + dtot[c:c + 1]))
            da[...] = jnp.concatenate(parts, axis=0)
            dq[...] = dqt * pr["e"][...] * QSCALE
            dk[...] = dkt * pr["einv"][...] + dkh * pr["eout"][...]

        finish(pf, dqf, dkf, daf, False)
        finish(pb, dqb, dkb, dab, True)

    fwd_dir = lambda cb: (lambda i: (nb - 1 - i, cb))
    bwd_dir = lambda cb: (lambda i: (i, cb))

    def side(ix):
        return [pl.BlockSpec((tl, DK), ix(CB_Q)), pl.BlockSpec((tl, DK), ix(CB_K)), pl.BlockSpec((tl, DG), ix(CB_V)),
                pl.BlockSpec((tl, LRW), ix(CB_LR)), pl.BlockSpec((tl, DG), ix(0)),
                pl.BlockSpec((nc, DG, DK), lambda i: (ix(0)(i)[0], 0, 0))]

    def outs(ix):
        return [pl.BlockSpec((tl, DK), ix(0)), pl.BlockSpec((tl, DK), ix(0)), pl.BlockSpec((tl, DG), ix(0)),
                pl.BlockSpec((tl, DK), ix(0))]

    o_shape = [jax.ShapeDtypeStruct((L, DK), F32), jax.ShapeDtypeStruct((L, DK), F32),
               jax.ShapeDtypeStruct((L, DG), F32), jax.ShapeDtypeStruct((L, DK), F32)]
    return pl.pallas_call(
        body, name="gla_bwd", grid=(nb,),
        in_specs=side(fwd_dir) + side(bwd_dir) + [pl.BlockSpec((LRW, 2 * DK), lambda i: (0, 0)),
                                                  pl.BlockSpec((1, 2 * DK), lambda i: (0, 0))],
        out_specs=outs(fwd_dir) + outs(bwd_dir),
        out_shape=o_shape + o_shape,
        scratch_shapes=[pltpu.VMEM((DG, DK), F32), pltpu.VMEM((DG, DK), F32)]
        + ([pltpu.VMEM((tl, DK), F32)] * 6 + [pltpu.VMEM((nc, DK), F32)] * 2) * 2,
        compiler_params=_cp("arbitrary"),
    )(P, P, P, P, do, sf, P, P, P, P, do, sb, gcat, gbias)


def mix_bwd2(dgb, dcc, dgo, gl, P, conv_a, gcat, gbias, w_in, x, dres, g1, tl):
    L = P.shape[0]
    nt = L // tl

    def body(dgbr, dccr, dccp, dccn, dgor, dqf, dkf, dvf, daf, dqb, dkb, dvb, dab, gc, gv, lr, ca, gcr, bsr, wi,
             xr, drr, g1r, dP, dx, dg1, dgcat, dbias):
        i = pl.program_id(0)

        @pl.when(i == 0)
        def _():
            dg1[...] = jnp.zeros_like(dg1)
            dgcat[...] = jnp.zeros_like(dgcat)
            dbias[...] = jnp.zeros_like(dbias)

        p, n = _halo_rows(dccp, dccn, i, nt - 1)
        dc = _conv3_t(dccr[...], p, n, ca)
        pre = _mm(lr[...], gcr[...]) + bsr[...]
        da = jnp.concatenate([daf[...], dab[...]], axis=1)
        dpre = da * GATE_NORM * (1.0 - _sigmoid(pre))
        dpre16 = dpre.astype(BF16)
        dP[:, 0:DC] = dgbr[...].astype(BF16)
        dP[:, DC:2 * DC] = (dc * gv[...]).astype(BF16)
        dP[:, 2 * DC:3 * DC] = (dc * gc[...]).astype(BF16)
        dP[:, 1536:1792] = (dqf[...] + dqb[...]).astype(BF16)
        dP[:, 1792:2048] = (dkf[...] + dkb[...]).astype(BF16)
        dP[:, 2048:2560] = (dvf[...] + dvb[...]).astype(BF16)
        dP[:, 2560:3072] = dgor[...].astype(BF16)
        dP[:, 3072:3200] = _mm_nt(dpre16, gcr[...]).astype(BF16)
        dgcat[...] += _mm_tn(lr[...], dpre16)
        dbias[0:1, :] += jnp.sum(dpre, axis=0, keepdims=True)
        dh, dg = _rms_bwd(_mm_nt(dP[...], wi[...]), xr[...], g1r[...])
        dx[...] = drr[...] + dh
        dg1[...] += dg

    row = lambda n: pl.BlockSpec((tl, n), lambda i: (i, 0))
    t = lambda w, cb: pl.BlockSpec((tl, w), lambda i: (i, cb))
    full = lambda a: pl.BlockSpec(a.shape, lambda i: (0, 0))
    return pl.pallas_call(
        body, name="mix_bwd2", grid=(nt,),
        in_specs=[row(DC), row(DC), pl.BlockSpec((8, DC), _prev_blk(tl, 0)), pl.BlockSpec((8, DC), _next_blk(tl, L, 0)),
                  row(DG), row(DK), row(DK), row(DG), row(DK), row(DK), row(DK), row(DG), row(DK),
                  t(DC, CB_GC), t(DC, CB_GV), t(LRW, CB_LR), full(conv_a), full(gcat), full(gbias), full(w_in),
                  row(D), row(D), full(g1)],
        out_specs=[row(DINP), row(D), pl.BlockSpec((1, D), lambda i: (0, 0)), pl.BlockSpec((LRW, 2 * DK), lambda i: (0, 0)),
                   pl.BlockSpec((8, 2 * DK), lambda i: (0, 0))],
        out_shape=[jax.ShapeDtypeStruct((L, DINP), BF16), jax.ShapeDtypeStruct((L, D), F32),
                   jax.ShapeDtypeStruct((1, D), F32), jax.ShapeDtypeStruct((LRW, 2 * DK), F32),
                   jax.ShapeDtypeStruct((8, 2 * DK), F32)],
        compiler_params=_cp("arbitrary"),
    )(dgb, dcc, dcc, dcc, dgo, *gl, P, P, P, conv_a, gcat, gbias, w_in, x, dres, g1)


def _row_tile(rows, cols):
    if rows * cols * 4 <= 2 * 1024 * 1024:
        return rows
    best = 8
    for t in range(8, rows, 8):
        if rows % t == 0 and t * cols * 4 <= 2 * 1024 * 1024:
            best = t
    return best


def adamw(w, g, m, v, name):
    shape = w.shape
    cols = shape[-1]
    w2, g2, m2, v2 = (a.reshape(-1, cols) for a in (w, g, m, v))
    rows = w2.shape[0]
    tr = _row_tile(rows, cols)

    def body(wr, gr, mr, vr, dl, nm, nv):
        gg = gr[...]
        mm = B1 * mr[...] + (1.0 - B1) * gg
        vv = B2 * vr[...] + (1.0 - B2) * (gg * gg)
        m_hat = mm / (1.0 - B1 ** STEP)
        v_hat = vv / (1.0 - B2 ** STEP)
        dl[...] = -LR * (m_hat / (jnp.sqrt(v_hat) + AEPS) + WD * wr[...])
        nm[...] = mm
        nv[...] = vv

    blk = pl.BlockSpec((tr, cols), lambda i: (i, 0))
    o = jax.ShapeDtypeStruct((rows, cols), F32)
    d, nm, nv = pl.pallas_call(
        body, name=name, grid=(rows // tr,), in_specs=[blk] * 4, out_specs=[blk] * 3, out_shape=[o, o, o],
        compiler_params=_cp("parallel"),
    )(w2, g2, m2, v2)
    return d.reshape(shape), nm.reshape(shape), nv.reshape(shape)


def _place():
    return lax.axis_index("x"), lax.axis_index("y"), lax.axis_index("c")


def allgather8(v, name):
    mp, n = v.shape

    def body(x_ref, out_ref, send_sems, recv_sems, local_sem):
        x, y, c = _place()
        me, sibling = (x, y, c), (x, y, 1 - c)
        chips = [(1 - x, y), (x, 1 - y), (1 - x, 1 - y)]

        def rows(px, py, pc):
            return out_ref.at[pl.ds((4 * px + 2 * py + pc) * mp, mp), :]

        def copy(k, block, to, src=None):
            return pltpu.make_async_remote_copy(
                src_ref=rows(*block) if src is None else src, dst_ref=rows(*block),
                send_sem=send_sems.at[k], recv_sem=recv_sems.at[k], device_id=to, device_id_type=MESH)

        mine = pltpu.make_async_copy(x_ref, rows(*me), local_sem)
        mine.start()
        first = [copy(0, me, sibling, src=x_ref)]
        first += [copy(1 + j, me, (*chip, c), src=x_ref) for j, chip in enumerate(chips)]
        for cp in first:
            cp.start()
        passed = [copy(4 + j, (*chip, c), sibling) for j, chip in enumerate(chips)]
        for j, chip in enumerate(chips):
            copy(1 + j, (*chip, c), me).wait_recv()
            passed[j].start()
        copy(0, sibling, me).wait_recv()
        for j, chip in enumerate(chips):
            copy(4 + j, (*chip, 1 - c), me).wait_recv()
        for cp in first + passed:
            cp.wait_send()
        mine.wait()

    return pl.pallas_call(
        body, name=name, out_shape=jax.ShapeDtypeStruct((8 * mp, n), v.dtype),
        in_specs=[pl.BlockSpec(memory_space=pltpu.VMEM)], out_specs=pl.BlockSpec(memory_space=pltpu.VMEM),
        scratch_shapes=[pltpu.SemaphoreType.DMA((7,)), pltpu.SemaphoreType.DMA((7,)), pltpu.SemaphoreType.DMA],
        compiler_params=pltpu.CompilerParams(vmem_limit_bytes=VMEM_LIMIT),
    )(v)


def sum8(v, mp):
    def body(x_ref, o_ref):
        acc = x_ref[0:mp, :]
        for d in range(1, 8):
            acc = acc + x_ref[d * mp:(d + 1) * mp, :]
        o_ref[...] = acc

    return pl.pallas_call(body, name="sum8", out_shape=jax.ShapeDtypeStruct((mp, v.shape[1]), F32),
                          compiler_params=pltpu.CompilerParams(vmem_limit_bytes=VMEM_LIMIT))(v)


_ANY = pl.BlockSpec(memory_space=pl.ANY)


def _row_half(ref, lead, h):
    hr = ref.shape[-2] // 2
    return ref.at[(*lead, pl.ds(h * hr, hr), slice(None))]


def allgather_weights(slots):
    n = len(slots)

    def body(*refs):
        s_refs, o_refs, (send_sems, recv_sems) = refs[:n], refs[n:2 * n], refs[2 * n:]
        x, y, c = _place()
        me = 2 * x + y
        sibling = (x, y, 1 - c)
        chips = [(1 - x, y), (x, 1 - y), (1 - x, 1 - y)]

        def half(ref, slot, h):
            return _row_half(ref, (slot, slice(None)), h)

        def copy(k, src, dst, to):
            return pltpu.make_async_remote_copy(src_ref=src, dst_ref=dst, send_sem=send_sems.at[k],
                                                recv_sem=recv_sems.at[k], device_id=to, device_id_type=MESH)

        first = [copy(6 * a + k, half(s_refs[a], me, c), half(o_refs[a], me, c), (px, py, c))
                 for k, (px, py) in enumerate(chips) for a in range(n)]
        for cp in first:
            cp.start()
        passed = []
        for k, (px, py) in enumerate(chips):
            for a in range(n):
                got = half(o_refs[a], 2 * px + py, c)
                copy(6 * a + k, half(s_refs[a], me, c), got, (px, py, c)).wait_recv()
                cp = copy(6 * a + 3 + k, got, got, sibling)
                cp.start()
                passed.append(cp)
        for k, (px, py) in enumerate(chips):
            for a in range(n):
                got = half(o_refs[a], 2 * px + py, 1 - c)
                copy(6 * a + 3 + k, got, got, sibling).wait_recv()
        for cp in first + passed:
            cp.wait_send()

    return pl.pallas_call(
        body, name="allgather_weights", out_shape=[jax.ShapeDtypeStruct(s.shape, s.dtype) for s in slots],
        in_specs=[_ANY] * n, out_specs=[_ANY] * n, input_output_aliases={a: a for a in range(n)},
        scratch_shapes=[pltpu.SemaphoreType.DMA((6 * n,)), pltpu.SemaphoreType.DMA((6 * n,))],
    )(*slots)


_HBM = pl.BlockSpec(memory_space=pltpu.HBM)
_SEM = pl.BlockSpec(memory_space=pltpu.SEMAPHORE)
_EFFECT = pltpu.SideEffectType.DATAFLOW_SIDE_EFFECTING


def gather_start(slots, name):
    n = len(slots)

    def body(*refs):
        s_refs, send_sems, recv_sems, token = refs[:n], refs[n], refs[n + 1], refs[-1]
        x, y, c = _place()
        me = 2 * x + y
        for k, (px, py) in enumerate([(1 - x, y), (x, 1 - y), (1 - x, 1 - y)]):
            for a in range(n):
                pltpu.make_async_remote_copy(
                    src_ref=s_refs[a].at[me], dst_ref=s_refs[a].at[me], send_sem=send_sems.at[3 * a + k],
                    recv_sem=recv_sems.at[3 * a + k], device_id=(px, py, c), device_id_type=MESH).start()
        token[...] = jnp.zeros_like(token)

    out = pl.pallas_call(
        body, name=name,
        out_shape=(pltpu.SemaphoreType.DMA((3 * n,)), pltpu.SemaphoreType.DMA((3 * n,)),
                   *[pltpu.HBM(s.shape, s.dtype) for s in slots], jax.ShapeDtypeStruct((8, 128), F32)),
        in_specs=[_HBM] * n, out_specs=(_SEM, _SEM, *[_HBM] * n, pl.BlockSpec(memory_space=pltpu.VMEM)),
        input_output_aliases={a: 2 + a for a in range(n)},
        compiler_params=pltpu.CompilerParams(has_side_effects=_EFFECT),
    )(*[pltpu.with_memory_space_constraint(s, pltpu.HBM) for s in slots])
    return out[0], out[1], list(out[2:2 + n]), out[-1]


def gather_wait(send_sems, recv_sems, slots, after, name):
    n = len(slots)

    def body(*refs):
        s_refs, ssem, rsem = refs[:n], refs[n], refs[n + 1]
        x, y, c = _place()
        me = 2 * x + y
        for k, (px, py) in enumerate([(1 - x, y), (x, 1 - y), (1 - x, 1 - y)]):
            for a in range(n):
                cp = pltpu.make_async_remote_copy(
                    src_ref=s_refs[a].at[me], dst_ref=s_refs[a].at[2 * px + py], send_sem=ssem.at[3 * a + k],
                    recv_sem=rsem.at[3 * a + k], device_id=(px, py, c), device_id_type=MESH)
                cp.wait_send()
                cp.wait_recv()

    return pl.pallas_call(
        body, name=name, out_shape=[pltpu.HBM(s.shape, s.dtype) for s in slots],
        in_specs=[_HBM] * n + [_SEM, _SEM, _ANY], out_specs=[_HBM] * n,
        input_output_aliases={a: a for a in range(n)},
        compiler_params=pltpu.CompilerParams(has_side_effects=_EFFECT),
    )(*slots, send_sems, recv_sems, after)


def rs_sibling_halves(gs):
    n = len(gs)

    def body(*refs):
        g_refs, r_refs, (send_sems, recv_sems) = refs[:n], refs[n:2 * n], refs[2 * n:]
        x, y, c = _place()
        cps = [pltpu.make_async_remote_copy(
            src_ref=_row_half(g_refs[a], (slice(None), slice(None)), 1 - c), dst_ref=r_refs[a],
            send_sem=send_sems.at[a], recv_sem=recv_sems.at[a], device_id=(x, y, 1 - c), device_id_type=MESH)
            for a in range(n)]
        for cp in cps:
            cp.start()
        for cp in cps:
            cp.wait()

    return pl.pallas_call(
        body, name="rs_sibling_halves",
        out_shape=[jax.ShapeDtypeStruct((*g.shape[:2], g.shape[2] // 2, g.shape[3]), F32) for g in gs],
        in_specs=[_ANY] * n, out_specs=[_ANY] * n,
        scratch_shapes=[pltpu.SemaphoreType.DMA((n,)), pltpu.SemaphoreType.DMA((n,))],
    )(*gs)


def rs_chipsum16(g, recv1, cidx, name):
    nl, hr, cols = recv1.shape[1:]

    def body(c_ref, g_ref, r_ref, o_ref):
        o_ref[...] = (g_ref[...] + r_ref[...]).astype(BF16)

    blk = (1, 1, hr, cols)
    return pl.pallas_call(
        body, name=name, out_shape=jax.ShapeDtypeStruct(recv1.shape, BF16),
        grid_spec=pltpu.PrefetchScalarGridSpec(
            num_scalar_prefetch=1, grid=(4, nl),
            in_specs=[pl.BlockSpec(blk, lambda j, l, c: (j, l, c[0], 0)), pl.BlockSpec(blk, lambda j, l, c: (j, l, 0, 0))],
            out_specs=pl.BlockSpec(blk, lambda j, l, c: (j, l, 0, 0))),
        compiler_params=_cp("parallel", "parallel"),
    )(cidx, g, recv1)


def rs_exchange_chips(cs):
    n = len(cs)

    def body(*refs):
        s_refs, r_refs, (send_sems, recv_sems) = refs[:n], refs[n:2 * n], refs[2 * n:]
        x, y, c = _place()
        chips = [(1 - x, y), (x, 1 - y), (1 - x, 1 - y)]
        cps = [pltpu.make_async_remote_copy(
            src_ref=s_refs[a].at[2 * px + py], dst_ref=r_refs[a].at[k], send_sem=send_sems.at[3 * a + k],
            recv_sem=recv_sems.at[3 * a + k], device_id=(px, py, c), device_id_type=MESH)
            for k, (px, py) in enumerate(chips) for a in range(n)]
        for cp in cps:
            cp.start()
        for cp in cps:
            cp.wait()

    return pl.pallas_call(
        body, name="rs_exchange_chips", out_shape=[jax.ShapeDtypeStruct((3, *s.shape[1:]), BF16) for s in cs],
        in_specs=[_ANY] * n, out_specs=[_ANY] * n,
        scratch_shapes=[pltpu.SemaphoreType.DMA((3 * n,)), pltpu.SemaphoreType.DMA((3 * n,))],
    )(*cs)


def exchange_start(cs, name):
    n = len(cs)
    lands = [lax.empty((3, *c.shape[1:]), BF16) for c in cs]

    def body(*refs):
        s_refs, l_refs, send_sems, recv_sems, token = refs[:n], refs[n:2 * n], refs[2 * n], refs[2 * n + 1], refs[-1]
        x, y, c = _place()
        for k, (px, py) in enumerate([(1 - x, y), (x, 1 - y), (1 - x, 1 - y)]):
            for a in range(n):
                pltpu.make_async_remote_copy(
                    src_ref=s_refs[a].at[2 * px + py], dst_ref=l_refs[a].at[k], send_sem=send_sems.at[3 * a + k],
                    recv_sem=recv_sems.at[3 * a + k], device_id=(px, py, c), device_id_type=MESH).start()
        token[...] = jnp.zeros_like(token)

    bufs = list(cs) + lands
    out = pl.pallas_call(
        body, name=name,
        out_shape=(pltpu.SemaphoreType.DMA((3 * n,)), pltpu.SemaphoreType.DMA((3 * n,)),
                   *[pltpu.HBM(b.shape, b.dtype) for b in bufs], jax.ShapeDtypeStruct((8, 128), F32)),
        in_specs=[_HBM] * (2 * n), out_specs=(_SEM, _SEM, *[_HBM] * (2 * n), pl.BlockSpec(memory_space=pltpu.VMEM)),
        input_output_aliases={i: 2 + i for i in range(2 * n)},
        compiler_params=pltpu.CompilerParams(has_side_effects=_EFFECT),
    )(*[pltpu.with_memory_space_constraint(b, pltpu.HBM) for b in bufs])
    return out[0], out[1], list(out[2:2 + n]), list(out[2 + n:2 + 2 * n]), out[-1]


def exchange_wait(send_sems, recv_sems, cs, lands, after, name):
    n = len(cs)

    def body(*refs):
        s_refs, l_refs, ssem, rsem = refs[:n], refs[n:2 * n], refs[2 * n], refs[2 * n + 1]
        x, y, c = _place()
        for k, (px, py) in enumerate([(1 - x, y), (x, 1 - y), (1 - x, 1 - y)]):
            for a in range(n):
                cp = pltpu.make_async_remote_copy(
                    src_ref=s_refs[a].at[2 * px + py], dst_ref=l_refs[a].at[k], send_sem=ssem.at[3 * a + k],
                    recv_sem=rsem.at[3 * a + k], device_id=(px, py, c), device_id_type=MESH)
                cp.wait_send()
                cp.wait_recv()

    bufs = list(cs) + list(lands)
    out = pl.pallas_call(
        body, name=name, out_shape=[pltpu.HBM(b.shape, b.dtype) for b in bufs],
        in_specs=[_HBM] * (2 * n) + [_SEM, _SEM, _ANY], out_specs=[_HBM] * (2 * n),
        input_output_aliases={i: i for i in range(2 * n)},
        compiler_params=pltpu.CompilerParams(has_side_effects=_EFFECT),
    )(*bufs, send_sems, recv_sems, after)
    return list(out[n:])


def rs_final_sum(g, recv1, recv2, idx, name):
    nl, hr, cols = recv1.shape[1:]

    def body(i_ref, g_ref, r1_ref, r2_ref, o_ref):
        acc = g_ref[0, 0] + r1_ref[0, 0]
        for k in range(3):
            acc = acc + r2_ref[k, 0].astype(F32)
        o_ref[0] = acc

    blk = (1, 1, hr, cols)
    return pl.pallas_call(
        body, name=name, out_shape=jax.ShapeDtypeStruct((nl, 2 * hr, cols), F32),
        grid_spec=pltpu.PrefetchScalarGridSpec(
            num_scalar_prefetch=1, grid=(nl,),
            in_specs=[pl.BlockSpec(blk, lambda l, ix: (ix[0], l, ix[1], 0)), pl.BlockSpec(blk, lambda l, ix: (ix[0], l, 0, 0)),
                      pl.BlockSpec((3, 1, hr, cols), lambda l, ix: (0, l, 0, 0))],
            out_specs=pl.BlockSpec((1, hr, cols), lambda l, ix: (l, ix[1], 0))),
        compiler_params=_cp("parallel"),
    )(idx, g, recv1, recv2)


def rs_share_halves(fulls):
    n = len(fulls)

    def body(*refs):
        h_refs, o_refs, (send_sems, recv_sems) = refs[:n], refs[n:2 * n], refs[2 * n:]
        x, y, c = _place()
        sibling = (x, y, 1 - c)

        def copy(a, h):
            return pltpu.make_async_remote_copy(
                src_ref=_row_half(h_refs[a], (slice(None),), h), dst_ref=_row_half(o_refs[a], (slice(None),), h),
                send_sem=send_sems.at[a], recv_sem=recv_sems.at[a], device_id=sibling, device_id_type=MESH)

        for a in range(n):
            copy(a, c).start()
        for a in range(n):
            copy(a, c).wait_send()
            copy(a, 1 - c).wait_recv()

    return pl.pallas_call(
        body, name="rs_share_halves", out_shape=[jax.ShapeDtypeStruct(f.shape, F32) for f in fulls],
        in_specs=[_ANY] * n, out_specs=[_ANY] * n, input_output_aliases={a: a for a in range(n)},
        scratch_shapes=[pltpu.SemaphoreType.DMA((n,)), pltpu.SemaphoreType.DMA((n,))],
    )(*fulls)


def _own_slot(shard, chip, dtype):
    return lax.dynamic_update_slice(lax.empty((4, *shard.shape), dtype), shard.astype(dtype)[None],
                                    (chip,) + (0,) * shard.ndim)


def kernel(x, norm_mix_pre, norm_mix_post, norm_ffn_pre, norm_ffn_post, w_in, conv_a, gate_up_fwd, gate_bias_fwd, gate_up_bwd, gate_bias_bwd, gla_head_norm, w_out, w_up, conv_ffn, w_down, loss_target, m_norm_mix_pre, m_norm_mix_post, m_norm_ffn_pre, m_norm_ffn_post, m_w_in, m_conv_a, m_gate_up_fwd, m_gate_bias_fwd, m_gate_up_bwd, m_gate_bias_bwd, m_gla_head_norm, m_w_out, m_w_up, m_conv_ffn, m_w_down, v_norm_mix_pre, v_norm_mix_post, v_norm_ffn_pre, v_norm_ffn_post, v_w_in, v_conv_a, v_gate_up_fwd, v_gate_bias_fwd, v_gate_up_bwd, v_gate_bias_bwd, v_gla_head_norm, v_w_out, v_w_up, v_conv_ffn, v_w_down):
    L = x.shape[1]
    xi, yi, ci = _place()
    chip = 2 * xi + yi
    tl_gla, tl_mix, tl_ffn = min(L, TL_GLA), min(L, TL_MIX), min(L, TL_FFN)

    big_w = (w_in, w_out, w_up, w_down)
    started = [gather_start([_own_slot(w[l], chip, BF16) for w in big_w], f"gather_start_{l}") for l in range(1, DEPTH)]
    tokens = [s[3] for s in started]
    gathered = [[a[:, 0] for a in allgather_weights([_own_slot(w[0:1], chip, BF16) for w in big_w])]]

    def layer_weights(bufs):
        a_in, a_out, a_up, a_down = bufs
        w_in_l = jnp.pad(jnp.concatenate([a_in[j] for j in range(4)], axis=1), ((0, 0), (0, DINP - DIN)))
        return w_in_l, a_out.reshape(D, D), a_up, a_down.reshape(DFF, D)

    small = jnp.concatenate([conv_a.reshape(-1), gate_up_fwd.reshape(-1), gate_up_bwd.reshape(-1), conv_ffn.reshape(-1)])
    ms = small.shape[0] // 128
    sg = allgather8(small.reshape(ms, 128), "allgather_small_weights").reshape(4, 2, ms * 128)[:, 0]

    def small_full(off, shape):
        n = shape[0] * shape[1] * shape[2]
        return jnp.concatenate([sg[j, off:off + n].reshape(shape) for j in range(4)], axis=2)

    o1 = DEPTH * 3 * 128
    o2 = o1 + DEPTH * RK * 64
    o3 = o2 + DEPTH * RK * 64
    conv_a_f = small_full(0, (DEPTH, 3, 128))
    gup_f = small_full(o1, (DEPTH, RK, 64))
    gup_b = small_full(o2, (DEPTH, RK, 64))
    conv_ffn_f = small_full(o3, (DEPTH, 3, 1408))

    def gcat_of(l):
        g = jnp.zeros((LRW, 2 * DK), F32)
        g = g.at[0:RK, 0:DK].set(gup_f[l]).at[RK:2 * RK, DK:2 * DK].set(gup_b[l])
        return g.astype(BF16)

    gcats = [gcat_of(l) for l in range(DEPTH)]
    gbiases = [jnp.concatenate([gate_bias_fwd[l], gate_bias_bwd[l]])[None, :] for l in range(DEPTH)]
    ghn4s = [jnp.tile(gla_head_norm[l], NH)[None, :] for l in range(DEPTH)]

    xc = x.reshape(L, D)
    saved = []
    W_in, W_out, W_up, W_down = [], [], [], []
    for l in range(DEPTH):
        if l > 0:
            ssem, rsem, bufs, _ = started[l - 1]
            gathered.append(gather_wait(ssem, rsem, bufs, xc, f"gather_wait_{l}"))
        for lst, w in zip((W_in, W_out, W_up, W_down), layer_weights(gathered[l])):
            lst.append(w)
        P, h1 = rms_matmul(xc, norm_mix_pre[l][None, :], W_in[l], 640, "proj_in", after=tokens if l == 0 else ())
        o_f, o_b, sf, sb = gla_fwd(P, gcats[l], gbiases[l], tl_gla)
        ycat, y, x1 = mix_out(P, o_f, o_b, conv_a_f[l], ghn4s[l], W_out[l], norm_mix_post[l][None, :], xc, tl_mix)
        U, h2 = rms_matmul(x1, norm_ffn_pre[l][None, :], W_up[l], WFF, "proj_up", n_out=2 * DFF, out_dtype=BF16,
                           w_spec=pl.BlockSpec((None, D, WFF), lambda i, j: (j, 0, 0)))
        y2, x2 = ffn_down(U, conv_ffn_f[l], W_down[l], norm_ffn_post[l][None, :], x1, tl_ffn)
        saved.append(dict(x=xc, h1=h1, P=P, o_f=o_f, o_b=o_b, sf=sf, sb=sb, ycat=ycat, y=y, x1=x1, h2=h2, U=U, y2=y2))
        xc = x2

    dx, loss_blk = loss_head(xc, loss_target.reshape(L, D), tl_mix)

    big = ("w_in", "w_out", "w_up", "w_down")
    cidx = jnp.reshape(ci, (1,)).astype(jnp.int32)
    idx = jnp.stack([chip, ci]).astype(jnp.int32)
    grads = [None] * DEPTH
    reduced = [None] * DEPTH
    tl_dw = min(L, 1024)
    pending = None
    token = ()

    def finish(pend, after):
        lp, gs_p, recv1_p, (ssem, rsem, cs_thru, lands, _) = pend
        recv2 = exchange_wait(ssem, rsem, cs_thru, lands, after, f"exchange_wait_{lp}")
        halves = [rs_final_sum(g, r1, r2, idx, "rs_final_sum_" + k) for g, r1, r2, k in zip(gs_p, recv1_p, recv2, big)]
        reduced[lp] = rs_share_halves(halves)

    for l in reversed(range(DEPTH)):
        s = saved[l]
        dy2, dg4 = rms_bwd_pre(dx, s["y2"], norm_ffn_post[l][None, :], tl_mix, after=token)
        du_g, du_v, z, dcf_g, dcf_v = ffn_bwd1(dy2, s["U"], conv_ffn_f[l], W_down[l], tl_ffn)
        g_down = matmul_tn(z, dy2, DFF // 2, D, tl_dw, "dw_down").reshape(4, 1, DFF // 4, D)
        dU_g, dU_v, dx1, dg3 = ffn_bwd2(du_g, du_v, conv_ffn_f[l], W_up[l], s["x1"], dx, norm_ffn_pre[l][None, :],
                                        min(L, TL_FFN2))
        g_up = matmul_tn(s["h2"], dU_g, D, WFF, tl_dw, "dw_up_gate",
                         into=(lax.empty((4, 1, D, WFF), F32), (None, None, D, WFF), lambda p, q: (q, 0, 0, 0)))
        g_up = matmul_tn(s["h2"], dU_v, D, WFF, tl_dw, "dw_up_val",
                         into=(g_up, (None, None, D, WFF), lambda p, q: (NFF + q, 0, 0, 0)))
        dy, dg2 = rms_bwd_pre(dx1, s["y"], norm_mix_post[l][None, :], tl_mix)
        dgb, dcc, dgo, do, dca, dghn = mix_bwd1(dy, W_out[l], s["P"], s["o_f"], s["o_b"], conv_a_f[l], ghn4s[l], tl_mix)
        g_out = matmul_tn(s["ycat"], dy, D, D, tl_dw, "dw_out").reshape(4, 1, D // 4, D)
        gl = gla_bwd(s["P"], do, s["sf"], s["sb"], gcats[l], gbiases[l], tl_gla)
        dP, dx, dg1, dgcat, dbias = mix_bwd2(dgb, dcc, dgo, gl, s["P"], conv_a_f[l], gcats[l], gbiases[l], W_in[l],
                                             s["x"], dx1, norm_mix_pre[l][None, :], tl_mix)
        dW_in = matmul_tn(s["h1"], dP, D, 640, tl_dw, "dw_in")
        g_in = jnp.stack([dW_in[:, (DIN // 4) * j:(DIN // 4) * (j + 1)] for j in range(4)])[:, None]
        grads[l] = dict(
            norm_mix_pre=dg1[0], norm_mix_post=dg2[0], norm_ffn_pre=dg3[0], norm_ffn_post=dg4[0],
            conv_a=dca[0:3], gate_up_fwd=dgcat[0:RK, 0:DK], gate_bias_fwd=dbias[0, 0:DK],
            gate_up_bwd=dgcat[RK:2 * RK, DK:2 * DK], gate_bias_bwd=dbias[0, DK:2 * DK], gla_head_norm=dghn[0],
            conv_ffn=jnp.concatenate([dcf_g[0:3], dcf_v[0:3]], axis=1))
        gs = [g_in, g_out, g_up, g_down]
        recv1 = rs_sibling_halves(gs)
        cs16 = [rs_chipsum16(g, r, cidx, "rs_chipsum16_" + k) for g, r, k in zip(gs, recv1, big)]
        flight = exchange_start(cs16, f"exchange_start_{l}")
        token = (flight[4],)
        if pending is not None:
            finish(pending, flight[4])
        pending = (l, gs, recv1, flight)
    finish(pending, pending[3][4])

    G = {k: jnp.stack([grads[l][k] for l in range(DEPTH)]) for k in grads[0]}

    small_names = ["norm_mix_pre", "norm_mix_post", "norm_ffn_pre", "norm_ffn_post", "conv_a", "gate_up_fwd",
                   "gate_bias_fwd", "gate_up_bwd", "gate_bias_bwd", "gla_head_norm", "conv_ffn"]
    flat = jnp.concatenate([G[k].reshape(-1) for k in small_names] + [loss_blk[0, 0:1]])
    n_small = flat.shape[0]
    mp = -(-n_small // 1024) * 8
    flat = jnp.pad(flat, (0, mp * 128 - n_small)).reshape(mp, 128)
    tot = sum8(allgather8(flat, "allgather_small_grads"), mp).reshape(-1)
    gsm = {}
    o = 0
    for k in small_names:
        n = G[k].size
        gsm[k] = tot[o:o + n].reshape(G[k].shape)
        o += n
    loss = tot[o]

    def my_cols(a, width):
        return lax.dynamic_slice_in_dim(a, chip * width, width, axis=2)

    gsm["conv_a"] = my_cols(gsm["conv_a"], 128)
    gsm["gate_up_fwd"] = my_cols(gsm["gate_up_fwd"], 64)
    gsm["gate_up_bwd"] = my_cols(gsm["gate_up_bwd"], 64)
    gsm["conv_ffn"] = my_cols(gsm["conv_ffn"], 1408)

    for a, k in enumerate(big):
        gsm[k] = jnp.concatenate([reduced[l][a] for l in range(DEPTH)], axis=0)

    names = ["norm_mix_pre", "norm_mix_post", "norm_ffn_pre", "norm_ffn_post", "w_in", "conv_a", "gate_up_fwd",
             "gate_bias_fwd", "gate_up_bwd", "gate_bias_bwd", "gla_head_norm", "w_out", "w_up", "conv_ffn", "w_down"]
    w = dict(norm_mix_pre=norm_mix_pre, norm_mix_post=norm_mix_post, norm_ffn_pre=norm_ffn_pre, norm_ffn_post=norm_ffn_post,
             w_in=w_in, conv_a=conv_a, gate_up_fwd=gate_up_fwd, gate_bias_fwd=gate_bias_fwd, gate_up_bwd=gate_up_bwd,
             gate_bias_bwd=gate_bias_bwd, gla_head_norm=gla_head_norm, w_out=w_out, w_up=w_up, conv_ffn=conv_ffn, w_down=w_down)
    m = dict(norm_mix_pre=m_norm_mix_pre, norm_mix_post=m_norm_mix_post, norm_ffn_pre=m_norm_ffn_pre, norm_ffn_post=m_norm_ffn_post,
             w_in=m_w_in, conv_a=m_conv_a, gate_up_fwd=m_gate_up_fwd, gate_bias_fwd=m_gate_bias_fwd, gate_up_bwd=m_gate_up_bwd,
             gate_bias_bwd=m_gate_bias_bwd, gla_head_norm=m_gla_head_norm, w_out=m_w_out, w_up=m_w_up, conv_ffn=m_conv_ffn, w_down=m_w_down)
    v = dict(norm_mix_pre=v_norm_mix_pre, norm_mix_post=v_norm_mix_post, norm_ffn_pre=v_norm_ffn_pre, norm_ffn_post=v_norm_ffn_post,
             w_in=v_w_in, conv_a=v_conv_a, gate_up_fwd=v_gate_up_fwd, gate_bias_fwd=v_gate_bias_fwd, gate_up_bwd=v_gate_up_bwd,
             gate_bias_bwd=v_gate_bias_bwd, gla_head_norm=v_gla_head_norm, w_out=v_w_out, w_up=v_w_up, conv_ffn=v_conv_ffn, w_down=v_w_down)
    upd = {k: adamw(w[k], gsm[k], m[k], v[k], "adamw_" + k) for k in names}
    return (loss, dx.reshape(1, L, D), *[gsm[k] for k in names], *[upd[k][0] for k in names],
            *[upd[k][1] for k in names], *[upd[k][2] for k in names])
```

```python
import functools

import jax
import jax.numpy as jnp
from jax import lax
from jax.experimental import pallas as pl
from jax.experimental.pallas import tpu as pltpu

F32 = jnp.float32
BF16 = jnp.bfloat16
MXU_DTYPE = jnp.bfloat16
MESH = pl.DeviceIdType.MESH

D = 1024
DC = 512
DG = 512
NH = 4
HV = 128
HK = 64
DK = 256
RK = 16
CH = 64
DFF = 2816
DIN = 3104
DINP = 3200
LRW = 128
DEPTH = 4
EPS = 1e-6
QSCALE = HK ** -0.5
GATE_NORM = 1.0 / 16.0
CB_GB, CB_GC, CB_GV, CB_GO = 0, 1, 2, 5
CB_Q, CB_K = 6, 7
CB_V = 4
CB_LR = 24
LR = 0.001
B1 = 0.9
B2 = 0.999
AEPS = 1e-08
WD = 0.01
STEP = 10
TM_PROJ = 1024
TL_GLA = 512
TL_MIX = 256
TL_FFN = 256
TL_FFN2 = 512
VMEM_LIMIT = 56 * 1024 * 1024


def _cp(*sem):
    return pltpu.CompilerParams(dimension_semantics=sem if sem else None, vmem_limit_bytes=VMEM_LIMIT)


def _mm(a, b):
    return jnp.dot(a.astype(MXU_DTYPE), b.astype(MXU_DTYPE), preferred_element_type=F32)


def _mm_nt(a, b):
    return lax.dot_general(a.astype(MXU_DTYPE), b.astype(MXU_DTYPE), (((1,), (1,)), ((), ())),
                           preferred_element_type=F32)


def _mm_tn(a, b):
    return lax.dot_general(a.astype(MXU_DTYPE), b.astype(MXU_DTYPE), (((0,), (0,)), ((), ())),
                           preferred_element_type=F32)


def _mm_tri(tri, b):
    t = tri.astype(BF16)
    b1 = b.astype(BF16)
    r1 = b - b1.astype(F32)
    b2 = r1.astype(BF16)
    b3 = (r1 - b2.astype(F32)).astype(BF16)
    dot = lambda u: jnp.dot(t, u, preferred_element_type=F32)
    return dot(b1) + dot(b2) + dot(b3)


def _rms(x, g):
    r = lax.rsqrt(jnp.mean(x * x, axis=-1, keepdims=True) + EPS)
    return x * r * g


def _rms_bwd(dout, y, g):
    r = lax.rsqrt(jnp.mean(y * y, axis=-1, keepdims=True) + EPS)
    yh = y * r
    dyh = dout * g
    dy = r * (dyh - yh * jnp.mean(dyh * yh, axis=-1, keepdims=True))
    dg = jnp.sum(dout * yh, axis=0, keepdims=True)
    return dy, dg


def _sigmoid(x):
    return 0.5 * jnp.tanh(0.5 * x) + 0.5


def _logsig(x):
    return jnp.minimum(x, 0.0) - jnp.log1p(jnp.exp(-jnp.abs(x)))


def _shifts(x, p8, n8):
    n = x.shape[0]
    xe = jnp.concatenate([p8, x, n8], axis=0)
    return pltpu.roll(xe, 1, 0)[8:8 + n], pltpu.roll(xe, n + 15, 0)[8:8 + n]


def _halo_rows(prev_ref, next_ref, i, last):
    hr = prev_ref.shape[0]
    p = jnp.where(i == 0, 0.0, prev_ref[...].astype(F32)[hr - 8:hr, :])
    n = jnp.where(i == last, 0.0, next_ref[...].astype(F32)[0:8, :])
    return p, n


def _conv3(x, xp, xn, w_ref):
    xm1, xp1 = _shifts(x, xp, xn)
    return w_ref[0:1, :] * xm1 + w_ref[1:2, :] * x + w_ref[2:3, :] * xp1, xm1, xp1


def _conv3_t(d, dp, dn, w_ref):
    dm1, dp1 = _shifts(d, dp, dn)
    return w_ref[0:1, :] * dp1 + w_ref[1:2, :] * d + w_ref[2:3, :] * dm1


HALO32 = 8
HALO16 = 16


def _prev_row_blk(i, tl, hr):
    return jnp.maximum(i * (tl // hr) - 1, 0)


def _next_row_blk(i, tl, nrows, hr):
    return jnp.minimum((i + 1) * (tl // hr), nrows // hr - 1)


def _prev_blk(tl, cb, hr=HALO32):
    return lambda i: (_prev_row_blk(i, tl, hr), cb)


def _next_blk(tl, nrows, cb, hr=HALO32):
    return lambda i: (_next_row_blk(i, tl, nrows, hr), cb)


def rms_matmul(x, g, w, tn, name, w_spec=None, n_out=None, out_dtype=F32, after=()):
    L = x.shape[0]
    N = w.shape[1] if n_out is None else n_out
    tm = min(L, TM_PROJ)
    if w_spec is None:
        w_spec = pl.BlockSpec((D, tn), lambda i, j: (0, j))

    def body(x_ref, g_ref, w_ref, *rest):
        o_ref, h_ref = rest[-2:]

        @pl.when(pl.program_id(1) == 0)
        def _():
            h_ref[...] = _rms(x_ref[...], g_ref[...]).astype(BF16)

        o_ref[...] = _mm(h_ref[...], w_ref[...]).astype(out_dtype)

    return pl.pallas_call(
        body, name=name, grid=(L // tm, N // tn),
        in_specs=[pl.BlockSpec((tm, D), lambda i, j: (i, 0)), pl.BlockSpec((1, D), lambda i, j: (0, 0)), w_spec]
        + [_ANY] * len(after),
        out_specs=[pl.BlockSpec((tm, tn), lambda i, j: (i, j)), pl.BlockSpec((tm, D), lambda i, j: (i, 0))],
        out_shape=[jax.ShapeDtypeStruct((L, N), out_dtype), jax.ShapeDtypeStruct((L, D), BF16)],
        compiler_params=_cp("parallel", "arbitrary"),
    )(x, g, w, *after)


def _gla_masks():
    def blk(shape, rdiv, cdiv):
        r = lax.broadcasted_iota(jnp.int32, shape, 0) // rdiv
        c = lax.broadcasted_iota(jnp.int32, shape, 1) // cdiv
        return (r == c).astype(F32)

    r = lax.broadcasted_iota(jnp.int32, (CH, CH), 0)
    c = lax.broadcasted_iota(jnp.int32, (CH, CH), 1)
    r4 = lax.broadcasted_iota(jnp.int32, (NH * CH, CH), 0) % CH
    c4 = lax.broadcasted_iota(jnp.int32, (NH * CH, CH), 1)
    return dict(
        bdq=blk((NH * CH, DK), CH, HK),
        bdo=blk((NH * CH, DG), CH, HV),
        bds=blk((DG, DK), HV, HK),
        tril=(r >= c).astype(F32), triu=(r <= c).astype(F32),
        tril4=r4 >= c4, triu4=r4 <= c4,
    )


def _tile4(x):
    return jnp.concatenate([x, x, x, x], axis=0)


def _gla_tile_prep(q, k, a, m, rev, nc):
    tri = m["triu"] if rev else m["tril"]
    chunks = [a[c * CH:(c + 1) * CH] for c in range(nc)]
    cum = jnp.concatenate([_mm_tri(tri, ac) for ac in chunks], axis=0)
    tot = jnp.concatenate([jnp.sum(ac, axis=0, keepdims=True) for ac in chunks], axis=0)
    tot_rows = jnp.concatenate([jnp.broadcast_to(tot[c:c + 1], (CH, DK)) for c in range(nc)], axis=0)
    e = jnp.exp(cum)
    einv = jnp.exp(-cum)
    eout = jnp.exp(tot_rows - cum)
    return dict(e=e, einv=einv, eout=eout, dec=jnp.exp(tot), qt=q * QSCALE * e, kt=k * einv, kh=k * eout)


def _gla_scores(qt16, kt16, m, rev):
    qs = _tile4(qt16) * m["bdq"].astype(qt16.dtype)
    return qs, jnp.where(m["triu4"] if rev else m["tril4"], _mm_nt(qs, kt16), 0.0)


def _gla_chunk_fwd(qt16, kt16, kh16, v, dec, st_ref, m, rev):
    _, sc = _gla_scores(qt16, kt16, m, rev)
    v16 = v.astype(BF16)
    r = _mm(sc, v16)
    o_intra = jnp.concatenate([r[h * CH:(h + 1) * CH, h * HV:(h + 1) * HV] for h in range(NH)], axis=1)
    st = st_ref[...]
    st16 = st.astype(BF16)
    o = o_intra + _mm_nt(qt16, st16)
    st_ref[...] = st * dec + _mm_tn(v16, kh16) * m["bds"]
    return o, st16


def _gates(lr_ref, gc_ref, bs_ref, cols):
    return _logsig(_mm(lr_ref[...], gc_ref[:, cols]) + bs_ref[:, cols]) * GATE_NORM


def gla_fwd(P, gcat, gbias, tl):
    L = P.shape[0]
    nb = L // tl
    nc = tl // CH

    def body(qf, kf, vf, lf, qb, kb, vb, lb, gc_ref, bs_ref, of, ob, sf, sb, stf, stb,
             qtf, ktf, khf, dcf, qtb, ktb, khb, dcb):
        @pl.when(pl.program_id(0) == 0)
        def _():
            stf[...] = jnp.zeros_like(stf)
            stb[...] = jnp.zeros_like(stb)

        m = _gla_masks()
        for (q, k, lr, cols, rev, qt, kt, kh, dc) in ((qf, kf, lf, slice(0, DK), False, qtf, ktf, khf, dcf),
                                                      (qb, kb, lb, slice(DK, 2 * DK), True, qtb, ktb, khb, dcb)):
            p = _gla_tile_prep(q[...], k[...], _gates(lr, gc_ref, bs_ref, cols), m, rev, nc)
            qt[...] = p["qt"].astype(BF16)
            kt[...] = p["kt"].astype(BF16)
            kh[...] = p["kh"].astype(BF16)
            dc[...] = p["dec"]

        def chunk(c, carry):
            rows = pl.ds(pl.multiple_of(c * CH, CH), CH)
            o, st = _gla_chunk_fwd(qtf[rows, :], ktf[rows, :], khf[rows, :], vf[rows, :], dcf[pl.ds(c, 1), :], stf, m, False)
            of[rows, :] = o
            sf[c] = st
            cb = nc - 1 - c
            rows = pl.ds(pl.multiple_of(cb * CH, CH), CH)
            o, st = _gla_chunk_fwd(qtb[rows, :], ktb[rows, :], khb[rows, :], vb[rows, :], dcb[pl.ds(cb, 1), :], stb, m, True)
            ob[rows, :] = o
            sb[cb] = st
            return carry

        lax.fori_loop(0, nc, chunk, 0, unroll=2)

    fw = lambda cb: (lambda i: (i, cb))
    bw = lambda cb: (lambda i: (nb - 1 - i, cb))
    return pl.pallas_call(
        body, name="gla_fwd", grid=(nb,),
        in_specs=[pl.BlockSpec((tl, DK), fw(CB_Q)), pl.BlockSpec((tl, DK), fw(CB_K)), pl.BlockSpec((tl, DG), fw(CB_V)),
                  pl.BlockSpec((tl, LRW), fw(CB_LR)),
                  pl.BlockSpec((tl, DK), bw(CB_Q)), pl.BlockSpec((tl, DK), bw(CB_K)), pl.BlockSpec((tl, DG), bw(CB_V)),
                  pl.BlockSpec((tl, LRW), bw(CB_LR)),
                  pl.BlockSpec((LRW, 2 * DK), lambda i: (0, 0)), pl.BlockSpec((1, 2 * DK), lambda i: (0, 0))],
        out_specs=[pl.BlockSpec((tl, DG), lambda i: (i, 0)), pl.BlockSpec((tl, DG), lambda i: (nb - 1 - i, 0)),
                   pl.BlockSpec((nc, DG, DK), lambda i: (i, 0, 0)), pl.BlockSpec((nc, DG, DK), lambda i: (nb - 1 - i, 0, 0))],
        out_shape=[jax.ShapeDtypeStruct((L, DG), F32), jax.ShapeDtypeStruct((L, DG), F32),
                   jax.ShapeDtypeStruct((L // CH, DG, DK), BF16), jax.ShapeDtypeStruct((L // CH, DG, DK), BF16)],
        scratch_shapes=[pltpu.VMEM((DG, DK), F32), pltpu.VMEM((DG, DK), F32)]
        + [pltpu.VMEM((tl, DK), BF16)] * 3 + [pltpu.VMEM((nc, DK), F32)]
        + [pltpu.VMEM((tl, DK), BF16)] * 3 + [pltpu.VMEM((nc, DK), F32)],
        compiler_params=_cp("arbitrary"),
    )(P, P, P, P, P, P, P, P, gcat, gbias)


def _headnorm(o):
    oh, rs = [], []
    for h in range(NH):
        oo = o[:, h * HV:(h + 1) * HV]
        r = lax.rsqrt(jnp.mean(oo * oo, axis=-1, keepdims=True) + EPS)
        oh.append(oo * r)
        rs.append(r)
    return jnp.concatenate(oh, axis=1), rs


def mix_out(P, o_f, o_b, conv_a, ghn4, w_out, g2, x, tl):
    L = P.shape[0]
    nt = L // tl

    def body(gb, gc, gv, go, gcp, gvp, gcn, gvn, of, ob, ca, gh, wo, g2r, xr, ycat, yr, x1):
        i = pl.program_id(0)
        cp, cn = _halo_rows(gcp, gcn, i, nt - 1)
        vp, vn = _halo_rows(gvp, gvn, i, nt - 1)
        c = gc[...] * gv[...]
        cc, _, _ = _conv3(c, cp * vp, cn * vn, ca)
        ya = gb[...] * cc
        oh, _ = _headnorm(of[...] + ob[...])
        g = go[...]
        yb = g * _sigmoid(g) * (oh * gh[...])
        yc = jnp.concatenate([ya, yb], axis=1).astype(BF16)
        ycat[...] = yc
        y = _mm(yc, wo[...])
        yr[...] = y
        x1[...] = xr[...] + _rms(y, g2r[...])

    t = lambda cb: pl.BlockSpec((tl, DC), lambda i: (i, cb))
    hp = lambda cb: pl.BlockSpec((8, DC), _prev_blk(tl, cb))
    hn = lambda cb: pl.BlockSpec((8, DC), _next_blk(tl, L, cb))
    row = lambda n: pl.BlockSpec((tl, n), lambda i: (i, 0))
    full = lambda a: pl.BlockSpec(a.shape, lambda i: (0, 0))
    return pl.pallas_call(
        body, name="mix_out", grid=(nt,),
        in_specs=[t(CB_GB), t(CB_GC), t(CB_GV), t(CB_GO), hp(CB_GC), hp(CB_GV), hn(CB_GC), hn(CB_GV),
                  row(DG), row(DG), full(conv_a), full(ghn4), full(w_out), full(g2), row(D)],
        out_specs=[row(D), row(D), row(D)],
        out_shape=[jax.ShapeDtypeStruct((L, D), BF16), jax.ShapeDtypeStruct((L, D), F32),
                   jax.ShapeDtypeStruct((L, D), F32)],
        compiler_params=_cp("parallel"),
    )(P, P, P, P, P, P, P, P, o_f, o_b, conv_a, ghn4, w_out, g2, x)


NFF = 2
WFF = DFF // NFF
FFN_COL_CHUNKS = ((0, 512), (512, 1024), (1024, WFF))


def ffn_down(U, conv_ffn, w_down, g4, x1, tl):
    L = U.shape[0]
    nt = L // tl

    def body(u, up, un, cf, wd, g4r, x1r, y2, x2, ug, uv, zr):
        i = pl.program_id(0)
        acc = jnp.zeros((tl, D), F32)
        for j in range(NFF):
            gs = slice(j * WFF, (j + 1) * WFF)
            vs = slice(DFF + j * WFF, DFF + (j + 1) * WFF)
            z = []
            for s in (gs, vs):
                p, n = _halo_rows(up.at[:, s], un.at[:, s], i, nt - 1)
                z.append(_conv3(u[:, s].astype(F32), p, n, cf.at[:, s])[0])
            zz = (z[0] * _sigmoid(z[0]) * z[1]).astype(BF16)
            ug[:, gs] = z[0].astype(BF16)
            uv[:, gs] = z[1].astype(BF16)
            zr[:, gs] = zz
            acc = acc + _mm(zz, wd[gs, :])
        y2[...] = acc
        x2[...] = x1r[...] + _rms(acc, g4r[...])

    row = lambda n: pl.BlockSpec((tl, n), lambda i: (i, 0))
    full = lambda a: pl.BlockSpec(a.shape, lambda i: (0, 0))
    half = jax.ShapeDtypeStruct((L, DFF), BF16)
    return pl.pallas_call(
        body, name="ffn_down", grid=(nt,),
        in_specs=[row(2 * DFF), pl.BlockSpec((HALO16, 2 * DFF), _prev_blk(tl, 0, HALO16)),
                  pl.BlockSpec((HALO16, 2 * DFF), _next_blk(tl, L, 0, HALO16)),
                  full(conv_ffn), full(w_down), full(g4), row(D)],
        out_specs=[row(D), row(D), row(DFF), row(DFF), row(DFF)],
        out_shape=[jax.ShapeDtypeStruct((L, D), F32), jax.ShapeDtypeStruct((L, D), F32), half, half, half],
        compiler_params=_cp("parallel"),
    )(U, U, U, conv_ffn, w_down, g4, x1)


def loss_head(y, target, tl):
    L = y.shape[0]

    def body(yr, tr, dy, ls):
        @pl.when(pl.program_id(0) == 0)
        def _():
            ls[...] = jnp.zeros_like(ls)

        err = yr[...] - tr[...]
        dy[...] = err * (1.0 / D)
        ls[...] += (0.5 / D) * jnp.sum(err * err)

    row = pl.BlockSpec((tl, D), lambda i: (i, 0))
    return pl.pallas_call(
        body, name="loss_head", grid=(L // tl,), in_specs=[row, row],
        out_specs=[row, pl.BlockSpec((8, 128), lambda i: (0, 0))],
        out_shape=[jax.ShapeDtypeStruct((L, D), F32), jax.ShapeDtypeStruct((8, 128), F32)],
        compiler_params=_cp("arbitrary"),
    )(y, target)


def rms_bwd_pre(dout, y, g, tl, after=()):
    L = y.shape[0]

    def body(dr, yr, gr, *rest):
        dy, dg = rest[-2:]

        @pl.when(pl.program_id(0) == 0)
        def _():
            dg[...] = jnp.zeros_like(dg)

        a, b = _rms_bwd(dr[...], yr[...], gr[...])
        dy[...] = a.astype(BF16)
        dg[...] += b

    row = pl.BlockSpec((tl, D), lambda i: (i, 0))
    vec = pl.BlockSpec((1, D), lambda i: (0, 0))
    return pl.pallas_call(
        body, name="rms_bwd_pre", grid=(L // tl,), in_specs=[row, row, vec] + [_ANY] * len(after), out_specs=[row, vec],
        out_shape=[jax.ShapeDtypeStruct((L, D), BF16), jax.ShapeDtypeStruct((1, D), F32)],
        compiler_params=_cp("arbitrary"),
    )(dout, y, g, *after)


def ffn_bwd1(dy2, ug, uv, w_down, tl):
    L = ug.shape[0]

    def body(dy, ugr, uvr, wd, dug, duv):
        a = ugr[...].astype(F32)
        b = uvr[...].astype(F32)
        sg = _sigmoid(a)
        silu = a * sg
        dz = _mm_nt(dy[...], wd[...])
        dug[...] = (dz * b * (sg + silu * (1.0 - sg))).astype(BF16)
        duv[...] = (dz * silu).astype(BF16)

    tile = pl.BlockSpec((tl, WFF), lambda j, i: (i, j))
    half = jax.ShapeDtypeStruct((L, DFF), BF16)
    return pl.pallas_call(
        body, name="ffn_bwd1", grid=(NFF, L // tl),
        in_specs=[pl.BlockSpec((tl, D), lambda j, i: (i, 0)), tile, tile, pl.BlockSpec((WFF, D), lambda j, i: (j, 0))],
        out_specs=[tile, tile], out_shape=[half, half],
        compiler_params=_cp("parallel", "parallel"),
    )(dy2, ug, uv, w_down)


def ffn_bwd2(du_g, du_v, U, conv_ffn, w_up, x1, dres, g3, tl):
    L = x1.shape[0]
    nt = L // tl

    def body(dg_, dv_, dgp, dgn, dvp, dvn, ugr, uvr, cg, cv, wg, wv, x1r, drr, g3r, dUg, dUv, dx1, dg3, dcg, dcv, acc):
        i = pl.program_id(0)
        j = pl.program_id(1)

        @pl.when((i == 0) & (j == 0))
        def _():
            dg3[...] = jnp.zeros_like(dg3)
            dcg[...] = jnp.zeros_like(dcg)
            dcv[...] = jnp.zeros_like(dcv)

        part = None
        for c0, c1 in FFN_COL_CHUNKS:
            cs = slice(c0, c1)
            for d_ref, dp_ref, dn_ref, cw_ref, u_ref, dc, dU, w in ((dg_, dgp, dgn, cg, ugr, dcg, dUg, wg),
                                                                    (dv_, dvp, dvn, cv, uvr, dcv, dUv, wv)):
                p8, n8 = _halo_rows(dp_ref.at[:, cs], dn_ref.at[:, cs], i, nt - 1)
                d = d_ref[:, cs].astype(F32)
                dm1, dp1 = _shifts(d, p8, n8)
                du = (cw_ref[0:1, cs] * dp1 + cw_ref[1:2, cs] * d + cw_ref[2:3, cs] * dm1).astype(BF16)
                dU[:, cs] = du
                u = u_ref[:, cs].astype(F32)
                for k, t in enumerate((dp1, d, dm1)):
                    dc[j, k:k + 1, cs] += jnp.sum(t * u, axis=0, keepdims=True)
                term = _mm_nt(du, w[:, cs])
                part = term if part is None else part + term

        @pl.when(j == 0)
        def _():
            acc[...] = part

        @pl.when(j > 0)
        def _():
            acc[...] += part

        @pl.when(j == NFF - 1)
        def _():
            dx, dg = _rms_bwd(acc[...], x1r[...], g3r[...])
            dx1[...] = drr[...] + dx
            dg3[...] += dg

    tile = pl.BlockSpec((tl, WFF), lambda i, j: (i, j))
    prev = pl.BlockSpec((HALO16, WFF), lambda i, j: (_prev_row_blk(i, tl, HALO16), j))
    nxt = pl.BlockSpec((HALO16, WFF), lambda i, j: (_next_row_blk(i, tl, L, HALO16), j))
    cw = lambda off: pl.BlockSpec((3, WFF), lambda i, j: (0, off + j))
    ww = lambda off: pl.BlockSpec((None, D, WFF), lambda i, j: (off + j, 0, 0))
    row = pl.BlockSpec((tl, D), lambda i, j: (i, 0))
    vec = pl.BlockSpec((1, D), lambda i, j: (0, 0))
    ut = lambda off: pl.BlockSpec((tl, WFF), lambda i, j: (i, off + j))
    dcs = pl.BlockSpec((NFF, 8, WFF), lambda i, j: (0, 0, 0))
    return pl.pallas_call(
        body, name="ffn_bwd2", grid=(nt, NFF),
        in_specs=[tile, tile, prev, nxt, prev, nxt, ut(0), ut(NFF), cw(0), cw(NFF), ww(0), ww(NFF), row, row, vec],
        out_specs=[tile, tile, row, vec, dcs, dcs],
        out_shape=[jax.ShapeDtypeStruct((L, DFF), BF16), jax.ShapeDtypeStruct((L, DFF), BF16),
                   jax.ShapeDtypeStruct((L, D), F32), jax.ShapeDtypeStruct((1, D), F32),
                   jax.ShapeDtypeStruct((NFF, 8, WFF), F32), jax.ShapeDtypeStruct((NFF, 8, WFF), F32)],
        scratch_shapes=[pltpu.VMEM((tl, D), F32)],
        compiler_params=_cp("arbitrary", "arbitrary"),
    )(du_g, du_v, du_g, du_g, du_v, du_v, U, U, conv_ffn, conv_ffn, w_up, w_up, x1, dres, g3)


def matmul_tn(a, b, ta, tn, tl, name, into=None):
    L, Ka = a.shape
    N = b.shape[1]

    def body(ar, br, *rest):
        o = rest[-1]

        @pl.when(pl.program_id(2) == 0)
        def _():
            o[...] = jnp.zeros_like(o)

        o[...] += _mm_tn(ar[...], br[...]).reshape(o.shape)

    in_specs = [pl.BlockSpec((tl, ta), lambda p, q, l: (l, p)), pl.BlockSpec((tl, tn), lambda p, q, l: (l, q))]
    if into is None:
        return pl.pallas_call(
            body, name=name, grid=(Ka // ta, N // tn, L // tl), in_specs=in_specs,
            out_specs=pl.BlockSpec((ta, tn), lambda p, q, l: (p, q)),
            out_shape=jax.ShapeDtypeStruct((Ka, N), F32),
            compiler_params=_cp("parallel", "parallel", "arbitrary"),
        )(a, b)
    buf, blk, idx = into
    return pl.pallas_call(
        body, name=name, grid=(Ka // ta, N // tn, L // tl), in_specs=in_specs + [_ANY],
        out_specs=pl.BlockSpec(blk, lambda p, q, l: idx(p, q)),
        out_shape=jax.ShapeDtypeStruct(buf.shape, F32), input_output_aliases={2: 0},
        compiler_params=_cp("parallel", "parallel", "arbitrary"),
    )(a, b, buf)


def mix_bwd1(dy, w_out, P, o_f, o_b, conv_a, ghn4, tl):
    L = P.shape[0]
    nt = L // tl

    def body(dyr, wo, gb, gc, gv, go, gcp, gvp, gcn, gvn, of, ob, ca, gh, dgb, dcc, dgo, do, dca, dgh):
        i = pl.program_id(0)

        @pl.when(i == 0)
        def _():
            dca[...] = jnp.zeros_like(dca)
            dgh[...] = jnp.zeros_like(dgh)

        dycat = _mm_nt(dyr[...], wo[...])
        dya = dycat[:, 0:DC]
        dyb = dycat[:, DC:D]
        cp, cn = _halo_rows(gcp, gcn, i, nt - 1)
        vp, vn = _halo_rows(gvp, gvn, i, nt - 1)
        c = gc[...] * gv[...]
        cc, c_m1, c_p1 = _conv3(c, cp * vp, cn * vn, ca)
        dgb[...] = dya * cc
        d = dya * gb[...]
        dcc[...] = d
        for k, s in enumerate((c_m1, c, c_p1)):
            dca[k:k + 1, :] += jnp.sum(d * s, axis=0, keepdims=True)
        oh, rs = _headnorm(of[...] + ob[...])
        g = go[...]
        sg = _sigmoid(g)
        silu = g * sg
        dgo[...] = dyb * (oh * gh[...]) * (sg * (1.0 + g * (1.0 - sg)))
        don = dyb * silu
        t = jnp.sum(don * oh, axis=0, keepdims=True)
        dgh[0:1, :] += t[:, 0:HV] + t[:, HV:2 * HV] + t[:, 2 * HV:3 * HV] + t[:, 3 * HV:4 * HV]
        doh = don * gh[...]
        parts = []
        for h in range(NH):
            hs = slice(h * HV, (h + 1) * HV)
            parts.append(rs[h] * (doh[:, hs] - oh[:, hs] * jnp.mean(doh[:, hs] * oh[:, hs], axis=-1, keepdims=True)))
        do[...] = jnp.concatenate(parts, axis=1)

    t = lambda cb: pl.BlockSpec((tl, DC), lambda i: (i, cb))
    hp = lambda cb: pl.BlockSpec((8, DC), _prev_blk(tl, cb))
    hn = lambda cb: pl.BlockSpec((8, DC), _next_blk(tl, L, cb))
    row = lambda n: pl.BlockSpec((tl, n), lambda i: (i, 0))
    full = lambda a: pl.BlockSpec(a.shape, lambda i: (0, 0))
    f32o = lambda n: jax.ShapeDtypeStruct((L, n), F32)
    return pl.pallas_call(
        body, name="mix_bwd1", grid=(nt,),
        in_specs=[row(D), full(w_out), t(CB_GB), t(CB_GC), t(CB_GV), t(CB_GO), hp(CB_GC), hp(CB_GV), hn(CB_GC), hn(CB_GV),
                  row(DG), row(DG), full(conv_a), full(ghn4)],
        out_specs=[row(DC), row(DC), row(DG), row(DG), pl.BlockSpec((8, DC), lambda i: (0, 0)),
                   pl.BlockSpec((8, HV), lambda i: (0, 0))],
        out_shape=[f32o(DC), f32o(DC), f32o(DG), f32o(DG), jax.ShapeDtypeStruct((8, DC), F32),
                   jax.ShapeDtypeStruct((8, HV), F32)],
        compiler_params=_cp("arbitrary"),
    )(dy, w_out, P, P, P, P, P, P, P, P, o_f, o_b, conv_a, ghn4)


def _gla_chunk_bwd(qt, kt, kh, v, do, st16, dec, g_ref, m, rev):
    qt16, kt16, kh16, v16, do16 = (t.astype(BF16) for t in (qt, kt, kh, v, do))
    qs, sc = _gla_scores(qt16, kt16, m, rev)
    g = g_ref[...]
    g16 = g.astype(BF16)
    dob = _tile4(do16) * m["bdo"].astype(BF16)
    dv = _mm_tn(sc, dob) + _mm_nt(kh16, g16)
    dsc = jnp.where(m["triu4"] if rev else m["tril4"], _mm_nt(dob, v16), 0.0)
    r1 = _mm(dsc, kt16) * m["bdq"]
    dqt = r1[0:CH] + r1[CH:2 * CH] + r1[2 * CH:3 * CH] + r1[3 * CH:4 * CH] + _mm(do16, st16)
    dkt = _mm_tn(dsc, qs)
    dkh = _mm(v16, g16)
    dd = jnp.sum(g * st16.astype(F32), axis=0, keepdims=True)
    g_ref[...] = g * dec + _mm_tn(do16, qt16) * m["bds"]
    return dv, dqt, dkt, dkh, dd


def gla_bwd(P, do, sf, sb, gcat, gbias, tl):
    L = P.shape[0]
    nb = L // tl
    nc = tl // CH

    def body(qf, kf, vf, lf, dof, sfr, qb, kb, vb, lb, dob, sbr, gc_ref, bs_ref,
             dqf, dkf, dvf, daf, dqb, dkb, dvb, dab, gf, gbk, *scr):
        @pl.when(pl.program_id(0) == 0)
        def _():
            gf[...] = jnp.zeros_like(gf)
            gbk[...] = jnp.zeros_like(gbk)

        m = _gla_masks()
        keys = ("qt", "kt", "kh", "e", "einv", "eout", "dec")
        pf = dict(zip(keys + ("dd",), scr[0:8]))
        pb = dict(zip(keys + ("dd",), scr[8:16]))
        for (q, k, lr, cols, rev, pr) in ((qf, kf, lf, slice(0, DK), False, pf), (qb, kb, lb, slice(DK, 2 * DK), True, pb)):
            p = _gla_tile_prep(q[...], k[...], _gates(lr, gc_ref, bs_ref, cols), m, rev, nc)
            for key in keys:
                pr[key][...] = p[key]

        def step(c, v, dor, st, g_ref, pr, dq, dk, dv, da, rev):
            rows = pl.ds(pl.multiple_of(c * CH, CH), CH)
            dvc, dqt, dkt, dkh, dd = _gla_chunk_bwd(pr["qt"][rows, :], pr["kt"][rows, :], pr["kh"][rows, :], v[rows, :],
                                                    dor[rows, :], st[c], pr["dec"][pl.ds(c, 1), :], g_ref, m, rev)
            dv[rows, :] = dvc
            dq[rows, :] = dqt
            dk[rows, :] = dkt
            da[rows, :] = dkh
            pr["dd"][pl.ds(c, 1), :] = dd

        def chunk(c, carry):
            step(nc - 1 - c, vf, dof, sfr, gf, pf, dqf, dkf, dvf, daf, False)
            step(c, vb, dob, sbr, gbk, pb, dqb, dkb, dvb, dab, True)
            return carry

        lax.fori_loop(0, nc, chunk, 0, unroll=2)

        def finish(pr, dq, dk, da, rev):
            dqt, dkt, dkh = dq[...], dk[...], da[...]
            kk = dkh * pr["kh"][...]
            dcum = dqt * pr["qt"][...] - dkt * pr["kt"][...] - kk
            dtot = pr["dd"][...] * pr["dec"][...]
            tri_t = m["tril"] if rev else m["triu"]
            parts = []
            for c in range(nc):
                rs = slice(c * CH, (c + 1) * CH)
                parts.append(_mm_tri(tri_t, dcum[rs]) + (jnp.sum(kk[rs], axis=0, keepdims=True) + dtot[c:c + 1]))
            da[...] = jnp.concatenate(parts, axis=0)
            dq[...] = dqt * pr["e"][...] * QSCALE
            dk[...] = dkt * pr["einv"][...] + dkh * pr["eout"][...]

        finish(pf, dqf, dkf, daf, False)
        finish(pb, dqb, dkb, dab, True)

    fwd_dir = lambda cb: (lambda i: (nb - 1 - i, cb))
    bwd_dir = lambda cb: (lambda i: (i, cb))

    def side(ix):
        return [pl.BlockSpec((tl, DK), ix(CB_Q)), pl.BlockSpec((tl, DK), ix(CB_K)), pl.BlockSpec((tl, DG), ix(CB_V)),
                pl.BlockSpec((tl, LRW), ix(CB_LR)), pl.BlockSpec((tl, DG), ix(0)),
                pl.BlockSpec((nc, DG, DK), lambda i: (ix(0)(i)[0], 0, 0))]

    def outs(ix):
        return [pl.BlockSpec((tl, DK), ix(0)), pl.BlockSpec((tl, DK), ix(0)), pl.BlockSpec((tl, DG), ix(0)),
                pl.BlockSpec((tl, DK), ix(0))]

    o_shape = [jax.ShapeDtypeStruct((L, DK), F32), jax.ShapeDtypeStruct((L, DK), F32),
               jax.ShapeDtypeStruct((L, DG), F32), jax.ShapeDtypeStruct((L, DK), F32)]
    return pl.pallas_call(
        body, name="gla_bwd", grid=(nb,),
        in_specs=side(fwd_dir) + side(bwd_dir) + [pl.BlockSpec((LRW, 2 * DK), lambda i: (0, 0)),
                                                  pl.BlockSpec((1, 2 * DK), lambda i: (0, 0))],
        out_specs=outs(fwd_dir) + outs(bwd_dir),
        out_shape=o_shape + o_shape,
        scratch_shapes=[pltpu.VMEM((DG, DK), F32), pltpu.VMEM((DG, DK), F32)]
        + ([pltpu.VMEM((tl, DK), F32)] * 6 + [pltpu.VMEM((nc, DK), F32)] * 2) * 2,
        compiler_params=_cp("arbitrary"),
    )(P, P, P, P, do, sf, P, P, P, P, do, sb, gcat, gbias)


def mix_bwd2(dgb, dcc, dgo, gl, P, conv_a, gcat, gbias, w_in, x, dres, g1, tl):
    L = P.shape[0]
    nt = L // tl

    def body(dgbr, dccr, dccp, dccn, dgor, dqf, dkf, dvf, daf, dqb, dkb, dvb, dab, gc, gv, lr, ca, gcr, bsr, wi,
             xr, drr, g1r, dP, dx, dg1, dgcat, dbias):
        i = pl.program_id(0)

        @pl.when(i == 0)
        def _():
            dg1[...] = jnp.zeros_like(dg1)
            dgcat[...] = jnp.zeros_like(dgcat)
            dbias[...] = jnp.zeros_like(dbias)

        p, n = _halo_rows(dccp, dccn, i, nt - 1)
        dc = _conv3_t(dccr[...], p, n, ca)
        pre = _mm(lr[...], gcr[...]) + bsr[...]
        da = jnp.concatenate([daf[...], dab[...]], axis=1)
        dpre = da * GATE_NORM * (1.0 - _sigmoid(pre))
        dpre16 = dpre.astype(BF16)
        dP[:, 0:DC] = dgbr[...].astype(BF16)
        dP[:, DC:2 * DC] = (dc * gv[...]).astype(BF16)
        dP[:, 2 * DC:3 * DC] = (dc * gc[...]).astype(BF16)
        dP[:, 1536:1792] = (dqf[...] + dqb[...]).astype(BF16)
        dP[:, 1792:2048] = (dkf[...] + dkb[...]).astype(BF16)
        dP[:, 2048:2560] = (dvf[...] + dvb[...]).astype(BF16)
        dP[:, 2560:3072] = dgor[...].astype(BF16)
        dP[:, 3072:3200] = _mm_nt(dpre16, gcr[...]).astype(BF16)
        dgcat[...] += _mm_tn(lr[...], dpre16)
        dbias[0:1, :] += jnp.sum(dpre, axis=0, keepdims=True)
        dh, dg = _rms_bwd(_mm_nt(dP[...], wi[...]), xr[...], g1r[...])
        dx[...] = drr[...] + dh
        dg1[...] += dg

    row = lambda n: pl.BlockSpec((tl, n), lambda i: (i, 0))
    t = lambda w, cb: pl.BlockSpec((tl, w), lambda i: (i, cb))
    full = lambda a: pl.BlockSpec(a.shape, lambda i: (0, 0))
    return pl.pallas_call(
        body, name="mix_bwd2", grid=(nt,),
        in_specs=[row(DC), row(DC), pl.BlockSpec((8, DC), _prev_blk(tl, 0)), pl.BlockSpec((8, DC), _next_blk(tl, L, 0)),
                  row(DG), row(DK), row(DK), row(DG), row(DK), row(DK), row(DK), row(DG), row(DK),
                  t(DC, CB_GC), t(DC, CB_GV), t(LRW, CB_LR), full(conv_a), full(gcat), full(gbias), full(w_in),
                  row(D), row(D), full(g1)],
        out_specs=[row(DINP), row(D), pl.BlockSpec((1, D), lambda i: (0, 0)), pl.BlockSpec((LRW, 2 * DK), lambda i: (0, 0)),
                   pl.BlockSpec((8, 2 * DK), lambda i: (0, 0))],
        out_shape=[jax.ShapeDtypeStruct((L, DINP), BF16), jax.ShapeDtypeStruct((L, D), F32),
                   jax.ShapeDtypeStruct((1, D), F32), jax.ShapeDtypeStruct((LRW, 2 * DK), F32),
                   jax.ShapeDtypeStruct((8, 2 * DK), F32)],
        compiler_params=_cp("arbitrary"),
    )(dgb, dcc, dcc, dcc, dgo, *gl, P, P, P, conv_a, gcat, gbias, w_in, x, dres, g1)


def _row_tile(rows, cols):
    if rows * cols * 4 <= 2 * 1024 * 1024:
        return rows
    best = 8
    for t in range(8, rows, 8):
        if rows % t == 0 and t * cols * 4 <= 2 * 1024 * 1024:
            best = t
    return best


def adamw(w, g, m, v, name):
    shape = w.shape
    cols = shape[-1]
    w2, g2, m2, v2 = (a.reshape(-1, cols) for a in (w, g, m, v))
    rows = w2.shape[0]
    tr = _row_tile(rows, cols)

    def body(wr, gr, mr, vr, dl, nm, nv):
        gg = gr[...]
        mm = B1 * mr[...] + (1.0 - B1) * gg
        vv = B2 * vr[...] + (1.0 - B2) * (gg * gg)
        m_hat = mm / (1.0 - B1 ** STEP)
        v_hat = vv / (1.0 - B2 ** STEP)
        dl[...] = -LR * (m_hat / (jnp.sqrt(v_hat) + AEPS) + WD * wr[...])
        nm[...] = mm
        nv[...] = vv

    blk = pl.BlockSpec((tr, cols), lambda i: (i, 0))
    o = jax.ShapeDtypeStruct((rows, cols), F32)
    d, nm, nv = pl.pallas_call(
        body, name=name, grid=(rows // tr,), in_specs=[blk] * 4, out_specs=[blk] * 3, out_shape=[o, o, o],
        compiler_params=_cp("parallel"),
    )(w2, g2, m2, v2)
    return d.reshape(shape), nm.reshape(shape), nv.reshape(shape)


def _place():
    return lax.axis_index("x"), lax.axis_index("y"), lax.axis_index("c")


def allgather8(v, name):
    mp, n = v.shape

    def body(x_ref, out_ref, send_sems, recv_sems, local_sem):
        x, y, c = _place()
        me, sibling = (x, y, c), (x, y, 1 - c)
        chips = [(1 - x, y), (x, 1 - y), (1 - x, 1 - y)]

        def rows(px, py, pc):
            return out_ref.at[pl.ds((4 * px + 2 * py + pc) * mp, mp), :]

        def copy(k, block, to, src=None):
            return pltpu.make_async_remote_copy(
                src_ref=rows(*block) if src is None else src, dst_ref=rows(*block),
                send_sem=send_sems.at[k], recv_sem=recv_sems.at[k], device_id=to, device_id_type=MESH)

        mine = pltpu.make_async_copy(x_ref, rows(*me), local_sem)
        mine.start()
        first = [copy(0, me, sibling, src=x_ref)]
        first += [copy(1 + j, me, (*chip, c), src=x_ref) for j, chip in enumerate(chips)]
        for cp in first:
            cp.start()
        passed = [copy(4 + j, (*chip, c), sibling) for j, chip in enumerate(chips)]
        for j, chip in enumerate(chips):
            copy(1 + j, (*chip, c), me).wait_recv()
            passed[j].start()
        copy(0, sibling, me).wait_recv()
        for j, chip in enumerate(chips):
            copy(4 + j, (*chip, 1 - c), me).wait_recv()
        for cp in first + passed:
            cp.wait_send()
        mine.wait()

    return pl.pallas_call(
        body, name=name, out_shape=jax.ShapeDtypeStruct((8 * mp, n), v.dtype),
        in_specs=[pl.BlockSpec(memory_space=pltpu.VMEM)], out_specs=pl.BlockSpec(memory_space=pltpu.VMEM),
        scratch_shapes=[pltpu.SemaphoreType.DMA((7,)), pltpu.SemaphoreType.DMA((7,)), pltpu.SemaphoreType.DMA],
        compiler_params=pltpu.CompilerParams(vmem_limit_bytes=VMEM_LIMIT),
    )(v)


def sum8(v, mp):
    def body(x_ref, o_ref):
        acc = x_ref[0:mp, :]
        for d in range(1, 8):
            acc = acc + x_ref[d * mp:(d + 1) * mp, :]
        o_ref[...] = acc

    return pl.pallas_call(body, name="sum8", out_shape=jax.ShapeDtypeStruct((mp, v.shape[1]), F32),
                          compiler_params=pltpu.CompilerParams(vmem_limit_bytes=VMEM_LIMIT))(v)


_ANY = pl.BlockSpec(memory_space=pl.ANY)


def _row_half(ref, lead, h):
    hr = ref.shape[-2] // 2
    return ref.at[(*lead, pl.ds(h * hr, hr), slice(None))]


def allgather_weights(slots):
    n = len(slots)

    def body(*refs):
        s_refs, o_refs, (send_sems, recv_sems) = refs[:n], refs[n:2 * n], refs[2 * n:]
        x, y, c = _place()
        me = 2 * x + y
        sibling = (x, y, 1 - c)
        chips = [(1 - x, y), (x, 1 - y), (1 - x, 1 - y)]

        def half(ref, slot, h):
            return _row_half(ref, (slot, slice(None)), h)

        def copy(k, src, dst, to):
            return pltpu.make_async_remote_copy(src_ref=src, dst_ref=dst, send_sem=send_sems.at[k],
                                                recv_sem=recv_sems.at[k], device_id=to, device_id_type=MESH)

        first = [copy(6 * a + k, half(s_refs[a], me, c), half(o_refs[a], me, c), (px, py, c))
                 for k, (px, py) in enumerate(chips) for a in range(n)]
        for cp in first:
            cp.start()
        passed = []
        for k, (px, py) in enumerate(chips):
            for a in range(n):
                got = half(o_refs[a], 2 * px + py, c)
                copy(6 * a + k, half(s_refs[a], me, c), got, (px, py, c)).wait_recv()
                cp = copy(6 * a + 3 + k, got, got, sibling)
                cp.start()
                passed.append(cp)
        for k, (px, py) in enumerate(chips):
            for a in range(n):
                got = half(o_refs[a], 2 * px + py, 1 - c)
                copy(6 * a + 3 + k, got, got, sibling).wait_recv()
        for cp in first + passed:
            cp.wait_send()

    return pl.pallas_call(
        body, name="allgather_weights", out_shape=[jax.ShapeDtypeStruct(s.shape, s.dtype) for s in slots],
        in_specs=[_ANY] * n, out_specs=[_ANY] * n, input_output_aliases={a: a for a in range(n)},
        scratch_shapes=[pltpu.SemaphoreType.DMA((6 * n,)), pltpu.SemaphoreType.DMA((6 * n,))],
    )(*slots)


_HBM = pl.BlockSpec(memory_space=pltpu.HBM)
_SEM = pl.BlockSpec(memory_space=pltpu.SEMAPHORE)
_EFFECT = pltpu.SideEffectType.DATAFLOW_SIDE_EFFECTING


def gather_start(slots, name):
    n = len(slots)

    def body(*refs):
        s_refs, send_sems, recv_sems, token = refs[:n], refs[n], refs[n + 1], refs[-1]
        x, y, c = _place()
        me = 2 * x + y
        for k, (px, py) in enumerate([(1 - x, y), (x, 1 - y), (1 - x, 1 - y)]):
            for a in range(n):
                pltpu.make_async_remote_copy(
                    src_ref=s_refs[a].at[me], dst_ref=s_refs[a].at[me], send_sem=send_sems.at[3 * a + k],
                    recv_sem=recv_sems.at[3 * a + k], device_id=(px, py, c), device_id_type=MESH).start()
        token[...] = jnp.zeros_like(token)

    out = pl.pallas_call(
        body, name=name,
        out_shape=(pltpu.SemaphoreType.DMA((3 * n,)), pltpu.SemaphoreType.DMA((3 * n,)),
                   *[pltpu.HBM(s.shape, s.dtype) for s in slots], jax.ShapeDtypeStruct((8, 128), F32)),
        in_specs=[_HBM] * n, out_specs=(_SEM, _SEM, *[_HBM] * n, pl.BlockSpec(memory_space=pltpu.VMEM)),
        input_output_aliases={a: 2 + a for a in range(n)},
        compiler_params=pltpu.CompilerParams(has_side_effects=_EFFECT),
    )(*[pltpu.with_memory_space_constraint(s, pltpu.HBM) for s in slots])
    return out[0], out[1], list(out[2:2 + n]), out[-1]


def gather_wait(send_sems, recv_sems, slots, after, name):
    n = len(slots)

    def body(*refs):
        s_refs, ssem, rsem = refs[:n], refs[n], refs[n + 1]
        x, y, c = _place()
        me = 2 * x + y
        for k, (px, py) in enumerate([(1 - x, y), (x, 1 - y), (1 - x, 1 - y)]):
            for a in range(n):
                cp = pltpu.make_async_remote_copy(
                    src_ref=s_refs[a].at[me], dst_ref=s_refs[a].at[2 * px + py], send_sem=ssem.at[3 * a + k],
                    recv_sem=rsem.at[3 * a + k], device_id=(px, py, c), device_id_type=MESH)
                cp.wait_send()
                cp.wait_recv()

    return pl.pallas_call(
        body, name=name, out_shape=[pltpu.HBM(s.shape, s.dtype) for s in slots],
        in_specs=[_HBM] * n + [_SEM, _SEM, _ANY], out_specs=[_HBM] * n,
        input_output_aliases={a: a for a in range(n)},
        compiler_params=pltpu.CompilerParams(has_side_effects=_EFFECT),
    )(*slots, send_sems, recv_sems, after)


def rs_sibling_halves(gs):
    n = len(gs)

    def body(*refs):
        g_refs, r_refs, (send_sems, recv_sems) = refs[:n], refs[n:2 * n], refs[2 * n:]
        x, y, c = _place()
        cps = [pltpu.make_async_remote_copy(
            src_ref=_row_half(g_refs[a], (slice(None), slice(None)), 1 - c), dst_ref=r_refs[a],
            send_sem=send_sems.at[a], recv_sem=recv_sems.at[a], device_id=(x, y, 1 - c), device_id_type=MESH)
            for a in range(n)]
        for cp in cps:
            cp.start()
        for cp in cps:
            cp.wait()

    return pl.pallas_call(
        body, name="rs_sibling_halves",
        out_shape=[jax.ShapeDtypeStruct((*g.shape[:2], g.shape[2] // 2, g.shape[3]), F32) for g in gs],
        in_specs=[_ANY] * n, out_specs=[_ANY] * n,
        scratch_shapes=[pltpu.SemaphoreType.DMA((n,)), pltpu.SemaphoreType.DMA((n,))],
    )(*gs)


def rs_chipsum16(g, recv1, cidx, name):
    nl, hr, cols = recv1.shape[1:]

    def body(c_ref, g_ref, r_ref, o_ref):
        o_ref[...] = (g_ref[...] + r_ref[...]).astype(BF16)

    blk = (1, 1, hr, cols)
    return pl.pallas_call(
        body, name=name, out_shape=jax.ShapeDtypeStruct(recv1.shape, BF16),
        grid_spec=pltpu.PrefetchScalarGridSpec(
            num_scalar_prefetch=1, grid=(4, nl),
            in_specs=[pl.BlockSpec(blk, lambda j, l, c: (j, l, c[0], 0)), pl.BlockSpec(blk, lambda j, l, c: (j, l, 0, 0))],
            out_specs=pl.BlockSpec(blk, lambda j, l, c: (j, l, 0, 0))),
        compiler_params=_cp("parallel", "parallel"),
    )(cidx, g, recv1)


def rs_exchange_chips(cs):
    n = len(cs)

    def body(*refs):
        s_refs, r_refs, (send_sems, recv_sems) = refs[:n], refs[n:2 * n], refs[2 * n:]
        x, y, c = _place()
        chips = [(1 - x, y), (x, 1 - y), (1 - x, 1 - y)]
        cps = [pltpu.make_async_remote_copy(
            src_ref=s_refs[a].at[2 * px + py], dst_ref=r_refs[a].at[k], send_sem=send_sems.at[3 * a + k],
            recv_sem=recv_sems.at[3 * a + k], device_id=(px, py, c), device_id_type=MESH)
            for k, (px, py) in enumerate(chips) for a in range(n)]
        for cp in cps:
            cp.start()
        for cp in cps:
            cp.wait()

    return pl.pallas_call(
        body, name="rs_exchange_chips", out_shape=[jax.ShapeDtypeStruct((3, *s.shape[1:]), BF16) for s in cs],
        in_specs=[_ANY] * n, out_specs=[_ANY] * n,
        scratch_shapes=[pltpu.SemaphoreType.DMA((3 * n,)), pltpu.SemaphoreType.DMA((3 * n,))],
    )(*cs)


def exchange_start(cs, name):
    n = len(cs)
    lands = [lax.empty((3, *c.shape[1:]), BF16) for c in cs]

    def body(*refs):
        s_refs, l_refs, send_sems, recv_sems, token = refs[:n], refs[n:2 * n], refs[2 * n], refs[2 * n + 1], refs[-1]
        x, y, c = _place()
        for k, (px, py) in enumerate([(1 - x, y), (x, 1 - y), (1 - x, 1 - y)]):
            for a in range(n):
                pltpu.make_async_remote_copy(
                    src_ref=s_refs[a].at[2 * px + py], dst_ref=l_refs[a].at[k], send_sem=send_sems.at[3 * a + k],
                    recv_sem=recv_sems.at[3 * a + k], device_id=(px, py, c), device_id_type=MESH).start()
        token[...] = jnp.zeros_like(token)

    bufs = list(cs) + lands
    out = pl.pallas_call(
        body, name=name,
        out_shape=(pltpu.SemaphoreType.DMA((3 * n,)), pltpu.SemaphoreType.DMA((3 * n,)),
                   *[pltpu.HBM(b.shape, b.dtype) for b in bufs], jax.ShapeDtypeStruct((8, 128), F32)),
        in_specs=[_HBM] * (2 * n), out_specs=(_SEM, _SEM, *[_HBM] * (2 * n), pl.BlockSpec(memory_space=pltpu.VMEM)),
        input_output_aliases={i: 2 + i for i in range(2 * n)},
        compiler_params=pltpu.CompilerParams(has_side_effects=_EFFECT),
    )(*[pltpu.with_memory_space_constraint(b, pltpu.HBM) for b in bufs])
    return out[0], out[1], list(out[2:2 + n]), list(out[2 + n:2 + 2 * n]), out[-1]


def exchange_wait(send_sems, recv_sems, cs, lands, after, name):
    n = len(cs)

    def body(*refs):
        s_refs, l_refs, ssem, rsem = refs[:n], refs[n:2 * n], refs[2 * n], refs[2 * n + 1]
        x, y, c = _place()
        for k, (px, py) in enumerate([(1 - x, y), (x, 1 - y), (1 - x, 1 - y)]):
            for a in range(n):
                cp = pltpu.make_async_remote_copy(
                    src_ref=s_refs[a].at[2 * px + py], dst_ref=l_refs[a].at[k], send_sem=ssem.at[3 * a + k],
                    recv_sem=rsem.at[3 * a + k], device_id=(px, py, c), device_id_type=MESH)
                cp.wait_send()
                cp.wait_recv()

    bufs = list(cs) + list(lands)
    out = pl.pallas_call(
        body, name=name, out_shape=[pltpu.HBM(b.shape, b.dtype) for b in bufs],
        in_specs=[_HBM] * (2 * n) + [_SEM, _SEM, _ANY], out_specs=[_HBM] * (2 * n),
        input_output_aliases={i: i for i in range(2 * n)},
        compiler_params=pltpu.CompilerParams(has_side_effects=_EFFECT),
    )(*bufs, send_sems, recv_sems, after)
    return list(out[n:])


def rs_final_sum(g, recv1, recv2, idx, name):
    nl, hr, cols = recv1.shape[1:]

    def body(i_ref, g_ref, r1_ref, r2_ref, o_ref):
        acc = g_ref[0, 0] + r1_ref[0, 0]
        for k in range(3):
            acc = acc + r2_ref[k, 0].astype(F32)
        o_ref[0] = acc

    blk = (1, 1, hr, cols)
    return pl.pallas_call(
        body, name=name, out_shape=jax.ShapeDtypeStruct((nl, 2 * hr, cols), F32),
        grid_spec=pltpu.PrefetchScalarGridSpec(
            num_scalar_prefetch=1, grid=(nl,),
            in_specs=[pl.BlockSpec(blk, lambda l, ix: (ix[0], l, ix[1], 0)), pl.BlockSpec(blk, lambda l, ix: (ix[0], l, 0, 0)),
                      pl.BlockSpec((3, 1, hr, cols), lambda l, ix: (0, l, 0, 0))],
            out_specs=pl.BlockSpec((1, hr, cols), lambda l, ix: (l, ix[1], 0))),
        compiler_params=_cp("parallel"),
    )(idx, g, recv1, recv2)


def rs_share_halves(fulls):
    n = len(fulls)

    def body(*refs):
        h_refs, o_refs, (send_sems, recv_sems) = refs[:n], refs[n:2 * n], refs[2 * n:]
        x, y, c = _place()
        sibling = (x, y, 1 - c)

        def copy(a, h):
            return pltpu.make_async_remote_copy(
                src_ref=_row_half(h_refs[a], (slice(None),), h), dst_ref=_row_half(o_refs[a], (slice(None),), h),
                send_sem=send_sems.at[a], recv_sem=recv_sems.at[a], device_id=sibling, device_id_type=MESH)

        for a in range(n):
            copy(a, c).start()
        for a in range(n):
            copy(a, c).wait_send()
            copy(a, 1 - c).wait_recv()

    return pl.pallas_call(
        body, name="rs_share_halves", out_shape=[jax.ShapeDtypeStruct(f.shape, F32) for f in fulls],
        in_specs=[_ANY] * n, out_specs=[_ANY] * n, input_output_aliases={a: a for a in range(n)},
        scratch_shapes=[pltpu.SemaphoreType.DMA((n,)), pltpu.SemaphoreType.DMA((n,))],
    )(*fulls)


def _own_slot(shard, chip, dtype):
    return lax.dynamic_update_slice(lax.empty((4, *shard.shape), dtype), shard.astype(dtype)[None],
                                    (chip,) + (0,) * shard.ndim)


def kernel(x, norm_mix_pre, norm_mix_post, norm_ffn_pre, norm_ffn_post, w_in, conv_a, gate_up_fwd, gate_bias_fwd, gate_up_bwd, gate_bias_bwd, gla_head_norm, w_out, w_up, conv_ffn, w_down, loss_target, m_norm_mix_pre, m_norm_mix_post, m_norm_ffn_pre, m_norm_ffn_post, m_w_in, m_conv_a, m_gate_up_fwd, m_gate_bias_fwd, m_gate_up_bwd, m_gate_bias_bwd, m_gla_head_norm, m_w_out, m_w_up, m_conv_ffn, m_w_down, v_norm_mix_pre, v_norm_mix_post, v_norm_ffn_pre, v_norm_ffn_post, v_w_in, v_conv_a, v_gate_up_fwd, v_gate_bias_fwd, v_gate_up_bwd, v_gate_bias_bwd, v_gla_head_norm, v_w_out, v_w_up, v_conv_ffn, v_w_down):
    L = x.shape[1]
    xi, yi, ci = _place()
    chip = 2 * xi + yi
    tl_gla, tl_mix, tl_ffn = min(L, TL_GLA), min(L, TL_MIX), min(L, TL_FFN)

    big_w = (w_in, w_out, w_up, w_down)
    started = [gather_start([_own_slot(w[l], chip, BF16) for w in big_w], f"gather_start_{l}") for l in range(1, DEPTH)]
    tokens = [s[3] for s in started]
    gathered = [[a[:, 0] for a in allgather_weights([_own_slot(w[0:1], chip, BF16) for w in big_w])]]

    def layer_weights(bufs):
        a_in, a_out, a_up, a_down = bufs
        w_in_l = jnp.pad(jnp.concatenate([a_in[j] for j in range(4)], axis=1), ((0, 0), (0, DINP - DIN)))
        return w_in_l, a_out.reshape(D, D), a_up, a_down.reshape(DFF, D)

    small = jnp.concatenate([conv_a.reshape(-1), gate_up_fwd.reshape(-1), gate_up_bwd.reshape(-1), conv_ffn.reshape(-1)])
    ms = small.shape[0] // 128
    sg = allgather8(small.reshape(ms, 128), "allgather_small_weights").reshape(4, 2, ms * 128)[:, 0]

    def small_full(off, shape):
        n = shape[0] * shape[1] * shape[2]
        return jnp.concatenate([sg[j, off:off + n].reshape(shape) for j in range(4)], axis=2)

    o1 = DEPTH * 3 * 128
    o2 = o1 + DEPTH * RK * 64
    o3 = o2 + DEPTH * RK * 64
    conv_a_f = small_full(0, (DEPTH, 3, 128))
    gup_f = small_full(o1, (DEPTH, RK, 64))
    gup_b = small_full(o2, (DEPTH, RK, 64))
    conv_ffn_f = small_full(o3, (DEPTH, 3, 1408))

    def gcat_of(l):
        g = jnp.zeros((LRW, 2 * DK), F32)
        g = g.at[0:RK, 0:DK].set(gup_f[l]).at[RK:2 * RK, DK:2 * DK].set(gup_b[l])
        return g.astype(BF16)

    gcats = [gcat_of(l) for l in range(DEPTH)]
    gbiases = [jnp.concatenate([gate_bias_fwd[l], gate_bias_bwd[l]])[None, :] for l in range(DEPTH)]
    ghn4s = [jnp.tile(gla_head_norm[l], NH)[None, :] for l in range(DEPTH)]

    xc = x.reshape(L, D)
    saved = []
    W_in, W_out, W_up, W_down = [], [], [], []
    for l in range(DEPTH):
        if l > 0:
            ssem, rsem, bufs, _ = started[l - 1]
            gathered.append(gather_wait(ssem, rsem, bufs, xc, f"gather_wait_{l}"))
        for lst, w in zip((W_in, W_out, W_up, W_down), layer_weights(gathered[l])):
            lst.append(w)
        P, h1 = rms_matmul(xc, norm_mix_pre[l][None, :], W_in[l], 640, "proj_in", after=tokens if l == 0 else ())
        o_f, o_b, sf, sb = gla_fwd(P, gcats[l], gbiases[l], tl_gla)
        ycat, y, x1 = mix_out(P, o_f, o_b, conv_a_f[l], ghn4s[l], W_out[l], norm_mix_post[l][None, :], xc, tl_mix)
        U, h2 = rms_matmul(x1, norm_ffn_pre[l][None, :], W_up[l], WFF, "proj_up", n_out=2 * DFF, out_dtype=BF16,
                           w_spec=pl.BlockSpec((None, D, WFF), lambda i, j: (j, 0, 0)))
        y2, x2, ug, uv, z = ffn_down(U, conv_ffn_f[l], W_down[l], norm_ffn_post[l][None, :], x1, tl_ffn)
        saved.append(dict(x=xc, h1=h1, P=P, o_f=o_f, o_b=o_b, sf=sf, sb=sb, ycat=ycat, y=y, x1=x1, h2=h2, U=U, y2=y2,
                          ug=ug, uv=uv, z=z))
        xc = x2

    dx, loss_blk = loss_head(xc, loss_target.reshape(L, D), tl_mix)

    big = ("w_in", "w_out", "w_up", "w_down")
    cidx = jnp.reshape(ci, (1,)).astype(jnp.int32)
    idx = jnp.stack([chip, ci]).astype(jnp.int32)
    grads = [None] * DEPTH
    reduced = [None] * DEPTH
    tl_dw = min(L, 1024)
    pending = None
    token = ()

    def finish(pend, after):
        lp, gs_p, recv1_p, (ssem, rsem, cs_thru, lands, _) = pend
        recv2 = exchange_wait(ssem, rsem, cs_thru, lands, after, f"exchange_wait_{lp}")
        halves = [rs_final_sum(g, r1, r2, idx, "rs_final_sum_" + k) for g, r1, r2, k in zip(gs_p, recv1_p, recv2, big)]
        reduced[lp] = rs_share_halves(halves)

    for l in reversed(range(DEPTH)):
        s = saved[l]
        dy2, dg4 = rms_bwd_pre(dx, s["y2"], norm_ffn_post[l][None, :], tl_mix, after=token)
        du_g, du_v = ffn_bwd1(dy2, s["ug"], s["uv"], W_down[l], min(L, TL_FFN2))
        g_down = matmul_tn(s["z"], dy2, DFF // 2, D, tl_dw, "dw_down").reshape(4, 1, DFF // 4, D)
        dU_g, dU_v, dx1, dg3, dcf_g, dcf_v = ffn_bwd2(du_g, du_v, s["U"], conv_ffn_f[l], W_up[l], s["x1"], dx,
                                                      norm_ffn_pre[l][None, :],
                                        min(L, TL_FFN2))
        g_up = matmul_tn(s["h2"], dU_g, D, WFF, tl_dw, "dw_up_gate",
                         into=(lax.empty((4, 1, D, WFF), F32), (None, None, D, WFF), lambda p, q: (q, 0, 0, 0)))
        g_up = matmul_tn(s["h2"], dU_v, D, WFF, tl_dw, "dw_up_val",
                         into=(g_up, (None, None, D, WFF), lambda p, q: (NFF + q, 0, 0, 0)))
        dy, dg2 = rms_bwd_pre(dx1, s["y"], norm_mix_post[l][None, :], tl_mix)
        dgb, dcc, dgo, do, dca, dghn = mix_bwd1(dy, W_out[l], s["P"], s["o_f"], s["o_b"], conv_a_f[l], ghn4s[l], tl_mix)
        g_out = matmul_tn(s["ycat"], dy, D, D, tl_dw, "dw_out").reshape(4, 1, D // 4, D)
        gl = gla_bwd(s["P"], do, s["sf"], s["sb"], gcats[l], gbiases[l], tl_gla)
        dP, dx, dg1, dgcat, dbias = mix_bwd2(dgb, dcc, dgo, gl, s["P"], conv_a_f[l], gcats[l], gbiases[l], W_in[l],
                                             s["x"], dx1, norm_mix_pre[l][None, :], tl_mix)
        dW_in = matmul_tn(s["h1"], dP, D, 640, tl_dw, "dw_in")
        g_in = jnp.stack([dW_in[:, (DIN // 4) * j:(DIN // 4) * (j + 1)] for j in range(4)])[:, None]
        grads[l] = dict(
            norm_mix_pre=dg1[0], norm_mix_post=dg2[0], norm_ffn_pre=dg3[0], norm_ffn_post=dg4[0],
            conv_a=dca[0:3], gate_up_fwd=dgcat[0:RK, 0:DK], gate_bias_fwd=dbias[0, 0:DK],
            gate_up_bwd=dgcat[RK:2 * RK, DK:2 * DK], gate_bias_bwd=dbias[0, DK:2 * DK], gla_head_norm=dghn[0],
            conv_ffn=jnp.concatenate([dcf_g[j, 0:3] for j in range(NFF)] + [dcf_v[j, 0:3] for j in range(NFF)], axis=1))
        gs = [g_in, g_out, g_up, g_down]
        recv1 = rs_sibling_halves(gs)
        cs16 = [rs_chipsum16(g, r, cidx, "rs_chipsum16_" + k) for g, r, k in zip(gs, recv1, big)]
        flight = exchange_start(cs16, f"exchange_start_{l}")
        token = (flight[4],)
        if pending is not None:
            finish(pending, flight[4])
        pending = (l, gs, recv1, flight)
    finish(pending, pending[3][4])

    G = {k: jnp.stack([grads[l][k] for l in range(DEPTH)]) for k in grads[0]}

    small_names = ["norm_mix_pre", "norm_mix_post", "norm_ffn_pre", "norm_ffn_post", "conv_a", "gate_up_fwd",
                   "gate_bias_fwd", "gate_up_bwd", "gate_bias_bwd", "gla_head_norm", "conv_ffn"]
    flat = jnp.concatenate([G[k].reshape(-1) for k in small_names] + [loss_blk[0, 0:1]])
    n_small = flat.shape[0]
    mp = -(-n_small // 1024) * 8
    flat = jnp.pad(flat, (0, mp * 128 - n_small)).reshape(mp, 128)
    tot = sum8(allgather8(flat, "allgather_small_grads"), mp).reshape(-1)
    gsm = {}
    o = 0
    for k in small_names:
        n = G[k].size
        gsm[k] = tot[o:o + n].reshape(G[k].shape)
        o += n
    loss = tot[o]

    def my_cols(a, width):
        return lax.dynamic_slice_in_dim(a, chip * width, width, axis=2)

    gsm["conv_a"] = my_cols(gsm["conv_a"], 128)
    gsm["gate_up_fwd"] = my_cols(gsm["gate_up_fwd"], 64)
    gsm["gate_up_bwd"] = my_cols(gsm["gate_up_bwd"], 64)
    gsm["conv_ffn"] = my_cols(gsm["conv_ffn"], 1408)

    for a, k in enumerate(big):
        gsm[k] = jnp.concatenate([reduced[l][a] for l in range(DEPTH)], axis=0)

    names = ["norm_mix_pre", "norm_mix_post", "norm_ffn_pre", "norm_ffn_post", "w_in", "conv_a", "gate_up_fwd",
             "gate_bias_fwd", "gate_up_bwd", "gate_bias_bwd", "gla_head_norm", "w_out", "w_up", "conv_ffn", "w_down"]
    w = dict(norm_mix_pre=norm_mix_pre, norm_mix_post=norm_mix_post, norm_ffn_pre=norm_ffn_pre, norm_ffn_post=norm_ffn_post,
             w_in=w_in, conv_a=conv_a, gate_up_fwd=gate_up_fwd, gate_bias_fwd=gate_bias_fwd, gate_up_bwd=gate_up_bwd,
             gate_bias_bwd=gate_bias_bwd, gla_head_norm=gla_head_norm, w_out=w_out, w_up=w_up, conv_ffn=conv_ffn, w_down=w_down)
    m = dict(norm_mix_pre=m_norm_mix_pre, norm_mix_post=m_norm_mix_post, norm_ffn_pre=m_norm_ffn_pre, norm_ffn_post=m_norm_ffn_post,
             w_in=m_w_in, conv_a=m_conv_a, gate_up_fwd=m_gate_up_fwd, gate_bias_fwd=m_gate_bias_fwd, gate_up_bwd=m_gate_up_bwd,
             gate_bias_bwd=m_gate_bias_bwd, gla_head_norm=m_gla_head_norm, w_out=m_w_out, w_up=m_w_up, conv_ffn=m_conv_ffn, w_down=m_w_down)
    v = dict(norm_mix_pre=v_norm_mix_pre, norm_mix_post=v_norm_mix_post, norm_ffn_pre=v_norm_ffn_pre, norm_ffn_post=v_norm_ffn_post,
             w_in=v_w_in, conv_a=v_conv_a, gate_up_fwd=v_gate_up_fwd, gate_bias_fwd=v_gate_bias_fwd, gate_up_bwd=v_gate_up_bwd,
             gate_bias_bwd=v_gate_bias_bwd, gla_head_norm=v_gla_head_norm, w_out=v_w_out, w_up=v_w_up, conv_ffn=v_conv_ffn, w_down=v_w_down)
    upd = {k: adamw(w[k], gsm[k], m[k], v[k], "adamw_" + k) for k in names}
    return (loss, dx.reshape(1, L, D), *[gsm[k] for k in names], *[upd[k][0] for k in names],
            *[upd[k][1] for k in names], *[upd[k][2] for k in names])
```

```python
import functools

import jax
import jax.numpy as jnp
from jax import lax
from jax.experimental import pallas as pl
from jax.experimental.pallas import tpu as pltpu

F32 = jnp.float32
BF16 = jnp.bfloat16
MXU_DTYPE = jnp.bfloat16
MESH = pl.DeviceIdType.MESH

D = 1024
DC = 512
DG = 512
NH = 4
HV = 128
HK = 64
DK = 256
RK = 16
CH = 64
DFF = 2816
DIN = 3104
DINP = 3200
LRW = 128
DEPTH = 4
EPS = 1e-6
QSCALE = HK ** -0.5
GATE_NORM = 1.0 / 16.0
CB_GB, CB_GC, CB_GV, CB_GO = 0, 1, 2, 5
CB_Q, CB_K = 6, 7
CB_V = 4
CB_LR = 24
LR = 0.001
B1 = 0.9
B2 = 0.999
AEPS = 1e-08
WD = 0.01
STEP = 10
TM_PROJ = 1024
TL_GLA = 512
TL_MIX = 256
TL_FFN = 256
TL_FFN2 = 512
VMEM_LIMIT = 56 * 1024 * 1024


def _cp(*sem):
    return pltpu.CompilerParams(dimension_semantics=sem if sem else None, vmem_limit_bytes=VMEM_LIMIT)


def _mm(a, b):
    return jnp.dot(a.astype(MXU_DTYPE), b.astype(MXU_DTYPE), preferred_element_type=F32)


def _mm_nt(a, b):
    return lax.dot_general(a.astype(MXU_DTYPE), b.astype(MXU_DTYPE), (((1,), (1,)), ((), ())),
                           preferred_element_type=F32)


def _mm_tn(a, b):
    return lax.dot_general(a.astype(MXU_DTYPE), b.astype(MXU_DTYPE), (((0,), (0,)), ((), ())),
                           preferred_element_type=F32)


def _mm_tri(tri, b):
    t = tri.astype(BF16)
    b1 = b.astype(BF16)
    r1 = b - b1.astype(F32)
    b2 = r1.astype(BF16)
    b3 = (r1 - b2.astype(F32)).astype(BF16)
    dot = lambda u: jnp.dot(t, u, preferred_element_type=F32)
    return dot(b1) + dot(b2) + dot(b3)


def _rms(x, g):
    r = lax.rsqrt(jnp.mean(x * x, axis=-1, keepdims=True) + EPS)
    return x * r * g


def _rms_bwd(dout, y, g):
    r = lax.rsqrt(jnp.mean(y * y, axis=-1, keepdims=True) + EPS)
    yh = y * r
    dyh = dout * g
    dy = r * (dyh - yh * jnp.mean(dyh * yh, axis=-1, keepdims=True))
    dg = jnp.sum(dout * yh, axis=0, keepdims=True)
    return dy, dg


def _sigmoid(x):
    return 0.5 * jnp.tanh(0.5 * x) + 0.5


def _logsig(x):
    return jnp.minimum(x, 0.0) - jnp.log1p(jnp.exp(-jnp.abs(x)))


def _shifts(x, p8, n8):
    n = x.shape[0]
    xe = jnp.concatenate([p8, x, n8], axis=0)
    return pltpu.roll(xe, 1, 0)[8:8 + n], pltpu.roll(xe, n + 15, 0)[8:8 + n]


def _halo_rows(prev_ref, next_ref, i, last):
    hr = prev_ref.shape[0]
    p = jnp.where(i == 0, 0.0, prev_ref[...].astype(F32)[hr - 8:hr, :])
    n = jnp.where(i == last, 0.0, next_ref[...].astype(F32)[0:8, :])
    return p, n


def _conv3(x, xp, xn, w_ref):
    xm1, xp1 = _shifts(x, xp, xn)
    return w_ref[0:1, :] * xm1 + w_ref[1:2, :] * x + w_ref[2:3, :] * xp1, xm1, xp1


def _conv3_t(d, dp, dn, w_ref):
    dm1, dp1 = _shifts(d, dp, dn)
    return w_ref[0:1, :] * dp1 + w_ref[1:2, :] * d + w_ref[2:3, :] * dm1


HALO32 = 8
HALO16 = 16


def _prev_row_blk(i, tl, hr):
    return jnp.maximum(i * (tl // hr) - 1, 0)


def _next_row_blk(i, tl, nrows, hr):
    return jnp.minimum((i + 1) * (tl // hr), nrows // hr - 1)


def _prev_blk(tl, cb, hr=HALO32):
    return lambda i: (_prev_row_blk(i, tl, hr), cb)


def _next_blk(tl, nrows, cb, hr=HALO32):
    return lambda i: (_next_row_blk(i, tl, nrows, hr), cb)


def rms_matmul(x, g, w, tn, name, w_spec=None, n_out=None, out_dtype=F32, after=()):
    L = x.shape[0]
    N = w.shape[1] if n_out is None else n_out
    tm = min(L, TM_PROJ)
    if w_spec is None:
        w_spec = pl.BlockSpec((D, tn), lambda i, j: (0, j))

    def body(x_ref, g_ref, w_ref, *rest):
        o_ref, h_ref = rest[-2:]

        @pl.when(pl.program_id(1) == 0)
        def _():
            h_ref[...] = _rms(x_ref[...], g_ref[...]).astype(BF16)

        o_ref[...] = _mm(h_ref[...], w_ref[...]).astype(out_dtype)

    return pl.pallas_call(
        body, name=name, grid=(L // tm, N // tn),
        in_specs=[pl.BlockSpec((tm, D), lambda i, j: (i, 0)), pl.BlockSpec((1, D), lambda i, j: (0, 0)), w_spec]
        + [_ANY] * len(after),
        out_specs=[pl.BlockSpec((tm, tn), lambda i, j: (i, j)), pl.BlockSpec((tm, D), lambda i, j: (i, 0))],
        out_shape=[jax.ShapeDtypeStruct((L, N), out_dtype), jax.ShapeDtypeStruct((L, D), BF16)],
        compiler_params=_cp("parallel", "arbitrary"),
    )(x, g, w, *after)


def _gla_masks():
    def blk(shape, rdiv, cdiv):
        r = lax.broadcasted_iota(jnp.int32, shape, 0) // rdiv
        c = lax.broadcasted_iota(jnp.int32, shape, 1) // cdiv
        return (r == c).astype(F32)

    r = lax.broadcasted_iota(jnp.int32, (CH, CH), 0)
    c = lax.broadcasted_iota(jnp.int32, (CH, CH), 1)
    r4 = lax.broadcasted_iota(jnp.int32, (NH * CH, CH), 0) % CH
    c4 = lax.broadcasted_iota(jnp.int32, (NH * CH, CH), 1)
    return dict(
        bdq=blk((NH * CH, DK), CH, HK),
        bdo=blk((NH * CH, DG), CH, HV),
        bds=blk((DG, DK), HV, HK),
        tril=(r >= c).astype(F32), triu=(r <= c).astype(F32),
        tril4=r4 >= c4, triu4=r4 <= c4,
    )


def _tile4(x):
    return jnp.concatenate([x, x, x, x], axis=0)


def _gla_tile_prep(q, k, a, m, rev, nc):
    tri = m["triu"] if rev else m["tril"]
    chunks = [a[c * CH:(c + 1) * CH] for c in range(nc)]
    cum = jnp.concatenate([_mm_tri(tri, ac) for ac in chunks], axis=0)
    tot = jnp.concatenate([jnp.sum(ac, axis=0, keepdims=True) for ac in chunks], axis=0)
    tot_rows = jnp.concatenate([jnp.broadcast_to(tot[c:c + 1], (CH, DK)) for c in range(nc)], axis=0)
    e = jnp.exp(cum)
    einv = jnp.exp(-cum)
    eout = jnp.exp(tot_rows - cum)
    q, k = q.astype(F32), k.astype(F32)
    return dict(e=e, einv=einv, eout=eout, dec=jnp.exp(tot), qt=q * QSCALE * e, kt=k * einv, kh=k * eout)


def _gla_scores(qt16, kt16, m, rev):
    qs = _tile4(qt16) * m["bdq"].astype(qt16.dtype)
    return qs, jnp.where(m["triu4"] if rev else m["tril4"], _mm_nt(qs, kt16), 0.0)


def _gla_chunk_fwd(qt16, kt16, kh16, v, dec, st_ref, m, rev):
    _, sc = _gla_scores(qt16, kt16, m, rev)
    v16 = v.astype(BF16)
    r = _mm(sc, v16)
    o_intra = jnp.concatenate([r[h * CH:(h + 1) * CH, h * HV:(h + 1) * HV] for h in range(NH)], axis=1)
    st = st_ref[...]
    st16 = st.astype(BF16)
    o = o_intra + _mm_nt(qt16, st16)
    st_ref[...] = st * dec + _mm_tn(v16, kh16) * m["bds"]
    return o, st16


def _gates(lr_ref, gc_ref, bs_ref, cols):
    return _logsig(_mm(lr_ref[...], gc_ref[:, cols]) + bs_ref[:, cols]) * GATE_NORM


def gla_fwd(P, gcat, gbias, tl):
    L = P.shape[0]
    nb = L // tl
    nc = tl // CH

    def body(qf, kf, vf, lf, qb, kb, vb, lb, gc_ref, bs_ref, of, ob, sf, sb, stf, stb,
             qtf, ktf, khf, dcf, qtb, ktb, khb, dcb):
        @pl.when(pl.program_id(0) == 0)
        def _():
            stf[...] = jnp.zeros_like(stf)
            stb[...] = jnp.zeros_like(stb)

        m = _gla_masks()
        for (q, k, lr, cols, rev, qt, kt, kh, dc) in ((qf, kf, lf, slice(0, DK), False, qtf, ktf, khf, dcf),
                                                      (qb, kb, lb, slice(DK, 2 * DK), True, qtb, ktb, khb, dcb)):
            p = _gla_tile_prep(q[...], k[...], _gates(lr, gc_ref, bs_ref, cols), m, rev, nc)
            qt[...] = p["qt"].astype(BF16)
            kt[...] = p["kt"].astype(BF16)
            kh[...] = p["kh"].astype(BF16)
            dc[...] = p["dec"]

        def chunk(c, carry):
            rows = pl.ds(pl.multiple_of(c * CH, CH), CH)
            o, st = _gla_chunk_fwd(qtf[rows, :], ktf[rows, :], khf[rows, :], vf[rows, :], dcf[pl.ds(c, 1), :], stf, m, False)
            of[rows, :] = o.astype(BF16)
            sf[c] = st
            cb = nc - 1 - c
            rows = pl.ds(pl.multiple_of(cb * CH, CH), CH)
            o, st = _gla_chunk_fwd(qtb[rows, :], ktb[rows, :], khb[rows, :], vb[rows, :], dcb[pl.ds(cb, 1), :], stb, m, True)
            ob[rows, :] = o.astype(BF16)
            sb[cb] = st
            return carry

        lax.fori_loop(0, nc, chunk, 0, unroll=2)

    fw = lambda cb: (lambda i: (i, cb))
    bw = lambda cb: (lambda i: (nb - 1 - i, cb))
    return pl.pallas_call(
        body, name="gla_fwd", grid=(nb,),
        in_specs=[pl.BlockSpec((tl, DK), fw(CB_Q)), pl.BlockSpec((tl, DK), fw(CB_K)), pl.BlockSpec((tl, DG), fw(CB_V)),
                  pl.BlockSpec((tl, LRW), fw(CB_LR)),
                  pl.BlockSpec((tl, DK), bw(CB_Q)), pl.BlockSpec((tl, DK), bw(CB_K)), pl.BlockSpec((tl, DG), bw(CB_V)),
                  pl.BlockSpec((tl, LRW), bw(CB_LR)),
                  pl.BlockSpec((LRW, 2 * DK), lambda i: (0, 0)), pl.BlockSpec((1, 2 * DK), lambda i: (0, 0))],
        out_specs=[pl.BlockSpec((tl, DG), lambda i: (i, 0)), pl.BlockSpec((tl, DG), lambda i: (nb - 1 - i, 0)),
                   pl.BlockSpec((nc, DG, DK), lambda i: (i, 0, 0)), pl.BlockSpec((nc, DG, DK), lambda i: (nb - 1 - i, 0, 0))],
        out_shape=[jax.ShapeDtypeStruct((L, DG), BF16), jax.ShapeDtypeStruct((L, DG), BF16),
                   jax.ShapeDtypeStruct((L // CH, DG, DK), BF16), jax.ShapeDtypeStruct((L // CH, DG, DK), BF16)],
        scratch_shapes=[pltpu.VMEM((DG, DK), F32), pltpu.VMEM((DG, DK), F32)]
        + [pltpu.VMEM((tl, DK), BF16)] * 3 + [pltpu.VMEM((nc, DK), F32)]
        + [pltpu.VMEM((tl, DK), BF16)] * 3 + [pltpu.VMEM((nc, DK), F32)],
        compiler_params=_cp("arbitrary"),
    )(P, P, P, P, P, P, P, P, gcat, gbias)


def _headnorm(o):
    oh, rs = [], []
    for h in range(NH):
        oo = o[:, h * HV:(h + 1) * HV]
        r = lax.rsqrt(jnp.mean(oo * oo, axis=-1, keepdims=True) + EPS)
        oh.append(oo * r)
        rs.append(r)
    return jnp.concatenate(oh, axis=1), rs


def mix_out(P, o_f, o_b, conv_a, ghn4, w_out, g2, x, tl):
    L = P.shape[0]
    nt = L // tl

    def body(gb, gc, gv, go, gcp, gvp, gcn, gvn, of, ob, ca, gh, wo, g2r, xr, ycat, yr, x1):
        i = pl.program_id(0)
        cp, cn = _halo_rows(gcp, gcn, i, nt - 1)
        vp, vn = _halo_rows(gvp, gvn, i, nt - 1)
        c = gc[...].astype(F32) * gv[...].astype(F32)
        cc, _, _ = _conv3(c, cp * vp, cn * vn, ca)
        ya = gb[...].astype(F32) * cc
        oh, _ = _headnorm(of[...].astype(F32) + ob[...].astype(F32))
        g = go[...].astype(F32)
        yb = g * _sigmoid(g) * (oh * gh[...])
        yc = jnp.concatenate([ya, yb], axis=1).astype(BF16)
        ycat[...] = yc
        y = _mm(yc, wo[...])
        yr[...] = y
        x1[...] = xr[...] + _rms(y, g2r[...])

    t = lambda cb: pl.BlockSpec((tl, DC), lambda i: (i, cb))
    hp = lambda cb: pl.BlockSpec((HALO16, DC), _prev_blk(tl, cb, HALO16))
    hn = lambda cb: pl.BlockSpec((HALO16, DC), _next_blk(tl, L, cb, HALO16))
    row = lambda n: pl.BlockSpec((tl, n), lambda i: (i, 0))
    full = lambda a: pl.BlockSpec(a.shape, lambda i: (0, 0))
    return pl.pallas_call(
        body, name="mix_out", grid=(nt,),
        in_specs=[t(CB_GB), t(CB_GC), t(CB_GV), t(CB_GO), hp(CB_GC), hp(CB_GV), hn(CB_GC), hn(CB_GV),
                  row(DG), row(DG), full(conv_a), full(ghn4), full(w_out), full(g2), row(D)],
        out_specs=[row(D), row(D), row(D)],
        out_shape=[jax.ShapeDtypeStruct((L, D), BF16), jax.ShapeDtypeStruct((L, D), F32),
                   jax.ShapeDtypeStruct((L, D), F32)],
        compiler_params=_cp("parallel"),
    )(P, P, P, P, P, P, P, P, o_f, o_b, conv_a, ghn4, w_out, g2, x)


NFF = 2
WFF = DFF // NFF
FFN_COL_CHUNKS = ((0, 512), (512, 1024), (1024, WFF))


def ffn_down(U, conv_ffn, w_down, g4, x1, tl):
    L = U.shape[0]
    nt = L // tl

    def body(u, up, un, cf, wd, g4r, x1r, y2, x2, ug, uv, zr):
        i = pl.program_id(0)
        acc = jnp.zeros((tl, D), F32)
        for j in range(NFF):
            gs = slice(j * WFF, (j + 1) * WFF)
            vs = slice(DFF + j * WFF, DFF + (j + 1) * WFF)
            z = []
            for s in (gs, vs):
                p, n = _halo_rows(up.at[:, s], un.at[:, s], i, nt - 1)
                z.append(_conv3(u[:, s].astype(F32), p, n, cf.at[:, s])[0])
            zz = (z[0] * _sigmoid(z[0]) * z[1]).astype(BF16)
            ug[:, gs] = z[0].astype(BF16)
            uv[:, gs] = z[1].astype(BF16)
            zr[:, gs] = zz
            acc = acc + _mm(zz, wd[gs, :])
        y2[...] = acc
        x2[...] = x1r[...] + _rms(acc, g4r[...])

    row = lambda n: pl.BlockSpec((tl, n), lambda i: (i, 0))
    full = lambda a: pl.BlockSpec(a.shape, lambda i: (0, 0))
    half = jax.ShapeDtypeStruct((L, DFF), BF16)
    return pl.pallas_call(
        body, name="ffn_down", grid=(nt,),
        in_specs=[row(2 * DFF), pl.BlockSpec((HALO16, 2 * DFF), _prev_blk(tl, 0, HALO16)),
                  pl.BlockSpec((HALO16, 2 * DFF), _next_blk(tl, L, 0, HALO16)),
                  full(conv_ffn), full(w_down), full(g4), row(D)],
        out_specs=[row(D), row(D), row(DFF), row(DFF), row(DFF)],
        out_shape=[jax.ShapeDtypeStruct((L, D), F32), jax.ShapeDtypeStruct((L, D), F32), half, half, half],
        compiler_params=_cp("parallel"),
    )(U, U, U, conv_ffn, w_down, g4, x1)


def loss_head(y, target, tl):
    L = y.shape[0]

    def body(yr, tr, dy, ls):
        @pl.when(pl.program_id(0) == 0)
        def _():
            ls[...] = jnp.zeros_like(ls)

        err = yr[...] - tr[...]
        dy[...] = err * (1.0 / D)
        ls[...] += (0.5 / D) * jnp.sum(err * err)

    row = pl.BlockSpec((tl, D), lambda i: (i, 0))
    return pl.pallas_call(
        body, name="loss_head", grid=(L // tl,), in_specs=[row, row],
        out_specs=[row, pl.BlockSpec((8, 128), lambda i: (0, 0))],
        out_shape=[jax.ShapeDtypeStruct((L, D), F32), jax.ShapeDtypeStruct((8, 128), F32)],
        compiler_params=_cp("arbitrary"),
    )(y, target)


def rms_bwd_pre(dout, y, g, tl, after=()):
    L = y.shape[0]

    def body(dr, yr, gr, *rest):
        dy, dg = rest[-2:]

        @pl.when(pl.program_id(0) == 0)
        def _():
            dg[...] = jnp.zeros_like(dg)

        a, b = _rms_bwd(dr[...], yr[...], gr[...])
        dy[...] = a.astype(BF16)
        dg[...] += b

    row = pl.BlockSpec((tl, D), lambda i: (i, 0))
    vec = pl.BlockSpec((1, D), lambda i: (0, 0))
    return pl.pallas_call(
        body, name="rms_bwd_pre", grid=(L // tl,), in_specs=[row, row, vec] + [_ANY] * len(after), out_specs=[row, vec],
        out_shape=[jax.ShapeDtypeStruct((L, D), BF16), jax.ShapeDtypeStruct((1, D), F32)],
        compiler_params=_cp("arbitrary"),
    )(dout, y, g, *after)


def ffn_bwd1(dy2, ug, uv, w_down, tl):
    L = ug.shape[0]

    def body(dy, ugr, uvr, wd, dug, duv):
        a = ugr[...].astype(F32)
        b = uvr[...].astype(F32)
        sg = _sigmoid(a)
        silu = a * sg
        dz = _mm_nt(dy[...], wd[...])
        dug[...] = (dz * b * (sg + silu * (1.0 - sg))).astype(BF16)
        duv[...] = (dz * silu).astype(BF16)

    tile = pl.BlockSpec((tl, WFF), lambda j, i: (i, j))
    half = jax.ShapeDtypeStruct((L, DFF), BF16)
    return pl.pallas_call(
        body, name="ffn_bwd1", grid=(NFF, L // tl),
        in_specs=[pl.BlockSpec((tl, D), lambda j, i: (i, 0)), tile, tile, pl.BlockSpec((WFF, D), lambda j, i: (j, 0))],
        out_specs=[tile, tile], out_shape=[half, half],
        compiler_params=_cp("parallel", "parallel"),
    )(dy2, ug, uv, w_down)


def ffn_bwd2(du_g, du_v, U, conv_ffn, w_up, x1, dres, g3, tl):
    L = x1.shape[0]
    nt = L // tl

    def body(dg_, dv_, dgp, dgn, dvp, dvn, ugr, uvr, cg, cv, wg, wv, x1r, drr, g3r, dUg, dUv, dx1, dg3, dcg, dcv, acc):
        i = pl.program_id(0)
        j = pl.program_id(1)

        @pl.when((i == 0) & (j == 0))
        def _():
            dg3[...] = jnp.zeros_like(dg3)
            dcg[...] = jnp.zeros_like(dcg)
            dcv[...] = jnp.zeros_like(dcv)

        part = None
        for c0, c1 in FFN_COL_CHUNKS:
            cs = slice(c0, c1)
            for d_ref, dp_ref, dn_ref, cw_ref, u_ref, dc, dU, w in ((dg_, dgp, dgn, cg, ugr, dcg, dUg, wg),
                                                                    (dv_, dvp, dvn, cv, uvr, dcv, dUv, wv)):
                p8, n8 = _halo_rows(dp_ref.at[:, cs], dn_ref.at[:, cs], i, nt - 1)
                d = d_ref[:, cs].astype(F32)
                dm1, dp1 = _shifts(d, p8, n8)
                du = (cw_ref[0:1, cs] * dp1 + cw_ref[1:2, cs] * d + cw_ref[2:3, cs] * dm1).astype(BF16)
                dU[:, cs] = du
                u = u_ref[:, cs].astype(F32)
                for k, t in enumerate((dp1, d, dm1)):
                    dc[j, k:k + 1, cs] += jnp.sum(t * u, axis=0, keepdims=True)
                term = _mm_nt(du, w[:, cs])
                part = term if part is None else part + term

        @pl.when(j == 0)
        def _():
            acc[...] = part

        @pl.when(j > 0)
        def _():
            acc[...] += part

        @pl.when(j == NFF - 1)
        def _():
            dx, dg = _rms_bwd(acc[...], x1r[...], g3r[...])
            dx1[...] = drr[...] + dx
            dg3[...] += dg

    tile = pl.BlockSpec((tl, WFF), lambda i, j: (i, j))
    prev = pl.BlockSpec((HALO16, WFF), lambda i, j: (_prev_row_blk(i, tl, HALO16), j))
    nxt = pl.BlockSpec((HALO16, WFF), lambda i, j: (_next_row_blk(i, tl, L, HALO16), j))
    cw = lambda off: pl.BlockSpec((3, WFF), lambda i, j: (0, off + j))
    ww = lambda off: pl.BlockSpec((None, D, WFF), lambda i, j: (off + j, 0, 0))
    row = pl.BlockSpec((tl, D), lambda i, j: (i, 0))
    vec = pl.BlockSpec((1, D), lambda i, j: (0, 0))
    ut = lambda off: pl.BlockSpec((tl, WFF), lambda i, j: (i, off + j))
    dcs = pl.BlockSpec((NFF, 8, WFF), lambda i, j: (0, 0, 0))
    return pl.pallas_call(
        body, name="ffn_bwd2", grid=(nt, NFF),
        in_specs=[tile, tile, prev, nxt, prev, nxt, ut(0), ut(NFF), cw(0), cw(NFF), ww(0), ww(NFF), row, row, vec],
        out_specs=[tile, tile, row, vec, dcs, dcs],
        out_shape=[jax.ShapeDtypeStruct((L, DFF), BF16), jax.ShapeDtypeStruct((L, DFF), BF16),
                   jax.ShapeDtypeStruct((L, D), F32), jax.ShapeDtypeStruct((1, D), F32),
                   jax.ShapeDtypeStruct((NFF, 8, WFF), F32), jax.ShapeDtypeStruct((NFF, 8, WFF), F32)],
        scratch_shapes=[pltpu.VMEM((tl, D), F32)],
        compiler_params=_cp("arbitrary", "arbitrary"),
    )(du_g, du_v, du_g, du_g, du_v, du_v, U, U, conv_ffn, conv_ffn, w_up, w_up, x1, dres, g3)


def matmul_tn(a, b, ta, tn, tl, name, into=None):
    L, Ka = a.shape
    N = b.shape[1]

    def body(ar, br, *rest):
        o = rest[-1]

        @pl.when(pl.program_id(2) == 0)
        def _():
            o[...] = jnp.zeros_like(o)

        o[...] += _mm_tn(ar[...], br[...]).reshape(o.shape)

    in_specs = [pl.BlockSpec((tl, ta), lambda p, q, l: (l, p)), pl.BlockSpec((tl, tn), lambda p, q, l: (l, q))]
    if into is None:
        return pl.pallas_call(
            body, name=name, grid=(Ka // ta, N // tn, L // tl), in_specs=in_specs,
            out_specs=pl.BlockSpec((ta, tn), lambda p, q, l: (p, q)),
            out_shape=jax.ShapeDtypeStruct((Ka, N), F32),
            compiler_params=_cp("parallel", "parallel", "arbitrary"),
        )(a, b)
    buf, blk, idx = into
    return pl.pallas_call(
        body, name=name, grid=(Ka // ta, N // tn, L // tl), in_specs=in_specs + [_ANY],
        out_specs=pl.BlockSpec(blk, lambda p, q, l: idx(p, q)),
        out_shape=jax.ShapeDtypeStruct(buf.shape, F32), input_output_aliases={2: 0},
        compiler_params=_cp("parallel", "parallel", "arbitrary"),
    )(a, b, buf)


def mix_bwd1(dy, w_out, P, o_f, o_b, conv_a, ghn4, tl):
    L = P.shape[0]
    nt = L // tl

    def body(dyr, wo, gb, gc, gv, go, gcp, gvp, gcn, gvn, of, ob, ca, gh, dgb, dcc, dgo, do, dca, dgh):
        i = pl.program_id(0)

        @pl.when(i == 0)
        def _():
            dca[...] = jnp.zeros_like(dca)
            dgh[...] = jnp.zeros_like(dgh)

        dycat = _mm_nt(dyr[...], wo[...])
        dya = dycat[:, 0:DC]
        dyb = dycat[:, DC:D]
        cp, cn = _halo_rows(gcp, gcn, i, nt - 1)
        vp, vn = _halo_rows(gvp, gvn, i, nt - 1)
        c = gc[...].astype(F32) * gv[...].astype(F32)
        cc, c_m1, c_p1 = _conv3(c, cp * vp, cn * vn, ca)
        dgb[...] = (dya * cc).astype(BF16)
        d = dya * gb[...].astype(F32)
        dcc[...] = d.astype(BF16)
        for k, s in enumerate((c_m1, c, c_p1)):
            dca[k:k + 1, :] += jnp.sum(d * s, axis=0, keepdims=True)
        oh, rs = _headnorm(of[...].astype(F32) + ob[...].astype(F32))
        g = go[...].astype(F32)
        sg = _sigmoid(g)
        silu = g * sg
        dgo[...] = (dyb * (oh * gh[...]) * (sg * (1.0 + g * (1.0 - sg)))).astype(BF16)
        don = dyb * silu
        t = jnp.sum(don * oh, axis=0, keepdims=True)
        dgh[0:1, :] += t[:, 0:HV] + t[:, HV:2 * HV] + t[:, 2 * HV:3 * HV] + t[:, 3 * HV:4 * HV]
        doh = don * gh[...]
        parts = []
        for h in range(NH):
            hs = slice(h * HV, (h + 1) * HV)
            parts.append(rs[h] * (doh[:, hs] - oh[:, hs] * jnp.mean(doh[:, hs] * oh[:, hs], axis=-1, keepdims=True)))
        do[...] = jnp.concatenate(parts, axis=1).astype(BF16)

    t = lambda cb: pl.BlockSpec((tl, DC), lambda i: (i, cb))
    hp = lambda cb: pl.BlockSpec((HALO16, DC), _prev_blk(tl, cb, HALO16))
    hn = lambda cb: pl.BlockSpec((HALO16, DC), _next_blk(tl, L, cb, HALO16))
    row = lambda n: pl.BlockSpec((tl, n), lambda i: (i, 0))
    full = lambda a: pl.BlockSpec(a.shape, lambda i: (0, 0))
    act16 = lambda n: jax.ShapeDtypeStruct((L, n), BF16)
    return pl.pallas_call(
        body, name="mix_bwd1", grid=(nt,),
        in_specs=[row(D), full(w_out), t(CB_GB), t(CB_GC), t(CB_GV), t(CB_GO), hp(CB_GC), hp(CB_GV), hn(CB_GC), hn(CB_GV),
                  row(DG), row(DG), full(conv_a), full(ghn4)],
        out_specs=[row(DC), row(DC), row(DG), row(DG), pl.BlockSpec((8, DC), lambda i: (0, 0)),
                   pl.BlockSpec((8, HV), lambda i: (0, 0))],
        out_shape=[act16(DC), act16(DC), act16(DG), act16(DG), jax.ShapeDtypeStruct((8, DC), F32),
                   jax.ShapeDtypeStruct((8, HV), F32)],
        compiler_params=_cp("arbitrary"),
    )(dy, w_out, P, P, P, P, P, P, P, P, o_f, o_b, conv_a, ghn4)


def _gla_chunk_bwd(qt, kt, kh, v, do, st16, dec, g_ref, m, rev):
    qt16, kt16, kh16, v16, do16 = (t.astype(BF16) for t in (qt, kt, kh, v, do))
    qs, sc = _gla_scores(qt16, kt16, m, rev)
    g = g_ref[...]
    g16 = g.astype(BF16)
    dob = _tile4(do16) * m["bdo"].astype(BF16)
    dv = _mm_tn(sc, dob) + _mm_nt(kh16, g16)
    dsc = jnp.where(m["triu4"] if rev else m["tril4"], _mm_nt(dob, v16), 0.0)
    r1 = _mm(dsc, kt16) * m["bdq"]
    dqt = r1[0:CH] + r1[CH:2 * CH] + r1[2 * CH:3 * CH] + r1[3 * CH:4 * CH] + _mm(do16, st16)
    dkt = _mm_tn(dsc, qs)
    dkh = _mm(v16, g16)
    dd = jnp.sum(g * st16.astype(F32), axis=0, keepdims=True)
    g_ref[...] = g * dec + _mm_tn(do16, qt16) * m["bds"]
    return dv, dqt, dkt, dkh, dd


def gla_bwd(P, do, sf, sb, gcat, gbias, tl):
    L = P.shape[0]
    nb = L // tl
    nc = tl // CH

    def body(qf, kf, vf, lf, dof, sfr, qb, kb, vb, lb, dob, sbr, gc_ref, bs_ref,
             dqf, dkf, dvf, daf, dqb, dkb, dvb, dab, gf, gbk, *scr):
        @pl.when(pl.program_id(0) == 0)
        def _():
            gf[...] = jnp.zeros_like(gf)
            gbk[...] = jnp.zeros_like(gbk)

        m = _gla_masks()
        keys = ("qt", "kt", "kh", "e", "einv", "eout", "dec")
        pf = dict(zip(keys + ("dd",), scr[0:8]))
        pb = dict(zip(keys + ("dd",), scr[8:16]))
        for (q, k, lr, cols, rev, pr) in ((qf, kf, lf, slice(0, DK), False, pf), (qb, kb, lb, slice(DK, 2 * DK), True, pb)):
            p = _gla_tile_prep(q[...], k[...], _gates(lr, gc_ref, bs_ref, cols), m, rev, nc)
            for key in keys:
                pr[key][...] = p[key]

        def step(c, v, dor, st, g_ref, pr, dq, dk, dv, da, rev):
            rows = pl.ds(pl.multiple_of(c * CH, CH), CH)
            dvc, dqt, dkt, dkh, dd = _gla_chunk_bwd(pr["qt"][rows, :], pr["kt"][rows, :], pr["kh"][rows, :], v[rows, :],
                                                    dor[rows, :], st[c], pr["dec"][pl.ds(c, 1), :], g_ref, m, rev)
            dv[rows, :] = dvc
            dq[rows, :] = dqt
            dk[rows, :] = dkt
            da[rows, :] = dkh
            pr["dd"][pl.ds(c, 1), :] = dd

        def chunk(c, carry):
            step(nc - 1 - c, vf, dof, sfr, gf, pf, dqf, dkf, dvf, daf, False)
            step(c, vb, dob, sbr, gbk, pb, dqb, dkb, dvb, dab, True)
            return carry

        lax.fori_loop(0, nc, chunk, 0, unroll=2)

        def finish(pr, dq, dk, da, rev):
            dqt, dkt, dkh = dq[...], dk[...], da[...]
            kk = dkh * pr["kh"][...]
            dcum = dqt * pr["qt"][...] - dkt * pr["kt"][...] - kk
            dtot = pr["dd"][...] * pr["dec"][...]
            tri_t = m["tril"] if rev else m["triu"]
            parts = []
            for c in range(nc):
                rs = slice(c * CH, (c + 1) * CH)
                parts.append(_mm_tri(tri_t, dcum[rs]) + (jnp.sum(kk[rs], axis=0, keepdims=True) + dtot[c:c + 1]))
            da[...] = jnp.concatenate(parts, axis=0)
            dq[...] = dqt * pr["e"][...] * QSCALE
            dk[...] = dkt * pr["einv"][...] + dkh * pr["eout"][...]

        finish(pf, dqf, dkf, daf, False)
        finish(pb, dqb, dkb, dab, True)

    fwd_dir = lambda cb: (lambda i: (nb - 1 - i, cb))
    bwd_dir = lambda cb: (lambda i: (i, cb))

    def side(ix):
        return [pl.BlockSpec((tl, DK), ix(CB_Q)), pl.BlockSpec((tl, DK), ix(CB_K)), pl.BlockSpec((tl, DG), ix(CB_V)),
                pl.BlockSpec((tl, LRW), ix(CB_LR)), pl.BlockSpec((tl, DG), ix(0)),
                pl.BlockSpec((nc, DG, DK), lambda i: (ix(0)(i)[0], 0, 0))]

    def outs(ix):
        return [pl.BlockSpec((tl, DK), ix(0)), pl.BlockSpec((tl, DK), ix(0)), pl.BlockSpec((tl, DG), ix(0)),
                pl.BlockSpec((tl, DK), ix(0))]

    o_shape = [jax.ShapeDtypeStruct((L, DK), F32), jax.ShapeDtypeStruct((L, DK), F32),
               jax.ShapeDtypeStruct((L, DG), F32), jax.ShapeDtypeStruct((L, DK), F32)]
    return pl.pallas_call(
        body, name="gla_bwd", grid=(nb,),
        in_specs=side(fwd_dir) + side(bwd_dir) + [pl.BlockSpec((LRW, 2 * DK), lambda i: (0, 0)),
                                                  pl.BlockSpec((1, 2 * DK), lambda i: (0, 0))],
        out_specs=outs(fwd_dir) + outs(bwd_dir),
        out_shape=o_shape + o_shape,
        scratch_shapes=[pltpu.VMEM((DG, DK), F32), pltpu.VMEM((DG, DK), F32)]
        + ([pltpu.VMEM((tl, DK), F32)] * 6 + [pltpu.VMEM((nc, DK), F32)] * 2) * 2,
        compiler_params=_cp("arbitrary"),
    )(P, P, P, P, do, sf, P, P, P, P, do, sb, gcat, gbias)


def mix_bwd2(dgb, dcc, dgo, gl, P, conv_a, gcat, gbias, w_in, x, dres, g1, tl):
    L = P.shape[0]
    nt = L // tl

    def body(dgbr, dccr, dccp, dccn, dgor, dqf, dkf, dvf, daf, dqb, dkb, dvb, dab, gc, gv, lr, ca, gcr, bsr, wi,
             xr, drr, g1r, dP, dx, dg1, dgcat, dbias):
        i = pl.program_id(0)

        @pl.when(i == 0)
        def _():
            dg1[...] = jnp.zeros_like(dg1)
            dgcat[...] = jnp.zeros_like(dgcat)
            dbias[...] = jnp.zeros_like(dbias)

        p, n = _halo_rows(dccp, dccn, i, nt - 1)
        dc = _conv3_t(dccr[...].astype(F32), p, n, ca)
        pre = _mm(lr[...], gcr[...]) + bsr[...]
        da = jnp.concatenate([daf[...], dab[...]], axis=1)
        dpre = da * GATE_NORM * (1.0 - _sigmoid(pre))
        dpre16 = dpre.astype(BF16)
        dP[:, 0:DC] = dgbr[...].astype(BF16)
        dP[:, DC:2 * DC] = (dc * gv[...].astype(F32)).astype(BF16)
        dP[:, 2 * DC:3 * DC] = (dc * gc[...].astype(F32)).astype(BF16)
        dP[:, 1536:1792] = (dqf[...] + dqb[...]).astype(BF16)
        dP[:, 1792:2048] = (dkf[...] + dkb[...]).astype(BF16)
        dP[:, 2048:2560] = (dvf[...] + dvb[...]).astype(BF16)
        dP[:, 2560:3072] = dgor[...].astype(BF16)
        dP[:, 3072:3200] = _mm_nt(dpre16, gcr[...]).astype(BF16)
        dgcat[...] += _mm_tn(lr[...], dpre16)
        dbias[0:1, :] += jnp.sum(dpre, axis=0, keepdims=True)
        dh, dg = _rms_bwd(_mm_nt(dP[...], wi[...]), xr[...], g1r[...])
        dx[...] = drr[...] + dh
        dg1[...] += dg

    row = lambda n: pl.BlockSpec((tl, n), lambda i: (i, 0))
    t = lambda w, cb: pl.BlockSpec((tl, w), lambda i: (i, cb))
    full = lambda a: pl.BlockSpec(a.shape, lambda i: (0, 0))
    return pl.pallas_call(
        body, name="mix_bwd2", grid=(nt,),
        in_specs=[row(DC), row(DC), pl.BlockSpec((HALO16, DC), _prev_blk(tl, 0, HALO16)),
                  pl.BlockSpec((HALO16, DC), _next_blk(tl, L, 0, HALO16)),
                  row(DG), row(DK), row(DK), row(DG), row(DK), row(DK), row(DK), row(DG), row(DK),
                  t(DC, CB_GC), t(DC, CB_GV), t(LRW, CB_LR), full(conv_a), full(gcat), full(gbias), full(w_in),
                  row(D), row(D), full(g1)],
        out_specs=[row(DINP), row(D), pl.BlockSpec((1, D), lambda i: (0, 0)), pl.BlockSpec((LRW, 2 * DK), lambda i: (0, 0)),
                   pl.BlockSpec((8, 2 * DK), lambda i: (0, 0))],
        out_shape=[jax.ShapeDtypeStruct((L, DINP), BF16), jax.ShapeDtypeStruct((L, D), F32),
                   jax.ShapeDtypeStruct((1, D), F32), jax.ShapeDtypeStruct((LRW, 2 * DK), F32),
                   jax.ShapeDtypeStruct((8, 2 * DK), F32)],
        compiler_params=_cp("arbitrary"),
    )(dgb, dcc, dcc, dcc, dgo, *gl, P, P, P, conv_a, gcat, gbias, w_in, x, dres, g1)


def _row_tile(rows, cols):
    if rows * cols * 4 <= 2 * 1024 * 1024:
        return rows
    best = 8
    for t in range(8, rows, 8):
        if rows % t == 0 and t * cols * 4 <= 2 * 1024 * 1024:
            best = t
    return best


def adamw(w, g, m, v, name):
    shape = w.shape
    cols = shape[-1]
    w2, g2, m2, v2 = (a.reshape(-1, cols) for a in (w, g, m, v))
    rows = w2.shape[0]
    tr = _row_tile(rows, cols)

    def body(wr, gr, mr, vr, dl, nm, nv):
        gg = gr[...]
        mm = B1 * mr[...] + (1.0 - B1) * gg
        vv = B2 * vr[...] + (1.0 - B2) * (gg * gg)
        m_hat = mm / (1.0 - B1 ** STEP)
        v_hat = vv / (1.0 - B2 ** STEP)
        dl[...] = -LR * (m_hat / (jnp.sqrt(v_hat) + AEPS) + WD * wr[...])
        nm[...] = mm
        nv[...] = vv

    blk = pl.BlockSpec((tr, cols), lambda i: (i, 0))
    o = jax.ShapeDtypeStruct((rows, cols), F32)
    d, nm, nv = pl.pallas_call(
        body, name=name, grid=(rows // tr,), in_specs=[blk] * 4, out_specs=[blk] * 3, out_shape=[o, o, o],
        compiler_params=_cp("parallel"),
    )(w2, g2, m2, v2)
    return d.reshape(shape), nm.reshape(shape), nv.reshape(shape)


def _place():
    return lax.axis_index("x"), lax.axis_index("y"), lax.axis_index("c")


def allgather8(v, name):
    mp, n = v.shape

    def body(x_ref, out_ref, send_sems, recv_sems, local_sem):
        x, y, c = _place()
        me, sibling = (x, y, c), (x, y, 1 - c)
        chips = [(1 - x, y), (x, 1 - y), (1 - x, 1 - y)]

        def rows(px, py, pc):
            return out_ref.at[pl.ds((4 * px + 2 * py + pc) * mp, mp), :]

        def copy(k, block, to, src=None):
            return pltpu.make_async_remote_copy(
                src_ref=rows(*block) if src is None else src, dst_ref=rows(*block),
                send_sem=send_sems.at[k], recv_sem=recv_sems.at[k], device_id=to, device_id_type=MESH)

        mine = pltpu.make_async_copy(x_ref, rows(*me), local_sem)
        mine.start()
        first = [copy(0, me, sibling, src=x_ref)]
        first += [copy(1 + j, me, (*chip, c), src=x_ref) for j, chip in enumerate(chips)]
        for cp in first:
            cp.start()
        passed = [copy(4 + j, (*chip, c), sibling) for j, chip in enumerate(chips)]
        for j, chip in enumerate(chips):
            copy(1 + j, (*chip, c), me).wait_recv()
            passed[j].start()
        copy(0, sibling, me).wait_recv()
        for j, chip in enumerate(chips):
            copy(4 + j, (*chip, 1 - c), me).wait_recv()
        for cp in first + passed:
            cp.wait_send()
        mine.wait()

    return pl.pallas_call(
        body, name=name, out_shape=jax.ShapeDtypeStruct((8 * mp, n), v.dtype),
        in_specs=[pl.BlockSpec(memory_space=pltpu.VMEM)], out_specs=pl.BlockSpec(memory_space=pltpu.VMEM),
        scratch_shapes=[pltpu.SemaphoreType.DMA((7,)), pltpu.SemaphoreType.DMA((7,)), pltpu.SemaphoreType.DMA],
        compiler_params=pltpu.CompilerParams(vmem_limit_bytes=VMEM_LIMIT),
    )(v)


def sum8(v, mp):
    def body(x_ref, o_ref):
        acc = x_ref[0:mp, :]
        for d in range(1, 8):
            acc = acc + x_ref[d * mp:(d + 1) * mp, :]
        o_ref[...] = acc

    return pl.pallas_call(body, name="sum8", out_shape=jax.ShapeDtypeStruct((mp, v.shape[1]), F32),
                          compiler_params=pltpu.CompilerParams(vmem_limit_bytes=VMEM_LIMIT))(v)


_ANY = pl.BlockSpec(memory_space=pl.ANY)


def _row_half(ref, lead, h):
    hr = ref.shape[-2] // 2
    return ref.at[(*lead, pl.ds(h * hr, hr), slice(None))]


def allgather_weights(slots):
    n = len(slots)

    def body(*refs):
        s_refs, o_refs, (send_sems, recv_sems) = refs[:n], refs[n:2 * n], refs[2 * n:]
        x, y, c = _place()
        me = 2 * x + y
        sibling = (x, y, 1 - c)
        chips = [(1 - x, y), (x, 1 - y), (1 - x, 1 - y)]

        def half(ref, slot, h):
            return _row_half(ref, (slot, slice(None)), h)

        def copy(k, src, dst, to):
            return pltpu.make_async_remote_copy(src_ref=src, dst_ref=dst, send_sem=send_sems.at[k],
                                                recv_sem=recv_sems.at[k], device_id=to, device_id_type=MESH)

        first = [copy(6 * a + k, half(s_refs[a], me, c), half(o_refs[a], me, c), (px, py, c))
                 for k, (px, py) in enumerate(chips) for a in range(n)]
        for cp in first:
            cp.start()
        passed = []
        for k, (px, py) in enumerate(chips):
            for a in range(n):
                got = half(o_refs[a], 2 * px + py, c)
                copy(6 * a + k, half(s_refs[a], me, c), got, (px, py, c)).wait_recv()
                cp = copy(6 * a + 3 + k, got, got, sibling)
                cp.start()
                passed.append(cp)
        for k, (px, py) in enumerate(chips):
            for a in range(n):
                got = half(o_refs[a], 2 * px + py, 1 - c)
                copy(6 * a + 3 + k, got, got, sibling).wait_recv()
        for cp in first + passed:
            cp.wait_send()

    return pl.pallas_call(
        body, name="allgather_weights", out_shape=[jax.ShapeDtypeStruct(s.shape, s.dtype) for s in slots],
        in_specs=[_ANY] * n, out_specs=[_ANY] * n, input_output_aliases={a: a for a in range(n)},
        scratch_shapes=[pltpu.SemaphoreType.DMA((6 * n,)), pltpu.SemaphoreType.DMA((6 * n,))],
    )(*slots)


_HBM = pl.BlockSpec(memory_space=pltpu.HBM)
_SEM = pl.BlockSpec(memory_space=pltpu.SEMAPHORE)
_EFFECT = pltpu.SideEffectType.DATAFLOW_SIDE_EFFECTING


def gather_start(slots, name):
    n = len(slots)

    def body(*refs):
        s_refs, send_sems, recv_sems, token = refs[:n], refs[n], refs[n + 1], refs[-1]
        x, y, c = _place()
        me = 2 * x + y
        for k, (px, py) in enumerate([(1 - x, y), (x, 1 - y), (1 - x, 1 - y)]):
            for a in range(n):
                pltpu.make_async_remote_copy(
                    src_ref=s_refs[a].at[me], dst_ref=s_refs[a].at[me], send_sem=send_sems.at[3 * a + k],
                    recv_sem=recv_sems.at[3 * a + k], device_id=(px, py, c), device_id_type=MESH).start()
        token[...] = jnp.zeros_like(token)

    out = pl.pallas_call(
        body, name=name,
        out_shape=(pltpu.SemaphoreType.DMA((3 * n,)), pltpu.SemaphoreType.DMA((3 * n,)),
                   *[pltpu.HBM(s.shape, s.dtype) for s in slots], jax.ShapeDtypeStruct((8, 128), F32)),
        in_specs=[_HBM] * n, out_specs=(_SEM, _SEM, *[_HBM] * n, pl.BlockSpec(memory_space=pltpu.VMEM)),
        input_output_aliases={a: 2 + a for a in range(n)},
        compiler_params=pltpu.CompilerParams(has_side_effects=_EFFECT),
    )(*[pltpu.with_memory_space_constraint(s, pltpu.HBM) for s in slots])
    return out[0], out[1], list(out[2:2 + n]), out[-1]


def gather_wait(send_sems, recv_sems, slots, after, name):
    n = len(slots)

    def body(*refs):
        s_refs, ssem, rsem = refs[:n], refs[n], refs[n + 1]
        x, y, c = _place()
        me = 2 * x + y
        for k, (px, py) in enumerate([(1 - x, y), (x, 1 - y), (1 - x, 1 - y)]):
            for a in range(n):
                cp = pltpu.make_async_remote_copy(
                    src_ref=s_refs[a].at[me], dst_ref=s_refs[a].at[2 * px + py], send_sem=ssem.at[3 * a + k],
                    recv_sem=rsem.at[3 * a + k], device_id=(px, py, c), device_id_type=MESH)
                cp.wait_send()
                cp.wait_recv()

    return pl.pallas_call(
        body, name=name, out_shape=[pltpu.HBM(s.shape, s.dtype) for s in slots],
        in_specs=[_HBM] * n + [_SEM, _SEM, _ANY], out_specs=[_HBM] * n,
        input_output_aliases={a: a for a in range(n)},
        compiler_params=pltpu.CompilerParams(has_side_effects=_EFFECT),
    )(*slots, send_sems, recv_sems, after)


def rs_sibling_halves(gs):
    n = len(gs)

    def body(*refs):
        g_refs, r_refs, (send_sems, recv_sems) = refs[:n], refs[n:2 * n], refs[2 * n:]
        x, y, c = _place()
        cps = [pltpu.make_async_remote_copy(
            src_ref=_row_half(g_refs[a], (slice(None), slice(None)), 1 - c), dst_ref=r_refs[a],
            send_sem=send_sems.at[a], recv_sem=recv_sems.at[a], device_id=(x, y, 1 - c), device_id_type=MESH)
            for a in range(n)]
        for cp in cps:
            cp.start()
        for cp in cps:
            cp.wait()

    return pl.pallas_call(
        body, name="rs_sibling_halves",
        out_shape=[jax.ShapeDtypeStruct((*g.shape[:2], g.shape[2] // 2, g.shape[3]), F32) for g in gs],
        in_specs=[_ANY] * n, out_specs=[_ANY] * n,
        scratch_shapes=[pltpu.SemaphoreType.DMA((n,)), pltpu.SemaphoreType.DMA((n,))],
    )(*gs)


def rs_chipsum16(g, recv1, cidx, name):
    nl, hr, cols = recv1.shape[1:]

    def body(c_ref, g_ref, r_ref, o_ref):
        o_ref[...] = (g_ref[...] + r_ref[...]).astype(BF16)

    blk = (1, 1, hr, cols)
    return pl.pallas_call(
        body, name=name, out_shape=jax.ShapeDtypeStruct(recv1.shape, BF16),
        grid_spec=pltpu.PrefetchScalarGridSpec(
            num_scalar_prefetch=1, grid=(4, nl),
            in_specs=[pl.BlockSpec(blk, lambda j, l, c: (j, l, c[0], 0)), pl.BlockSpec(blk, lambda j, l, c: (j, l, 0, 0))],
            out_specs=pl.BlockSpec(blk, lambda j, l, c: (j, l, 0, 0))),
        compiler_params=_cp("parallel", "parallel"),
    )(cidx, g, recv1)


def rs_exchange_chips(cs):
    n = len(cs)

    def body(*refs):
        s_refs, r_refs, (send_sems, recv_sems) = refs[:n], refs[n:2 * n], refs[2 * n:]
        x, y, c = _place()
        chips = [(1 - x, y), (x, 1 - y), (1 - x, 1 - y)]
        cps = [pltpu.make_async_remote_copy(
            src_ref=s_refs[a].at[2 * px + py], dst_ref=r_refs[a].at[k], send_sem=send_sems.at[3 * a + k],
            recv_sem=recv_sems.at[3 * a + k], device_id=(px, py, c), device_id_type=MESH)
            for k, (px, py) in enumerate(chips) for a in range(n)]
        for cp in cps:
            cp.start()
        for cp in cps:
            cp.wait()

    return pl.pallas_call(
        body, name="rs_exchange_chips", out_shape=[jax.ShapeDtypeStruct((3, *s.shape[1:]), BF16) for s in cs],
        in_specs=[_ANY] * n, out_specs=[_ANY] * n,
        scratch_shapes=[pltpu.SemaphoreType.DMA((3 * n,)), pltpu.SemaphoreType.DMA((3 * n,))],
    )(*cs)


def exchange_start(cs, name):
    n = len(cs)
    lands = [lax.empty((3, *c.shape[1:]), BF16) for c in cs]

    def body(*refs):
        s_refs, l_refs, send_sems, recv_sems, token = refs[:n], refs[n:2 * n], refs[2 * n], refs[2 * n + 1], refs[-1]
        x, y, c = _place()
        for k, (px, py) in enumerate([(1 - x, y), (x, 1 - y), (1 - x, 1 - y)]):
            for a in range(n):
                pltpu.make_async_remote_copy(
                    src_ref=s_refs[a].at[2 * px + py], dst_ref=l_refs[a].at[k], send_sem=send_sems.at[3 * a + k],
                    recv_sem=recv_sems.at[3 * a + k], device_id=(px, py, c), device_id_type=MESH).start()
        token[...] = jnp.zeros_like(token)

    bufs = list(cs) + lands
    out = pl.pallas_call(
        body, name=name,
        out_shape=(pltpu.SemaphoreType.DMA((3 * n,)), pltpu.SemaphoreType.DMA((3 * n,)),
                   *[pltpu.HBM(b.shape, b.dtype) for b in bufs], jax.ShapeDtypeStruct((8, 128), F32)),
        in_specs=[_HBM] * (2 * n), out_specs=(_SEM, _SEM, *[_HBM] * (2 * n), pl.BlockSpec(memory_space=pltpu.VMEM)),
        input_output_aliases={i: 2 + i for i in range(2 * n)},
        compiler_params=pltpu.CompilerParams(has_side_effects=_EFFECT),
    )(*[pltpu.with_memory_space_constraint(b, pltpu.HBM) for b in bufs])
    return out[0], out[1], list(out[2:2 + n]), list(out[2 + n:2 + 2 * n]), out[-1]


def exchange_wait(send_sems, recv_sems, cs, lands, after, name):
    n = len(cs)

    def body(*refs):
        s_refs, l_refs, ssem, rsem = refs[:n], refs[n:2 * n], refs[2 * n], refs[2 * n + 1]
        x, y, c = _place()
        for k, (px, py) in enumerate([(1 - x, y), (x, 1 - y), (1 - x, 1 - y)]):
            for a in range(n):
                cp = pltpu.make_async_remote_copy(
                    src_ref=s_refs[a].at[2 * px + py], dst_ref=l_refs[a].at[k], send_sem=ssem.at[3 * a + k],
                    recv_sem=rsem.at[3 * a + k], device_id=(px, py, c), device_id_type=MESH)
                cp.wait_send()
                cp.wait_recv()

    bufs = list(cs) + list(lands)
    out = pl.pallas_call(
        body, name=name, out_shape=[pltpu.HBM(b.shape, b.dtype) for b in bufs],
        in_specs=[_HBM] * (2 * n) + [_SEM, _SEM, _ANY], out_specs=[_HBM] * (2 * n),
        input_output_aliases={i: i for i in range(2 * n)},
        compiler_params=pltpu.CompilerParams(has_side_effects=_EFFECT),
    )(*bufs, send_sems, recv_sems, after)
    return list(out[n:])


def rs_final_sum(g, recv1, recv2, idx, name):
    nl, hr, cols = recv1.shape[1:]

    def body(i_ref, g_ref, r1_ref, r2_ref, o_ref):
        acc = g_ref[0, 0] + r1_ref[0, 0]
        for k in range(3):
            acc = acc + r2_ref[k, 0].astype(F32)
        o_ref[0] = acc

    blk = (1, 1, hr, cols)
    return pl.pallas_call(
        body, name=name, out_shape=jax.ShapeDtypeStruct((nl, 2 * hr, cols), F32),
        grid_spec=pltpu.PrefetchScalarGridSpec(
            num_scalar_prefetch=1, grid=(nl,),
            in_specs=[pl.BlockSpec(blk, lambda l, ix: (ix[0], l, ix[1], 0)), pl.BlockSpec(blk, lambda l, ix: (ix[0], l, 0, 0)),
                      pl.BlockSpec((3, 1, hr, cols), lambda l, ix: (0, l, 0, 0))],
            out_specs=pl.BlockSpec((1, hr, cols), lambda l, ix: (l, ix[1], 0))),
        compiler_params=_cp("parallel"),
    )(idx, g, recv1, recv2)


def rs_share_halves(fulls):
    n = len(fulls)

    def body(*refs):
        h_refs, o_refs, (send_sems, recv_sems) = refs[:n], refs[n:2 * n], refs[2 * n:]
        x, y, c = _place()
        sibling = (x, y, 1 - c)

        def copy(a, h):
            return pltpu.make_async_remote_copy(
                src_ref=_row_half(h_refs[a], (slice(None),), h), dst_ref=_row_half(o_refs[a], (slice(None),), h),
                send_sem=send_sems.at[a], recv_sem=recv_sems.at[a], device_id=sibling, device_id_type=MESH)

        for a in range(n):
            copy(a, c).start()
        for a in range(n):
            copy(a, c).wait_send()
            copy(a, 1 - c).wait_recv()

    return pl.pallas_call(
        body, name="rs_share_halves", out_shape=[jax.ShapeDtypeStruct(f.shape, F32) for f in fulls],
        in_specs=[_ANY] * n, out_specs=[_ANY] * n, input_output_aliases={a: a for a in range(n)},
        scratch_shapes=[pltpu.SemaphoreType.DMA((n,)), pltpu.SemaphoreType.DMA((n,))],
    )(*fulls)


def _own_slot(shard, chip, dtype):
    return lax.dynamic_update_slice(lax.empty((4, *shard.shape), dtype), shard.astype(dtype)[None],
                                    (chip,) + (0,) * shard.ndim)


def kernel(x, norm_mix_pre, norm_mix_post, norm_ffn_pre, norm_ffn_post, w_in, conv_a, gate_up_fwd, gate_bias_fwd, gate_up_bwd, gate_bias_bwd, gla_head_norm, w_out, w_up, conv_ffn, w_down, loss_target, m_norm_mix_pre, m_norm_mix_post, m_norm_ffn_pre, m_norm_ffn_post, m_w_in, m_conv_a, m_gate_up_fwd, m_gate_bias_fwd, m_gate_up_bwd, m_gate_bias_bwd, m_gla_head_norm, m_w_out, m_w_up, m_conv_ffn, m_w_down, v_norm_mix_pre, v_norm_mix_post, v_norm_ffn_pre, v_norm_ffn_post, v_w_in, v_conv_a, v_gate_up_fwd, v_gate_bias_fwd, v_gate_up_bwd, v_gate_bias_bwd, v_gla_head_norm, v_w_out, v_w_up, v_conv_ffn, v_w_down):
    L = x.shape[1]
    xi, yi, ci = _place()
    chip = 2 * xi + yi
    tl_gla, tl_mix, tl_ffn = min(L, TL_GLA), min(L, TL_MIX), min(L, TL_FFN)

    big_w = (w_in, w_out, w_up, w_down)
    started = [gather_start([_own_slot(w[l], chip, BF16) for w in big_w], f"gather_start_{l}") for l in range(1, DEPTH)]
    tokens = [s[3] for s in started]
    gathered = [[a[:, 0] for a in allgather_weights([_own_slot(w[0:1], chip, BF16) for w in big_w])]]

    def layer_weights(bufs):
        a_in, a_out, a_up, a_down = bufs
        w_in_l = jnp.pad(jnp.concatenate([a_in[j] for j in range(4)], axis=1), ((0, 0), (0, DINP - DIN)))
        return w_in_l, a_out.reshape(D, D), a_up, a_down.reshape(DFF, D)

    small = jnp.concatenate([conv_a.reshape(-1), gate_up_fwd.reshape(-1), gate_up_bwd.reshape(-1), conv_ffn.reshape(-1)])
    ms = small.shape[0] // 128
    sg = allgather8(small.reshape(ms, 128), "allgather_small_weights").reshape(4, 2, ms * 128)[:, 0]

    def small_full(off, shape):
        n = shape[0] * shape[1] * shape[2]
        return jnp.concatenate([sg[j, off:off + n].reshape(shape) for j in range(4)], axis=2)

    o1 = DEPTH * 3 * 128
    o2 = o1 + DEPTH * RK * 64
    o3 = o2 + DEPTH * RK * 64
    conv_a_f = small_full(0, (DEPTH, 3, 128))
    gup_f = small_full(o1, (DEPTH, RK, 64))
    gup_b = small_full(o2, (DEPTH, RK, 64))
    conv_ffn_f = small_full(o3, (DEPTH, 3, 1408))

    def gcat_of(l):
        g = jnp.zeros((LRW, 2 * DK), F32)
        g = g.at[0:RK, 0:DK].set(gup_f[l]).at[RK:2 * RK, DK:2 * DK].set(gup_b[l])
        return g.astype(BF16)

    gcats = [gcat_of(l) for l in range(DEPTH)]
    gbiases = [jnp.concatenate([gate_bias_fwd[l], gate_bias_bwd[l]])[None, :] for l in range(DEPTH)]
    ghn4s = [jnp.tile(gla_head_norm[l], NH)[None, :] for l in range(DEPTH)]

    xc = x.reshape(L, D)
    saved = []
    W_in, W_out, W_up, W_down = [], [], [], []
    for l in range(DEPTH):
        if l > 0:
            ssem, rsem, bufs, _ = started[l - 1]
            gathered.append(gather_wait(ssem, rsem, bufs, xc, f"gather_wait_{l}"))
        for lst, w in zip((W_in, W_out, W_up, W_down), layer_weights(gathered[l])):
            lst.append(w)
        P, h1 = rms_matmul(xc, norm_mix_pre[l][None, :], W_in[l], 640, "proj_in", out_dtype=BF16,
                           after=tokens if l == 0 else ())
        o_f, o_b, sf, sb = gla_fwd(P, gcats[l], gbiases[l], tl_gla)
        ycat, y, x1 = mix_out(P, o_f, o_b, conv_a_f[l], ghn4s[l], W_out[l], norm_mix_post[l][None, :], xc, tl_mix)
        U, h2 = rms_matmul(x1, norm_ffn_pre[l][None, :], W_up[l], WFF, "proj_up", n_out=2 * DFF, out_dtype=BF16,
                           w_spec=pl.BlockSpec((None, D, WFF), lambda i, j: (j, 0, 0)))
        y2, x2, ug, uv, z = ffn_down(U, conv_ffn_f[l], W_down[l], norm_ffn_post[l][None, :], x1, tl_ffn)
        saved.append(dict(x=xc, h1=h1, P=P, o_f=o_f, o_b=o_b, sf=sf, sb=sb, ycat=ycat, y=y, x1=x1, h2=h2, U=U, y2=y2,
                          ug=ug, uv=uv, z=z))
        xc = x2

    dx, loss_blk = loss_head(xc, loss_target.reshape(L, D), tl_mix)

    big = ("w_in", "w_out", "w_up", "w_down")
    cidx = jnp.reshape(ci, (1,)).astype(jnp.int32)
    idx = jnp.stack([chip, ci]).astype(jnp.int32)
    grads = [None] * DEPTH
    reduced = [None] * DEPTH
    tl_dw = min(L, 1024)
    pending = None
    token = ()

    def finish(pend, after):
        lp, gs_p, recv1_p, (ssem, rsem, cs_thru, lands, _) = pend
        recv2 = exchange_wait(ssem, rsem, cs_thru, lands, after, f"exchange_wait_{lp}")
        halves = [rs_final_sum(g, r1, r2, idx, "rs_final_sum_" + k) for g, r1, r2, k in zip(gs_p, recv1_p, recv2, big)]
        reduced[lp] = rs_share_halves(halves)

    for l in reversed(range(DEPTH)):
        s = saved[l]
        dy2, dg4 = rms_bwd_pre(dx, s["y2"], norm_ffn_post[l][None, :], tl_mix, after=token)
        du_g, du_v = ffn_bwd1(dy2, s["ug"], s["uv"], W_down[l], min(L, TL_FFN2))
        g_down = matmul_tn(s["z"], dy2, DFF // 2, D, tl_dw, "dw_down").reshape(4, 1, DFF // 4, D)
        dU_g, dU_v, dx1, dg3, dcf_g, dcf_v = ffn_bwd2(du_g, du_v, s["U"], conv_ffn_f[l], W_up[l], s["x1"], dx,
                                                      norm_ffn_pre[l][None, :],
                                        min(L, TL_FFN2))
        g_up = matmul_tn(s["h2"], dU_g, D, WFF, tl_dw, "dw_up_gate",
                         into=(lax.empty((4, 1, D, WFF), F32), (None, None, D, WFF), lambda p, q: (q, 0, 0, 0)))
        g_up = matmul_tn(s["h2"], dU_v, D, WFF, tl_dw, "dw_up_val",
                         into=(g_up, (None, None, D, WFF), lambda p, q: (NFF + q, 0, 0, 0)))
        dy, dg2 = rms_bwd_pre(dx1, s["y"], norm_mix_post[l][None, :], tl_mix)
        dgb, dcc, dgo, do, dca, dghn = mix_bwd1(dy, W_out[l], s["P"], s["o_f"], s["o_b"], conv_a_f[l], ghn4s[l], tl_mix)
        g_out = matmul_tn(s["ycat"], dy, D, D, tl_dw, "dw_out").reshape(4, 1, D // 4, D)
        gl = gla_bwd(s["P"], do, s["sf"], s["sb"], gcats[l], gbiases[l], tl_gla)
        dP, dx, dg1, dgcat, dbias = mix_bwd2(dgb, dcc, dgo, gl, s["P"], conv_a_f[l], gcats[l], gbiases[l], W_in[l],
                                             s["x"], dx1, norm_mix_pre[l][None, :], tl_mix)
        dW_in = matmul_tn(s["h1"], dP, D, 640, tl_dw, "dw_in")
        g_in = jnp.stack([dW_in[:, (DIN // 4) * j:(DIN // 4) * (j + 1)] for j in range(4)])[:, None]
        grads[l] = dict(
            norm_mix_pre=dg1[0], norm_mix_post=dg2[0], norm_ffn_pre=dg3[0], norm_ffn_post=dg4[0],
            conv_a=dca[0:3], gate_up_fwd=dgcat[0:RK, 0:DK], gate_bias_fwd=dbias[0, 0:DK],
            gate_up_bwd=dgcat[RK:2 * RK, DK:2 * DK], gate_bias_bwd=dbias[0, DK:2 * DK], gla_head_norm=dghn[0],
            conv_ffn=jnp.concatenate([dcf_g[j, 0:3] for j in range(NFF)] + [dcf_v[j, 0:3] for j in range(NFF)], axis=1))
        gs = [g_in, g_out, g_up, g_down]
        recv1 = rs_sibling_halves(gs)
        cs16 = [rs_chipsum16(g, r, cidx, "rs_chipsum16_" + k) for g, r, k in zip(gs, recv1, big)]
        flight = exchange_start(cs16, f"exchange_start_{l}")
        token = (flight[4],)
        if pending is not None:
            finish(pending, flight[4])
        pending = (l, gs, recv1, flight)
    finish(pending, pending[3][4])

    G = {k: jnp.stack([grads[l][k] for l in range(DEPTH)]) for k in grads[0]}

    small_names = ["norm_mix_pre", "norm_mix_post", "norm_ffn_pre", "norm_ffn_post", "conv_a", "gate_up_fwd",
                   "gate_bias_fwd", "gate_up_bwd", "gate_bias_bwd", "gla_head_norm", "conv_ffn"]
    flat = jnp.concatenate([G[k].reshape(-1) for k in small_names] + [loss_blk[0, 0:1]])
    n_small = flat.shape[0]
    mp = -(-n_small // 1024) * 8
    flat = jnp.pad(flat, (0, mp * 128 - n_small)).reshape(mp, 128)
    tot = sum8(allgather8(flat, "allgather_small_grads"), mp).reshape(-1)
    gsm = {}
    o = 0
    for k in small_names:
        n = G[k].size
        gsm[k] = tot[o:o + n].reshape(G[k].shape)
        o += n
    loss = tot[o]

    def my_cols(a, width):
        return lax.dynamic_slice_in_dim(a, chip * width, width, axis=2)

    gsm["conv_a"] = my_cols(gsm["conv_a"], 128)
    gsm["gate_up_fwd"] = my_cols(gsm["gate_up_fwd"], 64)
    gsm["gate_up_bwd"] = my_cols(gsm["gate_up_bwd"], 64)
    gsm["conv_ffn"] = my_cols(gsm["conv_ffn"], 1408)

    for a, k in enumerate(big):
        gsm[k] = jnp.concatenate([reduced[l][a] for l in range(DEPTH)], axis=0)

    names = ["norm_mix_pre", "norm_mix_post", "norm_ffn_pre", "norm_ffn_post", "w_in", "conv_a", "gate_up_fwd",
             "gate_bias_fwd", "gate_up_bwd", "gate_bias_bwd", "gla_head_norm", "w_out", "w_up", "conv_ffn", "w_down"]
    w = dict(norm_mix_pre=norm_mix_pre, norm_mix_post=norm_mix_post, norm_ffn_pre=norm_ffn_pre, norm_ffn_post=norm_ffn_post,
             w_in=w_in, conv_a=conv_a, gate_up_fwd=gate_up_fwd, gate_bias_fwd=gate_bias_fwd, gate_up_bwd=gate_up_bwd,
             gate_bias_bwd=gate_bias_bwd, gla_head_norm=gla_head_norm, w_out=w_out, w_up=w_up, conv_ffn=conv_ffn, w_down=w_down)
    m = dict(norm_mix_pre=m_norm_mix_pre, norm_mix_post=m_norm_mix_post, norm_ffn_pre=m_norm_ffn_pre, norm_ffn_post=m_norm_ffn_post,
             w_in=m_w_in, conv_a=m_conv_a, gate_up_fwd=m_gate_up_fwd, gate_bias_fwd=m_gate_bias_fwd, gate_up_bwd=m_gate_up_bwd,
             gate_bias_bwd=m_gate_bias_bwd, gla_head_norm=m_gla_head_norm, w_out=m_w_out, w_up=m_w_up, conv_ffn=m_conv_ffn, w_down=m_w_down)
    v = dict(norm_mix_pre=v_norm_mix_pre, norm_mix_post=v_norm_mix_post, norm_ffn_pre=v_norm_ffn_pre, norm_ffn_post=v_norm_ffn_post,
             w_in=v_w_in, conv_a=v_conv_a, gate_up_fwd=v_gate_up_fwd, gate_bias_fwd=v_gate_bias_fwd, gate_up_bwd=v_gate_up_bwd,
             gate_bias_bwd=v_gate_bias_bwd, gla_head_norm=v_gla_head_norm, w_out=v_w_out, w_up=v_w_up, conv_ffn=v_conv_ffn, w_down=v_w_down)
    upd = {k: adamw(w[k], gsm[k], m[k], v[k], "adamw_" + k) for k in names}
    return (loss, dx.reshape(1, L, D), *[gsm[k] for k in names], *[upd[k][0] for k in names],
            *[upd[k][1] for k in names], *[upd[k][2] for k in names])
```

```python
import functools

import jax
import jax.numpy as jnp
from jax import lax
from jax.experimental import pallas as pl
from jax.experimental.pallas import tpu as pltpu

F32 = jnp.float32
BF16 = jnp.bfloat16
MXU_DTYPE = jnp.bfloat16
MESH = pl.DeviceIdType.MESH

D = 1024
DC = 512
DG = 512
NH = 4
HV = 128
HK = 64
DK = 256
RK = 16
CH = 64
DFF = 2816
DIN = 3104
DINP = 3200
LRW = 128
DEPTH = 4
EPS = 1e-6
QSCALE = HK ** -0.5
GATE_NORM = 1.0 / 16.0
CB_GB, CB_GC, CB_GV, CB_GO = 0, 1, 2, 5
CB_Q, CB_K = 6, 7
CB_V = 4
CB_LR = 24
LR = 0.001
B1 = 0.9
B2 = 0.999
AEPS = 1e-08
WD = 0.01
STEP = 10
TM_PROJ = 1024
TL_GLA = 512
TL_MIX = 256
TL_MIX_OUT = 512
TL_ROW = 1024
TL_FFN = 256
TL_FFN2 = 512
VMEM_LIMIT = 56 * 1024 * 1024


def _cp(*sem):
    return pltpu.CompilerParams(dimension_semantics=sem if sem else None, vmem_limit_bytes=VMEM_LIMIT)


def _mm(a, b):
    return jnp.dot(a.astype(MXU_DTYPE), b.astype(MXU_DTYPE), preferred_element_type=F32)


def _mm_nt(a, b):
    return lax.dot_general(a.astype(MXU_DTYPE), b.astype(MXU_DTYPE), (((1,), (1,)), ((), ())),
                           preferred_element_type=F32)


def _mm_tn(a, b):
    return lax.dot_general(a.astype(MXU_DTYPE), b.astype(MXU_DTYPE), (((0,), (0,)), ((), ())),
                           preferred_element_type=F32)


def _mm_tri(tri, b):
    t = tri.astype(BF16)
    b1 = b.astype(BF16)
    r1 = b - b1.astype(F32)
    b2 = r1.astype(BF16)
    b3 = (r1 - b2.astype(F32)).astype(BF16)
    dot = lambda u: jnp.dot(t, u, preferred_element_type=F32)
    return dot(b1) + dot(b2) + dot(b3)


def _rms(x, g):
    r = lax.rsqrt(jnp.mean(x * x, axis=-1, keepdims=True) + EPS)
    return x * r * g


def _rms_bwd(dout, y, g):
    r = lax.rsqrt(jnp.mean(y * y, axis=-1, keepdims=True) + EPS)
    yh = y * r
    dyh = dout * g
    dy = r * (dyh - yh * jnp.mean(dyh * yh, axis=-1, keepdims=True))
    dg = jnp.sum(dout * yh, axis=0, keepdims=True)
    return dy, dg


def _sigmoid(x):
    return 0.5 * jnp.tanh(0.5 * x) + 0.5


def _logsig(x):
    return jnp.minimum(x, 0.0) - jnp.log1p(jnp.exp(-jnp.abs(x)))


def _shifts(x, p8, n8):
    n = x.shape[0]
    xe = jnp.concatenate([p8, x, n8], axis=0)
    return pltpu.roll(xe, 1, 0)[8:8 + n], pltpu.roll(xe, n + 15, 0)[8:8 + n]


def _halo_rows(prev_ref, next_ref, i, last):
    hr = prev_ref.shape[0]
    p = jnp.where(i == 0, 0.0, prev_ref[...].astype(F32)[hr - 8:hr, :])
    n = jnp.where(i == last, 0.0, next_ref[...].astype(F32)[0:8, :])
    return p, n


def _conv3(x, xp, xn, w_ref):
    xm1, xp1 = _shifts(x, xp, xn)
    return w_ref[0:1, :] * xm1 + w_ref[1:2, :] * x + w_ref[2:3, :] * xp1, xm1, xp1


def _conv3_t(d, dp, dn, w_ref):
    dm1, dp1 = _shifts(d, dp, dn)
    return w_ref[0:1, :] * dp1 + w_ref[1:2, :] * d + w_ref[2:3, :] * dm1


HALO32 = 8
HALO16 = 16


def _prev_row_blk(i, tl, hr):
    return jnp.maximum(i * (tl // hr) - 1, 0)


def _next_row_blk(i, tl, nrows, hr):
    return jnp.minimum((i + 1) * (tl // hr), nrows // hr - 1)


def _prev_blk(tl, cb, hr=HALO32):
    return lambda i: (_prev_row_blk(i, tl, hr), cb)


def _next_blk(tl, nrows, cb, hr=HALO32):
    return lambda i: (_next_row_blk(i, tl, nrows, hr), cb)


def rms_matmul(x, g, w, tn, name, w_spec=None, n_out=None, out_dtype=F32, after=()):
    L = x.shape[0]
    N = w.shape[1] if n_out is None else n_out
    tm = min(L, TM_PROJ)
    if w_spec is None:
        w_spec = pl.BlockSpec((D, tn), lambda i, j: (0, j))

    def body(x_ref, g_ref, w_ref, *rest):
        o_ref, h_ref = rest[-2:]

        @pl.when(pl.program_id(1) == 0)
        def _():
            h_ref[...] = _rms(x_ref[...], g_ref[...]).astype(BF16)

        o_ref[...] = _mm(h_ref[...], w_ref[...]).astype(out_dtype)

    return pl.pallas_call(
        body, name=name, grid=(L // tm, N // tn),
        in_specs=[pl.BlockSpec((tm, D), lambda i, j: (i, 0)), pl.BlockSpec((1, D), lambda i, j: (0, 0)), w_spec]
        + [_ANY] * len(after),
        out_specs=[pl.BlockSpec((tm, tn), lambda i, j: (i, j)), pl.BlockSpec((tm, D), lambda i, j: (i, 0))],
        out_shape=[jax.ShapeDtypeStruct((L, N), out_dtype), jax.ShapeDtypeStruct((L, D), BF16)],
        compiler_params=_cp("parallel", "arbitrary"),
    )(x, g, w, *after)


def _gla_masks():
    def blk(shape, rdiv, cdiv):
        r = lax.broadcasted_iota(jnp.int32, shape, 0) // rdiv
        c = lax.broadcasted_iota(jnp.int32, shape, 1) // cdiv
        return (r == c).astype(F32)

    r = lax.broadcasted_iota(jnp.int32, (CH, CH), 0)
    c = lax.broadcasted_iota(jnp.int32, (CH, CH), 1)
    r4 = lax.broadcasted_iota(jnp.int32, (NH * CH, CH), 0) % CH
    c4 = lax.broadcasted_iota(jnp.int32, (NH * CH, CH), 1)
    return dict(
        bdq=blk((NH * CH, DK), CH, HK),
        bdo=blk((NH * CH, DG), CH, HV),
        bds=blk((DG, DK), HV, HK),
        tril=(r >= c).astype(F32), triu=(r <= c).astype(F32),
        tril4=r4 >= c4, triu4=r4 <= c4,
    )


def _tile4(x):
    return jnp.concatenate([x, x, x, x], axis=0)


def _gla_tile_prep(q, k, a, m, rev, nc):
    tri = m["triu"] if rev else m["tril"]
    chunks = [a[c * CH:(c + 1) * CH] for c in range(nc)]
    cum = jnp.concatenate([_mm_tri(tri, ac) for ac in chunks], axis=0)
    tot = jnp.concatenate([jnp.sum(ac, axis=0, keepdims=True) for ac in chunks], axis=0)
    tot_rows = jnp.concatenate([jnp.broadcast_to(tot[c:c + 1], (CH, DK)) for c in range(nc)], axis=0)
    e = jnp.exp(cum)
    einv = jnp.exp(-cum)
    eout = jnp.exp(tot_rows - cum)
    q, k = q.astype(F32), k.astype(F32)
    return dict(e=e, einv=einv, eout=eout, dec=jnp.exp(tot), qt=q * QSCALE * e, kt=k * einv, kh=k * eout)


def _gla_scores(qt16, kt16, m, rev):
    qs = _tile4(qt16) * m["bdq"].astype(qt16.dtype)
    return qs, jnp.where(m["triu4"] if rev else m["tril4"], _mm_nt(qs, kt16), 0.0)


def _gla_chunk_fwd(qt16, kt16, kh16, v, dec, st_ref, m, rev):
    _, sc = _gla_scores(qt16, kt16, m, rev)
    v16 = v.astype(BF16)
    r = _mm(sc, v16)
    o_intra = jnp.concatenate([r[h * CH:(h + 1) * CH, h * HV:(h + 1) * HV] for h in range(NH)], axis=1)
    st = st_ref[...]
    st16 = st.astype(BF16)
    o = o_intra + _mm_nt(qt16, st16)
    st_ref[...] = st * dec + _mm_tn(v16, kh16) * m["bds"]
    return o, st16


def _gates(lr_ref, gc_ref, bs_ref, cols):
    return _logsig(_mm(lr_ref[...], gc_ref[:, cols]) + bs_ref[:, cols]) * GATE_NORM


def gla_fwd(P, gcat, gbias, tl):
    L = P.shape[0]
    nb = L // tl
    nc = tl // CH

    def body(qf, kf, vf, lf, qb, kb, vb, lb, gc_ref, bs_ref, of, ob, sf, sb, stf, stb,
             qtf, ktf, khf, dcf, qtb, ktb, khb, dcb):
        @pl.when(pl.program_id(0) == 0)
        def _():
            stf[...] = jnp.zeros_like(stf)
            stb[...] = jnp.zeros_like(stb)

        m = _gla_masks()
        for (q, k, lr, cols, rev, qt, kt, kh, dc) in ((qf, kf, lf, slice(0, DK), False, qtf, ktf, khf, dcf),
                                                      (qb, kb, lb, slice(DK, 2 * DK), True, qtb, ktb, khb, dcb)):
            p = _gla_tile_prep(q[...], k[...], _gates(lr, gc_ref, bs_ref, cols), m, rev, nc)
            qt[...] = p["qt"].astype(BF16)
            kt[...] = p["kt"].astype(BF16)
            kh[...] = p["kh"].astype(BF16)
            dc[...] = p["dec"]

        def chunk(c, carry):
            rows = pl.ds(pl.multiple_of(c * CH, CH), CH)
            o, st = _gla_chunk_fwd(qtf[rows, :], ktf[rows, :], khf[rows, :], vf[rows, :], dcf[pl.ds(c, 1), :], stf, m, False)
            of[rows, :] = o.astype(BF16)
            sf[c] = st
            cb = nc - 1 - c
            rows = pl.ds(pl.multiple_of(cb * CH, CH), CH)
            o, st = _gla_chunk_fwd(qtb[rows, :], ktb[rows, :], khb[rows, :], vb[rows, :], dcb[pl.ds(cb, 1), :], stb, m, True)
            ob[rows, :] = o.astype(BF16)
            sb[cb] = st
            return carry

        lax.fori_loop(0, nc, chunk, 0, unroll=2)

    fw = lambda cb: (lambda i: (i, cb))
    bw = lambda cb: (lambda i: (nb - 1 - i, cb))
    return pl.pallas_call(
        body, name="gla_fwd", grid=(nb,),
        in_specs=[pl.BlockSpec((tl, DK), fw(CB_Q)), pl.BlockSpec((tl, DK), fw(CB_K)), pl.BlockSpec((tl, DG), fw(CB_V)),
                  pl.BlockSpec((tl, LRW), fw(CB_LR)),
                  pl.BlockSpec((tl, DK), bw(CB_Q)), pl.BlockSpec((tl, DK), bw(CB_K)), pl.BlockSpec((tl, DG), bw(CB_V)),
                  pl.BlockSpec((tl, LRW), bw(CB_LR)),
                  pl.BlockSpec((LRW, 2 * DK), lambda i: (0, 0)), pl.BlockSpec((1, 2 * DK), lambda i: (0, 0))],
        out_specs=[pl.BlockSpec((tl, DG), lambda i: (i, 0)), pl.BlockSpec((tl, DG), lambda i: (nb - 1 - i, 0)),
                   pl.BlockSpec((nc, DG, DK), lambda i: (i, 0, 0)), pl.BlockSpec((nc, DG, DK), lambda i: (nb - 1 - i, 0, 0))],
        out_shape=[jax.ShapeDtypeStruct((L, DG), BF16), jax.ShapeDtypeStruct((L, DG), BF16),
                   jax.ShapeDtypeStruct((L // CH, DG, DK), BF16), jax.ShapeDtypeStruct((L // CH, DG, DK), BF16)],
        scratch_shapes=[pltpu.VMEM((DG, DK), F32), pltpu.VMEM((DG, DK), F32)]
        + [pltpu.VMEM((tl, DK), BF16)] * 3 + [pltpu.VMEM((nc, DK), F32)]
        + [pltpu.VMEM((tl, DK), BF16)] * 3 + [pltpu.VMEM((nc, DK), F32)],
        compiler_params=_cp("arbitrary"),
    )(P, P, P, P, P, P, P, P, gcat, gbias)


def _headnorm(o):
    oh, rs = [], []
    for h in range(NH):
        oo = o[:, h * HV:(h + 1) * HV]
        r = lax.rsqrt(jnp.mean(oo * oo, axis=-1, keepdims=True) + EPS)
        oh.append(oo * r)
        rs.append(r)
    return jnp.concatenate(oh, axis=1), rs


def mix_out(P, o_f, o_b, conv_a, ghn4, w_out, g2, x, tl):
    L = P.shape[0]
    nt = L // tl

    def body(gb, gc, gv, go, gcp, gvp, gcn, gvn, of, ob, ca, gh, wo, g2r, xr, ycat, yr, x1):
        i = pl.program_id(0)
        cp, cn = _halo_rows(gcp, gcn, i, nt - 1)
        vp, vn = _halo_rows(gvp, gvn, i, nt - 1)
        c = gc[...].astype(F32) * gv[...].astype(F32)
        cc, _, _ = _conv3(c, cp * vp, cn * vn, ca)
        ya = gb[...].astype(F32) * cc
        oh, _ = _headnorm(of[...].astype(F32) + ob[...].astype(F32))
        g = go[...].astype(F32)
        yb = g * _sigmoid(g) * (oh * gh[...])
        yc = jnp.concatenate([ya, yb], axis=1).astype(BF16)
        ycat[...] = yc
        y = _mm(yc, wo[...])
        yr[...] = y
        x1[...] = xr[...] + _rms(y, g2r[...])

    t = lambda cb: pl.BlockSpec((tl, DC), lambda i: (i, cb))
    hp = lambda cb: pl.BlockSpec((HALO16, DC), _prev_blk(tl, cb, HALO16))
    hn = lambda cb: pl.BlockSpec((HALO16, DC), _next_blk(tl, L, cb, HALO16))
    row = lambda n: pl.BlockSpec((tl, n), lambda i: (i, 0))
    full = lambda a: pl.BlockSpec(a.shape, lambda i: (0, 0))
    return pl.pallas_call(
        body, name="mix_out", grid=(nt,),
        in_specs=[t(CB_GB), t(CB_GC), t(CB_GV), t(CB_GO), hp(CB_GC), hp(CB_GV), hn(CB_GC), hn(CB_GV),
                  row(DG), row(DG), full(conv_a), full(ghn4), full(w_out), full(g2), row(D)],
        out_specs=[row(D), row(D), row(D)],
        out_shape=[jax.ShapeDtypeStruct((L, D), BF16), jax.ShapeDtypeStruct((L, D), F32),
                   jax.ShapeDtypeStruct((L, D), F32)],
        compiler_params=_cp("parallel"),
    )(P, P, P, P, P, P, P, P, o_f, o_b, conv_a, ghn4, w_out, g2, x)


NFF = 2
WFF = DFF // NFF
FFN_COL_CHUNKS = ((0, 512), (512, 1024), (1024, WFF))


def ffn_down(U, conv_ffn, w_down, g4, x1, tl):
    L = U.shape[0]
    nt = L // tl

    def body(u, up, un, cf, wd, g4r, x1r, y2, x2, ug, uv, zr):
        i = pl.program_id(0)
        acc = jnp.zeros((tl, D), F32)
        for j in range(NFF):
            gs = slice(j * WFF, (j + 1) * WFF)
            vs = slice(DFF + j * WFF, DFF + (j + 1) * WFF)
            z = []
            for s in (gs, vs):
                p, n = _halo_rows(up.at[:, s], un.at[:, s], i, nt - 1)
                z.append(_conv3(u[:, s].astype(F32), p, n, cf.at[:, s])[0])
            zz = (z[0] * _sigmoid(z[0]) * z[1]).astype(BF16)
            ug[:, gs] = z[0].astype(BF16)
            uv[:, gs] = z[1].astype(BF16)
            zr[:, gs] = zz
            acc = acc + _mm(zz, wd[gs, :])
        y2[...] = acc
        x2[...] = x1r[...] + _rms(acc, g4r[...])

    row = lambda n: pl.BlockSpec((tl, n), lambda i: (i, 0))
    full = lambda a: pl.BlockSpec(a.shape, lambda i: (0, 0))
    half = jax.ShapeDtypeStruct((L, DFF), BF16)
    return pl.pallas_call(
        body, name="ffn_down", grid=(nt,),
        in_specs=[row(2 * DFF), pl.BlockSpec((HALO16, 2 * DFF), _prev_blk(tl, 0, HALO16)),
                  pl.BlockSpec((HALO16, 2 * DFF), _next_blk(tl, L, 0, HALO16)),
                  full(conv_ffn), full(w_down), full(g4), row(D)],
        out_specs=[row(D), row(D), row(DFF), row(DFF), row(DFF)],
        out_shape=[jax.ShapeDtypeStruct((L, D), F32), jax.ShapeDtypeStruct((L, D), F32), half, half, half],
        compiler_params=_cp("parallel"),
    )(U, U, U, conv_ffn, w_down, g4, x1)


def loss_head(y, target, tl):
    L = y.shape[0]

    def body(yr, tr, dy, ls):
        @pl.when(pl.program_id(0) == 0)
        def _():
            ls[...] = jnp.zeros_like(ls)

        err = yr[...] - tr[...]
        dy[...] = err * (1.0 / D)
        ls[...] += (0.5 / D) * jnp.sum(err * err)

    row = pl.BlockSpec((tl, D), lambda i: (i, 0))
    return pl.pallas_call(
        body, name="loss_head", grid=(L // tl,), in_specs=[row, row],
        out_specs=[row, pl.BlockSpec((8, 128), lambda i: (0, 0))],
        out_shape=[jax.ShapeDtypeStruct((L, D), F32), jax.ShapeDtypeStruct((8, 128), F32)],
        compiler_params=_cp("arbitrary"),
    )(y, target)


def rms_bwd_pre(dout, y, g, tl, after=()):
    L = y.shape[0]

    def body(dr, yr, gr, *rest):
        dy, dg = rest[-2:]

        @pl.when(pl.program_id(0) == 0)
        def _():
            dg[...] = jnp.zeros_like(dg)

        a, b = _rms_bwd(dr[...], yr[...], gr[...])
        dy[...] = a.astype(BF16)
        dg[...] += b

    row = pl.BlockSpec((tl, D), lambda i: (i, 0))
    vec = pl.BlockSpec((1, D), lambda i: (0, 0))
    return pl.pallas_call(
        body, name="rms_bwd_pre", grid=(L // tl,), in_specs=[row, row, vec] + [_ANY] * len(after), out_specs=[row, vec],
        out_shape=[jax.ShapeDtypeStruct((L, D), BF16), jax.ShapeDtypeStruct((1, D), F32)],
        compiler_params=_cp("arbitrary"),
    )(dout, y, g, *after)


def ffn_bwd1(dy2, ug, uv, w_down, tl):
    L = ug.shape[0]

    def body(dy, ugr, uvr, wd, dug, duv):
        a = ugr[...].astype(F32)
        b = uvr[...].astype(F32)
        sg = _sigmoid(a)
        silu = a * sg
        dz = _mm_nt(dy[...], wd[...])
        dug[...] = (dz * b * (sg + silu * (1.0 - sg))).astype(BF16)
        duv[...] = (dz * silu).astype(BF16)

    tile = pl.BlockSpec((tl, WFF), lambda j, i: (i, j))
    half = jax.ShapeDtypeStruct((L, DFF), BF16)
    return pl.pallas_call(
        body, name="ffn_bwd1", grid=(NFF, L // tl),
        in_specs=[pl.BlockSpec((tl, D), lambda j, i: (i, 0)), tile, tile, pl.BlockSpec((WFF, D), lambda j, i: (j, 0))],
        out_specs=[tile, tile], out_shape=[half, half],
        compiler_params=_cp("parallel", "parallel"),
    )(dy2, ug, uv, w_down)


def ffn_bwd2(du_g, du_v, U, conv_ffn, w_up, x1, dres, g3, tl):
    L = x1.shape[0]
    nt = L // tl

    def body(dg_, dv_, dgp, dgn, dvp, dvn, ugr, uvr, cg, cv, wg, wv, x1r, drr, g3r, dUg, dUv, dx1, dg3, dcg, dcv, acc):
        i = pl.program_id(0)
        j = pl.program_id(1)

        @pl.when((i == 0) & (j == 0))
        def _():
            dg3[...] = jnp.zeros_like(dg3)
            dcg[...] = jnp.zeros_like(dcg)
            dcv[...] = jnp.zeros_like(dcv)

        part = None
        for c0, c1 in FFN_COL_CHUNKS:
            cs = slice(c0, c1)
            for d_ref, dp_ref, dn_ref, cw_ref, u_ref, dc, dU, w in ((dg_, dgp, dgn, cg, ugr, dcg, dUg, wg),
                                                                    (dv_, dvp, dvn, cv, uvr, dcv, dUv, wv)):
                p8, n8 = _halo_rows(dp_ref.at[:, cs], dn_ref.at[:, cs], i, nt - 1)
                d = d_ref[:, cs].astype(F32)
                dm1, dp1 = _shifts(d, p8, n8)
                du = (cw_ref[0:1, cs] * dp1 + cw_ref[1:2, cs] * d + cw_ref[2:3, cs] * dm1).astype(BF16)
                dU[:, cs] = du
                u = u_ref[:, cs].astype(F32)
                for k, t in enumerate((dp1, d, dm1)):
                    dc[j, k:k + 1, cs] += jnp.sum(t * u, axis=0, keepdims=True)
                term = _mm_nt(du, w[:, cs])
                part = term if part is None else part + term

        @pl.when(j == 0)
        def _():
            acc[...] = part

        @pl.when(j > 0)
        def _():
            acc[...] += part

        @pl.when(j == NFF - 1)
        def _():
            dx, dg = _rms_bwd(acc[...], x1r[...], g3r[...])
            dx1[...] = drr[...] + dx
            dg3[...] += dg

    tile = pl.BlockSpec((tl, WFF), lambda i, j: (i, j))
    prev = pl.BlockSpec((HALO16, WFF), lambda i, j: (_prev_row_blk(i, tl, HALO16), j))
    nxt = pl.BlockSpec((HALO16, WFF), lambda i, j: (_next_row_blk(i, tl, L, HALO16), j))
    cw = lambda off: pl.BlockSpec((3, WFF), lambda i, j: (0, off + j))
    ww = lambda off: pl.BlockSpec((None, D, WFF), lambda i, j: (off + j, 0, 0))
    row = pl.BlockSpec((tl, D), lambda i, j: (i, 0))
    vec = pl.BlockSpec((1, D), lambda i, j: (0, 0))
    ut = lambda off: pl.BlockSpec((tl, WFF), lambda i, j: (i, off + j))
    dcs = pl.BlockSpec((NFF, 8, WFF), lambda i, j: (0, 0, 0))
    return pl.pallas_call(
        body, name="ffn_bwd2", grid=(nt, NFF),
        in_specs=[tile, tile, prev, nxt, prev, nxt, ut(0), ut(NFF), cw(0), cw(NFF), ww(0), ww(NFF), row, row, vec],
        out_specs=[tile, tile, row, vec, dcs, dcs],
        out_shape=[jax.ShapeDtypeStruct((L, DFF), BF16), jax.ShapeDtypeStruct((L, DFF), BF16),
                   jax.ShapeDtypeStruct((L, D), F32), jax.ShapeDtypeStruct((1, D), F32),
                   jax.ShapeDtypeStruct((NFF, 8, WFF), F32), jax.ShapeDtypeStruct((NFF, 8, WFF), F32)],
        scratch_shapes=[pltpu.VMEM((tl, D), F32)],
        compiler_params=_cp("arbitrary", "arbitrary"),
    )(du_g, du_v, du_g, du_g, du_v, du_v, U, U, conv_ffn, conv_ffn, w_up, w_up, x1, dres, g3)


def matmul_tn(a, b, ta, tn, tl, name, into=None):
    L, Ka = a.shape
    N = b.shape[1]

    def body(ar, br, *rest):
        o = rest[-1]

        @pl.when(pl.program_id(2) == 0)
        def _():
            o[...] = jnp.zeros_like(o)

        o[...] += _mm_tn(ar[...], br[...]).reshape(o.shape)

    in_specs = [pl.BlockSpec((tl, ta), lambda p, q, l: (l, p)), pl.BlockSpec((tl, tn), lambda p, q, l: (l, q))]
    if into is None:
        return pl.pallas_call(
            body, name=name, grid=(Ka // ta, N // tn, L // tl), in_specs=in_specs,
            out_specs=pl.BlockSpec((ta, tn), lambda p, q, l: (p, q)),
            out_shape=jax.ShapeDtypeStruct((Ka, N), F32),
            compiler_params=_cp("parallel", "parallel", "arbitrary"),
        )(a, b)
    buf, blk, idx = into
    return pl.pallas_call(
        body, name=name, grid=(Ka // ta, N // tn, L // tl), in_specs=in_specs + [_ANY],
        out_specs=pl.BlockSpec(blk, lambda p, q, l: idx(p, q)),
        out_shape=jax.ShapeDtypeStruct(buf.shape, F32), input_output_aliases={2: 0},
        compiler_params=_cp("parallel", "parallel", "arbitrary"),
    )(a, b, buf)


def mix_bwd1(dy, w_out, P, o_f, o_b, conv_a, ghn4, tl):
    L = P.shape[0]
    nt = L // tl

    def body(dyr, wo, gb, gc, gv, go, gcp, gvp, gcn, gvn, of, ob, ca, gh, dgb, dcc, dgo, do, dca, dgh):
        i = pl.program_id(0)

        @pl.when(i == 0)
        def _():
            dca[...] = jnp.zeros_like(dca)
            dgh[...] = jnp.zeros_like(dgh)

        dycat = _mm_nt(dyr[...], wo[...])
        dya = dycat[:, 0:DC]
        dyb = dycat[:, DC:D]
        cp, cn = _halo_rows(gcp, gcn, i, nt - 1)
        vp, vn = _halo_rows(gvp, gvn, i, nt - 1)
        c = gc[...].astype(F32) * gv[...].astype(F32)
        cc, c_m1, c_p1 = _conv3(c, cp * vp, cn * vn, ca)
        dgb[...] = (dya * cc).astype(BF16)
        d = dya * gb[...].astype(F32)
        dcc[...] = d.astype(BF16)
        for k, s in enumerate((c_m1, c, c_p1)):
            dca[k:k + 1, :] += jnp.sum(d * s, axis=0, keepdims=True)
        oh, rs = _headnorm(of[...].astype(F32) + ob[...].astype(F32))
        g = go[...].astype(F32)
        sg = _sigmoid(g)
        silu = g * sg
        dgo[...] = (dyb * (oh * gh[...]) * (sg * (1.0 + g * (1.0 - sg)))).astype(BF16)
        don = dyb * silu
        t = jnp.sum(don * oh, axis=0, keepdims=True)
        dgh[0:1, :] += t[:, 0:HV] + t[:, HV:2 * HV] + t[:, 2 * HV:3 * HV] + t[:, 3 * HV:4 * HV]
        doh = don * gh[...]
        parts = []
        for h in range(NH):
            hs = slice(h * HV, (h + 1) * HV)
            parts.append(rs[h] * (doh[:, hs] - oh[:, hs] * jnp.mean(doh[:, hs] * oh[:, hs], axis=-1, keepdims=True)))
        do[...] = jnp.concatenate(parts, axis=1).astype(BF16)

    t = lambda cb: pl.BlockSpec((tl, DC), lambda i: (i, cb))
    hp = lambda cb: pl.BlockSpec((HALO16, DC), _prev_blk(tl, cb, HALO16))
    hn = lambda cb: pl.BlockSpec((HALO16, DC), _next_blk(tl, L, cb, HALO16))
    row = lambda n: pl.BlockSpec((tl, n), lambda i: (i, 0))
    full = lambda a: pl.BlockSpec(a.shape, lambda i: (0, 0))
    act16 = lambda n: jax.ShapeDtypeStruct((L, n), BF16)
    return pl.pallas_call(
        body, name="mix_bwd1", grid=(nt,),
        in_specs=[row(D), full(w_out), t(CB_GB), t(CB_GC), t(CB_GV), t(CB_GO), hp(CB_GC), hp(CB_GV), hn(CB_GC), hn(CB_GV),
                  row(DG), row(DG), full(conv_a), full(ghn4)],
        out_specs=[row(DC), row(DC), row(DG), row(DG), pl.BlockSpec((8, DC), lambda i: (0, 0)),
                   pl.BlockSpec((8, HV), lambda i: (0, 0))],
        out_shape=[act16(DC), act16(DC), act16(DG), act16(DG), jax.ShapeDtypeStruct((8, DC), F32),
                   jax.ShapeDtypeStruct((8, HV), F32)],
        compiler_params=_cp("arbitrary"),
    )(dy, w_out, P, P, P, P, P, P, P, P, o_f, o_b, conv_a, ghn4)


def _gla_chunk_bwd(qt, kt, kh, v, do, st16, dec, g_ref, m, rev):
    qt16, kt16, kh16, v16, do16 = (t.astype(BF16) for t in (qt, kt, kh, v, do))
    qs, sc = _gla_scores(qt16, kt16, m, rev)
    g = g_ref[...]
    g16 = g.astype(BF16)
    dob = _tile4(do16) * m["bdo"].astype(BF16)
    dv = _mm_tn(sc, dob) + _mm_nt(kh16, g16)
    dsc = jnp.where(m["triu4"] if rev else m["tril4"], _mm_nt(dob, v16), 0.0)
    r1 = _mm(dsc, kt16) * m["bdq"]
    dqt = r1[0:CH] + r1[CH:2 * CH] + r1[2 * CH:3 * CH] + r1[3 * CH:4 * CH] + _mm(do16, st16)
    dkt = _mm_tn(dsc, qs)
    dkh = _mm(v16, g16)
    dd = jnp.sum(g * st16.astype(F32), axis=0, keepdims=True)
    g_ref[...] = g * dec + _mm_tn(do16, qt16) * m["bds"]
    return dv, dqt, dkt, dkh, dd


def gla_bwd(P, do, sf, sb, gcat, gbias, tl):
    L = P.shape[0]
    nb = L // tl
    nc = tl // CH

    def body(qf, kf, vf, lf, dof, sfr, qb, kb, vb, lb, dob, sbr, gc_ref, bs_ref,
             dqf, dkf, dvf, daf, dqb, dkb, dvb, dab, gf, gbk, *scr):
        @pl.when(pl.program_id(0) == 0)
        def _():
            gf[...] = jnp.zeros_like(gf)
            gbk[...] = jnp.zeros_like(gbk)

        m = _gla_masks()
        keys = ("qt", "kt", "kh", "e", "einv", "eout", "dec")
        pf = dict(zip(keys + ("dd",), scr[0:8]))
        pb = dict(zip(keys + ("dd",), scr[8:16]))
        for (q, k, lr, cols, rev, pr) in ((qf, kf, lf, slice(0, DK), False, pf), (qb, kb, lb, slice(DK, 2 * DK), True, pb)):
            p = _gla_tile_prep(q[...], k[...], _gates(lr, gc_ref, bs_ref, cols), m, rev, nc)
            for key in keys:
                pr[key][...] = p[key]

        def step(c, v, dor, st, g_ref, pr, dq, dk, dv, da, rev):
            rows = pl.ds(pl.multiple_of(c * CH, CH), CH)
            dvc, dqt, dkt, dkh, dd = _gla_chunk_bwd(pr["qt"][rows, :], pr["kt"][rows, :], pr["kh"][rows, :], v[rows, :],
                                                    dor[rows, :], st[c], pr["dec"][pl.ds(c, 1), :], g_ref, m, rev)
            dv[rows, :] = dvc
            dq[rows, :] = dqt
            dk[rows, :] = dkt
            da[rows, :] = dkh
            pr["dd"][pl.ds(c, 1), :] = dd

        def chunk(c, carry):
            step(nc - 1 - c, vf, dof, sfr, gf, pf, dqf, dkf, dvf, daf, False)
            step(c, vb, dob, sbr, gbk, pb, dqb, dkb, dvb, dab, True)
            return carry

        lax.fori_loop(0, nc, chunk, 0, unroll=2)

        def finish(pr, dq, dk, da, rev):
            dqt, dkt, dkh = dq[...], dk[...], da[...]
            kk = dkh * pr["kh"][...]
            dcum = dqt * pr["qt"][...] - dkt * pr["kt"][...] - kk
            dtot = pr["dd"][...] * pr["dec"][...]
            tri_t = m["tril"] if rev else m["triu"]
            parts = []
            for c in range(nc):
                rs = slice(c * CH, (c + 1) * CH)
                parts.append(_mm_tri(tri_t, dcum[rs]) + (jnp.sum(kk[rs], axis=0, keepdims=True) + dtot[c:c + 1]))
            da[...] = jnp.concatenate(parts, axis=0)
            dq[...] = dqt * pr["e"][...] * QSCALE
            dk[...] = dkt * pr["einv"][...] + dkh * pr["eout"][...]

        finish(pf, dqf, dkf, daf, False)
        finish(pb, dqb, dkb, dab, True)

    fwd_dir = lambda cb: (lambda i: (nb - 1 - i, cb))
    bwd_dir = lambda cb: (lambda i: (i, cb))

    def side(ix):
        return [pl.BlockSpec((tl, DK), ix(CB_Q)), pl.BlockSpec((tl, DK), ix(CB_K)), pl.BlockSpec((tl, DG), ix(CB_V)),
                pl.BlockSpec((tl, LRW), ix(CB_LR)), pl.BlockSpec((tl, DG), ix(0)),
                pl.BlockSpec((nc, DG, DK), lambda i: (ix(0)(i)[0], 0, 0))]

    def outs(ix):
        return [pl.BlockSpec((tl, DK), ix(0)), pl.BlockSpec((tl, DK), ix(0)), pl.BlockSpec((tl, DG), ix(0)),
                pl.BlockSpec((tl, DK), ix(0))]

    o_shape = [jax.ShapeDtypeStruct((L, DK), F32), jax.ShapeDtypeStruct((L, DK), F32),
               jax.ShapeDtypeStruct((L, DG), F32), jax.ShapeDtypeStruct((L, DK), F32)]
    return pl.pallas_call(
        body, name="gla_bwd", grid=(nb,),
        in_specs=side(fwd_dir) + side(bwd_dir) + [pl.BlockSpec((LRW, 2 * DK), lambda i: (0, 0)),
                                                  pl.BlockSpec((1, 2 * DK), lambda i: (0, 0))],
        out_specs=outs(fwd_dir) + outs(bwd_dir),
        out_shape=o_shape + o_shape,
        scratch_shapes=[pltpu.VMEM((DG, DK), F32), pltpu.VMEM((DG, DK), F32)]
        + ([pltpu.VMEM((tl, DK), F32)] * 6 + [pltpu.VMEM((nc, DK), F32)] * 2) * 2,
        compiler_params=_cp("arbitrary"),
    )(P, P, P, P, do, sf, P, P, P, P, do, sb, gcat, gbias)


def mix_bwd2(dgb, dcc, dgo, gl, P, conv_a, gcat, gbias, w_in, x, dres, g1, tl):
    L = P.shape[0]
    nt = L // tl

    def body(dgbr, dccr, dccp, dccn, dgor, dqf, dkf, dvf, daf, dqb, dkb, dvb, dab, gc, gv, lr, ca, gcr, bsr, wi,
             xr, drr, g1r, dP, dx, dg1, dgcat, dbias):
        i = pl.program_id(0)

        @pl.when(i == 0)
        def _():
            dg1[...] = jnp.zeros_like(dg1)
            dgcat[...] = jnp.zeros_like(dgcat)
            dbias[...] = jnp.zeros_like(dbias)

        p, n = _halo_rows(dccp, dccn, i, nt - 1)
        dc = _conv3_t(dccr[...].astype(F32), p, n, ca)
        pre = _mm(lr[...], gcr[...]) + bsr[...]
        da = jnp.concatenate([daf[...], dab[...]], axis=1)
        dpre = da * GATE_NORM * (1.0 - _sigmoid(pre))
        dpre16 = dpre.astype(BF16)
        dP[:, 0:DC] = dgbr[...].astype(BF16)
        dP[:, DC:2 * DC] = (dc * gv[...].astype(F32)).astype(BF16)
        dP[:, 2 * DC:3 * DC] = (dc * gc[...].astype(F32)).astype(BF16)
        dP[:, 1536:1792] = (dqf[...] + dqb[...]).astype(BF16)
        dP[:, 1792:2048] = (dkf[...] + dkb[...]).astype(BF16)
        dP[:, 2048:2560] = (dvf[...] + dvb[...]).astype(BF16)
        dP[:, 2560:3072] = dgor[...].astype(BF16)
        dP[:, 3072:3200] = _mm_nt(dpre16, gcr[...]).astype(BF16)
        dgcat[...] += _mm_tn(lr[...], dpre16)
        dbias[0:1, :] += jnp.sum(dpre, axis=0, keepdims=True)
        dh, dg = _rms_bwd(_mm_nt(dP[...], wi[...]), xr[...], g1r[...])
        dx[...] = drr[...] + dh
        dg1[...] += dg

    row = lambda n: pl.BlockSpec((tl, n), lambda i: (i, 0))
    t = lambda w, cb: pl.BlockSpec((tl, w), lambda i: (i, cb))
    full = lambda a: pl.BlockSpec(a.shape, lambda i: (0, 0))
    return pl.pallas_call(
        body, name="mix_bwd2", grid=(nt,),
        in_specs=[row(DC), row(DC), pl.BlockSpec((HALO16, DC), _prev_blk(tl, 0, HALO16)),
                  pl.BlockSpec((HALO16, DC), _next_blk(tl, L, 0, HALO16)),
                  row(DG), row(DK), row(DK), row(DG), row(DK), row(DK), row(DK), row(DG), row(DK),
                  t(DC, CB_GC), t(DC, CB_GV), t(LRW, CB_LR), full(conv_a), full(gcat), full(gbias), full(w_in),
                  row(D), row(D), full(g1)],
        out_specs=[row(DINP), row(D), pl.BlockSpec((1, D), lambda i: (0, 0)), pl.BlockSpec((LRW, 2 * DK), lambda i: (0, 0)),
                   pl.BlockSpec((8, 2 * DK), lambda i: (0, 0))],
        out_shape=[jax.ShapeDtypeStruct((L, DINP), BF16), jax.ShapeDtypeStruct((L, D), F32),
                   jax.ShapeDtypeStruct((1, D), F32), jax.ShapeDtypeStruct((LRW, 2 * DK), F32),
                   jax.ShapeDtypeStruct((8, 2 * DK), F32)],
        compiler_params=_cp("arbitrary"),
    )(dgb, dcc, dcc, dcc, dgo, *gl, P, P, P, conv_a, gcat, gbias, w_in, x, dres, g1)


def _row_tile(rows, cols):
    if rows * cols * 4 <= 2 * 1024 * 1024:
        return rows
    best = 8
    for t in range(8, rows, 8):
        if rows % t == 0 and t * cols * 4 <= 2 * 1024 * 1024:
            best = t
    return best


def adamw(w, g, m, v, name):
    shape = w.shape
    cols = shape[-1]
    w2, g2, m2, v2 = (a.reshape(-1, cols) for a in (w, g, m, v))
    rows = w2.shape[0]
    tr = _row_tile(rows, cols)

    def body(wr, gr, mr, vr, dl, nm, nv):
        gg = gr[...]
        mm = B1 * mr[...] + (1.0 - B1) * gg
        vv = B2 * vr[...] + (1.0 - B2) * (gg * gg)
        m_hat = mm / (1.0 - B1 ** STEP)
        v_hat = vv / (1.0 - B2 ** STEP)
        dl[...] = -LR * (m_hat / (jnp.sqrt(v_hat) + AEPS) + WD * wr[...])
        nm[...] = mm
        nv[...] = vv

    blk = pl.BlockSpec((tr, cols), lambda i: (i, 0))
    o = jax.ShapeDtypeStruct((rows, cols), F32)
    d, nm, nv = pl.pallas_call(
        body, name=name, grid=(rows // tr,), in_specs=[blk] * 4, out_specs=[blk] * 3, out_shape=[o, o, o],
        compiler_params=_cp("parallel"),
    )(w2, g2, m2, v2)
    return d.reshape(shape), nm.reshape(shape), nv.reshape(shape)


def _place():
    return lax.axis_index("x"), lax.axis_index("y"), lax.axis_index("c")


def allgather8(v, name):
    mp, n = v.shape

    def body(x_ref, out_ref, send_sems, recv_sems, local_sem):
        x, y, c = _place()
        me, sibling = (x, y, c), (x, y, 1 - c)
        chips = [(1 - x, y), (x, 1 - y), (1 - x, 1 - y)]

        def rows(px, py, pc):
            return out_ref.at[pl.ds((4 * px + 2 * py + pc) * mp, mp), :]

        def copy(k, block, to, src=None):
            return pltpu.make_async_remote_copy(
                src_ref=rows(*block) if src is None else src, dst_ref=rows(*block),
                send_sem=send_sems.at[k], recv_sem=recv_sems.at[k], device_id=to, device_id_type=MESH)

        mine = pltpu.make_async_copy(x_ref, rows(*me), local_sem)
        mine.start()
        first = [copy(0, me, sibling, src=x_ref)]
        first += [copy(1 + j, me, (*chip, c), src=x_ref) for j, chip in enumerate(chips)]
        for cp in first:
            cp.start()
        passed = [copy(4 + j, (*chip, c), sibling) for j, chip in enumerate(chips)]
        for j, chip in enumerate(chips):
            copy(1 + j, (*chip, c), me).wait_recv()
            passed[j].start()
        copy(0, sibling, me).wait_recv()
        for j, chip in enumerate(chips):
            copy(4 + j, (*chip, 1 - c), me).wait_recv()
        for cp in first + passed:
            cp.wait_send()
        mine.wait()

    return pl.pallas_call(
        body, name=name, out_shape=jax.ShapeDtypeStruct((8 * mp, n), v.dtype),
        in_specs=[pl.BlockSpec(memory_space=pltpu.VMEM)], out_specs=pl.BlockSpec(memory_space=pltpu.VMEM),
        scratch_shapes=[pltpu.SemaphoreType.DMA((7,)), pltpu.SemaphoreType.DMA((7,)), pltpu.SemaphoreType.DMA],
        compiler_params=pltpu.CompilerParams(vmem_limit_bytes=VMEM_LIMIT),
    )(v)


def sum8(v, mp):
    def body(x_ref, o_ref):
        acc = x_ref[0:mp, :]
        for d in range(1, 8):
            acc = acc + x_ref[d * mp:(d + 1) * mp, :]
        o_ref[...] = acc

    return pl.pallas_call(body, name="sum8", out_shape=jax.ShapeDtypeStruct((mp, v.shape[1]), F32),
                          compiler_params=pltpu.CompilerParams(vmem_limit_bytes=VMEM_LIMIT))(v)


_ANY = pl.BlockSpec(memory_space=pl.ANY)


def _row_half(ref, lead, h):
    hr = ref.shape[-2] // 2
    return ref.at[(*lead, pl.ds(h * hr, hr), slice(None))]


def allgather_weights(slots):
    n = len(slots)

    def body(*refs):
        s_refs, o_refs, (send_sems, recv_sems) = refs[:n], refs[n:2 * n], refs[2 * n:]
        x, y, c = _place()
        me = 2 * x + y
        sibling = (x, y, 1 - c)
        chips = [(1 - x, y), (x, 1 - y), (1 - x, 1 - y)]

        def half(ref, slot, h):
            return _row_half(ref, (slot, slice(None)), h)

        def copy(k, src, dst, to):
            return pltpu.make_async_remote_copy(src_ref=src, dst_ref=dst, send_sem=send_sems.at[k],
                                                recv_sem=recv_sems.at[k], device_id=to, device_id_type=MESH)

        first = [copy(6 * a + k, half(s_refs[a], me, c), half(o_refs[a], me, c), (px, py, c))
                 for k, (px, py) in enumerate(chips) for a in range(n)]
        for cp in first:
            cp.start()
        passed = []
        for k, (px, py) in enumerate(chips):
            for a in range(n):
                got = half(o_refs[a], 2 * px + py, c)
                copy(6 * a + k, half(s_refs[a], me, c), got, (px, py, c)).wait_recv()
                cp = copy(6 * a + 3 + k, got, got, sibling)
                cp.start()
                passed.append(cp)
        for k, (px, py) in enumerate(chips):
            for a in range(n):
                got = half(o_refs[a], 2 * px + py, 1 - c)
                copy(6 * a + 3 + k, got, got, sibling).wait_recv()
        for cp in first + passed:
            cp.wait_send()

    return pl.pallas_call(
        body, name="allgather_weights", out_shape=[jax.ShapeDtypeStruct(s.shape, s.dtype) for s in slots],
        in_specs=[_ANY] * n, out_specs=[_ANY] * n, input_output_aliases={a: a for a in range(n)},
        scratch_shapes=[pltpu.SemaphoreType.DMA((6 * n,)), pltpu.SemaphoreType.DMA((6 * n,))],
    )(*slots)


_HBM = pl.BlockSpec(memory_space=pltpu.HBM)
_SEM = pl.BlockSpec(memory_space=pltpu.SEMAPHORE)
_EFFECT = pltpu.SideEffectType.DATAFLOW_SIDE_EFFECTING


def gather_start(slots, name, after=()):
    n = len(slots)
    na = len(after)

    def body(*refs):
        s_refs, send_sems, recv_sems, token = refs[:n], refs[n + na], refs[n + na + 1], refs[-1]
        x, y, c = _place()
        me = 2 * x + y
        for k, (px, py) in enumerate([(1 - x, y), (x, 1 - y), (1 - x, 1 - y)]):
            for a in range(n):
                pltpu.make_async_remote_copy(
                    src_ref=s_refs[a].at[me], dst_ref=s_refs[a].at[me], send_sem=send_sems.at[3 * a + k],
                    recv_sem=recv_sems.at[3 * a + k], device_id=(px, py, c), device_id_type=MESH).start()
        token[...] = jnp.zeros_like(token)

    out = pl.pallas_call(
        body, name=name,
        out_shape=(pltpu.SemaphoreType.DMA((3 * n,)), pltpu.SemaphoreType.DMA((3 * n,)),
                   *[pltpu.HBM(s.shape, s.dtype) for s in slots], jax.ShapeDtypeStruct((8, 128), F32)),
        in_specs=[_HBM] * n + [_ANY] * na, out_specs=(_SEM, _SEM, *[_HBM] * n, pl.BlockSpec(memory_space=pltpu.VMEM)),
        input_output_aliases={a: 2 + a for a in range(n)},
        compiler_params=pltpu.CompilerParams(has_side_effects=_EFFECT),
    )(*[pltpu.with_memory_space_constraint(s, pltpu.HBM) for s in slots], *after)
    return out[0], out[1], list(out[2:2 + n]), out[-1]


def gather_wait(send_sems, recv_sems, slots, after, name):
    n = len(slots)

    def body(*refs):
        s_refs, ssem, rsem = refs[:n], refs[n], refs[n + 1]
        x, y, c = _place()
        me = 2 * x + y
        for k, (px, py) in enumerate([(1 - x, y), (x, 1 - y), (1 - x, 1 - y)]):
            for a in range(n):
                cp = pltpu.make_async_remote_copy(
                    src_ref=s_refs[a].at[me], dst_ref=s_refs[a].at[2 * px + py], send_sem=ssem.at[3 * a + k],
                    recv_sem=rsem.at[3 * a + k], device_id=(px, py, c), device_id_type=MESH)
                cp.wait_send()
                cp.wait_recv()

    return pl.pallas_call(
        body, name=name, out_shape=[pltpu.HBM(s.shape, s.dtype) for s in slots],
        in_specs=[_HBM] * n + [_SEM, _SEM, _ANY], out_specs=[_HBM] * n,
        input_output_aliases={a: a for a in range(n)},
        compiler_params=pltpu.CompilerParams(has_side_effects=_EFFECT),
    )(*slots, send_sems, recv_sems, after)


def rs_sibling_halves(gs):
    n = len(gs)

    def body(*refs):
        g_refs, r_refs, (send_sems, recv_sems) = refs[:n], refs[n:2 * n], refs[2 * n:]
        x, y, c = _place()
        cps = [pltpu.make_async_remote_copy(
            src_ref=_row_half(g_refs[a], (slice(None), slice(None)), 1 - c), dst_ref=r_refs[a],
            send_sem=send_sems.at[a], recv_sem=recv_sems.at[a], device_id=(x, y, 1 - c), device_id_type=MESH)
            for a in range(n)]
        for cp in cps:
            cp.start()
        for cp in cps:
            cp.wait()

    return pl.pallas_call(
        body, name="rs_sibling_halves",
        out_shape=[jax.ShapeDtypeStruct((*g.shape[:2], g.shape[2] // 2, g.shape[3]), F32) for g in gs],
        in_specs=[_ANY] * n, out_specs=[_ANY] * n,
        scratch_shapes=[pltpu.SemaphoreType.DMA((n,)), pltpu.SemaphoreType.DMA((n,))],
    )(*gs)


def rs_chipsum16(g, recv1, cidx, name):
    nl, hr, cols = recv1.shape[1:]

    def body(c_ref, g_ref, r_ref, o_ref):
        o_ref[...] = (g_ref[...] + r_ref[...]).astype(BF16)

    blk = (1, 1, hr, cols)
    return pl.pallas_call(
        body, name=name, out_shape=jax.ShapeDtypeStruct(recv1.shape, BF16),
        grid_spec=pltpu.PrefetchScalarGridSpec(
            num_scalar_prefetch=1, grid=(4, nl),
            in_specs=[pl.BlockSpec(blk, lambda j, l, c: (j, l, c[0], 0)), pl.BlockSpec(blk, lambda j, l, c: (j, l, 0, 0))],
            out_specs=pl.BlockSpec(blk, lambda j, l, c: (j, l, 0, 0))),
        compiler_params=_cp("parallel", "parallel"),
    )(cidx, g, recv1)


def rs_exchange_chips(cs):
    n = len(cs)

    def body(*refs):
        s_refs, r_refs, (send_sems, recv_sems) = refs[:n], refs[n:2 * n], refs[2 * n:]
        x, y, c = _place()
        chips = [(1 - x, y), (x, 1 - y), (1 - x, 1 - y)]
        cps = [pltpu.make_async_remote_copy(
            src_ref=s_refs[a].at[2 * px + py], dst_ref=r_refs[a].at[k], send_sem=send_sems.at[3 * a + k],
            recv_sem=recv_sems.at[3 * a + k], device_id=(px, py, c), device_id_type=MESH)
            for k, (px, py) in enumerate(chips) for a in range(n)]
        for cp in cps:
            cp.start()
        for cp in cps:
            cp.wait()

    return pl.pallas_call(
        body, name="rs_exchange_chips", out_shape=[jax.ShapeDtypeStruct((3, *s.shape[1:]), BF16) for s in cs],
        in_specs=[_ANY] * n, out_specs=[_ANY] * n,
        scratch_shapes=[pltpu.SemaphoreType.DMA((3 * n,)), pltpu.SemaphoreType.DMA((3 * n,))],
    )(*cs)


def exchange_start(cs, name):
    n = len(cs)
    lands = [lax.empty((3, *c.shape[1:]), BF16) for c in cs]

    def body(*refs):
        s_refs, l_refs, send_sems, recv_sems, token = refs[:n], refs[n:2 * n], refs[2 * n], refs[2 * n + 1], refs[-1]
        x, y, c = _place()
        for k, (px, py) in enumerate([(1 - x, y), (x, 1 - y), (1 - x, 1 - y)]):
            for a in range(n):
                pltpu.make_async_remote_copy(
                    src_ref=s_refs[a].at[2 * px + py], dst_ref=l_refs[a].at[k], send_sem=send_sems.at[3 * a + k],
                    recv_sem=recv_sems.at[3 * a + k], device_id=(px, py, c), device_id_type=MESH).start()
        token[...] = jnp.zeros_like(token)

    bufs = list(cs) + lands
    out = pl.pallas_call(
        body, name=name,
        out_shape=(pltpu.SemaphoreType.DMA((3 * n,)), pltpu.SemaphoreType.DMA((3 * n,)),
                   *[pltpu.HBM(b.shape, b.dtype) for b in bufs], jax.ShapeDtypeStruct((8, 128), F32)),
        in_specs=[_HBM] * (2 * n), out_specs=(_SEM, _SEM, *[_HBM] * (2 * n), pl.BlockSpec(memory_space=pltpu.VMEM)),
        input_output_aliases={i: 2 + i for i in range(2 * n)},
        compiler_params=pltpu.CompilerParams(has_side_effects=_EFFECT),
    )(*[pltpu.with_memory_space_constraint(b, pltpu.HBM) for b in bufs])
    return out[0], out[1], list(out[2:2 + n]), list(out[2 + n:2 + 2 * n]), out[-1]


def exchange_wait(send_sems, recv_sems, cs, lands, after, name):
    n = len(cs)

    def body(*refs):
        s_refs, l_refs, ssem, rsem = refs[:n], refs[n:2 * n], refs[2 * n], refs[2 * n + 1]
        x, y, c = _place()
        for k, (px, py) in enumerate([(1 - x, y), (x, 1 - y), (1 - x, 1 - y)]):
            for a in range(n):
                cp = pltpu.make_async_remote_copy(
                    src_ref=s_refs[a].at[2 * px + py], dst_ref=l_refs[a].at[k], send_sem=ssem.at[3 * a + k],
                    recv_sem=rsem.at[3 * a + k], device_id=(px, py, c), device_id_type=MESH)
                cp.wait_send()
                cp.wait_recv()

    bufs = list(cs) + list(lands)
    out = pl.pallas_call(
        body, name=name, out_shape=[pltpu.HBM(b.shape, b.dtype) for b in bufs],
        in_specs=[_HBM] * (2 * n) + [_SEM, _SEM, _ANY], out_specs=[_HBM] * (2 * n),
        input_output_aliases={i: i for i in range(2 * n)},
        compiler_params=pltpu.CompilerParams(has_side_effects=_EFFECT),
    )(*bufs, send_sems, recv_sems, after)
    return list(out[n:])


def rs_final_sum(g, recv1, recv2, idx, name):
    nl, hr, cols = recv1.shape[1:]

    def body(i_ref, g_ref, r1_ref, r2_ref, o_ref):
        acc = g_ref[0, 0] + r1_ref[0, 0]
        for k in range(3):
            acc = acc + r2_ref[k, 0].astype(F32)
        o_ref[0] = acc

    blk = (1, 1, hr, cols)
    return pl.pallas_call(
        body, name=name, out_shape=jax.ShapeDtypeStruct((nl, 2 * hr, cols), F32),
        grid_spec=pltpu.PrefetchScalarGridSpec(
            num_scalar_prefetch=1, grid=(nl,),
            in_specs=[pl.BlockSpec(blk, lambda l, ix: (ix[0], l, ix[1], 0)), pl.BlockSpec(blk, lambda l, ix: (ix[0], l, 0, 0)),
                      pl.BlockSpec((3, 1, hr, cols), lambda l, ix: (0, l, 0, 0))],
            out_specs=pl.BlockSpec((1, hr, cols), lambda l, ix: (l, ix[1], 0))),
        compiler_params=_cp("parallel"),
    )(idx, g, recv1, recv2)


def rs_share_halves(fulls):
    n = len(fulls)

    def body(*refs):
        h_refs, o_refs, (send_sems, recv_sems) = refs[:n], refs[n:2 * n], refs[2 * n:]
        x, y, c = _place()
        sibling = (x, y, 1 - c)

        def copy(a, h):
            return pltpu.make_async_remote_copy(
                src_ref=_row_half(h_refs[a], (slice(None),), h), dst_ref=_row_half(o_refs[a], (slice(None),), h),
                send_sem=send_sems.at[a], recv_sem=recv_sems.at[a], device_id=sibling, device_id_type=MESH)

        for a in range(n):
            copy(a, c).start()
        for a in range(n):
            copy(a, c).wait_send()
            copy(a, 1 - c).wait_recv()

    return pl.pallas_call(
        body, name="rs_share_halves", out_shape=[jax.ShapeDtypeStruct(f.shape, F32) for f in fulls],
        in_specs=[_ANY] * n, out_specs=[_ANY] * n, input_output_aliases={a: a for a in range(n)},
        scratch_shapes=[pltpu.SemaphoreType.DMA((n,)), pltpu.SemaphoreType.DMA((n,))],
    )(*fulls)


def _own_slot(shard, chip, dtype):
    return lax.dynamic_update_slice(lax.empty((4, *shard.shape), dtype), shard.astype(dtype)[None],
                                    (chip,) + (0,) * shard.ndim)


def kernel(x, norm_mix_pre, norm_mix_post, norm_ffn_pre, norm_ffn_post, w_in, conv_a, gate_up_fwd, gate_bias_fwd, gate_up_bwd, gate_bias_bwd, gla_head_norm, w_out, w_up, conv_ffn, w_down, loss_target, m_norm_mix_pre, m_norm_mix_post, m_norm_ffn_pre, m_norm_ffn_post, m_w_in, m_conv_a, m_gate_up_fwd, m_gate_bias_fwd, m_gate_up_bwd, m_gate_bias_bwd, m_gla_head_norm, m_w_out, m_w_up, m_conv_ffn, m_w_down, v_norm_mix_pre, v_norm_mix_post, v_norm_ffn_pre, v_norm_ffn_post, v_w_in, v_conv_a, v_gate_up_fwd, v_gate_bias_fwd, v_gate_up_bwd, v_gate_bias_bwd, v_gla_head_norm, v_w_out, v_w_up, v_conv_ffn, v_w_down):
    L = x.shape[1]
    xi, yi, ci = _place()
    chip = 2 * xi + yi
    tl_gla, tl_mix, tl_ffn = min(L, TL_GLA), min(L, TL_MIX), min(L, TL_FFN)

    big_w = (w_in, w_out, w_up, w_down)
    a_in0 = allgather_weights([_own_slot(w_in[0:1], chip, BF16)])[0][:, 0]
    started = []
    prev = (a_in0,)
    for l in range(DEPTH):
        ws = big_w[1:] if l == 0 else big_w
        started.append(gather_start([_own_slot(w[l], chip, BF16) for w in ws], f"gather_start_{l}", after=prev))
        prev = (started[-1][3],)
    tokens = [s[3] for s in started]

    def full_w_in(a_in):
        return jnp.pad(jnp.concatenate([a_in[j] for j in range(4)], axis=1), ((0, 0), (0, DINP - DIN)))

    small = jnp.concatenate([conv_a.reshape(-1), gate_up_fwd.reshape(-1), gate_up_bwd.reshape(-1), conv_ffn.reshape(-1)])
    ms = small.shape[0] // 128
    sg = allgather8(small.reshape(ms, 128), "allgather_small_weights").reshape(4, 2, ms * 128)[:, 0]

    def small_full(off, shape):
        n = shape[0] * shape[1] * shape[2]
        return jnp.concatenate([sg[j, off:off + n].reshape(shape) for j in range(4)], axis=2)

    o1 = DEPTH * 3 * 128
    o2 = o1 + DEPTH * RK * 64
    o3 = o2 + DEPTH * RK * 64
    conv_a_f = small_full(0, (DEPTH, 3, 128))
    gup_f = small_full(o1, (DEPTH, RK, 64))
    gup_b = small_full(o2, (DEPTH, RK, 64))
    conv_ffn_f = small_full(o3, (DEPTH, 3, 1408))

    def gcat_of(l):
        g = jnp.zeros((LRW, 2 * DK), F32)
        g = g.at[0:RK, 0:DK].set(gup_f[l]).at[RK:2 * RK, DK:2 * DK].set(gup_b[l])
        return g.astype(BF16)

    gcats = [gcat_of(l) for l in range(DEPTH)]
    gbiases = [jnp.concatenate([gate_bias_fwd[l], gate_bias_bwd[l]])[None, :] for l in range(DEPTH)]
    ghn4s = [jnp.tile(gla_head_norm[l], NH)[None, :] for l in range(DEPTH)]

    xc = x.reshape(L, D)
    saved = []
    W_in, W_out, W_up, W_down = [], [], [], []
    tl_row = min(L, TL_ROW)
    for l in range(DEPTH):
        ssem, rsem, bufs, _ = started[l]
        if l > 0:
            a_in, a_out, a_up, a_down = gather_wait(ssem, rsem, bufs, xc, f"gather_wait_{l}")
        W_in.append(full_w_in(a_in0 if l == 0 else a_in))
        P, h1 = rms_matmul(xc, norm_mix_pre[l][None, :], W_in[l], 640, "proj_in", out_dtype=BF16,
                           after=tokens if l == 0 else ())
        o_f, o_b, sf, sb = gla_fwd(P, gcats[l], gbiases[l], tl_gla)
        if l == 0:
            a_out, a_up, a_down = gather_wait(ssem, rsem, bufs, o_f, "gather_wait_0")
        W_out.append(a_out.reshape(D, D))
        W_up.append(a_up)
        W_down.append(a_down.reshape(DFF, D))
        ycat, y, x1 = mix_out(P, o_f, o_b, conv_a_f[l], ghn4s[l], W_out[l], norm_mix_post[l][None, :], xc,
                              min(L, TL_MIX_OUT))
        U, h2 = rms_matmul(x1, norm_ffn_pre[l][None, :], W_up[l], WFF, "proj_up", n_out=2 * DFF, out_dtype=BF16,
                           w_spec=pl.BlockSpec((None, D, WFF), lambda i, j: (j, 0, 0)))
        y2, x2, ug, uv, z = ffn_down(U, conv_ffn_f[l], W_down[l], norm_ffn_post[l][None, :], x1, tl_ffn)
        saved.append(dict(x=xc, h1=h1, P=P, o_f=o_f, o_b=o_b, sf=sf, sb=sb, ycat=ycat, y=y, x1=x1, h2=h2, U=U, y2=y2,
                          ug=ug, uv=uv, z=z))
        xc = x2

    dx, loss_blk = loss_head(xc, loss_target.reshape(L, D), tl_row)

    big = ("w_in", "w_out", "w_up", "w_down")
    cidx = jnp.reshape(ci, (1,)).astype(jnp.int32)
    idx = jnp.stack([chip, ci]).astype(jnp.int32)
    grads = [None] * DEPTH
    reduced = [None] * DEPTH
    tl_dw = min(L, 1024)
    pending = None
    token = ()

    def finish(pend, after):
        lp, gs_p, recv1_p, (ssem, rsem, cs_thru, lands, _) = pend
        recv2 = exchange_wait(ssem, rsem, cs_thru, lands, after, f"exchange_wait_{lp}")
        halves = [rs_final_sum(g, r1, r2, idx, "rs_final_sum_" + k) for g, r1, r2, k in zip(gs_p, recv1_p, recv2, big)]
        reduced[lp] = rs_share_halves(halves)

    for l in reversed(range(DEPTH)):
        s = saved[l]
        dy2, dg4 = rms_bwd_pre(dx, s["y2"], norm_ffn_post[l][None, :], tl_row, after=token)
        du_g, du_v = ffn_bwd1(dy2, s["ug"], s["uv"], W_down[l], min(L, TL_FFN2))
        g_down = matmul_tn(s["z"], dy2, DFF // 2, D, tl_dw, "dw_down").reshape(4, 1, DFF // 4, D)
        dU_g, dU_v, dx1, dg3, dcf_g, dcf_v = ffn_bwd2(du_g, du_v, s["U"], conv_ffn_f[l], W_up[l], s["x1"], dx,
                                                      norm_ffn_pre[l][None, :],
                                        min(L, TL_FFN2))
        g_up = matmul_tn(s["h2"], dU_g, D, WFF, tl_dw, "dw_up_gate",
                         into=(lax.empty((4, 1, D, WFF), F32), (None, None, D, WFF), lambda p, q: (q, 0, 0, 0)))
        g_up = matmul_tn(s["h2"], dU_v, D, WFF, tl_dw, "dw_up_val",
                         into=(g_up, (None, None, D, WFF), lambda p, q: (NFF + q, 0, 0, 0)))
        dy, dg2 = rms_bwd_pre(dx1, s["y"], norm_mix_post[l][None, :], tl_row)
        dgb, dcc, dgo, do, dca, dghn = mix_bwd1(dy, W_out[l], s["P"], s["o_f"], s["o_b"], conv_a_f[l], ghn4s[l],
                                                min(L, TL_MIX_OUT))
        g_out = matmul_tn(s["ycat"], dy, D, D, tl_dw, "dw_out").reshape(4, 1, D // 4, D)
        gl = gla_bwd(s["P"], do, s["sf"], s["sb"], gcats[l], gbiases[l], tl_gla)
        dP, dx, dg1, dgcat, dbias = mix_bwd2(dgb, dcc, dgo, gl, s["P"], conv_a_f[l], gcats[l], gbiases[l], W_in[l],
                                             s["x"], dx1, norm_mix_pre[l][None, :], tl_mix)
        dW_in = matmul_tn(s["h1"], dP, D, 640, tl_dw, "dw_in")
        g_in = jnp.stack([dW_in[:, (DIN // 4) * j:(DIN // 4) * (j + 1)] for j in range(4)])[:, None]
        grads[l] = dict(
            norm_mix_pre=dg1[0], norm_mix_post=dg2[0], norm_ffn_pre=dg3[0], norm_ffn_post=dg4[0],
            conv_a=dca[0:3], gate_up_fwd=dgcat[0:RK, 0:DK], gate_bias_fwd=dbias[0, 0:DK],
            gate_up_bwd=dgcat[RK:2 * RK, DK:2 * DK], gate_bias_bwd=dbias[0, DK:2 * DK], gla_head_norm=dghn[0],
            conv_ffn=jnp.concatenate([dcf_g[j, 0:3] for j in range(NFF)] + [dcf_v[j, 0:3] for j in range(NFF)], axis=1))
        gs = [g_in, g_out, g_up, g_down]
        recv1 = rs_sibling_halves(gs)
        cs16 = [rs_chipsum16(g, r, cidx, "rs_chipsum16_" + k) for g, r, k in zip(gs, recv1, big)]
        flight = exchange_start(cs16, f"exchange_start_{l}")
        token = (flight[4],)
        if pending is not None:
            finish(pending, flight[4])
        pending = (l, gs, recv1, flight)
    finish(pending, pending[3][4])

    G = {k: jnp.stack([grads[l][k] for l in range(DEPTH)]) for k in grads[0]}

    small_names = ["norm_mix_pre", "norm_mix_post", "norm_ffn_pre", "norm_ffn_post", "conv_a", "gate_up_fwd",
                   "gate_bias_fwd", "gate_up_bwd", "gate_bias_bwd", "gla_head_norm", "conv_ffn"]
    flat = jnp.concatenate([G[k].reshape(-1) for k in small_names] + [loss_blk[0, 0:1]])
    n_small = flat.shape[0]
    mp = -(-n_small // 1024) * 8
    flat = jnp.pad(flat, (0, mp * 128 - n_small)).reshape(mp, 128)
    tot = sum8(allgather8(flat, "allgather_small_grads"), mp).reshape(-1)
    gsm = {}
    o = 0
    for k in small_names:
        n = G[k].size
        gsm[k] = tot[o:o + n].reshape(G[k].shape)
        o += n
    loss = tot[o]

    def my_cols(a, width):
        return lax.dynamic_slice_in_dim(a, chip * width, width, axis=2)

    gsm["conv_a"] = my_cols(gsm["conv_a"], 128)
    gsm["gate_up_fwd"] = my_cols(gsm["gate_up_fwd"], 64)
    gsm["gate_up_bwd"] = my_cols(gsm["gate_up_bwd"], 64)
    gsm["conv_ffn"] = my_cols(gsm["conv_ffn"], 1408)

    for a, k in enumerate(big):
        gsm[k] = jnp.concatenate([reduced[l][a] for l in range(DEPTH)], axis=0)

    names = ["norm_mix_pre", "norm_mix_post", "norm_ffn_pre", "norm_ffn_post", "w_in", "conv_a", "gate_up_fwd",
             "gate_bias_fwd", "gate_up_bwd", "gate_bias_bwd", "gla_head_norm", "w_out", "w_up", "conv_ffn", "w_down"]
    w = dict(norm_mix_pre=norm_mix_pre, norm_mix_post=norm_mix_post, norm_ffn_pre=norm_ffn_pre, norm_ffn_post=norm_ffn_post,
             w_in=w_in, conv_a=conv_a, gate_up_fwd=gate_up_fwd, gate_bias_fwd=gate_bias_fwd, gate_up_bwd=gate_up_bwd,
             gate_bias_bwd=gate_bias_bwd, gla_head_norm=gla_head_norm, w_out=w_out, w_up=w_up, conv_ffn=conv_ffn, w_down=w_down)
    m = dict(norm_mix_pre=m_norm_mix_pre, norm_mix_post=m_norm_mix_post, norm_ffn_pre=m_norm_ffn_pre, norm_ffn_post=m_norm_ffn_post,
             w_in=m_w_in, conv_a=m_conv_a, gate_up_fwd=m_gate_up_fwd, gate_bias_fwd=m_gate_bias_fwd, gate_up_bwd=m_gate_up_bwd,
             gate_bias_bwd=m_gate_bias_bwd, gla_head_norm=m_gla_head_norm, w_out=m_w_out, w_up=m_w_up, conv_ffn=m_conv_ffn, w_down=m_w_down)
    v = dict(norm_mix_pre=v_norm_mix_pre, norm_mix_post=v_norm_mix_post, norm_ffn_pre=v_norm_ffn_pre, norm_ffn_post=v_norm_ffn_post,
             w_in=v_w_in, conv_a=v_conv_a, gate_up_fwd=v_gate_up_fwd, gate_bias_fwd=v_gate_bias_fwd, gate_up_bwd=v_gate_up_bwd,
             gate_bias_bwd=v_gate_bias_bwd, gla_head_norm=v_gla_head_norm, w_out=v_w_out, w_up=v_w_up, conv_ffn=v_conv_ffn, w_down=v_w_down)
    upd = {k: adamw(w[k], gsm[k], m[k], v[k], "adamw_" + k) for k in names}
    return (loss, dx.reshape(1, L, D), *[gsm[k] for k in names], *[upd[k][0] for k in names],
            *[upd[k][1] for k in names], *[upd[k][2] for k in names])
```

```python
import functools

import jax
import jax.numpy as jnp
from jax import lax
from jax.experimental import pallas as pl
from jax.experimental.pallas import tpu as pltpu

F32 = jnp.float32
BF16 = jnp.bfloat16
MXU_DTYPE = jnp.bfloat16
MESH = pl.DeviceIdType.MESH

D = 1024
DC = 512
DG = 512
NH = 4
HV = 128
HK = 64
DK = 256
RK = 16
CH = 64
DFF = 2816
DIN = 3104
DINP = 3200
LRW = 128
DEPTH = 4
EPS = 1e-6
QSCALE = HK ** -0.5
GATE_NORM = 1.0 / 16.0
CB_GB, CB_GC, CB_GV, CB_GO = 0, 1, 2, 5
CB_Q, CB_K = 6, 7
CB_V = 4
CB_LR = 24
LR = 0.001
B1 = 0.9
B2 = 0.999
AEPS = 1e-08
WD = 0.01
STEP = 10
TM_PROJ = 1024
TL_GLA = 512
TL_MIX = 256
TL_MIX_OUT = 512
TL_ROW = 1024
TL_FFN = 256
TL_FFN2 = 512
VMEM_LIMIT = 56 * 1024 * 1024


def _cp(*sem):
    return pltpu.CompilerParams(dimension_semantics=sem if sem else None, vmem_limit_bytes=VMEM_LIMIT)


def _mm(a, b):
    return jnp.dot(a.astype(MXU_DTYPE), b.astype(MXU_DTYPE), preferred_element_type=F32)


def _mm_nt(a, b):
    return lax.dot_general(a.astype(MXU_DTYPE), b.astype(MXU_DTYPE), (((1,), (1,)), ((), ())),
                           preferred_element_type=F32)


def _mm_tn(a, b):
    return lax.dot_general(a.astype(MXU_DTYPE), b.astype(MXU_DTYPE), (((0,), (0,)), ((), ())),
                           preferred_element_type=F32)


def _mm_tri(tri, b):
    t = tri.astype(BF16)
    b1 = b.astype(BF16)
    r1 = b - b1.astype(F32)
    b2 = r1.astype(BF16)
    b3 = (r1 - b2.astype(F32)).astype(BF16)
    dot = lambda u: jnp.dot(t, u, preferred_element_type=F32)
    return dot(b1) + dot(b2) + dot(b3)


def _rms(x, g):
    r = lax.rsqrt(jnp.mean(x * x, axis=-1, keepdims=True) + EPS)
    return x * r * g


def _rms_bwd(dout, y, g):
    r = lax.rsqrt(jnp.mean(y * y, axis=-1, keepdims=True) + EPS)
    yh = y * r
    dyh = dout * g
    dy = r * (dyh - yh * jnp.mean(dyh * yh, axis=-1, keepdims=True))
    dg = jnp.sum(dout * yh, axis=0, keepdims=True)
    return dy, dg


def _sigmoid(x):
    return 0.5 * jnp.tanh(0.5 * x) + 0.5


def _logsig(x):
    return jnp.minimum(x, 0.0) - jnp.log1p(jnp.exp(-jnp.abs(x)))


def _shifts(x, p8, n8):
    n = x.shape[0]
    xe = jnp.concatenate([p8, x, n8], axis=0)
    return pltpu.roll(xe, 1, 0)[8:8 + n], pltpu.roll(xe, n + 15, 0)[8:8 + n]


def _halo_rows(prev_ref, next_ref, i, last):
    hr = prev_ref.shape[0]
    p = jnp.where(i == 0, 0.0, prev_ref[...].astype(F32)[hr - 8:hr, :])
    n = jnp.where(i == last, 0.0, next_ref[...].astype(F32)[0:8, :])
    return p, n


def _conv3(x, xp, xn, w_ref):
    xm1, xp1 = _shifts(x, xp, xn)
    return w_ref[0:1, :] * xm1 + w_ref[1:2, :] * x + w_ref[2:3, :] * xp1, xm1, xp1


def _conv3_t(d, dp, dn, w_ref):
    dm1, dp1 = _shifts(d, dp, dn)
    return w_ref[0:1, :] * dp1 + w_ref[1:2, :] * d + w_ref[2:3, :] * dm1


HALO32 = 8
HALO16 = 16


def _prev_row_blk(i, tl, hr):
    return jnp.maximum(i * (tl // hr) - 1, 0)


def _next_row_blk(i, tl, nrows, hr):
    return jnp.minimum((i + 1) * (tl // hr), nrows // hr - 1)


def _prev_blk(tl, cb, hr=HALO32):
    return lambda i: (_prev_row_blk(i, tl, hr), cb)


def _next_blk(tl, nrows, cb, hr=HALO32):
    return lambda i: (_next_row_blk(i, tl, nrows, hr), cb)


def rms_matmul(x, g, w, tn, name, w_spec=None, n_out=None, out_dtype=F32, after=()):
    L = x.shape[0]
    N = w.shape[1] if n_out is None else n_out
    tm = min(L, TM_PROJ)
    if w_spec is None:
        w_spec = pl.BlockSpec((D, tn), lambda i, j: (0, j))

    def body(x_ref, g_ref, w_ref, *rest):
        o_ref, h_ref = rest[-2:]

        @pl.when(pl.program_id(1) == 0)
        def _():
            h_ref[...] = _rms(x_ref[...], g_ref[...]).astype(BF16)

        o_ref[...] = _mm(h_ref[...], w_ref[...]).astype(out_dtype)

    return pl.pallas_call(
        body, name=name, grid=(L // tm, N // tn),
        in_specs=[pl.BlockSpec((tm, D), lambda i, j: (i, 0)), pl.BlockSpec((1, D), lambda i, j: (0, 0)), w_spec]
        + [_ANY] * len(after),
        out_specs=[pl.BlockSpec((tm, tn), lambda i, j: (i, j)), pl.BlockSpec((tm, D), lambda i, j: (i, 0))],
        out_shape=[jax.ShapeDtypeStruct((L, N), out_dtype), jax.ShapeDtypeStruct((L, D), BF16)],
        compiler_params=_cp("parallel", "arbitrary"),
    )(x, g, w, *after)


def _gla_masks():
    def blk(shape, rdiv, cdiv):
        r = lax.broadcasted_iota(jnp.int32, shape, 0) // rdiv
        c = lax.broadcasted_iota(jnp.int32, shape, 1) // cdiv
        return (r == c).astype(F32)

    r = lax.broadcasted_iota(jnp.int32, (CH, CH), 0)
    c = lax.broadcasted_iota(jnp.int32, (CH, CH), 1)
    r4 = lax.broadcasted_iota(jnp.int32, (NH * CH, CH), 0) % CH
    c4 = lax.broadcasted_iota(jnp.int32, (NH * CH, CH), 1)
    return dict(
        bdq=blk((NH * CH, DK), CH, HK),
        bdo=blk((NH * CH, DG), CH, HV),
        bds=blk((DG, DK), HV, HK),
        tril=(r >= c).astype(F32), triu=(r <= c).astype(F32),
        tril4=r4 >= c4, triu4=r4 <= c4,
    )


def _tile4(x):
    return jnp.concatenate([x, x, x, x], axis=0)


def _gla_tile_prep(q, k, a, m, rev, nc):
    tri = m["triu"] if rev else m["tril"]
    chunks = [a[c * CH:(c + 1) * CH] for c in range(nc)]
    cum = jnp.concatenate([_mm_tri(tri, ac) for ac in chunks], axis=0)
    tot = jnp.concatenate([jnp.sum(ac, axis=0, keepdims=True) for ac in chunks], axis=0)
    tot_rows = jnp.concatenate([jnp.broadcast_to(tot[c:c + 1], (CH, DK)) for c in range(nc)], axis=0)
    e = jnp.exp(cum)
    einv = jnp.exp(-cum)
    eout = jnp.exp(tot_rows - cum)
    q, k = q.astype(F32), k.astype(F32)
    return dict(e=e, einv=einv, eout=eout, dec=jnp.exp(tot), qt=q * QSCALE * e, kt=k * einv, kh=k * eout)


def _gla_scores(qt16, kt16, m, rev):
    qs = _tile4(qt16) * m["bdq"].astype(qt16.dtype)
    return qs, jnp.where(m["triu4"] if rev else m["tril4"], _mm_nt(qs, kt16), 0.0)


def _gla_chunk_fwd(qt16, kt16, kh16, v, dec, st_ref, m, rev):
    _, sc = _gla_scores(qt16, kt16, m, rev)
    v16 = v.astype(BF16)
    r = _mm(sc, v16)
    o_intra = jnp.concatenate([r[h * CH:(h + 1) * CH, h * HV:(h + 1) * HV] for h in range(NH)], axis=1)
    st = st_ref[...]
    st16 = st.astype(BF16)
    o = o_intra + _mm_nt(qt16, st16)
    st_ref[...] = st * dec + _mm_tn(v16, kh16) * m["bds"]
    return o, st16


def _gates(lr_ref, gc_ref, bs_ref, cols):
    return _logsig(_mm(lr_ref[...], gc_ref[:, cols]) + bs_ref[:, cols]) * GATE_NORM


def gla_fwd(P, gcat, gbias, tl):
    L = P.shape[0]
    nb = L // tl
    nc = tl // CH

    def body(qf, kf, vf, lf, qb, kb, vb, lb, gc_ref, bs_ref, of, ob, sf, sb, stf, stb,
             qtf, ktf, khf, dcf, qtb, ktb, khb, dcb):
        @pl.when(pl.program_id(0) == 0)
        def _():
            stf[...] = jnp.zeros_like(stf)
            stb[...] = jnp.zeros_like(stb)

        m = _gla_masks()
        for (q, k, lr, cols, rev, qt, kt, kh, dc) in ((qf, kf, lf, slice(0, DK), False, qtf, ktf, khf, dcf),
                                                      (qb, kb, lb, slice(DK, 2 * DK), True, qtb, ktb, khb, dcb)):
            p = _gla_tile_prep(q[...], k[...], _gates(lr, gc_ref, bs_ref, cols), m, rev, nc)
            qt[...] = p["qt"].astype(BF16)
            kt[...] = p["kt"].astype(BF16)
            kh[...] = p["kh"].astype(BF16)
            dc[...] = p["dec"]

        def chunk(c, carry):
            rows = pl.ds(pl.multiple_of(c * CH, CH), CH)
            o, st = _gla_chunk_fwd(qtf[rows, :], ktf[rows, :], khf[rows, :], vf[rows, :], dcf[pl.ds(c, 1), :], stf, m, False)
            of[rows, :] = o.astype(BF16)
            sf[c] = st
            cb = nc - 1 - c
            rows = pl.ds(pl.multiple_of(cb * CH, CH), CH)
            o, st = _gla_chunk_fwd(qtb[rows, :], ktb[rows, :], khb[rows, :], vb[rows, :], dcb[pl.ds(cb, 1), :], stb, m, True)
            ob[rows, :] = o.astype(BF16)
            sb[cb] = st
            return carry

        lax.fori_loop(0, nc, chunk, 0, unroll=2)

    fw = lambda cb: (lambda i: (i, cb))
    bw = lambda cb: (lambda i: (nb - 1 - i, cb))
    return pl.pallas_call(
        body, name="gla_fwd", grid=(nb,),
        in_specs=[pl.BlockSpec((tl, DK), fw(CB_Q)), pl.BlockSpec((tl, DK), fw(CB_K)), pl.BlockSpec((tl, DG), fw(CB_V)),
                  pl.BlockSpec((tl, LRW), fw(CB_LR)),
                  pl.BlockSpec((tl, DK), bw(CB_Q)), pl.BlockSpec((tl, DK), bw(CB_K)), pl.BlockSpec((tl, DG), bw(CB_V)),
                  pl.BlockSpec((tl, LRW), bw(CB_LR)),
                  pl.BlockSpec((LRW, 2 * DK), lambda i: (0, 0)), pl.BlockSpec((1, 2 * DK), lambda i: (0, 0))],
        out_specs=[pl.BlockSpec((tl, DG), lambda i: (i, 0)), pl.BlockSpec((tl, DG), lambda i: (nb - 1 - i, 0)),
                   pl.BlockSpec((nc, DG, DK), lambda i: (i, 0, 0)), pl.BlockSpec((nc, DG, DK), lambda i: (nb - 1 - i, 0, 0))],
        out_shape=[jax.ShapeDtypeStruct((L, DG), BF16), jax.ShapeDtypeStruct((L, DG), BF16),
                   jax.ShapeDtypeStruct((L // CH, DG, DK), BF16), jax.ShapeDtypeStruct((L // CH, DG, DK), BF16)],
        scratch_shapes=[pltpu.VMEM((DG, DK), F32), pltpu.VMEM((DG, DK), F32)]
        + [pltpu.VMEM((tl, DK), BF16)] * 3 + [pltpu.VMEM((nc, DK), F32)]
        + [pltpu.VMEM((tl, DK), BF16)] * 3 + [pltpu.VMEM((nc, DK), F32)],
        compiler_params=_cp("arbitrary"),
    )(P, P, P, P, P, P, P, P, gcat, gbias)


def _headnorm(o):
    oh, rs = [], []
    for h in range(NH):
        oo = o[:, h * HV:(h + 1) * HV]
        r = lax.rsqrt(jnp.mean(oo * oo, axis=-1, keepdims=True) + EPS)
        oh.append(oo * r)
        rs.append(r)
    return jnp.concatenate(oh, axis=1), rs


def mix_out(P, o_f, o_b, conv_a, ghn4, w_out, g2, x, tl):
    L = P.shape[0]
    nt = L // tl

    def body(gb, gc, gv, go, gcp, gvp, gcn, gvn, of, ob, ca, gh, wo, g2r, xr, ycat, yr, x1):
        i = pl.program_id(0)
        cp, cn = _halo_rows(gcp, gcn, i, nt - 1)
        vp, vn = _halo_rows(gvp, gvn, i, nt - 1)
        c = gc[...].astype(F32) * gv[...].astype(F32)
        cc, _, _ = _conv3(c, cp * vp, cn * vn, ca)
        ya = gb[...].astype(F32) * cc
        oh, _ = _headnorm(of[...].astype(F32) + ob[...].astype(F32))
        g = go[...].astype(F32)
        yb = g * _sigmoid(g) * (oh * gh[...])
        yc = jnp.concatenate([ya, yb], axis=1).astype(BF16)
        ycat[...] = yc
        y = _mm(yc, wo[...])
        yr[...] = y
        x1[...] = xr[...] + _rms(y, g2r[...])

    t = lambda cb: pl.BlockSpec((tl, DC), lambda i: (i, cb))
    hp = lambda cb: pl.BlockSpec((HALO16, DC), _prev_blk(tl, cb, HALO16))
    hn = lambda cb: pl.BlockSpec((HALO16, DC), _next_blk(tl, L, cb, HALO16))
    row = lambda n: pl.BlockSpec((tl, n), lambda i: (i, 0))
    full = lambda a: pl.BlockSpec(a.shape, lambda i: (0, 0))
    return pl.pallas_call(
        body, name="mix_out", grid=(nt,),
        in_specs=[t(CB_GB), t(CB_GC), t(CB_GV), t(CB_GO), hp(CB_GC), hp(CB_GV), hn(CB_GC), hn(CB_GV),
                  row(DG), row(DG), full(conv_a), full(ghn4), full(w_out), full(g2), row(D)],
        out_specs=[row(D), row(D), row(D)],
        out_shape=[jax.ShapeDtypeStruct((L, D), BF16), jax.ShapeDtypeStruct((L, D), F32),
                   jax.ShapeDtypeStruct((L, D), F32)],
        compiler_params=_cp("parallel"),
    )(P, P, P, P, P, P, P, P, o_f, o_b, conv_a, ghn4, w_out, g2, x)


NFF = 2
WFF = DFF // NFF
FFN_COL_CHUNKS = ((0, 512), (512, 1024), (1024, WFF))


def ffn_down(U, conv_ffn, w_down, g4, x1, tl):
    L = U.shape[0]
    nt = L // tl

    def body(u, up, un, cf, wd, g4r, x1r, y2, x2, ug, uv, zr):
        i = pl.program_id(0)
        acc = jnp.zeros((tl, D), F32)
        for j in range(NFF):
            gs = slice(j * WFF, (j + 1) * WFF)
            vs = slice(DFF + j * WFF, DFF + (j + 1) * WFF)
            z = []
            for s in (gs, vs):
                p, n = _halo_rows(up.at[:, s], un.at[:, s], i, nt - 1)
                z.append(_conv3(u[:, s].astype(F32), p, n, cf.at[:, s])[0])
            zz = (z[0] * _sigmoid(z[0]) * z[1]).astype(BF16)
            ug[:, gs] = z[0].astype(BF16)
            uv[:, gs] = z[1].astype(BF16)
            zr[:, gs] = zz
            acc = acc + _mm(zz, wd[gs, :])
        y2[...] = acc
        x2[...] = x1r[...] + _rms(acc, g4r[...])

    row = lambda n: pl.BlockSpec((tl, n), lambda i: (i, 0))
    full = lambda a: pl.BlockSpec(a.shape, lambda i: (0, 0))
    half = jax.ShapeDtypeStruct((L, DFF), BF16)
    return pl.pallas_call(
        body, name="ffn_down", grid=(nt,),
        in_specs=[row(2 * DFF), pl.BlockSpec((HALO16, 2 * DFF), _prev_blk(tl, 0, HALO16)),
                  pl.BlockSpec((HALO16, 2 * DFF), _next_blk(tl, L, 0, HALO16)),
                  full(conv_ffn), full(w_down), full(g4), row(D)],
        out_specs=[row(D), row(D), row(DFF), row(DFF), row(DFF)],
        out_shape=[jax.ShapeDtypeStruct((L, D), F32), jax.ShapeDtypeStruct((L, D), F32), half, half, half],
        compiler_params=_cp("parallel"),
    )(U, U, U, conv_ffn, w_down, g4, x1)


def loss_head(y, target, tl):
    L = y.shape[0]

    def body(yr, tr, dy, ls):
        @pl.when(pl.program_id(0) == 0)
        def _():
            ls[...] = jnp.zeros_like(ls)

        err = yr[...] - tr[...]
        dy[...] = err * (1.0 / D)
        ls[...] += (0.5 / D) * jnp.sum(err * err)

    row = pl.BlockSpec((tl, D), lambda i: (i, 0))
    return pl.pallas_call(
        body, name="loss_head", grid=(L // tl,), in_specs=[row, row],
        out_specs=[row, pl.BlockSpec((8, 128), lambda i: (0, 0))],
        out_shape=[jax.ShapeDtypeStruct((L, D), F32), jax.ShapeDtypeStruct((8, 128), F32)],
        compiler_params=_cp("arbitrary"),
    )(y, target)


def rms_bwd_pre(dout, y, g, tl, after=()):
    L = y.shape[0]

    def body(dr, yr, gr, *rest):
        dy, dg = rest[-2:]

        @pl.when(pl.program_id(0) == 0)
        def _():
            dg[...] = jnp.zeros_like(dg)

        a, b = _rms_bwd(dr[...], yr[...], gr[...])
        dy[...] = a.astype(BF16)
        dg[...] += b

    row = pl.BlockSpec((tl, D), lambda i: (i, 0))
    vec = pl.BlockSpec((1, D), lambda i: (0, 0))
    return pl.pallas_call(
        body, name="rms_bwd_pre", grid=(L // tl,), in_specs=[row, row, vec] + [_ANY] * len(after), out_specs=[row, vec],
        out_shape=[jax.ShapeDtypeStruct((L, D), BF16), jax.ShapeDtypeStruct((1, D), F32)],
        compiler_params=_cp("arbitrary"),
    )(dout, y, g, *after)


def ffn_bwd1(dy2, ug, uv, w_down, tl):
    L = ug.shape[0]

    def body(dy, ugr, uvr, wd, dug, duv):
        a = ugr[...].astype(F32)
        b = uvr[...].astype(F32)
        sg = _sigmoid(a)
        silu = a * sg
        dz = _mm_nt(dy[...], wd[...])
        dug[...] = (dz * b * (sg + silu * (1.0 - sg))).astype(BF16)
        duv[...] = (dz * silu).astype(BF16)

    tile = pl.BlockSpec((tl, WFF), lambda j, i: (i, j))
    half = jax.ShapeDtypeStruct((L, DFF), BF16)
    return pl.pallas_call(
        body, name="ffn_bwd1", grid=(NFF, L // tl),
        in_specs=[pl.BlockSpec((tl, D), lambda j, i: (i, 0)), tile, tile, pl.BlockSpec((WFF, D), lambda j, i: (j, 0))],
        out_specs=[tile, tile], out_shape=[half, half],
        compiler_params=_cp("parallel", "parallel"),
    )(dy2, ug, uv, w_down)


def ffn_bwd2(du_g, du_v, U, conv_ffn, w_up, x1, dres, g3, tl):
    L = x1.shape[0]
    nt = L // tl

    def body(dg_, dv_, dgp, dgn, dvp, dvn, ugr, uvr, cg, cv, wg, wv, x1r, drr, g3r, dUg, dUv, dx1, dg3, dcg, dcv, acc):
        i = pl.program_id(0)
        j = pl.program_id(1)

        @pl.when((i == 0) & (j == 0))
        def _():
            dg3[...] = jnp.zeros_like(dg3)
            dcg[...] = jnp.zeros_like(dcg)
            dcv[...] = jnp.zeros_like(dcv)

        part = None
        for c0, c1 in FFN_COL_CHUNKS:
            cs = slice(c0, c1)
            for d_ref, dp_ref, dn_ref, cw_ref, u_ref, dc, dU, w in ((dg_, dgp, dgn, cg, ugr, dcg, dUg, wg),
                                                                    (dv_, dvp, dvn, cv, uvr, dcv, dUv, wv)):
                p8, n8 = _halo_rows(dp_ref.at[:, cs], dn_ref.at[:, cs], i, nt - 1)
                d = d_ref[:, cs].astype(F32)
                dm1, dp1 = _shifts(d, p8, n8)
                du = (cw_ref[0:1, cs] * dp1 + cw_ref[1:2, cs] * d + cw_ref[2:3, cs] * dm1).astype(BF16)
                dU[:, cs] = du
                u = u_ref[:, cs].astype(F32)
                for k, t in enumerate((dp1, d, dm1)):
                    dc[j, k:k + 1, cs] += jnp.sum(t * u, axis=0, keepdims=True)
                term = _mm_nt(du, w[:, cs])
                part = term if part is None else part + term

        @pl.when(j == 0)
        def _():
            acc[...] = part

        @pl.when(j > 0)
        def _():
            acc[...] += part

        @pl.when(j == NFF - 1)
        def _():
            dx, dg = _rms_bwd(acc[...], x1r[...], g3r[...])
            dx1[...] = drr[...] + dx
            dg3[...] += dg

    tile = pl.BlockSpec((tl, WFF), lambda i, j: (i, j))
    prev = pl.BlockSpec((HALO16, WFF), lambda i, j: (_prev_row_blk(i, tl, HALO16), j))
    nxt = pl.BlockSpec((HALO16, WFF), lambda i, j: (_next_row_blk(i, tl, L, HALO16), j))
    cw = lambda off: pl.BlockSpec((3, WFF), lambda i, j: (0, off + j))
    ww = lambda off: pl.BlockSpec((None, D, WFF), lambda i, j: (off + j, 0, 0))
    row = pl.BlockSpec((tl, D), lambda i, j: (i, 0))
    vec = pl.BlockSpec((1, D), lambda i, j: (0, 0))
    ut = lambda off: pl.BlockSpec((tl, WFF), lambda i, j: (i, off + j))
    dcs = pl.BlockSpec((NFF, 8, WFF), lambda i, j: (0, 0, 0))
    return pl.pallas_call(
        body, name="ffn_bwd2", grid=(nt, NFF),
        in_specs=[tile, tile, prev, nxt, prev, nxt, ut(0), ut(NFF), cw(0), cw(NFF), ww(0), ww(NFF), row, row, vec],
        out_specs=[tile, tile, row, vec, dcs, dcs],
        out_shape=[jax.ShapeDtypeStruct((L, DFF), BF16), jax.ShapeDtypeStruct((L, DFF), BF16),
                   jax.ShapeDtypeStruct((L, D), F32), jax.ShapeDtypeStruct((1, D), F32),
                   jax.ShapeDtypeStruct((NFF, 8, WFF), F32), jax.ShapeDtypeStruct((NFF, 8, WFF), F32)],
        scratch_shapes=[pltpu.VMEM((tl, D), F32)],
        compiler_params=_cp("arbitrary", "arbitrary"),
    )(du_g, du_v, du_g, du_g, du_v, du_v, U, U, conv_ffn, conv_ffn, w_up, w_up, x1, dres, g3)


def matmul_tn(a, b, ta, tn, tl, name, into=None):
    L, Ka = a.shape
    N = b.shape[1]

    def body(ar, br, *rest):
        o = rest[-1]

        @pl.when(pl.program_id(2) == 0)
        def _():
            o[...] = jnp.zeros_like(o)

        o[...] += _mm_tn(ar[...], br[...]).reshape(o.shape)

    in_specs = [pl.BlockSpec((tl, ta), lambda p, q, l: (l, p)), pl.BlockSpec((tl, tn), lambda p, q, l: (l, q))]
    if into is None:
        return pl.pallas_call(
            body, name=name, grid=(Ka // ta, N // tn, L // tl), in_specs=in_specs,
            out_specs=pl.BlockSpec((ta, tn), lambda p, q, l: (p, q)),
            out_shape=jax.ShapeDtypeStruct((Ka, N), F32),
            compiler_params=_cp("parallel", "parallel", "arbitrary"),
        )(a, b)
    buf, blk, idx = into
    return pl.pallas_call(
        body, name=name, grid=(Ka // ta, N // tn, L // tl), in_specs=in_specs + [_ANY],
        out_specs=pl.BlockSpec(blk, lambda p, q, l: idx(p, q)),
        out_shape=jax.ShapeDtypeStruct(buf.shape, F32), input_output_aliases={2: 0},
        compiler_params=_cp("parallel", "parallel", "arbitrary"),
    )(a, b, buf)


def mix_bwd1(dy, w_out, P, o_f, o_b, conv_a, ghn4, tl):
    L = P.shape[0]
    nt = L // tl

    def body(dyr, wo, gb, gc, gv, go, gcp, gvp, gcn, gvn, of, ob, ca, gh, dgb, dcc, dgo, do, dca, dgh):
        i = pl.program_id(0)

        @pl.when(i == 0)
        def _():
            dca[...] = jnp.zeros_like(dca)
            dgh[...] = jnp.zeros_like(dgh)

        dycat = _mm_nt(dyr[...], wo[...])
        dya = dycat[:, 0:DC]
        dyb = dycat[:, DC:D]
        cp, cn = _halo_rows(gcp, gcn, i, nt - 1)
        vp, vn = _halo_rows(gvp, gvn, i, nt - 1)
        c = gc[...].astype(F32) * gv[...].astype(F32)
        cc, c_m1, c_p1 = _conv3(c, cp * vp, cn * vn, ca)
        dgb[...] = (dya * cc).astype(BF16)
        d = dya * gb[...].astype(F32)
        dcc[...] = d.astype(BF16)
        for k, s in enumerate((c_m1, c, c_p1)):
            dca[k:k + 1, :] += jnp.sum(d * s, axis=0, keepdims=True)
        oh, rs = _headnorm(of[...].astype(F32) + ob[...].astype(F32))
        g = go[...].astype(F32)
        sg = _sigmoid(g)
        silu = g * sg
        dgo[...] = (dyb * (oh * gh[...]) * (sg * (1.0 + g * (1.0 - sg)))).astype(BF16)
        don = dyb * silu
        t = jnp.sum(don * oh, axis=0, keepdims=True)
        dgh[0:1, :] += t[:, 0:HV] + t[:, HV:2 * HV] + t[:, 2 * HV:3 * HV] + t[:, 3 * HV:4 * HV]
        doh = don * gh[...]
        parts = []
        for h in range(NH):
            hs = slice(h * HV, (h + 1) * HV)
            parts.append(rs[h] * (doh[:, hs] - oh[:, hs] * jnp.mean(doh[:, hs] * oh[:, hs], axis=-1, keepdims=True)))
        do[...] = jnp.concatenate(parts, axis=1).astype(BF16)

    t = lambda cb: pl.BlockSpec((tl, DC), lambda i: (i, cb))
    hp = lambda cb: pl.BlockSpec((HALO16, DC), _prev_blk(tl, cb, HALO16))
    hn = lambda cb: pl.BlockSpec((HALO16, DC), _next_blk(tl, L, cb, HALO16))
    row = lambda n: pl.BlockSpec((tl, n), lambda i: (i, 0))
    full = lambda a: pl.BlockSpec(a.shape, lambda i: (0, 0))
    act16 = lambda n: jax.ShapeDtypeStruct((L, n), BF16)
    return pl.pallas_call(
        body, name="mix_bwd1", grid=(nt,),
        in_specs=[row(D), full(w_out), t(CB_GB), t(CB_GC), t(CB_GV), t(CB_GO), hp(CB_GC), hp(CB_GV), hn(CB_GC), hn(CB_GV),
                  row(DG), row(DG), full(conv_a), full(ghn4)],
        out_specs=[row(DC), row(DC), row(DG), row(DG), pl.BlockSpec((8, DC), lambda i: (0, 0)),
                   pl.BlockSpec((8, HV), lambda i: (0, 0))],
        out_shape=[act16(DC), act16(DC), act16(DG), act16(DG), jax.ShapeDtypeStruct((8, DC), F32),
                   jax.ShapeDtypeStruct((8, HV), F32)],
        compiler_params=_cp("arbitrary"),
    )(dy, w_out, P, P, P, P, P, P, P, P, o_f, o_b, conv_a, ghn4)


def _gla_chunk_bwd(qt, kt, kh, v, do, st16, dec, g_ref, m, rev):
    qt16, kt16, kh16, v16, do16 = (t.astype(BF16) for t in (qt, kt, kh, v, do))
    qs, sc = _gla_scores(qt16, kt16, m, rev)
    g = g_ref[...]
    g16 = g.astype(BF16)
    dob = _tile4(do16) * m["bdo"].astype(BF16)
    dv = _mm_tn(sc, dob) + _mm_nt(kh16, g16)
    dsc = jnp.where(m["triu4"] if rev else m["tril4"], _mm_nt(dob, v16), 0.0)
    r1 = _mm(dsc, kt16) * m["bdq"]
    dqt = r1[0:CH] + r1[CH:2 * CH] + r1[2 * CH:3 * CH] + r1[3 * CH:4 * CH] + _mm(do16, st16)
    dkt = _mm_tn(dsc, qs)
    dkh = _mm(v16, g16)
    dd = jnp.sum(g * st16.astype(F32), axis=0, keepdims=True)
    g_ref[...] = g * dec + _mm_tn(do16, qt16) * m["bds"]
    return dv, dqt, dkt, dkh, dd


def gla_bwd(P, do, sf, sb, gcat, gbias, tl):
    L = P.shape[0]
    nb = L // tl
    nc = tl // CH

    def body(qf, kf, vf, lf, dof, sfr, qb, kb, vb, lb, dob, sbr, gc_ref, bs_ref,
             dqf, dkf, dvf, daf, dqb, dkb, dvb, dab, gf, gbk, *scr):
        @pl.when(pl.program_id(0) == 0)
        def _():
            gf[...] = jnp.zeros_like(gf)
            gbk[...] = jnp.zeros_like(gbk)

        m = _gla_masks()
        keys = ("qt", "kt", "kh", "e", "einv", "eout", "dec")
        names = keys + ("dd", "dqt", "dkt", "dkh")
        pf = dict(zip(names, scr[0:11]))
        pb = dict(zip(names, scr[11:22]))
        for (q, k, lr, cols, rev, pr) in ((qf, kf, lf, slice(0, DK), False, pf), (qb, kb, lb, slice(DK, 2 * DK), True, pb)):
            p = _gla_tile_prep(q[...], k[...], _gates(lr, gc_ref, bs_ref, cols), m, rev, nc)
            for key in keys:
                pr[key][...] = p[key]

        def step(c, v, dor, st, g_ref, pr, dv, rev):
            rows = pl.ds(pl.multiple_of(c * CH, CH), CH)
            dvc, dqt, dkt, dkh, dd = _gla_chunk_bwd(pr["qt"][rows, :], pr["kt"][rows, :], pr["kh"][rows, :], v[rows, :],
                                                    dor[rows, :], st[c], pr["dec"][pl.ds(c, 1), :], g_ref, m, rev)
            dv[rows, :] = dvc.astype(BF16)
            pr["dqt"][rows, :] = dqt
            pr["dkt"][rows, :] = dkt
            pr["dkh"][rows, :] = dkh
            pr["dd"][pl.ds(c, 1), :] = dd

        def chunk(c, carry):
            step(nc - 1 - c, vf, dof, sfr, gf, pf, dvf, False)
            step(c, vb, dob, sbr, gbk, pb, dvb, True)
            return carry

        lax.fori_loop(0, nc, chunk, 0, unroll=2)

        def finish(pr, dq, dk, da, rev):
            dqt, dkt, dkh = pr["dqt"][...], pr["dkt"][...], pr["dkh"][...]
            kk = dkh * pr["kh"][...]
            dcum = dqt * pr["qt"][...] - dkt * pr["kt"][...] - kk
            dtot = pr["dd"][...] * pr["dec"][...]
            tri_t = m["tril"] if rev else m["triu"]
            parts = []
            for c in range(nc):
                rs = slice(c * CH, (c + 1) * CH)
                parts.append(_mm_tri(tri_t, dcum[rs]) + (jnp.sum(kk[rs], axis=0, keepdims=True) + dtot[c:c + 1]))
            da[...] = jnp.concatenate(parts, axis=0).astype(BF16)
            dq[...] = (dqt * pr["e"][...] * QSCALE).astype(BF16)
            dk[...] = (dkt * pr["einv"][...] + dkh * pr["eout"][...]).astype(BF16)

        finish(pf, dqf, dkf, daf, False)
        finish(pb, dqb, dkb, dab, True)

    fwd_dir = lambda cb: (lambda i: (nb - 1 - i, cb))
    bwd_dir = lambda cb: (lambda i: (i, cb))

    def side(ix):
        return [pl.BlockSpec((tl, DK), ix(CB_Q)), pl.BlockSpec((tl, DK), ix(CB_K)), pl.BlockSpec((tl, DG), ix(CB_V)),
                pl.BlockSpec((tl, LRW), ix(CB_LR)), pl.BlockSpec((tl, DG), ix(0)),
                pl.BlockSpec((nc, DG, DK), lambda i: (ix(0)(i)[0], 0, 0))]

    def outs(ix):
        return [pl.BlockSpec((tl, DK), ix(0)), pl.BlockSpec((tl, DK), ix(0)), pl.BlockSpec((tl, DG), ix(0)),
                pl.BlockSpec((tl, DK), ix(0))]

    o_shape = [jax.ShapeDtypeStruct((L, DK), BF16), jax.ShapeDtypeStruct((L, DK), BF16),
               jax.ShapeDtypeStruct((L, DG), BF16), jax.ShapeDtypeStruct((L, DK), BF16)]
    return pl.pallas_call(
        body, name="gla_bwd", grid=(nb,),
        in_specs=side(fwd_dir) + side(bwd_dir) + [pl.BlockSpec((LRW, 2 * DK), lambda i: (0, 0)),
                                                  pl.BlockSpec((1, 2 * DK), lambda i: (0, 0))],
        out_specs=outs(fwd_dir) + outs(bwd_dir),
        out_shape=o_shape + o_shape,
        scratch_shapes=[pltpu.VMEM((DG, DK), F32), pltpu.VMEM((DG, DK), F32)]
        + ([pltpu.VMEM((tl, DK), F32)] * 6 + [pltpu.VMEM((nc, DK), F32)] * 2 + [pltpu.VMEM((tl, DK), F32)] * 3) * 2,
        compiler_params=_cp("arbitrary"),
    )(P, P, P, P, do, sf, P, P, P, P, do, sb, gcat, gbias)


def mix_bwd2(dgb, dcc, dgo, gl, P, conv_a, gcat, gbias, w_in, x, dres, g1, tl):
    L = P.shape[0]
    nt = L // tl

    def body(dgbr, dccr, dccp, dccn, dgor, dqf, dkf, dvf, daf, dqb, dkb, dvb, dab, gc, gv, lr, ca, gcr, bsr, wi,
             xr, drr, g1r, dP, dx, dg1, dgcat, dbias):
        i = pl.program_id(0)

        @pl.when(i == 0)
        def _():
            dg1[...] = jnp.zeros_like(dg1)
            dgcat[...] = jnp.zeros_like(dgcat)
            dbias[...] = jnp.zeros_like(dbias)

        p, n = _halo_rows(dccp, dccn, i, nt - 1)
        dc = _conv3_t(dccr[...].astype(F32), p, n, ca)
        pre = _mm(lr[...], gcr[...]) + bsr[...]
        da = jnp.concatenate([daf[...], dab[...]], axis=1).astype(F32)
        add32 = lambda a, b: a[...].astype(F32) + b[...].astype(F32)
        dpre = da * GATE_NORM * (1.0 - _sigmoid(pre))
        dpre16 = dpre.astype(BF16)
        dP[:, 0:DC] = dgbr[...].astype(BF16)
        dP[:, DC:2 * DC] = (dc * gv[...].astype(F32)).astype(BF16)
        dP[:, 2 * DC:3 * DC] = (dc * gc[...].astype(F32)).astype(BF16)
        dP[:, 1536:1792] = add32(dqf, dqb).astype(BF16)
        dP[:, 1792:2048] = add32(dkf, dkb).astype(BF16)
        dP[:, 2048:2560] = add32(dvf, dvb).astype(BF16)
        dP[:, 2560:3072] = dgor[...].astype(BF16)
        dP[:, 3072:3200] = _mm_nt(dpre16, gcr[...]).astype(BF16)
        dgcat[...] += _mm_tn(lr[...], dpre16)
        dbias[0:1, :] += jnp.sum(dpre, axis=0, keepdims=True)
        dh, dg = _rms_bwd(_mm_nt(dP[...], wi[...]), xr[...], g1r[...])
        dx[...] = drr[...] + dh
        dg1[...] += dg

    row = lambda n: pl.BlockSpec((tl, n), lambda i: (i, 0))
    t = lambda w, cb: pl.BlockSpec((tl, w), lambda i: (i, cb))
    full = lambda a: pl.BlockSpec(a.shape, lambda i: (0, 0))
    return pl.pallas_call(
        body, name="mix_bwd2", grid=(nt,),
        in_specs=[row(DC), row(DC), pl.BlockSpec((HALO16, DC), _prev_blk(tl, 0, HALO16)),
                  pl.BlockSpec((HALO16, DC), _next_blk(tl, L, 0, HALO16)),
                  row(DG), row(DK), row(DK), row(DG), row(DK), row(DK), row(DK), row(DG), row(DK),
                  t(DC, CB_GC), t(DC, CB_GV), t(LRW, CB_LR), full(conv_a), full(gcat), full(gbias), full(w_in),
                  row(D), row(D), full(g1)],
        out_specs=[row(DINP), row(D), pl.BlockSpec((1, D), lambda i: (0, 0)), pl.BlockSpec((LRW, 2 * DK), lambda i: (0, 0)),
                   pl.BlockSpec((8, 2 * DK), lambda i: (0, 0))],
        out_shape=[jax.ShapeDtypeStruct((L, DINP), BF16), jax.ShapeDtypeStruct((L, D), F32),
                   jax.ShapeDtypeStruct((1, D), F32), jax.ShapeDtypeStruct((LRW, 2 * DK), F32),
                   jax.ShapeDtypeStruct((8, 2 * DK), F32)],
        compiler_params=_cp("arbitrary"),
    )(dgb, dcc, dcc, dcc, dgo, *gl, P, P, P, conv_a, gcat, gbias, w_in, x, dres, g1)


def _row_tile(rows, cols):
    if rows * cols * 4 <= 2 * 1024 * 1024:
        return rows
    best = 8
    for t in range(8, rows, 8):
        if rows % t == 0 and t * cols * 4 <= 2 * 1024 * 1024:
            best = t
    return best


def adamw(w, g, m, v, name):
    shape = w.shape
    cols = shape[-1]
    w2, g2, m2, v2 = (a.reshape(-1, cols) for a in (w, g, m, v))
    rows = w2.shape[0]
    tr = _row_tile(rows, cols)

    def body(wr, gr, mr, vr, dl, nm, nv):
        gg = gr[...]
        mm = B1 * mr[...] + (1.0 - B1) * gg
        vv = B2 * vr[...] + (1.0 - B2) * (gg * gg)
        m_hat = mm / (1.0 - B1 ** STEP)
        v_hat = vv / (1.0 - B2 ** STEP)
        dl[...] = -LR * (m_hat / (jnp.sqrt(v_hat) + AEPS) + WD * wr[...])
        nm[...] = mm
        nv[...] = vv

    blk = pl.BlockSpec((tr, cols), lambda i: (i, 0))
    o = jax.ShapeDtypeStruct((rows, cols), F32)
    d, nm, nv = pl.pallas_call(
        body, name=name, grid=(rows // tr,), in_specs=[blk] * 4, out_specs=[blk] * 3, out_shape=[o, o, o],
        compiler_params=_cp("parallel"),
    )(w2, g2, m2, v2)
    return d.reshape(shape), nm.reshape(shape), nv.reshape(shape)


def _place():
    return lax.axis_index("x"), lax.axis_index("y"), lax.axis_index("c")


def allgather8(v, name):
    mp, n = v.shape

    def body(x_ref, out_ref, send_sems, recv_sems, local_sem):
        x, y, c = _place()
        me, sibling = (x, y, c), (x, y, 1 - c)
        chips = [(1 - x, y), (x, 1 - y), (1 - x, 1 - y)]

        def rows(px, py, pc):
            return out_ref.at[pl.ds((4 * px + 2 * py + pc) * mp, mp), :]

        def copy(k, block, to, src=None):
            return pltpu.make_async_remote_copy(
                src_ref=rows(*block) if src is None else src, dst_ref=rows(*block),
                send_sem=send_sems.at[k], recv_sem=recv_sems.at[k], device_id=to, device_id_type=MESH)

        mine = pltpu.make_async_copy(x_ref, rows(*me), local_sem)
        mine.start()
        first = [copy(0, me, sibling, src=x_ref)]
        first += [copy(1 + j, me, (*chip, c), src=x_ref) for j, chip in enumerate(chips)]
        for cp in first:
            cp.start()
        passed = [copy(4 + j, (*chip, c), sibling) for j, chip in enumerate(chips)]
        for j, chip in enumerate(chips):
            copy(1 + j, (*chip, c), me).wait_recv()
            passed[j].start()
        copy(0, sibling, me).wait_recv()
        for j, chip in enumerate(chips):
            copy(4 + j, (*chip, 1 - c), me).wait_recv()
        for cp in first + passed:
            cp.wait_send()
        mine.wait()

    return pl.pallas_call(
        body, name=name, out_shape=jax.ShapeDtypeStruct((8 * mp, n), v.dtype),
        in_specs=[pl.BlockSpec(memory_space=pltpu.VMEM)], out_specs=pl.BlockSpec(memory_space=pltpu.VMEM),
        scratch_shapes=[pltpu.SemaphoreType.DMA((7,)), pltpu.SemaphoreType.DMA((7,)), pltpu.SemaphoreType.DMA],
        compiler_params=pltpu.CompilerParams(vmem_limit_bytes=VMEM_LIMIT),
    )(v)


def sum8(v, mp):
    def body(x_ref, o_ref):
        acc = x_ref[0:mp, :]
        for d in range(1, 8):
            acc = acc + x_ref[d * mp:(d + 1) * mp, :]
        o_ref[...] = acc

    return pl.pallas_call(body, name="sum8", out_shape=jax.ShapeDtypeStruct((mp, v.shape[1]), F32),
                          compiler_params=pltpu.CompilerParams(vmem_limit_bytes=VMEM_LIMIT))(v)


_ANY = pl.BlockSpec(memory_space=pl.ANY)


def _row_half(ref, lead, h):
    hr = ref.shape[-2] // 2
    return ref.at[(*lead, pl.ds(h * hr, hr), slice(None))]


def allgather_weights(slots):
    n = len(slots)

    def body(*refs):
        s_refs, o_refs, (send_sems, recv_sems) = refs[:n], refs[n:2 * n], refs[2 * n:]
        x, y, c = _place()
        me = 2 * x + y
        sibling = (x, y, 1 - c)
        chips = [(1 - x, y), (x, 1 - y), (1 - x, 1 - y)]

        def half(ref, slot, h):
            return _row_half(ref, (slot, slice(None)), h)

        def copy(k, src, dst, to):
            return pltpu.make_async_remote_copy(src_ref=src, dst_ref=dst, send_sem=send_sems.at[k],
                                                recv_sem=recv_sems.at[k], device_id=to, device_id_type=MESH)

        first = [copy(6 * a + k, half(s_refs[a], me, c), half(o_refs[a], me, c), (px, py, c))
                 for k, (px, py) in enumerate(chips) for a in range(n)]
        for cp in first:
            cp.start()
        passed = []
        for k, (px, py) in enumerate(chips):
            for a in range(n):
                got = half(o_refs[a], 2 * px + py, c)
                copy(6 * a + k, half(s_refs[a], me, c), got, (px, py, c)).wait_recv()
                cp = copy(6 * a + 3 + k, got, got, sibling)
                cp.start()
                passed.append(cp)
        for k, (px, py) in enumerate(chips):
            for a in range(n):
                got = half(o_refs[a], 2 * px + py, 1 - c)
                copy(6 * a + 3 + k, got, got, sibling).wait_recv()
        for cp in first + passed:
            cp.wait_send()

    return pl.pallas_call(
        body, name="allgather_weights", out_shape=[jax.ShapeDtypeStruct(s.shape, s.dtype) for s in slots],
        in_specs=[_ANY] * n, out_specs=[_ANY] * n, input_output_aliases={a: a for a in range(n)},
        scratch_shapes=[pltpu.SemaphoreType.DMA((6 * n,)), pltpu.SemaphoreType.DMA((6 * n,))],
    )(*slots)


_HBM = pl.BlockSpec(memory_space=pltpu.HBM)
_SEM = pl.BlockSpec(memory_space=pltpu.SEMAPHORE)
_EFFECT = pltpu.SideEffectType.DATAFLOW_SIDE_EFFECTING


def gather_start(slots, name, after=()):
    n = len(slots)
    na = len(after)

    def body(*refs):
        s_refs, send_sems, recv_sems, token = refs[:n], refs[n + na], refs[n + na + 1], refs[-1]
        x, y, c = _place()
        me = 2 * x + y
        for k, (px, py) in enumerate([(1 - x, y), (x, 1 - y), (1 - x, 1 - y)]):
            for a in range(n):
                pltpu.make_async_remote_copy(
                    src_ref=s_refs[a].at[me], dst_ref=s_refs[a].at[me], send_sem=send_sems.at[3 * a + k],
                    recv_sem=recv_sems.at[3 * a + k], device_id=(px, py, c), device_id_type=MESH).start()
        token[...] = jnp.zeros_like(token)

    out = pl.pallas_call(
        body, name=name,
        out_shape=(pltpu.SemaphoreType.DMA((3 * n,)), pltpu.SemaphoreType.DMA((3 * n,)),
                   *[pltpu.HBM(s.shape, s.dtype) for s in slots], jax.ShapeDtypeStruct((8, 128), F32)),
        in_specs=[_HBM] * n + [_ANY] * na, out_specs=(_SEM, _SEM, *[_HBM] * n, pl.BlockSpec(memory_space=pltpu.VMEM)),
        input_output_aliases={a: 2 + a for a in range(n)},
        compiler_params=pltpu.CompilerParams(has_side_effects=_EFFECT),
    )(*[pltpu.with_memory_space_constraint(s, pltpu.HBM) for s in slots], *after)
    return out[0], out[1], list(out[2:2 + n]), out[-1]


def gather_wait(send_sems, recv_sems, slots, after, name):
    n = len(slots)

    def body(*refs):
        s_refs, ssem, rsem = refs[:n], refs[n], refs[n + 1]
        x, y, c = _place()
        me = 2 * x + y
        for k, (px, py) in enumerate([(1 - x, y), (x, 1 - y), (1 - x, 1 - y)]):
            for a in range(n):
                cp = pltpu.make_async_remote_copy(
                    src_ref=s_refs[a].at[me], dst_ref=s_refs[a].at[2 * px + py], send_sem=ssem.at[3 * a + k],
                    recv_sem=rsem.at[3 * a + k], device_id=(px, py, c), device_id_type=MESH)
                cp.wait_send()
                cp.wait_recv()

    return pl.pallas_call(
        body, name=name, out_shape=[pltpu.HBM(s.shape, s.dtype) for s in slots],
        in_specs=[_HBM] * n + [_SEM, _SEM, _ANY], out_specs=[_HBM] * n,
        input_output_aliases={a: a for a in range(n)},
        compiler_params=pltpu.CompilerParams(has_side_effects=_EFFECT),
    )(*slots, send_sems, recv_sems, after)


def rs_sibling_halves(gs):
    n = len(gs)

    def body(*refs):
        g_refs, r_refs, (send_sems, recv_sems) = refs[:n], refs[n:2 * n], refs[2 * n:]
        x, y, c = _place()
        cps = [pltpu.make_async_remote_copy(
            src_ref=_row_half(g_refs[a], (slice(None), slice(None)), 1 - c), dst_ref=r_refs[a],
            send_sem=send_sems.at[a], recv_sem=recv_sems.at[a], device_id=(x, y, 1 - c), device_id_type=MESH)
            for a in range(n)]
        for cp in cps:
            cp.start()
        for cp in cps:
            cp.wait()

    return pl.pallas_call(
        body, name="rs_sibling_halves",
        out_shape=[jax.ShapeDtypeStruct((*g.shape[:2], g.shape[2] // 2, g.shape[3]), F32) for g in gs],
        in_specs=[_ANY] * n, out_specs=[_ANY] * n,
        scratch_shapes=[pltpu.SemaphoreType.DMA((n,)), pltpu.SemaphoreType.DMA((n,))],
    )(*gs)


def rs_chipsum16(g, recv1, cidx, name):
    nl, hr, cols = recv1.shape[1:]

    def body(c_ref, g_ref, r_ref, o_ref):
        o_ref[...] = (g_ref[...] + r_ref[...]).astype(BF16)

    blk = (1, 1, hr, cols)
    return pl.pallas_call(
        body, name=name, out_shape=jax.ShapeDtypeStruct(recv1.shape, BF16),
        grid_spec=pltpu.PrefetchScalarGridSpec(
            num_scalar_prefetch=1, grid=(4, nl),
            in_specs=[pl.BlockSpec(blk, lambda j, l, c: (j, l, c[0], 0)), pl.BlockSpec(blk, lambda j, l, c: (j, l, 0, 0))],
            out_specs=pl.BlockSpec(blk, lambda j, l, c: (j, l, 0, 0))),
        compiler_params=_cp("parallel", "parallel"),
    )(cidx, g, recv1)


def sibling_start(gs, name):
    n = len(gs)
    lands = [lax.empty((*g.shape[:2], g.shape[2] // 2, g.shape[3]), F32) for g in gs]

    def body(*refs):
        g_refs, l_refs, send_sems, recv_sems, token = refs[:n], refs[n:2 * n], refs[2 * n], refs[2 * n + 1], refs[-1]
        x, y, c = _place()
        for a in range(n):
            pltpu.make_async_remote_copy(
                src_ref=_row_half(g_refs[a], (slice(None), slice(None)), 1 - c), dst_ref=l_refs[a],
                send_sem=send_sems.at[a], recv_sem=recv_sems.at[a], device_id=(x, y, 1 - c), device_id_type=MESH).start()
        token[...] = jnp.zeros_like(token)

    bufs = list(gs) + lands
    out = pl.pallas_call(
        body, name=name,
        out_shape=(pltpu.SemaphoreType.DMA((n,)), pltpu.SemaphoreType.DMA((n,)),
                   *[pltpu.HBM(b.shape, b.dtype) for b in bufs], jax.ShapeDtypeStruct((8, 128), F32)),
        in_specs=[_HBM] * (2 * n), out_specs=(_SEM, _SEM, *[_HBM] * (2 * n), pl.BlockSpec(memory_space=pltpu.VMEM)),
        input_output_aliases={i: 2 + i for i in range(2 * n)},
        compiler_params=pltpu.CompilerParams(has_side_effects=_EFFECT),
    )(*[pltpu.with_memory_space_constraint(b, pltpu.HBM) for b in bufs])
    return out[0], out[1], list(out[2:2 + n]), list(out[2 + n:2 + 2 * n]), out[-1]


def sibling_wait(send_sems, recv_sems, gs, lands, after, name):
    n = len(gs)

    def body(*refs):
        g_refs, l_refs, ssem, rsem = refs[:n], refs[n:2 * n], refs[2 * n], refs[2 * n + 1]
        x, y, c = _place()
        for a in range(n):
            cp = pltpu.make_async_remote_copy(
                src_ref=_row_half(g_refs[a], (slice(None), slice(None)), 1 - c), dst_ref=l_refs[a],
                send_sem=ssem.at[a], recv_sem=rsem.at[a], device_id=(x, y, 1 - c), device_id_type=MESH)
            cp.wait_send()
            cp.wait_recv()

    bufs = list(gs) + list(lands)
    out = pl.pallas_call(
        body, name=name, out_shape=[pltpu.HBM(b.shape, b.dtype) for b in bufs],
        in_specs=[_HBM] * (2 * n) + [_SEM, _SEM, _ANY], out_specs=[_HBM] * (2 * n),
        input_output_aliases={i: i for i in range(2 * n)},
        compiler_params=pltpu.CompilerParams(has_side_effects=_EFFECT),
    )(*bufs, send_sems, recv_sems, after)
    return list(out[:n]), list(out[n:])


def exchange_start(cs, name):
    n = len(cs)
    lands = [lax.empty((3, *c.shape[1:]), BF16) for c in cs]

    def body(*refs):
        s_refs, l_refs, send_sems, recv_sems, token = refs[:n], refs[n:2 * n], refs[2 * n], refs[2 * n + 1], refs[-1]
        x, y, c = _place()
        for k, (px, py) in enumerate([(1 - x, y), (x, 1 - y), (1 - x, 1 - y)]):
            for a in range(n):
                pltpu.make_async_remote_copy(
                    src_ref=s_refs[a].at[2 * px + py], dst_ref=l_refs[a].at[k], send_sem=send_sems.at[3 * a + k],
                    recv_sem=recv_sems.at[3 * a + k], device_id=(px, py, c), device_id_type=MESH).start()
        token[...] = jnp.zeros_like(token)

    bufs = list(cs) + lands
    out = pl.pallas_call(
        body, name=name,
        out_shape=(pltpu.SemaphoreType.DMA((3 * n,)), pltpu.SemaphoreType.DMA((3 * n,)),
                   *[pltpu.HBM(b.shape, b.dtype) for b in bufs], jax.ShapeDtypeStruct((8, 128), F32)),
        in_specs=[_HBM] * (2 * n), out_specs=(_SEM, _SEM, *[_HBM] * (2 * n), pl.BlockSpec(memory_space=pltpu.VMEM)),
        input_output_aliases={i: 2 + i for i in range(2 * n)},
        compiler_params=pltpu.CompilerParams(has_side_effects=_EFFECT),
    )(*[pltpu.with_memory_space_constraint(b, pltpu.HBM) for b in bufs])
    return out[0], out[1], list(out[2:2 + n]), list(out[2 + n:2 + 2 * n]), out[-1]


def exchange_wait(send_sems, recv_sems, cs, lands, after, name):
    n = len(cs)

    def body(*refs):
        s_refs, l_refs, ssem, rsem = refs[:n], refs[n:2 * n], refs[2 * n], refs[2 * n + 1]
        x, y, c = _place()
        for k, (px, py) in enumerate([(1 - x, y), (x, 1 - y), (1 - x, 1 - y)]):
            for a in range(n):
                cp = pltpu.make_async_remote_copy(
                    src_ref=s_refs[a].at[2 * px + py], dst_ref=l_refs[a].at[k], send_sem=ssem.at[3 * a + k],
                    recv_sem=rsem.at[3 * a + k], device_id=(px, py, c), device_id_type=MESH)
                cp.wait_send()
                cp.wait_recv()

    bufs = list(cs) + list(lands)
    out = pl.pallas_call(
        body, name=name, out_shape=[pltpu.HBM(b.shape, b.dtype) for b in bufs],
        in_specs=[_HBM] * (2 * n) + [_SEM, _SEM, _ANY], out_specs=[_HBM] * (2 * n),
        input_output_aliases={i: i for i in range(2 * n)},
        compiler_params=pltpu.CompilerParams(has_side_effects=_EFFECT),
    )(*bufs, send_sems, recv_sems, after)
    return list(out[n:])


def rs_final_sum(g, recv1, recv2, idx, name):
    nl, hr, cols = recv1.shape[1:]

    def body(i_ref, g_ref, r1_ref, r2_ref, o_ref):
        acc = g_ref[0, 0] + r1_ref[0, 0]
        for k in range(3):
            acc = acc + r2_ref[k, 0].astype(F32)
        o_ref[0] = acc

    blk = (1, 1, hr, cols)
    return pl.pallas_call(
        body, name=name, out_shape=jax.ShapeDtypeStruct((nl, 2 * hr, cols), F32),
        grid_spec=pltpu.PrefetchScalarGridSpec(
            num_scalar_prefetch=1, grid=(nl,),
            in_specs=[pl.BlockSpec(blk, lambda l, ix: (ix[0], l, ix[1], 0)), pl.BlockSpec(blk, lambda l, ix: (ix[0], l, 0, 0)),
                      pl.BlockSpec((3, 1, hr, cols), lambda l, ix: (0, l, 0, 0))],
            out_specs=pl.BlockSpec((1, hr, cols), lambda l, ix: (l, ix[1], 0))),
        compiler_params=_cp("parallel"),
    )(idx, g, recv1, recv2)


def rs_share_halves(fulls):
    n = len(fulls)

    def body(*refs):
        h_refs, o_refs, (send_sems, recv_sems) = refs[:n], refs[n:2 * n], refs[2 * n:]
        x, y, c = _place()
        sibling = (x, y, 1 - c)

        def copy(a, h):
            return pltpu.make_async_remote_copy(
                src_ref=_row_half(h_refs[a], (slice(None),), h), dst_ref=_row_half(o_refs[a], (slice(None),), h),
                send_sem=send_sems.at[a], recv_sem=recv_sems.at[a], device_id=sibling, device_id_type=MESH)

        for a in range(n):
            copy(a, c).start()
        for a in range(n):
            copy(a, c).wait_send()
            copy(a, 1 - c).wait_recv()

    return pl.pallas_call(
        body, name="rs_share_halves", out_shape=[jax.ShapeDtypeStruct(f.shape, F32) for f in fulls],
        in_specs=[_ANY] * n, out_specs=[_ANY] * n, input_output_aliases={a: a for a in range(n)},
        scratch_shapes=[pltpu.SemaphoreType.DMA((n,)), pltpu.SemaphoreType.DMA((n,))],
    )(*fulls)


def _own_slot(shard, chip, dtype):
    return lax.dynamic_update_slice(lax.empty((4, *shard.shape), dtype), shard.astype(dtype)[None],
                                    (chip,) + (0,) * shard.ndim)


def kernel(x, norm_mix_pre, norm_mix_post, norm_ffn_pre, norm_ffn_post, w_in, conv_a, gate_up_fwd, gate_bias_fwd, gate_up_bwd, gate_bias_bwd, gla_head_norm, w_out, w_up, conv_ffn, w_down, loss_target, m_norm_mix_pre, m_norm_mix_post, m_norm_ffn_pre, m_norm_ffn_post, m_w_in, m_conv_a, m_gate_up_fwd, m_gate_bias_fwd, m_gate_up_bwd, m_gate_bias_bwd, m_gla_head_norm, m_w_out, m_w_up, m_conv_ffn, m_w_down, v_norm_mix_pre, v_norm_mix_post, v_norm_ffn_pre, v_norm_ffn_post, v_w_in, v_conv_a, v_gate_up_fwd, v_gate_bias_fwd, v_gate_up_bwd, v_gate_bias_bwd, v_gla_head_norm, v_w_out, v_w_up, v_conv_ffn, v_w_down):
    L = x.shape[1]
    xi, yi, ci = _place()
    chip = 2 * xi + yi
    tl_gla, tl_mix, tl_ffn = min(L, TL_GLA), min(L, TL_MIX), min(L, TL_FFN)

    big_w = (w_in, w_out, w_up, w_down)
    a_in0 = allgather_weights([_own_slot(w_in[0:1], chip, BF16)])[0][:, 0]
    started = []
    prev = (a_in0,)
    for l in range(DEPTH):
        ws = big_w[1:] if l == 0 else big_w
        started.append(gather_start([_own_slot(w[l], chip, BF16) for w in ws], f"gather_start_{l}", after=prev))
        prev = (started[-1][3],)
    tokens = [s[3] for s in started]

    def full_w_in(a_in):
        return jnp.pad(jnp.concatenate([a_in[j] for j in range(4)], axis=1), ((0, 0), (0, DINP - DIN)))

    small = jnp.concatenate([conv_a.reshape(-1), gate_up_fwd.reshape(-1), gate_up_bwd.reshape(-1), conv_ffn.reshape(-1)])
    ms = small.shape[0] // 128
    sg = allgather8(small.reshape(ms, 128), "allgather_small_weights").reshape(4, 2, ms * 128)[:, 0]

    def small_full(off, shape):
        n = shape[0] * shape[1] * shape[2]
        return jnp.concatenate([sg[j, off:off + n].reshape(shape) for j in range(4)], axis=2)

    o1 = DEPTH * 3 * 128
    o2 = o1 + DEPTH * RK * 64
    o3 = o2 + DEPTH * RK * 64
    conv_a_f = small_full(0, (DEPTH, 3, 128))
    gup_f = small_full(o1, (DEPTH, RK, 64))
    gup_b = small_full(o2, (DEPTH, RK, 64))
    conv_ffn_f = small_full(o3, (DEPTH, 3, 1408))

    def gcat_of(l):
        g = jnp.zeros((LRW, 2 * DK), F32)
        g = g.at[0:RK, 0:DK].set(gup_f[l]).at[RK:2 * RK, DK:2 * DK].set(gup_b[l])
        return g.astype(BF16)

    gcats = [gcat_of(l) for l in range(DEPTH)]
    gbiases = [jnp.concatenate([gate_bias_fwd[l], gate_bias_bwd[l]])[None, :] for l in range(DEPTH)]
    ghn4s = [jnp.tile(gla_head_norm[l], NH)[None, :] for l in range(DEPTH)]

    xc = x.reshape(L, D)
    saved = []
    W_in, W_out, W_up, W_down = [], [], [], []
    tl_row = min(L, TL_ROW)
    for l in range(DEPTH):
        ssem, rsem, bufs, _ = started[l]
        if l > 0:
            a_in, a_out, a_up, a_down = gather_wait(ssem, rsem, bufs, xc, f"gather_wait_{l}")
        W_in.append(full_w_in(a_in0 if l == 0 else a_in))
        P, h1 = rms_matmul(xc, norm_mix_pre[l][None, :], W_in[l], 640, "proj_in", out_dtype=BF16,
                           after=tokens if l == 0 else ())
        o_f, o_b, sf, sb = gla_fwd(P, gcats[l], gbiases[l], tl_gla)
        if l == 0:
            a_out, a_up, a_down = gather_wait(ssem, rsem, bufs, o_f, "gather_wait_0")
        W_out.append(a_out.reshape(D, D))
        W_up.append(a_up)
        W_down.append(a_down.reshape(DFF, D))
        ycat, y, x1 = mix_out(P, o_f, o_b, conv_a_f[l], ghn4s[l], W_out[l], norm_mix_post[l][None, :], xc,
                              min(L, TL_MIX_OUT))
        U, h2 = rms_matmul(x1, norm_ffn_pre[l][None, :], W_up[l], WFF, "proj_up", n_out=2 * DFF, out_dtype=BF16,
                           w_spec=pl.BlockSpec((None, D, WFF), lambda i, j: (j, 0, 0)))
        y2, x2, ug, uv, z = ffn_down(U, conv_ffn_f[l], W_down[l], norm_ffn_post[l][None, :], x1, tl_ffn)
        saved.append(dict(x=xc, h1=h1, P=P, o_f=o_f, o_b=o_b, sf=sf, sb=sb, ycat=ycat, y=y, x1=x1, h2=h2, U=U, y2=y2,
                          ug=ug, uv=uv, z=z))
        xc = x2

    dx, loss_blk = loss_head(xc, loss_target.reshape(L, D), tl_row)

    big = ("w_in", "w_out", "w_up", "w_down")
    cidx = jnp.reshape(ci, (1,)).astype(jnp.int32)
    idx = jnp.stack([chip, ci]).astype(jnp.int32)
    grads = [None] * DEPTH
    reduced = [None] * DEPTH
    tl_dw = min(L, 1024)
    state = dict(flight=None, sibling=None)
    token = ()

    def finish(pend, after):
        lp, gs_p, recv1_p, (ssem, rsem, cs_thru, lands, _) = pend
        recv2 = exchange_wait(ssem, rsem, cs_thru, lands, after, f"exchange_wait_{lp}")
        halves = [rs_final_sum(g, r1, r2, idx, "rs_final_sum_" + k) for g, r1, r2, k in zip(gs_p, recv1_p, recv2, big)]
        reduced[lp] = rs_share_halves(halves)

    def advance(after):
        ls, (ssem, rsem, gs_thru, lands, _) = state["sibling"]
        gs_s, recv1 = sibling_wait(ssem, rsem, gs_thru, lands, after, f"sibling_wait_{ls}")
        cs16 = [rs_chipsum16(g, r, cidx, "rs_chipsum16_" + k) for g, r, k in zip(gs_s, recv1, big)]
        flight = exchange_start(cs16, f"exchange_start_{ls}")
        if state["flight"] is not None:
            finish(state["flight"], flight[4])
        state["flight"] = (ls, gs_s, recv1, flight)
        return flight[4]

    for l in reversed(range(DEPTH)):
        s = saved[l]
        dy2, dg4 = rms_bwd_pre(dx, s["y2"], norm_ffn_post[l][None, :], tl_row, after=token)
        du_g, du_v = ffn_bwd1(dy2, s["ug"], s["uv"], W_down[l], min(L, TL_FFN2))
        token2 = (advance(du_g),) if state["sibling"] is not None else ()
        g_down = matmul_tn(s["z"], dy2, DFF // 2, D, tl_dw, "dw_down").reshape(4, 1, DFF // 4, D)
        dU_g, dU_v, dx1, dg3, dcf_g, dcf_v = ffn_bwd2(du_g, du_v, s["U"], conv_ffn_f[l], W_up[l], s["x1"], dx,
                                                      norm_ffn_pre[l][None, :],
                                        min(L, TL_FFN2))
        g_up = matmul_tn(s["h2"], dU_g, D, WFF, tl_dw, "dw_up_gate",
                         into=(lax.empty((4, 1, D, WFF), F32), (None, None, D, WFF), lambda p, q: (q, 0, 0, 0)))
        g_up = matmul_tn(s["h2"], dU_v, D, WFF, tl_dw, "dw_up_val",
                         into=(g_up, (None, None, D, WFF), lambda p, q: (NFF + q, 0, 0, 0)))
        dy, dg2 = rms_bwd_pre(dx1, s["y"], norm_mix_post[l][None, :], tl_row, after=token2)
        dgb, dcc, dgo, do, dca, dghn = mix_bwd1(dy, W_out[l], s["P"], s["o_f"], s["o_b"], conv_a_f[l], ghn4s[l],
                                                min(L, TL_MIX_OUT))
        g_out = matmul_tn(s["ycat"], dy, D, D, tl_dw, "dw_out").reshape(4, 1, D // 4, D)
        gl = gla_bwd(s["P"], do, s["sf"], s["sb"], gcats[l], gbiases[l], tl_gla)
        dP, dx, dg1, dgcat, dbias = mix_bwd2(dgb, dcc, dgo, gl, s["P"], conv_a_f[l], gcats[l], gbiases[l], W_in[l],
                                             s["x"], dx1, norm_mix_pre[l][None, :], tl_mix)
        dW_in = matmul_tn(s["h1"], dP, D, 640, tl_dw, "dw_in")
        g_in = jnp.stack([dW_in[:, (DIN // 4) * j:(DIN // 4) * (j + 1)] for j in range(4)])[:, None]
        grads[l] = dict(
            norm_mix_pre=dg1[0], norm_mix_post=dg2[0], norm_ffn_pre=dg3[0], norm_ffn_post=dg4[0],
            conv_a=dca[0:3], gate_up_fwd=dgcat[0:RK, 0:DK], gate_bias_fwd=dbias[0, 0:DK],
            gate_up_bwd=dgcat[RK:2 * RK, DK:2 * DK], gate_bias_bwd=dbias[0, DK:2 * DK], gla_head_norm=dghn[0],
            conv_ffn=jnp.concatenate([dcf_g[j, 0:3] for j in range(NFF)] + [dcf_v[j, 0:3] for j in range(NFF)], axis=1))
        sib = sibling_start([g_in, g_out, g_up, g_down], f"sibling_start_{l}")
        state["sibling"] = (l, sib)
        token = (sib[4],)
    last = advance(token[0])
    finish(state["flight"], last)

    G = {k: jnp.stack([grads[l][k] for l in range(DEPTH)]) for k in grads[0]}

    small_names = ["norm_mix_pre", "norm_mix_post", "norm_ffn_pre", "norm_ffn_post", "conv_a", "gate_up_fwd",
                   "gate_bias_fwd", "gate_up_bwd", "gate_bias_bwd", "gla_head_norm", "conv_ffn"]
    flat = jnp.concatenate([G[k].reshape(-1) for k in small_names] + [loss_blk[0, 0:1]])
    n_small = flat.shape[0]
    mp = -(-n_small // 1024) * 8
    flat = jnp.pad(flat, (0, mp * 128 - n_small)).reshape(mp, 128)
    tot = sum8(allgather8(flat, "allgather_small_grads"), mp).reshape(-1)
    gsm = {}
    o = 0
    for k in small_names:
        n = G[k].size
        gsm[k] = tot[o:o + n].reshape(G[k].shape)
        o += n
    loss = tot[o]

    def my_cols(a, width):
        return lax.dynamic_slice_in_dim(a, chip * width, width, axis=2)

    gsm["conv_a"] = my_cols(gsm["conv_a"], 128)
    gsm["gate_up_fwd"] = my_cols(gsm["gate_up_fwd"], 64)
    gsm["gate_up_bwd"] = my_cols(gsm["gate_up_bwd"], 64)
    gsm["conv_ffn"] = my_cols(gsm["conv_ffn"], 1408)

    for a, k in enumerate(big):
        gsm[k] = jnp.concatenate([reduced[l][a] for l in range(DEPTH)], axis=0)

    names = ["norm_mix_pre", "norm_mix_post", "norm_ffn_pre", "norm_ffn_post", "w_in", "conv_a", "gate_up_fwd",
             "gate_bias_fwd", "gate_up_bwd", "gate_bias_bwd", "gla_head_norm", "w_out", "w_up", "conv_ffn", "w_down"]
    w = dict(norm_mix_pre=norm_mix_pre, norm_mix_post=norm_mix_post, norm_ffn_pre=norm_ffn_pre, norm_ffn_post=norm_ffn_post,
             w_in=w_in, conv_a=conv_a, gate_up_fwd=gate_up_fwd, gate_bias_fwd=gate_bias_fwd, gate_up_bwd=gate_up_bwd,
             gate_bias_bwd=gate_bias_bwd, gla_head_norm=gla_head_norm, w_out=w_out, w_up=w_up, conv_ffn=conv_ffn, w_down=w_down)
    m = dict(norm_mix_pre=m_norm_mix_pre, norm_mix_post=m_norm_mix_post, norm_ffn_pre=m_norm_ffn_pre, norm_ffn_post=m_norm_ffn_post,
             w_in=m_w_in, conv_a=m_conv_a, gate_up_fwd=m_gate_up_fwd, gate_bias_fwd=m_gate_bias_fwd, gate_up_bwd=m_gate_up_bwd,
             gate_bias_bwd=m_gate_bias_bwd, gla_head_norm=m_gla_head_norm, w_out=m_w_out, w_up=m_w_up, conv_ffn=m_conv_ffn, w_down=m_w_down)
    v = dict(norm_mix_pre=v_norm_mix_pre, norm_mix_post=v_norm_mix_post, norm_ffn_pre=v_norm_ffn_pre, norm_ffn_post=v_norm_ffn_post,
             w_in=v_w_in, conv_a=v_conv_a, gate_up_fwd=v_gate_up_fwd, gate_bias_fwd=v_gate_bias_fwd, gate_up_bwd=v_gate_up_bwd,
             gate_bias_bwd=v_gate_bias_bwd, gla_head_norm=v_gla_head_norm, w_out=v_w_out, w_up=v_w_up, conv_ffn=v_conv_ffn, w_down=v_w_down)
    upd = {k: adamw(w[k], gsm[k], m[k], v[k], "adamw_" + k) for k in names}
    return (loss, dx.reshape(1, L, D), *[gsm[k] for k in names], *[upd[k][0] for k in names],
            *[upd[k][1] for k in names], *[upd[k][2] for k in names])
```

```python
import functools

import jax
import jax.numpy as jnp
from jax import lax
from jax.experimental import pallas as pl
from jax.experimental.pallas import tpu as pltpu

F32 = jnp.float32
BF16 = jnp.bfloat16
MXU_DTYPE = jnp.bfloat16
MESH = pl.DeviceIdType.MESH

D = 1024
DC = 512
DG = 512
NH = 4
HV = 128
HK = 64
DK = 256
RK = 16
CH = 64
DFF = 2816
DIN = 3104
DINP = 3200
LRW = 128
DEPTH = 4
EPS = 1e-6
QSCALE = HK ** -0.5
GATE_NORM = 1.0 / 16.0
CB_GB, CB_GC, CB_GV, CB_GO = 0, 1, 2, 5
CB_Q, CB_K = 6, 7
CB_V = 4
CB_LR = 24
LR = 0.001
B1 = 0.9
B2 = 0.999
AEPS = 1e-08
WD = 0.01
STEP = 10
TM_PROJ = 1024
TL_GLA = 512
TL_MIX = 256
TL_MIX_OUT = 512
TL_ROW = 1024
TL_FFN = 256
TL_FFN2 = 512
VMEM_LIMIT = 56 * 1024 * 1024


def _cp(*sem):
    return pltpu.CompilerParams(dimension_semantics=sem if sem else None, vmem_limit_bytes=VMEM_LIMIT)


def _mm(a, b):
    return jnp.dot(a.astype(MXU_DTYPE), b.astype(MXU_DTYPE), preferred_element_type=F32)


def _mm_nt(a, b):
    return lax.dot_general(a.astype(MXU_DTYPE), b.astype(MXU_DTYPE), (((1,), (1,)), ((), ())),
                           preferred_element_type=F32)


def _mm_tn(a, b):
    return lax.dot_general(a.astype(MXU_DTYPE), b.astype(MXU_DTYPE), (((0,), (0,)), ((), ())),
                           preferred_element_type=F32)


def _mm_tri(tri, b):
    t = tri.astype(BF16)
    b1 = b.astype(BF16)
    r1 = b - b1.astype(F32)
    b2 = r1.astype(BF16)
    b3 = (r1 - b2.astype(F32)).astype(BF16)
    dot = lambda u: jnp.dot(t, u, preferred_element_type=F32)
    return dot(b1) + dot(b2) + dot(b3)


def _rms(x, g):
    r = lax.rsqrt(jnp.mean(x * x, axis=-1, keepdims=True) + EPS)
    return x * r * g


def _rms_bwd(dout, y, g):
    r = lax.rsqrt(jnp.mean(y * y, axis=-1, keepdims=True) + EPS)
    yh = y * r
    dyh = dout * g
    dy = r * (dyh - yh * jnp.mean(dyh * yh, axis=-1, keepdims=True))
    dg = jnp.sum(dout * yh, axis=0, keepdims=True)
    return dy, dg


def _sigmoid(x):
    return 0.5 * jnp.tanh(0.5 * x) + 0.5


def _logsig(x):
    return jnp.minimum(x, 0.0) - jnp.log1p(jnp.exp(-jnp.abs(x)))


def _shifts(x, p8, n8):
    n = x.shape[0]
    xe = jnp.concatenate([p8, x, n8], axis=0)
    return pltpu.roll(xe, 1, 0)[8:8 + n], pltpu.roll(xe, n + 15, 0)[8:8 + n]


def _halo_rows(prev_ref, next_ref, i, last):
    hr = prev_ref.shape[0]
    p = jnp.where(i == 0, 0.0, prev_ref[...].astype(F32)[hr - 8:hr, :])
    n = jnp.where(i == last, 0.0, next_ref[...].astype(F32)[0:8, :])
    return p, n


def _conv3(x, xp, xn, w_ref):
    xm1, xp1 = _shifts(x, xp, xn)
    return w_ref[0:1, :] * xm1 + w_ref[1:2, :] * x + w_ref[2:3, :] * xp1, xm1, xp1


def _conv3_t(d, dp, dn, w_ref):
    dm1, dp1 = _shifts(d, dp, dn)
    return w_ref[0:1, :] * dp1 + w_ref[1:2, :] * d + w_ref[2:3, :] * dm1


HALO32 = 8
HALO16 = 16


def _prev_row_blk(i, tl, hr):
    return jnp.maximum(i * (tl // hr) - 1, 0)


def _next_row_blk(i, tl, nrows, hr):
    return jnp.minimum((i + 1) * (tl // hr), nrows // hr - 1)


def _prev_blk(tl, cb, hr=HALO32):
    return lambda i: (_prev_row_blk(i, tl, hr), cb)


def _next_blk(tl, nrows, cb, hr=HALO32):
    return lambda i: (_next_row_blk(i, tl, nrows, hr), cb)


def rms_matmul(x, g, w, tn, name, w_spec=None, n_out=None, out_dtype=F32, after=(), tm=TM_PROJ):
    L = x.shape[0]
    N = w.shape[1] if n_out is None else n_out
    tm = min(L, tm)
    if w_spec is None:
        w_spec = pl.BlockSpec((D, tn), lambda i, j: (0, j))

    def body(x_ref, g_ref, w_ref, *rest):
        o_ref, h_ref = rest[-2:]

        @pl.when(pl.program_id(1) == 0)
        def _():
            h_ref[...] = _rms(x_ref[...], g_ref[...]).astype(BF16)

        o_ref[...] = _mm(h_ref[...], w_ref[...]).astype(out_dtype)

    return pl.pallas_call(
        body, name=name, grid=(L // tm, N // tn),
        in_specs=[pl.BlockSpec((tm, D), lambda i, j: (i, 0)), pl.BlockSpec((1, D), lambda i, j: (0, 0)), w_spec]
        + [_ANY] * len(after),
        out_specs=[pl.BlockSpec((tm, tn), lambda i, j: (i, j)), pl.BlockSpec((tm, D), lambda i, j: (i, 0))],
        out_shape=[jax.ShapeDtypeStruct((L, N), out_dtype), jax.ShapeDtypeStruct((L, D), BF16)],
        compiler_params=_cp("parallel", "arbitrary"),
    )(x, g, w, *after)


def _gla_masks():
    def blk(shape, rdiv, cdiv):
        r = lax.broadcasted_iota(jnp.int32, shape, 0) // rdiv
        c = lax.broadcasted_iota(jnp.int32, shape, 1) // cdiv
        return (r == c).astype(F32)

    r = lax.broadcasted_iota(jnp.int32, (CH, CH), 0)
    c = lax.broadcasted_iota(jnp.int32, (CH, CH), 1)
    r4 = lax.broadcasted_iota(jnp.int32, (NH * CH, CH), 0) % CH
    c4 = lax.broadcasted_iota(jnp.int32, (NH * CH, CH), 1)
    return dict(
        bdq=blk((NH * CH, DK), CH, HK),
        bdo=blk((NH * CH, DG), CH, HV),
        bds=blk((DG, DK), HV, HK),
        tril=(r >= c).astype(F32), triu=(r <= c).astype(F32),
        tril4=r4 >= c4, triu4=r4 <= c4,
    )


def _tile4(x):
    return jnp.concatenate([x, x, x, x], axis=0)


def _gla_tile_prep(q, k, a, m, rev, nc):
    tri = m["triu"] if rev else m["tril"]
    chunks = [a[c * CH:(c + 1) * CH] for c in range(nc)]
    cum = jnp.concatenate([_mm_tri(tri, ac) for ac in chunks], axis=0)
    tot = jnp.concatenate([jnp.sum(ac, axis=0, keepdims=True) for ac in chunks], axis=0)
    tot_rows = jnp.concatenate([jnp.broadcast_to(tot[c:c + 1], (CH, DK)) for c in range(nc)], axis=0)
    e = jnp.exp(cum)
    einv = jnp.exp(-cum)
    eout = jnp.exp(tot_rows - cum)
    q, k = q.astype(F32), k.astype(F32)
    return dict(e=e, einv=einv, eout=eout, dec=jnp.exp(tot), qt=q * QSCALE * e, kt=k * einv, kh=k * eout)


def _gla_scores(qt16, kt16, m, rev):
    qs = _tile4(qt16) * m["bdq"].astype(qt16.dtype)
    return qs, jnp.where(m["triu4"] if rev else m["tril4"], _mm_nt(qs, kt16), 0.0)


def _gla_chunk_fwd(qt16, kt16, kh16, v, dec, st_ref, m, rev):
    _, sc = _gla_scores(qt16, kt16, m, rev)
    v16 = v.astype(BF16)
    r = _mm(sc, v16)
    o_intra = jnp.concatenate([r[h * CH:(h + 1) * CH, h * HV:(h + 1) * HV] for h in range(NH)], axis=1)
    st = st_ref[...]
    st16 = st.astype(BF16)
    o = o_intra + _mm_nt(qt16, st16)
    st_ref[...] = st * dec + _mm_tn(v16, kh16) * m["bds"]
    return o, st16


def _gates(lr_ref, gc_ref, bs_ref, cols):
    return _logsig(_mm(lr_ref[...], gc_ref[:, cols]) + bs_ref[:, cols]) * GATE_NORM


def gla_fwd(P, gcat, gbias, tl):
    L = P.shape[0]
    nb = L // tl
    nc = tl // CH

    def body(qf, kf, vf, lf, qb, kb, vb, lb, gc_ref, bs_ref, of, ob, sf, sb, stf, stb,
             qtf, ktf, khf, dcf, qtb, ktb, khb, dcb):
        @pl.when(pl.program_id(0) == 0)
        def _():
            stf[...] = jnp.zeros_like(stf)
            stb[...] = jnp.zeros_like(stb)

        m = _gla_masks()
        for (q, k, lr, cols, rev, qt, kt, kh, dc) in ((qf, kf, lf, slice(0, DK), False, qtf, ktf, khf, dcf),
                                                      (qb, kb, lb, slice(DK, 2 * DK), True, qtb, ktb, khb, dcb)):
            p = _gla_tile_prep(q[...], k[...], _gates(lr, gc_ref, bs_ref, cols), m, rev, nc)
            qt[...] = p["qt"].astype(BF16)
            kt[...] = p["kt"].astype(BF16)
            kh[...] = p["kh"].astype(BF16)
            dc[...] = p["dec"]

        def chunk(c, carry):
            rows = pl.ds(pl.multiple_of(c * CH, CH), CH)
            o, st = _gla_chunk_fwd(qtf[rows, :], ktf[rows, :], khf[rows, :], vf[rows, :], dcf[pl.ds(c, 1), :], stf, m, False)
            of[rows, :] = o.astype(BF16)
            sf[c] = st
            cb = nc - 1 - c
            rows = pl.ds(pl.multiple_of(cb * CH, CH), CH)
            o, st = _gla_chunk_fwd(qtb[rows, :], ktb[rows, :], khb[rows, :], vb[rows, :], dcb[pl.ds(cb, 1), :], stb, m, True)
            ob[rows, :] = o.astype(BF16)
            sb[cb] = st
            return carry

        lax.fori_loop(0, nc, chunk, 0, unroll=2)

    fw = lambda cb: (lambda i: (i, cb))
    bw = lambda cb: (lambda i: (nb - 1 - i, cb))
    return pl.pallas_call(
        body, name="gla_fwd", grid=(nb,),
        in_specs=[pl.BlockSpec((tl, DK), fw(CB_Q)), pl.BlockSpec((tl, DK), fw(CB_K)), pl.BlockSpec((tl, DG), fw(CB_V)),
                  pl.BlockSpec((tl, LRW), fw(CB_LR)),
                  pl.BlockSpec((tl, DK), bw(CB_Q)), pl.BlockSpec((tl, DK), bw(CB_K)), pl.BlockSpec((tl, DG), bw(CB_V)),
                  pl.BlockSpec((tl, LRW), bw(CB_LR)),
                  pl.BlockSpec((LRW, 2 * DK), lambda i: (0, 0)), pl.BlockSpec((1, 2 * DK), lambda i: (0, 0))],
        out_specs=[pl.BlockSpec((tl, DG), lambda i: (i, 0)), pl.BlockSpec((tl, DG), lambda i: (nb - 1 - i, 0)),
                   pl.BlockSpec((nc, DG, DK), lambda i: (i, 0, 0)), pl.BlockSpec((nc, DG, DK), lambda i: (nb - 1 - i, 0, 0))],
        out_shape=[jax.ShapeDtypeStruct((L, DG), BF16), jax.ShapeDtypeStruct((L, DG), BF16),
                   jax.ShapeDtypeStruct((L // CH, DG, DK), BF16), jax.ShapeDtypeStruct((L // CH, DG, DK), BF16)],
        scratch_shapes=[pltpu.VMEM((DG, DK), F32), pltpu.VMEM((DG, DK), F32)]
        + [pltpu.VMEM((tl, DK), BF16)] * 3 + [pltpu.VMEM((nc, DK), F32)]
        + [pltpu.VMEM((tl, DK), BF16)] * 3 + [pltpu.VMEM((nc, DK), F32)],
        compiler_params=_cp("arbitrary"),
    )(P, P, P, P, P, P, P, P, gcat, gbias)


def _headnorm(o):
    oh, rs = [], []
    for h in range(NH):
        oo = o[:, h * HV:(h + 1) * HV]
        r = lax.rsqrt(jnp.mean(oo * oo, axis=-1, keepdims=True) + EPS)
        oh.append(oo * r)
        rs.append(r)
    return jnp.concatenate(oh, axis=1), rs


def mix_out(P, o_f, o_b, conv_a, ghn4, w_out, g2, x, tl):
    L = P.shape[0]
    nt = L // tl

    def body(gb, gc, gv, go, gcp, gvp, gcn, gvn, of, ob, ca, gh, wo, g2r, xr, ycat, yr, x1):
        i = pl.program_id(0)
        cp, cn = _halo_rows(gcp, gcn, i, nt - 1)
        vp, vn = _halo_rows(gvp, gvn, i, nt - 1)
        c = gc[...].astype(F32) * gv[...].astype(F32)
        cc, _, _ = _conv3(c, cp * vp, cn * vn, ca)
        ya = gb[...].astype(F32) * cc
        oh, _ = _headnorm(of[...].astype(F32) + ob[...].astype(F32))
        g = go[...].astype(F32)
        yb = g * _sigmoid(g) * (oh * gh[...])
        yc = jnp.concatenate([ya, yb], axis=1).astype(BF16)
        ycat[...] = yc
        y = _mm(yc, wo[...])
        yr[...] = y
        x1[...] = xr[...] + _rms(y, g2r[...])

    t = lambda cb: pl.BlockSpec((tl, DC), lambda i: (i, cb))
    hp = lambda cb: pl.BlockSpec((HALO16, DC), _prev_blk(tl, cb, HALO16))
    hn = lambda cb: pl.BlockSpec((HALO16, DC), _next_blk(tl, L, cb, HALO16))
    row = lambda n: pl.BlockSpec((tl, n), lambda i: (i, 0))
    full = lambda a: pl.BlockSpec(a.shape, lambda i: (0, 0))
    return pl.pallas_call(
        body, name="mix_out", grid=(nt,),
        in_specs=[t(CB_GB), t(CB_GC), t(CB_GV), t(CB_GO), hp(CB_GC), hp(CB_GV), hn(CB_GC), hn(CB_GV),
                  row(DG), row(DG), full(conv_a), full(ghn4), full(w_out), full(g2), row(D)],
        out_specs=[row(D), row(D), row(D)],
        out_shape=[jax.ShapeDtypeStruct((L, D), BF16), jax.ShapeDtypeStruct((L, D), F32),
                   jax.ShapeDtypeStruct((L, D), F32)],
        compiler_params=_cp("parallel"),
    )(P, P, P, P, P, P, P, P, o_f, o_b, conv_a, ghn4, w_out, g2, x)


NFF = 2
WFF = DFF // NFF
FFN_COL_CHUNKS = ((0, 512), (512, 1024), (1024, WFF))


def ffn_down(U, conv_ffn, w_down, g4, x1, tl):
    L = U.shape[0]
    nt = L // tl

    def body(u, up, un, cf, wd, g4r, x1r, y2, x2, ug, uv, zr):
        i = pl.program_id(0)
        acc = jnp.zeros((tl, D), F32)
        for j in range(NFF):
            gs = slice(j * WFF, (j + 1) * WFF)
            vs = slice(DFF + j * WFF, DFF + (j + 1) * WFF)
            z = []
            for s in (gs, vs):
                p, n = _halo_rows(up.at[:, s], un.at[:, s], i, nt - 1)
                z.append(_conv3(u[:, s].astype(F32), p, n, cf.at[:, s])[0])
            zz = (z[0] * _sigmoid(z[0]) * z[1]).astype(BF16)
            ug[:, gs] = z[0].astype(BF16)
            uv[:, gs] = z[1].astype(BF16)
            zr[:, gs] = zz
            acc = acc + _mm(zz, wd[gs, :])
        y2[...] = acc
        x2[...] = x1r[...] + _rms(acc, g4r[...])

    row = lambda n: pl.BlockSpec((tl, n), lambda i: (i, 0))
    full = lambda a: pl.BlockSpec(a.shape, lambda i: (0, 0))
    half = jax.ShapeDtypeStruct((L, DFF), BF16)
    return pl.pallas_call(
        body, name="ffn_down", grid=(nt,),
        in_specs=[row(2 * DFF), pl.BlockSpec((HALO16, 2 * DFF), _prev_blk(tl, 0, HALO16)),
                  pl.BlockSpec((HALO16, 2 * DFF), _next_blk(tl, L, 0, HALO16)),
                  full(conv_ffn), full(w_down), full(g4), row(D)],
        out_specs=[row(D), row(D), row(DFF), row(DFF), row(DFF)],
        out_shape=[jax.ShapeDtypeStruct((L, D), F32), jax.ShapeDtypeStruct((L, D), F32), half, half, half],
        compiler_params=_cp("parallel"),
    )(U, U, U, conv_ffn, w_down, g4, x1)


def loss_head(y, target, tl):
    L = y.shape[0]

    def body(yr, tr, dy, ls):
        @pl.when(pl.program_id(0) == 0)
        def _():
            ls[...] = jnp.zeros_like(ls)

        err = yr[...] - tr[...]
        dy[...] = err * (1.0 / D)
        ls[...] += (0.5 / D) * jnp.sum(err * err)

    row = pl.BlockSpec((tl, D), lambda i: (i, 0))
    return pl.pallas_call(
        body, name="loss_head", grid=(L // tl,), in_specs=[row, row],
        out_specs=[row, pl.BlockSpec((8, 128), lambda i: (0, 0))],
        out_shape=[jax.ShapeDtypeStruct((L, D), F32), jax.ShapeDtypeStruct((8, 128), F32)],
        compiler_params=_cp("arbitrary"),
    )(y, target)


def rms_bwd_pre(dout, y, g, tl, after=()):
    L = y.shape[0]

    def body(dr, yr, gr, *rest):
        dy, dg = rest[-2:]

        @pl.when(pl.program_id(0) == 0)
        def _():
            dg[...] = jnp.zeros_like(dg)

        a, b = _rms_bwd(dr[...], yr[...], gr[...])
        dy[...] = a.astype(BF16)
        dg[...] += b

    row = pl.BlockSpec((tl, D), lambda i: (i, 0))
    vec = pl.BlockSpec((1, D), lambda i: (0, 0))
    return pl.pallas_call(
        body, name="rms_bwd_pre", grid=(L // tl,), in_specs=[row, row, vec] + [_ANY] * len(after), out_specs=[row, vec],
        out_shape=[jax.ShapeDtypeStruct((L, D), BF16), jax.ShapeDtypeStruct((1, D), F32)],
        compiler_params=_cp("arbitrary"),
    )(dout, y, g, *after)


def ffn_bwd1(dy2, ug, uv, w_down, tl):
    L = ug.shape[0]

    def body(dy, ugr, uvr, wd, dug, duv):
        a = ugr[...].astype(F32)
        b = uvr[...].astype(F32)
        sg = _sigmoid(a)
        silu = a * sg
        dz = _mm_nt(dy[...], wd[...])
        dug[...] = (dz * b * (sg + silu * (1.0 - sg))).astype(BF16)
        duv[...] = (dz * silu).astype(BF16)

    tile = pl.BlockSpec((tl, WFF), lambda j, i: (i, j))
    half = jax.ShapeDtypeStruct((L, DFF), BF16)
    return pl.pallas_call(
        body, name="ffn_bwd1", grid=(NFF, L // tl),
        in_specs=[pl.BlockSpec((tl, D), lambda j, i: (i, 0)), tile, tile, pl.BlockSpec((WFF, D), lambda j, i: (j, 0))],
        out_specs=[tile, tile], out_shape=[half, half],
        compiler_params=_cp("parallel", "parallel"),
    )(dy2, ug, uv, w_down)


def ffn_bwd2(du_g, du_v, U, conv_ffn, w_up, x1, dres, g3, tl):
    L = x1.shape[0]
    nt = L // tl

    def body(dg_, dv_, dgp, dgn, dvp, dvn, ugr, uvr, cg, cv, wg, wv, x1r, drr, g3r, dUg, dUv, dx1, dg3, dcg, dcv, acc):
        i = pl.program_id(0)
        j = pl.program_id(1)

        @pl.when((i == 0) & (j == 0))
        def _():
            dg3[...] = jnp.zeros_like(dg3)
            dcg[...] = jnp.zeros_like(dcg)
            dcv[...] = jnp.zeros_like(dcv)

        part = None
        for c0, c1 in FFN_COL_CHUNKS:
            cs = slice(c0, c1)
            for d_ref, dp_ref, dn_ref, cw_ref, u_ref, dc, dU, w in ((dg_, dgp, dgn, cg, ugr, dcg, dUg, wg),
                                                                    (dv_, dvp, dvn, cv, uvr, dcv, dUv, wv)):
                p8, n8 = _halo_rows(dp_ref.at[:, cs], dn_ref.at[:, cs], i, nt - 1)
                d = d_ref[:, cs].astype(F32)
                dm1, dp1 = _shifts(d, p8, n8)
                du = (cw_ref[0:1, cs] * dp1 + cw_ref[1:2, cs] * d + cw_ref[2:3, cs] * dm1).astype(BF16)
                dU[:, cs] = du
                u = u_ref[:, cs].astype(F32)
                for k, t in enumerate((dp1, d, dm1)):
                    dc[j, k:k + 1, cs] += jnp.sum(t * u, axis=0, keepdims=True)
                term = _mm_nt(du, w[:, cs])
                part = term if part is None else part + term

        @pl.when(j == 0)
        def _():
            acc[...] = part

        @pl.when(j > 0)
        def _():
            acc[...] += part

        @pl.when(j == NFF - 1)
        def _():
            dx, dg = _rms_bwd(acc[...], x1r[...], g3r[...])
            dx1[...] = drr[...] + dx
            dg3[...] += dg

    tile = pl.BlockSpec((tl, WFF), lambda i, j: (i, j))
    prev = pl.BlockSpec((HALO16, WFF), lambda i, j: (_prev_row_blk(i, tl, HALO16), j))
    nxt = pl.BlockSpec((HALO16, WFF), lambda i, j: (_next_row_blk(i, tl, L, HALO16), j))
    cw = lambda off: pl.BlockSpec((3, WFF), lambda i, j: (0, off + j))
    ww = lambda off: pl.BlockSpec((None, D, WFF), lambda i, j: (off + j, 0, 0))
    row = pl.BlockSpec((tl, D), lambda i, j: (i, 0))
    vec = pl.BlockSpec((1, D), lambda i, j: (0, 0))
    ut = lambda off: pl.BlockSpec((tl, WFF), lambda i, j: (i, off + j))
    dcs = pl.BlockSpec((NFF, 8, WFF), lambda i, j: (0, 0, 0))
    return pl.pallas_call(
        body, name="ffn_bwd2", grid=(nt, NFF),
        in_specs=[tile, tile, prev, nxt, prev, nxt, ut(0), ut(NFF), cw(0), cw(NFF), ww(0), ww(NFF), row, row, vec],
        out_specs=[tile, tile, row, vec, dcs, dcs],
        out_shape=[jax.ShapeDtypeStruct((L, DFF), BF16), jax.ShapeDtypeStruct((L, DFF), BF16),
                   jax.ShapeDtypeStruct((L, D), F32), jax.ShapeDtypeStruct((1, D), F32),
                   jax.ShapeDtypeStruct((NFF, 8, WFF), F32), jax.ShapeDtypeStruct((NFF, 8, WFF), F32)],
        scratch_shapes=[pltpu.VMEM((tl, D), F32)],
        compiler_params=_cp("arbitrary", "arbitrary"),
    )(du_g, du_v, du_g, du_g, du_v, du_v, U, U, conv_ffn, conv_ffn, w_up, w_up, x1, dres, g3)


def matmul_tn(a, b, ta, tn, tl, name, into=None, after=()):
    L, Ka = a.shape
    N = b.shape[1]

    def body(ar, br, *rest):
        o = rest[-1]

        @pl.when(pl.program_id(2) == 0)
        def _():
            o[...] = jnp.zeros_like(o)

        o[...] += _mm_tn(ar[...], br[...]).reshape(o.shape)

    in_specs = [pl.BlockSpec((tl, ta), lambda p, q, l: (l, p)), pl.BlockSpec((tl, tn), lambda p, q, l: (l, q))]
    if into is None:
        return pl.pallas_call(
            body, name=name, grid=(Ka // ta, N // tn, L // tl), in_specs=in_specs + [_ANY] * len(after),
            out_specs=pl.BlockSpec((ta, tn), lambda p, q, l: (p, q)),
            out_shape=jax.ShapeDtypeStruct((Ka, N), F32),
            compiler_params=_cp("parallel", "parallel", "arbitrary"),
        )(a, b, *after)
    buf, blk, idx = into
    return pl.pallas_call(
        body, name=name, grid=(Ka // ta, N // tn, L // tl), in_specs=in_specs + [_ANY],
        out_specs=pl.BlockSpec(blk, lambda p, q, l: idx(p, q)),
        out_shape=jax.ShapeDtypeStruct(buf.shape, F32), input_output_aliases={2: 0},
        compiler_params=_cp("parallel", "parallel", "arbitrary"),
    )(a, b, buf)


def mix_bwd1(dy, w_out, P, o_f, o_b, conv_a, ghn4, tl):
    L = P.shape[0]
    nt = L // tl

    def body(dyr, wo, gb, gc, gv, go, gcp, gvp, gcn, gvn, of, ob, ca, gh, dgb, dcc, dgo, do, dca, dgh):
        i = pl.program_id(0)

        @pl.when(i == 0)
        def _():
            dca[...] = jnp.zeros_like(dca)
            dgh[...] = jnp.zeros_like(dgh)

        dycat = _mm_nt(dyr[...], wo[...])
        dya = dycat[:, 0:DC]
        dyb = dycat[:, DC:D]
        cp, cn = _halo_rows(gcp, gcn, i, nt - 1)
        vp, vn = _halo_rows(gvp, gvn, i, nt - 1)
        c = gc[...].astype(F32) * gv[...].astype(F32)
        cc, c_m1, c_p1 = _conv3(c, cp * vp, cn * vn, ca)
        dgb[...] = (dya * cc).astype(BF16)
        d = dya * gb[...].astype(F32)
        dcc[...] = d.astype(BF16)
        for k, s in enumerate((c_m1, c, c_p1)):
            dca[k:k + 1, :] += jnp.sum(d * s, axis=0, keepdims=True)
        oh, rs = _headnorm(of[...].astype(F32) + ob[...].astype(F32))
        g = go[...].astype(F32)
        sg = _sigmoid(g)
        silu = g * sg
        dgo[...] = (dyb * (oh * gh[...]) * (sg * (1.0 + g * (1.0 - sg)))).astype(BF16)
        don = dyb * silu
        t = jnp.sum(don * oh, axis=0, keepdims=True)
        dgh[0:1, :] += t[:, 0:HV] + t[:, HV:2 * HV] + t[:, 2 * HV:3 * HV] + t[:, 3 * HV:4 * HV]
        doh = don * gh[...]
        parts = []
        for h in range(NH):
            hs = slice(h * HV, (h + 1) * HV)
            parts.append(rs[h] * (doh[:, hs] - oh[:, hs] * jnp.mean(doh[:, hs] * oh[:, hs], axis=-1, keepdims=True)))
        do[...] = jnp.concatenate(parts, axis=1).astype(BF16)

    t = lambda cb: pl.BlockSpec((tl, DC), lambda i: (i, cb))
    hp = lambda cb: pl.BlockSpec((HALO16, DC), _prev_blk(tl, cb, HALO16))
    hn = lambda cb: pl.BlockSpec((HALO16, DC), _next_blk(tl, L, cb, HALO16))
    row = lambda n: pl.BlockSpec((tl, n), lambda i: (i, 0))
    full = lambda a: pl.BlockSpec(a.shape, lambda i: (0, 0))
    act16 = lambda n: jax.ShapeDtypeStruct((L, n), BF16)
    return pl.pallas_call(
        body, name="mix_bwd1", grid=(nt,),
        in_specs=[row(D), full(w_out), t(CB_GB), t(CB_GC), t(CB_GV), t(CB_GO), hp(CB_GC), hp(CB_GV), hn(CB_GC), hn(CB_GV),
                  row(DG), row(DG), full(conv_a), full(ghn4)],
        out_specs=[row(DC), row(DC), row(DG), row(DG), pl.BlockSpec((8, DC), lambda i: (0, 0)),
                   pl.BlockSpec((8, HV), lambda i: (0, 0))],
        out_shape=[act16(DC), act16(DC), act16(DG), act16(DG), jax.ShapeDtypeStruct((8, DC), F32),
                   jax.ShapeDtypeStruct((8, HV), F32)],
        compiler_params=_cp("arbitrary"),
    )(dy, w_out, P, P, P, P, P, P, P, P, o_f, o_b, conv_a, ghn4)


def _gla_chunk_bwd(qt, kt, kh, v, do, st16, dec, g_ref, m, rev):
    qt16, kt16, kh16, v16, do16 = (t.astype(BF16) for t in (qt, kt, kh, v, do))
    qs, sc = _gla_scores(qt16, kt16, m, rev)
    g = g_ref[...]
    g16 = g.astype(BF16)
    dob = _tile4(do16) * m["bdo"].astype(BF16)
    dv = _mm_tn(sc, dob) + _mm_nt(kh16, g16)
    dsc = jnp.where(m["triu4"] if rev else m["tril4"], _mm_nt(dob, v16), 0.0)
    r1 = _mm(dsc, kt16) * m["bdq"]
    dqt = r1[0:CH] + r1[CH:2 * CH] + r1[2 * CH:3 * CH] + r1[3 * CH:4 * CH] + _mm(do16, st16)
    dkt = _mm_tn(dsc, qs)
    dkh = _mm(v16, g16)
    dd = jnp.sum(g * st16.astype(F32), axis=0, keepdims=True)
    g_ref[...] = g * dec + _mm_tn(do16, qt16) * m["bds"]
    return dv, dqt, dkt, dkh, dd


def gla_bwd(P, do, sf, sb, gcat, gbias, tl):
    L = P.shape[0]
    nb = L // tl
    nc = tl // CH

    def body(qf, kf, vf, lf, dof, sfr, qb, kb, vb, lb, dob, sbr, gc_ref, bs_ref,
             dqf, dkf, dvf, daf, dqb, dkb, dvb, dab, gf, gbk, *scr):
        @pl.when(pl.program_id(0) == 0)
        def _():
            gf[...] = jnp.zeros_like(gf)
            gbk[...] = jnp.zeros_like(gbk)

        m = _gla_masks()
        keys = ("qt", "kt", "kh", "e", "einv", "eout", "dec")
        names = keys + ("dd", "dqt", "dkt", "dkh")
        pf = dict(zip(names, scr[0:11]))
        pb = dict(zip(names, scr[11:22]))
        for (q, k, lr, cols, rev, pr) in ((qf, kf, lf, slice(0, DK), False, pf), (qb, kb, lb, slice(DK, 2 * DK), True, pb)):
            p = _gla_tile_prep(q[...], k[...], _gates(lr, gc_ref, bs_ref, cols), m, rev, nc)
            for key in keys:
                pr[key][...] = p[key]

        def step(c, v, dor, st, g_ref, pr, dv, rev):
            rows = pl.ds(pl.multiple_of(c * CH, CH), CH)
            dvc, dqt, dkt, dkh, dd = _gla_chunk_bwd(pr["qt"][rows, :], pr["kt"][rows, :], pr["kh"][rows, :], v[rows, :],
                                                    dor[rows, :], st[c], pr["dec"][pl.ds(c, 1), :], g_ref, m, rev)
            dv[rows, :] = dvc.astype(BF16)
            pr["dqt"][rows, :] = dqt
            pr["dkt"][rows, :] = dkt
            pr["dkh"][rows, :] = dkh
            pr["dd"][pl.ds(c, 1), :] = dd

        def chunk(c, carry):
            step(nc - 1 - c, vf, dof, sfr, gf, pf, dvf, False)
            step(c, vb, dob, sbr, gbk, pb, dvb, True)
            return carry

        lax.fori_loop(0, nc, chunk, 0, unroll=2)

        def finish(pr, dq, dk, da, rev):
            dqt, dkt, dkh = pr["dqt"][...], pr["dkt"][...], pr["dkh"][...]
            kk = dkh * pr["kh"][...]
            dcum = dqt * pr["qt"][...] - dkt * pr["kt"][...] - kk
            dtot = pr["dd"][...] * pr["dec"][...]
            tri_t = m["tril"] if rev else m["triu"]
            parts = []
            for c in range(nc):
                rs = slice(c * CH, (c + 1) * CH)
                parts.append(_mm_tri(tri_t, dcum[rs]) + (jnp.sum(kk[rs], axis=0, keepdims=True) + dtot[c:c + 1]))
            da[...] = jnp.concatenate(parts, axis=0).astype(BF16)
            dq[...] = (dqt * pr["e"][...] * QSCALE).astype(BF16)
            dk[...] = (dkt * pr["einv"][...] + dkh * pr["eout"][...]).astype(BF16)

        finish(pf, dqf, dkf, daf, False)
        finish(pb, dqb, dkb, dab, True)

    fwd_dir = lambda cb: (lambda i: (nb - 1 - i, cb))
    bwd_dir = lambda cb: (lambda i: (i, cb))

    def side(ix):
        return [pl.BlockSpec((tl, DK), ix(CB_Q)), pl.BlockSpec((tl, DK), ix(CB_K)), pl.BlockSpec((tl, DG), ix(CB_V)),
                pl.BlockSpec((tl, LRW), ix(CB_LR)), pl.BlockSpec((tl, DG), ix(0)),
                pl.BlockSpec((nc, DG, DK), lambda i: (ix(0)(i)[0], 0, 0))]

    def outs(ix):
        return [pl.BlockSpec((tl, DK), ix(0)), pl.BlockSpec((tl, DK), ix(0)), pl.BlockSpec((tl, DG), ix(0)),
                pl.BlockSpec((tl, DK), ix(0))]

    o_shape = [jax.ShapeDtypeStruct((L, DK), BF16), jax.ShapeDtypeStruct((L, DK), BF16),
               jax.ShapeDtypeStruct((L, DG), BF16), jax.ShapeDtypeStruct((L, DK), BF16)]
    return pl.pallas_call(
        body, name="gla_bwd", grid=(nb,),
        in_specs=side(fwd_dir) + side(bwd_dir) + [pl.BlockSpec((LRW, 2 * DK), lambda i: (0, 0)),
                                                  pl.BlockSpec((1, 2 * DK), lambda i: (0, 0))],
        out_specs=outs(fwd_dir) + outs(bwd_dir),
        out_shape=o_shape + o_shape,
        scratch_shapes=[pltpu.VMEM((DG, DK), F32), pltpu.VMEM((DG, DK), F32)]
        + ([pltpu.VMEM((tl, DK), F32)] * 6 + [pltpu.VMEM((nc, DK), F32)] * 2 + [pltpu.VMEM((tl, DK), F32)] * 3) * 2,
        compiler_params=_cp("arbitrary"),
    )(P, P, P, P, do, sf, P, P, P, P, do, sb, gcat, gbias)


def mix_bwd2(dgb, dcc, dgo, gl, P, conv_a, gcat, gbias, w_in, x, dres, g1, tl):
    L = P.shape[0]
    nt = L // tl

    def body(dgbr, dccr, dccp, dccn, dgor, dqf, dkf, dvf, daf, dqb, dkb, dvb, dab, gc, gv, lr, ca, gcr, bsr, wi,
             xr, drr, g1r, dP, dx, dg1, dgcat, dbias):
        i = pl.program_id(0)

        @pl.when(i == 0)
        def _():
            dg1[...] = jnp.zeros_like(dg1)
            dgcat[...] = jnp.zeros_like(dgcat)
            dbias[...] = jnp.zeros_like(dbias)

        p, n = _halo_rows(dccp, dccn, i, nt - 1)
        dc = _conv3_t(dccr[...].astype(F32), p, n, ca)
        pre = _mm(lr[...], gcr[...]) + bsr[...]
        da = jnp.concatenate([daf[...], dab[...]], axis=1).astype(F32)
        add32 = lambda a, b: a[...].astype(F32) + b[...].astype(F32)
        dpre = da * GATE_NORM * (1.0 - _sigmoid(pre))
        dpre16 = dpre.astype(BF16)
        dP[:, 0:DC] = dgbr[...].astype(BF16)
        dP[:, DC:2 * DC] = (dc * gv[...].astype(F32)).astype(BF16)
        dP[:, 2 * DC:3 * DC] = (dc * gc[...].astype(F32)).astype(BF16)
        dP[:, 1536:1792] = add32(dqf, dqb).astype(BF16)
        dP[:, 1792:2048] = add32(dkf, dkb).astype(BF16)
        dP[:, 2048:2560] = add32(dvf, dvb).astype(BF16)
        dP[:, 2560:3072] = dgor[...].astype(BF16)
        dP[:, 3072:3200] = _mm_nt(dpre16, gcr[...]).astype(BF16)
        dgcat[...] += _mm_tn(lr[...], dpre16)
        dbias[0:1, :] += jnp.sum(dpre, axis=0, keepdims=True)
        dh, dg = _rms_bwd(_mm_nt(dP[...], wi[...]), xr[...], g1r[...])
        dx[...] = drr[...] + dh
        dg1[...] += dg

    row = lambda n: pl.BlockSpec((tl, n), lambda i: (i, 0))
    t = lambda w, cb: pl.BlockSpec((tl, w), lambda i: (i, cb))
    full = lambda a: pl.BlockSpec(a.shape, lambda i: (0, 0))
    return pl.pallas_call(
        body, name="mix_bwd2", grid=(nt,),
        in_specs=[row(DC), row(DC), pl.BlockSpec((HALO16, DC), _prev_blk(tl, 0, HALO16)),
                  pl.BlockSpec((HALO16, DC), _next_blk(tl, L, 0, HALO16)),
                  row(DG), row(DK), row(DK), row(DG), row(DK), row(DK), row(DK), row(DG), row(DK),
                  t(DC, CB_GC), t(DC, CB_GV), t(LRW, CB_LR), full(conv_a), full(gcat), full(gbias), full(w_in),
                  row(D), row(D), full(g1)],
        out_specs=[row(DINP), row(D), pl.BlockSpec((1, D), lambda i: (0, 0)), pl.BlockSpec((LRW, 2 * DK), lambda i: (0, 0)),
                   pl.BlockSpec((8, 2 * DK), lambda i: (0, 0))],
        out_shape=[jax.ShapeDtypeStruct((L, DINP), BF16), jax.ShapeDtypeStruct((L, D), F32),
                   jax.ShapeDtypeStruct((1, D), F32), jax.ShapeDtypeStruct((LRW, 2 * DK), F32),
                   jax.ShapeDtypeStruct((8, 2 * DK), F32)],
        compiler_params=_cp("arbitrary"),
    )(dgb, dcc, dcc, dcc, dgo, *gl, P, P, P, conv_a, gcat, gbias, w_in, x, dres, g1)


def _row_tile(rows, cols):
    if rows * cols * 4 <= 2 * 1024 * 1024:
        return rows
    best = 8
    for t in range(8, rows, 8):
        if rows % t == 0 and t * cols * 4 <= 2 * 1024 * 1024:
            best = t
    return best


def adamw(w, g, m, v, name):
    shape = w.shape
    cols = shape[-1]
    w2, g2, m2, v2 = (a.reshape(-1, cols) for a in (w, g, m, v))
    rows = w2.shape[0]
    tr = _row_tile(rows, cols)

    def body(wr, gr, mr, vr, dl, nm, nv):
        gg = gr[...]
        mm = B1 * mr[...] + (1.0 - B1) * gg
        vv = B2 * vr[...] + (1.0 - B2) * (gg * gg)
        m_hat = mm / (1.0 - B1 ** STEP)
        v_hat = vv / (1.0 - B2 ** STEP)
        dl[...] = -LR * (m_hat / (jnp.sqrt(v_hat) + AEPS) + WD * wr[...])
        nm[...] = mm
        nv[...] = vv

    blk = pl.BlockSpec((tr, cols), lambda i: (i, 0))
    o = jax.ShapeDtypeStruct((rows, cols), F32)
    d, nm, nv = pl.pallas_call(
        body, name=name, grid=(rows // tr,), in_specs=[blk] * 4, out_specs=[blk] * 3, out_shape=[o, o, o],
        compiler_params=_cp("parallel"),
    )(w2, g2, m2, v2)
    return d.reshape(shape), nm.reshape(shape), nv.reshape(shape)


def _place():
    return lax.axis_index("x"), lax.axis_index("y"), lax.axis_index("c")


def allgather8(v, name):
    mp, n = v.shape

    def body(x_ref, out_ref, send_sems, recv_sems, local_sem):
        x, y, c = _place()
        me, sibling = (x, y, c), (x, y, 1 - c)
        chips = [(1 - x, y), (x, 1 - y), (1 - x, 1 - y)]

        def rows(px, py, pc):
            return out_ref.at[pl.ds((4 * px + 2 * py + pc) * mp, mp), :]

        def copy(k, block, to, src=None):
            return pltpu.make_async_remote_copy(
                src_ref=rows(*block) if src is None else src, dst_ref=rows(*block),
                send_sem=send_sems.at[k], recv_sem=recv_sems.at[k], device_id=to, device_id_type=MESH)

        mine = pltpu.make_async_copy(x_ref, rows(*me), local_sem)
        mine.start()
        first = [copy(0, me, sibling, src=x_ref)]
        first += [copy(1 + j, me, (*chip, c), src=x_ref) for j, chip in enumerate(chips)]
        for cp in first:
            cp.start()
        passed = [copy(4 + j, (*chip, c), sibling) for j, chip in enumerate(chips)]
        for j, chip in enumerate(chips):
            copy(1 + j, (*chip, c), me).wait_recv()
            passed[j].start()
        copy(0, sibling, me).wait_recv()
        for j, chip in enumerate(chips):
            copy(4 + j, (*chip, 1 - c), me).wait_recv()
        for cp in first + passed:
            cp.wait_send()
        mine.wait()

    return pl.pallas_call(
        body, name=name, out_shape=jax.ShapeDtypeStruct((8 * mp, n), v.dtype),
        in_specs=[pl.BlockSpec(memory_space=pltpu.VMEM)], out_specs=pl.BlockSpec(memory_space=pltpu.VMEM),
        scratch_shapes=[pltpu.SemaphoreType.DMA((7,)), pltpu.SemaphoreType.DMA((7,)), pltpu.SemaphoreType.DMA],
        compiler_params=pltpu.CompilerParams(vmem_limit_bytes=VMEM_LIMIT),
    )(v)


def sum8(v, mp):
    def body(x_ref, o_ref):
        acc = x_ref[0:mp, :]
        for d in range(1, 8):
            acc = acc + x_ref[d * mp:(d + 1) * mp, :]
        o_ref[...] = acc

    return pl.pallas_call(body, name="sum8", out_shape=jax.ShapeDtypeStruct((mp, v.shape[1]), F32),
                          compiler_params=pltpu.CompilerParams(vmem_limit_bytes=VMEM_LIMIT))(v)


_ANY = pl.BlockSpec(memory_space=pl.ANY)


def _row_half(ref, lead, h):
    hr = ref.shape[-2] // 2
    return ref.at[(*lead, pl.ds(h * hr, hr), slice(None))]


def allgather_weights(slots):
    n = len(slots)

    def body(*refs):
        s_refs, o_refs, (send_sems, recv_sems) = refs[:n], refs[n:2 * n], refs[2 * n:]
        x, y, c = _place()
        me = 2 * x + y
        sibling = (x, y, 1 - c)
        chips = [(1 - x, y), (x, 1 - y), (1 - x, 1 - y)]

        def half(ref, slot, h):
            return _row_half(ref, (slot, slice(None)), h)

        def copy(k, src, dst, to):
            return pltpu.make_async_remote_copy(src_ref=src, dst_ref=dst, send_sem=send_sems.at[k],
                                                recv_sem=recv_sems.at[k], device_id=to, device_id_type=MESH)

        first = [copy(6 * a + k, half(s_refs[a], me, c), half(o_refs[a], me, c), (px, py, c))
                 for k, (px, py) in enumerate(chips) for a in range(n)]
        for cp in first:
            cp.start()
        passed = []
        for k, (px, py) in enumerate(chips):
            for a in range(n):
                got = half(o_refs[a], 2 * px + py, c)
                copy(6 * a + k, half(s_refs[a], me, c), got, (px, py, c)).wait_recv()
                cp = copy(6 * a + 3 + k, got, got, sibling)
                cp.start()
                passed.append(cp)
        for k, (px, py) in enumerate(chips):
            for a in range(n):
                got = half(o_refs[a], 2 * px + py, 1 - c)
                copy(6 * a + 3 + k, got, got, sibling).wait_recv()
        for cp in first + passed:
            cp.wait_send()

    return pl.pallas_call(
        body, name="allgather_weights", out_shape=[jax.ShapeDtypeStruct(s.shape, s.dtype) for s in slots],
        in_specs=[_ANY] * n, out_specs=[_ANY] * n, input_output_aliases={a: a for a in range(n)},
        scratch_shapes=[pltpu.SemaphoreType.DMA((6 * n,)), pltpu.SemaphoreType.DMA((6 * n,))],
    )(*slots)


_HBM = pl.BlockSpec(memory_space=pltpu.HBM)
_SEM = pl.BlockSpec(memory_space=pltpu.SEMAPHORE)
_EFFECT = pltpu.SideEffectType.DATAFLOW_SIDE_EFFECTING


def gather_start(slots, name, after=()):
    n = len(slots)
    na = len(after)

    def body(*refs):
        s_refs, send_sems, recv_sems, token = refs[:n], refs[n + na], refs[n + na + 1], refs[-1]
        x, y, c = _place()
        me = 2 * x + y
        for k, (px, py) in enumerate([(1 - x, y), (x, 1 - y), (1 - x, 1 - y)]):
            for a in range(n):
                pltpu.make_async_remote_copy(
                    src_ref=s_refs[a].at[me], dst_ref=s_refs[a].at[me], send_sem=send_sems.at[3 * a + k],
                    recv_sem=recv_sems.at[3 * a + k], device_id=(px, py, c), device_id_type=MESH).start()
        token[...] = jnp.zeros_like(token)

    out = pl.pallas_call(
        body, name=name,
        out_shape=(pltpu.SemaphoreType.DMA((3 * n,)), pltpu.SemaphoreType.DMA((3 * n,)),
                   *[pltpu.HBM(s.shape, s.dtype) for s in slots], jax.ShapeDtypeStruct((8, 128), F32)),
        in_specs=[_HBM] * n + [_ANY] * na, out_specs=(_SEM, _SEM, *[_HBM] * n, pl.BlockSpec(memory_space=pltpu.VMEM)),
        input_output_aliases={a: 2 + a for a in range(n)},
        compiler_params=pltpu.CompilerParams(has_side_effects=_EFFECT),
    )(*[pltpu.with_memory_space_constraint(s, pltpu.HBM) for s in slots], *after)
    return out[0], out[1], list(out[2:2 + n]), out[-1]


def gather_wait(send_sems, recv_sems, slots, after, name):
    n = len(slots)

    def body(*refs):
        s_refs, ssem, rsem = refs[:n], refs[n], refs[n + 1]
        x, y, c = _place()
        me = 2 * x + y
        for k, (px, py) in enumerate([(1 - x, y), (x, 1 - y), (1 - x, 1 - y)]):
            for a in range(n):
                cp = pltpu.make_async_remote_copy(
                    src_ref=s_refs[a].at[me], dst_ref=s_refs[a].at[2 * px + py], send_sem=ssem.at[3 * a + k],
                    recv_sem=rsem.at[3 * a + k], device_id=(px, py, c), device_id_type=MESH)
                cp.wait_send()
                cp.wait_recv()

    return pl.pallas_call(
        body, name=name, out_shape=[pltpu.HBM(s.shape, s.dtype) for s in slots],
        in_specs=[_HBM] * n + [_SEM, _SEM, _ANY], out_specs=[_HBM] * n,
        input_output_aliases={a: a for a in range(n)},
        compiler_params=pltpu.CompilerParams(has_side_effects=_EFFECT),
    )(*slots, send_sems, recv_sems, after)


def rs_chipsum16(g, recv1, cidx, name):
    nl, hr, cols = recv1.shape[1:]

    def body(c_ref, g_ref, r_ref, o_ref):
        o_ref[...] = (g_ref[...] + r_ref[...]).astype(BF16)

    blk = (1, 1, hr, cols)
    return pl.pallas_call(
        body, name=name, out_shape=jax.ShapeDtypeStruct(recv1.shape, BF16),
        grid_spec=pltpu.PrefetchScalarGridSpec(
            num_scalar_prefetch=1, grid=(4, nl),
            in_specs=[pl.BlockSpec(blk, lambda j, l, c: (j, l, c[0], 0)), pl.BlockSpec(blk, lambda j, l, c: (j, l, 0, 0))],
            out_specs=pl.BlockSpec(blk, lambda j, l, c: (j, l, 0, 0))),
        compiler_params=_cp("parallel", "parallel"),
    )(cidx, g, recv1)


def sibling_start(gs, name):
    n = len(gs)
    lands = [lax.empty((*g.shape[:2], g.shape[2] // 2, g.shape[3]), F32) for g in gs]

    def body(*refs):
        g_refs, l_refs, send_sems, recv_sems, token = refs[:n], refs[n:2 * n], refs[2 * n], refs[2 * n + 1], refs[-1]
        x, y, c = _place()
        for a in range(n):
            pltpu.make_async_remote_copy(
                src_ref=_row_half(g_refs[a], (slice(None), slice(None)), 1 - c), dst_ref=l_refs[a],
                send_sem=send_sems.at[a], recv_sem=recv_sems.at[a], device_id=(x, y, 1 - c), device_id_type=MESH).start()
        token[...] = jnp.zeros_like(token)

    bufs = list(gs) + lands
    out = pl.pallas_call(
        body, name=name,
        out_shape=(pltpu.SemaphoreType.DMA((n,)), pltpu.SemaphoreType.DMA((n,)),
                   *[pltpu.HBM(b.shape, b.dtype) for b in bufs], jax.ShapeDtypeStruct((8, 128), F32)),
        in_specs=[_HBM] * (2 * n), out_specs=(_SEM, _SEM, *[_HBM] * (2 * n), pl.BlockSpec(memory_space=pltpu.VMEM)),
        input_output_aliases={i: 2 + i for i in range(2 * n)},
        compiler_params=pltpu.CompilerParams(has_side_effects=_EFFECT),
    )(*[pltpu.with_memory_space_constraint(b, pltpu.HBM) for b in bufs])
    return out[0], out[1], list(out[2:2 + n]), list(out[2 + n:2 + 2 * n]), out[-1]


def sibling_wait(send_sems, recv_sems, gs, lands, after, name):
    n = len(gs)

    def body(*refs):
        g_refs, l_refs, ssem, rsem = refs[:n], refs[n:2 * n], refs[2 * n], refs[2 * n + 1]
        x, y, c = _place()
        for a in range(n):
            cp = pltpu.make_async_remote_copy(
                src_ref=_row_half(g_refs[a], (slice(None), slice(None)), 1 - c), dst_ref=l_refs[a],
                send_sem=ssem.at[a], recv_sem=rsem.at[a], device_id=(x, y, 1 - c), device_id_type=MESH)
            cp.wait_send()
            cp.wait_recv()

    bufs = list(gs) + list(lands)
    out = pl.pallas_call(
        body, name=name, out_shape=[pltpu.HBM(b.shape, b.dtype) for b in bufs],
        in_specs=[_HBM] * (2 * n) + [_SEM, _SEM, _ANY], out_specs=[_HBM] * (2 * n),
        input_output_aliases={i: i for i in range(2 * n)},
        compiler_params=pltpu.CompilerParams(has_side_effects=_EFFECT),
    )(*bufs, send_sems, recv_sems, after)
    return list(out[:n]), list(out[n:])


def exchange_start(cs, name):
    n = len(cs)
    lands = [lax.empty((3, *c.shape[1:]), BF16) for c in cs]

    def body(*refs):
        s_refs, l_refs, send_sems, recv_sems, token = refs[:n], refs[n:2 * n], refs[2 * n], refs[2 * n + 1], refs[-1]
        x, y, c = _place()
        for k, (px, py) in enumerate([(1 - x, y), (x, 1 - y), (1 - x, 1 - y)]):
            for a in range(n):
                pltpu.make_async_remote_copy(
                    src_ref=s_refs[a].at[2 * px + py], dst_ref=l_refs[a].at[k], send_sem=send_sems.at[3 * a + k],
                    recv_sem=recv_sems.at[3 * a + k], device_id=(px, py, c), device_id_type=MESH).start()
        token[...] = jnp.zeros_like(token)

    bufs = list(cs) + lands
    out = pl.pallas_call(
        body, name=name,
        out_shape=(pltpu.SemaphoreType.DMA((3 * n,)), pltpu.SemaphoreType.DMA((3 * n,)),
                   *[pltpu.HBM(b.shape, b.dtype) for b in bufs], jax.ShapeDtypeStruct((8, 128), F32)),
        in_specs=[_HBM] * (2 * n), out_specs=(_SEM, _SEM, *[_HBM] * (2 * n), pl.BlockSpec(memory_space=pltpu.VMEM)),
        input_output_aliases={i: 2 + i for i in range(2 * n)},
        compiler_params=pltpu.CompilerParams(has_side_effects=_EFFECT),
    )(*[pltpu.with_memory_space_constraint(b, pltpu.HBM) for b in bufs])
    return out[0], out[1], list(out[2:2 + n]), list(out[2 + n:2 + 2 * n]), out[-1]


def exchange_wait(send_sems, recv_sems, cs, lands, after, name):
    n = len(cs)

    def body(*refs):
        s_refs, l_refs, ssem, rsem = refs[:n], refs[n:2 * n], refs[2 * n], refs[2 * n + 1]
        x, y, c = _place()
        for k, (px, py) in enumerate([(1 - x, y), (x, 1 - y), (1 - x, 1 - y)]):
            for a in range(n):
                cp = pltpu.make_async_remote_copy(
                    src_ref=s_refs[a].at[2 * px + py], dst_ref=l_refs[a].at[k], send_sem=ssem.at[3 * a + k],
                    recv_sem=rsem.at[3 * a + k], device_id=(px, py, c), device_id_type=MESH)
                cp.wait_send()
                cp.wait_recv()

    bufs = list(cs) + list(lands)
    out = pl.pallas_call(
        body, name=name, out_shape=[pltpu.HBM(b.shape, b.dtype) for b in bufs],
        in_specs=[_HBM] * (2 * n) + [_SEM, _SEM, _ANY], out_specs=[_HBM] * (2 * n),
        input_output_aliases={i: i for i in range(2 * n)},
        compiler_params=pltpu.CompilerParams(has_side_effects=_EFFECT),
    )(*bufs, send_sems, recv_sems, after)
    return list(out[n:])


def rs_final_sum(g, recv1, recv2, idx, name):
    nl, hr, cols = recv1.shape[1:]

    def body(i_ref, g_ref, r1_ref, r2_ref, o_ref):
        acc = g_ref[0, 0] + r1_ref[0, 0]
        for k in range(3):
            acc = acc + r2_ref[k, 0].astype(F32)
        o_ref[0] = acc

    blk = (1, 1, hr, cols)
    return pl.pallas_call(
        body, name=name, out_shape=jax.ShapeDtypeStruct((nl, 2 * hr, cols), F32),
        grid_spec=pltpu.PrefetchScalarGridSpec(
            num_scalar_prefetch=1, grid=(nl,),
            in_specs=[pl.BlockSpec(blk, lambda l, ix: (ix[0], l, ix[1], 0)), pl.BlockSpec(blk, lambda l, ix: (ix[0], l, 0, 0)),
                      pl.BlockSpec((3, 1, hr, cols), lambda l, ix: (0, l, 0, 0))],
            out_specs=pl.BlockSpec((1, hr, cols), lambda l, ix: (l, ix[1], 0))),
        compiler_params=_cp("parallel"),
    )(idx, g, recv1, recv2)


def rs_share_halves(fulls):
    n = len(fulls)

    def body(*refs):
        h_refs, o_refs, (send_sems, recv_sems) = refs[:n], refs[n:2 * n], refs[2 * n:]
        x, y, c = _place()
        sibling = (x, y, 1 - c)

        def copy(a, h):
            return pltpu.make_async_remote_copy(
                src_ref=_row_half(h_refs[a], (slice(None),), h), dst_ref=_row_half(o_refs[a], (slice(None),), h),
                send_sem=send_sems.at[a], recv_sem=recv_sems.at[a], device_id=sibling, device_id_type=MESH)

        for a in range(n):
            copy(a, c).start()
        for a in range(n):
            copy(a, c).wait_send()
            copy(a, 1 - c).wait_recv()

    return pl.pallas_call(
        body, name="rs_share_halves", out_shape=[jax.ShapeDtypeStruct(f.shape, F32) for f in fulls],
        in_specs=[_ANY] * n, out_specs=[_ANY] * n, input_output_aliases={a: a for a in range(n)},
        scratch_shapes=[pltpu.SemaphoreType.DMA((n,)), pltpu.SemaphoreType.DMA((n,))],
    )(*fulls)


def _own_slot(shard, chip, dtype):
    return lax.dynamic_update_slice(lax.empty((4, *shard.shape), dtype), shard.astype(dtype)[None],
                                    (chip,) + (0,) * shard.ndim)


def kernel(x, norm_mix_pre, norm_mix_post, norm_ffn_pre, norm_ffn_post, w_in, conv_a, gate_up_fwd, gate_bias_fwd, gate_up_bwd, gate_bias_bwd, gla_head_norm, w_out, w_up, conv_ffn, w_down, loss_target, m_norm_mix_pre, m_norm_mix_post, m_norm_ffn_pre, m_norm_ffn_post, m_w_in, m_conv_a, m_gate_up_fwd, m_gate_bias_fwd, m_gate_up_bwd, m_gate_bias_bwd, m_gla_head_norm, m_w_out, m_w_up, m_conv_ffn, m_w_down, v_norm_mix_pre, v_norm_mix_post, v_norm_ffn_pre, v_norm_ffn_post, v_w_in, v_conv_a, v_gate_up_fwd, v_gate_bias_fwd, v_gate_up_bwd, v_gate_bias_bwd, v_gla_head_norm, v_w_out, v_w_up, v_conv_ffn, v_w_down):
    L = x.shape[1]
    xi, yi, ci = _place()
    chip = 2 * xi + yi
    tl_gla, tl_mix, tl_ffn = min(L, TL_GLA), min(L, TL_MIX), min(L, TL_FFN)

    big_w = (w_in, w_out, w_up, w_down)
    a_in0 = allgather_weights([_own_slot(w_in[0:1], chip, BF16)])[0][:, 0]
    started = []
    prev = (a_in0,)
    for l in range(DEPTH):
        ws = big_w[1:] if l == 0 else big_w
        started.append(gather_start([_own_slot(w[l], chip, BF16) for w in ws], f"gather_start_{l}", after=prev))
        prev = (started[-1][3],)
    tokens = [s[3] for s in started]

    def full_w_in(a_in):
        return jnp.pad(jnp.concatenate([a_in[j] for j in range(4)], axis=1), ((0, 0), (0, DINP - DIN)))

    small = jnp.concatenate([conv_a.reshape(-1), gate_up_fwd.reshape(-1), gate_up_bwd.reshape(-1), conv_ffn.reshape(-1)])
    ms = small.shape[0] // 128
    sg = allgather8(small.reshape(ms, 128), "allgather_small_weights").reshape(4, 2, ms * 128)[:, 0]

    def small_full(off, shape):
        n = shape[0] * shape[1] * shape[2]
        return jnp.concatenate([sg[j, off:off + n].reshape(shape) for j in range(4)], axis=2)

    o1 = DEPTH * 3 * 128
    o2 = o1 + DEPTH * RK * 64
    o3 = o2 + DEPTH * RK * 64
    conv_a_f = small_full(0, (DEPTH, 3, 128))
    gup_f = small_full(o1, (DEPTH, RK, 64))
    gup_b = small_full(o2, (DEPTH, RK, 64))
    conv_ffn_f = small_full(o3, (DEPTH, 3, 1408))

    def gcat_of(l):
        g = jnp.zeros((LRW, 2 * DK), F32)
        g = g.at[0:RK, 0:DK].set(gup_f[l]).at[RK:2 * RK, DK:2 * DK].set(gup_b[l])
        return g.astype(BF16)

    gcats = [gcat_of(l) for l in range(DEPTH)]
    gbiases = [jnp.concatenate([gate_bias_fwd[l], gate_bias_bwd[l]])[None, :] for l in range(DEPTH)]
    ghn4s = [jnp.tile(gla_head_norm[l], NH)[None, :] for l in range(DEPTH)]

    xc = x.reshape(L, D)
    saved = []
    W_in, W_out, W_up, W_down = [], [], [], []
    tl_row = min(L, TL_ROW)
    for l in range(DEPTH):
        ssem, rsem, bufs, _ = started[l]
        if l > 0:
            a_in, a_out, a_up, a_down = gather_wait(ssem, rsem, bufs, xc, f"gather_wait_{l}")
        W_in.append(full_w_in(a_in0 if l == 0 else a_in))
        P, h1 = rms_matmul(xc, norm_mix_pre[l][None, :], W_in[l], DINP, "proj_in", out_dtype=BF16, tm=TM_PROJ // 2,
                           after=tokens if l == 0 else ())
        o_f, o_b, sf, sb = gla_fwd(P, gcats[l], gbiases[l], tl_gla)
        if l == 0:
            a_out, a_up, a_down = gather_wait(ssem, rsem, bufs, o_f, "gather_wait_0")
        W_out.append(a_out.reshape(D, D))
        W_up.append(a_up)
        W_down.append(a_down.reshape(DFF, D))
        ycat, y, x1 = mix_out(P, o_f, o_b, conv_a_f[l], ghn4s[l], W_out[l], norm_mix_post[l][None, :], xc,
                              min(L, TL_MIX_OUT))
        U, h2 = rms_matmul(x1, norm_ffn_pre[l][None, :], W_up[l], WFF, "proj_up", n_out=2 * DFF, out_dtype=BF16,
                           w_spec=pl.BlockSpec((None, D, WFF), lambda i, j: (j, 0, 0)))
        y2, x2, ug, uv, z = ffn_down(U, conv_ffn_f[l], W_down[l], norm_ffn_post[l][None, :], x1, tl_ffn)
        saved.append(dict(x=xc, h1=h1, P=P, o_f=o_f, o_b=o_b, sf=sf, sb=sb, ycat=ycat, y=y, x1=x1, h2=h2, U=U, y2=y2,
                          ug=ug, uv=uv, z=z))
        xc = x2

    dx, loss_blk = loss_head(xc, loss_target.reshape(L, D), tl_row)

    big = ("w_in", "w_out", "w_up", "w_down")
    cidx = jnp.reshape(ci, (1,)).astype(jnp.int32)
    idx = jnp.stack([chip, ci]).astype(jnp.int32)
    grads = [None] * DEPTH
    reduced = [dict() for _ in range(DEPTH)]
    tl_dw = min(L, 1024)
    groups = dict(ffn=("w_up", "w_down"), mix=("w_in", "w_out"))
    state = {grp: dict(flight=None, sibling=None) for grp in groups}
    token = ()

    def finish(grp, after):
        lp, gs_p, recv1_p, (ssem, rsem, cs_thru, lands, _) = state[grp]["flight"]
        recv2 = exchange_wait(ssem, rsem, cs_thru, lands, after, f"exchange_wait_{grp}_{lp}")
        halves = [rs_final_sum(g, r1, r2, idx, "rs_final_sum_" + k)
                  for g, r1, r2, k in zip(gs_p, recv1_p, recv2, groups[grp])]
        reduced[lp].update(zip(groups[grp], rs_share_halves(halves)))

    def advance(grp, after):
        st = state[grp]
        ls, (ssem, rsem, gs_thru, lands, _) = st["sibling"]
        gs_s, recv1 = sibling_wait(ssem, rsem, gs_thru, lands, after, f"sibling_wait_{grp}_{ls}")
        cs16 = [rs_chipsum16(g, r, cidx, "rs_chipsum16_" + k) for g, r, k in zip(gs_s, recv1, groups[grp])]
        flight = exchange_start(cs16, f"exchange_start_{grp}_{ls}")
        if st["flight"] is not None:
            finish(grp, flight[4])
        st["flight"] = (ls, gs_s, recv1, flight)
        st["sibling"] = None
        return flight[4]

    for l in reversed(range(DEPTH)):
        s = saved[l]
        dy2, dg4 = rms_bwd_pre(dx, s["y2"], norm_ffn_post[l][None, :], tl_row, after=token)
        du_g, du_v = ffn_bwd1(dy2, s["ug"], s["uv"], W_down[l], min(L, TL_FFN2))
        token2 = (advance("mix", du_g),) if state["mix"]["sibling"] is not None else ()
        g_down = matmul_tn(s["z"], dy2, DFF // 2, D, tl_dw, "dw_down").reshape(4, 1, DFF // 4, D)
        dU_g, dU_v, dx1, dg3, dcf_g, dcf_v = ffn_bwd2(du_g, du_v, s["U"], conv_ffn_f[l], W_up[l], s["x1"], dx,
                                                      norm_ffn_pre[l][None, :],
                                        min(L, TL_FFN2))
        g_up = matmul_tn(s["h2"], dU_g, D, WFF, tl_dw, "dw_up_gate",
                         into=(lax.empty((4, 1, D, WFF), F32), (None, None, D, WFF), lambda p, q: (q, 0, 0, 0)))
        g_up = matmul_tn(s["h2"], dU_v, D, WFF, tl_dw, "dw_up_val",
                         into=(g_up, (None, None, D, WFF), lambda p, q: (NFF + q, 0, 0, 0)))
        sib = sibling_start([g_up, g_down], f"sibling_start_ffn_{l}")
        state["ffn"]["sibling"] = (l, sib)
        dy, dg2 = rms_bwd_pre(dx1, s["y"], norm_mix_post[l][None, :], tl_row, after=token2 + (sib[4],))
        dgb, dcc, dgo, do, dca, dghn = mix_bwd1(dy, W_out[l], s["P"], s["o_f"], s["o_b"], conv_a_f[l], ghn4s[l],
                                                min(L, TL_MIX_OUT))
        token3 = advance("ffn", dgb)
        g_out = matmul_tn(s["ycat"], dy, D, D, tl_dw, "dw_out", after=(token3,)).reshape(4, 1, D // 4, D)
        gl = gla_bwd(s["P"], do, s["sf"], s["sb"], gcats[l], gbiases[l], tl_gla)
        dP, dx, dg1, dgcat, dbias = mix_bwd2(dgb, dcc, dgo, gl, s["P"], conv_a_f[l], gcats[l], gbiases[l], W_in[l],
                                             s["x"], dx1, norm_mix_pre[l][None, :], tl_mix)
        dW_in = matmul_tn(s["h1"], dP, D // 2, DINP, tl_dw, "dw_in")
        g_in = jnp.stack([dW_in[:, (DIN // 4) * j:(DIN // 4) * (j + 1)] for j in range(4)])[:, None]
        grads[l] = dict(
            norm_mix_pre=dg1[0], norm_mix_post=dg2[0], norm_ffn_pre=dg3[0], norm_ffn_post=dg4[0],
            conv_a=dca[0:3], gate_up_fwd=dgcat[0:RK, 0:DK], gate_bias_fwd=dbias[0, 0:DK],
            gate_up_bwd=dgcat[RK:2 * RK, DK:2 * DK], gate_bias_bwd=dbias[0, DK:2 * DK], gla_head_norm=dghn[0],
            conv_ffn=jnp.concatenate([dcf_g[j, 0:3] for j in range(NFF)] + [dcf_v[j, 0:3] for j in range(NFF)], axis=1))
        sib = sibling_start([g_in, g_out], f"sibling_start_mix_{l}")
        state["mix"]["sibling"] = (l, sib)
        token = (sib[4],)
    last = advance("mix", token[0])
    finish("ffn", last)
    finish("mix", last)

    G = {k: jnp.stack([grads[l][k] for l in range(DEPTH)]) for k in grads[0]}

    small_names = ["norm_mix_pre", "norm_mix_post", "norm_ffn_pre", "norm_ffn_post", "conv_a", "gate_up_fwd",
                   "gate_bias_fwd", "gate_up_bwd", "gate_bias_bwd", "gla_head_norm", "conv_ffn"]
    flat = jnp.concatenate([G[k].reshape(-1) for k in small_names] + [loss_blk[0, 0:1]])
    n_small = flat.shape[0]
    mp = -(-n_small // 1024) * 8
    flat = jnp.pad(flat, (0, mp * 128 - n_small)).reshape(mp, 128)
    tot = sum8(allgather8(flat, "allgather_small_grads"), mp).reshape(-1)
    gsm = {}
    o = 0
    for k in small_names:
        n = G[k].size
        gsm[k] = tot[o:o + n].reshape(G[k].shape)
        o += n
    loss = tot[o]

    def my_cols(a, width):
        return lax.dynamic_slice_in_dim(a, chip * width, width, axis=2)

    gsm["conv_a"] = my_cols(gsm["conv_a"], 128)
    gsm["gate_up_fwd"] = my_cols(gsm["gate_up_fwd"], 64)
    gsm["gate_up_bwd"] = my_cols(gsm["gate_up_bwd"], 64)
    gsm["conv_ffn"] = my_cols(gsm["conv_ffn"], 1408)

    for k in big:
        gsm[k] = jnp.concatenate([reduced[l][k] for l in range(DEPTH)], axis=0)

    names = ["norm_mix_pre", "norm_mix_post", "norm_ffn_pre", "norm_ffn_post", "w_in", "conv_a", "gate_up_fwd",
             "gate_bias_fwd", "gate_up_bwd", "gate_bias_bwd", "gla_head_norm", "w_out", "w_up", "conv_ffn", "w_down"]
    w = dict(norm_mix_pre=norm_mix_pre, norm_mix_post=norm_mix_post, norm_ffn_pre=norm_ffn_pre, norm_ffn_post=norm_ffn_post,
             w_in=w_in, conv_a=conv_a, gate_up_fwd=gate_up_fwd, gate_bias_fwd=gate_bias_fwd, gate_up_bwd=gate_up_bwd,
             gate_bias_bwd=gate_bias_bwd, gla_head_norm=gla_head_norm, w_out=w_out, w_up=w_up, conv_ffn=conv_ffn, w_down=w_down)
    m = dict(norm_mix_pre=m_norm_mix_pre, norm_mix_post=m_norm_mix_post, norm_ffn_pre=m_norm_ffn_pre, norm_ffn_post=m_norm_ffn_post,
             w_in=m_w_in, conv_a=m_conv_a, gate_up_fwd=m_gate_up_fwd, gate_bias_fwd=m_gate_bias_fwd, gate_up_bwd=m_gate_up_bwd,
             gate_bias_bwd=m_gate_bias_bwd, gla_head_norm=m_gla_head_norm, w_out=m_w_out, w_up=m_w_up, conv_ffn=m_conv_ffn, w_down=m_w_down)
    v = dict(norm_mix_pre=v_norm_mix_pre, norm_mix_post=v_norm_mix_post, norm_ffn_pre=v_norm_ffn_pre, norm_ffn_post=v_norm_ffn_post,
             w_in=v_w_in, conv_a=v_conv_a, gate_up_fwd=v_gate_up_fwd, gate_bias_fwd=v_gate_bias_fwd, gate_up_bwd=v_gate_up_bwd,
             gate_bias_bwd=v_gate_bias_bwd, gla_head_norm=v_gla_head_norm, w_out=v_w_out, w_up=v_w_up, conv_ffn=v_conv_ffn, w_down=v_w_down)
    upd = {k: adamw(w[k], gsm[k], m[k], v[k], "adamw_" + k) for k in names}
    return (loss, dx.reshape(1, L, D), *[gsm[k] for k in names], *[upd[k][0] for k in names],
            *[upd[k][1] for k in names], *[upd[k][2] for k in names])
```

```python
import functools

import jax
import jax.numpy as jnp
from jax import lax
from jax.experimental import pallas as pl
from jax.experimental.pallas import tpu as pltpu

F32 = jnp.float32
BF16 = jnp.bfloat16
MXU_DTYPE = jnp.bfloat16
MESH = pl.DeviceIdType.MESH

D = 1024
DC = 512
DG = 512
NH = 4
HV = 128
HK = 64
DK = 256
RK = 16
CH = 64
DFF = 2816
DIN = 3104
DINP = 3200
LRW = 128
DEPTH = 4
EPS = 1e-6
QSCALE = HK ** -0.5
GATE_NORM = 1.0 / 16.0
CB_GB, CB_GC, CB_GV, CB_GO = 0, 1, 2, 5
CB_Q, CB_K = 6, 7
CB_V = 4
CB_LR = 24
LR = 0.001
B1 = 0.9
B2 = 0.999
AEPS = 1e-08
WD = 0.01
STEP = 10
TM_PROJ = 1024
TL_GLA = 512
TL_MIX = 256
TL_MIX_OUT = 512
TL_ROW = 1024
TL_FFN = 256
TL_FFN2 = 512
VMEM_LIMIT = 56 * 1024 * 1024


def _cp(*sem):
    return pltpu.CompilerParams(dimension_semantics=sem if sem else None, vmem_limit_bytes=VMEM_LIMIT)


def _mm(a, b):
    return jnp.dot(a.astype(MXU_DTYPE), b.astype(MXU_DTYPE), preferred_element_type=F32)


def _mm_nt(a, b):
    return lax.dot_general(a.astype(MXU_DTYPE), b.astype(MXU_DTYPE), (((1,), (1,)), ((), ())),
                           preferred_element_type=F32)


def _mm_tn(a, b):
    return lax.dot_general(a.astype(MXU_DTYPE), b.astype(MXU_DTYPE), (((0,), (0,)), ((), ())),
                           preferred_element_type=F32)


def _mm_tri(tri, b):
    t = tri.astype(BF16)
    b1 = b.astype(BF16)
    r1 = b - b1.astype(F32)
    b2 = r1.astype(BF16)
    b3 = (r1 - b2.astype(F32)).astype(BF16)
    dot = lambda u: jnp.dot(t, u, preferred_element_type=F32)
    return dot(b1) + dot(b2) + dot(b3)


def _rms(x, g):
    r = lax.rsqrt(jnp.mean(x * x, axis=-1, keepdims=True) + EPS)
    return x * r * g


def _rms_bwd(dout, y, g):
    r = lax.rsqrt(jnp.mean(y * y, axis=-1, keepdims=True) + EPS)
    yh = y * r
    dyh = dout * g
    dy = r * (dyh - yh * jnp.mean(dyh * yh, axis=-1, keepdims=True))
    dg = jnp.sum(dout * yh, axis=0, keepdims=True)
    return dy, dg


def _sigmoid(x):
    return 0.5 * jnp.tanh(0.5 * x) + 0.5


def _logsig(x):
    return jnp.minimum(x, 0.0) - jnp.log1p(jnp.exp(-jnp.abs(x)))


def _shifts(x, p8, n8):
    n = x.shape[0]
    xe = jnp.concatenate([p8, x, n8], axis=0)
    return pltpu.roll(xe, 1, 0)[8:8 + n], pltpu.roll(xe, n + 15, 0)[8:8 + n]


def _halo_rows(prev_ref, next_ref, i, last):
    hr = prev_ref.shape[0]
    p = jnp.where(i == 0, 0.0, prev_ref[...].astype(F32)[hr - 8:hr, :])
    n = jnp.where(i == last, 0.0, next_ref[...].astype(F32)[0:8, :])
    return p, n


def _conv3(x, xp, xn, w_ref):
    xm1, xp1 = _shifts(x, xp, xn)
    return w_ref[0:1, :] * xm1 + w_ref[1:2, :] * x + w_ref[2:3, :] * xp1, xm1, xp1


def _conv3_t(d, dp, dn, w_ref):
    dm1, dp1 = _shifts(d, dp, dn)
    return w_ref[0:1, :] * dp1 + w_ref[1:2, :] * d + w_ref[2:3, :] * dm1


HALO32 = 8
HALO16 = 16


def _prev_row_blk(i, tl, hr):
    return jnp.maximum(i * (tl // hr) - 1, 0)


def _next_row_blk(i, tl, nrows, hr):
    return jnp.minimum((i + 1) * (tl // hr), nrows // hr - 1)


def _prev_blk(tl, cb, hr=HALO32):
    return lambda i: (_prev_row_blk(i, tl, hr), cb)


def _next_blk(tl, nrows, cb, hr=HALO32):
    return lambda i: (_next_row_blk(i, tl, nrows, hr), cb)


def rms_matmul(x, g, w, tn, name, w_spec=None, n_out=None, out_dtype=F32, after=(), tm=TM_PROJ):
    L = x.shape[0]
    N = w.shape[1] if n_out is None else n_out
    tm = min(L, tm)
    if w_spec is None:
        w_spec = pl.BlockSpec((D, tn), lambda i, j: (0, j))

    def body(x_ref, g_ref, w_ref, *rest):
        o_ref, h_ref = rest[-2:]

        @pl.when(pl.program_id(1) == 0)
        def _():
            h_ref[...] = _rms(x_ref[...], g_ref[...]).astype(BF16)

        o_ref[...] = _mm(h_ref[...], w_ref[...]).astype(out_dtype)

    return pl.pallas_call(
        body, name=name, grid=(L // tm, N // tn),
        in_specs=[pl.BlockSpec((tm, D), lambda i, j: (i, 0)), pl.BlockSpec((1, D), lambda i, j: (0, 0)), w_spec]
        + [_ANY] * len(after),
        out_specs=[pl.BlockSpec((tm, tn), lambda i, j: (i, j)), pl.BlockSpec((tm, D), lambda i, j: (i, 0))],
        out_shape=[jax.ShapeDtypeStruct((L, N), out_dtype), jax.ShapeDtypeStruct((L, D), BF16)],
        compiler_params=_cp("parallel", "arbitrary"),
    )(x, g, w, *after)


def _gla_masks():
    def blk(shape, rdiv, cdiv):
        r = lax.broadcasted_iota(jnp.int32, shape, 0) // rdiv
        c = lax.broadcasted_iota(jnp.int32, shape, 1) // cdiv
        return (r == c).astype(F32)

    r = lax.broadcasted_iota(jnp.int32, (CH, CH), 0)
    c = lax.broadcasted_iota(jnp.int32, (CH, CH), 1)
    r4 = lax.broadcasted_iota(jnp.int32, (NH * CH, CH), 0) % CH
    c4 = lax.broadcasted_iota(jnp.int32, (NH * CH, CH), 1)
    return dict(
        bdq=blk((NH * CH, DK), CH, HK),
        bdo=blk((NH * CH, DG), CH, HV),
        bds=blk((DG, DK), HV, HK),
        tril=(r >= c).astype(F32), triu=(r <= c).astype(F32),
        tril4=r4 >= c4, triu4=r4 <= c4,
    )


def _tile4(x):
    return jnp.concatenate([x, x, x, x], axis=0)


def _gla_tile_prep(q, k, a, m, rev, nc):
    tri = m["triu"] if rev else m["tril"]
    chunks = [a[c * CH:(c + 1) * CH] for c in range(nc)]
    cum = jnp.concatenate([_mm_tri(tri, ac) for ac in chunks], axis=0)
    tot = jnp.concatenate([jnp.sum(ac, axis=0, keepdims=True) for ac in chunks], axis=0)
    tot_rows = jnp.concatenate([jnp.broadcast_to(tot[c:c + 1], (CH, DK)) for c in range(nc)], axis=0)
    e = jnp.exp(cum)
    einv = jnp.exp(-cum)
    eout = jnp.exp(tot_rows - cum)
    q, k = q.astype(F32), k.astype(F32)
    return dict(e=e, einv=einv, eout=eout, dec=jnp.exp(tot), qt=q * QSCALE * e, kt=k * einv, kh=k * eout)


def _gla_scores(qt16, kt16, m, rev):
    qs = _tile4(qt16) * m["bdq"].astype(qt16.dtype)
    return qs, jnp.where(m["triu4"] if rev else m["tril4"], _mm_nt(qs, kt16), 0.0)


def _gla_chunk_fwd(qt16, kt16, kh16, v, dec, st_ref, m, rev):
    _, sc = _gla_scores(qt16, kt16, m, rev)
    v16 = v.astype(BF16)
    r = _mm(sc, v16)
    o_intra = jnp.concatenate([r[h * CH:(h + 1) * CH, h * HV:(h + 1) * HV] for h in range(NH)], axis=1)
    st = st_ref[...]
    st16 = st.astype(BF16)
    o = o_intra + _mm_nt(qt16, st16)
    st_ref[...] = st * dec + _mm_tn(v16, kh16) * m["bds"]
    return o, st16


def _gates(lr_ref, gc_ref, bs_ref, cols):
    return _logsig(_mm(lr_ref[...], gc_ref[:, cols]) + bs_ref[:, cols]) * GATE_NORM


def gla_fwd(P, gcat, gbias, tl):
    L = P.shape[0]
    nb = L // tl
    nc = tl // CH

    def body(qf, kf, vf, lf, qb, kb, vb, lb, gc_ref, bs_ref, of, ob, sf, sb, stf, stb,
             qtf, ktf, khf, dcf, qtb, ktb, khb, dcb):
        @pl.when(pl.program_id(0) == 0)
        def _():
            stf[...] = jnp.zeros_like(stf)
            stb[...] = jnp.zeros_like(stb)

        m = _gla_masks()
        for (q, k, lr, cols, rev, qt, kt, kh, dc) in ((qf, kf, lf, slice(0, DK), False, qtf, ktf, khf, dcf),
                                                      (qb, kb, lb, slice(DK, 2 * DK), True, qtb, ktb, khb, dcb)):
            p = _gla_tile_prep(q[...], k[...], _gates(lr, gc_ref, bs_ref, cols), m, rev, nc)
            qt[...] = p["qt"].astype(BF16)
            kt[...] = p["kt"].astype(BF16)
            kh[...] = p["kh"].astype(BF16)
            dc[...] = p["dec"]

        def chunk(c, carry):
            rows = pl.ds(pl.multiple_of(c * CH, CH), CH)
            o, st = _gla_chunk_fwd(qtf[rows, :], ktf[rows, :], khf[rows, :], vf[rows, :], dcf[pl.ds(c, 1), :], stf, m, False)
            of[rows, :] = o.astype(BF16)
            sf[c] = st
            cb = nc - 1 - c
            rows = pl.ds(pl.multiple_of(cb * CH, CH), CH)
            o, st = _gla_chunk_fwd(qtb[rows, :], ktb[rows, :], khb[rows, :], vb[rows, :], dcb[pl.ds(cb, 1), :], stb, m, True)
            ob[rows, :] = o.astype(BF16)
            sb[cb] = st
            return carry

        lax.fori_loop(0, nc, chunk, 0, unroll=2)

    fw = lambda cb: (lambda i: (i, cb))
    bw = lambda cb: (lambda i: (nb - 1 - i, cb))
    return pl.pallas_call(
        body, name="gla_fwd", grid=(nb,),
        in_specs=[pl.BlockSpec((tl, DK), fw(CB_Q)), pl.BlockSpec((tl, DK), fw(CB_K)), pl.BlockSpec((tl, DG), fw(CB_V)),
                  pl.BlockSpec((tl, LRW), fw(CB_LR)),
                  pl.BlockSpec((tl, DK), bw(CB_Q)), pl.BlockSpec((tl, DK), bw(CB_K)), pl.BlockSpec((tl, DG), bw(CB_V)),
                  pl.BlockSpec((tl, LRW), bw(CB_LR)),
                  pl.BlockSpec((LRW, 2 * DK), lambda i: (0, 0)), pl.BlockSpec((1, 2 * DK), lambda i: (0, 0))],
        out_specs=[pl.BlockSpec((tl, DG), lambda i: (i, 0)), pl.BlockSpec((tl, DG), lambda i: (nb - 1 - i, 0)),
                   pl.BlockSpec((nc, DG, DK), lambda i: (i, 0, 0)), pl.BlockSpec((nc, DG, DK), lambda i: (nb - 1 - i, 0, 0))],
        out_shape=[jax.ShapeDtypeStruct((L, DG), BF16), jax.ShapeDtypeStruct((L, DG), BF16),
                   jax.ShapeDtypeStruct((L // CH, DG, DK), BF16), jax.ShapeDtypeStruct((L // CH, DG, DK), BF16)],
        scratch_shapes=[pltpu.VMEM((DG, DK), F32), pltpu.VMEM((DG, DK), F32)]
        + [pltpu.VMEM((tl, DK), BF16)] * 3 + [pltpu.VMEM((nc, DK), F32)]
        + [pltpu.VMEM((tl, DK), BF16)] * 3 + [pltpu.VMEM((nc, DK), F32)],
        compiler_params=_cp("arbitrary"),
    )(P, P, P, P, P, P, P, P, gcat, gbias)


def _headnorm(o):
    oh, rs = [], []
    for h in range(NH):
        oo = o[:, h * HV:(h + 1) * HV]
        r = lax.rsqrt(jnp.mean(oo * oo, axis=-1, keepdims=True) + EPS)
        oh.append(oo * r)
        rs.append(r)
    return jnp.concatenate(oh, axis=1), rs


def mix_out(P, o_f, o_b, conv_a, ghn4, w_out, g2, x, tl):
    L = P.shape[0]
    nt = L // tl

    def body(gb, gc, gv, go, gcp, gvp, gcn, gvn, of, ob, ca, gh, wo, g2r, xr, ycat, yr, x1):
        i = pl.program_id(0)
        cp, cn = _halo_rows(gcp, gcn, i, nt - 1)
        vp, vn = _halo_rows(gvp, gvn, i, nt - 1)
        c = gc[...].astype(F32) * gv[...].astype(F32)
        cc, _, _ = _conv3(c, cp * vp, cn * vn, ca)
        ya = gb[...].astype(F32) * cc
        oh, _ = _headnorm(of[...].astype(F32) + ob[...].astype(F32))
        g = go[...].astype(F32)
        yb = g * _sigmoid(g) * (oh * gh[...])
        yc = jnp.concatenate([ya, yb], axis=1).astype(BF16)
        ycat[...] = yc
        y = _mm(yc, wo[...])
        yr[...] = y
        x1[...] = xr[...] + _rms(y, g2r[...])

    t = lambda cb: pl.BlockSpec((tl, DC), lambda i: (i, cb))
    hp = lambda cb: pl.BlockSpec((HALO16, DC), _prev_blk(tl, cb, HALO16))
    hn = lambda cb: pl.BlockSpec((HALO16, DC), _next_blk(tl, L, cb, HALO16))
    row = lambda n: pl.BlockSpec((tl, n), lambda i: (i, 0))
    full = lambda a: pl.BlockSpec(a.shape, lambda i: (0, 0))
    return pl.pallas_call(
        body, name="mix_out", grid=(nt,),
        in_specs=[t(CB_GB), t(CB_GC), t(CB_GV), t(CB_GO), hp(CB_GC), hp(CB_GV), hn(CB_GC), hn(CB_GV),
                  row(DG), row(DG), full(conv_a), full(ghn4), full(w_out), full(g2), row(D)],
        out_specs=[row(D), row(D), row(D)],
        out_shape=[jax.ShapeDtypeStruct((L, D), BF16), jax.ShapeDtypeStruct((L, D), F32),
                   jax.ShapeDtypeStruct((L, D), F32)],
        compiler_params=_cp("parallel"),
    )(P, P, P, P, P, P, P, P, o_f, o_b, conv_a, ghn4, w_out, g2, x)


NFF = 2
WFF = DFF // NFF
FFN_COL_CHUNKS = ((0, 512), (512, 1024), (1024, WFF))


def ffn_down(U, conv_ffn, w_down, g4, x1, tl):
    L = U.shape[0]
    nt = L // tl

    def body(u, up, un, cf, wd, g4r, x1r, y2, x2, ug, uv, zr):
        i = pl.program_id(0)
        acc = jnp.zeros((tl, D), F32)
        for j in range(NFF):
            gs = slice(j * WFF, (j + 1) * WFF)
            vs = slice(DFF + j * WFF, DFF + (j + 1) * WFF)
            z = []
            for s in (gs, vs):
                p, n = _halo_rows(up.at[:, s], un.at[:, s], i, nt - 1)
                z.append(_conv3(u[:, s].astype(F32), p, n, cf.at[:, s])[0])
            zz = (z[0] * _sigmoid(z[0]) * z[1]).astype(BF16)
            ug[:, gs] = z[0].astype(BF16)
            uv[:, gs] = z[1].astype(BF16)
            zr[:, gs] = zz
            acc = acc + _mm(zz, wd[gs, :])
        y2[...] = acc
        x2[...] = x1r[...] + _rms(acc, g4r[...])

    row = lambda n: pl.BlockSpec((tl, n), lambda i: (i, 0))
    full = lambda a: pl.BlockSpec(a.shape, lambda i: (0, 0))
    half = jax.ShapeDtypeStruct((L, DFF), BF16)
    return pl.pallas_call(
        body, name="ffn_down", grid=(nt,),
        in_specs=[row(2 * DFF), pl.BlockSpec((HALO16, 2 * DFF), _prev_blk(tl, 0, HALO16)),
                  pl.BlockSpec((HALO16, 2 * DFF), _next_blk(tl, L, 0, HALO16)),
                  full(conv_ffn), full(w_down), full(g4), row(D)],
        out_specs=[row(D), row(D), row(DFF), row(DFF), row(DFF)],
        out_shape=[jax.ShapeDtypeStruct((L, D), F32), jax.ShapeDtypeStruct((L, D), F32), half, half, half],
        compiler_params=_cp("parallel"),
    )(U, U, U, conv_ffn, w_down, g4, x1)


def ffn_fwd(x1, g3, w_up, conv_ffn, w_down, g4, tl):
    L = x1.shape[0]
    nt = L // tl
    hh = HALO16

    def body(xr, xp, xn, g3r, wu, cf, wd, g4r, U, h2, ug, uv, zr, y2, x2):
        i = pl.program_id(0)
        he = _rms(jnp.concatenate([xp[...], xr[...], xn[...]], axis=0), g3r[...]).astype(BF16)
        h2[...] = he[hh:hh + tl]
        acc = jnp.zeros((tl, D), F32)
        for j in range(NFF):
            conv = []
            for blk, off in ((j, j * WFF), (NFF + j, DFF + j * WFF)):
                ue = _mm(he, wu[blk])
                p8 = jnp.where(i == 0, 0.0, ue[hh - 8:hh])
                n8 = jnp.where(i == nt - 1, 0.0, ue[hh + tl:hh + tl + 8])
                mid = ue[hh:hh + tl]
                U[:, off:off + WFF] = mid.astype(BF16)
                conv.append(_conv3(mid, p8, n8, cf.at[:, off:off + WFF])[0])
            gs = slice(j * WFF, (j + 1) * WFF)
            zz = (conv[0] * _sigmoid(conv[0]) * conv[1]).astype(BF16)
            ug[:, gs] = conv[0].astype(BF16)
            uv[:, gs] = conv[1].astype(BF16)
            zr[:, gs] = zz
            acc = acc + _mm(zz, wd[gs, :])
        y2[...] = acc
        x2[...] = xr[...] + _rms(acc, g4r[...])

    row = lambda n: pl.BlockSpec((tl, n), lambda i: (i, 0))
    full = lambda a: pl.BlockSpec(a.shape, lambda i: (0,) * a.ndim)
    once = lambda a: pl.BlockSpec(a.shape, lambda i: (0,) * a.ndim, pipeline_mode=pl.Buffered(1))
    half = jax.ShapeDtypeStruct((L, DFF), BF16)
    return pl.pallas_call(
        body, name="ffn_fwd", grid=(nt,),
        in_specs=[row(D), pl.BlockSpec((hh, D), lambda i: (_prev_row_blk(i, tl, hh), 0)),
                  pl.BlockSpec((hh, D), lambda i: (_next_row_blk(i, tl, L, hh), 0)),
                  full(g3), once(w_up), full(conv_ffn), once(w_down), full(g4)],
        out_specs=[row(2 * DFF), row(D), row(DFF), row(DFF), row(DFF), row(D), row(D)],
        out_shape=[jax.ShapeDtypeStruct((L, 2 * DFF), BF16), jax.ShapeDtypeStruct((L, D), BF16), half, half, half,
                   jax.ShapeDtypeStruct((L, D), F32), jax.ShapeDtypeStruct((L, D), F32)],
        compiler_params=_cp("parallel"),
    )(x1, x1, x1, g3, w_up, conv_ffn, w_down, g4)


def loss_head(y, target, tl):
    L = y.shape[0]

    def body(yr, tr, dy, ls):
        @pl.when(pl.program_id(0) == 0)
        def _():
            ls[...] = jnp.zeros_like(ls)

        err = yr[...] - tr[...]
        dy[...] = err * (1.0 / D)
        ls[...] += (0.5 / D) * jnp.sum(err * err)

    row = pl.BlockSpec((tl, D), lambda i: (i, 0))
    return pl.pallas_call(
        body, name="loss_head", grid=(L // tl,), in_specs=[row, row],
        out_specs=[row, pl.BlockSpec((8, 128), lambda i: (0, 0))],
        out_shape=[jax.ShapeDtypeStruct((L, D), F32), jax.ShapeDtypeStruct((8, 128), F32)],
        compiler_params=_cp("arbitrary"),
    )(y, target)


def rms_bwd_pre(dout, y, g, tl, after=()):
    L = y.shape[0]

    def body(dr, yr, gr, *rest):
        dy, dg = rest[-2:]

        @pl.when(pl.program_id(0) == 0)
        def _():
            dg[...] = jnp.zeros_like(dg)

        a, b = _rms_bwd(dr[...], yr[...], gr[...])
        dy[...] = a.astype(BF16)
        dg[...] += b

    row = pl.BlockSpec((tl, D), lambda i: (i, 0))
    vec = pl.BlockSpec((1, D), lambda i: (0, 0))
    return pl.pallas_call(
        body, name="rms_bwd_pre", grid=(L // tl,), in_specs=[row, row, vec] + [_ANY] * len(after), out_specs=[row, vec],
        out_shape=[jax.ShapeDtypeStruct((L, D), BF16), jax.ShapeDtypeStruct((1, D), F32)],
        compiler_params=_cp("arbitrary"),
    )(dout, y, g, *after)


def ffn_bwd1(dy2, ug, uv, w_down, tl):
    L = ug.shape[0]

    def body(dy, ugr, uvr, wd, dug, duv):
        a = ugr[...].astype(F32)
        b = uvr[...].astype(F32)
        sg = _sigmoid(a)
        silu = a * sg
        dz = _mm_nt(dy[...], wd[...])
        dug[...] = (dz * b * (sg + silu * (1.0 - sg))).astype(BF16)
        duv[...] = (dz * silu).astype(BF16)

    tile = pl.BlockSpec((tl, WFF), lambda j, i: (i, j))
    half = jax.ShapeDtypeStruct((L, DFF), BF16)
    return pl.pallas_call(
        body, name="ffn_bwd1", grid=(NFF, L // tl),
        in_specs=[pl.BlockSpec((tl, D), lambda j, i: (i, 0)), tile, tile, pl.BlockSpec((WFF, D), lambda j, i: (j, 0))],
        out_specs=[tile, tile], out_shape=[half, half],
        compiler_params=_cp("parallel", "parallel"),
    )(dy2, ug, uv, w_down)


def ffn_bwd2(du_g, du_v, U, conv_ffn, w_up, x1, dres, g3, tl):
    L = x1.shape[0]
    nt = L // tl

    def body(dg_, dv_, dgp, dgn, dvp, dvn, ugr, uvr, cg, cv, wg, wv, x1r, drr, g3r, dUg, dUv, dx1, dg3, dcg, dcv, acc):
        i = pl.program_id(0)
        j = pl.program_id(1)

        @pl.when((i == 0) & (j == 0))
        def _():
            dg3[...] = jnp.zeros_like(dg3)
            dcg[...] = jnp.zeros_like(dcg)
            dcv[...] = jnp.zeros_like(dcv)

        part = None
        for c0, c1 in FFN_COL_CHUNKS:
            cs = slice(c0, c1)
            for d_ref, dp_ref, dn_ref, cw_ref, u_ref, dc, dU, w in ((dg_, dgp, dgn, cg, ugr, dcg, dUg, wg),
                                                                    (dv_, dvp, dvn, cv, uvr, dcv, dUv, wv)):
                p8, n8 = _halo_rows(dp_ref.at[:, cs], dn_ref.at[:, cs], i, nt - 1)
                d = d_ref[:, cs].astype(F32)
                dm1, dp1 = _shifts(d, p8, n8)
                du = (cw_ref[0:1, cs] * dp1 + cw_ref[1:2, cs] * d + cw_ref[2:3, cs] * dm1).astype(BF16)
                dU[:, cs] = du
                u = u_ref[:, cs].astype(F32)
                for k, t in enumerate((dp1, d, dm1)):
                    dc[j, k:k + 1, cs] += jnp.sum(t * u, axis=0, keepdims=True)
                term = _mm_nt(du, w[:, cs])
                part = term if part is None else part + term

        @pl.when(j == 0)
        def _():
            acc[...] = part

        @pl.when(j > 0)
        def _():
            acc[...] += part

        @pl.when(j == NFF - 1)
        def _():
            dx, dg = _rms_bwd(acc[...], x1r[...], g3r[...])
            dx1[...] = drr[...] + dx
            dg3[...] += dg

    tile = pl.BlockSpec((tl, WFF), lambda i, j: (i, j))
    prev = pl.BlockSpec((HALO16, WFF), lambda i, j: (_prev_row_blk(i, tl, HALO16), j))
    nxt = pl.BlockSpec((HALO16, WFF), lambda i, j: (_next_row_blk(i, tl, L, HALO16), j))
    cw = lambda off: pl.BlockSpec((3, WFF), lambda i, j: (0, off + j))
    ww = lambda off: pl.BlockSpec((None, D, WFF), lambda i, j: (off + j, 0, 0))
    row = pl.BlockSpec((tl, D), lambda i, j: (i, 0))
    vec = pl.BlockSpec((1, D), lambda i, j: (0, 0))
    ut = lambda off: pl.BlockSpec((tl, WFF), lambda i, j: (i, off + j))
    dcs = pl.BlockSpec((NFF, 8, WFF), lambda i, j: (0, 0, 0))
    return pl.pallas_call(
        body, name="ffn_bwd2", grid=(nt, NFF),
        in_specs=[tile, tile, prev, nxt, prev, nxt, ut(0), ut(NFF), cw(0), cw(NFF), ww(0), ww(NFF), row, row, vec],
        out_specs=[tile, tile, row, vec, dcs, dcs],
        out_shape=[jax.ShapeDtypeStruct((L, DFF), BF16), jax.ShapeDtypeStruct((L, DFF), BF16),
                   jax.ShapeDtypeStruct((L, D), F32), jax.ShapeDtypeStruct((1, D), F32),
                   jax.ShapeDtypeStruct((NFF, 8, WFF), F32), jax.ShapeDtypeStruct((NFF, 8, WFF), F32)],
        scratch_shapes=[pltpu.VMEM((tl, D), F32)],
        compiler_params=_cp("arbitrary", "arbitrary"),
    )(du_g, du_v, du_g, du_g, du_v, du_v, U, U, conv_ffn, conv_ffn, w_up, w_up, x1, dres, g3)


def matmul_tn(a, b, ta, tn, tl, name, into=None, after=()):
    L, Ka = a.shape
    N = b.shape[1]

    def body(ar, br, *rest):
        o = rest[-1]

        @pl.when(pl.program_id(2) == 0)
        def _():
            o[...] = jnp.zeros_like(o)

        o[...] += _mm_tn(ar[...], br[...]).reshape(o.shape)

    in_specs = [pl.BlockSpec((tl, ta), lambda p, q, l: (l, p)), pl.BlockSpec((tl, tn), lambda p, q, l: (l, q))]
    if into is None:
        return pl.pallas_call(
            body, name=name, grid=(Ka // ta, N // tn, L // tl), in_specs=in_specs + [_ANY] * len(after),
            out_specs=pl.BlockSpec((ta, tn), lambda p, q, l: (p, q)),
            out_shape=jax.ShapeDtypeStruct((Ka, N), F32),
            compiler_params=_cp("parallel", "parallel", "arbitrary"),
        )(a, b, *after)
    buf, blk, idx = into
    return pl.pallas_call(
        body, name=name, grid=(Ka // ta, N // tn, L // tl), in_specs=in_specs + [_ANY],
        out_specs=pl.BlockSpec(blk, lambda p, q, l: idx(p, q)),
        out_shape=jax.ShapeDtypeStruct(buf.shape, F32), input_output_aliases={2: 0},
        compiler_params=_cp("parallel", "parallel", "arbitrary"),
    )(a, b, buf)


def mix_bwd1(dy, w_out, P, o_f, o_b, conv_a, ghn4, tl):
    L = P.shape[0]
    nt = L // tl

    def body(dyr, wo, gb, gc, gv, go, gcp, gvp, gcn, gvn, of, ob, ca, gh, dgb, dcc, dgo, do, dca, dgh):
        i = pl.program_id(0)

        @pl.when(i == 0)
        def _():
            dca[...] = jnp.zeros_like(dca)
            dgh[...] = jnp.zeros_like(dgh)

        dycat = _mm_nt(dyr[...], wo[...])
        dya = dycat[:, 0:DC]
        dyb = dycat[:, DC:D]
        cp, cn = _halo_rows(gcp, gcn, i, nt - 1)
        vp, vn = _halo_rows(gvp, gvn, i, nt - 1)
        c = gc[...].astype(F32) * gv[...].astype(F32)
        cc, c_m1, c_p1 = _conv3(c, cp * vp, cn * vn, ca)
        dgb[...] = (dya * cc).astype(BF16)
        d = dya * gb[...].astype(F32)
        dcc[...] = d.astype(BF16)
        for k, s in enumerate((c_m1, c, c_p1)):
            dca[k:k + 1, :] += jnp.sum(d * s, axis=0, keepdims=True)
        oh, rs = _headnorm(of[...].astype(F32) + ob[...].astype(F32))
        g = go[...].astype(F32)
        sg = _sigmoid(g)
        silu = g * sg
        dgo[...] = (dyb * (oh * gh[...]) * (sg * (1.0 + g * (1.0 - sg)))).astype(BF16)
        don = dyb * silu
        t = jnp.sum(don * oh, axis=0, keepdims=True)
        dgh[0:1, :] += t[:, 0:HV] + t[:, HV:2 * HV] + t[:, 2 * HV:3 * HV] + t[:, 3 * HV:4 * HV]
        doh = don * gh[...]
        parts = []
        for h in range(NH):
            hs = slice(h * HV, (h + 1) * HV)
            parts.append(rs[h] * (doh[:, hs] - oh[:, hs] * jnp.mean(doh[:, hs] * oh[:, hs], axis=-1, keepdims=True)))
        do[...] = jnp.concatenate(parts, axis=1).astype(BF16)

    t = lambda cb: pl.BlockSpec((tl, DC), lambda i: (i, cb))
    hp = lambda cb: pl.BlockSpec((HALO16, DC), _prev_blk(tl, cb, HALO16))
    hn = lambda cb: pl.BlockSpec((HALO16, DC), _next_blk(tl, L, cb, HALO16))
    row = lambda n: pl.BlockSpec((tl, n), lambda i: (i, 0))
    full = lambda a: pl.BlockSpec(a.shape, lambda i: (0, 0))
    act16 = lambda n: jax.ShapeDtypeStruct((L, n), BF16)
    return pl.pallas_call(
        body, name="mix_bwd1", grid=(nt,),
        in_specs=[row(D), full(w_out), t(CB_GB), t(CB_GC), t(CB_GV), t(CB_GO), hp(CB_GC), hp(CB_GV), hn(CB_GC), hn(CB_GV),
                  row(DG), row(DG), full(conv_a), full(ghn4)],
        out_specs=[row(DC), row(DC), row(DG), row(DG), pl.BlockSpec((8, DC), lambda i: (0, 0)),
                   pl.BlockSpec((8, HV), lambda i: (0, 0))],
        out_shape=[act16(DC), act16(DC), act16(DG), act16(DG), jax.ShapeDtypeStruct((8, DC), F32),
                   jax.ShapeDtypeStruct((8, HV), F32)],
        compiler_params=_cp("arbitrary"),
    )(dy, w_out, P, P, P, P, P, P, P, P, o_f, o_b, conv_a, ghn4)


def _gla_chunk_bwd(qt, kt, kh, v, do, st16, dec, g_ref, m, rev):
    qt16, kt16, kh16, v16, do16 = (t.astype(BF16) for t in (qt, kt, kh, v, do))
    qs, sc = _gla_scores(qt16, kt16, m, rev)
    g = g_ref[...]
    g16 = g.astype(BF16)
    dob = _tile4(do16) * m["bdo"].astype(BF16)
    dv = _mm_tn(sc, dob) + _mm_nt(kh16, g16)
    dsc = jnp.where(m["triu4"] if rev else m["tril4"], _mm_nt(dob, v16), 0.0)
    r1 = _mm(dsc, kt16) * m["bdq"]
    dqt = r1[0:CH] + r1[CH:2 * CH] + r1[2 * CH:3 * CH] + r1[3 * CH:4 * CH] + _mm(do16, st16)
    dkt = _mm_tn(dsc, qs)
    dkh = _mm(v16, g16)
    dd = jnp.sum(g * st16.astype(F32), axis=0, keepdims=True)
    g_ref[...] = g * dec + _mm_tn(do16, qt16) * m["bds"]
    return dv, dqt, dkt, dkh, dd


def gla_bwd(P, do, sf, sb, gcat, gbias, tl):
    L = P.shape[0]
    nb = L // tl
    nc = tl // CH

    def body(qf, kf, vf, lf, dof, sfr, qb, kb, vb, lb, dob, sbr, gc_ref, bs_ref,
             dqf, dkf, dvf, daf, dqb, dkb, dvb, dab, gf, gbk, *scr):
        @pl.when(pl.program_id(0) == 0)
        def _():
            gf[...] = jnp.zeros_like(gf)
            gbk[...] = jnp.zeros_like(gbk)

        m = _gla_masks()
        keys = ("qt", "kt", "kh", "e", "einv", "eout", "dec")
        names = keys + ("dd", "dqt", "dkt", "dkh")
        pf = dict(zip(names, scr[0:11]))
        pb = dict(zip(names, scr[11:22]))
        for (q, k, lr, cols, rev, pr) in ((qf, kf, lf, slice(0, DK), False, pf), (qb, kb, lb, slice(DK, 2 * DK), True, pb)):
            p = _gla_tile_prep(q[...], k[...], _gates(lr, gc_ref, bs_ref, cols), m, rev, nc)
            for key in keys:
                pr[key][...] = p[key]

        def step(c, v, dor, st, g_ref, pr, dv, rev):
            rows = pl.ds(pl.multiple_of(c * CH, CH), CH)
            dvc, dqt, dkt, dkh, dd = _gla_chunk_bwd(pr["qt"][rows, :], pr["kt"][rows, :], pr["kh"][rows, :], v[rows, :],
                                                    dor[rows, :], st[c], pr["dec"][pl.ds(c, 1), :], g_ref, m, rev)
            dv[rows, :] = dvc.astype(BF16)
            pr["dqt"][rows, :] = dqt
            pr["dkt"][rows, :] = dkt
            pr["dkh"][rows, :] = dkh
            pr["dd"][pl.ds(c, 1), :] = dd

        def chunk(c, carry):
            step(nc - 1 - c, vf, dof, sfr, gf, pf, dvf, False)
            step(c, vb, dob, sbr, gbk, pb, dvb, True)
            return carry

        lax.fori_loop(0, nc, chunk, 0, unroll=2)

        def finish(pr, dq, dk, da, rev):
            dqt, dkt, dkh = pr["dqt"][...], pr["dkt"][...], pr["dkh"][...]
            kk = dkh * pr["kh"][...]
            dcum = dqt * pr["qt"][...] - dkt * pr["kt"][...] - kk
            dtot = pr["dd"][...] * pr["dec"][...]
            tri_t = m["tril"] if rev else m["triu"]
            parts = []
            for c in range(nc):
                rs = slice(c * CH, (c + 1) * CH)
                parts.append(_mm_tri(tri_t, dcum[rs]) + (jnp.sum(kk[rs], axis=0, keepdims=True) + dtot[c:c + 1]))
            da[...] = jnp.concatenate(parts, axis=0).astype(BF16)
            dq[...] = (dqt * pr["e"][...] * QSCALE).astype(BF16)
            dk[...] = (dkt * pr["einv"][...] + dkh * pr["eout"][...]).astype(BF16)

        finish(pf, dqf, dkf, daf, False)
        finish(pb, dqb, dkb, dab, True)

    fwd_dir = lambda cb: (lambda i: (nb - 1 - i, cb))
    bwd_dir = lambda cb: (lambda i: (i, cb))

    def side(ix):
        return [pl.BlockSpec((tl, DK), ix(CB_Q)), pl.BlockSpec((tl, DK), ix(CB_K)), pl.BlockSpec((tl, DG), ix(CB_V)),
                pl.BlockSpec((tl, LRW), ix(CB_LR)), pl.BlockSpec((tl, DG), ix(0)),
                pl.BlockSpec((nc, DG, DK), lambda i: (ix(0)(i)[0], 0, 0))]

    def outs(ix):
        return [pl.BlockSpec((tl, DK), ix(0)), pl.BlockSpec((tl, DK), ix(0)), pl.BlockSpec((tl, DG), ix(0)),
                pl.BlockSpec((tl, DK), ix(0))]

    o_shape = [jax.ShapeDtypeStruct((L, DK), BF16), jax.ShapeDtypeStruct((L, DK), BF16),
               jax.ShapeDtypeStruct((L, DG), BF16), jax.ShapeDtypeStruct((L, DK), BF16)]
    return pl.pallas_call(
        body, name="gla_bwd", grid=(nb,),
        in_specs=side(fwd_dir) + side(bwd_dir) + [pl.BlockSpec((LRW, 2 * DK), lambda i: (0, 0)),
                                                  pl.BlockSpec((1, 2 * DK), lambda i: (0, 0))],
        out_specs=outs(fwd_dir) + outs(bwd_dir),
        out_shape=o_shape + o_shape,
        scratch_shapes=[pltpu.VMEM((DG, DK), F32), pltpu.VMEM((DG, DK), F32)]
        + ([pltpu.VMEM((tl, DK), F32)] * 6 + [pltpu.VMEM((nc, DK), F32)] * 2 + [pltpu.VMEM((tl, DK), F32)] * 3) * 2,
        compiler_params=_cp("arbitrary"),
    )(P, P, P, P, do, sf, P, P, P, P, do, sb, gcat, gbias)


def mix_bwd2(dgb, dcc, dgo, gl, P, conv_a, gcat, gbias, w_in, x, dres, g1, tl):
    L = P.shape[0]
    nt = L // tl

    def body(dgbr, dccr, dccp, dccn, dgor, dqf, dkf, dvf, daf, dqb, dkb, dvb, dab, gc, gv, lr, ca, gcr, bsr, wi,
             xr, drr, g1r, dP, dx, dg1, dgcat, dbias):
        i = pl.program_id(0)

        @pl.when(i == 0)
        def _():
            dg1[...] = jnp.zeros_like(dg1)
            dgcat[...] = jnp.zeros_like(dgcat)
            dbias[...] = jnp.zeros_like(dbias)

        p, n = _halo_rows(dccp, dccn, i, nt - 1)
        dc = _conv3_t(dccr[...].astype(F32), p, n, ca)
        pre = _mm(lr[...], gcr[...]) + bsr[...]
        da = jnp.concatenate([daf[...], dab[...]], axis=1).astype(F32)
        add32 = lambda a, b: a[...].astype(F32) + b[...].astype(F32)
        dpre = da * GATE_NORM * (1.0 - _sigmoid(pre))
        dpre16 = dpre.astype(BF16)
        dP[:, 0:DC] = dgbr[...].astype(BF16)
        dP[:, DC:2 * DC] = (dc * gv[...].astype(F32)).astype(BF16)
        dP[:, 2 * DC:3 * DC] = (dc * gc[...].astype(F32)).astype(BF16)
        dP[:, 1536:1792] = add32(dqf, dqb).astype(BF16)
        dP[:, 1792:2048] = add32(dkf, dkb).astype(BF16)
        dP[:, 2048:2560] = add32(dvf, dvb).astype(BF16)
        dP[:, 2560:3072] = dgor[...].astype(BF16)
        dP[:, 3072:3200] = _mm_nt(dpre16, gcr[...]).astype(BF16)
        dgcat[...] += _mm_tn(lr[...], dpre16)
        dbias[0:1, :] += jnp.sum(dpre, axis=0, keepdims=True)
        dh, dg = _rms_bwd(_mm_nt(dP[...], wi[...]), xr[...], g1r[...])
        dx[...] = drr[...] + dh
        dg1[...] += dg

    row = lambda n: pl.BlockSpec((tl, n), lambda i: (i, 0))
    t = lambda w, cb: pl.BlockSpec((tl, w), lambda i: (i, cb))
    full = lambda a: pl.BlockSpec(a.shape, lambda i: (0, 0))
    return pl.pallas_call(
        body, name="mix_bwd2", grid=(nt,),
        in_specs=[row(DC), row(DC), pl.BlockSpec((HALO16, DC), _prev_blk(tl, 0, HALO16)),
                  pl.BlockSpec((HALO16, DC), _next_blk(tl, L, 0, HALO16)),
                  row(DG), row(DK), row(DK), row(DG), row(DK), row(DK), row(DK), row(DG), row(DK),
                  t(DC, CB_GC), t(DC, CB_GV), t(LRW, CB_LR), full(conv_a), full(gcat), full(gbias), full(w_in),
                  row(D), row(D), full(g1)],
        out_specs=[row(DINP), row(D), pl.BlockSpec((1, D), lambda i: (0, 0)), pl.BlockSpec((LRW, 2 * DK), lambda i: (0, 0)),
                   pl.BlockSpec((8, 2 * DK), lambda i: (0, 0))],
        out_shape=[jax.ShapeDtypeStruct((L, DINP), BF16), jax.ShapeDtypeStruct((L, D), F32),
                   jax.ShapeDtypeStruct((1, D), F32), jax.ShapeDtypeStruct((LRW, 2 * DK), F32),
                   jax.ShapeDtypeStruct((8, 2 * DK), F32)],
        compiler_params=_cp("arbitrary"),
    )(dgb, dcc, dcc, dcc, dgo, *gl, P, P, P, conv_a, gcat, gbias, w_in, x, dres, g1)


def _row_tile(rows, cols):
    if rows * cols * 4 <= 2 * 1024 * 1024:
        return rows
    best = 8
    for t in range(8, rows, 8):
        if rows % t == 0 and t * cols * 4 <= 2 * 1024 * 1024:
            best = t
    return best


def adamw(w, g, m, v, name):
    shape = w.shape
    cols = shape[-1]
    w2, g2, m2, v2 = (a.reshape(-1, cols) for a in (w, g, m, v))
    rows = w2.shape[0]
    tr = _row_tile(rows, cols)

    def body(wr, gr, mr, vr, dl, nm, nv):
        gg = gr[...]
        mm = B1 * mr[...] + (1.0 - B1) * gg
        vv = B2 * vr[...] + (1.0 - B2) * (gg * gg)
        m_hat = mm / (1.0 - B1 ** STEP)
        v_hat = vv / (1.0 - B2 ** STEP)
        dl[...] = -LR * (m_hat / (jnp.sqrt(v_hat) + AEPS) + WD * wr[...])
        nm[...] = mm
        nv[...] = vv

    blk = pl.BlockSpec((tr, cols), lambda i: (i, 0))
    o = jax.ShapeDtypeStruct((rows, cols), F32)
    d, nm, nv = pl.pallas_call(
        body, name=name, grid=(rows // tr,), in_specs=[blk] * 4, out_specs=[blk] * 3, out_shape=[o, o, o],
        compiler_params=_cp("parallel"),
    )(w2, g2, m2, v2)
    return d.reshape(shape), nm.reshape(shape), nv.reshape(shape)


def _place():
    return lax.axis_index("x"), lax.axis_index("y"), lax.axis_index("c")


def allgather8(v, name):
    mp, n = v.shape

    def body(x_ref, out_ref, send_sems, recv_sems, local_sem):
        x, y, c = _place()
        me, sibling = (x, y, c), (x, y, 1 - c)
        chips = [(1 - x, y), (x, 1 - y), (1 - x, 1 - y)]

        def rows(px, py, pc):
            return out_ref.at[pl.ds((4 * px + 2 * py + pc) * mp, mp), :]

        def copy(k, block, to, src=None):
            return pltpu.make_async_remote_copy(
                src_ref=rows(*block) if src is None else src, dst_ref=rows(*block),
                send_sem=send_sems.at[k], recv_sem=recv_sems.at[k], device_id=to, device_id_type=MESH)

        mine = pltpu.make_async_copy(x_ref, rows(*me), local_sem)
        mine.start()
        first = [copy(0, me, sibling, src=x_ref)]
        first += [copy(1 + j, me, (*chip, c), src=x_ref) for j, chip in enumerate(chips)]
        for cp in first:
            cp.start()
        passed = [copy(4 + j, (*chip, c), sibling) for j, chip in enumerate(chips)]
        for j, chip in enumerate(chips):
            copy(1 + j, (*chip, c), me).wait_recv()
            passed[j].start()
        copy(0, sibling, me).wait_recv()
        for j, chip in enumerate(chips):
            copy(4 + j, (*chip, 1 - c), me).wait_recv()
        for cp in first + passed:
            cp.wait_send()
        mine.wait()

    return pl.pallas_call(
        body, name=name, out_shape=jax.ShapeDtypeStruct((8 * mp, n), v.dtype),
        in_specs=[pl.BlockSpec(memory_space=pltpu.VMEM)], out_specs=pl.BlockSpec(memory_space=pltpu.VMEM),
        scratch_shapes=[pltpu.SemaphoreType.DMA((7,)), pltpu.SemaphoreType.DMA((7,)), pltpu.SemaphoreType.DMA],
        compiler_params=pltpu.CompilerParams(vmem_limit_bytes=VMEM_LIMIT),
    )(v)


def sum8(v, mp):
    def body(x_ref, o_ref):
        acc = x_ref[0:mp, :]
        for d in range(1, 8):
            acc = acc + x_ref[d * mp:(d + 1) * mp, :]
        o_ref[...] = acc

    return pl.pallas_call(body, name="sum8", out_shape=jax.ShapeDtypeStruct((mp, v.shape[1]), F32),
                          compiler_params=pltpu.CompilerParams(vmem_limit_bytes=VMEM_LIMIT))(v)


_ANY = pl.BlockSpec(memory_space=pl.ANY)


def _row_half(ref, lead, h):
    hr = ref.shape[-2] // 2
    return ref.at[(*lead, pl.ds(h * hr, hr), slice(None))]


def allgather_weights(slots):
    n = len(slots)

    def body(*refs):
        s_refs, o_refs, (send_sems, recv_sems) = refs[:n], refs[n:2 * n], refs[2 * n:]
        x, y, c = _place()
        me = 2 * x + y
        sibling = (x, y, 1 - c)
        chips = [(1 - x, y), (x, 1 - y), (1 - x, 1 - y)]

        def half(ref, slot, h):
            return _row_half(ref, (slot, slice(None)), h)

        def copy(k, src, dst, to):
            return pltpu.make_async_remote_copy(src_ref=src, dst_ref=dst, send_sem=send_sems.at[k],
                                                recv_sem=recv_sems.at[k], device_id=to, device_id_type=MESH)

        first = [copy(6 * a + k, half(s_refs[a], me, c), half(o_refs[a], me, c), (px, py, c))
                 for k, (px, py) in enumerate(chips) for a in range(n)]
        for cp in first:
            cp.start()
        passed = []
        for k, (px, py) in enumerate(chips):
            for a in range(n):
                got = half(o_refs[a], 2 * px + py, c)
                copy(6 * a + k, half(s_refs[a], me, c), got, (px, py, c)).wait_recv()
                cp = copy(6 * a + 3 + k, got, got, sibling)
                cp.start()
                passed.append(cp)
        for k, (px, py) in enumerate(chips):
            for a in range(n):
                got = half(o_refs[a], 2 * px + py, 1 - c)
                copy(6 * a + 3 + k, got, got, sibling).wait_recv()
        for cp in first + passed:
            cp.wait_send()

    return pl.pallas_call(
        body, name="allgather_weights", out_shape=[jax.ShapeDtypeStruct(s.shape, s.dtype) for s in slots],
        in_specs=[_ANY] * n, out_specs=[_ANY] * n, input_output_aliases={a: a for a in range(n)},
        scratch_shapes=[pltpu.SemaphoreType.DMA((6 * n,)), pltpu.SemaphoreType.DMA((6 * n,))],
    )(*slots)


_HBM = pl.BlockSpec(memory_space=pltpu.HBM)
_SEM = pl.BlockSpec(memory_space=pltpu.SEMAPHORE)
_EFFECT = pltpu.SideEffectType.DATAFLOW_SIDE_EFFECTING


def gather_start(slots, name, after=()):
    n = len(slots)
    na = len(after)

    def body(*refs):
        s_refs, send_sems, recv_sems, token = refs[:n], refs[n + na], refs[n + na + 1], refs[-1]
        x, y, c = _place()
        me = 2 * x + y
        for k, (px, py) in enumerate([(1 - x, y), (x, 1 - y), (1 - x, 1 - y)]):
            for a in range(n):
                pltpu.make_async_remote_copy(
                    src_ref=s_refs[a].at[me], dst_ref=s_refs[a].at[me], send_sem=send_sems.at[3 * a + k],
                    recv_sem=recv_sems.at[3 * a + k], device_id=(px, py, c), device_id_type=MESH).start()
        token[...] = jnp.zeros_like(token)

    out = pl.pallas_call(
        body, name=name,
        out_shape=(pltpu.SemaphoreType.DMA((3 * n,)), pltpu.SemaphoreType.DMA((3 * n,)),
                   *[pltpu.HBM(s.shape, s.dtype) for s in slots], jax.ShapeDtypeStruct((8, 128), F32)),
        in_specs=[_HBM] * n + [_ANY] * na, out_specs=(_SEM, _SEM, *[_HBM] * n, pl.BlockSpec(memory_space=pltpu.VMEM)),
        input_output_aliases={a: 2 + a for a in range(n)},
        compiler_params=pltpu.CompilerParams(has_side_effects=_EFFECT),
    )(*[pltpu.with_memory_space_constraint(s, pltpu.HBM) for s in slots], *after)
    return out[0], out[1], list(out[2:2 + n]), out[-1]


def gather_wait(send_sems, recv_sems, slots, after, name):
    n = len(slots)

    def body(*refs):
        s_refs, ssem, rsem = refs[:n], refs[n], refs[n + 1]
        x, y, c = _place()
        me = 2 * x + y
        for k, (px, py) in enumerate([(1 - x, y), (x, 1 - y), (1 - x, 1 - y)]):
            for a in range(n):
                cp = pltpu.make_async_remote_copy(
                    src_ref=s_refs[a].at[me], dst_ref=s_refs[a].at[2 * px + py], send_sem=ssem.at[3 * a + k],
                    recv_sem=rsem.at[3 * a + k], device_id=(px, py, c), device_id_type=MESH)
                cp.wait_send()
                cp.wait_recv()

    return pl.pallas_call(
        body, name=name, out_shape=[pltpu.HBM(s.shape, s.dtype) for s in slots],
        in_specs=[_HBM] * n + [_SEM, _SEM, _ANY], out_specs=[_HBM] * n,
        input_output_aliases={a: a for a in range(n)},
        compiler_params=pltpu.CompilerParams(has_side_effects=_EFFECT),
    )(*slots, send_sems, recv_sems, after)


def rs_chipsum16(g, recv1, cidx, name):
    nl, hr, cols = recv1.shape[1:]

    def body(c_ref, g_ref, r_ref, o_ref):
        o_ref[...] = (g_ref[...] + r_ref[...]).astype(BF16)

    blk = (1, 1, hr, cols)
    return pl.pallas_call(
        body, name=name, out_shape=jax.ShapeDtypeStruct(recv1.shape, BF16),
        grid_spec=pltpu.PrefetchScalarGridSpec(
            num_scalar_prefetch=1, grid=(4, nl),
            in_specs=[pl.BlockSpec(blk, lambda j, l, c: (j, l, c[0], 0)), pl.BlockSpec(blk, lambda j, l, c: (j, l, 0, 0))],
            out_specs=pl.BlockSpec(blk, lambda j, l, c: (j, l, 0, 0))),
        compiler_params=_cp("parallel", "parallel"),
    )(cidx, g, recv1)


def sibling_start(gs, name):
    n = len(gs)
    lands = [lax.empty((*g.shape[:2], g.shape[2] // 2, g.shape[3]), F32) for g in gs]

    def body(*refs):
        g_refs, l_refs, send_sems, recv_sems, token = refs[:n], refs[n:2 * n], refs[2 * n], refs[2 * n + 1], refs[-1]
        x, y, c = _place()
        for a in range(n):
            pltpu.make_async_remote_copy(
                src_ref=_row_half(g_refs[a], (slice(None), slice(None)), 1 - c), dst_ref=l_refs[a],
                send_sem=send_sems.at[a], recv_sem=recv_sems.at[a], device_id=(x, y, 1 - c), device_id_type=MESH).start()
        token[...] = jnp.zeros_like(token)

    bufs = list(gs) + lands
    out = pl.pallas_call(
        body, name=name,
        out_shape=(pltpu.SemaphoreType.DMA((n,)), pltpu.SemaphoreType.DMA((n,)),
                   *[pltpu.HBM(b.shape, b.dtype) for b in bufs], jax.ShapeDtypeStruct((8, 128), F32)),
        in_specs=[_HBM] * (2 * n), out_specs=(_SEM, _SEM, *[_HBM] * (2 * n), pl.BlockSpec(memory_space=pltpu.VMEM)),
        input_output_aliases={i: 2 + i for i in range(2 * n)},
        compiler_params=pltpu.CompilerParams(has_side_effects=_EFFECT),
    )(*[pltpu.with_memory_space_constraint(b, pltpu.HBM) for b in bufs])
    return out[0], out[1], list(out[2:2 + n]), list(out[2 + n:2 + 2 * n]), out[-1]


def sibling_wait(send_sems, recv_sems, gs, lands, after, name):
    n = len(gs)

    def body(*refs):
        g_refs, l_refs, ssem, rsem = refs[:n], refs[n:2 * n], refs[2 * n], refs[2 * n + 1]
        x, y, c = _place()
        for a in range(n):
            cp = pltpu.make_async_remote_copy(
                src_ref=_row_half(g_refs[a], (slice(None), slice(None)), 1 - c), dst_ref=l_refs[a],
                send_sem=ssem.at[a], recv_sem=rsem.at[a], device_id=(x, y, 1 - c), device_id_type=MESH)
            cp.wait_send()
            cp.wait_recv()

    bufs = list(gs) + list(lands)
    out = pl.pallas_call(
        body, name=name, out_shape=[pltpu.HBM(b.shape, b.dtype) for b in bufs],
        in_specs=[_HBM] * (2 * n) + [_SEM, _SEM, _ANY], out_specs=[_HBM] * (2 * n),
        input_output_aliases={i: i for i in range(2 * n)},
        compiler_params=pltpu.CompilerParams(has_side_effects=_EFFECT),
    )(*bufs, send_sems, recv_sems, after)
    return list(out[:n]), list(out[n:])


def exchange_start(cs, name):
    n = len(cs)
    lands = [lax.empty((3, *c.shape[1:]), BF16) for c in cs]

    def body(*refs):
        s_refs, l_refs, send_sems, recv_sems, token = refs[:n], refs[n:2 * n], refs[2 * n], refs[2 * n + 1], refs[-1]
        x, y, c = _place()
        for k, (px, py) in enumerate([(1 - x, y), (x, 1 - y), (1 - x, 1 - y)]):
            for a in range(n):
                pltpu.make_async_remote_copy(
                    src_ref=s_refs[a].at[2 * px + py], dst_ref=l_refs[a].at[k], send_sem=send_sems.at[3 * a + k],
                    recv_sem=recv_sems.at[3 * a + k], device_id=(px, py, c), device_id_type=MESH).start()
        token[...] = jnp.zeros_like(token)

    bufs = list(cs) + lands
    out = pl.pallas_call(
        body, name=name,
        out_shape=(pltpu.SemaphoreType.DMA((3 * n,)), pltpu.SemaphoreType.DMA((3 * n,)),
                   *[pltpu.HBM(b.shape, b.dtype) for b in bufs], jax.ShapeDtypeStruct((8, 128), F32)),
        in_specs=[_HBM] * (2 * n), out_specs=(_SEM, _SEM, *[_HBM] * (2 * n), pl.BlockSpec(memory_space=pltpu.VMEM)),
        input_output_aliases={i: 2 + i for i in range(2 * n)},
        compiler_params=pltpu.CompilerParams(has_side_effects=_EFFECT),
    )(*[pltpu.with_memory_space_constraint(b, pltpu.HBM) for b in bufs])
    return out[0], out[1], list(out[2:2 + n]), list(out[2 + n:2 + 2 * n]), out[-1]


def exchange_wait(send_sems, recv_sems, cs, lands, after, name):
    n = len(cs)

    def body(*refs):
        s_refs, l_refs, ssem, rsem = refs[:n], refs[n:2 * n], refs[2 * n], refs[2 * n + 1]
        x, y, c = _place()
        for k, (px, py) in enumerate([(1 - x, y), (x, 1 - y), (1 - x, 1 - y)]):
            for a in range(n):
                cp = pltpu.make_async_remote_copy(
                    src_ref=s_refs[a].at[2 * px + py], dst_ref=l_refs[a].at[k], send_sem=ssem.at[3 * a + k],
                    recv_sem=rsem.at[3 * a + k], device_id=(px, py, c), device_id_type=MESH)
                cp.wait_send()
                cp.wait_recv()

    bufs = list(cs) + list(lands)
    out = pl.pallas_call(
        body, name=name, out_shape=[pltpu.HBM(b.shape, b.dtype) for b in bufs],
        in_specs=[_HBM] * (2 * n) + [_SEM, _SEM, _ANY], out_specs=[_HBM] * (2 * n),
        input_output_aliases={i: i for i in range(2 * n)},
        compiler_params=pltpu.CompilerParams(has_side_effects=_EFFECT),
    )(*bufs, send_sems, recv_sems, after)
    return list(out[n:])


def rs_final_sum(g, recv1, recv2, idx, name):
    nl, hr, cols = recv1.shape[1:]

    def body(i_ref, g_ref, r1_ref, r2_ref, o_ref):
        acc = g_ref[0, 0] + r1_ref[0, 0]
        for k in range(3):
            acc = acc + r2_ref[k, 0].astype(F32)
        o_ref[0] = acc

    blk = (1, 1, hr, cols)
    return pl.pallas_call(
        body, name=name, out_shape=jax.ShapeDtypeStruct((nl, 2 * hr, cols), F32),
        grid_spec=pltpu.PrefetchScalarGridSpec(
            num_scalar_prefetch=1, grid=(nl,),
            in_specs=[pl.BlockSpec(blk, lambda l, ix: (ix[0], l, ix[1], 0)), pl.BlockSpec(blk, lambda l, ix: (ix[0], l, 0, 0)),
                      pl.BlockSpec((3, 1, hr, cols), lambda l, ix: (0, l, 0, 0))],
            out_specs=pl.BlockSpec((1, hr, cols), lambda l, ix: (l, ix[1], 0))),
        compiler_params=_cp("parallel"),
    )(idx, g, recv1, recv2)


def rs_share_halves(fulls):
    n = len(fulls)

    def body(*refs):
        h_refs, o_refs, (send_sems, recv_sems) = refs[:n], refs[n:2 * n], refs[2 * n:]
        x, y, c = _place()
        sibling = (x, y, 1 - c)

        def copy(a, h):
            return pltpu.make_async_remote_copy(
                src_ref=_row_half(h_refs[a], (slice(None),), h), dst_ref=_row_half(o_refs[a], (slice(None),), h),
                send_sem=send_sems.at[a], recv_sem=recv_sems.at[a], device_id=sibling, device_id_type=MESH)

        for a in range(n):
            copy(a, c).start()
        for a in range(n):
            copy(a, c).wait_send()
            copy(a, 1 - c).wait_recv()

    return pl.pallas_call(
        body, name="rs_share_halves", out_shape=[jax.ShapeDtypeStruct(f.shape, F32) for f in fulls],
        in_specs=[_ANY] * n, out_specs=[_ANY] * n, input_output_aliases={a: a for a in range(n)},
        scratch_shapes=[pltpu.SemaphoreType.DMA((n,)), pltpu.SemaphoreType.DMA((n,))],
    )(*fulls)


def _own_slot(shard, chip, dtype):
    return lax.dynamic_update_slice(lax.empty((4, *shard.shape), dtype), shard.astype(dtype)[None],
                                    (chip,) + (0,) * shard.ndim)


def kernel(x, norm_mix_pre, norm_mix_post, norm_ffn_pre, norm_ffn_post, w_in, conv_a, gate_up_fwd, gate_bias_fwd, gate_up_bwd, gate_bias_bwd, gla_head_norm, w_out, w_up, conv_ffn, w_down, loss_target, m_norm_mix_pre, m_norm_mix_post, m_norm_ffn_pre, m_norm_ffn_post, m_w_in, m_conv_a, m_gate_up_fwd, m_gate_bias_fwd, m_gate_up_bwd, m_gate_bias_bwd, m_gla_head_norm, m_w_out, m_w_up, m_conv_ffn, m_w_down, v_norm_mix_pre, v_norm_mix_post, v_norm_ffn_pre, v_norm_ffn_post, v_w_in, v_conv_a, v_gate_up_fwd, v_gate_bias_fwd, v_gate_up_bwd, v_gate_bias_bwd, v_gla_head_norm, v_w_out, v_w_up, v_conv_ffn, v_w_down):
    L = x.shape[1]
    xi, yi, ci = _place()
    chip = 2 * xi + yi
    tl_gla, tl_mix, tl_ffn = min(L, TL_GLA), min(L, TL_MIX), min(L, TL_FFN)

    big_w = (w_in, w_out, w_up, w_down)
    a_in0 = allgather_weights([_own_slot(w_in[0:1], chip, BF16)])[0][:, 0]
    started = []
    prev = (a_in0,)
    for l in range(DEPTH):
        ws = big_w[1:] if l == 0 else big_w
        started.append(gather_start([_own_slot(w[l], chip, BF16) for w in ws], f"gather_start_{l}", after=prev))
        prev = (started[-1][3],)
    tokens = [s[3] for s in started]

    def full_w_in(a_in):
        return jnp.pad(jnp.concatenate([a_in[j] for j in range(4)], axis=1), ((0, 0), (0, DINP - DIN)))

    small = jnp.concatenate([conv_a.reshape(-1), gate_up_fwd.reshape(-1), gate_up_bwd.reshape(-1), conv_ffn.reshape(-1)])
    ms = small.shape[0] // 128
    sg = allgather8(small.reshape(ms, 128), "allgather_small_weights").reshape(4, 2, ms * 128)[:, 0]

    def small_full(off, shape):
        n = shape[0] * shape[1] * shape[2]
        return jnp.concatenate([sg[j, off:off + n].reshape(shape) for j in range(4)], axis=2)

    o1 = DEPTH * 3 * 128
    o2 = o1 + DEPTH * RK * 64
    o3 = o2 + DEPTH * RK * 64
    conv_a_f = small_full(0, (DEPTH, 3, 128))
    gup_f = small_full(o1, (DEPTH, RK, 64))
    gup_b = small_full(o2, (DEPTH, RK, 64))
    conv_ffn_f = small_full(o3, (DEPTH, 3, 1408))

    def gcat_of(l):
        g = jnp.zeros((LRW, 2 * DK), F32)
        g = g.at[0:RK, 0:DK].set(gup_f[l]).at[RK:2 * RK, DK:2 * DK].set(gup_b[l])
        return g.astype(BF16)

    gcats = [gcat_of(l) for l in range(DEPTH)]
    gbiases = [jnp.concatenate([gate_bias_fwd[l], gate_bias_bwd[l]])[None, :] for l in range(DEPTH)]
    ghn4s = [jnp.tile(gla_head_norm[l], NH)[None, :] for l in range(DEPTH)]

    xc = x.reshape(L, D)
    saved = []
    W_in, W_out, W_up, W_down = [], [], [], []
    tl_row = min(L, TL_ROW)
    for l in range(DEPTH):
        ssem, rsem, bufs, _ = started[l]
        if l > 0:
            a_in, a_out, a_up, a_down = gather_wait(ssem, rsem, bufs, xc, f"gather_wait_{l}")
        W_in.append(full_w_in(a_in0 if l == 0 else a_in))
        P, h1 = rms_matmul(xc, norm_mix_pre[l][None, :], W_in[l], DINP, "proj_in", out_dtype=BF16, tm=TM_PROJ // 2,
                           after=tokens if l == 0 else ())
        o_f, o_b, sf, sb = gla_fwd(P, gcats[l], gbiases[l], tl_gla)
        if l == 0:
            a_out, a_up, a_down = gather_wait(ssem, rsem, bufs, o_f, "gather_wait_0")
        W_out.append(a_out.reshape(D, D))
        W_up.append(a_up)
        W_down.append(a_down.reshape(DFF, D))
        ycat, y, x1 = mix_out(P, o_f, o_b, conv_a_f[l], ghn4s[l], W_out[l], norm_mix_post[l][None, :], xc,
                              min(L, TL_MIX_OUT))
        U, h2, ug, uv, z, y2, x2 = ffn_fwd(x1, norm_ffn_pre[l][None, :], W_up[l], conv_ffn_f[l], W_down[l],
                                           norm_ffn_post[l][None, :], tl_ffn)
        saved.append(dict(x=xc, h1=h1, P=P, o_f=o_f, o_b=o_b, sf=sf, sb=sb, ycat=ycat, y=y, x1=x1, h2=h2, U=U, y2=y2,
                          ug=ug, uv=uv, z=z))
        xc = x2

    dx, loss_blk = loss_head(xc, loss_target.reshape(L, D), tl_row)

    big = ("w_in", "w_out", "w_up", "w_down")
    cidx = jnp.reshape(ci, (1,)).astype(jnp.int32)
    idx = jnp.stack([chip, ci]).astype(jnp.int32)
    grads = [None] * DEPTH
    reduced = [dict() for _ in range(DEPTH)]
    tl_dw = min(L, 1024)
    groups = dict(ffn=("w_up", "w_down"), mix=("w_in", "w_out"))
    state = {grp: dict(flight=None, sibling=None) for grp in groups}
    token = ()

    def finish(grp, after):
        lp, gs_p, recv1_p, (ssem, rsem, cs_thru, lands, _) = state[grp]["flight"]
        recv2 = exchange_wait(ssem, rsem, cs_thru, lands, after, f"exchange_wait_{grp}_{lp}")
        halves = [rs_final_sum(g, r1, r2, idx, "rs_final_sum_" + k)
                  for g, r1, r2, k in zip(gs_p, recv1_p, recv2, groups[grp])]
        reduced[lp].update(zip(groups[grp], rs_share_halves(halves)))

    def advance(grp, after):
        st = state[grp]
        ls, (ssem, rsem, gs_thru, lands, _) = st["sibling"]
        gs_s, recv1 = sibling_wait(ssem, rsem, gs_thru, lands, after, f"sibling_wait_{grp}_{ls}")
        cs16 = [rs_chipsum16(g, r, cidx, "rs_chipsum16_" + k) for g, r, k in zip(gs_s, recv1, groups[grp])]
        flight = exchange_start(cs16, f"exchange_start_{grp}_{ls}")
        if st["flight"] is not None:
            finish(grp, flight[4])
        st["flight"] = (ls, gs_s, recv1, flight)
        st["sibling"] = None
        return flight[4]

    for l in reversed(range(DEPTH)):
        s = saved[l]
        dy2, dg4 = rms_bwd_pre(dx, s["y2"], norm_ffn_post[l][None, :], tl_row, after=token)
        du_g, du_v = ffn_bwd1(dy2, s["ug"], s["uv"], W_down[l], min(L, TL_FFN2))
        token2 = (advance("mix", du_g),) if state["mix"]["sibling"] is not None else ()
        g_down = matmul_tn(s["z"], dy2, DFF // 2, D, tl_dw, "dw_down").reshape(4, 1, DFF // 4, D)
        dU_g, dU_v, dx1, dg3, dcf_g, dcf_v = ffn_bwd2(du_g, du_v, s["U"], conv_ffn_f[l], W_up[l], s["x1"], dx,
                                                      norm_ffn_pre[l][None, :],
                                        min(L, TL_FFN2))
        g_up = matmul_tn(s["h2"], dU_g, D, WFF, tl_dw, "dw_up_gate",
                         into=(lax.empty((4, 1, D, WFF), F32), (None, None, D, WFF), lambda p, q: (q, 0, 0, 0)))
        g_up = matmul_tn(s["h2"], dU_v, D, WFF, tl_dw, "dw_up_val",
                         into=(g_up, (None, None, D, WFF), lambda p, q: (NFF + q, 0, 0, 0)))
        sib = sibling_start([g_up, g_down], f"sibling_start_ffn_{l}")
        state["ffn"]["sibling"] = (l, sib)
        dy, dg2 = rms_bwd_pre(dx1, s["y"], norm_mix_post[l][None, :], tl_row, after=token2 + (sib[4],))
        dgb, dcc, dgo, do, dca, dghn = mix_bwd1(dy, W_out[l], s["P"], s["o_f"], s["o_b"], conv_a_f[l], ghn4s[l],
                                                min(L, TL_MIX_OUT))
        token3 = advance("ffn", dgb)
        g_out = matmul_tn(s["ycat"], dy, D, D, tl_dw, "dw_out", after=(token3,)).reshape(4, 1, D // 4, D)
        gl = gla_bwd(s["P"], do, s["sf"], s["sb"], gcats[l], gbiases[l], tl_gla)
        dP, dx, dg1, dgcat, dbias = mix_bwd2(dgb, dcc, dgo, gl, s["P"], conv_a_f[l], gcats[l], gbiases[l], W_in[l],
                                             s["x"], dx1, norm_mix_pre[l][None, :], tl_mix)
        dW_in = matmul_tn(s["h1"], dP, D // 2, DINP, tl_dw, "dw_in")
        g_in = jnp.stack([dW_in[:, (DIN // 4) * j:(DIN // 4) * (j + 1)] for j in range(4)])[:, None]
        grads[l] = dict(
            norm_mix_pre=dg1[0], norm_mix_post=dg2[0], norm_ffn_pre=dg3[0], norm_ffn_post=dg4[0],
            conv_a=dca[0:3], gate_up_fwd=dgcat[0:RK, 0:DK], gate_bias_fwd=dbias[0, 0:DK],
            gate_up_bwd=dgcat[RK:2 * RK, DK:2 * DK], gate_bias_bwd=dbias[0, DK:2 * DK], gla_head_norm=dghn[0],
            conv_ffn=jnp.concatenate([dcf_g[j, 0:3] for j in range(NFF)] + [dcf_v[j, 0:3] for j in range(NFF)], axis=1))
        sib = sibling_start([g_in, g_out], f"sibling_start_mix_{l}")
        state["mix"]["sibling"] = (l, sib)
        token = (sib[4],)
    last = advance("mix", token[0])
    finish("ffn", last)
    finish("mix", last)

    G = {k: jnp.stack([grads[l][k] for l in range(DEPTH)]) for k in grads[0]}

    small_names = ["norm_mix_pre", "norm_mix_post", "norm_ffn_pre", "norm_ffn_post", "conv_a", "gate_up_fwd",
                   "gate_bias_fwd", "gate_up_bwd", "gate_bias_bwd", "gla_head_norm", "conv_ffn"]
    flat = jnp.concatenate([G[k].reshape(-1) for k in small_names] + [loss_blk[0, 0:1]])
    n_small = flat.shape[0]
    mp = -(-n_small // 1024) * 8
    flat = jnp.pad(flat, (0, mp * 128 - n_small)).reshape(mp, 128)
    tot = sum8(allgather8(flat, "allgather_small_grads"), mp).reshape(-1)
    gsm = {}
    o = 0
    for k in small_names:
        n = G[k].size
        gsm[k] = tot[o:o + n].reshape(G[k].shape)
        o += n
    loss = tot[o]

    def my_cols(a, width):
        return lax.dynamic_slice_in_dim(a, chip * width, width, axis=2)

    gsm["conv_a"] = my_cols(gsm["conv_a"], 128)
    gsm["gate_up_fwd"] = my_cols(gsm["gate_up_fwd"], 64)
    gsm["gate_up_bwd"] = my_cols(gsm["gate_up_bwd"], 64)
    gsm["conv_ffn"] = my_cols(gsm["conv_ffn"], 1408)

    for k in big:
        gsm[k] = jnp.concatenate([reduced[l][k] for l in range(DEPTH)], axis=0)

    names = ["norm_mix_pre", "norm_mix_post", "norm_ffn_pre", "norm_ffn_post", "w_in", "conv_a", "gate_up_fwd",
             "gate_bias_fwd", "gate_up_bwd", "gate_bias_bwd", "gla_head_norm", "w_out", "w_up", "conv_ffn", "w_down"]
    w = dict(norm_mix_pre=norm_mix_pre, norm_mix_post=norm_mix_post, norm_ffn_pre=norm_ffn_pre, norm_ffn_post=norm_ffn_post,
             w_in=w_in, conv_a=conv_a, gate_up_fwd=gate_up_fwd, gate_bias_fwd=gate_bias_fwd, gate_up_bwd=gate_up_bwd,
             gate_bias_bwd=gate_bias_bwd, gla_head_norm=gla_head_norm, w_out=w_out, w_up=w_up, conv_ffn=conv_ffn, w_down=w_down)
    m = dict(norm_mix_pre=m_norm_mix_pre, norm_mix_post=m_norm_mix_post, norm_ffn_pre=m_norm_ffn_pre, norm_ffn_post=m_norm_ffn_post,
             w_in=m_w_in, conv_a=m_conv_a, gate_up_fwd=m_gate_up_fwd, gate_bias_fwd=m_gate_bias_fwd, gate_up_bwd=m_gate_up_bwd,
             gate_bias_bwd=m_gate_bias_bwd, gla_head_norm=m_gla_head_norm, w_out=m_w_out, w_up=m_w_up, conv_ffn=m_conv_ffn, w_down=m_w_down)
    v = dict(norm_mix_pre=v_norm_mix_pre, norm_mix_post=v_norm_mix_post, norm_ffn_pre=v_norm_ffn_pre, norm_ffn_post=v_norm_ffn_post,
             w_in=v_w_in, conv_a=v_conv_a, gate_up_fwd=v_gate_up_fwd, gate_bias_fwd=v_gate_bias_fwd, gate_up_bwd=v_gate_up_bwd,
             gate_bias_bwd=v_gate_bias_bwd, gla_head_norm=v_gla_head_norm, w_out=v_w_out, w_up=v_w_up, conv_ffn=v_conv_ffn, w_down=v_w_down)
    upd = {k: adamw(w[k], gsm[k], m[k], v[k], "adamw_" + k) for k in names}
    return (loss, dx.reshape(1, L, D), *[gsm[k] for k in names], *[upd[k][0] for k in names],
            *[upd[k][1] for k in names], *[upd[k][2] for k in names])
```

```python
import functools

import jax
import jax.numpy as jnp
from jax import lax
from jax.experimental import pallas as pl
from jax.experimental.pallas import tpu as pltpu

F32 = jnp.float32
BF16 = jnp.bfloat16
MXU_DTYPE = jnp.bfloat16
MESH = pl.DeviceIdType.MESH

D = 1024
DC = 512
DG = 512
NH = 4
HV = 128
HK = 64
DK = 256
RK = 16
CH = 64
DFF = 2816
DIN = 3104
DINP = 3200
LRW = 128
DEPTH = 4
EPS = 1e-6
QSCALE = HK ** -0.5
GATE_NORM = 1.0 / 16.0
CB_GB, CB_GC, CB_GV, CB_GO = 0, 1, 2, 5
CB_Q, CB_K = 6, 7
CB_V = 4
CB_LR = 24
LR = 0.001
B1 = 0.9
B2 = 0.999
AEPS = 1e-08
WD = 0.01
STEP = 10
TM_PROJ = 1024
TL_GLA = 512
TL_MIX = 256
TL_MIX_OUT = 512
TL_ROW = 1024
TL_FFN = 256
TL_FFN2 = 512
VMEM_LIMIT = 56 * 1024 * 1024


def _cp(*sem):
    return pltpu.CompilerParams(dimension_semantics=sem if sem else None, vmem_limit_bytes=VMEM_LIMIT)


def _mm(a, b):
    return jnp.dot(a.astype(MXU_DTYPE), b.astype(MXU_DTYPE), preferred_element_type=F32)


def _mm_nt(a, b):
    return lax.dot_general(a.astype(MXU_DTYPE), b.astype(MXU_DTYPE), (((1,), (1,)), ((), ())),
                           preferred_element_type=F32)


def _mm_tn(a, b):
    return lax.dot_general(a.astype(MXU_DTYPE), b.astype(MXU_DTYPE), (((0,), (0,)), ((), ())),
                           preferred_element_type=F32)


def _mm_tri(tri, b):
    t = tri.astype(BF16)
    b1 = b.astype(BF16)
    r1 = b - b1.astype(F32)
    b2 = r1.astype(BF16)
    b3 = (r1 - b2.astype(F32)).astype(BF16)
    dot = lambda u: jnp.dot(t, u, preferred_element_type=F32)
    return dot(b1) + dot(b2) + dot(b3)


def _rms(x, g):
    r = lax.rsqrt(jnp.mean(x * x, axis=-1, keepdims=True) + EPS)
    return x * r * g


def _rms_bwd(dout, y, g):
    r = lax.rsqrt(jnp.mean(y * y, axis=-1, keepdims=True) + EPS)
    yh = y * r
    dyh = dout * g
    dy = r * (dyh - yh * jnp.mean(dyh * yh, axis=-1, keepdims=True))
    dg = jnp.sum(dout * yh, axis=0, keepdims=True)
    return dy, dg


def _sigmoid(x):
    return 0.5 * jnp.tanh(0.5 * x) + 0.5


def _logsig(x):
    return jnp.minimum(x, 0.0) - jnp.log1p(jnp.exp(-jnp.abs(x)))


def _shifts(x, p8, n8):
    n = x.shape[0]
    xe = jnp.concatenate([p8, x, n8], axis=0)
    return pltpu.roll(xe, 1, 0)[8:8 + n], pltpu.roll(xe, n + 15, 0)[8:8 + n]


def _halo_rows(prev_ref, next_ref, i, last):
    hr = prev_ref.shape[0]
    p = jnp.where(i == 0, 0.0, prev_ref[...].astype(F32)[hr - 8:hr, :])
    n = jnp.where(i == last, 0.0, next_ref[...].astype(F32)[0:8, :])
    return p, n


def _conv3(x, xp, xn, w_ref):
    xm1, xp1 = _shifts(x, xp, xn)
    return w_ref[0:1, :] * xm1 + w_ref[1:2, :] * x + w_ref[2:3, :] * xp1, xm1, xp1


def _conv3_t(d, dp, dn, w_ref):
    dm1, dp1 = _shifts(d, dp, dn)
    return w_ref[0:1, :] * dp1 + w_ref[1:2, :] * d + w_ref[2:3, :] * dm1


HALO32 = 8
HALO16 = 16


def _prev_row_blk(i, tl, hr):
    return jnp.maximum(i * (tl // hr) - 1, 0)


def _next_row_blk(i, tl, nrows, hr):
    return jnp.minimum((i + 1) * (tl // hr), nrows // hr - 1)


def _prev_blk(tl, cb, hr=HALO32):
    return lambda i: (_prev_row_blk(i, tl, hr), cb)


def _next_blk(tl, nrows, cb, hr=HALO32):
    return lambda i: (_next_row_blk(i, tl, nrows, hr), cb)


def rms_matmul(x, g, w, tn, name, w_spec=None, n_out=None, out_dtype=F32, after=(), tm=TM_PROJ):
    L = x.shape[0]
    N = w.shape[1] if n_out is None else n_out
    tm = min(L, tm)
    if w_spec is None:
        w_spec = pl.BlockSpec((D, tn), lambda i, j: (0, j))

    def body(x_ref, g_ref, w_ref, *rest):
        o_ref, h_ref = rest[-2:]

        @pl.when(pl.program_id(1) == 0)
        def _():
            h_ref[...] = _rms(x_ref[...], g_ref[...]).astype(BF16)

        o_ref[...] = _mm(h_ref[...], w_ref[...]).astype(out_dtype)

    return pl.pallas_call(
        body, name=name, grid=(L // tm, N // tn),
        in_specs=[pl.BlockSpec((tm, D), lambda i, j: (i, 0)), pl.BlockSpec((1, D), lambda i, j: (0, 0)), w_spec]
        + [_ANY] * len(after),
        out_specs=[pl.BlockSpec((tm, tn), lambda i, j: (i, j)), pl.BlockSpec((tm, D), lambda i, j: (i, 0))],
        out_shape=[jax.ShapeDtypeStruct((L, N), out_dtype), jax.ShapeDtypeStruct((L, D), BF16)],
        compiler_params=_cp("parallel", "arbitrary"),
    )(x, g, w, *after)


def _gla_masks():
    def blk(shape, rdiv, cdiv):
        r = lax.broadcasted_iota(jnp.int32, shape, 0) // rdiv
        c = lax.broadcasted_iota(jnp.int32, shape, 1) // cdiv
        return (r == c).astype(F32)

    r = lax.broadcasted_iota(jnp.int32, (CH, CH), 0)
    c = lax.broadcasted_iota(jnp.int32, (CH, CH), 1)
    r4 = lax.broadcasted_iota(jnp.int32, (NH * CH, CH), 0) % CH
    c4 = lax.broadcasted_iota(jnp.int32, (NH * CH, CH), 1)
    return dict(
        bdq=blk((NH * CH, DK), CH, HK),
        bdo=blk((NH * CH, DG), CH, HV),
        bds=blk((DG, DK), HV, HK),
        tril=(r >= c).astype(F32), triu=(r <= c).astype(F32),
        tril4=r4 >= c4, triu4=r4 <= c4,
    )


def _tile4(x):
    return jnp.concatenate([x, x, x, x], axis=0)


def _gla_tile_prep(q, k, a, m, rev, nc):
    tri = m["triu"] if rev else m["tril"]
    chunks = [a[c * CH:(c + 1) * CH] for c in range(nc)]
    cum = jnp.concatenate([_mm_tri(tri, ac) for ac in chunks], axis=0)
    tot = jnp.concatenate([jnp.sum(ac, axis=0, keepdims=True) for ac in chunks], axis=0)
    tot_rows = jnp.concatenate([jnp.broadcast_to(tot[c:c + 1], (CH, DK)) for c in range(nc)], axis=0)
    e = jnp.exp(cum)
    einv = jnp.exp(-cum)
    eout = jnp.exp(tot_rows - cum)
    q, k = q.astype(F32), k.astype(F32)
    return dict(e=e, einv=einv, eout=eout, dec=jnp.exp(tot), qt=q * QSCALE * e, kt=k * einv, kh=k * eout)


def _gla_scores(qt16, kt16, m, rev):
    qs = _tile4(qt16) * m["bdq"].astype(qt16.dtype)
    return qs, jnp.where(m["triu4"] if rev else m["tril4"], _mm_nt(qs, kt16), 0.0)


def _gla_chunk_fwd(qt16, kt16, kh16, v, dec, st_ref, m, rev):
    _, sc = _gla_scores(qt16, kt16, m, rev)
    v16 = v.astype(BF16)
    r = _mm(sc, v16)
    o_intra = jnp.concatenate([r[h * CH:(h + 1) * CH, h * HV:(h + 1) * HV] for h in range(NH)], axis=1)
    st = st_ref[...]
    st16 = st.astype(BF16)
    o = o_intra + _mm_nt(qt16, st16)
    st_ref[...] = st * dec + _mm_tn(v16, kh16) * m["bds"]
    return o, st16


def _gates(lr_ref, gc_ref, bs_ref, cols):
    return _logsig(_mm(lr_ref[...], gc_ref[:, cols]) + bs_ref[:, cols]) * GATE_NORM


def gla_fwd(P, gcat, gbias, tl):
    L = P.shape[0]
    nb = L // tl
    nc = tl // CH

    def body(qf, kf, vf, lf, qb, kb, vb, lb, gc_ref, bs_ref, of, ob, sf, sb, stf, stb,
             qtf, ktf, khf, dcf, qtb, ktb, khb, dcb):
        @pl.when(pl.program_id(0) == 0)
        def _():
            stf[...] = jnp.zeros_like(stf)
            stb[...] = jnp.zeros_like(stb)

        m = _gla_masks()
        for (q, k, lr, cols, rev, qt, kt, kh, dc) in ((qf, kf, lf, slice(0, DK), False, qtf, ktf, khf, dcf),
                                                      (qb, kb, lb, slice(DK, 2 * DK), True, qtb, ktb, khb, dcb)):
            p = _gla_tile_prep(q[...], k[...], _gates(lr, gc_ref, bs_ref, cols), m, rev, nc)
            qt[...] = p["qt"].astype(BF16)
            kt[...] = p["kt"].astype(BF16)
            kh[...] = p["kh"].astype(BF16)
            dc[...] = p["dec"]

        def chunk(c, carry):
            rows = pl.ds(pl.multiple_of(c * CH, CH), CH)
            o, st = _gla_chunk_fwd(qtf[rows, :], ktf[rows, :], khf[rows, :], vf[rows, :], dcf[pl.ds(c, 1), :], stf, m, False)
            of[rows, :] = o.astype(BF16)
            sf[c] = st
            cb = nc - 1 - c
            rows = pl.ds(pl.multiple_of(cb * CH, CH), CH)
            o, st = _gla_chunk_fwd(qtb[rows, :], ktb[rows, :], khb[rows, :], vb[rows, :], dcb[pl.ds(cb, 1), :], stb, m, True)
            ob[rows, :] = o.astype(BF16)
            sb[cb] = st
            return carry

        lax.fori_loop(0, nc, chunk, 0, unroll=2)

    fw = lambda cb: (lambda i: (i, cb))
    bw = lambda cb: (lambda i: (nb - 1 - i, cb))
    return pl.pallas_call(
        body, name="gla_fwd", grid=(nb,),
        in_specs=[pl.BlockSpec((tl, DK), fw(CB_Q)), pl.BlockSpec((tl, DK), fw(CB_K)), pl.BlockSpec((tl, DG), fw(CB_V)),
                  pl.BlockSpec((tl, LRW), fw(CB_LR)),
                  pl.BlockSpec((tl, DK), bw(CB_Q)), pl.BlockSpec((tl, DK), bw(CB_K)), pl.BlockSpec((tl, DG), bw(CB_V)),
                  pl.BlockSpec((tl, LRW), bw(CB_LR)),
                  pl.BlockSpec((LRW, 2 * DK), lambda i: (0, 0)), pl.BlockSpec((1, 2 * DK), lambda i: (0, 0))],
        out_specs=[pl.BlockSpec((tl, DG), lambda i: (i, 0)), pl.BlockSpec((tl, DG), lambda i: (nb - 1 - i, 0)),
                   pl.BlockSpec((nc, DG, DK), lambda i: (i, 0, 0)), pl.BlockSpec((nc, DG, DK), lambda i: (nb - 1 - i, 0, 0))],
        out_shape=[jax.ShapeDtypeStruct((L, DG), BF16), jax.ShapeDtypeStruct((L, DG), BF16),
                   jax.ShapeDtypeStruct((L // CH, DG, DK), BF16), jax.ShapeDtypeStruct((L // CH, DG, DK), BF16)],
        scratch_shapes=[pltpu.VMEM((DG, DK), F32), pltpu.VMEM((DG, DK), F32)]
        + [pltpu.VMEM((tl, DK), BF16)] * 3 + [pltpu.VMEM((nc, DK), F32)]
        + [pltpu.VMEM((tl, DK), BF16)] * 3 + [pltpu.VMEM((nc, DK), F32)],
        compiler_params=_cp("arbitrary"),
    )(P, P, P, P, P, P, P, P, gcat, gbias)


def _headnorm(o):
    oh, rs = [], []
    for h in range(NH):
        oo = o[:, h * HV:(h + 1) * HV]
        r = lax.rsqrt(jnp.mean(oo * oo, axis=-1, keepdims=True) + EPS)
        oh.append(oo * r)
        rs.append(r)
    return jnp.concatenate(oh, axis=1), rs


def mix_out(P, o_f, o_b, conv_a, ghn4, w_out, g2, x, tl):
    L = P.shape[0]
    nt = L // tl

    def body(gb, gc, gv, go, gcp, gvp, gcn, gvn, of, ob, ca, gh, wo, g2r, xr, ycat, yr, x1):
        i = pl.program_id(0)
        cp, cn = _halo_rows(gcp, gcn, i, nt - 1)
        vp, vn = _halo_rows(gvp, gvn, i, nt - 1)
        c = gc[...].astype(F32) * gv[...].astype(F32)
        cc, _, _ = _conv3(c, cp * vp, cn * vn, ca)
        ya = gb[...].astype(F32) * cc
        oh, _ = _headnorm(of[...].astype(F32) + ob[...].astype(F32))
        g = go[...].astype(F32)
        yb = g * _sigmoid(g) * (oh * gh[...])
        yc = jnp.concatenate([ya, yb], axis=1).astype(BF16)
        ycat[...] = yc
        y = _mm(yc, wo[...])
        yr[...] = y
        x1[...] = xr[...] + _rms(y, g2r[...])

    t = lambda cb: pl.BlockSpec((tl, DC), lambda i: (i, cb))
    hp = lambda cb: pl.BlockSpec((HALO16, DC), _prev_blk(tl, cb, HALO16))
    hn = lambda cb: pl.BlockSpec((HALO16, DC), _next_blk(tl, L, cb, HALO16))
    row = lambda n: pl.BlockSpec((tl, n), lambda i: (i, 0))
    full = lambda a: pl.BlockSpec(a.shape, lambda i: (0, 0))
    return pl.pallas_call(
        body, name="mix_out", grid=(nt,),
        in_specs=[t(CB_GB), t(CB_GC), t(CB_GV), t(CB_GO), hp(CB_GC), hp(CB_GV), hn(CB_GC), hn(CB_GV),
                  row(DG), row(DG), full(conv_a), full(ghn4), full(w_out), full(g2), row(D)],
        out_specs=[row(D), row(D), row(D)],
        out_shape=[jax.ShapeDtypeStruct((L, D), BF16), jax.ShapeDtypeStruct((L, D), F32),
                   jax.ShapeDtypeStruct((L, D), F32)],
        compiler_params=_cp("parallel"),
    )(P, P, P, P, P, P, P, P, o_f, o_b, conv_a, ghn4, w_out, g2, x)


NFF = 2
WFF = DFF // NFF
FFN_COL_CHUNKS = ((0, 512), (512, 1024), (1024, WFF))


def ffn_down(U, conv_ffn, w_down, g4, x1, tl):
    L = U.shape[0]
    nt = L // tl

    def body(u, up, un, cf, wd, g4r, x1r, y2, x2, ug, uv, zr):
        i = pl.program_id(0)
        acc = jnp.zeros((tl, D), F32)
        for j in range(NFF):
            gs = slice(j * WFF, (j + 1) * WFF)
            vs = slice(DFF + j * WFF, DFF + (j + 1) * WFF)
            z = []
            for s in (gs, vs):
                p, n = _halo_rows(up.at[:, s], un.at[:, s], i, nt - 1)
                z.append(_conv3(u[:, s].astype(F32), p, n, cf.at[:, s])[0])
            zz = (z[0] * _sigmoid(z[0]) * z[1]).astype(BF16)
            ug[:, gs] = z[0].astype(BF16)
            uv[:, gs] = z[1].astype(BF16)
            zr[:, gs] = zz
            acc = acc + _mm(zz, wd[gs, :])
        y2[...] = acc
        x2[...] = x1r[...] + _rms(acc, g4r[...])

    row = lambda n: pl.BlockSpec((tl, n), lambda i: (i, 0))
    full = lambda a: pl.BlockSpec(a.shape, lambda i: (0, 0))
    half = jax.ShapeDtypeStruct((L, DFF), BF16)
    return pl.pallas_call(
        body, name="ffn_down", grid=(nt,),
        in_specs=[row(2 * DFF), pl.BlockSpec((HALO16, 2 * DFF), _prev_blk(tl, 0, HALO16)),
                  pl.BlockSpec((HALO16, 2 * DFF), _next_blk(tl, L, 0, HALO16)),
                  full(conv_ffn), full(w_down), full(g4), row(D)],
        out_specs=[row(D), row(D), row(DFF), row(DFF), row(DFF)],
        out_shape=[jax.ShapeDtypeStruct((L, D), F32), jax.ShapeDtypeStruct((L, D), F32), half, half, half],
        compiler_params=_cp("parallel"),
    )(U, U, U, conv_ffn, w_down, g4, x1)


def ffn_fwd(x1, g3, w_up, conv_ffn, w_down, g4, tl):
    L = x1.shape[0]
    nt = L // tl
    hh = HALO16

    def body(xr, xp, xn, g3r, wu, cf, wd, g4r, U, h2, ug, uv, zr, y2, x2):
        i = pl.program_id(0)
        he = _rms(jnp.concatenate([xp[...], xr[...], xn[...]], axis=0), g3r[...]).astype(BF16)
        h2[...] = he[hh:hh + tl]
        acc = jnp.zeros((tl, D), F32)
        for j in range(NFF):
            conv = []
            for blk, off in ((j, j * WFF), (NFF + j, DFF + j * WFF)):
                ue = _mm(he, wu[blk])
                p8 = jnp.where(i == 0, 0.0, ue[hh - 8:hh])
                n8 = jnp.where(i == nt - 1, 0.0, ue[hh + tl:hh + tl + 8])
                mid = ue[hh:hh + tl]
                U[:, off:off + WFF] = mid.astype(BF16)
                conv.append(_conv3(mid, p8, n8, cf.at[:, off:off + WFF])[0])
            gs = slice(j * WFF, (j + 1) * WFF)
            zz = (conv[0] * _sigmoid(conv[0]) * conv[1]).astype(BF16)
            ug[:, gs] = conv[0].astype(BF16)
            uv[:, gs] = conv[1].astype(BF16)
            zr[:, gs] = zz
            acc = acc + _mm(zz, wd[gs, :])
        y2[...] = acc
        x2[...] = xr[...] + _rms(acc, g4r[...])

    row = lambda n: pl.BlockSpec((tl, n), lambda i: (i, 0))
    full = lambda a: pl.BlockSpec(a.shape, lambda i: (0,) * a.ndim)
    once = lambda a: pl.BlockSpec(a.shape, lambda i: (0,) * a.ndim, pipeline_mode=pl.Buffered(1))
    half = jax.ShapeDtypeStruct((L, DFF), BF16)
    return pl.pallas_call(
        body, name="ffn_fwd", grid=(nt,),
        in_specs=[row(D), pl.BlockSpec((hh, D), lambda i: (_prev_row_blk(i, tl, hh), 0)),
                  pl.BlockSpec((hh, D), lambda i: (_next_row_blk(i, tl, L, hh), 0)),
                  full(g3), once(w_up), full(conv_ffn), once(w_down), full(g4)],
        out_specs=[row(2 * DFF), row(D), row(DFF), row(DFF), row(DFF), row(D), row(D)],
        out_shape=[jax.ShapeDtypeStruct((L, 2 * DFF), BF16), jax.ShapeDtypeStruct((L, D), BF16), half, half, half,
                   jax.ShapeDtypeStruct((L, D), F32), jax.ShapeDtypeStruct((L, D), F32)],
        compiler_params=_cp("parallel"),
    )(x1, x1, x1, g3, w_up, conv_ffn, w_down, g4)


def loss_head(y, target, tl):
    L = y.shape[0]

    def body(yr, tr, dy, ls):
        @pl.when(pl.program_id(0) == 0)
        def _():
            ls[...] = jnp.zeros_like(ls)

        err = yr[...] - tr[...]
        dy[...] = err * (1.0 / D)
        ls[...] += (0.5 / D) * jnp.sum(err * err)

    row = pl.BlockSpec((tl, D), lambda i: (i, 0))
    return pl.pallas_call(
        body, name="loss_head", grid=(L // tl,), in_specs=[row, row],
        out_specs=[row, pl.BlockSpec((8, 128), lambda i: (0, 0))],
        out_shape=[jax.ShapeDtypeStruct((L, D), F32), jax.ShapeDtypeStruct((8, 128), F32)],
        compiler_params=_cp("arbitrary"),
    )(y, target)


def rms_bwd_pre(dout, y, g, tl, after=()):
    L = y.shape[0]

    def body(dr, yr, gr, *rest):
        dy, dg = rest[-2:]

        @pl.when(pl.program_id(0) == 0)
        def _():
            dg[...] = jnp.zeros_like(dg)

        a, b = _rms_bwd(dr[...], yr[...], gr[...])
        dy[...] = a.astype(BF16)
        dg[...] += b

    row = pl.BlockSpec((tl, D), lambda i: (i, 0))
    vec = pl.BlockSpec((1, D), lambda i: (0, 0))
    return pl.pallas_call(
        body, name="rms_bwd_pre", grid=(L // tl,), in_specs=[row, row, vec] + [_ANY] * len(after), out_specs=[row, vec],
        out_shape=[jax.ShapeDtypeStruct((L, D), BF16), jax.ShapeDtypeStruct((1, D), F32)],
        compiler_params=_cp("arbitrary"),
    )(dout, y, g, *after)


def ffn_bwd1(dy2, ug, uv, w_down, tl):
    L = ug.shape[0]

    def body(dy, ugr, uvr, wd, dug, duv):
        a = ugr[...].astype(F32)
        b = uvr[...].astype(F32)
        sg = _sigmoid(a)
        silu = a * sg
        dz = _mm_nt(dy[...], wd[...])
        dug[...] = (dz * b * (sg + silu * (1.0 - sg))).astype(BF16)
        duv[...] = (dz * silu).astype(BF16)

    tile = pl.BlockSpec((tl, WFF), lambda j, i: (i, j))
    half = jax.ShapeDtypeStruct((L, DFF), BF16)
    return pl.pallas_call(
        body, name="ffn_bwd1", grid=(NFF, L // tl),
        in_specs=[pl.BlockSpec((tl, D), lambda j, i: (i, 0)), tile, tile, pl.BlockSpec((WFF, D), lambda j, i: (j, 0))],
        out_specs=[tile, tile], out_shape=[half, half],
        compiler_params=_cp("parallel", "parallel"),
    )(dy2, ug, uv, w_down)


def ffn_bwd2(du_g, du_v, U, conv_ffn, w_up, x1, dres, g3, tl):
    L = x1.shape[0]
    nt = L // tl

    def body(dg_, dv_, dgp, dgn, dvp, dvn, ugr, uvr, cg, cv, wg, wv, x1r, drr, g3r, dUg, dUv, dx1, dg3, dcg, dcv, acc):
        i = pl.program_id(0)
        j = pl.program_id(1)

        @pl.when((i == 0) & (j == 0))
        def _():
            dg3[...] = jnp.zeros_like(dg3)
            dcg[...] = jnp.zeros_like(dcg)
            dcv[...] = jnp.zeros_like(dcv)

        part = None
        for c0, c1 in FFN_COL_CHUNKS:
            cs = slice(c0, c1)
            for d_ref, dp_ref, dn_ref, cw_ref, u_ref, dc, dU, w in ((dg_, dgp, dgn, cg, ugr, dcg, dUg, wg),
                                                                    (dv_, dvp, dvn, cv, uvr, dcv, dUv, wv)):
                p8, n8 = _halo_rows(dp_ref.at[:, cs], dn_ref.at[:, cs], i, nt - 1)
                d = d_ref[:, cs].astype(F32)
                dm1, dp1 = _shifts(d, p8, n8)
                du = (cw_ref[0:1, cs] * dp1 + cw_ref[1:2, cs] * d + cw_ref[2:3, cs] * dm1).astype(BF16)
                dU[:, cs] = du
                u = u_ref[:, cs].astype(F32)
                for k, t in enumerate((dp1, d, dm1)):
                    dc[j, k:k + 1, cs] += jnp.sum(t * u, axis=0, keepdims=True)
                term = _mm_nt(du, w[:, cs])
                part = term if part is None else part + term

        @pl.when(j == 0)
        def _():
            acc[...] = part

        @pl.when(j > 0)
        def _():
            acc[...] += part

        @pl.when(j == NFF - 1)
        def _():
            dx, dg = _rms_bwd(acc[...], x1r[...], g3r[...])
            dx1[...] = drr[...] + dx
            dg3[...] += dg

    tile = pl.BlockSpec((tl, WFF), lambda i, j: (i, j))
    prev = pl.BlockSpec((HALO16, WFF), lambda i, j: (_prev_row_blk(i, tl, HALO16), j))
    nxt = pl.BlockSpec((HALO16, WFF), lambda i, j: (_next_row_blk(i, tl, L, HALO16), j))
    cw = lambda off: pl.BlockSpec((3, WFF), lambda i, j: (0, off + j))
    ww = lambda off: pl.BlockSpec((None, D, WFF), lambda i, j: (off + j, 0, 0))
    row = pl.BlockSpec((tl, D), lambda i, j: (i, 0))
    vec = pl.BlockSpec((1, D), lambda i, j: (0, 0))
    ut = lambda off: pl.BlockSpec((tl, WFF), lambda i, j: (i, off + j))
    dcs = pl.BlockSpec((NFF, 8, WFF), lambda i, j: (0, 0, 0))
    return pl.pallas_call(
        body, name="ffn_bwd2", grid=(nt, NFF),
        in_specs=[tile, tile, prev, nxt, prev, nxt, ut(0), ut(NFF), cw(0), cw(NFF), ww(0), ww(NFF), row, row, vec],
        out_specs=[tile, tile, row, vec, dcs, dcs],
        out_shape=[jax.ShapeDtypeStruct((L, DFF), BF16), jax.ShapeDtypeStruct((L, DFF), BF16),
                   jax.ShapeDtypeStruct((L, D), F32), jax.ShapeDtypeStruct((1, D), F32),
                   jax.ShapeDtypeStruct((NFF, 8, WFF), F32), jax.ShapeDtypeStruct((NFF, 8, WFF), F32)],
        scratch_shapes=[pltpu.VMEM((tl, D), F32)],
        compiler_params=_cp("arbitrary", "arbitrary"),
    )(du_g, du_v, du_g, du_g, du_v, du_v, U, U, conv_ffn, conv_ffn, w_up, w_up, x1, dres, g3)


def ffn_bwd(dy2, ug, uv, U, conv_ffn, w_down, w_up, x1, dres, g3, tl):
    L = x1.shape[0]
    nt = L // tl
    hh = HALO16

    def body(dyr, dyp, dyn, ugr, ugp, ugn, uvr, uvp, uvn, Ur, cf, wd, wu, x1r, drr, g3r, dUg, dUv, dx1, dg3, dcg, dcv):
        i = pl.program_id(0)

        @pl.when(i == 0)
        def _():
            dg3[...] = jnp.zeros_like(dg3)
            dcg[...] = jnp.zeros_like(dcg)
            dcv[...] = jnp.zeros_like(dcv)

        ext = lambda p, t, n, cs: jnp.concatenate([p[:, cs], t[:, cs], n[:, cs]], axis=0).astype(F32)
        dye = jnp.concatenate([dyp[...], dyr[...], dyn[...]], axis=0)
        acc = jnp.zeros((tl, D), F32)
        for j in range(NFF):
            dze = _mm_nt(dye, wd[j * WFF:(j + 1) * WFF, :])
            for c0, c1 in FFN_COL_CHUNKS:
                cs = slice(j * WFF + c0, j * WFF + c1)
                a = ext(ugp, ugr, ugn, cs)
                b = ext(uvp, uvr, uvn, cs)
                sg = _sigmoid(a)
                silu = a * sg
                dz = dze[:, c0:c1]
                for de, off, blk, dc, dU in ((dz * b * (sg + silu * (1.0 - sg)), 0, j, dcg, dUg),
                                            (dz * silu, DFF, NFF + j, dcv, dUv)):
                    d = de[hh:hh + tl]
                    p8 = jnp.where(i == 0, 0.0, de[hh - 8:hh])
                    n8 = jnp.where(i == nt - 1, 0.0, de[hh + tl:hh + tl + 8])
                    dm1, dp1 = _shifts(d, p8, n8)
                    wc = slice(off + j * WFF + c0, off + j * WFF + c1)
                    du = (cf[0:1, wc] * dp1 + cf[1:2, wc] * d + cf[2:3, wc] * dm1).astype(BF16)
                    dU[:, cs] = du
                    u = Ur[:, wc].astype(F32)
                    for k, t in enumerate((dp1, d, dm1)):
                        dc[k:k + 1, cs] += jnp.sum(t * u, axis=0, keepdims=True)
                    acc = acc + _mm_nt(du, wu[blk, :, c0:c1])
        dx, dg = _rms_bwd(acc, x1r[...], g3r[...])
        dx1[...] = drr[...] + dx
        dg3[...] += dg

    row = lambda n: pl.BlockSpec((tl, n), lambda i: (i, 0))
    prev = lambda n: pl.BlockSpec((hh, n), lambda i: (_prev_row_blk(i, tl, hh), 0))
    nxt = lambda n: pl.BlockSpec((hh, n), lambda i: (_next_row_blk(i, tl, L, hh), 0))
    full = lambda a: pl.BlockSpec(a.shape, lambda i: (0,) * a.ndim)
    once = lambda a: pl.BlockSpec(a.shape, lambda i: (0,) * a.ndim, pipeline_mode=pl.Buffered(1))
    half = jax.ShapeDtypeStruct((L, DFF), BF16)
    dcs = pl.BlockSpec((8, DFF), lambda i: (0, 0))
    return pl.pallas_call(
        body, name="ffn_bwd", grid=(nt,),
        in_specs=[row(D), prev(D), nxt(D), row(DFF), prev(DFF), nxt(DFF), row(DFF), prev(DFF), nxt(DFF), row(2 * DFF),
                  full(conv_ffn), once(w_down), once(w_up), row(D), row(D), full(g3)],
        out_specs=[row(DFF), row(DFF), row(D), pl.BlockSpec((1, D), lambda i: (0, 0)), dcs, dcs],
        out_shape=[half, half, jax.ShapeDtypeStruct((L, D), F32), jax.ShapeDtypeStruct((1, D), F32),
                   jax.ShapeDtypeStruct((8, DFF), F32), jax.ShapeDtypeStruct((8, DFF), F32)],
        compiler_params=_cp("arbitrary"),
    )(dy2, dy2, dy2, ug, ug, ug, uv, uv, uv, U, conv_ffn, w_down, w_up, x1, dres, g3)


def matmul_tn(a, b, ta, tn, tl, name, into=None, after=()):
    L, Ka = a.shape
    N = b.shape[1]

    def body(ar, br, *rest):
        o = rest[-1]

        @pl.when(pl.program_id(2) == 0)
        def _():
            o[...] = jnp.zeros_like(o)

        o[...] += _mm_tn(ar[...], br[...]).reshape(o.shape)

    in_specs = [pl.BlockSpec((tl, ta), lambda p, q, l: (l, p)), pl.BlockSpec((tl, tn), lambda p, q, l: (l, q))]
    if into is None:
        return pl.pallas_call(
            body, name=name, grid=(Ka // ta, N // tn, L // tl), in_specs=in_specs + [_ANY] * len(after),
            out_specs=pl.BlockSpec((ta, tn), lambda p, q, l: (p, q)),
            out_shape=jax.ShapeDtypeStruct((Ka, N), F32),
            compiler_params=_cp("parallel", "parallel", "arbitrary"),
        )(a, b, *after)
    buf, blk, idx = into
    return pl.pallas_call(
        body, name=name, grid=(Ka // ta, N // tn, L // tl), in_specs=in_specs + [_ANY],
        out_specs=pl.BlockSpec(blk, lambda p, q, l: idx(p, q)),
        out_shape=jax.ShapeDtypeStruct(buf.shape, F32), input_output_aliases={2: 0},
        compiler_params=_cp("parallel", "parallel", "arbitrary"),
    )(a, b, buf)


def mix_bwd1(dy, w_out, P, o_f, o_b, conv_a, ghn4, tl):
    L = P.shape[0]
    nt = L // tl

    def body(dyr, wo, gb, gc, gv, go, gcp, gvp, gcn, gvn, of, ob, ca, gh, dgb, dcc, dgo, do, dca, dgh):
        i = pl.program_id(0)

        @pl.when(i == 0)
        def _():
            dca[...] = jnp.zeros_like(dca)
            dgh[...] = jnp.zeros_like(dgh)

        dycat = _mm_nt(dyr[...], wo[...])
        dya = dycat[:, 0:DC]
        dyb = dycat[:, DC:D]
        cp, cn = _halo_rows(gcp, gcn, i, nt - 1)
        vp, vn = _halo_rows(gvp, gvn, i, nt - 1)
        c = gc[...].astype(F32) * gv[...].astype(F32)
        cc, c_m1, c_p1 = _conv3(c, cp * vp, cn * vn, ca)
        dgb[...] = (dya * cc).astype(BF16)
        d = dya * gb[...].astype(F32)
        dcc[...] = d.astype(BF16)
        for k, s in enumerate((c_m1, c, c_p1)):
            dca[k:k + 1, :] += jnp.sum(d * s, axis=0, keepdims=True)
        oh, rs = _headnorm(of[...].astype(F32) + ob[...].astype(F32))
        g = go[...].astype(F32)
        sg = _sigmoid(g)
        silu = g * sg
        dgo[...] = (dyb * (oh * gh[...]) * (sg * (1.0 + g * (1.0 - sg)))).astype(BF16)
        don = dyb * silu
        t = jnp.sum(don * oh, axis=0, keepdims=True)
        dgh[0:1, :] += t[:, 0:HV] + t[:, HV:2 * HV] + t[:, 2 * HV:3 * HV] + t[:, 3 * HV:4 * HV]
        doh = don * gh[...]
        parts = []
        for h in range(NH):
            hs = slice(h * HV, (h + 1) * HV)
            parts.append(rs[h] * (doh[:, hs] - oh[:, hs] * jnp.mean(doh[:, hs] * oh[:, hs], axis=-1, keepdims=True)))
        do[...] = jnp.concatenate(parts, axis=1).astype(BF16)

    t = lambda cb: pl.BlockSpec((tl, DC), lambda i: (i, cb))
    hp = lambda cb: pl.BlockSpec((HALO16, DC), _prev_blk(tl, cb, HALO16))
    hn = lambda cb: pl.BlockSpec((HALO16, DC), _next_blk(tl, L, cb, HALO16))
    row = lambda n: pl.BlockSpec((tl, n), lambda i: (i, 0))
    full = lambda a: pl.BlockSpec(a.shape, lambda i: (0, 0))
    act16 = lambda n: jax.ShapeDtypeStruct((L, n), BF16)
    return pl.pallas_call(
        body, name="mix_bwd1", grid=(nt,),
        in_specs=[row(D), full(w_out), t(CB_GB), t(CB_GC), t(CB_GV), t(CB_GO), hp(CB_GC), hp(CB_GV), hn(CB_GC), hn(CB_GV),
                  row(DG), row(DG), full(conv_a), full(ghn4)],
        out_specs=[row(DC), row(DC), row(DG), row(DG), pl.BlockSpec((8, DC), lambda i: (0, 0)),
                   pl.BlockSpec((8, HV), lambda i: (0, 0))],
        out_shape=[act16(DC), act16(DC), act16(DG), act16(DG), jax.ShapeDtypeStruct((8, DC), F32),
                   jax.ShapeDtypeStruct((8, HV), F32)],
        compiler_params=_cp("arbitrary"),
    )(dy, w_out, P, P, P, P, P, P, P, P, o_f, o_b, conv_a, ghn4)


def _gla_chunk_bwd(qt, kt, kh, v, do, st16, dec, g_ref, m, rev):
    qt16, kt16, kh16, v16, do16 = (t.astype(BF16) for t in (qt, kt, kh, v, do))
    qs, sc = _gla_scores(qt16, kt16, m, rev)
    g = g_ref[...]
    g16 = g.astype(BF16)
    dob = _tile4(do16) * m["bdo"].astype(BF16)
    dv = _mm_tn(sc, dob) + _mm_nt(kh16, g16)
    dsc = jnp.where(m["triu4"] if rev else m["tril4"], _mm_nt(dob, v16), 0.0)
    r1 = _mm(dsc, kt16) * m["bdq"]
    dqt = r1[0:CH] + r1[CH:2 * CH] + r1[2 * CH:3 * CH] + r1[3 * CH:4 * CH] + _mm(do16, st16)
    dkt = _mm_tn(dsc, qs)
    dkh = _mm(v16, g16)
    dd = jnp.sum(g * st16.astype(F32), axis=0, keepdims=True)
    g_ref[...] = g * dec + _mm_tn(do16, qt16) * m["bds"]
    return dv, dqt, dkt, dkh, dd


def gla_bwd(P, do, sf, sb, gcat, gbias, tl):
    L = P.shape[0]
    nb = L // tl
    nc = tl // CH

    def body(qf, kf, vf, lf, dof, sfr, qb, kb, vb, lb, dob, sbr, gc_ref, bs_ref,
             dqf, dkf, dvf, daf, dqb, dkb, dvb, dab, gf, gbk, *scr):
        @pl.when(pl.program_id(0) == 0)
        def _():
            gf[...] = jnp.zeros_like(gf)
            gbk[...] = jnp.zeros_like(gbk)

        m = _gla_masks()
        keys = ("qt", "kt", "kh", "e", "einv", "eout", "dec")
        names = keys + ("dd", "dqt", "dkt", "dkh")
        pf = dict(zip(names, scr[0:11]))
        pb = dict(zip(names, scr[11:22]))
        for (q, k, lr, cols, rev, pr) in ((qf, kf, lf, slice(0, DK), False, pf), (qb, kb, lb, slice(DK, 2 * DK), True, pb)):
            p = _gla_tile_prep(q[...], k[...], _gates(lr, gc_ref, bs_ref, cols), m, rev, nc)
            for key in keys:
                pr[key][...] = p[key]

        def step(c, v, dor, st, g_ref, pr, dv, rev):
            rows = pl.ds(pl.multiple_of(c * CH, CH), CH)
            dvc, dqt, dkt, dkh, dd = _gla_chunk_bwd(pr["qt"][rows, :], pr["kt"][rows, :], pr["kh"][rows, :], v[rows, :],
                                                    dor[rows, :], st[c], pr["dec"][pl.ds(c, 1), :], g_ref, m, rev)
            dv[rows, :] = dvc.astype(BF16)
            pr["dqt"][rows, :] = dqt
            pr["dkt"][rows, :] = dkt
            pr["dkh"][rows, :] = dkh
            pr["dd"][pl.ds(c, 1), :] = dd

        def chunk(c, carry):
            step(nc - 1 - c, vf, dof, sfr, gf, pf, dvf, False)
            step(c, vb, dob, sbr, gbk, pb, dvb, True)
            return carry

        lax.fori_loop(0, nc, chunk, 0, unroll=2)

        def finish(pr, dq, dk, da, rev):
            dqt, dkt, dkh = pr["dqt"][...], pr["dkt"][...], pr["dkh"][...]
            kk = dkh * pr["kh"][...]
            dcum = dqt * pr["qt"][...] - dkt * pr["kt"][...] - kk
            dtot = pr["dd"][...] * pr["dec"][...]
            tri_t = m["tril"] if rev else m["triu"]
            parts = []
            for c in range(nc):
                rs = slice(c * CH, (c + 1) * CH)
                parts.append(_mm_tri(tri_t, dcum[rs]) + (jnp.sum(kk[rs], axis=0, keepdims=True) + dtot[c:c + 1]))
            da[...] = jnp.concatenate(parts, axis=0).astype(BF16)
            dq[...] = (dqt * pr["e"][...] * QSCALE).astype(BF16)
            dk[...] = (dkt * pr["einv"][...] + dkh * pr["eout"][...]).astype(BF16)

        finish(pf, dqf, dkf, daf, False)
        finish(pb, dqb, dkb, dab, True)

    fwd_dir = lambda cb: (lambda i: (nb - 1 - i, cb))
    bwd_dir = lambda cb: (lambda i: (i, cb))

    def side(ix):
        return [pl.BlockSpec((tl, DK), ix(CB_Q)), pl.BlockSpec((tl, DK), ix(CB_K)), pl.BlockSpec((tl, DG), ix(CB_V)),
                pl.BlockSpec((tl, LRW), ix(CB_LR)), pl.BlockSpec((tl, DG), ix(0)),
                pl.BlockSpec((nc, DG, DK), lambda i: (ix(0)(i)[0], 0, 0))]

    def outs(ix):
        return [pl.BlockSpec((tl, DK), ix(0)), pl.BlockSpec((tl, DK), ix(0)), pl.BlockSpec((tl, DG), ix(0)),
                pl.BlockSpec((tl, DK), ix(0))]

    o_shape = [jax.ShapeDtypeStruct((L, DK), BF16), jax.ShapeDtypeStruct((L, DK), BF16),
               jax.ShapeDtypeStruct((L, DG), BF16), jax.ShapeDtypeStruct((L, DK), BF16)]
    return pl.pallas_call(
        body, name="gla_bwd", grid=(nb,),
        in_specs=side(fwd_dir) + side(bwd_dir) + [pl.BlockSpec((LRW, 2 * DK), lambda i: (0, 0)),
                                                  pl.BlockSpec((1, 2 * DK), lambda i: (0, 0))],
        out_specs=outs(fwd_dir) + outs(bwd_dir),
        out_shape=o_shape + o_shape,
        scratch_shapes=[pltpu.VMEM((DG, DK), F32), pltpu.VMEM((DG, DK), F32)]
        + ([pltpu.VMEM((tl, DK), F32)] * 6 + [pltpu.VMEM((nc, DK), F32)] * 2 + [pltpu.VMEM((tl, DK), F32)] * 3) * 2,
        compiler_params=_cp("arbitrary"),
    )(P, P, P, P, do, sf, P, P, P, P, do, sb, gcat, gbias)


def mix_bwd2(dgb, dcc, dgo, gl, P, conv_a, gcat, gbias, w_in, x, dres, g1, tl):
    L = P.shape[0]
    nt = L // tl

    def body(dgbr, dccr, dccp, dccn, dgor, dqf, dkf, dvf, daf, dqb, dkb, dvb, dab, gc, gv, lr, ca, gcr, bsr, wi,
             xr, drr, g1r, dP, dx, dg1, dgcat, dbias):
        i = pl.program_id(0)

        @pl.when(i == 0)
        def _():
            dg1[...] = jnp.zeros_like(dg1)
            dgcat[...] = jnp.zeros_like(dgcat)
            dbias[...] = jnp.zeros_like(dbias)

        p, n = _halo_rows(dccp, dccn, i, nt - 1)
        dc = _conv3_t(dccr[...].astype(F32), p, n, ca)
        pre = _mm(lr[...], gcr[...]) + bsr[...]
        da = jnp.concatenate([daf[...], dab[...]], axis=1).astype(F32)
        add32 = lambda a, b: a[...].astype(F32) + b[...].astype(F32)
        dpre = da * GATE_NORM * (1.0 - _sigmoid(pre))
        dpre16 = dpre.astype(BF16)
        dP[:, 0:DC] = dgbr[...].astype(BF16)
        dP[:, DC:2 * DC] = (dc * gv[...].astype(F32)).astype(BF16)
        dP[:, 2 * DC:3 * DC] = (dc * gc[...].astype(F32)).astype(BF16)
        dP[:, 1536:1792] = add32(dqf, dqb).astype(BF16)
        dP[:, 1792:2048] = add32(dkf, dkb).astype(BF16)
        dP[:, 2048:2560] = add32(dvf, dvb).astype(BF16)
        dP[:, 2560:3072] = dgor[...].astype(BF16)
        dP[:, 3072:3200] = _mm_nt(dpre16, gcr[...]).astype(BF16)
        dgcat[...] += _mm_tn(lr[...], dpre16)
        dbias[0:1, :] += jnp.sum(dpre, axis=0, keepdims=True)
        dh, dg = _rms_bwd(_mm_nt(dP[...], wi[...]), xr[...], g1r[...])
        dx[...] = drr[...] + dh
        dg1[...] += dg

    row = lambda n: pl.BlockSpec((tl, n), lambda i: (i, 0))
    t = lambda w, cb: pl.BlockSpec((tl, w), lambda i: (i, cb))
    full = lambda a: pl.BlockSpec(a.shape, lambda i: (0, 0))
    return pl.pallas_call(
        body, name="mix_bwd2", grid=(nt,),
        in_specs=[row(DC), row(DC), pl.BlockSpec((HALO16, DC), _prev_blk(tl, 0, HALO16)),
                  pl.BlockSpec((HALO16, DC), _next_blk(tl, L, 0, HALO16)),
                  row(DG), row(DK), row(DK), row(DG), row(DK), row(DK), row(DK), row(DG), row(DK),
                  t(DC, CB_GC), t(DC, CB_GV), t(LRW, CB_LR), full(conv_a), full(gcat), full(gbias), full(w_in),
                  row(D), row(D), full(g1)],
        out_specs=[row(DINP), row(D), pl.BlockSpec((1, D), lambda i: (0, 0)), pl.BlockSpec((LRW, 2 * DK), lambda i: (0, 0)),
                   pl.BlockSpec((8, 2 * DK), lambda i: (0, 0))],
        out_shape=[jax.ShapeDtypeStruct((L, DINP), BF16), jax.ShapeDtypeStruct((L, D), F32),
                   jax.ShapeDtypeStruct((1, D), F32), jax.ShapeDtypeStruct((LRW, 2 * DK), F32),
                   jax.ShapeDtypeStruct((8, 2 * DK), F32)],
        compiler_params=_cp("arbitrary"),
    )(dgb, dcc, dcc, dcc, dgo, *gl, P, P, P, conv_a, gcat, gbias, w_in, x, dres, g1)


def _row_tile(rows, cols):
    if rows * cols * 4 <= 2 * 1024 * 1024:
        return rows
    best = 8
    for t in range(8, rows, 8):
        if rows % t == 0 and t * cols * 4 <= 2 * 1024 * 1024:
            best = t
    return best


def adamw(w, g, m, v, name):
    shape = w.shape
    cols = shape[-1]
    w2, g2, m2, v2 = (a.reshape(-1, cols) for a in (w, g, m, v))
    rows = w2.shape[0]
    tr = _row_tile(rows, cols)

    def body(wr, gr, mr, vr, dl, nm, nv):
        gg = gr[...]
        mm = B1 * mr[...] + (1.0 - B1) * gg
        vv = B2 * vr[...] + (1.0 - B2) * (gg * gg)
        m_hat = mm / (1.0 - B1 ** STEP)
        v_hat = vv / (1.0 - B2 ** STEP)
        dl[...] = -LR * (m_hat / (jnp.sqrt(v_hat) + AEPS) + WD * wr[...])
        nm[...] = mm
        nv[...] = vv

    blk = pl.BlockSpec((tr, cols), lambda i: (i, 0))
    o = jax.ShapeDtypeStruct((rows, cols), F32)
    d, nm, nv = pl.pallas_call(
        body, name=name, grid=(rows // tr,), in_specs=[blk] * 4, out_specs=[blk] * 3, out_shape=[o, o, o],
        compiler_params=_cp("parallel"),
    )(w2, g2, m2, v2)
    return d.reshape(shape), nm.reshape(shape), nv.reshape(shape)


def _place():
    return lax.axis_index("x"), lax.axis_index("y"), lax.axis_index("c")


def allgather8(v, name):
    mp, n = v.shape

    def body(x_ref, out_ref, send_sems, recv_sems, local_sem):
        x, y, c = _place()
        me, sibling = (x, y, c), (x, y, 1 - c)
        chips = [(1 - x, y), (x, 1 - y), (1 - x, 1 - y)]

        def rows(px, py, pc):
            return out_ref.at[pl.ds((4 * px + 2 * py + pc) * mp, mp), :]

        def copy(k, block, to, src=None):
            return pltpu.make_async_remote_copy(
                src_ref=rows(*block) if src is None else src, dst_ref=rows(*block),
                send_sem=send_sems.at[k], recv_sem=recv_sems.at[k], device_id=to, device_id_type=MESH)

        mine = pltpu.make_async_copy(x_ref, rows(*me), local_sem)
        mine.start()
        first = [copy(0, me, sibling, src=x_ref)]
        first += [copy(1 + j, me, (*chip, c), src=x_ref) for j, chip in enumerate(chips)]
        for cp in first:
            cp.start()
        passed = [copy(4 + j, (*chip, c), sibling) for j, chip in enumerate(chips)]
        for j, chip in enumerate(chips):
            copy(1 + j, (*chip, c), me).wait_recv()
            passed[j].start()
        copy(0, sibling, me).wait_recv()
        for j, chip in enumerate(chips):
            copy(4 + j, (*chip, 1 - c), me).wait_recv()
        for cp in first + passed:
            cp.wait_send()
        mine.wait()

    return pl.pallas_call(
        body, name=name, out_shape=jax.ShapeDtypeStruct((8 * mp, n), v.dtype),
        in_specs=[pl.BlockSpec(memory_space=pltpu.VMEM)], out_specs=pl.BlockSpec(memory_space=pltpu.VMEM),
        scratch_shapes=[pltpu.SemaphoreType.DMA((7,)), pltpu.SemaphoreType.DMA((7,)), pltpu.SemaphoreType.DMA],
        compiler_params=pltpu.CompilerParams(vmem_limit_bytes=VMEM_LIMIT),
    )(v)


def sum8(v, mp):
    def body(x_ref, o_ref):
        acc = x_ref[0:mp, :]
        for d in range(1, 8):
            acc = acc + x_ref[d * mp:(d + 1) * mp, :]
        o_ref[...] = acc

    return pl.pallas_call(body, name="sum8", out_shape=jax.ShapeDtypeStruct((mp, v.shape[1]), F32),
                          compiler_params=pltpu.CompilerParams(vmem_limit_bytes=VMEM_LIMIT))(v)


_ANY = pl.BlockSpec(memory_space=pl.ANY)


def _row_half(ref, lead, h):
    hr = ref.shape[-2] // 2
    return ref.at[(*lead, pl.ds(h * hr, hr), slice(None))]


def allgather_weights(slots):
    n = len(slots)

    def body(*refs):
        s_refs, o_refs, (send_sems, recv_sems) = refs[:n], refs[n:2 * n], refs[2 * n:]
        x, y, c = _place()
        me = 2 * x + y
        sibling = (x, y, 1 - c)
        chips = [(1 - x, y), (x, 1 - y), (1 - x, 1 - y)]

        def half(ref, slot, h):
            return _row_half(ref, (slot, slice(None)), h)

        def copy(k, src, dst, to):
            return pltpu.make_async_remote_copy(src_ref=src, dst_ref=dst, send_sem=send_sems.at[k],
                                                recv_sem=recv_sems.at[k], device_id=to, device_id_type=MESH)

        first = [copy(6 * a + k, half(s_refs[a], me, c), half(o_refs[a], me, c), (px, py, c))
                 for k, (px, py) in enumerate(chips) for a in range(n)]
        for cp in first:
            cp.start()
        passed = []
        for k, (px, py) in enumerate(chips):
            for a in range(n):
                got = half(o_refs[a], 2 * px + py, c)
                copy(6 * a + k, half(s_refs[a], me, c), got, (px, py, c)).wait_recv()
                cp = copy(6 * a + 3 + k, got, got, sibling)
                cp.start()
                passed.append(cp)
        for k, (px, py) in enumerate(chips):
            for a in range(n):
                got = half(o_refs[a], 2 * px + py, 1 - c)
                copy(6 * a + 3 + k, got, got, sibling).wait_recv()
        for cp in first + passed:
            cp.wait_send()

    return pl.pallas_call(
        body, name="allgather_weights", out_shape=[jax.ShapeDtypeStruct(s.shape, s.dtype) for s in slots],
        in_specs=[_ANY] * n, out_specs=[_ANY] * n, input_output_aliases={a: a for a in range(n)},
        scratch_shapes=[pltpu.SemaphoreType.DMA((6 * n,)), pltpu.SemaphoreType.DMA((6 * n,))],
    )(*slots)


_HBM = pl.BlockSpec(memory_space=pltpu.HBM)
_SEM = pl.BlockSpec(memory_space=pltpu.SEMAPHORE)
_EFFECT = pltpu.SideEffectType.DATAFLOW_SIDE_EFFECTING


def gather_start(slots, name, after=()):
    n = len(slots)
    na = len(after)

    def body(*refs):
        s_refs, send_sems, recv_sems, token = refs[:n], refs[n + na], refs[n + na + 1], refs[-1]
        x, y, c = _place()
        me = 2 * x + y
        for k, (px, py) in enumerate([(1 - x, y), (x, 1 - y), (1 - x, 1 - y)]):
            for a in range(n):
                pltpu.make_async_remote_copy(
                    src_ref=s_refs[a].at[me], dst_ref=s_refs[a].at[me], send_sem=send_sems.at[3 * a + k],
                    recv_sem=recv_sems.at[3 * a + k], device_id=(px, py, c), device_id_type=MESH).start()
        token[...] = jnp.zeros_like(token)

    out = pl.pallas_call(
        body, name=name,
        out_shape=(pltpu.SemaphoreType.DMA((3 * n,)), pltpu.SemaphoreType.DMA((3 * n,)),
                   *[pltpu.HBM(s.shape, s.dtype) for s in slots], jax.ShapeDtypeStruct((8, 128), F32)),
        in_specs=[_HBM] * n + [_ANY] * na, out_specs=(_SEM, _SEM, *[_HBM] * n, pl.BlockSpec(memory_space=pltpu.VMEM)),
        input_output_aliases={a: 2 + a for a in range(n)},
        compiler_params=pltpu.CompilerParams(has_side_effects=_EFFECT),
    )(*[pltpu.with_memory_space_constraint(s, pltpu.HBM) for s in slots], *after)
    return out[0], out[1], list(out[2:2 + n]), out[-1]


def gather_wait(send_sems, recv_sems, slots, after, name):
    n = len(slots)

    def body(*refs):
        s_refs, ssem, rsem = refs[:n], refs[n], refs[n + 1]
        x, y, c = _place()
        me = 2 * x + y
        for k, (px, py) in enumerate([(1 - x, y), (x, 1 - y), (1 - x, 1 - y)]):
            for a in range(n):
                cp = pltpu.make_async_remote_copy(
                    src_ref=s_refs[a].at[me], dst_ref=s_refs[a].at[2 * px + py], send_sem=ssem.at[3 * a + k],
                    recv_sem=rsem.at[3 * a + k], device_id=(px, py, c), device_id_type=MESH)
                cp.wait_send()
                cp.wait_recv()

    return pl.pallas_call(
        body, name=name, out_shape=[pltpu.HBM(s.shape, s.dtype) for s in slots],
        in_specs=[_HBM] * n + [_SEM, _SEM, _ANY], out_specs=[_HBM] * n,
        input_output_aliases={a: a for a in range(n)},
        compiler_params=pltpu.CompilerParams(has_side_effects=_EFFECT),
    )(*slots, send_sems, recv_sems, after)


def rs_chipsum16(g, recv1, cidx, name):
    nl, hr, cols = recv1.shape[1:]

    def body(c_ref, g_ref, r_ref, o_ref):
        o_ref[...] = (g_ref[...] + r_ref[...]).astype(BF16)

    blk = (1, 1, hr, cols)
    return pl.pallas_call(
        body, name=name, out_shape=jax.ShapeDtypeStruct(recv1.shape, BF16),
        grid_spec=pltpu.PrefetchScalarGridSpec(
            num_scalar_prefetch=1, grid=(4, nl),
            in_specs=[pl.BlockSpec(blk, lambda j, l, c: (j, l, c[0], 0)), pl.BlockSpec(blk, lambda j, l, c: (j, l, 0, 0))],
            out_specs=pl.BlockSpec(blk, lambda j, l, c: (j, l, 0, 0))),
        compiler_params=_cp("parallel", "parallel"),
    )(cidx, g, recv1)


def sibling_start(gs, name):
    n = len(gs)
    lands = [lax.empty((*g.shape[:2], g.shape[2] // 2, g.shape[3]), F32) for g in gs]

    def body(*refs):
        g_refs, l_refs, send_sems, recv_sems, token = refs[:n], refs[n:2 * n], refs[2 * n], refs[2 * n + 1], refs[-1]
        x, y, c = _place()
        for a in range(n):
            pltpu.make_async_remote_copy(
                src_ref=_row_half(g_refs[a], (slice(None), slice(None)), 1 - c), dst_ref=l_refs[a],
                send_sem=send_sems.at[a], recv_sem=recv_sems.at[a], device_id=(x, y, 1 - c), device_id_type=MESH).start()
        token[...] = jnp.zeros_like(token)

    bufs = list(gs) + lands
    out = pl.pallas_call(
        body, name=name,
        out_shape=(pltpu.SemaphoreType.DMA((n,)), pltpu.SemaphoreType.DMA((n,)),
                   *[pltpu.HBM(b.shape, b.dtype) for b in bufs], jax.ShapeDtypeStruct((8, 128), F32)),
        in_specs=[_HBM] * (2 * n), out_specs=(_SEM, _SEM, *[_HBM] * (2 * n), pl.BlockSpec(memory_space=pltpu.VMEM)),
        input_output_aliases={i: 2 + i for i in range(2 * n)},
        compiler_params=pltpu.CompilerParams(has_side_effects=_EFFECT),
    )(*[pltpu.with_memory_space_constraint(b, pltpu.HBM) for b in bufs])
    return out[0], out[1], list(out[2:2 + n]), list(out[2 + n:2 + 2 * n]), out[-1]


def sibling_wait(send_sems, recv_sems, gs, lands, after, name):
    n = len(gs)

    def body(*refs):
        g_refs, l_refs, ssem, rsem = refs[:n], refs[n:2 * n], refs[2 * n], refs[2 * n + 1]
        x, y, c = _place()
        for a in range(n):
            cp = pltpu.make_async_remote_copy(
                src_ref=_row_half(g_refs[a], (slice(None), slice(None)), 1 - c), dst_ref=l_refs[a],
                send_sem=ssem.at[a], recv_sem=rsem.at[a], device_id=(x, y, 1 - c), device_id_type=MESH)
            cp.wait_send()
            cp.wait_recv()

    bufs = list(gs) + list(lands)
    out = pl.pallas_call(
        body, name=name, out_shape=[pltpu.HBM(b.shape, b.dtype) for b in bufs],
        in_specs=[_HBM] * (2 * n) + [_SEM, _SEM, _ANY], out_specs=[_HBM] * (2 * n),
        input_output_aliases={i: i for i in range(2 * n)},
        compiler_params=pltpu.CompilerParams(has_side_effects=_EFFECT),
    )(*bufs, send_sems, recv_sems, after)
    return list(out[:n]), list(out[n:])


def exchange_start(cs, name):
    n = len(cs)
    lands = [lax.empty((3, *c.shape[1:]), BF16) for c in cs]

    def body(*refs):
        s_refs, l_refs, send_sems, recv_sems, token = refs[:n], refs[n:2 * n], refs[2 * n], refs[2 * n + 1], refs[-1]
        x, y, c = _place()
        for k, (px, py) in enumerate([(1 - x, y), (x, 1 - y), (1 - x, 1 - y)]):
            for a in range(n):
                pltpu.make_async_remote_copy(
                    src_ref=s_refs[a].at[2 * px + py], dst_ref=l_refs[a].at[k], send_sem=send_sems.at[3 * a + k],
                    recv_sem=recv_sems.at[3 * a + k], device_id=(px, py, c), device_id_type=MESH).start()
        token[...] = jnp.zeros_like(token)

    bufs = list(cs) + lands
    out = pl.pallas_call(
        body, name=name,
        out_shape=(pltpu.SemaphoreType.DMA((3 * n,)), pltpu.SemaphoreType.DMA((3 * n,)),
                   *[pltpu.HBM(b.shape, b.dtype) for b in bufs], jax.ShapeDtypeStruct((8, 128), F32)),
        in_specs=[_HBM] * (2 * n), out_specs=(_SEM, _SEM, *[_HBM] * (2 * n), pl.BlockSpec(memory_space=pltpu.VMEM)),
        input_output_aliases={i: 2 + i for i in range(2 * n)},
        compiler_params=pltpu.CompilerParams(has_side_effects=_EFFECT),
    )(*[pltpu.with_memory_space_constraint(b, pltpu.HBM) for b in bufs])
    return out[0], out[1], list(out[2:2 + n]), list(out[2 + n:2 + 2 * n]), out[-1]


def exchange_wait(send_sems, recv_sems, cs, lands, after, name):
    n = len(cs)

    def body(*refs):
        s_refs, l_refs, ssem, rsem = refs[:n], refs[n:2 * n], refs[2 * n], refs[2 * n + 1]
        x, y, c = _place()
        for k, (px, py) in enumerate([(1 - x, y), (x, 1 - y), (1 - x, 1 - y)]):
            for a in range(n):
                cp = pltpu.make_async_remote_copy(
                    src_ref=s_refs[a].at[2 * px + py], dst_ref=l_refs[a].at[k], send_sem=ssem.at[3 * a + k],
                    recv_sem=rsem.at[3 * a + k], device_id=(px, py, c), device_id_type=MESH)
                cp.wait_send()
                cp.wait_recv()

    bufs = list(cs) + list(lands)
    out = pl.pallas_call(
        body, name=name, out_shape=[pltpu.HBM(b.shape, b.dtype) for b in bufs],
        in_specs=[_HBM] * (2 * n) + [_SEM, _SEM, _ANY], out_specs=[_HBM] * (2 * n),
        input_output_aliases={i: i for i in range(2 * n)},
        compiler_params=pltpu.CompilerParams(has_side_effects=_EFFECT),
    )(*bufs, send_sems, recv_sems, after)
    return list(out[n:])


def rs_final_sum(g, recv1, recv2, idx, name):
    nl, hr, cols = recv1.shape[1:]

    def body(i_ref, g_ref, r1_ref, r2_ref, o_ref):
        acc = g_ref[0, 0] + r1_ref[0, 0]
        for k in range(3):
            acc = acc + r2_ref[k, 0].astype(F32)
        o_ref[0] = acc

    blk = (1, 1, hr, cols)
    return pl.pallas_call(
        body, name=name, out_shape=jax.ShapeDtypeStruct((nl, 2 * hr, cols), F32),
        grid_spec=pltpu.PrefetchScalarGridSpec(
            num_scalar_prefetch=1, grid=(nl,),
            in_specs=[pl.BlockSpec(blk, lambda l, ix: (ix[0], l, ix[1], 0)), pl.BlockSpec(blk, lambda l, ix: (ix[0], l, 0, 0)),
                      pl.BlockSpec((3, 1, hr, cols), lambda l, ix: (0, l, 0, 0))],
            out_specs=pl.BlockSpec((1, hr, cols), lambda l, ix: (l, ix[1], 0))),
        compiler_params=_cp("parallel"),
    )(idx, g, recv1, recv2)


def rs_share_halves(fulls):
    n = len(fulls)

    def body(*refs):
        h_refs, o_refs, (send_sems, recv_sems) = refs[:n], refs[n:2 * n], refs[2 * n:]
        x, y, c = _place()
        sibling = (x, y, 1 - c)

        def copy(a, h):
            return pltpu.make_async_remote_copy(
                src_ref=_row_half(h_refs[a], (slice(None),), h), dst_ref=_row_half(o_refs[a], (slice(None),), h),
                send_sem=send_sems.at[a], recv_sem=recv_sems.at[a], device_id=sibling, device_id_type=MESH)

        for a in range(n):
            copy(a, c).start()
        for a in range(n):
            copy(a, c).wait_send()
            copy(a, 1 - c).wait_recv()

    return pl.pallas_call(
        body, name="rs_share_halves", out_shape=[jax.ShapeDtypeStruct(f.shape, F32) for f in fulls],
        in_specs=[_ANY] * n, out_specs=[_ANY] * n, input_output_aliases={a: a for a in range(n)},
        scratch_shapes=[pltpu.SemaphoreType.DMA((n,)), pltpu.SemaphoreType.DMA((n,))],
    )(*fulls)


def _own_slot(shard, chip, dtype):
    return lax.dynamic_update_slice(lax.empty((4, *shard.shape), dtype), shard.astype(dtype)[None],
                                    (chip,) + (0,) * shard.ndim)


def kernel(x, norm_mix_pre, norm_mix_post, norm_ffn_pre, norm_ffn_post, w_in, conv_a, gate_up_fwd, gate_bias_fwd, gate_up_bwd, gate_bias_bwd, gla_head_norm, w_out, w_up, conv_ffn, w_down, loss_target, m_norm_mix_pre, m_norm_mix_post, m_norm_ffn_pre, m_norm_ffn_post, m_w_in, m_conv_a, m_gate_up_fwd, m_gate_bias_fwd, m_gate_up_bwd, m_gate_bias_bwd, m_gla_head_norm, m_w_out, m_w_up, m_conv_ffn, m_w_down, v_norm_mix_pre, v_norm_mix_post, v_norm_ffn_pre, v_norm_ffn_post, v_w_in, v_conv_a, v_gate_up_fwd, v_gate_bias_fwd, v_gate_up_bwd, v_gate_bias_bwd, v_gla_head_norm, v_w_out, v_w_up, v_conv_ffn, v_w_down):
    L = x.shape[1]
    xi, yi, ci = _place()
    chip = 2 * xi + yi
    tl_gla, tl_mix, tl_ffn = min(L, TL_GLA), min(L, TL_MIX), min(L, TL_FFN)

    big_w = (w_in, w_out, w_up, w_down)
    a_in0 = allgather_weights([_own_slot(w_in[0:1], chip, BF16)])[0][:, 0]
    started = []
    prev = (a_in0,)
    for l in range(DEPTH):
        ws = big_w[1:] if l == 0 else big_w
        started.append(gather_start([_own_slot(w[l], chip, BF16) for w in ws], f"gather_start_{l}", after=prev))
        prev = (started[-1][3],)
    tokens = [s[3] for s in started]

    def full_w_in(a_in):
        return jnp.pad(jnp.concatenate([a_in[j] for j in range(4)], axis=1), ((0, 0), (0, DINP - DIN)))

    small = jnp.concatenate([conv_a.reshape(-1), gate_up_fwd.reshape(-1), gate_up_bwd.reshape(-1), conv_ffn.reshape(-1)])
    ms = small.shape[0] // 128
    sg = allgather8(small.reshape(ms, 128), "allgather_small_weights").reshape(4, 2, ms * 128)[:, 0]

    def small_full(off, shape):
        n = shape[0] * shape[1] * shape[2]
        return jnp.concatenate([sg[j, off:off + n].reshape(shape) for j in range(4)], axis=2)

    o1 = DEPTH * 3 * 128
    o2 = o1 + DEPTH * RK * 64
    o3 = o2 + DEPTH * RK * 64
    conv_a_f = small_full(0, (DEPTH, 3, 128))
    gup_f = small_full(o1, (DEPTH, RK, 64))
    gup_b = small_full(o2, (DEPTH, RK, 64))
    conv_ffn_f = small_full(o3, (DEPTH, 3, 1408))

    def gcat_of(l):
        g = jnp.zeros((LRW, 2 * DK), F32)
        g = g.at[0:RK, 0:DK].set(gup_f[l]).at[RK:2 * RK, DK:2 * DK].set(gup_b[l])
        return g.astype(BF16)

    gcats = [gcat_of(l) for l in range(DEPTH)]
    gbiases = [jnp.concatenate([gate_bias_fwd[l], gate_bias_bwd[l]])[None, :] for l in range(DEPTH)]
    ghn4s = [jnp.tile(gla_head_norm[l], NH)[None, :] for l in range(DEPTH)]

    xc = x.reshape(L, D)
    saved = []
    W_in, W_out, W_up, W_down = [], [], [], []
    tl_row = min(L, TL_ROW)
    for l in range(DEPTH):
        ssem, rsem, bufs, _ = started[l]
        if l > 0:
            a_in, a_out, a_up, a_down = gather_wait(ssem, rsem, bufs, xc, f"gather_wait_{l}")
        W_in.append(full_w_in(a_in0 if l == 0 else a_in))
        P, h1 = rms_matmul(xc, norm_mix_pre[l][None, :], W_in[l], DINP, "proj_in", out_dtype=BF16, tm=TM_PROJ // 2,
                           after=tokens if l == 0 else ())
        o_f, o_b, sf, sb = gla_fwd(P, gcats[l], gbiases[l], tl_gla)
        if l == 0:
            a_out, a_up, a_down = gather_wait(ssem, rsem, bufs, o_f, "gather_wait_0")
        W_out.append(a_out.reshape(D, D))
        W_up.append(a_up)
        W_down.append(a_down.reshape(DFF, D))
        ycat, y, x1 = mix_out(P, o_f, o_b, conv_a_f[l], ghn4s[l], W_out[l], norm_mix_post[l][None, :], xc,
                              min(L, TL_MIX_OUT))
        U, h2, ug, uv, z, y2, x2 = ffn_fwd(x1, norm_ffn_pre[l][None, :], W_up[l], conv_ffn_f[l], W_down[l],
                                           norm_ffn_post[l][None, :], tl_ffn)
        saved.append(dict(x=xc, h1=h1, P=P, o_f=o_f, o_b=o_b, sf=sf, sb=sb, ycat=ycat, y=y, x1=x1, h2=h2, U=U, y2=y2,
                          ug=ug, uv=uv, z=z))
        xc = x2

    dx, loss_blk = loss_head(xc, loss_target.reshape(L, D), tl_row)

    big = ("w_in", "w_out", "w_up", "w_down")
    cidx = jnp.reshape(ci, (1,)).astype(jnp.int32)
    idx = jnp.stack([chip, ci]).astype(jnp.int32)
    grads = [None] * DEPTH
    reduced = [dict() for _ in range(DEPTH)]
    tl_dw = min(L, 1024)
    groups = dict(ffn=("w_up", "w_down"), mix=("w_in", "w_out"))
    state = {grp: dict(flight=None, sibling=None) for grp in groups}
    token = ()

    def finish(grp, after):
        lp, gs_p, recv1_p, (ssem, rsem, cs_thru, lands, _) = state[grp]["flight"]
        recv2 = exchange_wait(ssem, rsem, cs_thru, lands, after, f"exchange_wait_{grp}_{lp}")
        halves = [rs_final_sum(g, r1, r2, idx, "rs_final_sum_" + k)
                  for g, r1, r2, k in zip(gs_p, recv1_p, recv2, groups[grp])]
        reduced[lp].update(zip(groups[grp], rs_share_halves(halves)))

    def advance(grp, after):
        st = state[grp]
        ls, (ssem, rsem, gs_thru, lands, _) = st["sibling"]
        gs_s, recv1 = sibling_wait(ssem, rsem, gs_thru, lands, after, f"sibling_wait_{grp}_{ls}")
        cs16 = [rs_chipsum16(g, r, cidx, "rs_chipsum16_" + k) for g, r, k in zip(gs_s, recv1, groups[grp])]
        flight = exchange_start(cs16, f"exchange_start_{grp}_{ls}")
        if st["flight"] is not None:
            finish(grp, flight[4])
        st["flight"] = (ls, gs_s, recv1, flight)
        st["sibling"] = None
        return flight[4]

    for l in reversed(range(DEPTH)):
        s = saved[l]
        dy2, dg4 = rms_bwd_pre(dx, s["y2"], norm_ffn_post[l][None, :], tl_row, after=token)
        g_down = matmul_tn(s["z"], dy2, DFF // 2, D, tl_dw, "dw_down").reshape(4, 1, DFF // 4, D)
        token2 = (advance("mix", g_down),) if state["mix"]["sibling"] is not None else ()
        dU_g, dU_v, dx1, dg3, dcf_g, dcf_v = ffn_bwd(dy2, s["ug"], s["uv"], s["U"], conv_ffn_f[l], W_down[l], W_up[l],
                                                     s["x1"], dx, norm_ffn_pre[l][None, :], tl_ffn)
        g_up = matmul_tn(s["h2"], dU_g, D, WFF, tl_dw, "dw_up_gate",
                         into=(lax.empty((4, 1, D, WFF), F32), (None, None, D, WFF), lambda p, q: (q, 0, 0, 0)))
        g_up = matmul_tn(s["h2"], dU_v, D, WFF, tl_dw, "dw_up_val",
                         into=(g_up, (None, None, D, WFF), lambda p, q: (NFF + q, 0, 0, 0)))
        sib = sibling_start([g_up, g_down], f"sibling_start_ffn_{l}")
        state["ffn"]["sibling"] = (l, sib)
        dy, dg2 = rms_bwd_pre(dx1, s["y"], norm_mix_post[l][None, :], tl_row, after=token2 + (sib[4],))
        dgb, dcc, dgo, do, dca, dghn = mix_bwd1(dy, W_out[l], s["P"], s["o_f"], s["o_b"], conv_a_f[l], ghn4s[l],
                                                min(L, TL_MIX_OUT))
        token3 = advance("ffn", dgb)
        g_out = matmul_tn(s["ycat"], dy, D, D, tl_dw, "dw_out", after=(token3,)).reshape(4, 1, D // 4, D)
        gl = gla_bwd(s["P"], do, s["sf"], s["sb"], gcats[l], gbiases[l], tl_gla)
        dP, dx, dg1, dgcat, dbias = mix_bwd2(dgb, dcc, dgo, gl, s["P"], conv_a_f[l], gcats[l], gbiases[l], W_in[l],
                                             s["x"], dx1, norm_mix_pre[l][None, :], tl_mix)
        dW_in = matmul_tn(s["h1"], dP, D // 2, DINP, tl_dw, "dw_in")
        g_in = jnp.stack([dW_in[:, (DIN // 4) * j:(DIN // 4) * (j + 1)] for j in range(4)])[:, None]
        grads[l] = dict(
            norm_mix_pre=dg1[0], norm_mix_post=dg2[0], norm_ffn_pre=dg3[0], norm_ffn_post=dg4[0],
            conv_a=dca[0:3], gate_up_fwd=dgcat[0:RK, 0:DK], gate_bias_fwd=dbias[0, 0:DK],
            gate_up_bwd=dgcat[RK:2 * RK, DK:2 * DK], gate_bias_bwd=dbias[0, DK:2 * DK], gla_head_norm=dghn[0],
            conv_ffn=jnp.concatenate([dcf_g[0:3], dcf_v[0:3]], axis=1))
        sib = sibling_start([g_in, g_out], f"sibling_start_mix_{l}")
        state["mix"]["sibling"] = (l, sib)
        token = (sib[4],)
    last = advance("mix", token[0])
    finish("ffn", last)
    finish("mix", last)

    G = {k: jnp.stack([grads[l][k] for l in range(DEPTH)]) for k in grads[0]}

    small_names = ["norm_mix_pre", "norm_mix_post", "norm_ffn_pre", "norm_ffn_post", "conv_a", "gate_up_fwd",
                   "gate_bias_fwd", "gate_up_bwd", "gate_bias_bwd", "gla_head_norm", "conv_ffn"]
    flat = jnp.concatenate([G[k].reshape(-1) for k in small_names] + [loss_blk[0, 0:1]])
    n_small = flat.shape[0]
    mp = -(-n_small // 1024) * 8
    flat = jnp.pad(flat, (0, mp * 128 - n_small)).reshape(mp, 128)
    tot = sum8(allgather8(flat, "allgather_small_grads"), mp).reshape(-1)
    gsm = {}
    o = 0
    for k in small_names:
        n = G[k].size
        gsm[k] = tot[o:o + n].reshape(G[k].shape)
        o += n
    loss = tot[o]

    def my_cols(a, width):
        return lax.dynamic_slice_in_dim(a, chip * width, width, axis=2)

    gsm["conv_a"] = my_cols(gsm["conv_a"], 128)
    gsm["gate_up_fwd"] = my_cols(gsm["gate_up_fwd"], 64)
    gsm["gate_up_bwd"] = my_cols(gsm["gate_up_bwd"], 64)
    gsm["conv_ffn"] = my_cols(gsm["conv_ffn"], 1408)

    for k in big:
        gsm[k] = jnp.concatenate([reduced[l][k] for l in range(DEPTH)], axis=0)

    names = ["norm_mix_pre", "norm_mix_post", "norm_ffn_pre", "norm_ffn_post", "w_in", "conv_a", "gate_up_fwd",
             "gate_bias_fwd", "gate_up_bwd", "gate_bias_bwd", "gla_head_norm", "w_out", "w_up", "conv_ffn", "w_down"]
    w = dict(norm_mix_pre=norm_mix_pre, norm_mix_post=norm_mix_post, norm_ffn_pre=norm_ffn_pre, norm_ffn_post=norm_ffn_post,
             w_in=w_in, conv_a=conv_a, gate_up_fwd=gate_up_fwd, gate_bias_fwd=gate_bias_fwd, gate_up_bwd=gate_up_bwd,
             gate_bias_bwd=gate_bias_bwd, gla_head_norm=gla_head_norm, w_out=w_out, w_up=w_up, conv_ffn=conv_ffn, w_down=w_down)
    m = dict(norm_mix_pre=m_norm_mix_pre, norm_mix_post=m_norm_mix_post, norm_ffn_pre=m_norm_ffn_pre, norm_ffn_post=m_norm_ffn_post,
             w_in=m_w_in, conv_a=m_conv_a, gate_up_fwd=m_gate_up_fwd, gate_bias_fwd=m_gate_bias_fwd, gate_up_bwd=m_gate_up_bwd,
             gate_bias_bwd=m_gate_bias_bwd, gla_head_norm=m_gla_head_norm, w_out=m_w_out, w_up=m_w_up, conv_ffn=m_conv_ffn, w_down=m_w_down)
    v = dict(norm_mix_pre=v_norm_mix_pre, norm_mix_post=v_norm_mix_post, norm_ffn_pre=v_norm_ffn_pre, norm_ffn_post=v_norm_ffn_post,
             w_in=v_w_in, conv_a=v_conv_a, gate_up_fwd=v_gate_up_fwd, gate_bias_fwd=v_gate_bias_fwd, gate_up_bwd=v_gate_up_bwd,
             gate_bias_bwd=v_gate_bias_bwd, gla_head_norm=v_gla_head_norm, w_out=v_w_out, w_up=v_w_up, conv_ffn=v_conv_ffn, w_down=v_w_down)
    upd = {k: adamw(w[k], gsm[k], m[k], v[k], "adamw_" + k) for k in names}
    return (loss, dx.reshape(1, L, D), *[gsm[k] for k in names], *[upd[k][0] for k in names],
            *[upd[k][1] for k in names], *[upd[k][2] for k in names])
```

```python
import functools

import jax
import jax.numpy as jnp
from jax import lax
from jax.experimental import pallas as pl
from jax.experimental.pallas import tpu as pltpu

F32 = jnp.float32
BF16 = jnp.bfloat16
MXU_DTYPE = jnp.bfloat16
MESH = pl.DeviceIdType.MESH

D = 1024
DC = 512
DG = 512
NH = 4
HV = 128
HK = 64
DK = 256
RK = 16
CH = 64
DFF = 2816
DIN = 3104
DINP = 3200
LRW = 128
DEPTH = 4
EPS = 1e-6
QSCALE = HK ** -0.5
GATE_NORM = 1.0 / 16.0
CB_GB, CB_GC, CB_GV, CB_GO = 0, 1, 2, 5
CB_Q, CB_K = 6, 7
CB_V = 4
CB_LR = 24
LR = 0.001
B1 = 0.9
B2 = 0.999
AEPS = 1e-08
WD = 0.01
STEP = 10
TM_PROJ = 1024
TL_GLA = 512
TL_MIX = 256
TL_MIX_OUT = 512
TL_ROW = 1024
TL_FFN = 256
VMEM_LIMIT = 56 * 1024 * 1024


def _cp(*sem):
    return pltpu.CompilerParams(dimension_semantics=sem if sem else None, vmem_limit_bytes=VMEM_LIMIT)


def _mm(a, b):
    return jnp.dot(a.astype(MXU_DTYPE), b.astype(MXU_DTYPE), preferred_element_type=F32)


def _mm_nt(a, b):
    return lax.dot_general(a.astype(MXU_DTYPE), b.astype(MXU_DTYPE), (((1,), (1,)), ((), ())),
                           preferred_element_type=F32)


def _mm_tn(a, b):
    return lax.dot_general(a.astype(MXU_DTYPE), b.astype(MXU_DTYPE), (((0,), (0,)), ((), ())),
                           preferred_element_type=F32)


def _mm_tri(tri, b):
    t = tri.astype(BF16)
    b1 = b.astype(BF16)
    r1 = b - b1.astype(F32)
    b2 = r1.astype(BF16)
    b3 = (r1 - b2.astype(F32)).astype(BF16)
    dot = lambda u: jnp.dot(t, u, preferred_element_type=F32)
    return dot(b1) + dot(b2) + dot(b3)


def _rms(x, g):
    r = lax.rsqrt(jnp.mean(x * x, axis=-1, keepdims=True) + EPS)
    return x * r * g


def _rms_bwd(dout, y, g):
    r = lax.rsqrt(jnp.mean(y * y, axis=-1, keepdims=True) + EPS)
    yh = y * r
    dyh = dout * g
    dy = r * (dyh - yh * jnp.mean(dyh * yh, axis=-1, keepdims=True))
    dg = jnp.sum(dout * yh, axis=0, keepdims=True)
    return dy, dg


def _sigmoid(x):
    return 0.5 * jnp.tanh(0.5 * x) + 0.5


def _logsig(x):
    return jnp.minimum(x, 0.0) - jnp.log1p(jnp.exp(-jnp.abs(x)))


def _shifts(x, p8, n8):
    n = x.shape[0]
    xe = jnp.concatenate([p8, x, n8], axis=0)
    return pltpu.roll(xe, 1, 0)[8:8 + n], pltpu.roll(xe, n + 15, 0)[8:8 + n]


def _halo_rows(prev_ref, next_ref, i, last):
    hr = prev_ref.shape[0]
    p = jnp.where(i == 0, 0.0, prev_ref[...].astype(F32)[hr - 8:hr, :])
    n = jnp.where(i == last, 0.0, next_ref[...].astype(F32)[0:8, :])
    return p, n


def _conv3(x, xp, xn, w_ref):
    xm1, xp1 = _shifts(x, xp, xn)
    return w_ref[0:1, :] * xm1 + w_ref[1:2, :] * x + w_ref[2:3, :] * xp1, xm1, xp1


def _conv3_t(d, dp, dn, w_ref):
    dm1, dp1 = _shifts(d, dp, dn)
    return w_ref[0:1, :] * dp1 + w_ref[1:2, :] * d + w_ref[2:3, :] * dm1


HALO32 = 8
HALO16 = 16


def _prev_row_blk(i, tl, hr):
    return jnp.maximum(i * (tl // hr) - 1, 0)


def _next_row_blk(i, tl, nrows, hr):
    return jnp.minimum((i + 1) * (tl // hr), nrows // hr - 1)


def _prev_blk(tl, cb, hr=HALO32):
    return lambda i: (_prev_row_blk(i, tl, hr), cb)


def _next_blk(tl, nrows, cb, hr=HALO32):
    return lambda i: (_next_row_blk(i, tl, nrows, hr), cb)


def rms_matmul(x, g, w, tn, name, w_spec=None, n_out=None, out_dtype=F32, after=(), tm=TM_PROJ):
    L = x.shape[0]
    N = w.shape[1] if n_out is None else n_out
    tm = min(L, tm)
    if w_spec is None:
        w_spec = pl.BlockSpec((D, tn), lambda i, j: (0, j))

    def body(x_ref, g_ref, w_ref, *rest):
        o_ref, h_ref = rest[-2:]

        @pl.when(pl.program_id(1) == 0)
        def _():
            h_ref[...] = _rms(x_ref[...], g_ref[...]).astype(BF16)

        o_ref[...] = _mm(h_ref[...], w_ref[...]).astype(out_dtype)

    return pl.pallas_call(
        body, name=name, grid=(L // tm, N // tn),
        in_specs=[pl.BlockSpec((tm, D), lambda i, j: (i, 0)), pl.BlockSpec((1, D), lambda i, j: (0, 0)), w_spec]
        + [_ANY] * len(after),
        out_specs=[pl.BlockSpec((tm, tn), lambda i, j: (i, j)), pl.BlockSpec((tm, D), lambda i, j: (i, 0))],
        out_shape=[jax.ShapeDtypeStruct((L, N), out_dtype), jax.ShapeDtypeStruct((L, D), BF16)],
        compiler_params=_cp("parallel", "arbitrary"),
    )(x, g, w, *after)


def _gla_masks():
    def blk(shape, rdiv, cdiv):
        r = lax.broadcasted_iota(jnp.int32, shape, 0) // rdiv
        c = lax.broadcasted_iota(jnp.int32, shape, 1) // cdiv
        return (r == c).astype(F32)

    r = lax.broadcasted_iota(jnp.int32, (CH, CH), 0)
    c = lax.broadcasted_iota(jnp.int32, (CH, CH), 1)
    r4 = lax.broadcasted_iota(jnp.int32, (NH * CH, CH), 0) % CH
    c4 = lax.broadcasted_iota(jnp.int32, (NH * CH, CH), 1)
    return dict(
        bdq=blk((NH * CH, DK), CH, HK),
        bdo=blk((NH * CH, DG), CH, HV),
        bds=blk((DG, DK), HV, HK),
        tril=(r >= c).astype(F32), triu=(r <= c).astype(F32),
        tril4=r4 >= c4, triu4=r4 <= c4,
    )


def _tile4(x):
    return jnp.concatenate([x, x, x, x], axis=0)


def _gla_tile_prep(q, k, a, m, rev, nc):
    tri = m["triu"] if rev else m["tril"]
    chunks = [a[c * CH:(c + 1) * CH] for c in range(nc)]
    cum = jnp.concatenate([_mm_tri(tri, ac) for ac in chunks], axis=0)
    tot = jnp.concatenate([jnp.sum(ac, axis=0, keepdims=True) for ac in chunks], axis=0)
    tot_rows = jnp.concatenate([jnp.broadcast_to(tot[c:c + 1], (CH, DK)) for c in range(nc)], axis=0)
    e = jnp.exp(cum)
    einv = jnp.exp(-cum)
    eout = jnp.exp(tot_rows - cum)
    q, k = q.astype(F32), k.astype(F32)
    return dict(e=e, einv=einv, eout=eout, dec=jnp.exp(tot), qt=q * QSCALE * e, kt=k * einv, kh=k * eout)


def _gla_scores(qt16, kt16, m, rev):
    qs = _tile4(qt16) * m["bdq"].astype(qt16.dtype)
    return qs, jnp.where(m["triu4"] if rev else m["tril4"], _mm_nt(qs, kt16), 0.0)


def _gla_chunk_fwd(qt16, kt16, kh16, v, dec, st_ref, m, rev):
    _, sc = _gla_scores(qt16, kt16, m, rev)
    v16 = v.astype(BF16)
    r = _mm(sc, v16)
    o_intra = jnp.concatenate([r[h * CH:(h + 1) * CH, h * HV:(h + 1) * HV] for h in range(NH)], axis=1)
    st = st_ref[...]
    st16 = st.astype(BF16)
    o = o_intra + _mm_nt(qt16, st16)
    st_ref[...] = st * dec + _mm_tn(v16, kh16) * m["bds"]
    return o, st16


def _gates(lr_ref, gc_ref, bs_ref, cols):
    return _logsig(_mm(lr_ref[...], gc_ref[:, cols]) + bs_ref[:, cols]) * GATE_NORM


def gla_fwd(P, gcat, gbias, tl):
    L = P.shape[0]
    nb = L // tl
    nc = tl // CH

    def body(qf, kf, vf, lf, qb, kb, vb, lb, gc_ref, bs_ref, of, ob, sf, sb, stf, stb,
             qtf, ktf, khf, dcf, qtb, ktb, khb, dcb):
        @pl.when(pl.program_id(0) == 0)
        def _():
            stf[...] = jnp.zeros_like(stf)
            stb[...] = jnp.zeros_like(stb)

        m = _gla_masks()
        for (q, k, lr, cols, rev, qt, kt, kh, dc) in ((qf, kf, lf, slice(0, DK), False, qtf, ktf, khf, dcf),
                                                      (qb, kb, lb, slice(DK, 2 * DK), True, qtb, ktb, khb, dcb)):
            p = _gla_tile_prep(q[...], k[...], _gates(lr, gc_ref, bs_ref, cols), m, rev, nc)
            qt[...] = p["qt"].astype(BF16)
            kt[...] = p["kt"].astype(BF16)
            kh[...] = p["kh"].astype(BF16)
            dc[...] = p["dec"]

        def chunk(c, carry):
            rows = pl.ds(pl.multiple_of(c * CH, CH), CH)
            o, st = _gla_chunk_fwd(qtf[rows, :], ktf[rows, :], khf[rows, :], vf[rows, :], dcf[pl.ds(c, 1), :], stf, m, False)
            of[rows, :] = o.astype(BF16)
            sf[c] = st
            cb = nc - 1 - c
            rows = pl.ds(pl.multiple_of(cb * CH, CH), CH)
            o, st = _gla_chunk_fwd(qtb[rows, :], ktb[rows, :], khb[rows, :], vb[rows, :], dcb[pl.ds(cb, 1), :], stb, m, True)
            ob[rows, :] = o.astype(BF16)
            sb[cb] = st
            return carry

        lax.fori_loop(0, nc, chunk, 0, unroll=2)

    fw = lambda cb: (lambda i: (i, cb))
    bw = lambda cb: (lambda i: (nb - 1 - i, cb))
    return pl.pallas_call(
        body, name="gla_fwd", grid=(nb,),
        in_specs=[pl.BlockSpec((tl, DK), fw(CB_Q)), pl.BlockSpec((tl, DK), fw(CB_K)), pl.BlockSpec((tl, DG), fw(CB_V)),
                  pl.BlockSpec((tl, LRW), fw(CB_LR)),
                  pl.BlockSpec((tl, DK), bw(CB_Q)), pl.BlockSpec((tl, DK), bw(CB_K)), pl.BlockSpec((tl, DG), bw(CB_V)),
                  pl.BlockSpec((tl, LRW), bw(CB_LR)),
                  pl.BlockSpec((LRW, 2 * DK), lambda i: (0, 0)), pl.BlockSpec((1, 2 * DK), lambda i: (0, 0))],
        out_specs=[pl.BlockSpec((tl, DG), lambda i: (i, 0)), pl.BlockSpec((tl, DG), lambda i: (nb - 1 - i, 0)),
                   pl.BlockSpec((nc, DG, DK), lambda i: (i, 0, 0)), pl.BlockSpec((nc, DG, DK), lambda i: (nb - 1 - i, 0, 0))],
        out_shape=[jax.ShapeDtypeStruct((L, DG), BF16), jax.ShapeDtypeStruct((L, DG), BF16),
                   jax.ShapeDtypeStruct((L // CH, DG, DK), BF16), jax.ShapeDtypeStruct((L // CH, DG, DK), BF16)],
        scratch_shapes=[pltpu.VMEM((DG, DK), F32), pltpu.VMEM((DG, DK), F32)]
        + [pltpu.VMEM((tl, DK), BF16)] * 3 + [pltpu.VMEM((nc, DK), F32)]
        + [pltpu.VMEM((tl, DK), BF16)] * 3 + [pltpu.VMEM((nc, DK), F32)],
        compiler_params=_cp("arbitrary"),
    )(P, P, P, P, P, P, P, P, gcat, gbias)


def _headnorm(o):
    oh, rs = [], []
    for h in range(NH):
        oo = o[:, h * HV:(h + 1) * HV]
        r = lax.rsqrt(jnp.mean(oo * oo, axis=-1, keepdims=True) + EPS)
        oh.append(oo * r)
        rs.append(r)
    return jnp.concatenate(oh, axis=1), rs


def mix_out(P, o_f, o_b, conv_a, ghn4, w_out, g2, x, tl):
    L = P.shape[0]
    nt = L // tl

    def body(gb, gc, gv, go, gcp, gvp, gcn, gvn, of, ob, ca, gh, wo, g2r, xr, ycat, yr, x1):
        i = pl.program_id(0)
        cp, cn = _halo_rows(gcp, gcn, i, nt - 1)
        vp, vn = _halo_rows(gvp, gvn, i, nt - 1)
        c = gc[...].astype(F32) * gv[...].astype(F32)
        cc, _, _ = _conv3(c, cp * vp, cn * vn, ca)
        ya = gb[...].astype(F32) * cc
        oh, _ = _headnorm(of[...].astype(F32) + ob[...].astype(F32))
        g = go[...].astype(F32)
        yb = g * _sigmoid(g) * (oh * gh[...])
        yc = jnp.concatenate([ya, yb], axis=1).astype(BF16)
        ycat[...] = yc
        y = _mm(yc, wo[...])
        yr[...] = y
        x1[...] = xr[...] + _rms(y, g2r[...])

    t = lambda cb: pl.BlockSpec((tl, DC), lambda i: (i, cb))
    hp = lambda cb: pl.BlockSpec((HALO16, DC), _prev_blk(tl, cb, HALO16))
    hn = lambda cb: pl.BlockSpec((HALO16, DC), _next_blk(tl, L, cb, HALO16))
    row = lambda n: pl.BlockSpec((tl, n), lambda i: (i, 0))
    full = lambda a: pl.BlockSpec(a.shape, lambda i: (0, 0))
    return pl.pallas_call(
        body, name="mix_out", grid=(nt,),
        in_specs=[t(CB_GB), t(CB_GC), t(CB_GV), t(CB_GO), hp(CB_GC), hp(CB_GV), hn(CB_GC), hn(CB_GV),
                  row(DG), row(DG), full(conv_a), full(ghn4), full(w_out), full(g2), row(D)],
        out_specs=[row(D), row(D), row(D)],
        out_shape=[jax.ShapeDtypeStruct((L, D), BF16), jax.ShapeDtypeStruct((L, D), F32),
                   jax.ShapeDtypeStruct((L, D), F32)],
        compiler_params=_cp("parallel"),
    )(P, P, P, P, P, P, P, P, o_f, o_b, conv_a, ghn4, w_out, g2, x)


NFF = 2
WFF = DFF // NFF
FFN_COL_CHUNKS = ((0, 512), (512, 1024), (1024, WFF))


def ffn_fwd(x1, g3, w_up, conv_ffn, w_down, g4, tl):
    L = x1.shape[0]
    nt = L // tl
    hh = HALO16

    def body(xr, xp, xn, g3r, wu, cf, wd, g4r, U, h2, ug, uv, zr, y2, x2):
        i = pl.program_id(0)
        he = _rms(jnp.concatenate([xp[...], xr[...], xn[...]], axis=0), g3r[...]).astype(BF16)
        h2[...] = he[hh:hh + tl]
        acc = jnp.zeros((tl, D), F32)
        for j in range(NFF):
            conv = []
            for blk, off in ((j, j * WFF), (NFF + j, DFF + j * WFF)):
                ue = _mm(he, wu[blk])
                p8 = jnp.where(i == 0, 0.0, ue[hh - 8:hh])
                n8 = jnp.where(i == nt - 1, 0.0, ue[hh + tl:hh + tl + 8])
                mid = ue[hh:hh + tl]
                U[:, off:off + WFF] = mid.astype(BF16)
                conv.append(_conv3(mid, p8, n8, cf.at[:, off:off + WFF])[0])
            gs = slice(j * WFF, (j + 1) * WFF)
            zz = (conv[0] * _sigmoid(conv[0]) * conv[1]).astype(BF16)
            ug[:, gs] = conv[0].astype(BF16)
            uv[:, gs] = conv[1].astype(BF16)
            zr[:, gs] = zz
            acc = acc + _mm(zz, wd[gs, :])
        y2[...] = acc
        x2[...] = xr[...] + _rms(acc, g4r[...])

    row = lambda n: pl.BlockSpec((tl, n), lambda i: (i, 0))
    full = lambda a: pl.BlockSpec(a.shape, lambda i: (0,) * a.ndim)
    once = lambda a: pl.BlockSpec(a.shape, lambda i: (0,) * a.ndim, pipeline_mode=pl.Buffered(1))
    half = jax.ShapeDtypeStruct((L, DFF), BF16)
    return pl.pallas_call(
        body, name="ffn_fwd", grid=(nt,),
        in_specs=[row(D), pl.BlockSpec((hh, D), lambda i: (_prev_row_blk(i, tl, hh), 0)),
                  pl.BlockSpec((hh, D), lambda i: (_next_row_blk(i, tl, L, hh), 0)),
                  full(g3), once(w_up), full(conv_ffn), once(w_down), full(g4)],
        out_specs=[row(2 * DFF), row(D), row(DFF), row(DFF), row(DFF), row(D), row(D)],
        out_shape=[jax.ShapeDtypeStruct((L, 2 * DFF), BF16), jax.ShapeDtypeStruct((L, D), BF16), half, half, half,
                   jax.ShapeDtypeStruct((L, D), F32), jax.ShapeDtypeStruct((L, D), F32)],
        compiler_params=_cp("parallel"),
    )(x1, x1, x1, g3, w_up, conv_ffn, w_down, g4)


def loss_head(y, target, tl):
    L = y.shape[0]

    def body(yr, tr, dy, ls):
        @pl.when(pl.program_id(0) == 0)
        def _():
            ls[...] = jnp.zeros_like(ls)

        err = yr[...] - tr[...]
        dy[...] = err * (1.0 / D)
        ls[...] += (0.5 / D) * jnp.sum(err * err)

    row = pl.BlockSpec((tl, D), lambda i: (i, 0))
    return pl.pallas_call(
        body, name="loss_head", grid=(L // tl,), in_specs=[row, row],
        out_specs=[row, pl.BlockSpec((8, 128), lambda i: (0, 0))],
        out_shape=[jax.ShapeDtypeStruct((L, D), F32), jax.ShapeDtypeStruct((8, 128), F32)],
        compiler_params=_cp("arbitrary"),
    )(y, target)


def rms_bwd_pre(dout, y, g, tl, after=()):
    L = y.shape[0]

    def body(dr, yr, gr, *rest):
        dy, dg = rest[-2:]

        @pl.when(pl.program_id(0) == 0)
        def _():
            dg[...] = jnp.zeros_like(dg)

        a, b = _rms_bwd(dr[...], yr[...], gr[...])
        dy[...] = a.astype(BF16)
        dg[...] += b

    row = pl.BlockSpec((tl, D), lambda i: (i, 0))
    vec = pl.BlockSpec((1, D), lambda i: (0, 0))
    return pl.pallas_call(
        body, name="rms_bwd_pre", grid=(L // tl,), in_specs=[row, row, vec] + [_ANY] * len(after), out_specs=[row, vec],
        out_shape=[jax.ShapeDtypeStruct((L, D), BF16), jax.ShapeDtypeStruct((1, D), F32)],
        compiler_params=_cp("arbitrary"),
    )(dout, y, g, *after)


def ffn_bwd(dy2, ug, uv, U, conv_ffn, w_down, w_up, x1, dres, g3, tl):
    L = x1.shape[0]
    nt = L // tl
    hh = HALO16

    def body(dyr, dyp, dyn, ugr, ugp, ugn, uvr, uvp, uvn, Ur, cf, wd, wu, x1r, drr, g3r, dUg, dUv, dx1, dg3, dcg, dcv):
        i = pl.program_id(0)

        @pl.when(i == 0)
        def _():
            dg3[...] = jnp.zeros_like(dg3)
            dcg[...] = jnp.zeros_like(dcg)
            dcv[...] = jnp.zeros_like(dcv)

        ext = lambda p, t, n, cs: jnp.concatenate([p[:, cs], t[:, cs], n[:, cs]], axis=0).astype(F32)
        dye = jnp.concatenate([dyp[...], dyr[...], dyn[...]], axis=0)
        acc = jnp.zeros((tl, D), F32)
        for j in range(NFF):
            dze = _mm_nt(dye, wd[j * WFF:(j + 1) * WFF, :])
            for c0, c1 in FFN_COL_CHUNKS:
                cs = slice(j * WFF + c0, j * WFF + c1)
                a = ext(ugp, ugr, ugn, cs)
                b = ext(uvp, uvr, uvn, cs)
                sg = _sigmoid(a)
                silu = a * sg
                dz = dze[:, c0:c1]
                for de, off, blk, dc, dU in ((dz * b * (sg + silu * (1.0 - sg)), 0, j, dcg, dUg),
                                            (dz * silu, DFF, NFF + j, dcv, dUv)):
                    d = de[hh:hh + tl]
                    p8 = jnp.where(i == 0, 0.0, de[hh - 8:hh])
                    n8 = jnp.where(i == nt - 1, 0.0, de[hh + tl:hh + tl + 8])
                    dm1, dp1 = _shifts(d, p8, n8)
                    wc = slice(off + j * WFF + c0, off + j * WFF + c1)
                    du = (cf[0:1, wc] * dp1 + cf[1:2, wc] * d + cf[2:3, wc] * dm1).astype(BF16)
                    dU[:, cs] = du
                    u = Ur[:, wc].astype(F32)
                    for k, t in enumerate((dp1, d, dm1)):
                        dc[k:k + 1, cs] += jnp.sum(t * u, axis=0, keepdims=True)
                    acc = acc + _mm_nt(du, wu[blk, :, c0:c1])
        dx, dg = _rms_bwd(acc, x1r[...], g3r[...])
        dx1[...] = drr[...] + dx
        dg3[...] += dg

    row = lambda n: pl.BlockSpec((tl, n), lambda i: (i, 0))
    prev = lambda n: pl.BlockSpec((hh, n), lambda i: (_prev_row_blk(i, tl, hh), 0))
    nxt = lambda n: pl.BlockSpec((hh, n), lambda i: (_next_row_blk(i, tl, L, hh), 0))
    full = lambda a: pl.BlockSpec(a.shape, lambda i: (0,) * a.ndim)
    once = lambda a: pl.BlockSpec(a.shape, lambda i: (0,) * a.ndim, pipeline_mode=pl.Buffered(1))
    half = jax.ShapeDtypeStruct((L, DFF), BF16)
    dcs = pl.BlockSpec((8, DFF), lambda i: (0, 0))
    return pl.pallas_call(
        body, name="ffn_bwd", grid=(nt,),
        in_specs=[row(D), prev(D), nxt(D), row(DFF), prev(DFF), nxt(DFF), row(DFF), prev(DFF), nxt(DFF), row(2 * DFF),
                  full(conv_ffn), once(w_down), once(w_up), row(D), row(D), full(g3)],
        out_specs=[row(DFF), row(DFF), row(D), pl.BlockSpec((1, D), lambda i: (0, 0)), dcs, dcs],
        out_shape=[half, half, jax.ShapeDtypeStruct((L, D), F32), jax.ShapeDtypeStruct((1, D), F32),
                   jax.ShapeDtypeStruct((8, DFF), F32), jax.ShapeDtypeStruct((8, DFF), F32)],
        compiler_params=_cp("arbitrary"),
    )(dy2, dy2, dy2, ug, ug, ug, uv, uv, uv, U, conv_ffn, w_down, w_up, x1, dres, g3)


def matmul_tn(a, b, ta, tn, tl, name, into=None, after=()):
    L, Ka = a.shape
    N = b.shape[1]

    def body(ar, br, *rest):
        o = rest[-1]

        @pl.when(pl.program_id(2) == 0)
        def _():
            o[...] = jnp.zeros_like(o)

        o[...] += _mm_tn(ar[...], br[...]).reshape(o.shape)

    in_specs = [pl.BlockSpec((tl, ta), lambda p, q, l: (l, p)), pl.BlockSpec((tl, tn), lambda p, q, l: (l, q))]
    if into is None:
        return pl.pallas_call(
            body, name=name, grid=(Ka // ta, N // tn, L // tl), in_specs=in_specs + [_ANY] * len(after),
            out_specs=pl.BlockSpec((ta, tn), lambda p, q, l: (p, q)),
            out_shape=jax.ShapeDtypeStruct((Ka, N), F32),
            compiler_params=_cp("parallel", "parallel", "arbitrary"),
        )(a, b, *after)
    buf, blk, idx = into
    return pl.pallas_call(
        body, name=name, grid=(Ka // ta, N // tn, L // tl), in_specs=in_specs + [_ANY],
        out_specs=pl.BlockSpec(blk, lambda p, q, l: idx(p, q)),
        out_shape=jax.ShapeDtypeStruct(buf.shape, F32), input_output_aliases={2: 0},
        compiler_params=_cp("parallel", "parallel", "arbitrary"),
    )(a, b, buf)


def mix_bwd1(dx1, y, g2, ycat, w_out, P, o_f, o_b, conv_a, ghn4, tl, after=()):
    L = P.shape[0]
    nt = L // tl
    na = len(after)

    def body(dxr, yr, g2r, ycr, wo, gb, gc, gv, go, gcp, gvp, gcn, gvn, of, ob, ca, gh, *rest):
        dgb, dcc, dgo, do, dca, dgh, dg2, dwo = rest[na:]
        i = pl.program_id(0)

        @pl.when(i == 0)
        def _():
            dca[...] = jnp.zeros_like(dca)
            dgh[...] = jnp.zeros_like(dgh)
            dg2[...] = jnp.zeros_like(dg2)
            dwo[...] = jnp.zeros_like(dwo)

        dyv, dg = _rms_bwd(dxr[...], yr[...], g2r[...])
        dy16 = dyv.astype(BF16)
        dg2[...] += dg
        dwo[...] += _mm_tn(ycr[...], dy16)
        dycat = _mm_nt(dy16, wo[...])
        dya = dycat[:, 0:DC]
        dyb = dycat[:, DC:D]
        cp, cn = _halo_rows(gcp, gcn, i, nt - 1)
        vp, vn = _halo_rows(gvp, gvn, i, nt - 1)
        c = gc[...].astype(F32) * gv[...].astype(F32)
        cc, c_m1, c_p1 = _conv3(c, cp * vp, cn * vn, ca)
        dgb[...] = (dya * cc).astype(BF16)
        d = dya * gb[...].astype(F32)
        dcc[...] = d.astype(BF16)
        for k, s in enumerate((c_m1, c, c_p1)):
            dca[k:k + 1, :] += jnp.sum(d * s, axis=0, keepdims=True)
        oh, rs = _headnorm(of[...].astype(F32) + ob[...].astype(F32))
        g = go[...].astype(F32)
        sg = _sigmoid(g)
        silu = g * sg
        dgo[...] = (dyb * (oh * gh[...]) * (sg * (1.0 + g * (1.0 - sg)))).astype(BF16)
        don = dyb * silu
        t = jnp.sum(don * oh, axis=0, keepdims=True)
        dgh[0:1, :] += t[:, 0:HV] + t[:, HV:2 * HV] + t[:, 2 * HV:3 * HV] + t[:, 3 * HV:4 * HV]
        doh = don * gh[...]
        parts = []
        for h in range(NH):
            hs = slice(h * HV, (h + 1) * HV)
            parts.append(rs[h] * (doh[:, hs] - oh[:, hs] * jnp.mean(doh[:, hs] * oh[:, hs], axis=-1, keepdims=True)))
        do[...] = jnp.concatenate(parts, axis=1).astype(BF16)

    t = lambda cb: pl.BlockSpec((tl, DC), lambda i: (i, cb))
    hp = lambda cb: pl.BlockSpec((HALO16, DC), _prev_blk(tl, cb, HALO16))
    hn = lambda cb: pl.BlockSpec((HALO16, DC), _next_blk(tl, L, cb, HALO16))
    row = lambda n: pl.BlockSpec((tl, n), lambda i: (i, 0))
    full = lambda a: pl.BlockSpec(a.shape, lambda i: (0, 0))
    act16 = lambda n: jax.ShapeDtypeStruct((L, n), BF16)
    return pl.pallas_call(
        body, name="mix_bwd1", grid=(nt,),
        in_specs=[row(D), row(D), full(g2), row(D), full(w_out), t(CB_GB), t(CB_GC), t(CB_GV), t(CB_GO),
                  hp(CB_GC), hp(CB_GV), hn(CB_GC), hn(CB_GV), row(DG), row(DG), full(conv_a), full(ghn4)] + [_ANY] * na,
        out_specs=[row(DC), row(DC), row(DG), row(DG), pl.BlockSpec((8, DC), lambda i: (0, 0)),
                   pl.BlockSpec((8, HV), lambda i: (0, 0)), pl.BlockSpec((1, D), lambda i: (0, 0)),
                   pl.BlockSpec((D, D), lambda i: (0, 0))],
        out_shape=[act16(DC), act16(DC), act16(DG), act16(DG), jax.ShapeDtypeStruct((8, DC), F32),
                   jax.ShapeDtypeStruct((8, HV), F32), jax.ShapeDtypeStruct((1, D), F32),
                   jax.ShapeDtypeStruct((D, D), F32)],
        compiler_params=_cp("arbitrary"),
    )(dx1, y, g2, ycat, w_out, P, P, P, P, P, P, P, P, o_f, o_b, conv_a, ghn4, *after)


def _gla_chunk_bwd(qt, kt, kh, v, do, st16, dec, g_ref, m, rev):
    qt16, kt16, kh16, v16, do16 = (t.astype(BF16) for t in (qt, kt, kh, v, do))
    qs, sc = _gla_scores(qt16, kt16, m, rev)
    g = g_ref[...]
    g16 = g.astype(BF16)
    dob = _tile4(do16) * m["bdo"].astype(BF16)
    dv = _mm_tn(sc, dob) + _mm_nt(kh16, g16)
    dsc = jnp.where(m["triu4"] if rev else m["tril4"], _mm_nt(dob, v16), 0.0)
    r1 = _mm(dsc, kt16) * m["bdq"]
    dqt = r1[0:CH] + r1[CH:2 * CH] + r1[2 * CH:3 * CH] + r1[3 * CH:4 * CH] + _mm(do16, st16)
    dkt = _mm_tn(dsc, qs)
    dkh = _mm(v16, g16)
    dd = jnp.sum(g * st16.astype(F32), axis=0, keepdims=True)
    g_ref[...] = g * dec + _mm_tn(do16, qt16) * m["bds"]
    return dv, dqt, dkt, dkh, dd


def gla_bwd(P, do, sf, sb, gcat, gbias, tl):
    L = P.shape[0]
    nb = L // tl
    nc = tl // CH

    def body(qf, kf, vf, lf, dof, sfr, qb, kb, vb, lb, dob, sbr, gc_ref, bs_ref,
             dqf, dkf, dvf, daf, dqb, dkb, dvb, dab, gf, gbk, *scr):
        @pl.when(pl.program_id(0) == 0)
        def _():
            gf[...] = jnp.zeros_like(gf)
            gbk[...] = jnp.zeros_like(gbk)

        m = _gla_masks()
        keys = ("qt", "kt", "kh", "e", "einv", "eout", "dec")
        names = keys + ("dd", "dqt", "dkt", "dkh")
        pf = dict(zip(names, scr[0:11]))
        pb = dict(zip(names, scr[11:22]))
        for (q, k, lr, cols, rev, pr) in ((qf, kf, lf, slice(0, DK), False, pf), (qb, kb, lb, slice(DK, 2 * DK), True, pb)):
            p = _gla_tile_prep(q[...], k[...], _gates(lr, gc_ref, bs_ref, cols), m, rev, nc)
            for key in keys:
                pr[key][...] = p[key]

        def step(c, v, dor, st, g_ref, pr, dv, rev):
            rows = pl.ds(pl.multiple_of(c * CH, CH), CH)
            dvc, dqt, dkt, dkh, dd = _gla_chunk_bwd(pr["qt"][rows, :], pr["kt"][rows, :], pr["kh"][rows, :], v[rows, :],
                                                    dor[rows, :], st[c], pr["dec"][pl.ds(c, 1), :], g_ref, m, rev)
            dv[rows, :] = dvc.astype(BF16)
            pr["dqt"][rows, :] = dqt
            pr["dkt"][rows, :] = dkt
            pr["dkh"][rows, :] = dkh
            pr["dd"][pl.ds(c, 1), :] = dd

        def chunk(c, carry):
            step(nc - 1 - c, vf, dof, sfr, gf, pf, dvf, False)
            step(c, vb, dob, sbr, gbk, pb, dvb, True)
            return carry

        lax.fori_loop(0, nc, chunk, 0, unroll=2)

        def finish(pr, dq, dk, da, rev):
            dqt, dkt, dkh = pr["dqt"][...], pr["dkt"][...], pr["dkh"][...]
            kk = dkh * pr["kh"][...]
            dcum = dqt * pr["qt"][...] - dkt * pr["kt"][...] - kk
            dtot = pr["dd"][...] * pr["dec"][...]
            tri_t = m["tril"] if rev else m["triu"]
            parts = []
            for c in range(nc):
                rs = slice(c * CH, (c + 1) * CH)
                parts.append(_mm_tri(tri_t, dcum[rs]) + (jnp.sum(kk[rs], axis=0, keepdims=True) + dtot[c:c + 1]))
            da[...] = jnp.concatenate(parts, axis=0).astype(BF16)
            dq[...] = (dqt * pr["e"][...] * QSCALE).astype(BF16)
            dk[...] = (dkt * pr["einv"][...] + dkh * pr["eout"][...]).astype(BF16)

        finish(pf, dqf, dkf, daf, False)
        finish(pb, dqb, dkb, dab, True)

    fwd_dir = lambda cb: (lambda i: (nb - 1 - i, cb))
    bwd_dir = lambda cb: (lambda i: (i, cb))

    def side(ix):
        return [pl.BlockSpec((tl, DK), ix(CB_Q)), pl.BlockSpec((tl, DK), ix(CB_K)), pl.BlockSpec((tl, DG), ix(CB_V)),
                pl.BlockSpec((tl, LRW), ix(CB_LR)), pl.BlockSpec((tl, DG), ix(0)),
                pl.BlockSpec((nc, DG, DK), lambda i: (ix(0)(i)[0], 0, 0))]

    def outs(ix):
        return [pl.BlockSpec((tl, DK), ix(0)), pl.BlockSpec((tl, DK), ix(0)), pl.BlockSpec((tl, DG), ix(0)),
                pl.BlockSpec((tl, DK), ix(0))]

    o_shape = [jax.ShapeDtypeStruct((L, DK), BF16), jax.ShapeDtypeStruct((L, DK), BF16),
               jax.ShapeDtypeStruct((L, DG), BF16), jax.ShapeDtypeStruct((L, DK), BF16)]
    return pl.pallas_call(
        body, name="gla_bwd", grid=(nb,),
        in_specs=side(fwd_dir) + side(bwd_dir) + [pl.BlockSpec((LRW, 2 * DK), lambda i: (0, 0)),
                                                  pl.BlockSpec((1, 2 * DK), lambda i: (0, 0))],
        out_specs=outs(fwd_dir) + outs(bwd_dir),
        out_shape=o_shape + o_shape,
        scratch_shapes=[pltpu.VMEM((DG, DK), F32), pltpu.VMEM((DG, DK), F32)]
        + ([pltpu.VMEM((tl, DK), F32)] * 6 + [pltpu.VMEM((nc, DK), F32)] * 2 + [pltpu.VMEM((tl, DK), F32)] * 3) * 2,
        compiler_params=_cp("arbitrary"),
    )(P, P, P, P, do, sf, P, P, P, P, do, sb, gcat, gbias)


def mix_bwd2(dgb, dcc, dgo, gl, P, conv_a, gcat, gbias, w_in, x, dres, g1, tl):
    L = P.shape[0]
    nt = L // tl

    def body(dgbr, dccr, dccp, dccn, dgor, dqf, dkf, dvf, daf, dqb, dkb, dvb, dab, gc, gv, lr, ca, gcr, bsr, wi,
             xr, drr, g1r, dP, dx, dg1, dgcat, dbias):
        i = pl.program_id(0)

        @pl.when(i == 0)
        def _():
            dg1[...] = jnp.zeros_like(dg1)
            dgcat[...] = jnp.zeros_like(dgcat)
            dbias[...] = jnp.zeros_like(dbias)

        p, n = _halo_rows(dccp, dccn, i, nt - 1)
        dc = _conv3_t(dccr[...].astype(F32), p, n, ca)
        pre = _mm(lr[...], gcr[...]) + bsr[...]
        da = jnp.concatenate([daf[...], dab[...]], axis=1).astype(F32)
        add32 = lambda a, b: a[...].astype(F32) + b[...].astype(F32)
        dpre = da * GATE_NORM * (1.0 - _sigmoid(pre))
        dpre16 = dpre.astype(BF16)
        dP[:, 0:DC] = dgbr[...].astype(BF16)
        dP[:, DC:2 * DC] = (dc * gv[...].astype(F32)).astype(BF16)
        dP[:, 2 * DC:3 * DC] = (dc * gc[...].astype(F32)).astype(BF16)
        dP[:, 1536:1792] = add32(dqf, dqb).astype(BF16)
        dP[:, 1792:2048] = add32(dkf, dkb).astype(BF16)
        dP[:, 2048:2560] = add32(dvf, dvb).astype(BF16)
        dP[:, 2560:3072] = dgor[...].astype(BF16)
        dP[:, 3072:3200] = _mm_nt(dpre16, gcr[...]).astype(BF16)
        dgcat[...] += _mm_tn(lr[...], dpre16)
        dbias[0:1, :] += jnp.sum(dpre, axis=0, keepdims=True)
        dh, dg = _rms_bwd(_mm_nt(dP[...], wi[...]), xr[...], g1r[...])
        dx[...] = drr[...] + dh
        dg1[...] += dg

    row = lambda n: pl.BlockSpec((tl, n), lambda i: (i, 0))
    t = lambda w, cb: pl.BlockSpec((tl, w), lambda i: (i, cb))
    full = lambda a: pl.BlockSpec(a.shape, lambda i: (0, 0))
    return pl.pallas_call(
        body, name="mix_bwd2", grid=(nt,),
        in_specs=[row(DC), row(DC), pl.BlockSpec((HALO16, DC), _prev_blk(tl, 0, HALO16)),
                  pl.BlockSpec((HALO16, DC), _next_blk(tl, L, 0, HALO16)),
                  row(DG), row(DK), row(DK), row(DG), row(DK), row(DK), row(DK), row(DG), row(DK),
                  t(DC, CB_GC), t(DC, CB_GV), t(LRW, CB_LR), full(conv_a), full(gcat), full(gbias), full(w_in),
                  row(D), row(D), full(g1)],
        out_specs=[row(DINP), row(D), pl.BlockSpec((1, D), lambda i: (0, 0)), pl.BlockSpec((LRW, 2 * DK), lambda i: (0, 0)),
                   pl.BlockSpec((8, 2 * DK), lambda i: (0, 0))],
        out_shape=[jax.ShapeDtypeStruct((L, DINP), BF16), jax.ShapeDtypeStruct((L, D), F32),
                   jax.ShapeDtypeStruct((1, D), F32), jax.ShapeDtypeStruct((LRW, 2 * DK), F32),
                   jax.ShapeDtypeStruct((8, 2 * DK), F32)],
        compiler_params=_cp("arbitrary"),
    )(dgb, dcc, dcc, dcc, dgo, *gl, P, P, P, conv_a, gcat, gbias, w_in, x, dres, g1)


def _row_tile(rows, cols):
    if rows * cols * 4 <= 2 * 1024 * 1024:
        return rows
    best = 8
    for t in range(8, rows, 8):
        if rows % t == 0 and t * cols * 4 <= 2 * 1024 * 1024:
            best = t
    return best


def adamw(w, g, m, v, name):
    shape = w.shape
    cols = shape[-1]
    w2, g2, m2, v2 = (a.reshape(-1, cols) for a in (w, g, m, v))
    rows = w2.shape[0]
    tr = _row_tile(rows, cols)

    def body(wr, gr, mr, vr, dl, nm, nv):
        gg = gr[...]
        mm = B1 * mr[...] + (1.0 - B1) * gg
        vv = B2 * vr[...] + (1.0 - B2) * (gg * gg)
        m_hat = mm / (1.0 - B1 ** STEP)
        v_hat = vv / (1.0 - B2 ** STEP)
        dl[...] = -LR * (m_hat / (jnp.sqrt(v_hat) + AEPS) + WD * wr[...])
        nm[...] = mm
        nv[...] = vv

    blk = pl.BlockSpec((tr, cols), lambda i: (i, 0))
    o = jax.ShapeDtypeStruct((rows, cols), F32)
    d, nm, nv = pl.pallas_call(
        body, name=name, grid=(rows // tr,), in_specs=[blk] * 4, out_specs=[blk] * 3, out_shape=[o, o, o],
        compiler_params=_cp("parallel"),
    )(w2, g2, m2, v2)
    return d.reshape(shape), nm.reshape(shape), nv.reshape(shape)


def _place():
    return lax.axis_index("x"), lax.axis_index("y"), lax.axis_index("c")


def allgather8(v, name):
    mp, n = v.shape

    def body(x_ref, out_ref, send_sems, recv_sems, local_sem):
        x, y, c = _place()
        me, sibling = (x, y, c), (x, y, 1 - c)
        chips = [(1 - x, y), (x, 1 - y), (1 - x, 1 - y)]

        def rows(px, py, pc):
            return out_ref.at[pl.ds((4 * px + 2 * py + pc) * mp, mp), :]

        def copy(k, block, to, src=None):
            return pltpu.make_async_remote_copy(
                src_ref=rows(*block) if src is None else src, dst_ref=rows(*block),
                send_sem=send_sems.at[k], recv_sem=recv_sems.at[k], device_id=to, device_id_type=MESH)

        mine = pltpu.make_async_copy(x_ref, rows(*me), local_sem)
        mine.start()
        first = [copy(0, me, sibling, src=x_ref)]
        first += [copy(1 + j, me, (*chip, c), src=x_ref) for j, chip in enumerate(chips)]
        for cp in first:
            cp.start()
        passed = [copy(4 + j, (*chip, c), sibling) for j, chip in enumerate(chips)]
        for j, chip in enumerate(chips):
            copy(1 + j, (*chip, c), me).wait_recv()
            passed[j].start()
        copy(0, sibling, me).wait_recv()
        for j, chip in enumerate(chips):
            copy(4 + j, (*chip, 1 - c), me).wait_recv()
        for cp in first + passed:
            cp.wait_send()
        mine.wait()

    return pl.pallas_call(
        body, name=name, out_shape=jax.ShapeDtypeStruct((8 * mp, n), v.dtype),
        in_specs=[pl.BlockSpec(memory_space=pltpu.VMEM)], out_specs=pl.BlockSpec(memory_space=pltpu.VMEM),
        scratch_shapes=[pltpu.SemaphoreType.DMA((7,)), pltpu.SemaphoreType.DMA((7,)), pltpu.SemaphoreType.DMA],
        compiler_params=pltpu.CompilerParams(vmem_limit_bytes=VMEM_LIMIT),
    )(v)


def sum8(v, mp):
    def body(x_ref, o_ref):
        acc = x_ref[0:mp, :]
        for d in range(1, 8):
            acc = acc + x_ref[d * mp:(d + 1) * mp, :]
        o_ref[...] = acc

    return pl.pallas_call(body, name="sum8", out_shape=jax.ShapeDtypeStruct((mp, v.shape[1]), F32),
                          compiler_params=pltpu.CompilerParams(vmem_limit_bytes=VMEM_LIMIT))(v)


_ANY = pl.BlockSpec(memory_space=pl.ANY)


def _row_half(ref, lead, h):
    hr = ref.shape[-2] // 2
    return ref.at[(*lead, pl.ds(h * hr, hr), slice(None))]


def allgather_weights(slots):
    n = len(slots)

    def body(*refs):
        s_refs, o_refs, (send_sems, recv_sems) = refs[:n], refs[n:2 * n], refs[2 * n:]
        x, y, c = _place()
        me = 2 * x + y
        sibling = (x, y, 1 - c)
        chips = [(1 - x, y), (x, 1 - y), (1 - x, 1 - y)]

        def half(ref, slot, h):
            return _row_half(ref, (slot, slice(None)), h)

        def copy(k, src, dst, to):
            return pltpu.make_async_remote_copy(src_ref=src, dst_ref=dst, send_sem=send_sems.at[k],
                                                recv_sem=recv_sems.at[k], device_id=to, device_id_type=MESH)

        first = [copy(6 * a + k, half(s_refs[a], me, c), half(o_refs[a], me, c), (px, py, c))
                 for k, (px, py) in enumerate(chips) for a in range(n)]
        for cp in first:
            cp.start()
        passed = []
        for k, (px, py) in enumerate(chips):
            for a in range(n):
                got = half(o_refs[a], 2 * px + py, c)
                copy(6 * a + k, half(s_refs[a], me, c), got, (px, py, c)).wait_recv()
                cp = copy(6 * a + 3 + k, got, got, sibling)
                cp.start()
                passed.append(cp)
        for k, (px, py) in enumerate(chips):
            for a in range(n):
                got = half(o_refs[a], 2 * px + py, 1 - c)
                copy(6 * a + 3 + k, got, got, sibling).wait_recv()
        for cp in first + passed:
            cp.wait_send()

    return pl.pallas_call(
        body, name="allgather_weights", out_shape=[jax.ShapeDtypeStruct(s.shape, s.dtype) for s in slots],
        in_specs=[_ANY] * n, out_specs=[_ANY] * n, input_output_aliases={a: a for a in range(n)},
        scratch_shapes=[pltpu.SemaphoreType.DMA((6 * n,)), pltpu.SemaphoreType.DMA((6 * n,))],
    )(*slots)


_HBM = pl.BlockSpec(memory_space=pltpu.HBM)
_SEM = pl.BlockSpec(memory_space=pltpu.SEMAPHORE)
_EFFECT = pltpu.SideEffectType.DATAFLOW_SIDE_EFFECTING


def gather_start(slots, name, after=()):
    n = len(slots)
    na = len(after)

    def body(*refs):
        s_refs, send_sems, recv_sems, token = refs[:n], refs[n + na], refs[n + na + 1], refs[-1]
        x, y, c = _place()
        me = 2 * x + y
        for k, (px, py) in enumerate([(1 - x, y), (x, 1 - y), (1 - x, 1 - y)]):
            for a in range(n):
                pltpu.make_async_remote_copy(
                    src_ref=s_refs[a].at[me], dst_ref=s_refs[a].at[me], send_sem=send_sems.at[3 * a + k],
                    recv_sem=recv_sems.at[3 * a + k], device_id=(px, py, c), device_id_type=MESH).start()
        token[...] = jnp.zeros_like(token)

    out = pl.pallas_call(
        body, name=name,
        out_shape=(pltpu.SemaphoreType.DMA((3 * n,)), pltpu.SemaphoreType.DMA((3 * n,)),
                   *[pltpu.HBM(s.shape, s.dtype) for s in slots], jax.ShapeDtypeStruct((8, 128), F32)),
        in_specs=[_HBM] * n + [_ANY] * na, out_specs=(_SEM, _SEM, *[_HBM] * n, pl.BlockSpec(memory_space=pltpu.VMEM)),
        input_output_aliases={a: 2 + a for a in range(n)},
        compiler_params=pltpu.CompilerParams(has_side_effects=_EFFECT),
    )(*[pltpu.with_memory_space_constraint(s, pltpu.HBM) for s in slots], *after)
    return out[0], out[1], list(out[2:2 + n]), out[-1]


def gather_wait(send_sems, recv_sems, slots, after, name):
    n = len(slots)

    def body(*refs):
        s_refs, ssem, rsem = refs[:n], refs[n], refs[n + 1]
        x, y, c = _place()
        me = 2 * x + y
        for k, (px, py) in enumerate([(1 - x, y), (x, 1 - y), (1 - x, 1 - y)]):
            for a in range(n):
                cp = pltpu.make_async_remote_copy(
                    src_ref=s_refs[a].at[me], dst_ref=s_refs[a].at[2 * px + py], send_sem=ssem.at[3 * a + k],
                    recv_sem=rsem.at[3 * a + k], device_id=(px, py, c), device_id_type=MESH)
                cp.wait_send()
                cp.wait_recv()

    return pl.pallas_call(
        body, name=name, out_shape=[pltpu.HBM(s.shape, s.dtype) for s in slots],
        in_specs=[_HBM] * n + [_SEM, _SEM, _ANY], out_specs=[_HBM] * n,
        input_output_aliases={a: a for a in range(n)},
        compiler_params=pltpu.CompilerParams(has_side_effects=_EFFECT),
    )(*slots, send_sems, recv_sems, after)


def rs_chipsum16(g, recv1, cidx, name):
    nl, hr, cols = recv1.shape[1:]

    def body(c_ref, g_ref, r_ref, o_ref):
        o_ref[...] = (g_ref[...] + r_ref[...]).astype(BF16)

    blk = (1, 1, hr, cols)
    return pl.pallas_call(
        body, name=name, out_shape=jax.ShapeDtypeStruct(recv1.shape, BF16),
        grid_spec=pltpu.PrefetchScalarGridSpec(
            num_scalar_prefetch=1, grid=(4, nl),
            in_specs=[pl.BlockSpec(blk, lambda j, l, c: (j, l, c[0], 0)), pl.BlockSpec(blk, lambda j, l, c: (j, l, 0, 0))],
            out_specs=pl.BlockSpec(blk, lambda j, l, c: (j, l, 0, 0))),
        compiler_params=_cp("parallel", "parallel"),
    )(cidx, g, recv1)


def sibling_start(gs, name):
    n = len(gs)
    lands = [lax.empty((*g.shape[:2], g.shape[2] // 2, g.shape[3]), F32) for g in gs]

    def body(*refs):
        g_refs, l_refs, send_sems, recv_sems, token = refs[:n], refs[n:2 * n], refs[2 * n], refs[2 * n + 1], refs[-1]
        x, y, c = _place()
        for a in range(n):
            pltpu.make_async_remote_copy(
                src_ref=_row_half(g_refs[a], (slice(None), slice(None)), 1 - c), dst_ref=l_refs[a],
                send_sem=send_sems.at[a], recv_sem=recv_sems.at[a], device_id=(x, y, 1 - c), device_id_type=MESH).start()
        token[...] = jnp.zeros_like(token)

    bufs = list(gs) + lands
    out = pl.pallas_call(
        body, name=name,
        out_shape=(pltpu.SemaphoreType.DMA((n,)), pltpu.SemaphoreType.DMA((n,)),
                   *[pltpu.HBM(b.shape, b.dtype) for b in bufs], jax.ShapeDtypeStruct((8, 128), F32)),
        in_specs=[_HBM] * (2 * n), out_specs=(_SEM, _SEM, *[_HBM] * (2 * n), pl.BlockSpec(memory_space=pltpu.VMEM)),
        input_output_aliases={i: 2 + i for i in range(2 * n)},
        compiler_params=pltpu.CompilerParams(has_side_effects=_EFFECT),
    )(*[pltpu.with_memory_space_constraint(b, pltpu.HBM) for b in bufs])
    return out[0], out[1], list(out[2:2 + n]), list(out[2 + n:2 + 2 * n]), out[-1]


def sibling_wait(send_sems, recv_sems, gs, lands, after, name):
    n = len(gs)

    def body(*refs):
        g_refs, l_refs, ssem, rsem = refs[:n], refs[n:2 * n], refs[2 * n], refs[2 * n + 1]
        x, y, c = _place()
        for a in range(n):
            cp = pltpu.make_async_remote_copy(
                src_ref=_row_half(g_refs[a], (slice(None), slice(None)), 1 - c), dst_ref=l_refs[a],
                send_sem=ssem.at[a], recv_sem=rsem.at[a], device_id=(x, y, 1 - c), device_id_type=MESH)
            cp.wait_send()
            cp.wait_recv()

    bufs = list(gs) + list(lands)
    out = pl.pallas_call(
        body, name=name, out_shape=[pltpu.HBM(b.shape, b.dtype) for b in bufs],
        in_specs=[_HBM] * (2 * n) + [_SEM, _SEM, _ANY], out_specs=[_HBM] * (2 * n),
        input_output_aliases={i: i for i in range(2 * n)},
        compiler_params=pltpu.CompilerParams(has_side_effects=_EFFECT),
    )(*bufs, send_sems, recv_sems, after)
    return list(out[:n]), list(out[n:])


def exchange_start(cs, name):
    n = len(cs)
    lands = [lax.empty((3, *c.shape[1:]), BF16) for c in cs]

    def body(*refs):
        s_refs, l_refs, send_sems, recv_sems, token = refs[:n], refs[n:2 * n], refs[2 * n], refs[2 * n + 1], refs[-1]
        x, y, c = _place()
        for k, (px, py) in enumerate([(1 - x, y), (x, 1 - y), (1 - x, 1 - y)]):
            for a in range(n):
                pltpu.make_async_remote_copy(
                    src_ref=s_refs[a].at[2 * px + py], dst_ref=l_refs[a].at[k], send_sem=send_sems.at[3 * a + k],
                    recv_sem=recv_sems.at[3 * a + k], device_id=(px, py, c), device_id_type=MESH).start()
        token[...] = jnp.zeros_like(token)

    bufs = list(cs) + lands
    out = pl.pallas_call(
        body, name=name,
        out_shape=(pltpu.SemaphoreType.DMA((3 * n,)), pltpu.SemaphoreType.DMA((3 * n,)),
                   *[pltpu.HBM(b.shape, b.dtype) for b in bufs], jax.ShapeDtypeStruct((8, 128), F32)),
        in_specs=[_HBM] * (2 * n), out_specs=(_SEM, _SEM, *[_HBM] * (2 * n), pl.BlockSpec(memory_space=pltpu.VMEM)),
        input_output_aliases={i: 2 + i for i in range(2 * n)},
        compiler_params=pltpu.CompilerParams(has_side_effects=_EFFECT),
    )(*[pltpu.with_memory_space_constraint(b, pltpu.HBM) for b in bufs])
    return out[0], out[1], list(out[2:2 + n]), list(out[2 + n:2 + 2 * n]), out[-1]


def exchange_wait(send_sems, recv_sems, cs, lands, after, name):
    n = len(cs)

    def body(*refs):
        s_refs, l_refs, ssem, rsem = refs[:n], refs[n:2 * n], refs[2 * n], refs[2 * n + 1]
        x, y, c = _place()
        for k, (px, py) in enumerate([(1 - x, y), (x, 1 - y), (1 - x, 1 - y)]):
            for a in range(n):
                cp = pltpu.make_async_remote_copy(
                    src_ref=s_refs[a].at[2 * px + py], dst_ref=l_refs[a].at[k], send_sem=ssem.at[3 * a + k],
                    recv_sem=rsem.at[3 * a + k], device_id=(px, py, c), device_id_type=MESH)
                cp.wait_send()
                cp.wait_recv()

    bufs = list(cs) + list(lands)
    out = pl.pallas_call(
        body, name=name, out_shape=[pltpu.HBM(b.shape, b.dtype) for b in bufs],
        in_specs=[_HBM] * (2 * n) + [_SEM, _SEM, _ANY], out_specs=[_HBM] * (2 * n),
        input_output_aliases={i: i for i in range(2 * n)},
        compiler_params=pltpu.CompilerParams(has_side_effects=_EFFECT),
    )(*bufs, send_sems, recv_sems, after)
    return list(out[n:])


def rs_final_sum(g, recv1, recv2, idx, name):
    nl, hr, cols = recv1.shape[1:]

    def body(i_ref, g_ref, r1_ref, r2_ref, o_ref):
        acc = g_ref[0, 0] + r1_ref[0, 0]
        for k in range(3):
            acc = acc + r2_ref[k, 0].astype(F32)
        o_ref[0] = acc

    blk = (1, 1, hr, cols)
    return pl.pallas_call(
        body, name=name, out_shape=jax.ShapeDtypeStruct((nl, 2 * hr, cols), F32),
        grid_spec=pltpu.PrefetchScalarGridSpec(
            num_scalar_prefetch=1, grid=(nl,),
            in_specs=[pl.BlockSpec(blk, lambda l, ix: (ix[0], l, ix[1], 0)), pl.BlockSpec(blk, lambda l, ix: (ix[0], l, 0, 0)),
                      pl.BlockSpec((3, 1, hr, cols), lambda l, ix: (0, l, 0, 0))],
            out_specs=pl.BlockSpec((1, hr, cols), lambda l, ix: (l, ix[1], 0))),
        compiler_params=_cp("parallel"),
    )(idx, g, recv1, recv2)


def rs_share_halves(fulls):
    n = len(fulls)

    def body(*refs):
        h_refs, o_refs, (send_sems, recv_sems) = refs[:n], refs[n:2 * n], refs[2 * n:]
        x, y, c = _place()
        sibling = (x, y, 1 - c)

        def copy(a, h):
            return pltpu.make_async_remote_copy(
                src_ref=_row_half(h_refs[a], (slice(None),), h), dst_ref=_row_half(o_refs[a], (slice(None),), h),
                send_sem=send_sems.at[a], recv_sem=recv_sems.at[a], device_id=sibling, device_id_type=MESH)

        for a in range(n):
            copy(a, c).start()
        for a in range(n):
            copy(a, c).wait_send()
            copy(a, 1 - c).wait_recv()

    return pl.pallas_call(
        body, name="rs_share_halves", out_shape=[jax.ShapeDtypeStruct(f.shape, F32) for f in fulls],
        in_specs=[_ANY] * n, out_specs=[_ANY] * n, input_output_aliases={a: a for a in range(n)},
        scratch_shapes=[pltpu.SemaphoreType.DMA((n,)), pltpu.SemaphoreType.DMA((n,))],
    )(*fulls)


def _own_slot(shard, chip, dtype):
    return lax.dynamic_update_slice(lax.empty((4, *shard.shape), dtype), shard.astype(dtype)[None],
                                    (chip,) + (0,) * shard.ndim)


def kernel(x, norm_mix_pre, norm_mix_post, norm_ffn_pre, norm_ffn_post, w_in, conv_a, gate_up_fwd, gate_bias_fwd, gate_up_bwd, gate_bias_bwd, gla_head_norm, w_out, w_up, conv_ffn, w_down, loss_target, m_norm_mix_pre, m_norm_mix_post, m_norm_ffn_pre, m_norm_ffn_post, m_w_in, m_conv_a, m_gate_up_fwd, m_gate_bias_fwd, m_gate_up_bwd, m_gate_bias_bwd, m_gla_head_norm, m_w_out, m_w_up, m_conv_ffn, m_w_down, v_norm_mix_pre, v_norm_mix_post, v_norm_ffn_pre, v_norm_ffn_post, v_w_in, v_conv_a, v_gate_up_fwd, v_gate_bias_fwd, v_gate_up_bwd, v_gate_bias_bwd, v_gla_head_norm, v_w_out, v_w_up, v_conv_ffn, v_w_down):
    L = x.shape[1]
    xi, yi, ci = _place()
    chip = 2 * xi + yi
    tl_gla, tl_mix, tl_ffn = min(L, TL_GLA), min(L, TL_MIX), min(L, TL_FFN)

    big_w = (w_in, w_out, w_up, w_down)
    a_in0 = allgather_weights([_own_slot(w_in[0:1], chip, BF16)])[0][:, 0]
    started = []
    prev = (a_in0,)
    for l in range(DEPTH):
        ws = big_w[1:] if l == 0 else big_w
        started.append(gather_start([_own_slot(w[l], chip, BF16) for w in ws], f"gather_start_{l}", after=prev))
        prev = (started[-1][3],)
    tokens = [s[3] for s in started]

    def full_w_in(a_in):
        return jnp.pad(jnp.concatenate([a_in[j] for j in range(4)], axis=1), ((0, 0), (0, DINP - DIN)))

    small = jnp.concatenate([conv_a.reshape(-1), gate_up_fwd.reshape(-1), gate_up_bwd.reshape(-1), conv_ffn.reshape(-1)])
    ms = small.shape[0] // 128
    sg = allgather8(small.reshape(ms, 128), "allgather_small_weights").reshape(4, 2, ms * 128)[:, 0]

    def small_full(off, shape):
        n = shape[0] * shape[1] * shape[2]
        return jnp.concatenate([sg[j, off:off + n].reshape(shape) for j in range(4)], axis=2)

    o1 = DEPTH * 3 * 128
    o2 = o1 + DEPTH * RK * 64
    o3 = o2 + DEPTH * RK * 64
    conv_a_f = small_full(0, (DEPTH, 3, 128))
    gup_f = small_full(o1, (DEPTH, RK, 64))
    gup_b = small_full(o2, (DEPTH, RK, 64))
    conv_ffn_f = small_full(o3, (DEPTH, 3, 1408))

    def gcat_of(l):
        g = jnp.zeros((LRW, 2 * DK), F32)
        g = g.at[0:RK, 0:DK].set(gup_f[l]).at[RK:2 * RK, DK:2 * DK].set(gup_b[l])
        return g.astype(BF16)

    gcats = [gcat_of(l) for l in range(DEPTH)]
    gbiases = [jnp.concatenate([gate_bias_fwd[l], gate_bias_bwd[l]])[None, :] for l in range(DEPTH)]
    ghn4s = [jnp.tile(gla_head_norm[l], NH)[None, :] for l in range(DEPTH)]

    xc = x.reshape(L, D)
    saved = []
    W_in, W_out, W_up, W_down = [], [], [], []
    tl_row = min(L, TL_ROW)
    for l in range(DEPTH):
        ssem, rsem, bufs, _ = started[l]
        if l > 0:
            a_in, a_out, a_up, a_down = gather_wait(ssem, rsem, bufs, xc, f"gather_wait_{l}")
        W_in.append(full_w_in(a_in0 if l == 0 else a_in))
        P, h1 = rms_matmul(xc, norm_mix_pre[l][None, :], W_in[l], DINP, "proj_in", out_dtype=BF16, tm=TM_PROJ // 2,
                           after=tokens if l == 0 else ())
        o_f, o_b, sf, sb = gla_fwd(P, gcats[l], gbiases[l], tl_gla)
        if l == 0:
            a_out, a_up, a_down = gather_wait(ssem, rsem, bufs, o_f, "gather_wait_0")
        W_out.append(a_out.reshape(D, D))
        W_up.append(a_up)
        W_down.append(a_down.reshape(DFF, D))
        ycat, y, x1 = mix_out(P, o_f, o_b, conv_a_f[l], ghn4s[l], W_out[l], norm_mix_post[l][None, :], xc,
                              min(L, TL_MIX_OUT))
        U, h2, ug, uv, z, y2, x2 = ffn_fwd(x1, norm_ffn_pre[l][None, :], W_up[l], conv_ffn_f[l], W_down[l],
                                           norm_ffn_post[l][None, :], tl_ffn)
        saved.append(dict(x=xc, h1=h1, P=P, o_f=o_f, o_b=o_b, sf=sf, sb=sb, ycat=ycat, y=y, x1=x1, h2=h2, U=U, y2=y2,
                          ug=ug, uv=uv, z=z))
        xc = x2

    dx, loss_blk = loss_head(xc, loss_target.reshape(L, D), tl_row)

    big = ("w_in", "w_out", "w_up", "w_down")
    cidx = jnp.reshape(ci, (1,)).astype(jnp.int32)
    idx = jnp.stack([chip, ci]).astype(jnp.int32)
    grads = [None] * DEPTH
    reduced = [dict() for _ in range(DEPTH)]
    tl_dw = min(L, 1024)
    groups = dict(ffn=("w_up", "w_down"), mix=("w_in", "w_out"))
    state = {grp: dict(flight=None, sibling=None) for grp in groups}
    token = ()

    def finish(grp, after):
        lp, gs_p, recv1_p, (ssem, rsem, cs_thru, lands, _) = state[grp]["flight"]
        recv2 = exchange_wait(ssem, rsem, cs_thru, lands, after, f"exchange_wait_{grp}_{lp}")
        halves = [rs_final_sum(g, r1, r2, idx, "rs_final_sum_" + k)
                  for g, r1, r2, k in zip(gs_p, recv1_p, recv2, groups[grp])]
        reduced[lp].update(zip(groups[grp], rs_share_halves(halves)))

    def advance(grp, after):
        st = state[grp]
        ls, (ssem, rsem, gs_thru, lands, _) = st["sibling"]
        gs_s, recv1 = sibling_wait(ssem, rsem, gs_thru, lands, after, f"sibling_wait_{grp}_{ls}")
        cs16 = [rs_chipsum16(g, r, cidx, "rs_chipsum16_" + k) for g, r, k in zip(gs_s, recv1, groups[grp])]
        flight = exchange_start(cs16, f"exchange_start_{grp}_{ls}")
        if st["flight"] is not None:
            finish(grp, flight[4])
        st["flight"] = (ls, gs_s, recv1, flight)
        st["sibling"] = None
        return flight[4]

    for l in reversed(range(DEPTH)):
        s = saved[l]
        dy2, dg4 = rms_bwd_pre(dx, s["y2"], norm_ffn_post[l][None, :], tl_row, after=token)
        g_down = matmul_tn(s["z"], dy2, DFF // 2, D, tl_dw, "dw_down").reshape(4, 1, DFF // 4, D)
        token2 = (advance("mix", g_down),) if state["mix"]["sibling"] is not None else ()
        dU_g, dU_v, dx1, dg3, dcf_g, dcf_v = ffn_bwd(dy2, s["ug"], s["uv"], s["U"], conv_ffn_f[l], W_down[l], W_up[l],
                                                     s["x1"], dx, norm_ffn_pre[l][None, :], tl_ffn)
        g_up = matmul_tn(s["h2"], dU_g, D, WFF, tl_dw, "dw_up_gate",
                         into=(lax.empty((4, 1, D, WFF), F32), (None, None, D, WFF), lambda p, q: (q, 0, 0, 0)))
        g_up = matmul_tn(s["h2"], dU_v, D, WFF, tl_dw, "dw_up_val",
                         into=(g_up, (None, None, D, WFF), lambda p, q: (NFF + q, 0, 0, 0)))
        sib = sibling_start([g_up, g_down], f"sibling_start_ffn_{l}")
        state["ffn"]["sibling"] = (l, sib)
        dgb, dcc, dgo, do, dca, dghn, dg2, dW_out = mix_bwd1(
            dx1, s["y"], norm_mix_post[l][None, :], s["ycat"], W_out[l], s["P"], s["o_f"], s["o_b"], conv_a_f[l],
            ghn4s[l], min(L, TL_MIX_OUT), after=token2 + (sib[4],))
        g_out = dW_out.reshape(4, 1, D // 4, D)
        token3 = advance("ffn", dgb)
        gl = gla_bwd(s["P"], do, s["sf"], s["sb"], gcats[l], gbiases[l], tl_gla)
        dP, dx, dg1, dgcat, dbias = mix_bwd2(dgb, dcc, dgo, gl, s["P"], conv_a_f[l], gcats[l], gbiases[l], W_in[l],
                                             s["x"], dx1, norm_mix_pre[l][None, :], tl_mix)
        dW_in = matmul_tn(s["h1"], dP, D // 2, DINP, tl_dw, "dw_in", after=(token3,))
        g_in = jnp.stack([dW_in[:, (DIN // 4) * j:(DIN // 4) * (j + 1)] for j in range(4)])[:, None]
        grads[l] = dict(
            norm_mix_pre=dg1[0], norm_mix_post=dg2[0], norm_ffn_pre=dg3[0], norm_ffn_post=dg4[0],
            conv_a=dca[0:3], gate_up_fwd=dgcat[0:RK, 0:DK], gate_bias_fwd=dbias[0, 0:DK],
            gate_up_bwd=dgcat[RK:2 * RK, DK:2 * DK], gate_bias_bwd=dbias[0, DK:2 * DK], gla_head_norm=dghn[0],
            conv_ffn=jnp.concatenate([dcf_g[0:3], dcf_v[0:3]], axis=1))
        sib = sibling_start([g_in, g_out], f"sibling_start_mix_{l}")
        state["mix"]["sibling"] = (l, sib)
        token = (sib[4],)
    last = advance("mix", token[0])
    finish("ffn", last)
    finish("mix", last)

    G = {k: jnp.stack([grads[l][k] for l in range(DEPTH)]) for k in grads[0]}

    small_names = ["norm_mix_pre", "norm_mix_post", "norm_ffn_pre", "norm_ffn_post", "conv_a", "gate_up_fwd",
                   "gate_bias_fwd", "gate_up_bwd", "gate_bias_bwd", "gla_head_norm", "conv_ffn"]
    flat = jnp.concatenate([G[k].reshape(-1) for k in small_names] + [loss_blk[0, 0:1]])
    n_small = flat.shape[0]
    mp = -(-n_small // 1024) * 8
    flat = jnp.pad(flat, (0, mp * 128 - n_small)).reshape(mp, 128)
    tot = sum8(allgather8(flat, "allgather_small_grads"), mp).reshape(-1)
    gsm = {}
    o = 0
    for k in small_names:
        n = G[k].size
        gsm[k] = tot[o:o + n].reshape(G[k].shape)
        o += n
    loss = tot[o]

    def my_cols(a, width):
        return lax.dynamic_slice_in_dim(a, chip * width, width, axis=2)

    gsm["conv_a"] = my_cols(gsm["conv_a"], 128)
    gsm["gate_up_fwd"] = my_cols(gsm["gate_up_fwd"], 64)
    gsm["gate_up_bwd"] = my_cols(gsm["gate_up_bwd"], 64)
    gsm["conv_ffn"] = my_cols(gsm["conv_ffn"], 1408)

    for k in big:
        gsm[k] = jnp.concatenate([reduced[l][k] for l in range(DEPTH)], axis=0)

    names = ["norm_mix_pre", "norm_mix_post", "norm_ffn_pre", "norm_ffn_post", "w_in", "conv_a", "gate_up_fwd",
             "gate_bias_fwd", "gate_up_bwd", "gate_bias_bwd", "gla_head_norm", "w_out", "w_up", "conv_ffn", "w_down"]
    w = dict(norm_mix_pre=norm_mix_pre, norm_mix_post=norm_mix_post, norm_ffn_pre=norm_ffn_pre, norm_ffn_post=norm_ffn_post,
             w_in=w_in, conv_a=conv_a, gate_up_fwd=gate_up_fwd, gate_bias_fwd=gate_bias_fwd, gate_up_bwd=gate_up_bwd,
             gate_bias_bwd=gate_bias_bwd, gla_head_norm=gla_head_norm, w_out=w_out, w_up=w_up, conv_ffn=conv_ffn, w_down=w_down)
    m = dict(norm_mix_pre=m_norm_mix_pre, norm_mix_post=m_norm_mix_post, norm_ffn_pre=m_norm_ffn_pre, norm_ffn_post=m_norm_ffn_post,
             w_in=m_w_in, conv_a=m_conv_a, gate_up_fwd=m_gate_up_fwd, gate_bias_fwd=m_gate_bias_fwd, gate_up_bwd=m_gate_up_bwd,
             gate_bias_bwd=m_gate_bias_bwd, gla_head_norm=m_gla_head_norm, w_out=m_w_out, w_up=m_w_up, conv_ffn=m_conv_ffn, w_down=m_w_down)
    v = dict(norm_mix_pre=v_norm_mix_pre, norm_mix_post=v_norm_mix_post, norm_ffn_pre=v_norm_ffn_pre, norm_ffn_post=v_norm_ffn_post,
             w_in=v_w_in, conv_a=v_conv_a, gate_up_fwd=v_gate_up_fwd, gate_bias_fwd=v_gate_bias_fwd, gate_up_bwd=v_gate_up_bwd,
             gate_bias_bwd=v_gate_bias_bwd, gla_head_norm=v_gla_head_norm, w_out=v_w_out, w_up=v_w_up, conv_ffn=v_conv_ffn, w_down=v_w_down)
    upd = {k: adamw(w[k], gsm[k], m[k], v[k], "adamw_" + k) for k in names}
    return (loss, dx.reshape(1, L, D), *[gsm[k] for k in names], *[upd[k][0] for k in names],
            *[upd[k][1] for k in names], *[upd[k][2] for k in names])
```

```python
import functools

import jax
import jax.numpy as jnp
from jax import lax
from jax.experimental import pallas as pl
from jax.experimental.pallas import tpu as pltpu

F32 = jnp.float32
BF16 = jnp.bfloat16
MXU_DTYPE = jnp.bfloat16
MESH = pl.DeviceIdType.MESH

D = 1024
DC = 512
DG = 512
NH = 4
HV = 128
HK = 64
DK = 256
RK = 16
CH = 64
DFF = 2816
DIN = 3104
DINP = 3200
LRW = 128
DEPTH = 4
EPS = 1e-6
QSCALE = HK ** -0.5
GATE_NORM = 1.0 / 16.0
CB_GB, CB_GC, CB_GV, CB_GO = 0, 1, 2, 5
CB_Q, CB_K = 6, 7
CB_V = 4
CB_LR = 24
LR = 0.001
B1 = 0.9
B2 = 0.999
AEPS = 1e-08
WD = 0.01
STEP = 10
TM_PROJ = 1024
TL_GLA = 512
TL_MIX = 512
TL_MIX_OUT = 512
TL_ROW = 1024
TL_FFN = 256
VMEM_LIMIT = 56 * 1024 * 1024


def _cp(*sem):
    return pltpu.CompilerParams(dimension_semantics=sem if sem else None, vmem_limit_bytes=VMEM_LIMIT)


def _mm(a, b):
    return jnp.dot(a.astype(MXU_DTYPE), b.astype(MXU_DTYPE), preferred_element_type=F32)


def _mm_nt(a, b):
    return lax.dot_general(a.astype(MXU_DTYPE), b.astype(MXU_DTYPE), (((1,), (1,)), ((), ())),
                           preferred_element_type=F32)


def _mm_tn(a, b):
    return lax.dot_general(a.astype(MXU_DTYPE), b.astype(MXU_DTYPE), (((0,), (0,)), ((), ())),
                           preferred_element_type=F32)


def _mm_tri(tri, b):
    t = tri.astype(BF16)
    b1 = b.astype(BF16)
    r1 = b - b1.astype(F32)
    b2 = r1.astype(BF16)
    b3 = (r1 - b2.astype(F32)).astype(BF16)
    dot = lambda u: jnp.dot(t, u, preferred_element_type=F32)
    return dot(b1) + dot(b2) + dot(b3)


def _rms(x, g):
    r = lax.rsqrt(jnp.mean(x * x, axis=-1, keepdims=True) + EPS)
    return x * r * g


def _rms_bwd(dout, y, g):
    r = lax.rsqrt(jnp.mean(y * y, axis=-1, keepdims=True) + EPS)
    yh = y * r
    dyh = dout * g
    dy = r * (dyh - yh * jnp.mean(dyh * yh, axis=-1, keepdims=True))
    dg = jnp.sum(dout * yh, axis=0, keepdims=True)
    return dy, dg


def _sigmoid(x):
    return 0.5 * jnp.tanh(0.5 * x) + 0.5


def _logsig(x):
    return jnp.minimum(x, 0.0) - jnp.log1p(jnp.exp(-jnp.abs(x)))


def _shifts(x, p8, n8):
    n = x.shape[0]
    xe = jnp.concatenate([p8, x, n8], axis=0)
    return pltpu.roll(xe, 1, 0)[8:8 + n], pltpu.roll(xe, n + 15, 0)[8:8 + n]


def _halo_rows(prev_ref, next_ref, i, last):
    hr = prev_ref.shape[0]
    p = jnp.where(i == 0, 0.0, prev_ref[...].astype(F32)[hr - 8:hr, :])
    n = jnp.where(i == last, 0.0, next_ref[...].astype(F32)[0:8, :])
    return p, n


def _conv3(x, xp, xn, w_ref):
    xm1, xp1 = _shifts(x, xp, xn)
    return w_ref[0:1, :] * xm1 + w_ref[1:2, :] * x + w_ref[2:3, :] * xp1, xm1, xp1


def _conv3_t(d, dp, dn, w_ref):
    dm1, dp1 = _shifts(d, dp, dn)
    return w_ref[0:1, :] * dp1 + w_ref[1:2, :] * d + w_ref[2:3, :] * dm1


HALO32 = 8
HALO16 = 16


def _prev_row_blk(i, tl, hr):
    return jnp.maximum(i * (tl // hr) - 1, 0)


def _next_row_blk(i, tl, nrows, hr):
    return jnp.minimum((i + 1) * (tl // hr), nrows // hr - 1)


def _prev_blk(tl, cb, hr=HALO32):
    return lambda i: (_prev_row_blk(i, tl, hr), cb)


def _next_blk(tl, nrows, cb, hr=HALO32):
    return lambda i: (_next_row_blk(i, tl, nrows, hr), cb)


def rms_matmul(x, g, w, tn, name, w_spec=None, n_out=None, out_dtype=F32, after=(), tm=TM_PROJ):
    L = x.shape[0]
    N = w.shape[1] if n_out is None else n_out
    tm = min(L, tm)
    if w_spec is None:
        w_spec = pl.BlockSpec((D, tn), lambda i, j: (0, j))

    def body(x_ref, g_ref, w_ref, *rest):
        o_ref, h_ref = rest[-2:]

        @pl.when(pl.program_id(1) == 0)
        def _():
            h_ref[...] = _rms(x_ref[...], g_ref[...]).astype(BF16)

        o_ref[...] = _mm(h_ref[...], w_ref[...]).astype(out_dtype)

    return pl.pallas_call(
        body, name=name, grid=(L // tm, N // tn),
        in_specs=[pl.BlockSpec((tm, D), lambda i, j: (i, 0)), pl.BlockSpec((1, D), lambda i, j: (0, 0)), w_spec]
        + [_ANY] * len(after),
        out_specs=[pl.BlockSpec((tm, tn), lambda i, j: (i, j)), pl.BlockSpec((tm, D), lambda i, j: (i, 0))],
        out_shape=[jax.ShapeDtypeStruct((L, N), out_dtype), jax.ShapeDtypeStruct((L, D), BF16)],
        compiler_params=_cp("parallel", "arbitrary"),
    )(x, g, w, *after)


def _gla_masks():
    def blk(shape, rdiv, cdiv):
        r = lax.broadcasted_iota(jnp.int32, shape, 0) // rdiv
        c = lax.broadcasted_iota(jnp.int32, shape, 1) // cdiv
        return (r == c).astype(F32)

    r = lax.broadcasted_iota(jnp.int32, (CH, CH), 0)
    c = lax.broadcasted_iota(jnp.int32, (CH, CH), 1)
    r4 = lax.broadcasted_iota(jnp.int32, (NH * CH, CH), 0) % CH
    c4 = lax.broadcasted_iota(jnp.int32, (NH * CH, CH), 1)
    return dict(
        bdq=blk((NH * CH, DK), CH, HK),
        bdo=blk((NH * CH, DG), CH, HV),
        bds=blk((DG, DK), HV, HK),
        tril=(r >= c).astype(F32), triu=(r <= c).astype(F32),
        tril4=r4 >= c4, triu4=r4 <= c4,
    )


def _tile4(x):
    return jnp.concatenate([x, x, x, x], axis=0)


def _gla_tile_prep(q, k, a, m, rev, nc):
    tri = m["triu"] if rev else m["tril"]
    chunks = [a[c * CH:(c + 1) * CH] for c in range(nc)]
    cum = jnp.concatenate([_mm_tri(tri, ac) for ac in chunks], axis=0)
    tot = jnp.concatenate([jnp.sum(ac, axis=0, keepdims=True) for ac in chunks], axis=0)
    tot_rows = jnp.concatenate([jnp.broadcast_to(tot[c:c + 1], (CH, DK)) for c in range(nc)], axis=0)
    e = jnp.exp(cum)
    einv = jnp.exp(-cum)
    eout = jnp.exp(tot_rows - cum)
    q, k = q.astype(F32), k.astype(F32)
    return dict(e=e, einv=einv, eout=eout, dec=jnp.exp(tot), qt=q * QSCALE * e, kt=k * einv, kh=k * eout)


def _gla_scores(qt16, kt16, m, rev):
    qs = _tile4(qt16) * m["bdq"].astype(qt16.dtype)
    return qs, jnp.where(m["triu4"] if rev else m["tril4"], _mm_nt(qs, kt16), 0.0)


def _gla_chunk_fwd(qt16, kt16, kh16, v, dec, st_ref, m, rev):
    _, sc = _gla_scores(qt16, kt16, m, rev)
    v16 = v.astype(BF16)
    r = _mm(sc, v16)
    o_intra = jnp.concatenate([r[h * CH:(h + 1) * CH, h * HV:(h + 1) * HV] for h in range(NH)], axis=1)
    st = st_ref[...]
    st16 = st.astype(BF16)
    o = o_intra + _mm_nt(qt16, st16)
    st_ref[...] = st * dec + _mm_tn(v16, kh16) * m["bds"]
    return o, st16


def _gates(lr_ref, gc_ref, bs_ref, cols):
    return _logsig(_mm(lr_ref[...], gc_ref[:, cols]) + bs_ref[:, cols]) * GATE_NORM


def gla_fwd(P, gcat, gbias, tl):
    L = P.shape[0]
    nb = L // tl
    nc = tl // CH

    def body(qf, kf, vf, lf, qb, kb, vb, lb, gc_ref, bs_ref, of, ob, sf, sb, stf, stb,
             qtf, ktf, khf, dcf, qtb, ktb, khb, dcb):
        @pl.when(pl.program_id(0) == 0)
        def _():
            stf[...] = jnp.zeros_like(stf)
            stb[...] = jnp.zeros_like(stb)

        m = _gla_masks()
        for (q, k, lr, cols, rev, qt, kt, kh, dc) in ((qf, kf, lf, slice(0, DK), False, qtf, ktf, khf, dcf),
                                                      (qb, kb, lb, slice(DK, 2 * DK), True, qtb, ktb, khb, dcb)):
            p = _gla_tile_prep(q[...], k[...], _gates(lr, gc_ref, bs_ref, cols), m, rev, nc)
            qt[...] = p["qt"].astype(BF16)
            kt[...] = p["kt"].astype(BF16)
            kh[...] = p["kh"].astype(BF16)
            dc[...] = p["dec"]

        def chunk(c, carry):
            rows = pl.ds(pl.multiple_of(c * CH, CH), CH)
            o, st = _gla_chunk_fwd(qtf[rows, :], ktf[rows, :], khf[rows, :], vf[rows, :], dcf[pl.ds(c, 1), :], stf, m, False)
            of[rows, :] = o.astype(BF16)
            sf[c] = st
            cb = nc - 1 - c
            rows = pl.ds(pl.multiple_of(cb * CH, CH), CH)
            o, st = _gla_chunk_fwd(qtb[rows, :], ktb[rows, :], khb[rows, :], vb[rows, :], dcb[pl.ds(cb, 1), :], stb, m, True)
            ob[rows, :] = o.astype(BF16)
            sb[cb] = st
            return carry

        lax.fori_loop(0, nc, chunk, 0, unroll=2)

    fw = lambda cb: (lambda i: (i, cb))
    bw = lambda cb: (lambda i: (nb - 1 - i, cb))
    return pl.pallas_call(
        body, name="gla_fwd", grid=(nb,),
        in_specs=[pl.BlockSpec((tl, DK), fw(CB_Q)), pl.BlockSpec((tl, DK), fw(CB_K)), pl.BlockSpec((tl, DG), fw(CB_V)),
                  pl.BlockSpec((tl, LRW), fw(CB_LR)),
                  pl.BlockSpec((tl, DK), bw(CB_Q)), pl.BlockSpec((tl, DK), bw(CB_K)), pl.BlockSpec((tl, DG), bw(CB_V)),
                  pl.BlockSpec((tl, LRW), bw(CB_LR)),
                  pl.BlockSpec((LRW, 2 * DK), lambda i: (0, 0)), pl.BlockSpec((1, 2 * DK), lambda i: (0, 0))],
        out_specs=[pl.BlockSpec((tl, DG), lambda i: (i, 0)), pl.BlockSpec((tl, DG), lambda i: (nb - 1 - i, 0)),
                   pl.BlockSpec((nc, DG, DK), lambda i: (i, 0, 0)), pl.BlockSpec((nc, DG, DK), lambda i: (nb - 1 - i, 0, 0))],
        out_shape=[jax.ShapeDtypeStruct((L, DG), BF16), jax.ShapeDtypeStruct((L, DG), BF16),
                   jax.ShapeDtypeStruct((L // CH, DG, DK), BF16), jax.ShapeDtypeStruct((L // CH, DG, DK), BF16)],
        scratch_shapes=[pltpu.VMEM((DG, DK), F32), pltpu.VMEM((DG, DK), F32)]
        + [pltpu.VMEM((tl, DK), BF16)] * 3 + [pltpu.VMEM((nc, DK), F32)]
        + [pltpu.VMEM((tl, DK), BF16)] * 3 + [pltpu.VMEM((nc, DK), F32)],
        compiler_params=_cp("arbitrary"),
    )(P, P, P, P, P, P, P, P, gcat, gbias)


def _headnorm(o):
    oh, rs = [], []
    for h in range(NH):
        oo = o[:, h * HV:(h + 1) * HV]
        r = lax.rsqrt(jnp.mean(oo * oo, axis=-1, keepdims=True) + EPS)
        oh.append(oo * r)
        rs.append(r)
    return jnp.concatenate(oh, axis=1), rs


def mix_out(P, o_f, o_b, conv_a, ghn4, w_out, g2, x, tl):
    L = P.shape[0]
    nt = L // tl

    def body(gb, gc, gv, go, gcp, gvp, gcn, gvn, of, ob, ca, gh, wo, g2r, xr, ycat, yr, x1):
        i = pl.program_id(0)
        cp, cn = _halo_rows(gcp, gcn, i, nt - 1)
        vp, vn = _halo_rows(gvp, gvn, i, nt - 1)
        c = gc[...].astype(F32) * gv[...].astype(F32)
        cc, _, _ = _conv3(c, cp * vp, cn * vn, ca)
        ya = gb[...].astype(F32) * cc
        oh, _ = _headnorm(of[...].astype(F32) + ob[...].astype(F32))
        g = go[...].astype(F32)
        yb = g * _sigmoid(g) * (oh * gh[...])
        yc = jnp.concatenate([ya, yb], axis=1).astype(BF16)
        ycat[...] = yc
        y = _mm(yc, wo[...])
        yr[...] = y
        x1[...] = xr[...] + _rms(y, g2r[...])

    t = lambda cb: pl.BlockSpec((tl, DC), lambda i: (i, cb))
    hp = lambda cb: pl.BlockSpec((HALO16, DC), _prev_blk(tl, cb, HALO16))
    hn = lambda cb: pl.BlockSpec((HALO16, DC), _next_blk(tl, L, cb, HALO16))
    row = lambda n: pl.BlockSpec((tl, n), lambda i: (i, 0))
    full = lambda a: pl.BlockSpec(a.shape, lambda i: (0, 0))
    return pl.pallas_call(
        body, name="mix_out", grid=(nt,),
        in_specs=[t(CB_GB), t(CB_GC), t(CB_GV), t(CB_GO), hp(CB_GC), hp(CB_GV), hn(CB_GC), hn(CB_GV),
                  row(DG), row(DG), full(conv_a), full(ghn4), full(w_out), full(g2), row(D)],
        out_specs=[row(D), row(D), row(D)],
        out_shape=[jax.ShapeDtypeStruct((L, D), BF16), jax.ShapeDtypeStruct((L, D), F32),
                   jax.ShapeDtypeStruct((L, D), F32)],
        compiler_params=_cp("parallel"),
    )(P, P, P, P, P, P, P, P, o_f, o_b, conv_a, ghn4, w_out, g2, x)


NFF = 2
WFF = DFF // NFF
FFN_COL_CHUNKS = ((0, 512), (512, 1024), (1024, WFF))


def ffn_fwd(x1, g3, w_up, conv_ffn, w_down, g4, tl):
    L = x1.shape[0]
    nt = L // tl
    hh = HALO32

    def body(xr, xp, xn, g3r, wu, cf, wd, g4r, U, h2, ug, uv, zr, y2, x2):
        i = pl.program_id(0)
        he32 = _rms(jnp.concatenate([xp[...], xr[...], xn[...]], axis=0), g3r[...])
        he = he32.astype(BF16)
        h2[...] = he32[hh:hh + tl].astype(BF16)
        acc = jnp.zeros((tl, D), F32)
        for j in range(NFF):
            conv = []
            for blk, off in ((j, j * WFF), (NFF + j, DFF + j * WFF)):
                ue = _mm(he, wu[blk])
                p8 = jnp.where(i == 0, 0.0, ue[hh - 8:hh])
                n8 = jnp.where(i == nt - 1, 0.0, ue[hh + tl:hh + tl + 8])
                mid = ue[hh:hh + tl]
                U[:, off:off + WFF] = mid.astype(BF16)
                conv.append(_conv3(mid, p8, n8, cf.at[:, off:off + WFF])[0])
            gs = slice(j * WFF, (j + 1) * WFF)
            zz = (conv[0] * _sigmoid(conv[0]) * conv[1]).astype(BF16)
            ug[:, gs] = conv[0].astype(BF16)
            uv[:, gs] = conv[1].astype(BF16)
            zr[:, gs] = zz
            acc = acc + _mm(zz, wd[gs, :])
        y2[...] = acc
        x2[...] = xr[...] + _rms(acc, g4r[...])

    row = lambda n: pl.BlockSpec((tl, n), lambda i: (i, 0))
    full = lambda a: pl.BlockSpec(a.shape, lambda i: (0,) * a.ndim)
    once = lambda a: pl.BlockSpec(a.shape, lambda i: (0,) * a.ndim, pipeline_mode=pl.Buffered(1))
    half = jax.ShapeDtypeStruct((L, DFF), BF16)
    return pl.pallas_call(
        body, name="ffn_fwd", grid=(nt,),
        in_specs=[row(D), pl.BlockSpec((hh, D), lambda i: (_prev_row_blk(i, tl, hh), 0)),
                  pl.BlockSpec((hh, D), lambda i: (_next_row_blk(i, tl, L, hh), 0)),
                  full(g3), once(w_up), full(conv_ffn), once(w_down), full(g4)],
        out_specs=[row(2 * DFF), row(D), row(DFF), row(DFF), row(DFF), row(D), row(D)],
        out_shape=[jax.ShapeDtypeStruct((L, 2 * DFF), BF16), jax.ShapeDtypeStruct((L, D), BF16), half, half, half,
                   jax.ShapeDtypeStruct((L, D), F32), jax.ShapeDtypeStruct((L, D), F32)],
        compiler_params=_cp("parallel"),
    )(x1, x1, x1, g3, w_up, conv_ffn, w_down, g4)


def loss_head(y, target, tl):
    L = y.shape[0]

    def body(yr, tr, dy, ls):
        @pl.when(pl.program_id(0) == 0)
        def _():
            ls[...] = jnp.zeros_like(ls)

        err = yr[...] - tr[...]
        dy[...] = err * (1.0 / D)
        ls[...] += (0.5 / D) * jnp.sum(err * err)

    row = pl.BlockSpec((tl, D), lambda i: (i, 0))
    return pl.pallas_call(
        body, name="loss_head", grid=(L // tl,), in_specs=[row, row],
        out_specs=[row, pl.BlockSpec((8, 128), lambda i: (0, 0))],
        out_shape=[jax.ShapeDtypeStruct((L, D), F32), jax.ShapeDtypeStruct((8, 128), F32)],
        compiler_params=_cp("arbitrary"),
    )(y, target)


def rms_bwd_pre(dout, y, g, tl, after=()):
    L = y.shape[0]

    def body(dr, yr, gr, *rest):
        dy, dg = rest[-2:]

        @pl.when(pl.program_id(0) == 0)
        def _():
            dg[...] = jnp.zeros_like(dg)

        a, b = _rms_bwd(dr[...], yr[...], gr[...])
        dy[...] = a.astype(BF16)
        dg[...] += b

    row = pl.BlockSpec((tl, D), lambda i: (i, 0))
    vec = pl.BlockSpec((1, D), lambda i: (0, 0))
    return pl.pallas_call(
        body, name="rms_bwd_pre", grid=(L // tl,), in_specs=[row, row, vec] + [_ANY] * len(after), out_specs=[row, vec],
        out_shape=[jax.ShapeDtypeStruct((L, D), BF16), jax.ShapeDtypeStruct((1, D), F32)],
        compiler_params=_cp("arbitrary"),
    )(dout, y, g, *after)


def ffn_bwd(dy2, ug, uv, U, conv_ffn, w_down, w_up, x1, dres, g3, tl):
    L = x1.shape[0]
    nt = L // tl
    hh = HALO16

    def body(dyr, dyp, dyn, ugr, ugp, ugn, uvr, uvp, uvn, Ur, cf, wd, wu, x1r, drr, g3r, dUg, dUv, dx1, dg3, dcg, dcv):
        i = pl.program_id(0)

        @pl.when(i == 0)
        def _():
            dg3[...] = jnp.zeros_like(dg3)
            dcg[...] = jnp.zeros_like(dcg)
            dcv[...] = jnp.zeros_like(dcv)

        ext = lambda p, t, n, cs: jnp.concatenate([p[:, cs], t[:, cs], n[:, cs]], axis=0).astype(F32)
        dye = jnp.concatenate([dyp[...], dyr[...], dyn[...]], axis=0)
        acc = jnp.zeros((tl, D), F32)
        for j in range(NFF):
            dze = _mm_nt(dye, wd[j * WFF:(j + 1) * WFF, :])
            for c0, c1 in FFN_COL_CHUNKS:
                cs = slice(j * WFF + c0, j * WFF + c1)
                a = ext(ugp, ugr, ugn, cs)
                b = ext(uvp, uvr, uvn, cs)
                sg = _sigmoid(a)
                silu = a * sg
                dz = dze[:, c0:c1]
                for de, off, blk, dc, dU in ((dz * b * (sg + silu * (1.0 - sg)), 0, j, dcg, dUg),
                                            (dz * silu, DFF, NFF + j, dcv, dUv)):
                    d = de[hh:hh + tl]
                    p8 = jnp.where(i == 0, 0.0, de[hh - 8:hh])
                    n8 = jnp.where(i == nt - 1, 0.0, de[hh + tl:hh + tl + 8])
                    dm1, dp1 = _shifts(d, p8, n8)
                    wc = slice(off + j * WFF + c0, off + j * WFF + c1)
                    du = (cf[0:1, wc] * dp1 + cf[1:2, wc] * d + cf[2:3, wc] * dm1).astype(BF16)
                    dU[:, cs] = du
                    u = Ur[:, wc].astype(F32)
                    for k, t in enumerate((dp1, d, dm1)):
                        dc[k:k + 1, cs] += jnp.sum(t * u, axis=0, keepdims=True)
                    acc = acc + _mm_nt(du, wu[blk, :, c0:c1])
        dx, dg = _rms_bwd(acc, x1r[...], g3r[...])
        dx1[...] = drr[...] + dx
        dg3[...] += dg

    row = lambda n: pl.BlockSpec((tl, n), lambda i: (i, 0))
    prev = lambda n: pl.BlockSpec((hh, n), lambda i: (_prev_row_blk(i, tl, hh), 0))
    nxt = lambda n: pl.BlockSpec((hh, n), lambda i: (_next_row_blk(i, tl, L, hh), 0))
    full = lambda a: pl.BlockSpec(a.shape, lambda i: (0,) * a.ndim)
    once = lambda a: pl.BlockSpec(a.shape, lambda i: (0,) * a.ndim, pipeline_mode=pl.Buffered(1))
    half = jax.ShapeDtypeStruct((L, DFF), BF16)
    dcs = pl.BlockSpec((8, DFF), lambda i: (0, 0))
    return pl.pallas_call(
        body, name="ffn_bwd", grid=(nt,),
        in_specs=[row(D), prev(D), nxt(D), row(DFF), prev(DFF), nxt(DFF), row(DFF), prev(DFF), nxt(DFF), row(2 * DFF),
                  full(conv_ffn), once(w_down), once(w_up), row(D), row(D), full(g3)],
        out_specs=[row(DFF), row(DFF), row(D), pl.BlockSpec((1, D), lambda i: (0, 0)), dcs, dcs],
        out_shape=[half, half, jax.ShapeDtypeStruct((L, D), F32), jax.ShapeDtypeStruct((1, D), F32),
                   jax.ShapeDtypeStruct((8, DFF), F32), jax.ShapeDtypeStruct((8, DFF), F32)],
        compiler_params=_cp("arbitrary"),
    )(dy2, dy2, dy2, ug, ug, ug, uv, uv, uv, U, conv_ffn, w_down, w_up, x1, dres, g3)


def matmul_tn(a, b, ta, tn, tl, name, into=None, after=()):
    L, Ka = a.shape
    N = b.shape[1]

    def body(ar, br, *rest):
        o = rest[-1]

        @pl.when(pl.program_id(2) == 0)
        def _():
            o[...] = jnp.zeros_like(o)

        o[...] += _mm_tn(ar[...], br[...]).reshape(o.shape)

    in_specs = [pl.BlockSpec((tl, ta), lambda p, q, l: (l, p)), pl.BlockSpec((tl, tn), lambda p, q, l: (l, q))]
    if into is None:
        return pl.pallas_call(
            body, name=name, grid=(Ka // ta, N // tn, L // tl), in_specs=in_specs + [_ANY] * len(after),
            out_specs=pl.BlockSpec((ta, tn), lambda p, q, l: (p, q)),
            out_shape=jax.ShapeDtypeStruct((Ka, N), F32),
            compiler_params=_cp("parallel", "parallel", "arbitrary"),
        )(a, b, *after)
    buf, blk, idx = into
    return pl.pallas_call(
        body, name=name, grid=(Ka // ta, N // tn, L // tl), in_specs=in_specs + [_ANY],
        out_specs=pl.BlockSpec(blk, lambda p, q, l: idx(p, q)),
        out_shape=jax.ShapeDtypeStruct(buf.shape, F32), input_output_aliases={2: 0},
        compiler_params=_cp("parallel", "parallel", "arbitrary"),
    )(a, b, buf)


def mix_bwd1(dx1, y, g2, ycat, w_out, P, o_f, o_b, conv_a, ghn4, tl, after=()):
    L = P.shape[0]
    nt = L // tl
    na = len(after)

    def body(dxr, yr, g2r, ycr, wo, gb, gc, gv, go, gcp, gvp, gcn, gvn, of, ob, ca, gh, *rest):
        dgb, dcc, dgo, do, dca, dgh, dg2, dwo = rest[na:]
        i = pl.program_id(0)

        @pl.when(i == 0)
        def _():
            dca[...] = jnp.zeros_like(dca)
            dgh[...] = jnp.zeros_like(dgh)
            dg2[...] = jnp.zeros_like(dg2)
            dwo[...] = jnp.zeros_like(dwo)

        dyv, dg = _rms_bwd(dxr[...], yr[...], g2r[...])
        dy16 = dyv.astype(BF16)
        dg2[...] += dg
        dwo[...] += _mm_tn(ycr[...], dy16)
        dycat = _mm_nt(dy16, wo[...])
        dya = dycat[:, 0:DC]
        dyb = dycat[:, DC:D]
        cp, cn = _halo_rows(gcp, gcn, i, nt - 1)
        vp, vn = _halo_rows(gvp, gvn, i, nt - 1)
        c = gc[...].astype(F32) * gv[...].astype(F32)
        cc, c_m1, c_p1 = _conv3(c, cp * vp, cn * vn, ca)
        dgb[...] = (dya * cc).astype(BF16)
        d = dya * gb[...].astype(F32)
        dcc[...] = d.astype(BF16)
        for k, s in enumerate((c_m1, c, c_p1)):
            dca[k:k + 1, :] += jnp.sum(d * s, axis=0, keepdims=True)
        oh, rs = _headnorm(of[...].astype(F32) + ob[...].astype(F32))
        g = go[...].astype(F32)
        sg = _sigmoid(g)
        silu = g * sg
        dgo[...] = (dyb * (oh * gh[...]) * (sg * (1.0 + g * (1.0 - sg)))).astype(BF16)
        don = dyb * silu
        t = jnp.sum(don * oh, axis=0, keepdims=True)
        dgh[0:1, :] += t[:, 0:HV] + t[:, HV:2 * HV] + t[:, 2 * HV:3 * HV] + t[:, 3 * HV:4 * HV]
        doh = don * gh[...]
        parts = []
        for h in range(NH):
            hs = slice(h * HV, (h + 1) * HV)
            parts.append(rs[h] * (doh[:, hs] - oh[:, hs] * jnp.mean(doh[:, hs] * oh[:, hs], axis=-1, keepdims=True)))
        do[...] = jnp.concatenate(parts, axis=1).astype(BF16)

    t = lambda cb: pl.BlockSpec((tl, DC), lambda i: (i, cb))
    hp = lambda cb: pl.BlockSpec((HALO16, DC), _prev_blk(tl, cb, HALO16))
    hn = lambda cb: pl.BlockSpec((HALO16, DC), _next_blk(tl, L, cb, HALO16))
    row = lambda n: pl.BlockSpec((tl, n), lambda i: (i, 0))
    full = lambda a: pl.BlockSpec(a.shape, lambda i: (0, 0))
    act16 = lambda n: jax.ShapeDtypeStruct((L, n), BF16)
    return pl.pallas_call(
        body, name="mix_bwd1", grid=(nt,),
        in_specs=[row(D), row(D), full(g2), row(D), full(w_out), t(CB_GB), t(CB_GC), t(CB_GV), t(CB_GO),
                  hp(CB_GC), hp(CB_GV), hn(CB_GC), hn(CB_GV), row(DG), row(DG), full(conv_a), full(ghn4)] + [_ANY] * na,
        out_specs=[row(DC), row(DC), row(DG), row(DG), pl.BlockSpec((8, DC), lambda i: (0, 0)),
                   pl.BlockSpec((8, HV), lambda i: (0, 0)), pl.BlockSpec((1, D), lambda i: (0, 0)),
                   pl.BlockSpec((D, D), lambda i: (0, 0))],
        out_shape=[act16(DC), act16(DC), act16(DG), act16(DG), jax.ShapeDtypeStruct((8, DC), F32),
                   jax.ShapeDtypeStruct((8, HV), F32), jax.ShapeDtypeStruct((1, D), F32),
                   jax.ShapeDtypeStruct((D, D), F32)],
        compiler_params=_cp("arbitrary"),
    )(dx1, y, g2, ycat, w_out, P, P, P, P, P, P, P, P, o_f, o_b, conv_a, ghn4, *after)


def _gla_chunk_bwd(qt, kt, kh, v, do, st16, dec, g_ref, m, rev):
    qt16, kt16, kh16, v16, do16 = (t.astype(BF16) for t in (qt, kt, kh, v, do))
    qs, sc = _gla_scores(qt16, kt16, m, rev)
    g = g_ref[...]
    g16 = g.astype(BF16)
    dob = _tile4(do16) * m["bdo"].astype(BF16)
    dv = _mm_tn(sc, dob) + _mm_nt(kh16, g16)
    dsc = jnp.where(m["triu4"] if rev else m["tril4"], _mm_nt(dob, v16), 0.0)
    r1 = _mm(dsc, kt16) * m["bdq"]
    dqt = r1[0:CH] + r1[CH:2 * CH] + r1[2 * CH:3 * CH] + r1[3 * CH:4 * CH] + _mm(do16, st16)
    dkt = _mm_tn(dsc, qs)
    dkh = _mm(v16, g16)
    dd = jnp.sum(g * st16.astype(F32), axis=0, keepdims=True)
    g_ref[...] = g * dec + _mm_tn(do16, qt16) * m["bds"]
    return dv, dqt, dkt, dkh, dd


def gla_bwd(P, do, sf, sb, gcat, gbias, tl):
    L = P.shape[0]
    nb = L // tl
    nc = tl // CH

    def body(qf, kf, vf, lf, dof, sfr, qb, kb, vb, lb, dob, sbr, gc_ref, bs_ref,
             dqf, dkf, dvf, daf, dqb, dkb, dvb, dab, gf, gbk, *scr):
        @pl.when(pl.program_id(0) == 0)
        def _():
            gf[...] = jnp.zeros_like(gf)
            gbk[...] = jnp.zeros_like(gbk)

        m = _gla_masks()
        keys = ("qt", "kt", "kh", "e", "einv", "eout", "dec")
        names = keys + ("dd", "dqt", "dkt", "dkh")
        pf = dict(zip(names, scr[0:11]))
        pb = dict(zip(names, scr[11:22]))
        for (q, k, lr, cols, rev, pr) in ((qf, kf, lf, slice(0, DK), False, pf), (qb, kb, lb, slice(DK, 2 * DK), True, pb)):
            p = _gla_tile_prep(q[...], k[...], _gates(lr, gc_ref, bs_ref, cols), m, rev, nc)
            for key in keys:
                pr[key][...] = p[key]

        def step(c, v, dor, st, g_ref, pr, dv, rev):
            rows = pl.ds(pl.multiple_of(c * CH, CH), CH)
            dvc, dqt, dkt, dkh, dd = _gla_chunk_bwd(pr["qt"][rows, :], pr["kt"][rows, :], pr["kh"][rows, :], v[rows, :],
                                                    dor[rows, :], st[c], pr["dec"][pl.ds(c, 1), :], g_ref, m, rev)
            dv[rows, :] = dvc.astype(BF16)
            pr["dqt"][rows, :] = dqt
            pr["dkt"][rows, :] = dkt
            pr["dkh"][rows, :] = dkh
            pr["dd"][pl.ds(c, 1), :] = dd

        def chunk(c, carry):
            step(nc - 1 - c, vf, dof, sfr, gf, pf, dvf, False)
            step(c, vb, dob, sbr, gbk, pb, dvb, True)
            return carry

        lax.fori_loop(0, nc, chunk, 0, unroll=2)

        def finish(pr, dq, dk, da, rev):
            dqt, dkt, dkh = pr["dqt"][...], pr["dkt"][...], pr["dkh"][...]
            kk = dkh * pr["kh"][...]
            dcum = dqt * pr["qt"][...] - dkt * pr["kt"][...] - kk
            dtot = pr["dd"][...] * pr["dec"][...]
            tri_t = m["tril"] if rev else m["triu"]
            parts = []
            for c in range(nc):
                rs = slice(c * CH, (c + 1) * CH)
                parts.append(_mm_tri(tri_t, dcum[rs]) + (jnp.sum(kk[rs], axis=0, keepdims=True) + dtot[c:c + 1]))
            da[...] = jnp.concatenate(parts, axis=0).astype(BF16)
            dq[...] = (dqt * pr["e"][...] * QSCALE).astype(BF16)
            dk[...] = (dkt * pr["einv"][...] + dkh * pr["eout"][...]).astype(BF16)

        finish(pf, dqf, dkf, daf, False)
        finish(pb, dqb, dkb, dab, True)

    fwd_dir = lambda cb: (lambda i: (nb - 1 - i, cb))
    bwd_dir = lambda cb: (lambda i: (i, cb))

    def side(ix):
        return [pl.BlockSpec((tl, DK), ix(CB_Q)), pl.BlockSpec((tl, DK), ix(CB_K)), pl.BlockSpec((tl, DG), ix(CB_V)),
                pl.BlockSpec((tl, LRW), ix(CB_LR)), pl.BlockSpec((tl, DG), ix(0)),
                pl.BlockSpec((nc, DG, DK), lambda i: (ix(0)(i)[0], 0, 0))]

    def outs(ix):
        return [pl.BlockSpec((tl, DK), ix(0)), pl.BlockSpec((tl, DK), ix(0)), pl.BlockSpec((tl, DG), ix(0)),
                pl.BlockSpec((tl, DK), ix(0))]

    o_shape = [jax.ShapeDtypeStruct((L, DK), BF16), jax.ShapeDtypeStruct((L, DK), BF16),
               jax.ShapeDtypeStruct((L, DG), BF16), jax.ShapeDtypeStruct((L, DK), BF16)]
    return pl.pallas_call(
        body, name="gla_bwd", grid=(nb,),
        in_specs=side(fwd_dir) + side(bwd_dir) + [pl.BlockSpec((LRW, 2 * DK), lambda i: (0, 0)),
                                                  pl.BlockSpec((1, 2 * DK), lambda i: (0, 0))],
        out_specs=outs(fwd_dir) + outs(bwd_dir),
        out_shape=o_shape + o_shape,
        scratch_shapes=[pltpu.VMEM((DG, DK), F32), pltpu.VMEM((DG, DK), F32)]
        + ([pltpu.VMEM((tl, DK), F32)] * 6 + [pltpu.VMEM((nc, DK), F32)] * 2 + [pltpu.VMEM((tl, DK), F32)] * 3) * 2,
        compiler_params=_cp("arbitrary"),
    )(P, P, P, P, do, sf, P, P, P, P, do, sb, gcat, gbias)


def mix_bwd2(dgb, dcc, dgo, gl, P, conv_a, gcat, gbias, w_in, x, dres, g1, tl):
    L = P.shape[0]
    nt = L // tl

    def body(dgbr, dccr, dccp, dccn, dgor, dqf, dkf, dvf, daf, dqb, dkb, dvb, dab, gc, gv, lr, ca, gcr, bsr, wi,
             xr, drr, g1r, dP, dx, dg1, dgcat, dbias):
        i = pl.program_id(0)

        @pl.when(i == 0)
        def _():
            dg1[...] = jnp.zeros_like(dg1)
            dgcat[...] = jnp.zeros_like(dgcat)
            dbias[...] = jnp.zeros_like(dbias)

        p, n = _halo_rows(dccp, dccn, i, nt - 1)
        dc = _conv3_t(dccr[...].astype(F32), p, n, ca)
        pre = _mm(lr[...], gcr[...]) + bsr[...]
        da = jnp.concatenate([daf[...], dab[...]], axis=1).astype(F32)
        add32 = lambda a, b: a[...].astype(F32) + b[...].astype(F32)
        dpre = da * GATE_NORM * (1.0 - _sigmoid(pre))
        dpre16 = dpre.astype(BF16)
        dP[:, 0:DC] = dgbr[...].astype(BF16)
        dP[:, DC:2 * DC] = (dc * gv[...].astype(F32)).astype(BF16)
        dP[:, 2 * DC:3 * DC] = (dc * gc[...].astype(F32)).astype(BF16)
        dP[:, 1536:1792] = add32(dqf, dqb).astype(BF16)
        dP[:, 1792:2048] = add32(dkf, dkb).astype(BF16)
        dP[:, 2048:2560] = add32(dvf, dvb).astype(BF16)
        dP[:, 2560:3072] = dgor[...].astype(BF16)
        dP[:, 3072:3200] = _mm_nt(dpre16, gcr[...]).astype(BF16)
        dgcat[...] += _mm_tn(lr[...], dpre16)
        dbias[0:1, :] += jnp.sum(dpre, axis=0, keepdims=True)
        dh, dg = _rms_bwd(_mm_nt(dP[...], wi[...]), xr[...], g1r[...])
        dx[...] = drr[...] + dh
        dg1[...] += dg

    row = lambda n: pl.BlockSpec((tl, n), lambda i: (i, 0))
    t = lambda w, cb: pl.BlockSpec((tl, w), lambda i: (i, cb))
    full = lambda a: pl.BlockSpec(a.shape, lambda i: (0, 0))
    return pl.pallas_call(
        body, name="mix_bwd2", grid=(nt,),
        in_specs=[row(DC), row(DC), pl.BlockSpec((HALO16, DC), _prev_blk(tl, 0, HALO16)),
                  pl.BlockSpec((HALO16, DC), _next_blk(tl, L, 0, HALO16)),
                  row(DG), row(DK), row(DK), row(DG), row(DK), row(DK), row(DK), row(DG), row(DK),
                  t(DC, CB_GC), t(DC, CB_GV), t(LRW, CB_LR), full(conv_a), full(gcat), full(gbias),
                  pl.BlockSpec(w_in.shape, lambda i: (0, 0), pipeline_mode=pl.Buffered(1)),
                  row(D), row(D), full(g1)],
        out_specs=[row(DINP), row(D), pl.BlockSpec((1, D), lambda i: (0, 0)), pl.BlockSpec((LRW, 2 * DK), lambda i: (0, 0)),
                   pl.BlockSpec((8, 2 * DK), lambda i: (0, 0))],
        out_shape=[jax.ShapeDtypeStruct((L, DINP), BF16), jax.ShapeDtypeStruct((L, D), F32),
                   jax.ShapeDtypeStruct((1, D), F32), jax.ShapeDtypeStruct((LRW, 2 * DK), F32),
                   jax.ShapeDtypeStruct((8, 2 * DK), F32)],
        compiler_params=_cp("arbitrary"),
    )(dgb, dcc, dcc, dcc, dgo, *gl, P, P, P, conv_a, gcat, gbias, w_in, x, dres, g1)


def _row_tile(rows, cols):
    if rows * cols * 4 <= 2 * 1024 * 1024:
        return rows
    best = 8
    for t in range(8, rows, 8):
        if rows % t == 0 and t * cols * 4 <= 2 * 1024 * 1024:
            best = t
    return best


def adamw(w, g, m, v, name):
    shape = w.shape
    cols = shape[-1]
    w2, g2, m2, v2 = (a.reshape(-1, cols) for a in (w, g, m, v))
    rows = w2.shape[0]
    tr = _row_tile(rows, cols)

    def body(wr, gr, mr, vr, dl, nm, nv):
        gg = gr[...]
        mm = B1 * mr[...] + (1.0 - B1) * gg
        vv = B2 * vr[...] + (1.0 - B2) * (gg * gg)
        m_hat = mm / (1.0 - B1 ** STEP)
        v_hat = vv / (1.0 - B2 ** STEP)
        dl[...] = -LR * (m_hat / (jnp.sqrt(v_hat) + AEPS) + WD * wr[...])
        nm[...] = mm
        nv[...] = vv

    blk = pl.BlockSpec((tr, cols), lambda i: (i, 0))
    o = jax.ShapeDtypeStruct((rows, cols), F32)
    d, nm, nv = pl.pallas_call(
        body, name=name, grid=(rows // tr,), in_specs=[blk] * 4, out_specs=[blk] * 3, out_shape=[o, o, o],
        compiler_params=_cp("parallel"),
    )(w2, g2, m2, v2)
    return d.reshape(shape), nm.reshape(shape), nv.reshape(shape)


def _place():
    return lax.axis_index("x"), lax.axis_index("y"), lax.axis_index("c")


def allgather8(v, name):
    mp, n = v.shape

    def body(x_ref, out_ref, send_sems, recv_sems, local_sem):
        x, y, c = _place()
        me, sibling = (x, y, c), (x, y, 1 - c)
        chips = [(1 - x, y), (x, 1 - y), (1 - x, 1 - y)]

        def rows(px, py, pc):
            return out_ref.at[pl.ds((4 * px + 2 * py + pc) * mp, mp), :]

        def copy(k, block, to, src=None):
            return pltpu.make_async_remote_copy(
                src_ref=rows(*block) if src is None else src, dst_ref=rows(*block),
                send_sem=send_sems.at[k], recv_sem=recv_sems.at[k], device_id=to, device_id_type=MESH)

        mine = pltpu.make_async_copy(x_ref, rows(*me), local_sem)
        mine.start()
        first = [copy(0, me, sibling, src=x_ref)]
        first += [copy(1 + j, me, (*chip, c), src=x_ref) for j, chip in enumerate(chips)]
        for cp in first:
            cp.start()
        passed = [copy(4 + j, (*chip, c), sibling) for j, chip in enumerate(chips)]
        for j, chip in enumerate(chips):
            copy(1 + j, (*chip, c), me).wait_recv()
            passed[j].start()
        copy(0, sibling, me).wait_recv()
        for j, chip in enumerate(chips):
            copy(4 + j, (*chip, 1 - c), me).wait_recv()
        for cp in first + passed:
            cp.wait_send()
        mine.wait()

    return pl.pallas_call(
        body, name=name, out_shape=jax.ShapeDtypeStruct((8 * mp, n), v.dtype),
        in_specs=[pl.BlockSpec(memory_space=pltpu.VMEM)], out_specs=pl.BlockSpec(memory_space=pltpu.VMEM),
        scratch_shapes=[pltpu.SemaphoreType.DMA((7,)), pltpu.SemaphoreType.DMA((7,)), pltpu.SemaphoreType.DMA],
        compiler_params=pltpu.CompilerParams(vmem_limit_bytes=VMEM_LIMIT),
    )(v)


def sum8(v, mp):
    def body(x_ref, o_ref):
        acc = x_ref[0:mp, :]
        for d in range(1, 8):
            acc = acc + x_ref[d * mp:(d + 1) * mp, :]
        o_ref[...] = acc

    return pl.pallas_call(body, name="sum8", out_shape=jax.ShapeDtypeStruct((mp, v.shape[1]), F32),
                          compiler_params=pltpu.CompilerParams(vmem_limit_bytes=VMEM_LIMIT))(v)


_ANY = pl.BlockSpec(memory_space=pl.ANY)


def _row_half(ref, lead, h):
    hr = ref.shape[-2] // 2
    return ref.at[(*lead, pl.ds(h * hr, hr), slice(None))]


def allgather_weights(slots):
    n = len(slots)

    def body(*refs):
        s_refs, o_refs, (send_sems, recv_sems) = refs[:n], refs[n:2 * n], refs[2 * n:]
        x, y, c = _place()
        me = 2 * x + y
        sibling = (x, y, 1 - c)
        chips = [(1 - x, y), (x, 1 - y), (1 - x, 1 - y)]

        def half(ref, slot, h):
            return _row_half(ref, (slot, slice(None)), h)

        def copy(k, src, dst, to):
            return pltpu.make_async_remote_copy(src_ref=src, dst_ref=dst, send_sem=send_sems.at[k],
                                                recv_sem=recv_sems.at[k], device_id=to, device_id_type=MESH)

        first = [copy(6 * a + k, half(s_refs[a], me, c), half(o_refs[a], me, c), (px, py, c))
                 for k, (px, py) in enumerate(chips) for a in range(n)]
        for cp in first:
            cp.start()
        passed = []
        for k, (px, py) in enumerate(chips):
            for a in range(n):
                got = half(o_refs[a], 2 * px + py, c)
                copy(6 * a + k, half(s_refs[a], me, c), got, (px, py, c)).wait_recv()
                cp = copy(6 * a + 3 + k, got, got, sibling)
                cp.start()
                passed.append(cp)
        for k, (px, py) in enumerate(chips):
            for a in range(n):
                got = half(o_refs[a], 2 * px + py, 1 - c)
                copy(6 * a + 3 + k, got, got, sibling).wait_recv()
        for cp in first + passed:
            cp.wait_send()

    return pl.pallas_call(
        body, name="allgather_weights", out_shape=[jax.ShapeDtypeStruct(s.shape, s.dtype) for s in slots],
        in_specs=[_ANY] * n, out_specs=[_ANY] * n, input_output_aliases={a: a for a in range(n)},
        scratch_shapes=[pltpu.SemaphoreType.DMA((6 * n,)), pltpu.SemaphoreType.DMA((6 * n,))],
    )(*slots)


_HBM = pl.BlockSpec(memory_space=pltpu.HBM)
_SEM = pl.BlockSpec(memory_space=pltpu.SEMAPHORE)
_EFFECT = pltpu.SideEffectType.DATAFLOW_SIDE_EFFECTING


def gather_start(slots, name, after=()):
    n = len(slots)
    na = len(after)

    def body(*refs):
        s_refs, send_sems, recv_sems, token = refs[:n], refs[n + na], refs[n + na + 1], refs[-1]
        x, y, c = _place()
        me = 2 * x + y
        for k, (px, py) in enumerate([(1 - x, y), (x, 1 - y), (1 - x, 1 - y)]):
            for a in range(n):
                pltpu.make_async_remote_copy(
                    src_ref=s_refs[a].at[me], dst_ref=s_refs[a].at[me], send_sem=send_sems.at[3 * a + k],
                    recv_sem=recv_sems.at[3 * a + k], device_id=(px, py, c), device_id_type=MESH).start()
        token[...] = jnp.zeros_like(token)

    out = pl.pallas_call(
        body, name=name,
        out_shape=(pltpu.SemaphoreType.DMA((3 * n,)), pltpu.SemaphoreType.DMA((3 * n,)),
                   *[pltpu.HBM(s.shape, s.dtype) for s in slots], jax.ShapeDtypeStruct((8, 128), F32)),
        in_specs=[_HBM] * n + [_ANY] * na, out_specs=(_SEM, _SEM, *[_HBM] * n, pl.BlockSpec(memory_space=pltpu.VMEM)),
        input_output_aliases={a: 2 + a for a in range(n)},
        compiler_params=pltpu.CompilerParams(has_side_effects=_EFFECT),
    )(*[pltpu.with_memory_space_constraint(s, pltpu.HBM) for s in slots], *after)
    return out[0], out[1], list(out[2:2 + n]), out[-1]


def gather_wait(send_sems, recv_sems, slots, after, name):
    n = len(slots)

    def body(*refs):
        s_refs, ssem, rsem = refs[:n], refs[n], refs[n + 1]
        x, y, c = _place()
        me = 2 * x + y
        for k, (px, py) in enumerate([(1 - x, y), (x, 1 - y), (1 - x, 1 - y)]):
            for a in range(n):
                cp = pltpu.make_async_remote_copy(
                    src_ref=s_refs[a].at[me], dst_ref=s_refs[a].at[2 * px + py], send_sem=ssem.at[3 * a + k],
                    recv_sem=rsem.at[3 * a + k], device_id=(px, py, c), device_id_type=MESH)
                cp.wait_send()
                cp.wait_recv()

    return pl.pallas_call(
        body, name=name, out_shape=[pltpu.HBM(s.shape, s.dtype) for s in slots],
        in_specs=[_HBM] * n + [_SEM, _SEM, _ANY], out_specs=[_HBM] * n,
        input_output_aliases={a: a for a in range(n)},
        compiler_params=pltpu.CompilerParams(has_side_effects=_EFFECT),
    )(*slots, send_sems, recv_sems, after)


def rs_chipsum16(g, recv1, cidx, name):
    nl, hr, cols = recv1.shape[1:]

    def body(c_ref, g_ref, r_ref, o_ref):
        o_ref[...] = (g_ref[...] + r_ref[...]).astype(BF16)

    blk = (1, 1, hr, cols)
    return pl.pallas_call(
        body, name=name, out_shape=jax.ShapeDtypeStruct(recv1.shape, BF16),
        grid_spec=pltpu.PrefetchScalarGridSpec(
            num_scalar_prefetch=1, grid=(4, nl),
            in_specs=[pl.BlockSpec(blk, lambda j, l, c: (j, l, c[0], 0)), pl.BlockSpec(blk, lambda j, l, c: (j, l, 0, 0))],
            out_specs=pl.BlockSpec(blk, lambda j, l, c: (j, l, 0, 0))),
        compiler_params=_cp("parallel", "parallel"),
    )(cidx, g, recv1)


def sibling_start(gs, name):
    n = len(gs)
    lands = [lax.empty((*g.shape[:2], g.shape[2] // 2, g.shape[3]), F32) for g in gs]

    def body(*refs):
        g_refs, l_refs, send_sems, recv_sems, token = refs[:n], refs[n:2 * n], refs[2 * n], refs[2 * n + 1], refs[-1]
        x, y, c = _place()
        for a in range(n):
            pltpu.make_async_remote_copy(
                src_ref=_row_half(g_refs[a], (slice(None), slice(None)), 1 - c), dst_ref=l_refs[a],
                send_sem=send_sems.at[a], recv_sem=recv_sems.at[a], device_id=(x, y, 1 - c), device_id_type=MESH).start()
        token[...] = jnp.zeros_like(token)

    bufs = list(gs) + lands
    out = pl.pallas_call(
        body, name=name,
        out_shape=(pltpu.SemaphoreType.DMA((n,)), pltpu.SemaphoreType.DMA((n,)),
                   *[pltpu.HBM(b.shape, b.dtype) for b in bufs], jax.ShapeDtypeStruct((8, 128), F32)),
        in_specs=[_HBM] * (2 * n), out_specs=(_SEM, _SEM, *[_HBM] * (2 * n), pl.BlockSpec(memory_space=pltpu.VMEM)),
        input_output_aliases={i: 2 + i for i in range(2 * n)},
        compiler_params=pltpu.CompilerParams(has_side_effects=_EFFECT),
    )(*[pltpu.with_memory_space_constraint(b, pltpu.HBM) for b in bufs])
    return out[0], out[1], list(out[2:2 + n]), list(out[2 + n:2 + 2 * n]), out[-1]


def sibling_wait(send_sems, recv_sems, gs, lands, after, name):
    n = len(gs)

    def body(*refs):
        g_refs, l_refs, ssem, rsem = refs[:n], refs[n:2 * n], refs[2 * n], refs[2 * n + 1]
        x, y, c = _place()
        for a in range(n):
            cp = pltpu.make_async_remote_copy(
                src_ref=_row_half(g_refs[a], (slice(None), slice(None)), 1 - c), dst_ref=l_refs[a],
                send_sem=ssem.at[a], recv_sem=rsem.at[a], device_id=(x, y, 1 - c), device_id_type=MESH)
            cp.wait_send()
            cp.wait_recv()

    bufs = list(gs) + list(lands)
    out = pl.pallas_call(
        body, name=name, out_shape=[pltpu.HBM(b.shape, b.dtype) for b in bufs],
        in_specs=[_HBM] * (2 * n) + [_SEM, _SEM, _ANY], out_specs=[_HBM] * (2 * n),
        input_output_aliases={i: i for i in range(2 * n)},
        compiler_params=pltpu.CompilerParams(has_side_effects=_EFFECT),
    )(*bufs, send_sems, recv_sems, after)
    return list(out[:n]), list(out[n:])


def exchange_start(cs, name):
    n = len(cs)
    lands = [lax.empty((3, *c.shape[1:]), BF16) for c in cs]

    def body(*refs):
        s_refs, l_refs, send_sems, recv_sems, token = refs[:n], refs[n:2 * n], refs[2 * n], refs[2 * n + 1], refs[-1]
        x, y, c = _place()
        for k, (px, py) in enumerate([(1 - x, y), (x, 1 - y), (1 - x, 1 - y)]):
            for a in range(n):
                pltpu.make_async_remote_copy(
                    src_ref=s_refs[a].at[2 * px + py], dst_ref=l_refs[a].at[k], send_sem=send_sems.at[3 * a + k],
                    recv_sem=recv_sems.at[3 * a + k], device_id=(px, py, c), device_id_type=MESH).start()
        token[...] = jnp.zeros_like(token)

    bufs = list(cs) + lands
    out = pl.pallas_call(
        body, name=name,
        out_shape=(pltpu.SemaphoreType.DMA((3 * n,)), pltpu.SemaphoreType.DMA((3 * n,)),
                   *[pltpu.HBM(b.shape, b.dtype) for b in bufs], jax.ShapeDtypeStruct((8, 128), F32)),
        in_specs=[_HBM] * (2 * n), out_specs=(_SEM, _SEM, *[_HBM] * (2 * n), pl.BlockSpec(memory_space=pltpu.VMEM)),
        input_output_aliases={i: 2 + i for i in range(2 * n)},
        compiler_params=pltpu.CompilerParams(has_side_effects=_EFFECT),
    )(*[pltpu.with_memory_space_constraint(b, pltpu.HBM) for b in bufs])
    return out[0], out[1], list(out[2:2 + n]), list(out[2 + n:2 + 2 * n]), out[-1]


def exchange_wait(send_sems, recv_sems, cs, lands, after, name):
    n = len(cs)

    def body(*refs):
        s_refs, l_refs, ssem, rsem = refs[:n], refs[n:2 * n], refs[2 * n], refs[2 * n + 1]
        x, y, c = _place()
        for k, (px, py) in enumerate([(1 - x, y), (x, 1 - y), (1 - x, 1 - y)]):
            for a in range(n):
                cp = pltpu.make_async_remote_copy(
                    src_ref=s_refs[a].at[2 * px + py], dst_ref=l_refs[a].at[k], send_sem=ssem.at[3 * a + k],
                    recv_sem=rsem.at[3 * a + k], device_id=(px, py, c), device_id_type=MESH)
                cp.wait_send()
                cp.wait_recv()

    bufs = list(cs) + list(lands)
    out = pl.pallas_call(
        body, name=name, out_shape=[pltpu.HBM(b.shape, b.dtype) for b in bufs],
        in_specs=[_HBM] * (2 * n) + [_SEM, _SEM, _ANY], out_specs=[_HBM] * (2 * n),
        input_output_aliases={i: i for i in range(2 * n)},
        compiler_params=pltpu.CompilerParams(has_side_effects=_EFFECT),
    )(*bufs, send_sems, recv_sems, after)
    return list(out[n:])


def rs_final_sum(g, recv1, recv2, idx, name):
    nl, hr, cols = recv1.shape[1:]

    def body(i_ref, g_ref, r1_ref, r2_ref, o_ref):
        acc = g_ref[0, 0] + r1_ref[0, 0]
        for k in range(3):
            acc = acc + r2_ref[k, 0].astype(F32)
        o_ref[0] = acc

    blk = (1, 1, hr, cols)
    return pl.pallas_call(
        body, name=name, out_shape=jax.ShapeDtypeStruct((nl, 2 * hr, cols), F32),
        grid_spec=pltpu.PrefetchScalarGridSpec(
            num_scalar_prefetch=1, grid=(nl,),
            in_specs=[pl.BlockSpec(blk, lambda l, ix: (ix[0], l, ix[1], 0)), pl.BlockSpec(blk, lambda l, ix: (ix[0], l, 0, 0)),
                      pl.BlockSpec((3, 1, hr, cols), lambda l, ix: (0, l, 0, 0))],
            out_specs=pl.BlockSpec((1, hr, cols), lambda l, ix: (l, ix[1], 0))),
        compiler_params=_cp("parallel"),
    )(idx, g, recv1, recv2)


def rs_share_halves(fulls):
    n = len(fulls)

    def body(*refs):
        h_refs, o_refs, (send_sems, recv_sems) = refs[:n], refs[n:2 * n], refs[2 * n:]
        x, y, c = _place()
        sibling = (x, y, 1 - c)

        def copy(a, h):
            return pltpu.make_async_remote_copy(
                src_ref=_row_half(h_refs[a], (slice(None),), h), dst_ref=_row_half(o_refs[a], (slice(None),), h),
                send_sem=send_sems.at[a], recv_sem=recv_sems.at[a], device_id=sibling, device_id_type=MESH)

        for a in range(n):
            copy(a, c).start()
        for a in range(n):
            copy(a, c).wait_send()
            copy(a, 1 - c).wait_recv()

    return pl.pallas_call(
        body, name="rs_share_halves", out_shape=[jax.ShapeDtypeStruct(f.shape, F32) for f in fulls],
        in_specs=[_ANY] * n, out_specs=[_ANY] * n, input_output_aliases={a: a for a in range(n)},
        scratch_shapes=[pltpu.SemaphoreType.DMA((n,)), pltpu.SemaphoreType.DMA((n,))],
    )(*fulls)


def _own_slot(shard, chip, dtype):
    return lax.dynamic_update_slice(lax.empty((4, *shard.shape), dtype), shard.astype(dtype)[None],
                                    (chip,) + (0,) * shard.ndim)


def kernel(x, norm_mix_pre, norm_mix_post, norm_ffn_pre, norm_ffn_post, w_in, conv_a, gate_up_fwd, gate_bias_fwd, gate_up_bwd, gate_bias_bwd, gla_head_norm, w_out, w_up, conv_ffn, w_down, loss_target, m_norm_mix_pre, m_norm_mix_post, m_norm_ffn_pre, m_norm_ffn_post, m_w_in, m_conv_a, m_gate_up_fwd, m_gate_bias_fwd, m_gate_up_bwd, m_gate_bias_bwd, m_gla_head_norm, m_w_out, m_w_up, m_conv_ffn, m_w_down, v_norm_mix_pre, v_norm_mix_post, v_norm_ffn_pre, v_norm_ffn_post, v_w_in, v_conv_a, v_gate_up_fwd, v_gate_bias_fwd, v_gate_up_bwd, v_gate_bias_bwd, v_gla_head_norm, v_w_out, v_w_up, v_conv_ffn, v_w_down):
    L = x.shape[1]
    xi, yi, ci = _place()
    chip = 2 * xi + yi
    tl_gla, tl_mix, tl_ffn = min(L, TL_GLA), min(L, TL_MIX), min(L, TL_FFN)

    big_w = (w_in, w_out, w_up, w_down)
    a_in0 = allgather_weights([_own_slot(w_in[0:1], chip, BF16)])[0][:, 0]
    started = []
    prev = (a_in0,)
    for l in range(DEPTH):
        ws = big_w[1:] if l == 0 else big_w
        started.append(gather_start([_own_slot(w[l], chip, BF16) for w in ws], f"gather_start_{l}", after=prev))
        prev = (started[-1][3],)
    tokens = [s[3] for s in started]

    def full_w_in(a_in):
        return jnp.pad(jnp.concatenate([a_in[j] for j in range(4)], axis=1), ((0, 0), (0, DINP - DIN)))

    small = jnp.concatenate([conv_a.reshape(-1), gate_up_fwd.reshape(-1), gate_up_bwd.reshape(-1), conv_ffn.reshape(-1)])
    ms = small.shape[0] // 128
    sg = allgather8(small.reshape(ms, 128), "allgather_small_weights").reshape(4, 2, ms * 128)[:, 0]

    def small_full(off, shape):
        n = shape[0] * shape[1] * shape[2]
        return jnp.concatenate([sg[j, off:off + n].reshape(shape) for j in range(4)], axis=2)

    o1 = DEPTH * 3 * 128
    o2 = o1 + DEPTH * RK * 64
    o3 = o2 + DEPTH * RK * 64
    conv_a_f = small_full(0, (DEPTH, 3, 128))
    gup_f = small_full(o1, (DEPTH, RK, 64))
    gup_b = small_full(o2, (DEPTH, RK, 64))
    conv_ffn_f = small_full(o3, (DEPTH, 3, 1408))

    def gcat_of(l):
        g = jnp.zeros((LRW, 2 * DK), F32)
        g = g.at[0:RK, 0:DK].set(gup_f[l]).at[RK:2 * RK, DK:2 * DK].set(gup_b[l])
        return g.astype(BF16)

    gcats = [gcat_of(l) for l in range(DEPTH)]
    gbiases = [jnp.concatenate([gate_bias_fwd[l], gate_bias_bwd[l]])[None, :] for l in range(DEPTH)]
    ghn4s = [jnp.tile(gla_head_norm[l], NH)[None, :] for l in range(DEPTH)]

    xc = x.reshape(L, D)
    saved = []
    W_in, W_out, W_up, W_down = [], [], [], []
    tl_row = min(L, TL_ROW)
    for l in range(DEPTH):
        ssem, rsem, bufs, _ = started[l]
        if l > 0:
            a_in, a_out, a_up, a_down = gather_wait(ssem, rsem, bufs, xc, f"gather_wait_{l}")
        W_in.append(full_w_in(a_in0 if l == 0 else a_in))
        P, h1 = rms_matmul(xc, norm_mix_pre[l][None, :], W_in[l], DINP, "proj_in", out_dtype=BF16, tm=TM_PROJ // 2,
                           after=tokens if l == 0 else ())
        o_f, o_b, sf, sb = gla_fwd(P, gcats[l], gbiases[l], tl_gla)
        if l == 0:
            a_out, a_up, a_down = gather_wait(ssem, rsem, bufs, o_f, "gather_wait_0")
        W_out.append(a_out.reshape(D, D))
        W_up.append(a_up)
        W_down.append(a_down.reshape(DFF, D))
        ycat, y, x1 = mix_out(P, o_f, o_b, conv_a_f[l], ghn4s[l], W_out[l], norm_mix_post[l][None, :], xc,
                              min(L, TL_MIX_OUT))
        U, h2, ug, uv, z, y2, x2 = ffn_fwd(x1, norm_ffn_pre[l][None, :], W_up[l], conv_ffn_f[l], W_down[l],
                                           norm_ffn_post[l][None, :], tl_ffn)
        saved.append(dict(x=xc, h1=h1, P=P, o_f=o_f, o_b=o_b, sf=sf, sb=sb, ycat=ycat, y=y, x1=x1, h2=h2, U=U, y2=y2,
                          ug=ug, uv=uv, z=z))
        xc = x2

    dx, loss_blk = loss_head(xc, loss_target.reshape(L, D), tl_row)

    big = ("w_in", "w_out", "w_up", "w_down")
    cidx = jnp.reshape(ci, (1,)).astype(jnp.int32)
    idx = jnp.stack([chip, ci]).astype(jnp.int32)
    grads = [None] * DEPTH
    reduced = [dict() for _ in range(DEPTH)]
    tl_dw = min(L, 1024)
    groups = dict(ffn=("w_up", "w_down"), mix=("w_in", "w_out"))
    state = {grp: dict(flight=None, sibling=None) for grp in groups}
    token = ()

    def finish(grp, after):
        lp, gs_p, recv1_p, (ssem, rsem, cs_thru, lands, _) = state[grp]["flight"]
        recv2 = exchange_wait(ssem, rsem, cs_thru, lands, after, f"exchange_wait_{grp}_{lp}")
        halves = [rs_final_sum(g, r1, r2, idx, "rs_final_sum_" + k)
                  for g, r1, r2, k in zip(gs_p, recv1_p, recv2, groups[grp])]
        reduced[lp].update(zip(groups[grp], rs_share_halves(halves)))

    def advance(grp, after):
        st = state[grp]
        ls, (ssem, rsem, gs_thru, lands, _) = st["sibling"]
        gs_s, recv1 = sibling_wait(ssem, rsem, gs_thru, lands, after, f"sibling_wait_{grp}_{ls}")
        cs16 = [rs_chipsum16(g, r, cidx, "rs_chipsum16_" + k) for g, r, k in zip(gs_s, recv1, groups[grp])]
        flight = exchange_start(cs16, f"exchange_start_{grp}_{ls}")
        if st["flight"] is not None:
            finish(grp, flight[4])
        st["flight"] = (ls, gs_s, recv1, flight)
        st["sibling"] = None
        return flight[4]

    for l in reversed(range(DEPTH)):
        s = saved[l]
        dy2, dg4 = rms_bwd_pre(dx, s["y2"], norm_ffn_post[l][None, :], tl_row, after=token)
        g_down = matmul_tn(s["z"], dy2, DFF // 2, D, tl_dw, "dw_down").reshape(4, 1, DFF // 4, D)
        token2 = (advance("mix", g_down),) if state["mix"]["sibling"] is not None else ()
        dU_g, dU_v, dx1, dg3, dcf_g, dcf_v = ffn_bwd(dy2, s["ug"], s["uv"], s["U"], conv_ffn_f[l], W_down[l], W_up[l],
                                                     s["x1"], dx, norm_ffn_pre[l][None, :], tl_ffn)
        g_up = matmul_tn(s["h2"], dU_g, D, WFF, tl_dw, "dw_up_gate",
                         into=(lax.empty((4, 1, D, WFF), F32), (None, None, D, WFF), lambda p, q: (q, 0, 0, 0)))
        g_up = matmul_tn(s["h2"], dU_v, D, WFF, tl_dw, "dw_up_val",
                         into=(g_up, (None, None, D, WFF), lambda p, q: (NFF + q, 0, 0, 0)))
        sib = sibling_start([g_up, g_down], f"sibling_start_ffn_{l}")
        state["ffn"]["sibling"] = (l, sib)
        dgb, dcc, dgo, do, dca, dghn, dg2, dW_out = mix_bwd1(
            dx1, s["y"], norm_mix_post[l][None, :], s["ycat"], W_out[l], s["P"], s["o_f"], s["o_b"], conv_a_f[l],
            ghn4s[l], min(L, TL_MIX_OUT), after=token2 + (sib[4],))
        g_out = dW_out.reshape(4, 1, D // 4, D)
        token3 = advance("ffn", dgb)
        gl = gla_bwd(s["P"], do, s["sf"], s["sb"], gcats[l], gbiases[l], tl_gla)
        dP, dx, dg1, dgcat, dbias = mix_bwd2(dgb, dcc, dgo, gl, s["P"], conv_a_f[l], gcats[l], gbiases[l], W_in[l],
                                             s["x"], dx1, norm_mix_pre[l][None, :], tl_mix)
        dW_in = matmul_tn(s["h1"], dP, D // 2, DINP, tl_dw, "dw_in", after=(token3,))
        g_in = jnp.stack([dW_in[:, (DIN // 4) * j:(DIN // 4) * (j + 1)] for j in range(4)])[:, None]
        grads[l] = dict(
            norm_mix_pre=dg1[0], norm_mix_post=dg2[0], norm_ffn_pre=dg3[0], norm_ffn_post=dg4[0],
            conv_a=dca[0:3], gate_up_fwd=dgcat[0:RK, 0:DK], gate_bias_fwd=dbias[0, 0:DK],
            gate_up_bwd=dgcat[RK:2 * RK, DK:2 * DK], gate_bias_bwd=dbias[0, DK:2 * DK], gla_head_norm=dghn[0],
            conv_ffn=jnp.concatenate([dcf_g[0:3], dcf_v[0:3]], axis=1))
        sib = sibling_start([g_in, g_out], f"sibling_start_mix_{l}")
        state["mix"]["sibling"] = (l, sib)
        token = (sib[4],)
    last = advance("mix", token[0])
    finish("ffn", last)
    finish("mix", last)

    G = {k: jnp.stack([grads[l][k] for l in range(DEPTH)]) for k in grads[0]}

    small_names = ["norm_mix_pre", "norm_mix_post", "norm_ffn_pre", "norm_ffn_post", "conv_a", "gate_up_fwd",
                   "gate_bias_fwd", "gate_up_bwd", "gate_bias_bwd", "gla_head_norm", "conv_ffn"]
    flat = jnp.concatenate([G[k].reshape(-1) for k in small_names] + [loss_blk[0, 0:1]])
    n_small = flat.shape[0]
    mp = -(-n_small // 1024) * 8
    flat = jnp.pad(flat, (0, mp * 128 - n_small)).reshape(mp, 128)
    tot = sum8(allgather8(flat, "allgather_small_grads"), mp).reshape(-1)
    gsm = {}
    o = 0
    for k in small_names:
        n = G[k].size
        gsm[k] = tot[o:o + n].reshape(G[k].shape)
        o += n
    loss = tot[o]

    def my_cols(a, width):
        return lax.dynamic_slice_in_dim(a, chip * width, width, axis=2)

    gsm["conv_a"] = my_cols(gsm["conv_a"], 128)
    gsm["gate_up_fwd"] = my_cols(gsm["gate_up_fwd"], 64)
    gsm["gate_up_bwd"] = my_cols(gsm["gate_up_bwd"], 64)
    gsm["conv_ffn"] = my_cols(gsm["conv_ffn"], 1408)

    for k in big:
        gsm[k] = jnp.concatenate([reduced[l][k] for l in range(DEPTH)], axis=0)

    names = ["norm_mix_pre", "norm_mix_post", "norm_ffn_pre", "norm_ffn_post", "w_in", "conv_a", "gate_up_fwd",
             "gate_bias_fwd", "gate_up_bwd", "gate_bias_bwd", "gla_head_norm", "w_out", "w_up", "conv_ffn", "w_down"]
    w = dict(norm_mix_pre=norm_mix_pre, norm_mix_post=norm_mix_post, norm_ffn_pre=norm_ffn_pre, norm_ffn_post=norm_ffn_post,
             w_in=w_in, conv_a=conv_a, gate_up_fwd=gate_up_fwd, gate_bias_fwd=gate_bias_fwd, gate_up_bwd=gate_up_bwd,
             gate_bias_bwd=gate_bias_bwd, gla_head_norm=gla_head_norm, w_out=w_out, w_up=w_up, conv_ffn=conv_ffn, w_down=w_down)
    m = dict(norm_mix_pre=m_norm_mix_pre, norm_mix_post=m_norm_mix_post, norm_ffn_pre=m_norm_ffn_pre, norm_ffn_post=m_norm_ffn_post,
             w_in=m_w_in, conv_a=m_conv_a, gate_up_fwd=m_gate_up_fwd, gate_bias_fwd=m_gate_bias_fwd, gate_up_bwd=m_gate_up_bwd,
             gate_bias_bwd=m_gate_bias_bwd, gla_head_norm=m_gla_head_norm, w_out=m_w_out, w_up=m_w_up, conv_ffn=m_conv_ffn, w_down=m_w_down)
    v = dict(norm_mix_pre=v_norm_mix_pre, norm_mix_post=v_norm_mix_post, norm_ffn_pre=v_norm_ffn_pre, norm_ffn_post=v_norm_ffn_post,
             w_in=v_w_in, conv_a=v_conv_a, gate_up_fwd=v_gate_up_fwd, gate_bias_fwd=v_gate_bias_fwd, gate_up_bwd=v_gate_up_bwd,
             gate_bias_bwd=v_gate_bias_bwd, gla_head_norm=v_gla_head_norm, w_out=v_w_out, w_up=v_w_up, conv_ffn=v_conv_ffn, w_down=v_w_down)
    upd = {k: adamw(w[k], gsm[k], m[k], v[k], "adamw_" + k) for k in names}
    return (loss, dx.reshape(1, L, D), *[gsm[k] for k in names], *[upd[k][0] for k in names],
            *[upd[k][1] for k in names], *[upd[k][2] for k in names])
```

```python
import functools

import jax
import jax.numpy as jnp
from jax import lax
from jax.experimental import pallas as pl
from jax.experimental.pallas import tpu as pltpu

F32 = jnp.float32
BF16 = jnp.bfloat16
MXU_DTYPE = jnp.bfloat16
MESH = pl.DeviceIdType.MESH

D = 1024
DC = 512
DG = 512
NH = 4
HV = 128
HK = 64
DK = 256
RK = 16
CH = 64
DFF = 2816
DIN = 3104
DINP = 3200
LRW = 128
DEPTH = 4
EPS = 1e-6
QSCALE = HK ** -0.5
GATE_NORM = 1.0 / 16.0
CB_GB, CB_GC, CB_GV, CB_GO = 0, 1, 2, 5
CB_Q, CB_K = 6, 7
CB_V = 4
CB_LR = 24
LR = 0.001
B1 = 0.9
B2 = 0.999
AEPS = 1e-08
WD = 0.01
STEP = 10
TM_PROJ = 1024
TL_GLA = 512
TL_MIX = 512
TL_MIX_OUT = 512
TL_ROW = 1024
TL_FFN = 256
VMEM_LIMIT = 56 * 1024 * 1024


def _cp(*sem):
    return pltpu.CompilerParams(dimension_semantics=sem if sem else None, vmem_limit_bytes=VMEM_LIMIT)


def _mm(a, b):
    return jnp.dot(a.astype(MXU_DTYPE), b.astype(MXU_DTYPE), preferred_element_type=F32)


def _mm_nt(a, b):
    return lax.dot_general(a.astype(MXU_DTYPE), b.astype(MXU_DTYPE), (((1,), (1,)), ((), ())),
                           preferred_element_type=F32)


def _mm_tn(a, b):
    return lax.dot_general(a.astype(MXU_DTYPE), b.astype(MXU_DTYPE), (((0,), (0,)), ((), ())),
                           preferred_element_type=F32)


def _mm_tri(tri, b):
    t = tri.astype(BF16)
    b1 = b.astype(BF16)
    r1 = b - b1.astype(F32)
    b2 = r1.astype(BF16)
    b3 = (r1 - b2.astype(F32)).astype(BF16)
    dot = lambda u: jnp.dot(t, u, preferred_element_type=F32)
    return dot(b1) + dot(b2) + dot(b3)


def _rms(x, g):
    r = lax.rsqrt(jnp.mean(x * x, axis=-1, keepdims=True) + EPS)
    return x * r * g


def _rms_bwd(dout, y, g):
    r = lax.rsqrt(jnp.mean(y * y, axis=-1, keepdims=True) + EPS)
    yh = y * r
    dyh = dout * g
    dy = r * (dyh - yh * jnp.mean(dyh * yh, axis=-1, keepdims=True))
    dg = jnp.sum(dout * yh, axis=0, keepdims=True)
    return dy, dg


def _sigmoid(x):
    return 0.5 * jnp.tanh(0.5 * x) + 0.5


def _logsig(x):
    return jnp.minimum(x, 0.0) - jnp.log1p(jnp.exp(-jnp.abs(x)))


def _shifts(x, p8, n8):
    n = x.shape[0]
    xe = jnp.concatenate([p8, x, n8], axis=0)
    return pltpu.roll(xe, 1, 0)[8:8 + n], pltpu.roll(xe, n + 15, 0)[8:8 + n]


def _halo_rows(prev_ref, next_ref, i, last):
    hr = prev_ref.shape[0]
    p = jnp.where(i == 0, 0.0, prev_ref[...].astype(F32)[hr - 8:hr, :])
    n = jnp.where(i == last, 0.0, next_ref[...].astype(F32)[0:8, :])
    return p, n


def _conv3(x, xp, xn, w_ref):
    xm1, xp1 = _shifts(x, xp, xn)
    return w_ref[0:1, :] * xm1 + w_ref[1:2, :] * x + w_ref[2:3, :] * xp1, xm1, xp1


def _conv3_t(d, dp, dn, w_ref):
    dm1, dp1 = _shifts(d, dp, dn)
    return w_ref[0:1, :] * dp1 + w_ref[1:2, :] * d + w_ref[2:3, :] * dm1


HALO32 = 8
HALO16 = 16


def _prev_row_blk(i, tl, hr):
    return jnp.maximum(i * (tl // hr) - 1, 0)


def _next_row_blk(i, tl, nrows, hr):
    return jnp.minimum((i + 1) * (tl // hr), nrows // hr - 1)


def _prev_blk(tl, cb, hr=HALO32):
    return lambda i: (_prev_row_blk(i, tl, hr), cb)


def _next_blk(tl, nrows, cb, hr=HALO32):
    return lambda i: (_next_row_blk(i, tl, nrows, hr), cb)


def rms_matmul(x, g, w, tn, name, w_spec=None, n_out=None, out_dtype=F32, after=(), tm=TM_PROJ):
    L = x.shape[0]
    N = w.shape[1] if n_out is None else n_out
    tm = min(L, tm)
    if w_spec is None:
        w_spec = pl.BlockSpec((D, tn), lambda i, j: (0, j))

    def body(x_ref, g_ref, w_ref, *rest):
        o_ref, h_ref = rest[-2:]

        @pl.when(pl.program_id(1) == 0)
        def _():
            h_ref[...] = _rms(x_ref[...], g_ref[...]).astype(BF16)

        o_ref[...] = _mm(h_ref[...], w_ref[...]).astype(out_dtype)

    return pl.pallas_call(
        body, name=name, grid=(L // tm, N // tn),
        in_specs=[pl.BlockSpec((tm, D), lambda i, j: (i, 0)), pl.BlockSpec((1, D), lambda i, j: (0, 0)), w_spec]
        + [_ANY] * len(after),
        out_specs=[pl.BlockSpec((tm, tn), lambda i, j: (i, j)), pl.BlockSpec((tm, D), lambda i, j: (i, 0))],
        out_shape=[jax.ShapeDtypeStruct((L, N), out_dtype), jax.ShapeDtypeStruct((L, D), BF16)],
        compiler_params=_cp("parallel", "arbitrary"),
    )(x, g, w, *after)


def _gla_masks():
    def blk(shape, rdiv, cdiv):
        r = lax.broadcasted_iota(jnp.int32, shape, 0) // rdiv
        c = lax.broadcasted_iota(jnp.int32, shape, 1) // cdiv
        return (r == c).astype(F32)

    r = lax.broadcasted_iota(jnp.int32, (CH, CH), 0)
    c = lax.broadcasted_iota(jnp.int32, (CH, CH), 1)
    r4 = lax.broadcasted_iota(jnp.int32, (NH * CH, CH), 0) % CH
    c4 = lax.broadcasted_iota(jnp.int32, (NH * CH, CH), 1)
    return dict(
        bdq=blk((NH * CH, DK), CH, HK),
        bdo=blk((NH * CH, DG), CH, HV),
        bds=blk((DG, DK), HV, HK),
        tril=(r >= c).astype(F32), triu=(r <= c).astype(F32),
        tril4=r4 >= c4, triu4=r4 <= c4,
    )


def _tile4(x):
    return jnp.concatenate([x, x, x, x], axis=0)


def _gla_tile_prep(q, k, a, m, rev, nc):
    tri = m["triu"] if rev else m["tril"]
    chunks = [a[c * CH:(c + 1) * CH] for c in range(nc)]
    cum = jnp.concatenate([_mm_tri(tri, ac) for ac in chunks], axis=0)
    tot = jnp.concatenate([jnp.sum(ac, axis=0, keepdims=True) for ac in chunks], axis=0)
    tot_rows = jnp.concatenate([jnp.broadcast_to(tot[c:c + 1], (CH, DK)) for c in range(nc)], axis=0)
    e = jnp.exp(cum)
    einv = jnp.exp(-cum)
    eout = jnp.exp(tot_rows - cum)
    q, k = q.astype(F32), k.astype(F32)
    return dict(e=e, einv=einv, eout=eout, dec=jnp.exp(tot), qt=q * QSCALE * e, kt=k * einv, kh=k * eout)


def _gla_scores(qt16, kt16, m, rev):
    qs = _tile4(qt16) * m["bdq"].astype(qt16.dtype)
    return qs, jnp.where(m["triu4"] if rev else m["tril4"], _mm_nt(qs, kt16), 0.0)


def _gla_chunk_fwd(qt16, kt16, kh16, v, dec, st_ref, m, rev):
    _, sc = _gla_scores(qt16, kt16, m, rev)
    v16 = v.astype(BF16)
    r = _mm(sc, v16)
    o_intra = jnp.concatenate([r[h * CH:(h + 1) * CH, h * HV:(h + 1) * HV] for h in range(NH)], axis=1)
    st = st_ref[...]
    st16 = st.astype(BF16)
    o = o_intra + _mm_nt(qt16, st16)
    st_ref[...] = st * dec + _mm_tn(v16, kh16) * m["bds"]
    return o, st16


def _gates(lr_ref, gc_ref, bs_ref, cols):
    return _logsig(_mm(lr_ref[...], gc_ref[:, cols]) + bs_ref[:, cols]) * GATE_NORM


def gla_fwd(P, gcat, gbias, tl):
    L = P.shape[0]
    nb = L // tl
    nc = tl // CH

    def body(qf, kf, vf, lf, qb, kb, vb, lb, gc_ref, bs_ref, of, ob, sf, sb, stf, stb,
             qtf, ktf, khf, dcf, qtb, ktb, khb, dcb):
        @pl.when(pl.program_id(0) == 0)
        def _():
            stf[...] = jnp.zeros_like(stf)
            stb[...] = jnp.zeros_like(stb)

        m = _gla_masks()
        for (q, k, lr, cols, rev, qt, kt, kh, dc) in ((qf, kf, lf, slice(0, DK), False, qtf, ktf, khf, dcf),
                                                      (qb, kb, lb, slice(DK, 2 * DK), True, qtb, ktb, khb, dcb)):
            p = _gla_tile_prep(q[...], k[...], _gates(lr, gc_ref, bs_ref, cols), m, rev, nc)
            qt[...] = p["qt"].astype(BF16)
            kt[...] = p["kt"].astype(BF16)
            kh[...] = p["kh"].astype(BF16)
            dc[...] = p["dec"]

        def chunk(c, carry):
            rows = pl.ds(pl.multiple_of(c * CH, CH), CH)
            o, st = _gla_chunk_fwd(qtf[rows, :], ktf[rows, :], khf[rows, :], vf[rows, :], dcf[pl.ds(c, 1), :], stf, m, False)
            of[rows, :] = o.astype(BF16)
            sf[c] = st
            cb = nc - 1 - c
            rows = pl.ds(pl.multiple_of(cb * CH, CH), CH)
            o, st = _gla_chunk_fwd(qtb[rows, :], ktb[rows, :], khb[rows, :], vb[rows, :], dcb[pl.ds(cb, 1), :], stb, m, True)
            ob[rows, :] = o.astype(BF16)
            sb[cb] = st
            return carry

        lax.fori_loop(0, nc, chunk, 0, unroll=2)

    fw = lambda cb: (lambda i: (i, cb))
    bw = lambda cb: (lambda i: (nb - 1 - i, cb))
    return pl.pallas_call(
        body, name="gla_fwd", grid=(nb,),
        in_specs=[pl.BlockSpec((tl, DK), fw(CB_Q)), pl.BlockSpec((tl, DK), fw(CB_K)), pl.BlockSpec((tl, DG), fw(CB_V)),
                  pl.BlockSpec((tl, LRW), fw(CB_LR)),
                  pl.BlockSpec((tl, DK), bw(CB_Q)), pl.BlockSpec((tl, DK), bw(CB_K)), pl.BlockSpec((tl, DG), bw(CB_V)),
                  pl.BlockSpec((tl, LRW), bw(CB_LR)),
                  pl.BlockSpec((LRW, 2 * DK), lambda i: (0, 0)), pl.BlockSpec((1, 2 * DK), lambda i: (0, 0))],
        out_specs=[pl.BlockSpec((tl, DG), lambda i: (i, 0)), pl.BlockSpec((tl, DG), lambda i: (nb - 1 - i, 0)),
                   pl.BlockSpec((nc, DG, DK), lambda i: (i, 0, 0)), pl.BlockSpec((nc, DG, DK), lambda i: (nb - 1 - i, 0, 0))],
        out_shape=[jax.ShapeDtypeStruct((L, DG), BF16), jax.ShapeDtypeStruct((L, DG), BF16),
                   jax.ShapeDtypeStruct((L // CH, DG, DK), BF16), jax.ShapeDtypeStruct((L // CH, DG, DK), BF16)],
        scratch_shapes=[pltpu.VMEM((DG, DK), F32), pltpu.VMEM((DG, DK), F32)]
        + [pltpu.VMEM((tl, DK), BF16)] * 3 + [pltpu.VMEM((nc, DK), F32)]
        + [pltpu.VMEM((tl, DK), BF16)] * 3 + [pltpu.VMEM((nc, DK), F32)],
        compiler_params=_cp("arbitrary"),
    )(P, P, P, P, P, P, P, P, gcat, gbias)


def _headnorm(o):
    oh, rs = [], []
    for h in range(NH):
        oo = o[:, h * HV:(h + 1) * HV]
        r = lax.rsqrt(jnp.mean(oo * oo, axis=-1, keepdims=True) + EPS)
        oh.append(oo * r)
        rs.append(r)
    return jnp.concatenate(oh, axis=1), rs


def mix_out(P, o_f, o_b, conv_a, ghn4, w_out, g2, x, tl):
    L = P.shape[0]
    nt = L // tl

    def body(gb, gc, gv, go, gcp, gvp, gcn, gvn, of, ob, ca, gh, wo, g2r, xr, ycat, yr, x1):
        i = pl.program_id(0)
        cp, cn = _halo_rows(gcp, gcn, i, nt - 1)
        vp, vn = _halo_rows(gvp, gvn, i, nt - 1)
        c = gc[...].astype(F32) * gv[...].astype(F32)
        cc, _, _ = _conv3(c, cp * vp, cn * vn, ca)
        ya = gb[...].astype(F32) * cc
        oh, _ = _headnorm(of[...].astype(F32) + ob[...].astype(F32))
        g = go[...].astype(F32)
        yb = g * _sigmoid(g) * (oh * gh[...])
        yc = jnp.concatenate([ya, yb], axis=1).astype(BF16)
        ycat[...] = yc
        y = _mm(yc, wo[...])
        yr[...] = y
        x1[...] = xr[...] + _rms(y, g2r[...])

    t = lambda cb: pl.BlockSpec((tl, DC), lambda i: (i, cb))
    hp = lambda cb: pl.BlockSpec((HALO16, DC), _prev_blk(tl, cb, HALO16))
    hn = lambda cb: pl.BlockSpec((HALO16, DC), _next_blk(tl, L, cb, HALO16))
    row = lambda n: pl.BlockSpec((tl, n), lambda i: (i, 0))
    full = lambda a: pl.BlockSpec(a.shape, lambda i: (0, 0))
    return pl.pallas_call(
        body, name="mix_out", grid=(nt,),
        in_specs=[t(CB_GB), t(CB_GC), t(CB_GV), t(CB_GO), hp(CB_GC), hp(CB_GV), hn(CB_GC), hn(CB_GV),
                  row(DG), row(DG), full(conv_a), full(ghn4), full(w_out), full(g2), row(D)],
        out_specs=[row(D), row(D), row(D)],
        out_shape=[jax.ShapeDtypeStruct((L, D), BF16), jax.ShapeDtypeStruct((L, D), F32),
                   jax.ShapeDtypeStruct((L, D), F32)],
        compiler_params=_cp("parallel"),
    )(P, P, P, P, P, P, P, P, o_f, o_b, conv_a, ghn4, w_out, g2, x)


NFF = 2
WFF = DFF // NFF
FFN_COL_CHUNKS = ((0, 512), (512, 1024), (1024, WFF))


def ffn_fwd(x1, g3, w_up, conv_ffn, w_down, g4, tl):
    L = x1.shape[0]
    nt = L // tl
    hh = HALO32

    def body(xr, xp, xn, g3r, wu, cf, wd, g4r, U, h2, ug, uv, zr, y2, x2):
        i = pl.program_id(0)
        he32 = _rms(jnp.concatenate([xp[...], xr[...], xn[...]], axis=0), g3r[...])
        he = he32.astype(BF16)
        h2[...] = he32[hh:hh + tl].astype(BF16)
        acc = jnp.zeros((tl, D), F32)
        for j in range(NFF):
            conv = []
            for blk, off in ((j, j * WFF), (NFF + j, DFF + j * WFF)):
                ue = _mm(he, wu[blk])
                p8 = jnp.where(i == 0, 0.0, ue[hh - 8:hh])
                n8 = jnp.where(i == nt - 1, 0.0, ue[hh + tl:hh + tl + 8])
                mid = ue[hh:hh + tl]
                U[:, off:off + WFF] = mid.astype(BF16)
                conv.append(_conv3(mid, p8, n8, cf.at[:, off:off + WFF])[0])
            gs = slice(j * WFF, (j + 1) * WFF)
            zz = (conv[0] * _sigmoid(conv[0]) * conv[1]).astype(BF16)
            ug[:, gs] = conv[0].astype(BF16)
            uv[:, gs] = conv[1].astype(BF16)
            zr[:, gs] = zz
            acc = acc + _mm(zz, wd[gs, :])
        y2[...] = acc
        x2[...] = xr[...] + _rms(acc, g4r[...])

    row = lambda n: pl.BlockSpec((tl, n), lambda i: (i, 0))
    full = lambda a: pl.BlockSpec(a.shape, lambda i: (0,) * a.ndim)
    once = lambda a: pl.BlockSpec(a.shape, lambda i: (0,) * a.ndim, pipeline_mode=pl.Buffered(1))
    half = jax.ShapeDtypeStruct((L, DFF), BF16)
    return pl.pallas_call(
        body, name="ffn_fwd", grid=(nt,),
        in_specs=[row(D), pl.BlockSpec((hh, D), lambda i: (_prev_row_blk(i, tl, hh), 0)),
                  pl.BlockSpec((hh, D), lambda i: (_next_row_blk(i, tl, L, hh), 0)),
                  full(g3), once(w_up), full(conv_ffn), once(w_down), full(g4)],
        out_specs=[row(2 * DFF), row(D), row(DFF), row(DFF), row(DFF), row(D), row(D)],
        out_shape=[jax.ShapeDtypeStruct((L, 2 * DFF), BF16), jax.ShapeDtypeStruct((L, D), BF16), half, half, half,
                   jax.ShapeDtypeStruct((L, D), F32), jax.ShapeDtypeStruct((L, D), F32)],
        compiler_params=_cp("parallel"),
    )(x1, x1, x1, g3, w_up, conv_ffn, w_down, g4)


def loss_head(y, target, tl):
    L = y.shape[0]

    def body(yr, tr, dy, ls):
        @pl.when(pl.program_id(0) == 0)
        def _():
            ls[...] = jnp.zeros_like(ls)

        err = yr[...] - tr[...]
        dy[...] = err * (1.0 / D)
        ls[...] += (0.5 / D) * jnp.sum(err * err)

    row = pl.BlockSpec((tl, D), lambda i: (i, 0))
    return pl.pallas_call(
        body, name="loss_head", grid=(L // tl,), in_specs=[row, row],
        out_specs=[row, pl.BlockSpec((8, 128), lambda i: (0, 0))],
        out_shape=[jax.ShapeDtypeStruct((L, D), F32), jax.ShapeDtypeStruct((8, 128), F32)],
        compiler_params=_cp("arbitrary"),
    )(y, target)


def rms_bwd_pre(dout, y, g, tl, after=()):
    L = y.shape[0]

    def body(dr, yr, gr, *rest):
        dy, dg = rest[-2:]

        @pl.when(pl.program_id(0) == 0)
        def _():
            dg[...] = jnp.zeros_like(dg)

        a, b = _rms_bwd(dr[...], yr[...], gr[...])
        dy[...] = a.astype(BF16)
        dg[...] += b

    row = pl.BlockSpec((tl, D), lambda i: (i, 0))
    vec = pl.BlockSpec((1, D), lambda i: (0, 0))
    return pl.pallas_call(
        body, name="rms_bwd_pre", grid=(L // tl,), in_specs=[row, row, vec] + [_ANY] * len(after), out_specs=[row, vec],
        out_shape=[jax.ShapeDtypeStruct((L, D), BF16), jax.ShapeDtypeStruct((1, D), F32)],
        compiler_params=_cp("arbitrary"),
    )(dout, y, g, *after)


def ffn_bwd(dy2, ug, uv, U, conv_ffn, w_down, w_up, x1, dres, g3, tl):
    L = x1.shape[0]
    nt = L // tl
    hh = HALO16

    def body(dyr, dyp, dyn, ugr, ugp, ugn, uvr, uvp, uvn, Ur, cf, wd, wu, x1r, drr, g3r, dUg, dUv, dx1, dg3, dcg, dcv):
        i = pl.program_id(0)

        @pl.when(i == 0)
        def _():
            dg3[...] = jnp.zeros_like(dg3)
            dcg[...] = jnp.zeros_like(dcg)
            dcv[...] = jnp.zeros_like(dcv)

        ext = lambda p, t, n, cs: jnp.concatenate([p[:, cs], t[:, cs], n[:, cs]], axis=0).astype(F32)
        dye = jnp.concatenate([dyp[...], dyr[...], dyn[...]], axis=0)
        acc = jnp.zeros((tl, D), F32)
        for j in range(NFF):
            dze = _mm_nt(dye, wd[j * WFF:(j + 1) * WFF, :])
            for c0, c1 in FFN_COL_CHUNKS:
                cs = slice(j * WFF + c0, j * WFF + c1)
                a = ext(ugp, ugr, ugn, cs)
                b = ext(uvp, uvr, uvn, cs)
                sg = _sigmoid(a)
                silu = a * sg
                dz = dze[:, c0:c1]
                for de, off, blk, dc, dU in ((dz * b * (sg + silu * (1.0 - sg)), 0, j, dcg, dUg),
                                            (dz * silu, DFF, NFF + j, dcv, dUv)):
                    d = de[hh:hh + tl]
                    p8 = jnp.where(i == 0, 0.0, de[hh - 8:hh])
                    n8 = jnp.where(i == nt - 1, 0.0, de[hh + tl:hh + tl + 8])
                    dm1, dp1 = _shifts(d, p8, n8)
                    wc = slice(off + j * WFF + c0, off + j * WFF + c1)
                    du = (cf[0:1, wc] * dp1 + cf[1:2, wc] * d + cf[2:3, wc] * dm1).astype(BF16)
                    dU[:, cs] = du
                    u = Ur[:, wc].astype(F32)
                    for k, t in enumerate((dp1, d, dm1)):
                        dc[k:k + 1, cs] += jnp.sum(t * u, axis=0, keepdims=True)
                    acc = acc + _mm_nt(du, wu[blk, :, c0:c1])
        dx, dg = _rms_bwd(acc, x1r[...], g3r[...])
        dx1[...] = drr[...] + dx
        dg3[...] += dg

    row = lambda n: pl.BlockSpec((tl, n), lambda i: (i, 0))
    prev = lambda n: pl.BlockSpec((hh, n), lambda i: (_prev_row_blk(i, tl, hh), 0))
    nxt = lambda n: pl.BlockSpec((hh, n), lambda i: (_next_row_blk(i, tl, L, hh), 0))
    full = lambda a: pl.BlockSpec(a.shape, lambda i: (0,) * a.ndim)
    once = lambda a: pl.BlockSpec(a.shape, lambda i: (0,) * a.ndim, pipeline_mode=pl.Buffered(1))
    half = jax.ShapeDtypeStruct((L, DFF), BF16)
    dcs = pl.BlockSpec((8, DFF), lambda i: (0, 0))
    return pl.pallas_call(
        body, name="ffn_bwd", grid=(nt,),
        in_specs=[row(D), prev(D), nxt(D), row(DFF), prev(DFF), nxt(DFF), row(DFF), prev(DFF), nxt(DFF), row(2 * DFF),
                  full(conv_ffn), once(w_down), once(w_up), row(D), row(D), full(g3)],
        out_specs=[row(DFF), row(DFF), row(D), pl.BlockSpec((1, D), lambda i: (0, 0)), dcs, dcs],
        out_shape=[half, half, jax.ShapeDtypeStruct((L, D), F32), jax.ShapeDtypeStruct((1, D), F32),
                   jax.ShapeDtypeStruct((8, DFF), F32), jax.ShapeDtypeStruct((8, DFF), F32)],
        compiler_params=_cp("arbitrary"),
    )(dy2, dy2, dy2, ug, ug, ug, uv, uv, uv, U, conv_ffn, w_down, w_up, x1, dres, g3)


def matmul_tn(a, b, ta, tn, tl, name, into=None, after=()):
    L, Ka = a.shape
    N = b.shape[1]

    def body(ar, br, *rest):
        o = rest[-1]

        @pl.when(pl.program_id(2) == 0)
        def _():
            o[...] = jnp.zeros_like(o)

        o[...] += _mm_tn(ar[...], br[...]).reshape(o.shape)

    in_specs = [pl.BlockSpec((tl, ta), lambda p, q, l: (l, p)), pl.BlockSpec((tl, tn), lambda p, q, l: (l, q))]
    if into is None:
        return pl.pallas_call(
            body, name=name, grid=(Ka // ta, N // tn, L // tl), in_specs=in_specs + [_ANY] * len(after),
            out_specs=pl.BlockSpec((ta, tn), lambda p, q, l: (p, q)),
            out_shape=jax.ShapeDtypeStruct((Ka, N), F32),
            compiler_params=_cp("parallel", "parallel", "arbitrary"),
        )(a, b, *after)
    buf, blk, idx = into
    return pl.pallas_call(
        body, name=name, grid=(Ka // ta, N // tn, L // tl), in_specs=in_specs + [_ANY],
        out_specs=pl.BlockSpec(blk, lambda p, q, l: idx(p, q)),
        out_shape=jax.ShapeDtypeStruct(buf.shape, F32), input_output_aliases={2: 0},
        compiler_params=_cp("parallel", "parallel", "arbitrary"),
    )(a, b, buf)


def mix_bwd1(dx1, y, g2, ycat, w_out, P, o_f, o_b, conv_a, ghn4, tl, after=()):
    L = P.shape[0]
    nt = L // tl
    na = len(after)

    def body(dxr, yr, g2r, ycr, wo, gb, gc, gv, go, gcp, gvp, gcn, gvn, of, ob, ca, gh, *rest):
        dgb, dcc, dgo, do, dca, dgh, dg2, dwo = rest[na:]
        i = pl.program_id(0)

        @pl.when(i == 0)
        def _():
            dca[...] = jnp.zeros_like(dca)
            dgh[...] = jnp.zeros_like(dgh)
            dg2[...] = jnp.zeros_like(dg2)
            dwo[...] = jnp.zeros_like(dwo)

        dyv, dg = _rms_bwd(dxr[...], yr[...], g2r[...])
        dy16 = dyv.astype(BF16)
        dg2[...] += dg
        dwo[...] += _mm_tn(ycr[...], dy16)
        dycat = _mm_nt(dy16, wo[...])
        dya = dycat[:, 0:DC]
        dyb = dycat[:, DC:D]
        cp, cn = _halo_rows(gcp, gcn, i, nt - 1)
        vp, vn = _halo_rows(gvp, gvn, i, nt - 1)
        c = gc[...].astype(F32) * gv[...].astype(F32)
        cc, c_m1, c_p1 = _conv3(c, cp * vp, cn * vn, ca)
        dgb[...] = (dya * cc).astype(BF16)
        d = dya * gb[...].astype(F32)
        dcc[...] = d.astype(BF16)
        for k, s in enumerate((c_m1, c, c_p1)):
            dca[k:k + 1, :] += jnp.sum(d * s, axis=0, keepdims=True)
        oh, rs = _headnorm(of[...].astype(F32) + ob[...].astype(F32))
        g = go[...].astype(F32)
        sg = _sigmoid(g)
        silu = g * sg
        dgo[...] = (dyb * (oh * gh[...]) * (sg * (1.0 + g * (1.0 - sg)))).astype(BF16)
        don = dyb * silu
        t = jnp.sum(don * oh, axis=0, keepdims=True)
        dgh[0:1, :] += t[:, 0:HV] + t[:, HV:2 * HV] + t[:, 2 * HV:3 * HV] + t[:, 3 * HV:4 * HV]
        doh = don * gh[...]
        parts = []
        for h in range(NH):
            hs = slice(h * HV, (h + 1) * HV)
            parts.append(rs[h] * (doh[:, hs] - oh[:, hs] * jnp.mean(doh[:, hs] * oh[:, hs], axis=-1, keepdims=True)))
        do[...] = jnp.concatenate(parts, axis=1).astype(BF16)

    t = lambda cb: pl.BlockSpec((tl, DC), lambda i: (i, cb))
    hp = lambda cb: pl.BlockSpec((HALO16, DC), _prev_blk(tl, cb, HALO16))
    hn = lambda cb: pl.BlockSpec((HALO16, DC), _next_blk(tl, L, cb, HALO16))
    row = lambda n: pl.BlockSpec((tl, n), lambda i: (i, 0))
    full = lambda a: pl.BlockSpec(a.shape, lambda i: (0, 0))
    act16 = lambda n: jax.ShapeDtypeStruct((L, n), BF16)
    return pl.pallas_call(
        body, name="mix_bwd1", grid=(nt,),
        in_specs=[row(D), row(D), full(g2), row(D), full(w_out), t(CB_GB), t(CB_GC), t(CB_GV), t(CB_GO),
                  hp(CB_GC), hp(CB_GV), hn(CB_GC), hn(CB_GV), row(DG), row(DG), full(conv_a), full(ghn4)] + [_ANY] * na,
        out_specs=[row(DC), row(DC), row(DG), row(DG), pl.BlockSpec((8, DC), lambda i: (0, 0)),
                   pl.BlockSpec((8, HV), lambda i: (0, 0)), pl.BlockSpec((1, D), lambda i: (0, 0)),
                   pl.BlockSpec((D, D), lambda i: (0, 0))],
        out_shape=[act16(DC), act16(DC), act16(DG), act16(DG), jax.ShapeDtypeStruct((8, DC), F32),
                   jax.ShapeDtypeStruct((8, HV), F32), jax.ShapeDtypeStruct((1, D), F32),
                   jax.ShapeDtypeStruct((D, D), F32)],
        compiler_params=_cp("arbitrary"),
    )(dx1, y, g2, ycat, w_out, P, P, P, P, P, P, P, P, o_f, o_b, conv_a, ghn4, *after)


def _gla_chunk_bwd(qt, kt, kh, v, do, st16, dec, g_ref, m, rev):
    qt16, kt16, kh16, v16, do16 = (t.astype(BF16) for t in (qt, kt, kh, v, do))
    qs, sc = _gla_scores(qt16, kt16, m, rev)
    g = g_ref[...]
    g16 = g.astype(BF16)
    dob = _tile4(do16) * m["bdo"].astype(BF16)
    dv = _mm_tn(sc, dob) + _mm_nt(kh16, g16)
    dsc = jnp.where(m["triu4"] if rev else m["tril4"], _mm_nt(dob, v16), 0.0)
    r1 = _mm(dsc, kt16) * m["bdq"]
    dqt = r1[0:CH] + r1[CH:2 * CH] + r1[2 * CH:3 * CH] + r1[3 * CH:4 * CH] + _mm(do16, st16)
    dkt = _mm_tn(dsc, qs)
    dkh = _mm(v16, g16)
    dd = jnp.sum(g * st16.astype(F32), axis=0, keepdims=True)
    g_ref[...] = g * dec + _mm_tn(do16, qt16) * m["bds"]
    return dv, dqt, dkt, dkh, dd


def gla_bwd(P, do, sf, sb, gcat, gbias, tl):
    L = P.shape[0]
    nb = L // tl
    nc = tl // CH

    def body(qf, kf, vf, lf, dof, sfr, qb, kb, vb, lb, dob, sbr, gc_ref, bs_ref,
             dqf, dkf, dvf, daf, dqb, dkb, dvb, dab, gf, gbk, *scr):
        @pl.when(pl.program_id(0) == 0)
        def _():
            gf[...] = jnp.zeros_like(gf)
            gbk[...] = jnp.zeros_like(gbk)

        m = _gla_masks()
        keys = ("qt", "kt", "kh", "e", "einv", "eout", "dec")
        names = keys + ("dd", "dqt", "dkt", "dkh")
        pf = dict(zip(names, scr[0:11]))
        pb = dict(zip(names, scr[11:22]))
        for (q, k, lr, cols, rev, pr) in ((qf, kf, lf, slice(0, DK), False, pf), (qb, kb, lb, slice(DK, 2 * DK), True, pb)):
            p = _gla_tile_prep(q[...], k[...], _gates(lr, gc_ref, bs_ref, cols), m, rev, nc)
            for key in keys:
                pr[key][...] = p[key]

        def step(c, v, dor, st, g_ref, pr, dv, rev):
            rows = pl.ds(pl.multiple_of(c * CH, CH), CH)
            dvc, dqt, dkt, dkh, dd = _gla_chunk_bwd(pr["qt"][rows, :], pr["kt"][rows, :], pr["kh"][rows, :], v[rows, :],
                                                    dor[rows, :], st[c], pr["dec"][pl.ds(c, 1), :], g_ref, m, rev)
            dv[rows, :] = dvc.astype(BF16)
            pr["dqt"][rows, :] = dqt
            pr["dkt"][rows, :] = dkt
            pr["dkh"][rows, :] = dkh
            pr["dd"][pl.ds(c, 1), :] = dd

        def chunk(c, carry):
            step(nc - 1 - c, vf, dof, sfr, gf, pf, dvf, False)
            step(c, vb, dob, sbr, gbk, pb, dvb, True)
            return carry

        lax.fori_loop(0, nc, chunk, 0, unroll=2)

        def finish(pr, dq, dk, da, rev):
            dqt, dkt, dkh = pr["dqt"][...], pr["dkt"][...], pr["dkh"][...]
            kk = dkh * pr["kh"][...]
            dcum = dqt * pr["qt"][...] - dkt * pr["kt"][...] - kk
            dtot = pr["dd"][...] * pr["dec"][...]
            tri_t = m["tril"] if rev else m["triu"]
            parts = []
            for c in range(nc):
                rs = slice(c * CH, (c + 1) * CH)
                parts.append(_mm_tri(tri_t, dcum[rs]) + (jnp.sum(kk[rs], axis=0, keepdims=True) + dtot[c:c + 1]))
            da[...] = jnp.concatenate(parts, axis=0).astype(BF16)
            dq[...] = (dqt * pr["e"][...] * QSCALE).astype(BF16)
            dk[...] = (dkt * pr["einv"][...] + dkh * pr["eout"][...]).astype(BF16)

        finish(pf, dqf, dkf, daf, False)
        finish(pb, dqb, dkb, dab, True)

    fwd_dir = lambda cb: (lambda i: (nb - 1 - i, cb))
    bwd_dir = lambda cb: (lambda i: (i, cb))

    def side(ix):
        return [pl.BlockSpec((tl, DK), ix(CB_Q)), pl.BlockSpec((tl, DK), ix(CB_K)), pl.BlockSpec((tl, DG), ix(CB_V)),
                pl.BlockSpec((tl, LRW), ix(CB_LR)), pl.BlockSpec((tl, DG), ix(0)),
                pl.BlockSpec((nc, DG, DK), lambda i: (ix(0)(i)[0], 0, 0))]

    def outs(ix):
        return [pl.BlockSpec((tl, DK), ix(0)), pl.BlockSpec((tl, DK), ix(0)), pl.BlockSpec((tl, DG), ix(0)),
                pl.BlockSpec((tl, DK), ix(0))]

    o_shape = [jax.ShapeDtypeStruct((L, DK), BF16), jax.ShapeDtypeStruct((L, DK), BF16),
               jax.ShapeDtypeStruct((L, DG), BF16), jax.ShapeDtypeStruct((L, DK), BF16)]
    return pl.pallas_call(
        body, name="gla_bwd", grid=(nb,),
        in_specs=side(fwd_dir) + side(bwd_dir) + [pl.BlockSpec((LRW, 2 * DK), lambda i: (0, 0)),
                                                  pl.BlockSpec((1, 2 * DK), lambda i: (0, 0))],
        out_specs=outs(fwd_dir) + outs(bwd_dir),
        out_shape=o_shape + o_shape,
        scratch_shapes=[pltpu.VMEM((DG, DK), F32), pltpu.VMEM((DG, DK), F32)]
        + ([pltpu.VMEM((tl, DK), F32)] * 6 + [pltpu.VMEM((nc, DK), F32)] * 2 + [pltpu.VMEM((tl, DK), F32)] * 3) * 2,
        compiler_params=_cp("arbitrary"),
    )(P, P, P, P, do, sf, P, P, P, P, do, sb, gcat, gbias)


def mix_bwd2(dgb, dcc, dgo, gl, P, conv_a, gcat, gbias, w_in, x, dres, g1, tl):
    L = P.shape[0]
    nt = L // tl

    def body(dgbr, dccr, dccp, dccn, dgor, dqf, dkf, dvf, daf, dqb, dkb, dvb, dab, gc, gv, lr, ca, gcr, bsr, wi,
             xr, drr, g1r, dP, dx, dg1, dgcat, dbias):
        i = pl.program_id(0)

        @pl.when(i == 0)
        def _():
            dg1[...] = jnp.zeros_like(dg1)
            dgcat[...] = jnp.zeros_like(dgcat)
            dbias[...] = jnp.zeros_like(dbias)

        p, n = _halo_rows(dccp, dccn, i, nt - 1)
        dc = _conv3_t(dccr[...].astype(F32), p, n, ca)
        pre = _mm(lr[...], gcr[...]) + bsr[...]
        da = jnp.concatenate([daf[...], dab[...]], axis=1).astype(F32)
        add32 = lambda a, b: a[...].astype(F32) + b[...].astype(F32)
        dpre = da * GATE_NORM * (1.0 - _sigmoid(pre))
        dpre16 = dpre.astype(BF16)
        dP[:, 0:DC] = dgbr[...].astype(BF16)
        dP[:, DC:2 * DC] = (dc * gv[...].astype(F32)).astype(BF16)
        dP[:, 2 * DC:3 * DC] = (dc * gc[...].astype(F32)).astype(BF16)
        dP[:, 1536:1792] = add32(dqf, dqb).astype(BF16)
        dP[:, 1792:2048] = add32(dkf, dkb).astype(BF16)
        dP[:, 2048:2560] = add32(dvf, dvb).astype(BF16)
        dP[:, 2560:3072] = dgor[...].astype(BF16)
        dP[:, 3072:3200] = _mm_nt(dpre16, gcr[...]).astype(BF16)
        dgcat[...] += _mm_tn(lr[...], dpre16)
        dbias[0:1, :] += jnp.sum(dpre, axis=0, keepdims=True)
        dh, dg = _rms_bwd(_mm_nt(dP[...], wi[...]), xr[...], g1r[...])
        dx[...] = drr[...] + dh
        dg1[...] += dg

    row = lambda n: pl.BlockSpec((tl, n), lambda i: (i, 0))
    t = lambda w, cb: pl.BlockSpec((tl, w), lambda i: (i, cb))
    full = lambda a: pl.BlockSpec(a.shape, lambda i: (0, 0))
    return pl.pallas_call(
        body, name="mix_bwd2", grid=(nt,),
        in_specs=[row(DC), row(DC), pl.BlockSpec((HALO16, DC), _prev_blk(tl, 0, HALO16)),
                  pl.BlockSpec((HALO16, DC), _next_blk(tl, L, 0, HALO16)),
                  row(DG), row(DK), row(DK), row(DG), row(DK), row(DK), row(DK), row(DG), row(DK),
                  t(DC, CB_GC), t(DC, CB_GV), t(LRW, CB_LR), full(conv_a), full(gcat), full(gbias),
                  pl.BlockSpec(w_in.shape, lambda i: (0, 0), pipeline_mode=pl.Buffered(1)),
                  row(D), row(D), full(g1)],
        out_specs=[row(DINP), row(D), pl.BlockSpec((1, D), lambda i: (0, 0)), pl.BlockSpec((LRW, 2 * DK), lambda i: (0, 0)),
                   pl.BlockSpec((8, 2 * DK), lambda i: (0, 0))],
        out_shape=[jax.ShapeDtypeStruct((L, DINP), BF16), jax.ShapeDtypeStruct((L, D), F32),
                   jax.ShapeDtypeStruct((1, D), F32), jax.ShapeDtypeStruct((LRW, 2 * DK), F32),
                   jax.ShapeDtypeStruct((8, 2 * DK), F32)],
        compiler_params=_cp("arbitrary"),
    )(dgb, dcc, dcc, dcc, dgo, *gl, P, P, P, conv_a, gcat, gbias, w_in, x, dres, g1)


def _row_tile(rows, cols):
    if rows * cols * 4 <= 2 * 1024 * 1024:
        return rows
    best = 8
    for t in range(8, rows, 8):
        if rows % t == 0 and t * cols * 4 <= 2 * 1024 * 1024:
            best = t
    return best


def adamw(w, g, m, v, name):
    shape = w.shape
    cols = shape[-1]
    w2, g2, m2, v2 = (a.reshape(-1, cols) for a in (w, g, m, v))
    rows = w2.shape[0]
    tr = _row_tile(rows, cols)

    def body(wr, gr, mr, vr, dl, nm, nv):
        gg = gr[...]
        mm = B1 * mr[...] + (1.0 - B1) * gg
        vv = B2 * vr[...] + (1.0 - B2) * (gg * gg)
        m_hat = mm / (1.0 - B1 ** STEP)
        v_hat = vv / (1.0 - B2 ** STEP)
        dl[...] = -LR * (m_hat / (jnp.sqrt(v_hat) + AEPS) + WD * wr[...])
        nm[...] = mm
        nv[...] = vv

    blk = pl.BlockSpec((tr, cols), lambda i: (i, 0))
    o = jax.ShapeDtypeStruct((rows, cols), F32)
    d, nm, nv = pl.pallas_call(
        body, name=name, grid=(rows // tr,), in_specs=[blk] * 4, out_specs=[blk] * 3, out_shape=[o, o, o],
        compiler_params=_cp("parallel"),
    )(w2, g2, m2, v2)
    return d.reshape(shape), nm.reshape(shape), nv.reshape(shape)


def _place():
    return lax.axis_index("x"), lax.axis_index("y"), lax.axis_index("c")


def allgather8(v, name):
    mp, n = v.shape

    def body(x_ref, out_ref, send_sems, recv_sems, local_sem):
        x, y, c = _place()
        me, sibling = (x, y, c), (x, y, 1 - c)
        chips = [(1 - x, y), (x, 1 - y), (1 - x, 1 - y)]

        def rows(px, py, pc):
            return out_ref.at[pl.ds((4 * px + 2 * py + pc) * mp, mp), :]

        def copy(k, block, to, src=None):
            return pltpu.make_async_remote_copy(
                src_ref=rows(*block) if src is None else src, dst_ref=rows(*block),
                send_sem=send_sems.at[k], recv_sem=recv_sems.at[k], device_id=to, device_id_type=MESH)

        mine = pltpu.make_async_copy(x_ref, rows(*me), local_sem)
        mine.start()
        first = [copy(0, me, sibling, src=x_ref)]
        first += [copy(1 + j, me, (*chip, c), src=x_ref) for j, chip in enumerate(chips)]
        for cp in first:
            cp.start()
        passed = [copy(4 + j, (*chip, c), sibling) for j, chip in enumerate(chips)]
        for j, chip in enumerate(chips):
            copy(1 + j, (*chip, c), me).wait_recv()
            passed[j].start()
        copy(0, sibling, me).wait_recv()
        for j, chip in enumerate(chips):
            copy(4 + j, (*chip, 1 - c), me).wait_recv()
        for cp in first + passed:
            cp.wait_send()
        mine.wait()

    return pl.pallas_call(
        body, name=name, out_shape=jax.ShapeDtypeStruct((8 * mp, n), v.dtype),
        in_specs=[pl.BlockSpec(memory_space=pltpu.VMEM)], out_specs=pl.BlockSpec(memory_space=pltpu.VMEM),
        scratch_shapes=[pltpu.SemaphoreType.DMA((7,)), pltpu.SemaphoreType.DMA((7,)), pltpu.SemaphoreType.DMA],
        compiler_params=pltpu.CompilerParams(vmem_limit_bytes=VMEM_LIMIT),
    )(v)


def sum8(v, mp):
    def body(x_ref, o_ref):
        acc = x_ref[0:mp, :]
        for d in range(1, 8):
            acc = acc + x_ref[d * mp:(d + 1) * mp, :]
        o_ref[...] = acc

    return pl.pallas_call(body, name="sum8", out_shape=jax.ShapeDtypeStruct((mp, v.shape[1]), F32),
                          compiler_params=pltpu.CompilerParams(vmem_limit_bytes=VMEM_LIMIT))(v)


_ANY = pl.BlockSpec(memory_space=pl.ANY)


def _row_half(ref, lead, h):
    hr = ref.shape[-2] // 2
    return ref.at[(*lead, pl.ds(h * hr, hr), slice(None))]


def allgather_weights(slots):
    n = len(slots)

    def body(*refs):
        s_refs, o_refs, (send_sems, recv_sems) = refs[:n], refs[n:2 * n], refs[2 * n:]
        x, y, c = _place()
        me = 2 * x + y
        sibling = (x, y, 1 - c)
        chips = [(1 - x, y), (x, 1 - y), (1 - x, 1 - y)]

        def half(ref, slot, h):
            return _row_half(ref, (slot, slice(None)), h)

        def copy(k, src, dst, to):
            return pltpu.make_async_remote_copy(src_ref=src, dst_ref=dst, send_sem=send_sems.at[k],
                                                recv_sem=recv_sems.at[k], device_id=to, device_id_type=MESH)

        first = [copy(6 * a + k, half(s_refs[a], me, c), half(o_refs[a], me, c), (px, py, c))
                 for k, (px, py) in enumerate(chips) for a in range(n)]
        for cp in first:
            cp.start()
        passed = []
        for k, (px, py) in enumerate(chips):
            for a in range(n):
                got = half(o_refs[a], 2 * px + py, c)
                copy(6 * a + k, half(s_refs[a], me, c), got, (px, py, c)).wait_recv()
                cp = copy(6 * a + 3 + k, got, got, sibling)
                cp.start()
                passed.append(cp)
        for k, (px, py) in enumerate(chips):
            for a in range(n):
                got = half(o_refs[a], 2 * px + py, 1 - c)
                copy(6 * a + 3 + k, got, got, sibling).wait_recv()
        for cp in first + passed:
            cp.wait_send()

    return pl.pallas_call(
        body, name="allgather_weights", out_shape=[jax.ShapeDtypeStruct(s.shape, s.dtype) for s in slots],
        in_specs=[_ANY] * n, out_specs=[_ANY] * n, input_output_aliases={a: a for a in range(n)},
        scratch_shapes=[pltpu.SemaphoreType.DMA((6 * n,)), pltpu.SemaphoreType.DMA((6 * n,))],
    )(*slots)


_HBM = pl.BlockSpec(memory_space=pltpu.HBM)
_SEM = pl.BlockSpec(memory_space=pltpu.SEMAPHORE)
_EFFECT = pltpu.SideEffectType.DATAFLOW_SIDE_EFFECTING


def gather_start(slots, name, after=()):
    n = len(slots)
    na = len(after)

    def body(*refs):
        s_refs, send_sems, recv_sems, token = refs[:n], refs[n + na], refs[n + na + 1], refs[-1]
        x, y, c = _place()
        me = 2 * x + y
        for k, (px, py) in enumerate([(1 - x, y), (x, 1 - y), (1 - x, 1 - y)]):
            for a in range(n):
                pltpu.make_async_remote_copy(
                    src_ref=s_refs[a].at[me], dst_ref=s_refs[a].at[me], send_sem=send_sems.at[3 * a + k],
                    recv_sem=recv_sems.at[3 * a + k], device_id=(px, py, c), device_id_type=MESH).start()
        token[...] = jnp.zeros_like(token)

    out = pl.pallas_call(
        body, name=name,
        out_shape=(pltpu.SemaphoreType.DMA((3 * n,)), pltpu.SemaphoreType.DMA((3 * n,)),
                   *[pltpu.HBM(s.shape, s.dtype) for s in slots], jax.ShapeDtypeStruct((8, 128), F32)),
        in_specs=[_HBM] * n + [_ANY] * na, out_specs=(_SEM, _SEM, *[_HBM] * n, pl.BlockSpec(memory_space=pltpu.VMEM)),
        input_output_aliases={a: 2 + a for a in range(n)},
        compiler_params=pltpu.CompilerParams(has_side_effects=_EFFECT),
    )(*[pltpu.with_memory_space_constraint(s, pltpu.HBM) for s in slots], *after)
    return out[0], out[1], list(out[2:2 + n]), out[-1]


def gather_wait(send_sems, recv_sems, slots, after, name):
    n = len(slots)

    def body(*refs):
        s_refs, ssem, rsem = refs[:n], refs[n], refs[n + 1]
        x, y, c = _place()
        me = 2 * x + y
        for k, (px, py) in enumerate([(1 - x, y), (x, 1 - y), (1 - x, 1 - y)]):
            for a in range(n):
                cp = pltpu.make_async_remote_copy(
                    src_ref=s_refs[a].at[me], dst_ref=s_refs[a].at[2 * px + py], send_sem=ssem.at[3 * a + k],
                    recv_sem=rsem.at[3 * a + k], device_id=(px, py, c), device_id_type=MESH)
                cp.wait_send()
                cp.wait_recv()

    return pl.pallas_call(
        body, name=name, out_shape=[pltpu.HBM(s.shape, s.dtype) for s in slots],
        in_specs=[_HBM] * n + [_SEM, _SEM, _ANY], out_specs=[_HBM] * n,
        input_output_aliases={a: a for a in range(n)},
        compiler_params=pltpu.CompilerParams(has_side_effects=_EFFECT),
    )(*slots, send_sems, recv_sems, after)


def rs_chipsum16(g, recv1, cidx, name):
    nl, hr, cols = recv1.shape[1:]

    def body(c_ref, g_ref, r_ref, o_ref):
        o_ref[...] = (g_ref[...] + r_ref[...]).astype(BF16)

    blk = (1, 1, hr, cols)
    return pl.pallas_call(
        body, name=name, out_shape=jax.ShapeDtypeStruct(recv1.shape, BF16),
        grid_spec=pltpu.PrefetchScalarGridSpec(
            num_scalar_prefetch=1, grid=(4, nl),
            in_specs=[pl.BlockSpec(blk, lambda j, l, c: (j, l, c[0], 0)), pl.BlockSpec(blk, lambda j, l, c: (j, l, 0, 0))],
            out_specs=pl.BlockSpec(blk, lambda j, l, c: (j, l, 0, 0))),
        compiler_params=_cp("parallel", "parallel"),
    )(cidx, g, recv1)


def sibling_start(gs, name):
    n = len(gs)
    lands = [lax.empty((*g.shape[:2], g.shape[2] // 2, g.shape[3]), F32) for g in gs]

    def body(*refs):
        g_refs, l_refs, send_sems, recv_sems, token = refs[:n], refs[n:2 * n], refs[2 * n], refs[2 * n + 1], refs[-1]
        x, y, c = _place()
        for a in range(n):
            pltpu.make_async_remote_copy(
                src_ref=_row_half(g_refs[a], (slice(None), slice(None)), 1 - c), dst_ref=l_refs[a],
                send_sem=send_sems.at[a], recv_sem=recv_sems.at[a], device_id=(x, y, 1 - c), device_id_type=MESH).start()
        token[...] = jnp.zeros_like(token)

    bufs = list(gs) + lands
    out = pl.pallas_call(
        body, name=name,
        out_shape=(pltpu.SemaphoreType.DMA((n,)), pltpu.SemaphoreType.DMA((n,)),
                   *[pltpu.HBM(b.shape, b.dtype) for b in bufs], jax.ShapeDtypeStruct((8, 128), F32)),
        in_specs=[_HBM] * (2 * n), out_specs=(_SEM, _SEM, *[_HBM] * (2 * n), pl.BlockSpec(memory_space=pltpu.VMEM)),
        input_output_aliases={i: 2 + i for i in range(2 * n)},
        compiler_params=pltpu.CompilerParams(has_side_effects=_EFFECT),
    )(*[pltpu.with_memory_space_constraint(b, pltpu.HBM) for b in bufs])
    return out[0], out[1], list(out[2:2 + n]), list(out[2 + n:2 + 2 * n]), out[-1]


def sibling_wait(send_sems, recv_sems, gs, lands, after, name):
    n = len(gs)

    def body(*refs):
        g_refs, l_refs, ssem, rsem = refs[:n], refs[n:2 * n], refs[2 * n], refs[2 * n + 1]
        x, y, c = _place()
        for a in range(n):
            cp = pltpu.make_async_remote_copy(
                src_ref=_row_half(g_refs[a], (slice(None), slice(None)), 1 - c), dst_ref=l_refs[a],
                send_sem=ssem.at[a], recv_sem=rsem.at[a], device_id=(x, y, 1 - c), device_id_type=MESH)
            cp.wait_send()
            cp.wait_recv()

    bufs = list(gs) + list(lands)
    out = pl.pallas_call(
        body, name=name, out_shape=[pltpu.HBM(b.shape, b.dtype) for b in bufs],
        in_specs=[_HBM] * (2 * n) + [_SEM, _SEM, _ANY], out_specs=[_HBM] * (2 * n),
        input_output_aliases={i: i for i in range(2 * n)},
        compiler_params=pltpu.CompilerParams(has_side_effects=_EFFECT),
    )(*bufs, send_sems, recv_sems, after)
    return list(out[:n]), list(out[n:])


def exchange_start(cs, name):
    n = len(cs)
    lands = [lax.empty((3, *c.shape[1:]), BF16) for c in cs]

    def body(*refs):
        s_refs, l_refs, send_sems, recv_sems, token = refs[:n], refs[n:2 * n], refs[2 * n], refs[2 * n + 1], refs[-1]
        x, y, c = _place()
        for k, (px, py) in enumerate([(1 - x, y), (x, 1 - y), (1 - x, 1 - y)]):
            for a in range(n):
                pltpu.make_async_remote_copy(
                    src_ref=s_refs[a].at[2 * px + py], dst_ref=l_refs[a].at[k], send_sem=send_sems.at[3 * a + k],
                    recv_sem=recv_sems.at[3 * a + k], device_id=(px, py, c), device_id_type=MESH).start()
        token[...] = jnp.zeros_like(token)

    bufs = list(cs) + lands
    out = pl.pallas_call(
        body, name=name,
        out_shape=(pltpu.SemaphoreType.DMA((3 * n,)), pltpu.SemaphoreType.DMA((3 * n,)),
                   *[pltpu.HBM(b.shape, b.dtype) for b in bufs], jax.ShapeDtypeStruct((8, 128), F32)),
        in_specs=[_HBM] * (2 * n), out_specs=(_SEM, _SEM, *[_HBM] * (2 * n), pl.BlockSpec(memory_space=pltpu.VMEM)),
        input_output_aliases={i: 2 + i for i in range(2 * n)},
        compiler_params=pltpu.CompilerParams(has_side_effects=_EFFECT),
    )(*[pltpu.with_memory_space_constraint(b, pltpu.HBM) for b in bufs])
    return out[0], out[1], list(out[2:2 + n]), list(out[2 + n:2 + 2 * n]), out[-1]


def exchange_wait(send_sems, recv_sems, cs, lands, after, name):
    n = len(cs)

    def body(*refs):
        s_refs, l_refs, ssem, rsem = refs[:n], refs[n:2 * n], refs[2 * n], refs[2 * n + 1]
        x, y, c = _place()
        for k, (px, py) in enumerate([(1 - x, y), (x, 1 - y), (1 - x, 1 - y)]):
            for a in range(n):
                cp = pltpu.make_async_remote_copy(
                    src_ref=s_refs[a].at[2 * px + py], dst_ref=l_refs[a].at[k], send_sem=ssem.at[3 * a + k],
                    recv_sem=rsem.at[3 * a + k], device_id=(px, py, c), device_id_type=MESH)
                cp.wait_send()
                cp.wait_recv()

    bufs = list(cs) + list(lands)
    out = pl.pallas_call(
        body, name=name, out_shape=[pltpu.HBM(b.shape, b.dtype) for b in bufs],
        in_specs=[_HBM] * (2 * n) + [_SEM, _SEM, _ANY], out_specs=[_HBM] * (2 * n),
        input_output_aliases={i: i for i in range(2 * n)},
        compiler_params=pltpu.CompilerParams(has_side_effects=_EFFECT),
    )(*bufs, send_sems, recv_sems, after)
    return list(out[n:])


def rs_final_sum(g, recv1, recv2, idx, name):
    nl, hr, cols = recv1.shape[1:]

    def body(i_ref, g_ref, r1_ref, r2_ref, o_ref):
        acc = g_ref[0, 0] + r1_ref[0, 0]
        for k in range(3):
            acc = acc + r2_ref[k, 0].astype(F32)
        o_ref[0] = acc

    blk = (1, 1, hr, cols)
    return pl.pallas_call(
        body, name=name, out_shape=jax.ShapeDtypeStruct((nl, 2 * hr, cols), F32),
        grid_spec=pltpu.PrefetchScalarGridSpec(
            num_scalar_prefetch=1, grid=(nl,),
            in_specs=[pl.BlockSpec(blk, lambda l, ix: (ix[0], l, ix[1], 0)), pl.BlockSpec(blk, lambda l, ix: (ix[0], l, 0, 0)),
                      pl.BlockSpec((3, 1, hr, cols), lambda l, ix: (0, l, 0, 0))],
            out_specs=pl.BlockSpec((1, hr, cols), lambda l, ix: (l, ix[1], 0))),
        compiler_params=_cp("parallel"),
    )(idx, g, recv1, recv2)


def rs_share_halves(fulls):
    n = len(fulls)

    def body(*refs):
        h_refs, o_refs, (send_sems, recv_sems) = refs[:n], refs[n:2 * n], refs[2 * n:]
        x, y, c = _place()
        sibling = (x, y, 1 - c)

        def copy(a, h):
            return pltpu.make_async_remote_copy(
                src_ref=_row_half(h_refs[a], (slice(None),), h), dst_ref=_row_half(o_refs[a], (slice(None),), h),
                send_sem=send_sems.at[a], recv_sem=recv_sems.at[a], device_id=sibling, device_id_type=MESH)

        for a in range(n):
            copy(a, c).start()
        for a in range(n):
            copy(a, c).wait_send()
            copy(a, 1 - c).wait_recv()

    return pl.pallas_call(
        body, name="rs_share_halves", out_shape=[jax.ShapeDtypeStruct(f.shape, F32) for f in fulls],
        in_specs=[_ANY] * n, out_specs=[_ANY] * n, input_output_aliases={a: a for a in range(n)},
        scratch_shapes=[pltpu.SemaphoreType.DMA((n,)), pltpu.SemaphoreType.DMA((n,))],
    )(*fulls)


def _own_slot(shard, chip, dtype):
    return lax.dynamic_update_slice(lax.empty((4, *shard.shape), dtype), shard.astype(dtype)[None],
                                    (chip,) + (0,) * shard.ndim)


def kernel(x, norm_mix_pre, norm_mix_post, norm_ffn_pre, norm_ffn_post, w_in, conv_a, gate_up_fwd, gate_bias_fwd, gate_up_bwd, gate_bias_bwd, gla_head_norm, w_out, w_up, conv_ffn, w_down, loss_target, m_norm_mix_pre, m_norm_mix_post, m_norm_ffn_pre, m_norm_ffn_post, m_w_in, m_conv_a, m_gate_up_fwd, m_gate_bias_fwd, m_gate_up_bwd, m_gate_bias_bwd, m_gla_head_norm, m_w_out, m_w_up, m_conv_ffn, m_w_down, v_norm_mix_pre, v_norm_mix_post, v_norm_ffn_pre, v_norm_ffn_post, v_w_in, v_conv_a, v_gate_up_fwd, v_gate_bias_fwd, v_gate_up_bwd, v_gate_bias_bwd, v_gla_head_norm, v_w_out, v_w_up, v_conv_ffn, v_w_down):
    L = x.shape[1]
    xi, yi, ci = _place()
    chip = 2 * xi + yi
    tl_gla, tl_mix, tl_ffn = min(L, TL_GLA), min(L, TL_MIX), min(L, TL_FFN)

    big_w = (w_in, w_out, w_up, w_down)
    a_in0 = allgather_weights([_own_slot(w_in[0:1], chip, BF16)])[0][:, 0]
    started = []
    prev = (a_in0,)
    for l in range(DEPTH):
        ws = big_w[1:] if l == 0 else big_w
        started.append(gather_start([_own_slot(w[l], chip, BF16) for w in ws], f"gather_start_{l}", after=prev))
        prev = (started[-1][3],)
    tokens = [s[3] for s in started]

    def full_w_in(a_in):
        return jnp.pad(jnp.concatenate([a_in[j] for j in range(4)], axis=1), ((0, 0), (0, DINP - DIN)))

    small = jnp.concatenate([conv_a.reshape(-1), gate_up_fwd.reshape(-1), gate_up_bwd.reshape(-1), conv_ffn.reshape(-1)])
    ms = small.shape[0] // 128
    sg = allgather8(small.reshape(ms, 128), "allgather_small_weights").reshape(4, 2, ms * 128)[:, 0]

    def small_full(off, shape):
        n = shape[0] * shape[1] * shape[2]
        return jnp.concatenate([sg[j, off:off + n].reshape(shape) for j in range(4)], axis=2)

    o1 = DEPTH * 3 * 128
    o2 = o1 + DEPTH * RK * 64
    o3 = o2 + DEPTH * RK * 64
    conv_a_f = small_full(0, (DEPTH, 3, 128))
    gup_f = small_full(o1, (DEPTH, RK, 64))
    gup_b = small_full(o2, (DEPTH, RK, 64))
    conv_ffn_f = small_full(o3, (DEPTH, 3, 1408))

    def gcat_of(l):
        g = jnp.zeros((LRW, 2 * DK), F32)
        g = g.at[0:RK, 0:DK].set(gup_f[l]).at[RK:2 * RK, DK:2 * DK].set(gup_b[l])
        return g.astype(BF16)

    gcats = [gcat_of(l) for l in range(DEPTH)]
    gbiases = [jnp.concatenate([gate_bias_fwd[l], gate_bias_bwd[l]])[None, :] for l in range(DEPTH)]
    ghn4s = [jnp.tile(gla_head_norm[l], NH)[None, :] for l in range(DEPTH)]

    xc = x.reshape(L, D)
    saved = []
    W_in, W_out, W_up, W_down = [], [], [], []
    tl_row = min(L, TL_ROW)
    for l in range(DEPTH):
        ssem, rsem, bufs, _ = started[l]
        if l > 0:
            a_in, a_out, a_up, a_down = gather_wait(ssem, rsem, bufs, xc, f"gather_wait_{l}")
        W_in.append(full_w_in(a_in0 if l == 0 else a_in))
        P, h1 = rms_matmul(xc, norm_mix_pre[l][None, :], W_in[l], DINP, "proj_in", out_dtype=BF16, tm=TM_PROJ,
                           after=tokens if l == 0 else ())
        o_f, o_b, sf, sb = gla_fwd(P, gcats[l], gbiases[l], tl_gla)
        if l == 0:
            a_out, a_up, a_down = gather_wait(ssem, rsem, bufs, o_f, "gather_wait_0")
        W_out.append(a_out.reshape(D, D))
        W_up.append(a_up)
        W_down.append(a_down.reshape(DFF, D))
        ycat, y, x1 = mix_out(P, o_f, o_b, conv_a_f[l], ghn4s[l], W_out[l], norm_mix_post[l][None, :], xc,
                              min(L, TL_MIX_OUT))
        U, h2, ug, uv, z, y2, x2 = ffn_fwd(x1, norm_ffn_pre[l][None, :], W_up[l], conv_ffn_f[l], W_down[l],
                                           norm_ffn_post[l][None, :], tl_ffn)
        saved.append(dict(x=xc, h1=h1, P=P, o_f=o_f, o_b=o_b, sf=sf, sb=sb, ycat=ycat, y=y, x1=x1, h2=h2, U=U, y2=y2,
                          ug=ug, uv=uv, z=z))
        xc = x2

    dx, loss_blk = loss_head(xc, loss_target.reshape(L, D), tl_row)

    big = ("w_in", "w_out", "w_up", "w_down")
    cidx = jnp.reshape(ci, (1,)).astype(jnp.int32)
    idx = jnp.stack([chip, ci]).astype(jnp.int32)
    grads = [None] * DEPTH
    reduced = [dict() for _ in range(DEPTH)]
    tl_dw = min(L, 2048)
    groups = dict(ffn=("w_up", "w_down"), mix=("w_in", "w_out"))
    state = {grp: dict(flight=None, sibling=None) for grp in groups}
    token = ()

    def finish(grp, after):
        lp, gs_p, recv1_p, (ssem, rsem, cs_thru, lands, _) = state[grp]["flight"]
        recv2 = exchange_wait(ssem, rsem, cs_thru, lands, after, f"exchange_wait_{grp}_{lp}")
        halves = [rs_final_sum(g, r1, r2, idx, "rs_final_sum_" + k)
                  for g, r1, r2, k in zip(gs_p, recv1_p, recv2, groups[grp])]
        reduced[lp].update(zip(groups[grp], rs_share_halves(halves)))

    def advance(grp, after):
        st = state[grp]
        ls, (ssem, rsem, gs_thru, lands, _) = st["sibling"]
        gs_s, recv1 = sibling_wait(ssem, rsem, gs_thru, lands, after, f"sibling_wait_{grp}_{ls}")
        cs16 = [rs_chipsum16(g, r, cidx, "rs_chipsum16_" + k) for g, r, k in zip(gs_s, recv1, groups[grp])]
        flight = exchange_start(cs16, f"exchange_start_{grp}_{ls}")
        if st["flight"] is not None:
            finish(grp, flight[4])
        st["flight"] = (ls, gs_s, recv1, flight)
        st["sibling"] = None
        return flight[4]

    for l in reversed(range(DEPTH)):
        s = saved[l]
        dy2, dg4 = rms_bwd_pre(dx, s["y2"], norm_ffn_post[l][None, :], tl_row, after=token)
        g_down = matmul_tn(s["z"], dy2, DFF // 2, D, tl_dw, "dw_down").reshape(4, 1, DFF // 4, D)
        token2 = (advance("mix", g_down),) if state["mix"]["sibling"] is not None else ()
        dU_g, dU_v, dx1, dg3, dcf_g, dcf_v = ffn_bwd(dy2, s["ug"], s["uv"], s["U"], conv_ffn_f[l], W_down[l], W_up[l],
                                                     s["x1"], dx, norm_ffn_pre[l][None, :], tl_ffn)
        g_up = matmul_tn(s["h2"], dU_g, D, WFF, tl_dw, "dw_up_gate",
                         into=(lax.empty((4, 1, D, WFF), F32), (None, None, D, WFF), lambda p, q: (q, 0, 0, 0)))
        g_up = matmul_tn(s["h2"], dU_v, D, WFF, tl_dw, "dw_up_val",
                         into=(g_up, (None, None, D, WFF), lambda p, q: (NFF + q, 0, 0, 0)))
        sib = sibling_start([g_up, g_down], f"sibling_start_ffn_{l}")
        state["ffn"]["sibling"] = (l, sib)
        dgb, dcc, dgo, do, dca, dghn, dg2, dW_out = mix_bwd1(
            dx1, s["y"], norm_mix_post[l][None, :], s["ycat"], W_out[l], s["P"], s["o_f"], s["o_b"], conv_a_f[l],
            ghn4s[l], min(L, TL_MIX_OUT), after=token2 + (sib[4],))
        g_out = dW_out.reshape(4, 1, D // 4, D)
        token3 = advance("ffn", dgb)
        gl = gla_bwd(s["P"], do, s["sf"], s["sb"], gcats[l], gbiases[l], tl_gla)
        dP, dx, dg1, dgcat, dbias = mix_bwd2(dgb, dcc, dgo, gl, s["P"], conv_a_f[l], gcats[l], gbiases[l], W_in[l],
                                             s["x"], dx1, norm_mix_pre[l][None, :], tl_mix)
        dW_in = matmul_tn(s["h1"], dP, D // 2, DINP, tl_dw, "dw_in", after=(token3,))
        g_in = jnp.stack([dW_in[:, (DIN // 4) * j:(DIN // 4) * (j + 1)] for j in range(4)])[:, None]
        grads[l] = dict(
            norm_mix_pre=dg1[0], norm_mix_post=dg2[0], norm_ffn_pre=dg3[0], norm_ffn_post=dg4[0],
            conv_a=dca[0:3], gate_up_fwd=dgcat[0:RK, 0:DK], gate_bias_fwd=dbias[0, 0:DK],
            gate_up_bwd=dgcat[RK:2 * RK, DK:2 * DK], gate_bias_bwd=dbias[0, DK:2 * DK], gla_head_norm=dghn[0],
            conv_ffn=jnp.concatenate([dcf_g[0:3], dcf_v[0:3]], axis=1))
        sib = sibling_start([g_in, g_out], f"sibling_start_mix_{l}")
        state["mix"]["sibling"] = (l, sib)
        token = (sib[4],)
    last = advance("mix", token[0])
    finish("ffn", last)
    finish("mix", last)

    G = {k: jnp.stack([grads[l][k] for l in range(DEPTH)]) for k in grads[0]}

    small_names = ["norm_mix_pre", "norm_mix_post", "norm_ffn_pre", "norm_ffn_post", "conv_a", "gate_up_fwd",
                   "gate_bias_fwd", "gate_up_bwd", "gate_bias_bwd", "gla_head_norm", "conv_ffn"]
    flat = jnp.concatenate([G[k].reshape(-1) for k in small_names] + [loss_blk[0, 0:1]])
    n_small = flat.shape[0]
    mp = -(-n_small // 1024) * 8
    flat = jnp.pad(flat, (0, mp * 128 - n_small)).reshape(mp, 128)
    tot = sum8(allgather8(flat, "allgather_small_grads"), mp).reshape(-1)
    gsm = {}
    o = 0
    for k in small_names:
        n = G[k].size
        gsm[k] = tot[o:o + n].reshape(G[k].shape)
        o += n
    loss = tot[o]

    def my_cols(a, width):
        return lax.dynamic_slice_in_dim(a, chip * width, width, axis=2)

    gsm["conv_a"] = my_cols(gsm["conv_a"], 128)
    gsm["gate_up_fwd"] = my_cols(gsm["gate_up_fwd"], 64)
    gsm["gate_up_bwd"] = my_cols(gsm["gate_up_bwd"], 64)
    gsm["conv_ffn"] = my_cols(gsm["conv_ffn"], 1408)

    for k in big:
        gsm[k] = jnp.concatenate([reduced[l][k] for l in range(DEPTH)], axis=0)

    names = ["norm_mix_pre", "norm_mix_post", "norm_ffn_pre", "norm_ffn_post", "w_in", "conv_a", "gate_up_fwd",
             "gate_bias_fwd", "gate_up_bwd", "gate_bias_bwd", "gla_head_norm", "w_out", "w_up", "conv_ffn", "w_down"]
    w = dict(norm_mix_pre=norm_mix_pre, norm_mix_post=norm_mix_post, norm_ffn_pre=norm_ffn_pre, norm_ffn_post=norm_ffn_post,
             w_in=w_in, conv_a=conv_a, gate_up_fwd=gate_up_fwd, gate_bias_fwd=gate_bias_fwd, gate_up_bwd=gate_up_bwd,
             gate_bias_bwd=gate_bias_bwd, gla_head_norm=gla_head_norm, w_out=w_out, w_up=w_up, conv_ffn=conv_ffn, w_down=w_down)
    m = dict(norm_mix_pre=m_norm_mix_pre, norm_mix_post=m_norm_mix_post, norm_ffn_pre=m_norm_ffn_pre, norm_ffn_post=m_norm_ffn_post,
             w_in=m_w_in, conv_a=m_conv_a, gate_up_fwd=m_gate_up_fwd, gate_bias_fwd=m_gate_bias_fwd, gate_up_bwd=m_gate_up_bwd,
             gate_bias_bwd=m_gate_bias_bwd, gla_head_norm=m_gla_head_norm, w_out=m_w_out, w_up=m_w_up, conv_ffn=m_conv_ffn, w_down=m_w_down)
    v = dict(norm_mix_pre=v_norm_mix_pre, norm_mix_post=v_norm_mix_post, norm_ffn_pre=v_norm_ffn_pre, norm_ffn_post=v_norm_ffn_post,
             w_in=v_w_in, conv_a=v_conv_a, gate_up_fwd=v_gate_up_fwd, gate_bias_fwd=v_gate_bias_fwd, gate_up_bwd=v_gate_up_bwd,
             gate_bias_bwd=v_gate_bias_bwd, gla_head_norm=v_gla_head_norm, w_out=v_w_out, w_up=v_w_up, conv_ffn=v_conv_ffn, w_down=v_w_down)
    upd = {k: adamw(w[k], gsm[k], m[k], v[k], "adamw_" + k) for k in names}
    return (loss, dx.reshape(1, L, D), *[gsm[k] for k in names], *[upd[k][0] for k in names],
            *[upd[k][1] for k in names], *[upd[k][2] for k in names])
```

```python
import functools

import jax
import jax.numpy as jnp
from jax import lax
from jax.experimental import pallas as pl
from jax.experimental.pallas import tpu as pltpu

F32 = jnp.float32
BF16 = jnp.bfloat16
MXU_DTYPE = jnp.bfloat16
MESH = pl.DeviceIdType.MESH

D = 1024
DC = 512
DG = 512
NH = 4
HV = 128
HK = 64
DK = 256
RK = 16
CH = 64
DFF = 2816
DIN = 3104
DINP = 3200
LRW = 128
DEPTH = 4
EPS = 1e-6
QSCALE = HK ** -0.5
GATE_NORM = 1.0 / 16.0
CB_GB, CB_GC, CB_GV, CB_GO = 0, 1, 2, 5
CB_Q, CB_K = 6, 7
CB_V = 4
CB_LR = 24
LR = 0.001
B1 = 0.9
B2 = 0.999
AEPS = 1e-08
WD = 0.01
STEP = 10
TM_PROJ = 1024
TL_GLA = 512
TL_MIX = 512
TL_MIX_OUT = 512
TL_ROW = 1024
TL_FFN = 256
VMEM_LIMIT = 56 * 1024 * 1024


def _cp(*sem):
    return pltpu.CompilerParams(dimension_semantics=sem if sem else None, vmem_limit_bytes=VMEM_LIMIT)


def _mm(a, b):
    return jnp.dot(a.astype(MXU_DTYPE), b.astype(MXU_DTYPE), preferred_element_type=F32)


def _mm_nt(a, b):
    return lax.dot_general(a.astype(MXU_DTYPE), b.astype(MXU_DTYPE), (((1,), (1,)), ((), ())),
                           preferred_element_type=F32)


def _mm_tn(a, b):
    return lax.dot_general(a.astype(MXU_DTYPE), b.astype(MXU_DTYPE), (((0,), (0,)), ((), ())),
                           preferred_element_type=F32)


def _mm_tri(tri, b):
    t = tri.astype(BF16)
    b1 = b.astype(BF16)
    r1 = b - b1.astype(F32)
    b2 = r1.astype(BF16)
    b3 = (r1 - b2.astype(F32)).astype(BF16)
    dot = lambda u: jnp.dot(t, u, preferred_element_type=F32)
    return dot(b1) + dot(b2) + dot(b3)


def _rms(x, g):
    r = lax.rsqrt(jnp.mean(x * x, axis=-1, keepdims=True) + EPS)
    return x * r * g


def _rms_bwd(dout, y, g):
    r = lax.rsqrt(jnp.mean(y * y, axis=-1, keepdims=True) + EPS)
    yh = y * r
    dyh = dout * g
    dy = r * (dyh - yh * jnp.mean(dyh * yh, axis=-1, keepdims=True))
    dg = jnp.sum(dout * yh, axis=0, keepdims=True)
    return dy, dg


def _sigmoid(x):
    return 0.5 * jnp.tanh(0.5 * x) + 0.5


def _logsig(x):
    return jnp.minimum(x, 0.0) - jnp.log1p(jnp.exp(-jnp.abs(x)))


def _shifts(x, p8, n8):
    n = x.shape[0]
    xe = jnp.concatenate([p8, x, n8], axis=0)
    return pltpu.roll(xe, 1, 0)[8:8 + n], pltpu.roll(xe, n + 15, 0)[8:8 + n]


def _halo_rows(prev_ref, next_ref, i, last):
    hr = prev_ref.shape[0]
    p = jnp.where(i == 0, 0.0, prev_ref[...].astype(F32)[hr - 8:hr, :])
    n = jnp.where(i == last, 0.0, next_ref[...].astype(F32)[0:8, :])
    return p, n


def _conv3(x, xp, xn, w_ref):
    xm1, xp1 = _shifts(x, xp, xn)
    return w_ref[0:1, :] * xm1 + w_ref[1:2, :] * x + w_ref[2:3, :] * xp1, xm1, xp1


def _conv3_t(d, dp, dn, w_ref):
    dm1, dp1 = _shifts(d, dp, dn)
    return w_ref[0:1, :] * dp1 + w_ref[1:2, :] * d + w_ref[2:3, :] * dm1


HALO32 = 8
HALO16 = 16


def _prev_row_blk(i, tl, hr):
    return jnp.maximum(i * (tl // hr) - 1, 0)


def _next_row_blk(i, tl, nrows, hr):
    return jnp.minimum((i + 1) * (tl // hr), nrows // hr - 1)


def _prev_blk(tl, cb, hr=HALO32):
    return lambda i: (_prev_row_blk(i, tl, hr), cb)


def _next_blk(tl, nrows, cb, hr=HALO32):
    return lambda i: (_next_row_blk(i, tl, nrows, hr), cb)


def rms_matmul(x, g, w, tn, name, w_spec=None, n_out=None, out_dtype=F32, after=(), tm=TM_PROJ):
    L = x.shape[0]
    N = w.shape[1] if n_out is None else n_out
    tm = min(L, tm)
    if w_spec is None:
        w_spec = pl.BlockSpec((D, tn), lambda i, j: (0, j))

    def body(x_ref, g_ref, w_ref, *rest):
        o_ref, h_ref = rest[-2:]

        @pl.when(pl.program_id(1) == 0)
        def _():
            h_ref[...] = _rms(x_ref[...], g_ref[...]).astype(BF16)

        o_ref[...] = _mm(h_ref[...], w_ref[...]).astype(out_dtype)

    return pl.pallas_call(
        body, name=name, grid=(L // tm, N // tn),
        in_specs=[pl.BlockSpec((tm, D), lambda i, j: (i, 0)), pl.BlockSpec((1, D), lambda i, j: (0, 0)), w_spec]
        + [_ANY] * len(after),
        out_specs=[pl.BlockSpec((tm, tn), lambda i, j: (i, j)), pl.BlockSpec((tm, D), lambda i, j: (i, 0))],
        out_shape=[jax.ShapeDtypeStruct((L, N), out_dtype), jax.ShapeDtypeStruct((L, D), BF16)],
        compiler_params=_cp("parallel", "arbitrary"),
    )(x, g, w, *after)


def _gla_masks():
    def blk(shape, rdiv, cdiv):
        r = lax.broadcasted_iota(jnp.int32, shape, 0) // rdiv
        c = lax.broadcasted_iota(jnp.int32, shape, 1) // cdiv
        return (r == c).astype(F32)

    r = lax.broadcasted_iota(jnp.int32, (CH, CH), 0)
    c = lax.broadcasted_iota(jnp.int32, (CH, CH), 1)
    r4 = lax.broadcasted_iota(jnp.int32, (NH * CH, CH), 0) % CH
    c4 = lax.broadcasted_iota(jnp.int32, (NH * CH, CH), 1)
    return dict(
        bdq=blk((NH * CH, DK), CH, HK),
        bdo=blk((NH * CH, DG), CH, HV),
        bds=blk((DG, DK), HV, HK),
        tril=(r >= c).astype(F32), triu=(r <= c).astype(F32),
        tril4=r4 >= c4, triu4=r4 <= c4,
    )


def _tile4(x):
    return jnp.concatenate([x, x, x, x], axis=0)


def _gla_tile_prep(q, k, a, m, rev, nc):
    tri = m["triu"] if rev else m["tril"]
    chunks = [a[c * CH:(c + 1) * CH] for c in range(nc)]
    cum = jnp.concatenate([_mm_tri(tri, ac) for ac in chunks], axis=0)
    tot = jnp.concatenate([jnp.sum(ac, axis=0, keepdims=True) for ac in chunks], axis=0)
    tot_rows = jnp.concatenate([jnp.broadcast_to(tot[c:c + 1], (CH, DK)) for c in range(nc)], axis=0)
    e = jnp.exp(cum)
    einv = jnp.exp(-cum)
    eout = jnp.exp(tot_rows - cum)
    q, k = q.astype(F32), k.astype(F32)
    return dict(e=e, einv=einv, eout=eout, dec=jnp.exp(tot), qt=q * QSCALE * e, kt=k * einv, kh=k * eout)


def _gla_scores(qt16, kt16, m, rev):
    qs = _tile4(qt16) * m["bdq"].astype(qt16.dtype)
    return qs, jnp.where(m["triu4"] if rev else m["tril4"], _mm_nt(qs, kt16), 0.0)


def _gla_chunk_fwd(qt16, kt16, kh16, v, dec, st_ref, m, rev):
    _, sc = _gla_scores(qt16, kt16, m, rev)
    v16 = v.astype(BF16)
    r = _mm(sc, v16)
    o_intra = jnp.concatenate([r[h * CH:(h + 1) * CH, h * HV:(h + 1) * HV] for h in range(NH)], axis=1)
    st = st_ref[...]
    st16 = st.astype(BF16)
    o = o_intra + _mm_nt(qt16, st16)
    st_ref[...] = st * dec + _mm_tn(v16, kh16) * m["bds"]
    return o, st16


def _gates(lr_ref, gc_ref, bs_ref, cols):
    return _logsig(_mm(lr_ref[...], gc_ref[:, cols]) + bs_ref[:, cols]) * GATE_NORM


def gla_fwd(P, gcat, gbias, tl):
    L = P.shape[0]
    nb = L // tl
    nc = tl // CH

    def body(qf, kf, vf, lf, qb, kb, vb, lb, gc_ref, bs_ref, of, ob, sf, sb, stf, stb,
             qtf, ktf, khf, dcf, qtb, ktb, khb, dcb):
        @pl.when(pl.program_id(0) == 0)
        def _():
            stf[...] = jnp.zeros_like(stf)
            stb[...] = jnp.zeros_like(stb)

        m = _gla_masks()
        for (q, k, lr, cols, rev, qt, kt, kh, dc) in ((qf, kf, lf, slice(0, DK), False, qtf, ktf, khf, dcf),
                                                      (qb, kb, lb, slice(DK, 2 * DK), True, qtb, ktb, khb, dcb)):
            p = _gla_tile_prep(q[...], k[...], _gates(lr, gc_ref, bs_ref, cols), m, rev, nc)
            qt[...] = p["qt"].astype(BF16)
            kt[...] = p["kt"].astype(BF16)
            kh[...] = p["kh"].astype(BF16)
            dc[...] = p["dec"]

        def chunk(c, carry):
            rows = pl.ds(pl.multiple_of(c * CH, CH), CH)
            o, st = _gla_chunk_fwd(qtf[rows, :], ktf[rows, :], khf[rows, :], vf[rows, :], dcf[pl.ds(c, 1), :], stf, m, False)
            of[rows, :] = o.astype(BF16)
            sf[c] = st
            cb = nc - 1 - c
            rows = pl.ds(pl.multiple_of(cb * CH, CH), CH)
            o, st = _gla_chunk_fwd(qtb[rows, :], ktb[rows, :], khb[rows, :], vb[rows, :], dcb[pl.ds(cb, 1), :], stb, m, True)
            ob[rows, :] = o.astype(BF16)
            sb[cb] = st
            return carry

        lax.fori_loop(0, nc, chunk, 0, unroll=4)

    fw = lambda cb: (lambda i: (i, cb))
    bw = lambda cb: (lambda i: (nb - 1 - i, cb))
    return pl.pallas_call(
        body, name="gla_fwd", grid=(nb,),
        in_specs=[pl.BlockSpec((tl, DK), fw(CB_Q)), pl.BlockSpec((tl, DK), fw(CB_K)), pl.BlockSpec((tl, DG), fw(CB_V)),
                  pl.BlockSpec((tl, LRW), fw(CB_LR)),
                  pl.BlockSpec((tl, DK), bw(CB_Q)), pl.BlockSpec((tl, DK), bw(CB_K)), pl.BlockSpec((tl, DG), bw(CB_V)),
                  pl.BlockSpec((tl, LRW), bw(CB_LR)),
                  pl.BlockSpec((LRW, 2 * DK), lambda i: (0, 0)), pl.BlockSpec((1, 2 * DK), lambda i: (0, 0))],
        out_specs=[pl.BlockSpec((tl, DG), lambda i: (i, 0)), pl.BlockSpec((tl, DG), lambda i: (nb - 1 - i, 0)),
                   pl.BlockSpec((nc, DG, DK), lambda i: (i, 0, 0)), pl.BlockSpec((nc, DG, DK), lambda i: (nb - 1 - i, 0, 0))],
        out_shape=[jax.ShapeDtypeStruct((L, DG), BF16), jax.ShapeDtypeStruct((L, DG), BF16),
                   jax.ShapeDtypeStruct((L // CH, DG, DK), BF16), jax.ShapeDtypeStruct((L // CH, DG, DK), BF16)],
        scratch_shapes=[pltpu.VMEM((DG, DK), F32), pltpu.VMEM((DG, DK), F32)]
        + [pltpu.VMEM((tl, DK), BF16)] * 3 + [pltpu.VMEM((nc, DK), F32)]
        + [pltpu.VMEM((tl, DK), BF16)] * 3 + [pltpu.VMEM((nc, DK), F32)],
        compiler_params=_cp("arbitrary"),
    )(P, P, P, P, P, P, P, P, gcat, gbias)


def _headnorm(o):
    oh, rs = [], []
    for h in range(NH):
        oo = o[:, h * HV:(h + 1) * HV]
        r = lax.rsqrt(jnp.mean(oo * oo, axis=-1, keepdims=True) + EPS)
        oh.append(oo * r)
        rs.append(r)
    return jnp.concatenate(oh, axis=1), rs


def mix_out(P, o_f, o_b, conv_a, ghn4, w_out, g2, x, tl):
    L = P.shape[0]
    nt = L // tl

    def body(gb, gc, gv, go, gcp, gvp, gcn, gvn, of, ob, ca, gh, wo, g2r, xr, ycat, yr, x1):
        i = pl.program_id(0)
        cp, cn = _halo_rows(gcp, gcn, i, nt - 1)
        vp, vn = _halo_rows(gvp, gvn, i, nt - 1)
        c = gc[...].astype(F32) * gv[...].astype(F32)
        cc, _, _ = _conv3(c, cp * vp, cn * vn, ca)
        ya = gb[...].astype(F32) * cc
        oh, _ = _headnorm(of[...].astype(F32) + ob[...].astype(F32))
        g = go[...].astype(F32)
        yb = g * _sigmoid(g) * (oh * gh[...])
        yc = jnp.concatenate([ya, yb], axis=1).astype(BF16)
        ycat[...] = yc
        y = _mm(yc, wo[...])
        yr[...] = y
        x1[...] = xr[...] + _rms(y, g2r[...])

    t = lambda cb: pl.BlockSpec((tl, DC), lambda i: (i, cb))
    hp = lambda cb: pl.BlockSpec((HALO16, DC), _prev_blk(tl, cb, HALO16))
    hn = lambda cb: pl.BlockSpec((HALO16, DC), _next_blk(tl, L, cb, HALO16))
    row = lambda n: pl.BlockSpec((tl, n), lambda i: (i, 0))
    full = lambda a: pl.BlockSpec(a.shape, lambda i: (0, 0))
    return pl.pallas_call(
        body, name="mix_out", grid=(nt,),
        in_specs=[t(CB_GB), t(CB_GC), t(CB_GV), t(CB_GO), hp(CB_GC), hp(CB_GV), hn(CB_GC), hn(CB_GV),
                  row(DG), row(DG), full(conv_a), full(ghn4), full(w_out), full(g2), row(D)],
        out_specs=[row(D), row(D), row(D)],
        out_shape=[jax.ShapeDtypeStruct((L, D), BF16), jax.ShapeDtypeStruct((L, D), F32),
                   jax.ShapeDtypeStruct((L, D), F32)],
        compiler_params=_cp("parallel"),
    )(P, P, P, P, P, P, P, P, o_f, o_b, conv_a, ghn4, w_out, g2, x)


NFF = 2
WFF = DFF // NFF
FFN_COL_CHUNKS = ((0, 512), (512, 1024), (1024, WFF))


def ffn_fwd(x1, g3, w_up, conv_ffn, w_down, g4, tl):
    L = x1.shape[0]
    nt = L // tl
    hh = HALO32

    def body(xr, xp, xn, g3r, wu, cf, wd, g4r, U, h2, ug, uv, zr, y2, x2):
        i = pl.program_id(0)
        he32 = _rms(jnp.concatenate([xp[...], xr[...], xn[...]], axis=0), g3r[...])
        he = he32.astype(BF16)
        h2[...] = he32[hh:hh + tl].astype(BF16)
        acc = jnp.zeros((tl, D), F32)
        for j in range(NFF):
            conv = []
            for blk, off in ((j, j * WFF), (NFF + j, DFF + j * WFF)):
                ue = _mm(he, wu[blk])
                p8 = jnp.where(i == 0, 0.0, ue[hh - 8:hh])
                n8 = jnp.where(i == nt - 1, 0.0, ue[hh + tl:hh + tl + 8])
                mid = ue[hh:hh + tl]
                U[:, off:off + WFF] = mid.astype(BF16)
                conv.append(_conv3(mid, p8, n8, cf.at[:, off:off + WFF])[0])
            gs = slice(j * WFF, (j + 1) * WFF)
            zz = (conv[0] * _sigmoid(conv[0]) * conv[1]).astype(BF16)
            ug[:, gs] = conv[0].astype(BF16)
            uv[:, gs] = conv[1].astype(BF16)
            zr[:, gs] = zz
            acc = acc + _mm(zz, wd[gs, :])
        y2[...] = acc
        x2[...] = xr[...] + _rms(acc, g4r[...])

    row = lambda n: pl.BlockSpec((tl, n), lambda i: (i, 0))
    full = lambda a: pl.BlockSpec(a.shape, lambda i: (0,) * a.ndim)
    once = lambda a: pl.BlockSpec(a.shape, lambda i: (0,) * a.ndim, pipeline_mode=pl.Buffered(1))
    half = jax.ShapeDtypeStruct((L, DFF), BF16)
    return pl.pallas_call(
        body, name="ffn_fwd", grid=(nt,),
        in_specs=[row(D), pl.BlockSpec((hh, D), lambda i: (_prev_row_blk(i, tl, hh), 0)),
                  pl.BlockSpec((hh, D), lambda i: (_next_row_blk(i, tl, L, hh), 0)),
                  full(g3), once(w_up), full(conv_ffn), once(w_down), full(g4)],
        out_specs=[row(2 * DFF), row(D), row(DFF), row(DFF), row(DFF), row(D), row(D)],
        out_shape=[jax.ShapeDtypeStruct((L, 2 * DFF), BF16), jax.ShapeDtypeStruct((L, D), BF16), half, half, half,
                   jax.ShapeDtypeStruct((L, D), F32), jax.ShapeDtypeStruct((L, D), F32)],
        compiler_params=_cp("parallel"),
    )(x1, x1, x1, g3, w_up, conv_ffn, w_down, g4)


def loss_head(y, target, tl):
    L = y.shape[0]

    def body(yr, tr, dy, ls):
        @pl.when(pl.program_id(0) == 0)
        def _():
            ls[...] = jnp.zeros_like(ls)

        err = yr[...] - tr[...]
        dy[...] = err * (1.0 / D)
        ls[...] += (0.5 / D) * jnp.sum(err * err)

    row = pl.BlockSpec((tl, D), lambda i: (i, 0))
    return pl.pallas_call(
        body, name="loss_head", grid=(L // tl,), in_specs=[row, row],
        out_specs=[row, pl.BlockSpec((8, 128), lambda i: (0, 0))],
        out_shape=[jax.ShapeDtypeStruct((L, D), F32), jax.ShapeDtypeStruct((8, 128), F32)],
        compiler_params=_cp("arbitrary"),
    )(y, target)


def rms_bwd_pre(dout, y, g, tl, after=()):
    L = y.shape[0]

    def body(dr, yr, gr, *rest):
        dy, dg = rest[-2:]

        @pl.when(pl.program_id(0) == 0)
        def _():
            dg[...] = jnp.zeros_like(dg)

        a, b = _rms_bwd(dr[...], yr[...], gr[...])
        dy[...] = a.astype(BF16)
        dg[...] += b

    row = pl.BlockSpec((tl, D), lambda i: (i, 0))
    vec = pl.BlockSpec((1, D), lambda i: (0, 0))
    return pl.pallas_call(
        body, name="rms_bwd_pre", grid=(L // tl,), in_specs=[row, row, vec] + [_ANY] * len(after), out_specs=[row, vec],
        out_shape=[jax.ShapeDtypeStruct((L, D), BF16), jax.ShapeDtypeStruct((1, D), F32)],
        compiler_params=_cp("arbitrary"),
    )(dout, y, g, *after)


def ffn_bwd(dy2, ug, uv, U, conv_ffn, w_down, w_up, x1, dres, g3, tl):
    L = x1.shape[0]
    nt = L // tl
    hh = HALO16

    def body(dyr, dyp, dyn, ugr, ugp, ugn, uvr, uvp, uvn, Ur, cf, wd, wu, x1r, drr, g3r, dUg, dUv, dx1, dg3, dcg, dcv):
        i = pl.program_id(0)

        @pl.when(i == 0)
        def _():
            dg3[...] = jnp.zeros_like(dg3)
            dcg[...] = jnp.zeros_like(dcg)
            dcv[...] = jnp.zeros_like(dcv)

        ext = lambda p, t, n, cs: jnp.concatenate([p[:, cs], t[:, cs], n[:, cs]], axis=0).astype(F32)
        dye = jnp.concatenate([dyp[...], dyr[...], dyn[...]], axis=0)
        acc = jnp.zeros((tl, D), F32)
        for j in range(NFF):
            dze = _mm_nt(dye, wd[j * WFF:(j + 1) * WFF, :])
            for c0, c1 in FFN_COL_CHUNKS:
                cs = slice(j * WFF + c0, j * WFF + c1)
                a = ext(ugp, ugr, ugn, cs)
                b = ext(uvp, uvr, uvn, cs)
                sg = _sigmoid(a)
                silu = a * sg
                dz = dze[:, c0:c1]
                for de, off, blk, dc, dU in ((dz * b * (sg + silu * (1.0 - sg)), 0, j, dcg, dUg),
                                            (dz * silu, DFF, NFF + j, dcv, dUv)):
                    d = de[hh:hh + tl]
                    p8 = jnp.where(i == 0, 0.0, de[hh - 8:hh])
                    n8 = jnp.where(i == nt - 1, 0.0, de[hh + tl:hh + tl + 8])
                    dm1, dp1 = _shifts(d, p8, n8)
                    wc = slice(off + j * WFF + c0, off + j * WFF + c1)
                    du = (cf[0:1, wc] * dp1 + cf[1:2, wc] * d + cf[2:3, wc] * dm1).astype(BF16)
                    dU[:, cs] = du
                    u = Ur[:, wc].astype(F32)
                    for k, t in enumerate((dp1, d, dm1)):
                        dc[k:k + 1, cs] += jnp.sum(t * u, axis=0, keepdims=True)
                    acc = acc + _mm_nt(du, wu[blk, :, c0:c1])
        dx, dg = _rms_bwd(acc, x1r[...], g3r[...])
        dx1[...] = drr[...] + dx
        dg3[...] += dg

    row = lambda n: pl.BlockSpec((tl, n), lambda i: (i, 0))
    prev = lambda n: pl.BlockSpec((hh, n), lambda i: (_prev_row_blk(i, tl, hh), 0))
    nxt = lambda n: pl.BlockSpec((hh, n), lambda i: (_next_row_blk(i, tl, L, hh), 0))
    full = lambda a: pl.BlockSpec(a.shape, lambda i: (0,) * a.ndim)
    once = lambda a: pl.BlockSpec(a.shape, lambda i: (0,) * a.ndim, pipeline_mode=pl.Buffered(1))
    half = jax.ShapeDtypeStruct((L, DFF), BF16)
    dcs = pl.BlockSpec((8, DFF), lambda i: (0, 0))
    return pl.pallas_call(
        body, name="ffn_bwd", grid=(nt,),
        in_specs=[row(D), prev(D), nxt(D), row(DFF), prev(DFF), nxt(DFF), row(DFF), prev(DFF), nxt(DFF), row(2 * DFF),
                  full(conv_ffn), once(w_down), once(w_up), row(D), row(D), full(g3)],
        out_specs=[row(DFF), row(DFF), row(D), pl.BlockSpec((1, D), lambda i: (0, 0)), dcs, dcs],
        out_shape=[half, half, jax.ShapeDtypeStruct((L, D), F32), jax.ShapeDtypeStruct((1, D), F32),
                   jax.ShapeDtypeStruct((8, DFF), F32), jax.ShapeDtypeStruct((8, DFF), F32)],
        compiler_params=_cp("arbitrary"),
    )(dy2, dy2, dy2, ug, ug, ug, uv, uv, uv, U, conv_ffn, w_down, w_up, x1, dres, g3)


def matmul_tn(a, b, ta, tn, tl, name, into=None, after=()):
    L, Ka = a.shape
    N = b.shape[1]

    def body(ar, br, *rest):
        o = rest[-1]

        @pl.when(pl.program_id(2) == 0)
        def _():
            o[...] = jnp.zeros_like(o)

        o[...] += _mm_tn(ar[...], br[...]).reshape(o.shape)

    in_specs = [pl.BlockSpec((tl, ta), lambda p, q, l: (l, p)), pl.BlockSpec((tl, tn), lambda p, q, l: (l, q))]
    if into is None:
        return pl.pallas_call(
            body, name=name, grid=(Ka // ta, N // tn, L // tl), in_specs=in_specs + [_ANY] * len(after),
            out_specs=pl.BlockSpec((ta, tn), lambda p, q, l: (p, q)),
            out_shape=jax.ShapeDtypeStruct((Ka, N), F32),
            compiler_params=_cp("parallel", "parallel", "arbitrary"),
        )(a, b, *after)
    buf, blk, idx = into
    return pl.pallas_call(
        body, name=name, grid=(Ka // ta, N // tn, L // tl), in_specs=in_specs + [_ANY],
        out_specs=pl.BlockSpec(blk, lambda p, q, l: idx(p, q)),
        out_shape=jax.ShapeDtypeStruct(buf.shape, F32), input_output_aliases={2: 0},
        compiler_params=_cp("parallel", "parallel", "arbitrary"),
    )(a, b, buf)


def mix_bwd1(dx1, y, g2, ycat, w_out, P, o_f, o_b, conv_a, ghn4, tl, after=()):
    L = P.shape[0]
    nt = L // tl
    na = len(after)

    def body(dxr, yr, g2r, ycr, wo, gb, gc, gv, go, gcp, gvp, gcn, gvn, of, ob, ca, gh, *rest):
        dgb, dcc, dgo, do, dca, dgh, dg2, dwo = rest[na:]
        i = pl.program_id(0)

        @pl.when(i == 0)
        def _():
            dca[...] = jnp.zeros_like(dca)
            dgh[...] = jnp.zeros_like(dgh)
            dg2[...] = jnp.zeros_like(dg2)
            dwo[...] = jnp.zeros_like(dwo)

        dyv, dg = _rms_bwd(dxr[...], yr[...], g2r[...])
        dy16 = dyv.astype(BF16)
        dg2[...] += dg
        dwo[...] += _mm_tn(ycr[...], dy16)
        dycat = _mm_nt(dy16, wo[...])
        dya = dycat[:, 0:DC]
        dyb = dycat[:, DC:D]
        cp, cn = _halo_rows(gcp, gcn, i, nt - 1)
        vp, vn = _halo_rows(gvp, gvn, i, nt - 1)
        c = gc[...].astype(F32) * gv[...].astype(F32)
        cc, c_m1, c_p1 = _conv3(c, cp * vp, cn * vn, ca)
        dgb[...] = (dya * cc).astype(BF16)
        d = dya * gb[...].astype(F32)
        dcc[...] = d.astype(BF16)
        for k, s in enumerate((c_m1, c, c_p1)):
            dca[k:k + 1, :] += jnp.sum(d * s, axis=0, keepdims=True)
        oh, rs = _headnorm(of[...].astype(F32) + ob[...].astype(F32))
        g = go[...].astype(F32)
        sg = _sigmoid(g)
        silu = g * sg
        dgo[...] = (dyb * (oh * gh[...]) * (sg * (1.0 + g * (1.0 - sg)))).astype(BF16)
        don = dyb * silu
        t = jnp.sum(don * oh, axis=0, keepdims=True)
        dgh[0:1, :] += t[:, 0:HV] + t[:, HV:2 * HV] + t[:, 2 * HV:3 * HV] + t[:, 3 * HV:4 * HV]
        doh = don * gh[...]
        parts = []
        for h in range(NH):
            hs = slice(h * HV, (h + 1) * HV)
            parts.append(rs[h] * (doh[:, hs] - oh[:, hs] * jnp.mean(doh[:, hs] * oh[:, hs], axis=-1, keepdims=True)))
        do[...] = jnp.concatenate(parts, axis=1).astype(BF16)

    t = lambda cb: pl.BlockSpec((tl, DC), lambda i: (i, cb))
    hp = lambda cb: pl.BlockSpec((HALO16, DC), _prev_blk(tl, cb, HALO16))
    hn = lambda cb: pl.BlockSpec((HALO16, DC), _next_blk(tl, L, cb, HALO16))
    row = lambda n: pl.BlockSpec((tl, n), lambda i: (i, 0))
    full = lambda a: pl.BlockSpec(a.shape, lambda i: (0, 0))
    act16 = lambda n: jax.ShapeDtypeStruct((L, n), BF16)
    return pl.pallas_call(
        body, name="mix_bwd1", grid=(nt,),
        in_specs=[row(D), row(D), full(g2), row(D), full(w_out), t(CB_GB), t(CB_GC), t(CB_GV), t(CB_GO),
                  hp(CB_GC), hp(CB_GV), hn(CB_GC), hn(CB_GV), row(DG), row(DG), full(conv_a), full(ghn4)] + [_ANY] * na,
        out_specs=[row(DC), row(DC), row(DG), row(DG), pl.BlockSpec((8, DC), lambda i: (0, 0)),
                   pl.BlockSpec((8, HV), lambda i: (0, 0)), pl.BlockSpec((1, D), lambda i: (0, 0)),
                   pl.BlockSpec((D, D), lambda i: (0, 0))],
        out_shape=[act16(DC), act16(DC), act16(DG), act16(DG), jax.ShapeDtypeStruct((8, DC), F32),
                   jax.ShapeDtypeStruct((8, HV), F32), jax.ShapeDtypeStruct((1, D), F32),
                   jax.ShapeDtypeStruct((D, D), F32)],
        compiler_params=_cp("arbitrary"),
    )(dx1, y, g2, ycat, w_out, P, P, P, P, P, P, P, P, o_f, o_b, conv_a, ghn4, *after)


def _gla_chunk_bwd(qt, kt, kh, v, do, st16, dec, g_ref, m, rev):
    qt16, kt16, kh16, v16, do16 = (t.astype(BF16) for t in (qt, kt, kh, v, do))
    qs, sc = _gla_scores(qt16, kt16, m, rev)
    g = g_ref[...]
    g16 = g.astype(BF16)
    dob = _tile4(do16) * m["bdo"].astype(BF16)
    dv = _mm_tn(sc, dob) + _mm_nt(kh16, g16)
    dsc = jnp.where(m["triu4"] if rev else m["tril4"], _mm_nt(dob, v16), 0.0)
    r1 = _mm(dsc, kt16) * m["bdq"]
    dqt = r1[0:CH] + r1[CH:2 * CH] + r1[2 * CH:3 * CH] + r1[3 * CH:4 * CH] + _mm(do16, st16)
    dkt = _mm_tn(dsc, qs)
    dkh = _mm(v16, g16)
    dd = jnp.sum(g * st16.astype(F32), axis=0, keepdims=True)
    g_ref[...] = g * dec + _mm_tn(do16, qt16) * m["bds"]
    return dv, dqt, dkt, dkh, dd


def gla_bwd(P, do, sf, sb, gcat, gbias, tl):
    L = P.shape[0]
    nb = L // tl
    nc = tl // CH

    def body(qf, kf, vf, lf, dof, sfr, qb, kb, vb, lb, dob, sbr, gc_ref, bs_ref,
             dqf, dkf, dvf, daf, dqb, dkb, dvb, dab, gf, gbk, *scr):
        @pl.when(pl.program_id(0) == 0)
        def _():
            gf[...] = jnp.zeros_like(gf)
            gbk[...] = jnp.zeros_like(gbk)

        m = _gla_masks()
        keys = ("qt", "kt", "kh", "e", "einv", "eout", "dec")
        names = keys + ("dd", "dqt", "dkt", "dkh")
        pf = dict(zip(names, scr[0:11]))
        pb = dict(zip(names, scr[11:22]))
        for (q, k, lr, cols, rev, pr) in ((qf, kf, lf, slice(0, DK), False, pf), (qb, kb, lb, slice(DK, 2 * DK), True, pb)):
            p = _gla_tile_prep(q[...], k[...], _gates(lr, gc_ref, bs_ref, cols), m, rev, nc)
            for key in keys:
                pr[key][...] = p[key]

        def step(c, v, dor, st, g_ref, pr, dv, rev):
            rows = pl.ds(pl.multiple_of(c * CH, CH), CH)
            dvc, dqt, dkt, dkh, dd = _gla_chunk_bwd(pr["qt"][rows, :], pr["kt"][rows, :], pr["kh"][rows, :], v[rows, :],
                                                    dor[rows, :], st[c], pr["dec"][pl.ds(c, 1), :], g_ref, m, rev)
            dv[rows, :] = dvc.astype(BF16)
            pr["dqt"][rows, :] = dqt
            pr["dkt"][rows, :] = dkt
            pr["dkh"][rows, :] = dkh
            pr["dd"][pl.ds(c, 1), :] = dd

        def chunk(c, carry):
            step(nc - 1 - c, vf, dof, sfr, gf, pf, dvf, False)
            step(c, vb, dob, sbr, gbk, pb, dvb, True)
            return carry

        lax.fori_loop(0, nc, chunk, 0, unroll=2)

        def finish(pr, dq, dk, da, rev):
            dqt, dkt, dkh = pr["dqt"][...], pr["dkt"][...], pr["dkh"][...]
            kk = dkh * pr["kh"][...]
            dcum = dqt * pr["qt"][...] - dkt * pr["kt"][...] - kk
            dtot = pr["dd"][...] * pr["dec"][...]
            tri_t = m["tril"] if rev else m["triu"]
            parts = []
            for c in range(nc):
                rs = slice(c * CH, (c + 1) * CH)
                parts.append(_mm_tri(tri_t, dcum[rs]) + (jnp.sum(kk[rs], axis=0, keepdims=True) + dtot[c:c + 1]))
            da[...] = jnp.concatenate(parts, axis=0).astype(BF16)
            dq[...] = (dqt * pr["e"][...] * QSCALE).astype(BF16)
            dk[...] = (dkt * pr["einv"][...] + dkh * pr["eout"][...]).astype(BF16)

        finish(pf, dqf, dkf, daf, False)
        finish(pb, dqb, dkb, dab, True)

    fwd_dir = lambda cb: (lambda i: (nb - 1 - i, cb))
    bwd_dir = lambda cb: (lambda i: (i, cb))

    def side(ix):
        return [pl.BlockSpec((tl, DK), ix(CB_Q)), pl.BlockSpec((tl, DK), ix(CB_K)), pl.BlockSpec((tl, DG), ix(CB_V)),
                pl.BlockSpec((tl, LRW), ix(CB_LR)), pl.BlockSpec((tl, DG), ix(0)),
                pl.BlockSpec((nc, DG, DK), lambda i: (ix(0)(i)[0], 0, 0))]

    def outs(ix):
        return [pl.BlockSpec((tl, DK), ix(0)), pl.BlockSpec((tl, DK), ix(0)), pl.BlockSpec((tl, DG), ix(0)),
                pl.BlockSpec((tl, DK), ix(0))]

    o_shape = [jax.ShapeDtypeStruct((L, DK), BF16), jax.ShapeDtypeStruct((L, DK), BF16),
               jax.ShapeDtypeStruct((L, DG), BF16), jax.ShapeDtypeStruct((L, DK), BF16)]
    return pl.pallas_call(
        body, name="gla_bwd", grid=(nb,),
        in_specs=side(fwd_dir) + side(bwd_dir) + [pl.BlockSpec((LRW, 2 * DK), lambda i: (0, 0)),
                                                  pl.BlockSpec((1, 2 * DK), lambda i: (0, 0))],
        out_specs=outs(fwd_dir) + outs(bwd_dir),
        out_shape=o_shape + o_shape,
        scratch_shapes=[pltpu.VMEM((DG, DK), F32), pltpu.VMEM((DG, DK), F32)]
        + ([pltpu.VMEM((tl, DK), F32)] * 6 + [pltpu.VMEM((nc, DK), F32)] * 2 + [pltpu.VMEM((tl, DK), F32)] * 3) * 2,
        compiler_params=_cp("arbitrary"),
    )(P, P, P, P, do, sf, P, P, P, P, do, sb, gcat, gbias)


def mix_bwd2(dgb, dcc, dgo, gl, P, conv_a, gcat, gbias, w_in, x, dres, g1, tl):
    L = P.shape[0]
    nt = L // tl

    def body(dgbr, dccr, dccp, dccn, dgor, dqf, dkf, dvf, daf, dqb, dkb, dvb, dab, gc, gv, lr, ca, gcr, bsr, wi,
             xr, drr, g1r, dP, dx, dg1, dgcat, dbias):
        i = pl.program_id(0)

        @pl.when(i == 0)
        def _():
            dg1[...] = jnp.zeros_like(dg1)
            dgcat[...] = jnp.zeros_like(dgcat)
            dbias[...] = jnp.zeros_like(dbias)

        p, n = _halo_rows(dccp, dccn, i, nt - 1)
        dc = _conv3_t(dccr[...].astype(F32), p, n, ca)
        pre = _mm(lr[...], gcr[...]) + bsr[...]
        da = jnp.concatenate([daf[...], dab[...]], axis=1).astype(F32)
        add32 = lambda a, b: a[...].astype(F32) + b[...].astype(F32)
        dpre = da * GATE_NORM * (1.0 - _sigmoid(pre))
        dpre16 = dpre.astype(BF16)
        dP[:, 0:DC] = dgbr[...].astype(BF16)
        dP[:, DC:2 * DC] = (dc * gv[...].astype(F32)).astype(BF16)
        dP[:, 2 * DC:3 * DC] = (dc * gc[...].astype(F32)).astype(BF16)
        dP[:, 1536:1792] = add32(dqf, dqb).astype(BF16)
        dP[:, 1792:2048] = add32(dkf, dkb).astype(BF16)
        dP[:, 2048:2560] = add32(dvf, dvb).astype(BF16)
        dP[:, 2560:3072] = dgor[...].astype(BF16)
        dP[:, 3072:3200] = _mm_nt(dpre16, gcr[...]).astype(BF16)
        dgcat[...] += _mm_tn(lr[...], dpre16)
        dbias[0:1, :] += jnp.sum(dpre, axis=0, keepdims=True)
        dh, dg = _rms_bwd(_mm_nt(dP[...], wi[...]), xr[...], g1r[...])
        dx[...] = drr[...] + dh
        dg1[...] += dg

    row = lambda n: pl.BlockSpec((tl, n), lambda i: (i, 0))
    t = lambda w, cb: pl.BlockSpec((tl, w), lambda i: (i, cb))
    full = lambda a: pl.BlockSpec(a.shape, lambda i: (0, 0))
    return pl.pallas_call(
        body, name="mix_bwd2", grid=(nt,),
        in_specs=[row(DC), row(DC), pl.BlockSpec((HALO16, DC), _prev_blk(tl, 0, HALO16)),
                  pl.BlockSpec((HALO16, DC), _next_blk(tl, L, 0, HALO16)),
                  row(DG), row(DK), row(DK), row(DG), row(DK), row(DK), row(DK), row(DG), row(DK),
                  t(DC, CB_GC), t(DC, CB_GV), t(LRW, CB_LR), full(conv_a), full(gcat), full(gbias),
                  pl.BlockSpec(w_in.shape, lambda i: (0, 0), pipeline_mode=pl.Buffered(1)),
                  row(D), row(D), full(g1)],
        out_specs=[row(DINP), row(D), pl.BlockSpec((1, D), lambda i: (0, 0)), pl.BlockSpec((LRW, 2 * DK), lambda i: (0, 0)),
                   pl.BlockSpec((8, 2 * DK), lambda i: (0, 0))],
        out_shape=[jax.ShapeDtypeStruct((L, DINP), BF16), jax.ShapeDtypeStruct((L, D), F32),
                   jax.ShapeDtypeStruct((1, D), F32), jax.ShapeDtypeStruct((LRW, 2 * DK), F32),
                   jax.ShapeDtypeStruct((8, 2 * DK), F32)],
        compiler_params=_cp("arbitrary"),
    )(dgb, dcc, dcc, dcc, dgo, *gl, P, P, P, conv_a, gcat, gbias, w_in, x, dres, g1)


def _row_tile(rows, cols):
    if rows * cols * 4 <= 2 * 1024 * 1024:
        return rows
    best = 8
    for t in range(8, rows, 8):
        if rows % t == 0 and t * cols * 4 <= 2 * 1024 * 1024:
            best = t
    return best


def adamw(w, g, m, v, name):
    shape = w.shape
    cols = shape[-1]
    w2, g2, m2, v2 = (a.reshape(-1, cols) for a in (w, g, m, v))
    rows = w2.shape[0]
    tr = _row_tile(rows, cols)

    def body(wr, gr, mr, vr, dl, nm, nv):
        gg = gr[...]
        mm = B1 * mr[...] + (1.0 - B1) * gg
        vv = B2 * vr[...] + (1.0 - B2) * (gg * gg)
        m_hat = mm / (1.0 - B1 ** STEP)
        v_hat = vv / (1.0 - B2 ** STEP)
        dl[...] = -LR * (m_hat / (jnp.sqrt(v_hat) + AEPS) + WD * wr[...])
        nm[...] = mm
        nv[...] = vv

    blk = pl.BlockSpec((tr, cols), lambda i: (i, 0))
    o = jax.ShapeDtypeStruct((rows, cols), F32)
    d, nm, nv = pl.pallas_call(
        body, name=name, grid=(rows // tr,), in_specs=[blk] * 4, out_specs=[blk] * 3, out_shape=[o, o, o],
        compiler_params=_cp("parallel"),
    )(w2, g2, m2, v2)
    return d.reshape(shape), nm.reshape(shape), nv.reshape(shape)


def _place():
    return lax.axis_index("x"), lax.axis_index("y"), lax.axis_index("c")


def allgather8(v, name):
    mp, n = v.shape

    def body(x_ref, out_ref, send_sems, recv_sems, local_sem):
        x, y, c = _place()
        me, sibling = (x, y, c), (x, y, 1 - c)
        chips = [(1 - x, y), (x, 1 - y), (1 - x, 1 - y)]

        def rows(px, py, pc):
            return out_ref.at[pl.ds((4 * px + 2 * py + pc) * mp, mp), :]

        def copy(k, block, to, src=None):
            return pltpu.make_async_remote_copy(
                src_ref=rows(*block) if src is None else src, dst_ref=rows(*block),
                send_sem=send_sems.at[k], recv_sem=recv_sems.at[k], device_id=to, device_id_type=MESH)

        mine = pltpu.make_async_copy(x_ref, rows(*me), local_sem)
        mine.start()
        first = [copy(0, me, sibling, src=x_ref)]
        first += [copy(1 + j, me, (*chip, c), src=x_ref) for j, chip in enumerate(chips)]
        for cp in first:
            cp.start()
        passed = [copy(4 + j, (*chip, c), sibling) for j, chip in enumerate(chips)]
        for j, chip in enumerate(chips):
            copy(1 + j, (*chip, c), me).wait_recv()
            passed[j].start()
        copy(0, sibling, me).wait_recv()
        for j, chip in enumerate(chips):
            copy(4 + j, (*chip, 1 - c), me).wait_recv()
        for cp in first + passed:
            cp.wait_send()
        mine.wait()

    return pl.pallas_call(
        body, name=name, out_shape=jax.ShapeDtypeStruct((8 * mp, n), v.dtype),
        in_specs=[pl.BlockSpec(memory_space=pltpu.VMEM)], out_specs=pl.BlockSpec(memory_space=pltpu.VMEM),
        scratch_shapes=[pltpu.SemaphoreType.DMA((7,)), pltpu.SemaphoreType.DMA((7,)), pltpu.SemaphoreType.DMA],
        compiler_params=pltpu.CompilerParams(vmem_limit_bytes=VMEM_LIMIT),
    )(v)


def sum8(v, mp):
    def body(x_ref, o_ref):
        acc = x_ref[0:mp, :]
        for d in range(1, 8):
            acc = acc + x_ref[d * mp:(d + 1) * mp, :]
        o_ref[...] = acc

    return pl.pallas_call(body, name="sum8", out_shape=jax.ShapeDtypeStruct((mp, v.shape[1]), F32),
                          compiler_params=pltpu.CompilerParams(vmem_limit_bytes=VMEM_LIMIT))(v)


_ANY = pl.BlockSpec(memory_space=pl.ANY)


def _row_half(ref, lead, h):
    hr = ref.shape[-2] // 2
    return ref.at[(*lead, pl.ds(h * hr, hr), slice(None))]


def allgather_weights(slots):
    n = len(slots)

    def body(*refs):
        s_refs, o_refs, (send_sems, recv_sems) = refs[:n], refs[n:2 * n], refs[2 * n:]
        x, y, c = _place()
        me = 2 * x + y
        sibling = (x, y, 1 - c)
        chips = [(1 - x, y), (x, 1 - y), (1 - x, 1 - y)]

        def half(ref, slot, h):
            return _row_half(ref, (slot, slice(None)), h)

        def copy(k, src, dst, to):
            return pltpu.make_async_remote_copy(src_ref=src, dst_ref=dst, send_sem=send_sems.at[k],
                                                recv_sem=recv_sems.at[k], device_id=to, device_id_type=MESH)

        first = [copy(6 * a + k, half(s_refs[a], me, c), half(o_refs[a], me, c), (px, py, c))
                 for k, (px, py) in enumerate(chips) for a in range(n)]
        for cp in first:
            cp.start()
        passed = []
        for k, (px, py) in enumerate(chips):
            for a in range(n):
                got = half(o_refs[a], 2 * px + py, c)
                copy(6 * a + k, half(s_refs[a], me, c), got, (px, py, c)).wait_recv()
                cp = copy(6 * a + 3 + k, got, got, sibling)
                cp.start()
                passed.append(cp)
        for k, (px, py) in enumerate(chips):
            for a in range(n):
                got = half(o_refs[a], 2 * px + py, 1 - c)
                copy(6 * a + 3 + k, got, got, sibling).wait_recv()
        for cp in first + passed:
            cp.wait_send()

    return pl.pallas_call(
        body, name="allgather_weights", out_shape=[jax.ShapeDtypeStruct(s.shape, s.dtype) for s in slots],
        in_specs=[_ANY] * n, out_specs=[_ANY] * n, input_output_aliases={a: a for a in range(n)},
        scratch_shapes=[pltpu.SemaphoreType.DMA((6 * n,)), pltpu.SemaphoreType.DMA((6 * n,))],
    )(*slots)


_HBM = pl.BlockSpec(memory_space=pltpu.HBM)
_SEM = pl.BlockSpec(memory_space=pltpu.SEMAPHORE)
_EFFECT = pltpu.SideEffectType.DATAFLOW_SIDE_EFFECTING


def gather_start(slots, name, after=()):
    n = len(slots)
    na = len(after)

    def body(*refs):
        s_refs, send_sems, recv_sems, token = refs[:n], refs[n + na], refs[n + na + 1], refs[-1]
        x, y, c = _place()
        me = 2 * x + y
        for k, (px, py) in enumerate([(1 - x, y), (x, 1 - y), (1 - x, 1 - y)]):
            for a in range(n):
                pltpu.make_async_remote_copy(
                    src_ref=s_refs[a].at[me], dst_ref=s_refs[a].at[me], send_sem=send_sems.at[3 * a + k],
                    recv_sem=recv_sems.at[3 * a + k], device_id=(px, py, c), device_id_type=MESH).start()
        token[...] = jnp.zeros_like(token)

    out = pl.pallas_call(
        body, name=name,
        out_shape=(pltpu.SemaphoreType.DMA((3 * n,)), pltpu.SemaphoreType.DMA((3 * n,)),
                   *[pltpu.HBM(s.shape, s.dtype) for s in slots], jax.ShapeDtypeStruct((8, 128), F32)),
        in_specs=[_HBM] * n + [_ANY] * na, out_specs=(_SEM, _SEM, *[_HBM] * n, pl.BlockSpec(memory_space=pltpu.VMEM)),
        input_output_aliases={a: 2 + a for a in range(n)},
        compiler_params=pltpu.CompilerParams(has_side_effects=_EFFECT),
    )(*[pltpu.with_memory_space_constraint(s, pltpu.HBM) for s in slots], *after)
    return out[0], out[1], list(out[2:2 + n]), out[-1]


def gather_wait(send_sems, recv_sems, slots, after, name):
    n = len(slots)

    def body(*refs):
        s_refs, ssem, rsem = refs[:n], refs[n], refs[n + 1]
        x, y, c = _place()
        me = 2 * x + y
        for k, (px, py) in enumerate([(1 - x, y), (x, 1 - y), (1 - x, 1 - y)]):
            for a in range(n):
                cp = pltpu.make_async_remote_copy(
                    src_ref=s_refs[a].at[me], dst_ref=s_refs[a].at[2 * px + py], send_sem=ssem.at[3 * a + k],
                    recv_sem=rsem.at[3 * a + k], device_id=(px, py, c), device_id_type=MESH)
                cp.wait_send()
                cp.wait_recv()

    return pl.pallas_call(
        body, name=name, out_shape=[pltpu.HBM(s.shape, s.dtype) for s in slots],
        in_specs=[_HBM] * n + [_SEM, _SEM, _ANY], out_specs=[_HBM] * n,
        input_output_aliases={a: a for a in range(n)},
        compiler_params=pltpu.CompilerParams(has_side_effects=_EFFECT),
    )(*slots, send_sems, recv_sems, after)


def rs_chipsum16(g, recv1, cidx, name):
    nl, hr, cols = recv1.shape[1:]

    def body(c_ref, g_ref, r_ref, o_ref):
        o_ref[...] = (g_ref[...] + r_ref[...]).astype(BF16)

    blk = (1, 1, hr, cols)
    return pl.pallas_call(
        body, name=name, out_shape=jax.ShapeDtypeStruct(recv1.shape, BF16),
        grid_spec=pltpu.PrefetchScalarGridSpec(
            num_scalar_prefetch=1, grid=(4, nl),
            in_specs=[pl.BlockSpec(blk, lambda j, l, c: (j, l, c[0], 0)), pl.BlockSpec(blk, lambda j, l, c: (j, l, 0, 0))],
            out_specs=pl.BlockSpec(blk, lambda j, l, c: (j, l, 0, 0))),
        compiler_params=_cp("parallel", "parallel"),
    )(cidx, g, recv1)


def sibling_start(gs, name):
    n = len(gs)
    lands = [lax.empty((*g.shape[:2], g.shape[2] // 2, g.shape[3]), F32) for g in gs]

    def body(*refs):
        g_refs, l_refs, send_sems, recv_sems, token = refs[:n], refs[n:2 * n], refs[2 * n], refs[2 * n + 1], refs[-1]
        x, y, c = _place()
        for a in range(n):
            pltpu.make_async_remote_copy(
                src_ref=_row_half(g_refs[a], (slice(None), slice(None)), 1 - c), dst_ref=l_refs[a],
                send_sem=send_sems.at[a], recv_sem=recv_sems.at[a], device_id=(x, y, 1 - c), device_id_type=MESH).start()
        token[...] = jnp.zeros_like(token)

    bufs = list(gs) + lands
    out = pl.pallas_call(
        body, name=name,
        out_shape=(pltpu.SemaphoreType.DMA((n,)), pltpu.SemaphoreType.DMA((n,)),
                   *[pltpu.HBM(b.shape, b.dtype) for b in bufs], jax.ShapeDtypeStruct((8, 128), F32)),
        in_specs=[_HBM] * (2 * n), out_specs=(_SEM, _SEM, *[_HBM] * (2 * n), pl.BlockSpec(memory_space=pltpu.VMEM)),
        input_output_aliases={i: 2 + i for i in range(2 * n)},
        compiler_params=pltpu.CompilerParams(has_side_effects=_EFFECT),
    )(*[pltpu.with_memory_space_constraint(b, pltpu.HBM) for b in bufs])
    return out[0], out[1], list(out[2:2 + n]), list(out[2 + n:2 + 2 * n]), out[-1]


def sibling_wait(send_sems, recv_sems, gs, lands, after, name):
    n = len(gs)

    def body(*refs):
        g_refs, l_refs, ssem, rsem = refs[:n], refs[n:2 * n], refs[2 * n], refs[2 * n + 1]
        x, y, c = _place()
        for a in range(n):
            cp = pltpu.make_async_remote_copy(
                src_ref=_row_half(g_refs[a], (slice(None), slice(None)), 1 - c), dst_ref=l_refs[a],
                send_sem=ssem.at[a], recv_sem=rsem.at[a], device_id=(x, y, 1 - c), device_id_type=MESH)
            cp.wait_send()
            cp.wait_recv()

    bufs = list(gs) + list(lands)
    out = pl.pallas_call(
        body, name=name, out_shape=[pltpu.HBM(b.shape, b.dtype) for b in bufs],
        in_specs=[_HBM] * (2 * n) + [_SEM, _SEM, _ANY], out_specs=[_HBM] * (2 * n),
        input_output_aliases={i: i for i in range(2 * n)},
        compiler_params=pltpu.CompilerParams(has_side_effects=_EFFECT),
    )(*bufs, send_sems, recv_sems, after)
    return list(out[:n]), list(out[n:])


def exchange_start(cs, name):
    n = len(cs)
    lands = [lax.empty((3, *c.shape[1:]), BF16) for c in cs]

    def body(*refs):
        s_refs, l_refs, send_sems, recv_sems, token = refs[:n], refs[n:2 * n], refs[2 * n], refs[2 * n + 1], refs[-1]
        x, y, c = _place()
        for k, (px, py) in enumerate([(1 - x, y), (x, 1 - y), (1 - x, 1 - y)]):
            for a in range(n):
                pltpu.make_async_remote_copy(
                    src_ref=s_refs[a].at[2 * px + py], dst_ref=l_refs[a].at[k], send_sem=send_sems.at[3 * a + k],
                    recv_sem=recv_sems.at[3 * a + k], device_id=(px, py, c), device_id_type=MESH).start()
        token[...] = jnp.zeros_like(token)

    bufs = list(cs) + lands
    out = pl.pallas_call(
        body, name=name,
        out_shape=(pltpu.SemaphoreType.DMA((3 * n,)), pltpu.SemaphoreType.DMA((3 * n,)),
                   *[pltpu.HBM(b.shape, b.dtype) for b in bufs], jax.ShapeDtypeStruct((8, 128), F32)),
        in_specs=[_HBM] * (2 * n), out_specs=(_SEM, _SEM, *[_HBM] * (2 * n), pl.BlockSpec(memory_space=pltpu.VMEM)),
        input_output_aliases={i: 2 + i for i in range(2 * n)},
        compiler_params=pltpu.CompilerParams(has_side_effects=_EFFECT),
    )(*[pltpu.with_memory_space_constraint(b, pltpu.HBM) for b in bufs])
    return out[0], out[1], list(out[2:2 + n]), list(out[2 + n:2 + 2 * n]), out[-1]


def exchange_wait(send_sems, recv_sems, cs, lands, after, name):
    n = len(cs)

    def body(*refs):
        s_refs, l_refs, ssem, rsem = refs[:n], refs[n:2 * n], refs[2 * n], refs[2 * n + 1]
        x, y, c = _place()
        for k, (px, py) in enumerate([(1 - x, y), (x, 1 - y), (1 - x, 1 - y)]):
            for a in range(n):
                cp = pltpu.make_async_remote_copy(
                    src_ref=s_refs[a].at[2 * px + py], dst_ref=l_refs[a].at[k], send_sem=ssem.at[3 * a + k],
                    recv_sem=rsem.at[3 * a + k], device_id=(px, py, c), device_id_type=MESH)
                cp.wait_send()
                cp.wait_recv()

    bufs = list(cs) + list(lands)
    out = pl.pallas_call(
        body, name=name, out_shape=[pltpu.HBM(b.shape, b.dtype) for b in bufs],
        in_specs=[_HBM] * (2 * n) + [_SEM, _SEM, _ANY], out_specs=[_HBM] * (2 * n),
        input_output_aliases={i: i for i in range(2 * n)},
        compiler_params=pltpu.CompilerParams(has_side_effects=_EFFECT),
    )(*bufs, send_sems, recv_sems, after)
    return list(out[n:])


def rs_final_sum(g, recv1, recv2, idx, name):
    nl, hr, cols = recv1.shape[1:]

    def body(i_ref, g_ref, r1_ref, r2_ref, o_ref):
        acc = g_ref[0, 0] + r1_ref[0, 0]
        for k in range(3):
            acc = acc + r2_ref[k, 0].astype(F32)
        o_ref[0] = acc

    blk = (1, 1, hr, cols)
    return pl.pallas_call(
        body, name=name, out_shape=jax.ShapeDtypeStruct((nl, 2 * hr, cols), F32),
        grid_spec=pltpu.PrefetchScalarGridSpec(
            num_scalar_prefetch=1, grid=(nl,),
            in_specs=[pl.BlockSpec(blk, lambda l, ix: (ix[0], l, ix[1], 0)), pl.BlockSpec(blk, lambda l, ix: (ix[0], l, 0, 0)),
                      pl.BlockSpec((3, 1, hr, cols), lambda l, ix: (0, l, 0, 0))],
            out_specs=pl.BlockSpec((1, hr, cols), lambda l, ix: (l, ix[1], 0))),
        compiler_params=_cp("parallel"),
    )(idx, g, recv1, recv2)


def rs_share_halves(fulls):
    n = len(fulls)

    def body(*refs):
        h_refs, o_refs, (send_sems, recv_sems) = refs[:n], refs[n:2 * n], refs[2 * n:]
        x, y, c = _place()
        sibling = (x, y, 1 - c)

        def copy(a, h):
            return pltpu.make_async_remote_copy(
                src_ref=_row_half(h_refs[a], (slice(None),), h), dst_ref=_row_half(o_refs[a], (slice(None),), h),
                send_sem=send_sems.at[a], recv_sem=recv_sems.at[a], device_id=sibling, device_id_type=MESH)

        for a in range(n):
            copy(a, c).start()
        for a in range(n):
            copy(a, c).wait_send()
            copy(a, 1 - c).wait_recv()

    return pl.pallas_call(
        body, name="rs_share_halves", out_shape=[jax.ShapeDtypeStruct(f.shape, F32) for f in fulls],
        in_specs=[_ANY] * n, out_specs=[_ANY] * n, input_output_aliases={a: a for a in range(n)},
        scratch_shapes=[pltpu.SemaphoreType.DMA((n,)), pltpu.SemaphoreType.DMA((n,))],
    )(*fulls)


def _own_slot(shard, chip, dtype):
    return lax.dynamic_update_slice(lax.empty((4, *shard.shape), dtype), shard.astype(dtype)[None],
                                    (chip,) + (0,) * shard.ndim)


def kernel(x, norm_mix_pre, norm_mix_post, norm_ffn_pre, norm_ffn_post, w_in, conv_a, gate_up_fwd, gate_bias_fwd, gate_up_bwd, gate_bias_bwd, gla_head_norm, w_out, w_up, conv_ffn, w_down, loss_target, m_norm_mix_pre, m_norm_mix_post, m_norm_ffn_pre, m_norm_ffn_post, m_w_in, m_conv_a, m_gate_up_fwd, m_gate_bias_fwd, m_gate_up_bwd, m_gate_bias_bwd, m_gla_head_norm, m_w_out, m_w_up, m_conv_ffn, m_w_down, v_norm_mix_pre, v_norm_mix_post, v_norm_ffn_pre, v_norm_ffn_post, v_w_in, v_conv_a, v_gate_up_fwd, v_gate_bias_fwd, v_gate_up_bwd, v_gate_bias_bwd, v_gla_head_norm, v_w_out, v_w_up, v_conv_ffn, v_w_down):
    L = x.shape[1]
    xi, yi, ci = _place()
    chip = 2 * xi + yi
    tl_gla, tl_mix, tl_ffn = min(L, TL_GLA), min(L, TL_MIX), min(L, TL_FFN)

    big_w = (w_in, w_out, w_up, w_down)
    a_in0 = allgather_weights([_own_slot(w_in[0:1], chip, BF16)])[0][:, 0]
    started = []
    prev = (a_in0,)
    for l in range(DEPTH):
        ws = big_w[1:] if l == 0 else big_w
        started.append(gather_start([_own_slot(w[l], chip, BF16) for w in ws], f"gather_start_{l}", after=prev))
        prev = (started[-1][3],)
    tokens = [s[3] for s in started]

    def full_w_in(a_in):
        return jnp.pad(jnp.concatenate([a_in[j] for j in range(4)], axis=1), ((0, 0), (0, DINP - DIN)))

    small = jnp.concatenate([conv_a.reshape(-1), gate_up_fwd.reshape(-1), gate_up_bwd.reshape(-1), conv_ffn.reshape(-1)])
    ms = small.shape[0] // 128
    sg = allgather8(small.reshape(ms, 128), "allgather_small_weights").reshape(4, 2, ms * 128)[:, 0]

    def small_full(off, shape):
        n = shape[0] * shape[1] * shape[2]
        return jnp.concatenate([sg[j, off:off + n].reshape(shape) for j in range(4)], axis=2)

    o1 = DEPTH * 3 * 128
    o2 = o1 + DEPTH * RK * 64
    o3 = o2 + DEPTH * RK * 64
    conv_a_f = small_full(0, (DEPTH, 3, 128))
    gup_f = small_full(o1, (DEPTH, RK, 64))
    gup_b = small_full(o2, (DEPTH, RK, 64))
    conv_ffn_f = small_full(o3, (DEPTH, 3, 1408))

    def gcat_of(l):
        g = jnp.zeros((LRW, 2 * DK), F32)
        g = g.at[0:RK, 0:DK].set(gup_f[l]).at[RK:2 * RK, DK:2 * DK].set(gup_b[l])
        return g.astype(BF16)

    gcats = [gcat_of(l) for l in range(DEPTH)]
    gbiases = [jnp.concatenate([gate_bias_fwd[l], gate_bias_bwd[l]])[None, :] for l in range(DEPTH)]
    ghn4s = [jnp.tile(gla_head_norm[l], NH)[None, :] for l in range(DEPTH)]

    xc = x.reshape(L, D)
    saved = []
    W_in, W_out, W_up, W_down = [], [], [], []
    tl_row = min(L, TL_ROW)
    for l in range(DEPTH):
        ssem, rsem, bufs, _ = started[l]
        if l > 0:
            a_in, a_out, a_up, a_down = gather_wait(ssem, rsem, bufs, xc, f"gather_wait_{l}")
        W_in.append(full_w_in(a_in0 if l == 0 else a_in))
        P, h1 = rms_matmul(xc, norm_mix_pre[l][None, :], W_in[l], DINP, "proj_in", out_dtype=BF16, tm=TM_PROJ,
                           after=tokens if l == 0 else ())
        o_f, o_b, sf, sb = gla_fwd(P, gcats[l], gbiases[l], tl_gla)
        if l == 0:
            a_out, a_up, a_down = gather_wait(ssem, rsem, bufs, o_f, "gather_wait_0")
        W_out.append(a_out.reshape(D, D))
        W_up.append(a_up)
        W_down.append(a_down.reshape(DFF, D))
        ycat, y, x1 = mix_out(P, o_f, o_b, conv_a_f[l], ghn4s[l], W_out[l], norm_mix_post[l][None, :], xc,
                              min(L, TL_MIX_OUT))
        U, h2, ug, uv, z, y2, x2 = ffn_fwd(x1, norm_ffn_pre[l][None, :], W_up[l], conv_ffn_f[l], W_down[l],
                                           norm_ffn_post[l][None, :], tl_ffn)
        saved.append(dict(x=xc, h1=h1, P=P, o_f=o_f, o_b=o_b, sf=sf, sb=sb, ycat=ycat, y=y, x1=x1, h2=h2, U=U, y2=y2,
                          ug=ug, uv=uv, z=z))
        xc = x2

    dx, loss_blk = loss_head(xc, loss_target.reshape(L, D), tl_row)

    big = ("w_in", "w_out", "w_up", "w_down")
    cidx = jnp.reshape(ci, (1,)).astype(jnp.int32)
    idx = jnp.stack([chip, ci]).astype(jnp.int32)
    grads = [None] * DEPTH
    reduced = [dict() for _ in range(DEPTH)]
    tl_dw = min(L, 2048)
    groups = dict(ffn=("w_up", "w_down"), mix=("w_in", "w_out"))
    state = {grp: dict(flight=None, sibling=None) for grp in groups}
    token = ()

    def finish(grp, after):
        lp, gs_p, recv1_p, (ssem, rsem, cs_thru, lands, _) = state[grp]["flight"]
        recv2 = exchange_wait(ssem, rsem, cs_thru, lands, after, f"exchange_wait_{grp}_{lp}")
        halves = [rs_final_sum(g, r1, r2, idx, "rs_final_sum_" + k)
                  for g, r1, r2, k in zip(gs_p, recv1_p, recv2, groups[grp])]
        reduced[lp].update(zip(groups[grp], rs_share_halves(halves)))

    def advance(grp, after):
        st = state[grp]
        ls, (ssem, rsem, gs_thru, lands, _) = st["sibling"]
        gs_s, recv1 = sibling_wait(ssem, rsem, gs_thru, lands, after, f"sibling_wait_{grp}_{ls}")
        cs16 = [rs_chipsum16(g, r, cidx, "rs_chipsum16_" + k) for g, r, k in zip(gs_s, recv1, groups[grp])]
        flight = exchange_start(cs16, f"exchange_start_{grp}_{ls}")
        if st["flight"] is not None:
            finish(grp, flight[4])
        st["flight"] = (ls, gs_s, recv1, flight)
        st["sibling"] = None
        return flight[4]

    for l in reversed(range(DEPTH)):
        s = saved[l]
        dy2, dg4 = rms_bwd_pre(dx, s["y2"], norm_ffn_post[l][None, :], tl_row, after=token)
        g_down = matmul_tn(s["z"], dy2, DFF // 2, D, tl_dw, "dw_down").reshape(4, 1, DFF // 4, D)
        token2 = (advance("mix", g_down),) if state["mix"]["sibling"] is not None else ()
        dU_g, dU_v, dx1, dg3, dcf_g, dcf_v = ffn_bwd(dy2, s["ug"], s["uv"], s["U"], conv_ffn_f[l], W_down[l], W_up[l],
                                                     s["x1"], dx, norm_ffn_pre[l][None, :], tl_ffn)
        g_up = matmul_tn(s["h2"], dU_g, D, WFF, tl_dw, "dw_up_gate",
                         into=(lax.empty((4, 1, D, WFF), F32), (None, None, D, WFF), lambda p, q: (q, 0, 0, 0)))
        g_up = matmul_tn(s["h2"], dU_v, D, WFF, tl_dw, "dw_up_val",
                         into=(g_up, (None, None, D, WFF), lambda p, q: (NFF + q, 0, 0, 0)))
        sib = sibling_start([g_up, g_down], f"sibling_start_ffn_{l}")
        state["ffn"]["sibling"] = (l, sib)
        dgb, dcc, dgo, do, dca, dghn, dg2, dW_out = mix_bwd1(
            dx1, s["y"], norm_mix_post[l][None, :], s["ycat"], W_out[l], s["P"], s["o_f"], s["o_b"], conv_a_f[l],
            ghn4s[l], min(L, TL_MIX_OUT), after=token2 + (sib[4],))
        g_out = dW_out.reshape(4, 1, D // 4, D)
        token3 = advance("ffn", dgb)
        gl = gla_bwd(s["P"], do, s["sf"], s["sb"], gcats[l], gbiases[l], tl_gla)
        dP, dx, dg1, dgcat, dbias = mix_bwd2(dgb, dcc, dgo, gl, s["P"], conv_a_f[l], gcats[l], gbiases[l], W_in[l],
                                             s["x"], dx1, norm_mix_pre[l][None, :], tl_mix)
        dW_in = matmul_tn(s["h1"], dP, D // 2, DINP, tl_dw, "dw_in", after=(token3,))
        g_in = jnp.stack([dW_in[:, (DIN // 4) * j:(DIN // 4) * (j + 1)] for j in range(4)])[:, None]
        grads[l] = dict(
            norm_mix_pre=dg1[0], norm_mix_post=dg2[0], norm_ffn_pre=dg3[0], norm_ffn_post=dg4[0],
            conv_a=dca[0:3], gate_up_fwd=dgcat[0:RK, 0:DK], gate_bias_fwd=dbias[0, 0:DK],
            gate_up_bwd=dgcat[RK:2 * RK, DK:2 * DK], gate_bias_bwd=dbias[0, DK:2 * DK], gla_head_norm=dghn[0],
            conv_ffn=jnp.concatenate([dcf_g[0:3], dcf_v[0:3]], axis=1))
        sib = sibling_start([g_in, g_out], f"sibling_start_mix_{l}")
        state["mix"]["sibling"] = (l, sib)
        token = (sib[4],)
    last = advance("mix", token[0])
    finish("ffn", last)
    finish("mix", last)

    G = {k: jnp.stack([grads[l][k] for l in range(DEPTH)]) for k in grads[0]}

    small_names = ["norm_mix_pre", "norm_mix_post", "norm_ffn_pre", "norm_ffn_post", "conv_a", "gate_up_fwd",
                   "gate_bias_fwd", "gate_up_bwd", "gate_bias_bwd", "gla_head_norm", "conv_ffn"]
    flat = jnp.concatenate([G[k].reshape(-1) for k in small_names] + [loss_blk[0, 0:1]])
    n_small = flat.shape[0]
    mp = -(-n_small // 1024) * 8
    flat = jnp.pad(flat, (0, mp * 128 - n_small)).reshape(mp, 128)
    tot = sum8(allgather8(flat, "allgather_small_grads"), mp).reshape(-1)
    gsm = {}
    o = 0
    for k in small_names:
        n = G[k].size
        gsm[k] = tot[o:o + n].reshape(G[k].shape)
        o += n
    loss = tot[o]

    def my_cols(a, width):
        return lax.dynamic_slice_in_dim(a, chip * width, width, axis=2)

    gsm["conv_a"] = my_cols(gsm["conv_a"], 128)
    gsm["gate_up_fwd"] = my_cols(gsm["gate_up_fwd"], 64)
    gsm["gate_up_bwd"] = my_cols(gsm["gate_up_bwd"], 64)
    gsm["conv_ffn"] = my_cols(gsm["conv_ffn"], 1408)

    for k in big:
        gsm[k] = jnp.concatenate([reduced[l][k] for l in range(DEPTH)], axis=0)

    names = ["norm_mix_pre", "norm_mix_post", "norm_ffn_pre", "norm_ffn_post", "w_in", "conv_a", "gate_up_fwd",
             "gate_bias_fwd", "gate_up_bwd", "gate_bias_bwd", "gla_head_norm", "w_out", "w_up", "conv_ffn", "w_down"]
    w = dict(norm_mix_pre=norm_mix_pre, norm_mix_post=norm_mix_post, norm_ffn_pre=norm_ffn_pre, norm_ffn_post=norm_ffn_post,
             w_in=w_in, conv_a=conv_a, gate_up_fwd=gate_up_fwd, gate_bias_fwd=gate_bias_fwd, gate_up_bwd=gate_up_bwd,
             gate_bias_bwd=gate_bias_bwd, gla_head_norm=gla_head_norm, w_out=w_out, w_up=w_up, conv_ffn=conv_ffn, w_down=w_down)
    m = dict(norm_mix_pre=m_norm_mix_pre, norm_mix_post=m_norm_mix_post, norm_ffn_pre=m_norm_ffn_pre, norm_ffn_post=m_norm_ffn_post,
             w_in=m_w_in, conv_a=m_conv_a, gate_up_fwd=m_gate_up_fwd, gate_bias_fwd=m_gate_bias_fwd, gate_up_bwd=m_gate_up_bwd,
             gate_bias_bwd=m_gate_bias_bwd, gla_head_norm=m_gla_head_norm, w_out=m_w_out, w_up=m_w_up, conv_ffn=m_conv_ffn, w_down=m_w_down)
    v = dict(norm_mix_pre=v_norm_mix_pre, norm_mix_post=v_norm_mix_post, norm_ffn_pre=v_norm_ffn_pre, norm_ffn_post=v_norm_ffn_post,
             w_in=v_w_in, conv_a=v_conv_a, gate_up_fwd=v_gate_up_fwd, gate_bias_fwd=v_gate_bias_fwd, gate_up_bwd=v_gate_up_bwd,
             gate_bias_bwd=v_gate_bias_bwd, gla_head_norm=v_gla_head_norm, w_out=v_w_out, w_up=v_w_up, conv_ffn=v_conv_ffn, w_down=v_w_down)
    upd = {k: adamw(w[k], gsm[k], m[k], v[k], "adamw_" + k) for k in names}
    return (loss, dx.reshape(1, L, D), *[gsm[k] for k in names], *[upd[k][0] for k in names],
            *[upd[k][1] for k in names], *[upd[k][2] for k in names])
```
